```python
import math
import jax, jax.numpy as jnp
from jax import lax
import numpy as np

D_MODEL = 1024
BATCH = 16
SEQ = 4096
DEPTH = 2

N_META = 16
N_A_LAYERS = DEPTH // 2
N_B_LAYERS = DEPTH - N_A_LAYERS
D_FF = 2816
SSM_WIDTH = D_MODEL // 2
SSM_GROUP = 16
SSM_GROUPS = SSM_WIDTH // SSM_GROUP
SSM_STATE = 64
STEP_MIN = 1e-3
STEP_MAX = 1e-1
HEAD_DIM = 64
N_Q_HEADS = D_MODEL // HEAD_DIM
N_KV_HEADS = 4
Q_PER_KV = N_Q_HEADS // N_KV_HEADS
WINDOW = 128
BLOCK = 128
ROPE_THETA = 10000.0
EPS = 1e-6
NEG_INF = -1e30

kernel_name = "yoco_s5_swa_sink_macaron"


def rms_norm(x, g):
    xf = x.astype(jnp.float32)
    y = xf * lax.rsqrt(jnp.mean(xf * xf, axis=-1, keepdims=True) + EPS)
    return (y * g.astype(jnp.float32)).astype(x.dtype)


def rope(x, pos):
    half = HEAD_DIM // 2
    freqs = ROPE_THETA ** (-jnp.arange(0, half, dtype=jnp.float32) * 2.0 / HEAD_DIM)
    ang = pos.astype(jnp.float32)[:, None] * freqs[None, :]
    bshape = (pos.shape[0],) + (1,) * (x.ndim - 3) + (half,)
    cos = jnp.cos(ang).reshape(bshape)
    sin = jnp.sin(ang).reshape(bshape)
    xf = x.astype(jnp.float32)
    x1, x2 = xf[..., :half], xf[..., half:]
    return jnp.concatenate([x1 * cos - x2 * sin, x2 * cos + x1 * sin], axis=-1).astype(x.dtype)


def swiglu_ffn(h, g, w_gate_up, w_down):
    a, b = jnp.split(rms_norm(h, g) @ w_gate_up, 2, axis=-1)
    return (jax.nn.silu(a) * b) @ w_down


def _complex_scan_op(e1, e2):
    a1r, a1i, b1r, b1i = e1
    a2r, a2i, b2r, b2i = e2
    return (a2r * a1r - a2i * a1i,
            a2r * a1i + a2i * a1r,
            a2r * b1r - a2i * b1i + b2r,
            a2r * b1i + a2i * b1r + b2i)


def s5_mixer(hn, w_in, lam_re, lam_im, b_re, b_im, c_re, c_im, log_step, d_skip, w_out):
    bsz, L, _ = hn.shape
    f = lambda t: t.astype(jnp.float32)
    u = f(hn @ w_in)
    ug = u.reshape(bsz, L, SSM_GROUPS, SSM_GROUP)
    lr, li = f(lam_re), f(lam_im)
    step = jnp.exp(f(log_step))[:, None]
    mag = jnp.exp(lr * step)
    ar = mag * jnp.cos(li * step)
    ai = mag * jnp.sin(li * step)
    den = lr * lr + li * li
    nr, ni = ar - 1.0, ai
    cr = (nr * lr + ni * li) / den
    ci = (ni * lr - nr * li) / den
    br, bi = f(b_re), f(b_im)
    bbar_r = cr[..., None] * br - ci[..., None] * bi
    bbar_i = cr[..., None] * bi + ci[..., None] * br
    bu_r = jnp.einsum('blgc,gpc->blgp', ug, bbar_r)
    bu_i = jnp.einsum('blgc,gpc->blgp', ug, bbar_i)
    a_r = jnp.broadcast_to(ar, (1, L, SSM_GROUPS, SSM_STATE))
    a_i = jnp.broadcast_to(ai, (1, L, SSM_GROUPS, SSM_STATE))
    _, _, xr, xi = lax.associative_scan(_complex_scan_op, (a_r, a_i, bu_r, bu_i), axis=1)
    y = jnp.einsum('blgp,gcp->blgc', xr, f(c_re)) - jnp.einsum('blgp,gcp->blgc', xi, f(c_im))
    y = y.reshape(bsz, L, SSM_WIDTH) + f(d_skip) * u
    z = jax.nn.gelu(y).astype(hn.dtype) @ w_out
    a, g = jnp.split(z, 2, axis=-1)
    return a * jax.nn.sigmoid(g)


def shared_kv(h, g_kv, w_kv, k_gain):
    bsz, L, _ = h.shape
    k, v = jnp.split(rms_norm(h, g_kv) @ w_kv, 2, axis=-1)
    k = k.reshape(bsz, L, N_KV_HEADS, HEAD_DIM)
    v = v.reshape(bsz, L, N_KV_HEADS, HEAD_DIM)
    k = rope(rms_norm(k, k_gain), jnp.arange(L))
    return k, v


def swa_sink_attention(hn, k, v, w_q, q_gain, sinks, w_o):
    bsz, S, _ = hn.shape
    nb = S // BLOCK
    q = (hn @ w_q).reshape(bsz, S, N_KV_HEADS, Q_PER_KV, HEAD_DIM)
    q = rope(rms_norm(q, q_gain), N_META + jnp.arange(S))
    qb = q.reshape(bsz, nb, BLOCK, N_KV_HEADS, Q_PER_KV, HEAD_DIM)
    k_meta, v_meta = k[:, :N_META], v[:, :N_META]
    k_blk = k[:, N_META:].reshape(bsz, nb, BLOCK, N_KV_HEADS, HEAD_DIM)
    v_blk = v[:, N_META:].reshape(bsz, nb, BLOCK, N_KV_HEADS, HEAD_DIM)
    pad = ((0, 0), (1, 0), (0, 0), (0, 0), (0, 0))
    k_band = jnp.concatenate([jnp.pad(k_blk, pad)[:, :-1], k_blk], axis=2)
    v_band = jnp.concatenate([jnp.pad(v_blk, pad)[:, :-1], v_blk], axis=2)
    scale = HEAD_DIM ** -0.5
    s_band = jnp.einsum('bnqhgd,bnkhd->bnhgqk', qb, k_band,
                        preferred_element_type=jnp.float32) * scale
    qi = jnp.arange(BLOCK)[:, None]
    kj = jnp.arange(2 * BLOCK)[None, :]
    rel = qi + BLOCK - kj
    blk = jnp.arange(nb)[:, None, None]
    valid = (rel >= 0) & (rel < WINDOW) & ((blk > 0) | (kj >= BLOCK))
    s_band = jnp.where(valid[None, :, None, None], s_band, NEG_INF)
    s_meta = jnp.einsum('bnqhgd,bmhd->bnhgqm', qb, k_meta,
                        preferred_element_type=jnp.float32) * scale
    sink = sinks.astype(jnp.float32).reshape(N_KV_HEADS, Q_PER_KV)[None, None, :, :, None]
    m = jnp.maximum(jnp.maximum(s_band.max(-1), s_meta.max(-1)), sink)
    p_band = jnp.exp(s_band - m[..., None])
    p_meta = jnp.exp(s_meta - m[..., None])
    denom = p_band.sum(-1) + p_meta.sum(-1) + jnp.exp(sink - m)
    o = (jnp.einsum('bnhgqk,bnkhd->bnqhgd', p_band, v_band.astype(jnp.float32))
         + jnp.einsum('bnhgqm,bmhd->bnqhgd', p_meta, v_meta.astype(jnp.float32)))
    o = o / jnp.moveaxis(denom, -1, 2)[..., None]
    return o.reshape(bsz, S, N_Q_HEADS * HEAD_DIM).astype(hn.dtype) @ w_o


def _fwd_setup_inputs(seed: int = 0) -> dict:
    key = jax.random.key(seed)
    ks = jax.random.split(key, 32)
    f32 = jnp.float32

    def nrm(k, shape, scale):
        return jax.random.normal(k, shape, f32) * scale

    H, G, P, C = SSM_WIDTH, SSM_GROUPS, SSM_STATE, SSM_GROUP
    return {
        "x": nrm(ks[0], (BATCH, SEQ, D_MODEL), 1.0),
        "meta_tokens": nrm(ks[1], (N_META, D_MODEL), 1.0),
        "ffn1_norm": 1.0 + nrm(ks[2], (DEPTH, D_MODEL), 0.02),
        "ffn1_w_gate_up": nrm(ks[3], (DEPTH, D_MODEL, 2 * D_FF), D_MODEL ** -0.5),
        "ffn1_w_down": nrm(ks[4], (DEPTH, D_FF, D_MODEL), D_FF ** -0.5),
        "mix_norm": 1.0 + nrm(ks[5], (DEPTH, D_MODEL), 0.02),
        "ffn2_norm": 1.0 + nrm(ks[6], (DEPTH, D_MODEL), 0.02),
        "ffn2_w_gate_up": nrm(ks[7], (DEPTH, D_MODEL, 2 * D_FF), D_MODEL ** -0.5),
        "ffn2_w_down": nrm(ks[8], (DEPTH, D_FF, D_MODEL), D_FF ** -0.5),
        "ssm_w_in": nrm(ks[9], (N_A_LAYERS, D_MODEL, H), D_MODEL ** -0.5),
        "ssm_lambda_re": -0.5 + nrm(ks[10], (N_A_LAYERS, G, P), 0.01),
        "ssm_lambda_im": jnp.pi * jnp.arange(P, dtype=f32) + nrm(ks[11], (N_A_LAYERS, G, P), 0.01),
        "ssm_b_re": nrm(ks[12], (N_A_LAYERS, G, P, C), (2 * C) ** -0.5),
        "ssm_b_im": nrm(ks[13], (N_A_LAYERS, G, P, C), (2 * C) ** -0.5),
        "ssm_c_re": nrm(ks[14], (N_A_LAYERS, G, C, P), P ** -0.5),
        "ssm_c_im": nrm(ks[15], (N_A_LAYERS, G, C, P), P ** -0.5),
        "ssm_log_step": jax.random.uniform(ks[16], (N_A_LAYERS, G), f32,
                                           minval=math.log(STEP_MIN), maxval=math.log(STEP_MAX)),
        "ssm_d": nrm(ks[17], (N_A_LAYERS, H), 1.0),
        "ssm_w_out": nrm(ks[18], (N_A_LAYERS, H, 2 * D_MODEL), H ** -0.5),
        "kv_norm": 1.0 + nrm(ks[19], (D_MODEL,), 0.02),
        "w_kv": nrm(ks[20], (D_MODEL, 2 * N_KV_HEADS * HEAD_DIM), D_MODEL ** -0.5),
        "k_norm": 1.0 + nrm(ks[21], (HEAD_DIM,), 0.02),
        "attn_w_q": nrm(ks[22], (N_B_LAYERS, D_MODEL, N_Q_HEADS * HEAD_DIM), D_MODEL ** -0.5),
        "q_norm": 1.0 + nrm(ks[23], (N_B_LAYERS, HEAD_DIM), 0.02),
        "attn_sinks": nrm(ks[24], (N_B_LAYERS, N_Q_HEADS), 0.5),
        "attn_w_o": nrm(ks[25], (N_B_LAYERS, N_Q_HEADS * HEAD_DIM, D_MODEL), (N_Q_HEADS * HEAD_DIM) ** -0.5),
    }


def _fwd_reference(x, meta_tokens, ffn1_norm, ffn1_w_gate_up, ffn1_w_down, mix_norm, ffn2_norm,
              ffn2_w_gate_up, ffn2_w_down, ssm_w_in, ssm_lambda_re, ssm_lambda_im, ssm_b_re,
              ssm_b_im, ssm_c_re, ssm_c_im, ssm_log_step, ssm_d, ssm_w_out, kv_norm, w_kv,
              k_norm, attn_w_q, q_norm, attn_sinks, attn_w_o):
    bsz = x.shape[0]
    meta = jnp.broadcast_to(meta_tokens.astype(x.dtype)[None], (bsz, N_META, D_MODEL))
    h = jnp.concatenate([meta, x], axis=1)
    k = v = None
    for layer in range(DEPTH):
        if layer == N_A_LAYERS:
            k, v = shared_kv(h, kv_norm, w_kv, k_norm)
            h = h[:, N_META:]
        h = h + 0.5 * swiglu_ffn(h, ffn1_norm[layer], ffn1_w_gate_up[layer], ffn1_w_down[layer])
        hn = rms_norm(h, mix_norm[layer])
        if layer < N_A_LAYERS:
            h = h + s5_mixer(hn, ssm_w_in[layer], ssm_lambda_re[layer], ssm_lambda_im[layer],
                             ssm_b_re[layer], ssm_b_im[layer], ssm_c_re[layer], ssm_c_im[layer],
                             ssm_log_step[layer], ssm_d[layer], ssm_w_out[layer])
        else:
            j = layer - N_A_LAYERS
            h = h + swa_sink_attention(hn, k, v, attn_w_q[j], q_norm[j], attn_sinks[j], attn_w_o[j])
        h = h + 0.5 * swiglu_ffn(h, ffn2_norm[layer], ffn2_w_gate_up[layer], ffn2_w_down[layer])
    return h


import jax as _jax
import jax.numpy as _jnp

TWIN_FORMAT = 'train_step'
FWD_PARAMS = ['x', 'meta_tokens', 'ffn1_norm', 'ffn1_w_gate_up', 'ffn1_w_down', 'mix_norm', 'ffn2_norm', 'ffn2_w_gate_up', 'ffn2_w_down', 'ssm_w_in', 'ssm_lambda_re', 'ssm_lambda_im', 'ssm_b_re', 'ssm_b_im', 'ssm_c_re', 'ssm_c_im', 'ssm_log_step', 'ssm_d', 'ssm_w_out', 'kv_norm', 'w_kv', 'k_norm', 'attn_w_q', 'q_norm', 'attn_sinks', 'attn_w_o']
TWIN_WEIGHTS = ['meta_tokens', 'ffn1_norm', 'ffn1_w_gate_up', 'ffn1_w_down', 'mix_norm', 'ffn2_norm', 'ffn2_w_gate_up', 'ffn2_w_down', 'ssm_w_in', 'ssm_lambda_re', 'ssm_lambda_im', 'ssm_b_re', 'ssm_b_im', 'ssm_c_re', 'ssm_c_im', 'ssm_log_step', 'ssm_d', 'ssm_w_out', 'kv_norm', 'w_kv', 'k_norm', 'attn_w_q', 'q_norm', 'attn_sinks', 'attn_w_o']
TWIN_DIFF_INPUT = 'x'
TWIN_INPUTS = ['x', 'meta_tokens', 'ffn1_norm', 'ffn1_w_gate_up', 'ffn1_w_down', 'mix_norm', 'ffn2_norm', 'ffn2_w_gate_up', 'ffn2_w_down', 'ssm_w_in', 'ssm_lambda_re', 'ssm_lambda_im', 'ssm_b_re', 'ssm_b_im', 'ssm_c_re', 'ssm_c_im', 'ssm_log_step', 'ssm_d', 'ssm_w_out', 'kv_norm', 'w_kv', 'k_norm', 'attn_w_q', 'q_norm', 'attn_sinks', 'attn_w_o', 'loss_target', 'm_meta_tokens', 'm_ffn1_norm', 'm_ffn1_w_gate_up', 'm_ffn1_w_down', 'm_mix_norm', 'm_ffn2_norm', 'm_ffn2_w_gate_up', 'm_ffn2_w_down', 'm_ssm_w_in', 'm_ssm_lambda_re', 'm_ssm_lambda_im', 'm_ssm_b_re', 'm_ssm_b_im', 'm_ssm_c_re', 'm_ssm_c_im', 'm_ssm_log_step', 'm_ssm_d', 'm_ssm_w_out', 'm_kv_norm', 'm_w_kv', 'm_k_norm', 'm_attn_w_q', 'm_q_norm', 'm_attn_sinks', 'm_attn_w_o', 'v_meta_tokens', 'v_ffn1_norm', 'v_ffn1_w_gate_up', 'v_ffn1_w_down', 'v_mix_norm', 'v_ffn2_norm', 'v_ffn2_w_gate_up', 'v_ffn2_w_down', 'v_ssm_w_in', 'v_ssm_lambda_re', 'v_ssm_lambda_im', 'v_ssm_b_re', 'v_ssm_b_im', 'v_ssm_c_re', 'v_ssm_c_im', 'v_ssm_log_step', 'v_ssm_d', 'v_ssm_w_out', 'v_kv_norm', 'v_w_kv', 'v_k_norm', 'v_attn_w_q', 'v_q_norm', 'v_attn_sinks', 'v_attn_w_o']
TWIN_OUTPUTS = ['loss', 'grad_x', 'grad_meta_tokens', 'grad_ffn1_norm', 'grad_ffn1_w_gate_up', 'grad_ffn1_w_down', 'grad_mix_norm', 'grad_ffn2_norm', 'grad_ffn2_w_gate_up', 'grad_ffn2_w_down', 'grad_ssm_w_in', 'grad_ssm_lambda_re', 'grad_ssm_lambda_im', 'grad_ssm_b_re', 'grad_ssm_b_im', 'grad_ssm_c_re', 'grad_ssm_c_im', 'grad_ssm_log_step', 'grad_ssm_d', 'grad_ssm_w_out', 'grad_kv_norm', 'grad_w_kv', 'grad_k_norm', 'grad_attn_w_q', 'grad_q_norm', 'grad_attn_sinks', 'grad_attn_w_o', 'delta_meta_tokens', 'delta_ffn1_norm', 'delta_ffn1_w_gate_up', 'delta_ffn1_w_down', 'delta_mix_norm', 'delta_ffn2_norm', 'delta_ffn2_w_gate_up', 'delta_ffn2_w_down', 'delta_ssm_w_in', 'delta_ssm_lambda_re', 'delta_ssm_lambda_im', 'delta_ssm_b_re', 'delta_ssm_b_im', 'delta_ssm_c_re', 'delta_ssm_c_im', 'delta_ssm_log_step', 'delta_ssm_d', 'delta_ssm_w_out', 'delta_kv_norm', 'delta_w_kv', 'delta_k_norm', 'delta_attn_w_q', 'delta_q_norm', 'delta_attn_sinks', 'delta_attn_w_o', 'new_m_meta_tokens', 'new_m_ffn1_norm', 'new_m_ffn1_w_gate_up', 'new_m_ffn1_w_down', 'new_m_mix_norm', 'new_m_ffn2_norm', 'new_m_ffn2_w_gate_up', 'new_m_ffn2_w_down', 'new_m_ssm_w_in', 'new_m_ssm_lambda_re', 'new_m_ssm_lambda_im', 'new_m_ssm_b_re', 'new_m_ssm_b_im', 'new_m_ssm_c_re', 'new_m_ssm_c_im', 'new_m_ssm_log_step', 'new_m_ssm_d', 'new_m_ssm_w_out', 'new_m_kv_norm', 'new_m_w_kv', 'new_m_k_norm', 'new_m_attn_w_q', 'new_m_q_norm', 'new_m_attn_sinks', 'new_m_attn_w_o', 'new_v_meta_tokens', 'new_v_ffn1_norm', 'new_v_ffn1_w_gate_up', 'new_v_ffn1_w_down', 'new_v_mix_norm', 'new_v_ffn2_norm', 'new_v_ffn2_w_gate_up', 'new_v_ffn2_w_down', 'new_v_ssm_w_in', 'new_v_ssm_lambda_re', 'new_v_ssm_lambda_im', 'new_v_ssm_b_re', 'new_v_ssm_b_im', 'new_v_ssm_c_re', 'new_v_ssm_c_im', 'new_v_ssm_log_step', 'new_v_ssm_d', 'new_v_ssm_w_out', 'new_v_kv_norm', 'new_v_w_kv', 'new_v_k_norm', 'new_v_attn_w_q', 'new_v_q_norm', 'new_v_attn_sinks', 'new_v_attn_w_o']
TWIN_LEAF_KINDS = {'loss': 'loss', 'grad_x': 'grad_x', 'grad_meta_tokens': 'grad_w', 'grad_ffn1_norm': 'grad_w', 'grad_ffn1_w_gate_up': 'grad_w', 'grad_ffn1_w_down': 'grad_w', 'grad_mix_norm': 'grad_w', 'grad_ffn2_norm': 'grad_w', 'grad_ffn2_w_gate_up': 'grad_w', 'grad_ffn2_w_down': 'grad_w', 'grad_ssm_w_in': 'grad_w', 'grad_ssm_lambda_re': 'grad_w', 'grad_ssm_lambda_im': 'grad_w', 'grad_ssm_b_re': 'grad_w', 'grad_ssm_b_im': 'grad_w', 'grad_ssm_c_re': 'grad_w', 'grad_ssm_c_im': 'grad_w', 'grad_ssm_log_step': 'grad_w', 'grad_ssm_d': 'grad_w', 'grad_ssm_w_out': 'grad_w', 'grad_kv_norm': 'grad_w', 'grad_w_kv': 'grad_w', 'grad_k_norm': 'grad_w', 'grad_attn_w_q': 'grad_w', 'grad_q_norm': 'grad_w', 'grad_attn_sinks': 'grad_w', 'grad_attn_w_o': 'grad_w', 'delta_meta_tokens': 'delta_w', 'delta_ffn1_norm': 'delta_w', 'delta_ffn1_w_gate_up': 'delta_w', 'delta_ffn1_w_down': 'delta_w', 'delta_mix_norm': 'delta_w', 'delta_ffn2_norm': 'delta_w', 'delta_ffn2_w_gate_up': 'delta_w', 'delta_ffn2_w_down': 'delta_w', 'delta_ssm_w_in': 'delta_w', 'delta_ssm_lambda_re': 'delta_w', 'delta_ssm_lambda_im': 'delta_w', 'delta_ssm_b_re': 'delta_w', 'delta_ssm_b_im': 'delta_w', 'delta_ssm_c_re': 'delta_w', 'delta_ssm_c_im': 'delta_w', 'delta_ssm_log_step': 'delta_w', 'delta_ssm_d': 'delta_w', 'delta_ssm_w_out': 'delta_w', 'delta_kv_norm': 'delta_w', 'delta_w_kv': 'delta_w', 'delta_k_norm': 'delta_w', 'delta_attn_w_q': 'delta_w', 'delta_q_norm': 'delta_w', 'delta_attn_sinks': 'delta_w', 'delta_attn_w_o': 'delta_w', 'new_m_meta_tokens': 'new_m', 'new_m_ffn1_norm': 'new_m', 'new_m_ffn1_w_gate_up': 'new_m', 'new_m_ffn1_w_down': 'new_m', 'new_m_mix_norm': 'new_m', 'new_m_ffn2_norm': 'new_m', 'new_m_ffn2_w_gate_up': 'new_m', 'new_m_ffn2_w_down': 'new_m', 'new_m_ssm_w_in': 'new_m', 'new_m_ssm_lambda_re': 'new_m', 'new_m_ssm_lambda_im': 'new_m', 'new_m_ssm_b_re': 'new_m', 'new_m_ssm_b_im': 'new_m', 'new_m_ssm_c_re': 'new_m', 'new_m_ssm_c_im': 'new_m', 'new_m_ssm_log_step': 'new_m', 'new_m_ssm_d': 'new_m', 'new_m_ssm_w_out': 'new_m', 'new_m_kv_norm': 'new_m', 'new_m_w_kv': 'new_m', 'new_m_k_norm': 'new_m', 'new_m_attn_w_q': 'new_m', 'new_m_q_norm': 'new_m', 'new_m_attn_sinks': 'new_m', 'new_m_attn_w_o': 'new_m', 'new_v_meta_tokens': 'new_v', 'new_v_ffn1_norm': 'new_v', 'new_v_ffn1_w_gate_up': 'new_v', 'new_v_ffn1_w_down': 'new_v', 'new_v_mix_norm': 'new_v', 'new_v_ffn2_norm': 'new_v', 'new_v_ffn2_w_gate_up': 'new_v', 'new_v_ffn2_w_down': 'new_v', 'new_v_ssm_w_in': 'new_v', 'new_v_ssm_lambda_re': 'new_v', 'new_v_ssm_lambda_im': 'new_v', 'new_v_ssm_b_re': 'new_v', 'new_v_ssm_b_im': 'new_v', 'new_v_ssm_c_re': 'new_v', 'new_v_ssm_c_im': 'new_v', 'new_v_ssm_log_step': 'new_v', 'new_v_ssm_d': 'new_v', 'new_v_ssm_w_out': 'new_v', 'new_v_kv_norm': 'new_v', 'new_v_w_kv': 'new_v', 'new_v_k_norm': 'new_v', 'new_v_attn_w_q': 'new_v', 'new_v_q_norm': 'new_v', 'new_v_attn_sinks': 'new_v', 'new_v_attn_w_o': 'new_v'}


def _forward(args):
    return _fwd_reference(*[args[k] for k in FWD_PARAMS])


def _output_shape():
    out = _jax.eval_shape(lambda: _forward(_fwd_setup_inputs(0)))
    return out.shape, out.dtype

N_MICROBATCH = 1
ADAM_LR = 0.001
ADAM_B1 = 0.9
ADAM_B2 = 0.999
ADAM_EPS = 1e-08
ADAM_WD = 0.01
ADAM_STEP = 10
PER_EXAMPLE_BATCH_AXIS = {'x': 0, 'loss_target': 0}
SHARED_INPUTS = []
_WEIGHT_DTYPES = {'meta_tokens': _jnp.float32, 'ffn1_norm': _jnp.float32, 'ffn1_w_gate_up': _jnp.float32, 'ffn1_w_down': _jnp.float32, 'mix_norm': _jnp.float32, 'ffn2_norm': _jnp.float32, 'ffn2_w_gate_up': _jnp.float32, 'ffn2_w_down': _jnp.float32, 'ssm_w_in': _jnp.float32, 'ssm_lambda_re': _jnp.float32, 'ssm_lambda_im': _jnp.float32, 'ssm_b_re': _jnp.float32, 'ssm_b_im': _jnp.float32, 'ssm_c_re': _jnp.float32, 'ssm_c_im': _jnp.float32, 'ssm_log_step': _jnp.float32, 'ssm_d': _jnp.float32, 'ssm_w_out': _jnp.float32, 'kv_norm': _jnp.float32, 'w_kv': _jnp.float32, 'k_norm': _jnp.float32, 'attn_w_q': _jnp.float32, 'q_norm': _jnp.float32, 'attn_sinks': _jnp.float32, 'attn_w_o': _jnp.float32}
MOMENT_SCALE = {'meta_tokens': 8.461961e-02, 'ffn1_norm': 1.223320e+01, 'ffn1_w_gate_up': 1.382654e-01, 'ffn1_w_down': 2.259098e-01, 'mix_norm': 6.480005e+00, 'ffn2_norm': 1.234557e+01, 'ffn2_w_gate_up': 1.689571e-01, 'ffn2_w_down': 2.676393e-01, 'ssm_w_in': 7.445950e-01, 'ssm_lambda_re': 6.521934e-02, 'ssm_lambda_im': 2.504213e-02, 'ssm_b_re': 3.541267e-02, 'ssm_b_im': 3.441623e-02, 'ssm_c_re': 5.159530e-02, 'ssm_c_im': 5.694903e-02, 'ssm_log_step': 1.328708e+01, 'ssm_d': 2.035860e+01, 'ssm_w_out': 2.825841e+00, 'kv_norm': 3.143147e+00, 'w_kv': 3.301450e+00, 'k_norm': 1.010336e+01, 'attn_w_q': 7.045635e-02, 'q_norm': 1.002693e+01, 'attn_sinks': 3.297080e-01, 'attn_w_o': 1.419626e+00}


def _to_microbatches(a, axis):
    t = _jnp.moveaxis(a, axis, 0)
    t = t.reshape((N_MICROBATCH, t.shape[0] // N_MICROBATCH) + t.shape[1:])
    return _jnp.moveaxis(t, 1, axis + 1)


def setup_inputs(seed: int = 0) -> dict:
    inp = _fwd_setup_inputs(seed)
    key = _jax.random.fold_in(_jax.random.key(seed), 7919)
    shape, _ = _output_shape()
    out = dict(inp)
    out["loss_target"] = _jax.random.normal(_jax.random.fold_in(key, 0), shape, _jnp.float32)
    for i, name in enumerate(TWIN_WEIGHTS):
        w = inp[name].astype(_jnp.float32)
        if MOMENT_SCALE is None:
            s = _jnp.sqrt(_jnp.mean(_jnp.square(w)) + 1e-30)
        else:
            s = MOMENT_SCALE[name]
        km, kv = _jax.random.split(_jax.random.fold_in(key, i + 1))
        out[name] = w
        out["m_" + name] = s * _jax.random.normal(km, w.shape, _jnp.float32)
        out["v_" + name] = (s * s) * _jax.random.uniform(kv, w.shape, _jnp.float32, 0.5, 1.5)
    if N_MICROBATCH > 1:
        for name, axis in PER_EXAMPLE_BATCH_AXIS.items():
            out[name] = _to_microbatches(out[name], axis)
    return {'x': out['x'], 'meta_tokens': out['meta_tokens'], 'ffn1_norm': out['ffn1_norm'], 'ffn1_w_gate_up': out['ffn1_w_gate_up'], 'ffn1_w_down': out['ffn1_w_down'], 'mix_norm': out['mix_norm'], 'ffn2_norm': out['ffn2_norm'], 'ffn2_w_gate_up': out['ffn2_w_gate_up'], 'ffn2_w_down': out['ffn2_w_down'], 'ssm_w_in': out['ssm_w_in'], 'ssm_lambda_re': out['ssm_lambda_re'], 'ssm_lambda_im': out['ssm_lambda_im'], 'ssm_b_re': out['ssm_b_re'], 'ssm_b_im': out['ssm_b_im'], 'ssm_c_re': out['ssm_c_re'], 'ssm_c_im': out['ssm_c_im'], 'ssm_log_step': out['ssm_log_step'], 'ssm_d': out['ssm_d'], 'ssm_w_out': out['ssm_w_out'], 'kv_norm': out['kv_norm'], 'w_kv': out['w_kv'], 'k_norm': out['k_norm'], 'attn_w_q': out['attn_w_q'], 'q_norm': out['q_norm'], 'attn_sinks': out['attn_sinks'], 'attn_w_o': out['attn_w_o'], 'loss_target': out['loss_target'], 'm_meta_tokens': out['m_meta_tokens'], 'm_ffn1_norm': out['m_ffn1_norm'], 'm_ffn1_w_gate_up': out['m_ffn1_w_gate_up'], 'm_ffn1_w_down': out['m_ffn1_w_down'], 'm_mix_norm': out['m_mix_norm'], 'm_ffn2_norm': out['m_ffn2_norm'], 'm_ffn2_w_gate_up': out['m_ffn2_w_gate_up'], 'm_ffn2_w_down': out['m_ffn2_w_down'], 'm_ssm_w_in': out['m_ssm_w_in'], 'm_ssm_lambda_re': out['m_ssm_lambda_re'], 'm_ssm_lambda_im': out['m_ssm_lambda_im'], 'm_ssm_b_re': out['m_ssm_b_re'], 'm_ssm_b_im': out['m_ssm_b_im'], 'm_ssm_c_re': out['m_ssm_c_re'], 'm_ssm_c_im': out['m_ssm_c_im'], 'm_ssm_log_step': out['m_ssm_log_step'], 'm_ssm_d': out['m_ssm_d'], 'm_ssm_w_out': out['m_ssm_w_out'], 'm_kv_norm': out['m_kv_norm'], 'm_w_kv': out['m_w_kv'], 'm_k_norm': out['m_k_norm'], 'm_attn_w_q': out['m_attn_w_q'], 'm_q_norm': out['m_q_norm'], 'm_attn_sinks': out['m_attn_sinks'], 'm_attn_w_o': out['m_attn_w_o'], 'v_meta_tokens': out['v_meta_tokens'], 'v_ffn1_norm': out['v_ffn1_norm'], 'v_ffn1_w_gate_up': out['v_ffn1_w_gate_up'], 'v_ffn1_w_down': out['v_ffn1_w_down'], 'v_mix_norm': out['v_mix_norm'], 'v_ffn2_norm': out['v_ffn2_norm'], 'v_ffn2_w_gate_up': out['v_ffn2_w_gate_up'], 'v_ffn2_w_down': out['v_ffn2_w_down'], 'v_ssm_w_in': out['v_ssm_w_in'], 'v_ssm_lambda_re': out['v_ssm_lambda_re'], 'v_ssm_lambda_im': out['v_ssm_lambda_im'], 'v_ssm_b_re': out['v_ssm_b_re'], 'v_ssm_b_im': out['v_ssm_b_im'], 'v_ssm_c_re': out['v_ssm_c_re'], 'v_ssm_c_im': out['v_ssm_c_im'], 'v_ssm_log_step': out['v_ssm_log_step'], 'v_ssm_d': out['v_ssm_d'], 'v_ssm_w_out': out['v_ssm_w_out'], 'v_kv_norm': out['v_kv_norm'], 'v_w_kv': out['v_w_kv'], 'v_k_norm': out['v_k_norm'], 'v_attn_w_q': out['v_attn_w_q'], 'v_q_norm': out['v_q_norm'], 'v_attn_sinks': out['v_attn_sinks'], 'v_attn_w_o': out['v_attn_w_o']}


def _loss(weights, diff, rest, loss_target):
    with _jax.named_scope("forward"):
        args = {**rest, TWIN_DIFF_INPUT: diff, **{k: w.astype(_WEIGHT_DTYPES[k]) for k, w in weights.items()}}
        y = _forward(args)
    with _jax.named_scope("loss_head"):
        err = _jnp.square(y.astype(_jnp.float32) - loss_target)
        return 0.5 * _jnp.sum(_jnp.mean(err, axis=-1)) if err.ndim else 0.5 * err


def _adamw(w, g, m, v):
    m = ADAM_B1 * m + (1.0 - ADAM_B1) * g
    v = ADAM_B2 * v + (1.0 - ADAM_B2) * _jnp.square(g)
    m_hat = m / (1.0 - ADAM_B1 ** ADAM_STEP)
    v_hat = v / (1.0 - ADAM_B2 ** ADAM_STEP)
    delta = -ADAM_LR * (m_hat / (_jnp.sqrt(v_hat) + ADAM_EPS) + ADAM_WD * w)
    return delta, m, v


def reference(x, meta_tokens, ffn1_norm, ffn1_w_gate_up, ffn1_w_down, mix_norm, ffn2_norm, ffn2_w_gate_up, ffn2_w_down, ssm_w_in, ssm_lambda_re, ssm_lambda_im, ssm_b_re, ssm_b_im, ssm_c_re, ssm_c_im, ssm_log_step, ssm_d, ssm_w_out, kv_norm, w_kv, k_norm, attn_w_q, q_norm, attn_sinks, attn_w_o, loss_target, m_meta_tokens, m_ffn1_norm, m_ffn1_w_gate_up, m_ffn1_w_down, m_mix_norm, m_ffn2_norm, m_ffn2_w_gate_up, m_ffn2_w_down, m_ssm_w_in, m_ssm_lambda_re, m_ssm_lambda_im, m_ssm_b_re, m_ssm_b_im, m_ssm_c_re, m_ssm_c_im, m_ssm_log_step, m_ssm_d, m_ssm_w_out, m_kv_norm, m_w_kv, m_k_norm, m_attn_w_q, m_q_norm, m_attn_sinks, m_attn_w_o, v_meta_tokens, v_ffn1_norm, v_ffn1_w_gate_up, v_ffn1_w_down, v_mix_norm, v_ffn2_norm, v_ffn2_w_gate_up, v_ffn2_w_down, v_ssm_w_in, v_ssm_lambda_re, v_ssm_lambda_im, v_ssm_b_re, v_ssm_b_im, v_ssm_c_re, v_ssm_c_im, v_ssm_log_step, v_ssm_d, v_ssm_w_out, v_kv_norm, v_w_kv, v_k_norm, v_attn_w_q, v_q_norm, v_attn_sinks, v_attn_w_o):
    given = dict(x=x, meta_tokens=meta_tokens, ffn1_norm=ffn1_norm, ffn1_w_gate_up=ffn1_w_gate_up, ffn1_w_down=ffn1_w_down, mix_norm=mix_norm, ffn2_norm=ffn2_norm, ffn2_w_gate_up=ffn2_w_gate_up, ffn2_w_down=ffn2_w_down, ssm_w_in=ssm_w_in, ssm_lambda_re=ssm_lambda_re, ssm_lambda_im=ssm_lambda_im, ssm_b_re=ssm_b_re, ssm_b_im=ssm_b_im, ssm_c_re=ssm_c_re, ssm_c_im=ssm_c_im, ssm_log_step=ssm_log_step, ssm_d=ssm_d, ssm_w_out=ssm_w_out, kv_norm=kv_norm, w_kv=w_kv, k_norm=k_norm, attn_w_q=attn_w_q, q_norm=q_norm, attn_sinks=attn_sinks, attn_w_o=attn_w_o, loss_target=loss_target, m_meta_tokens=m_meta_tokens, m_ffn1_norm=m_ffn1_norm, m_ffn1_w_gate_up=m_ffn1_w_gate_up, m_ffn1_w_down=m_ffn1_w_down, m_mix_norm=m_mix_norm, m_ffn2_norm=m_ffn2_norm, m_ffn2_w_gate_up=m_ffn2_w_gate_up, m_ffn2_w_down=m_ffn2_w_down, m_ssm_w_in=m_ssm_w_in, m_ssm_lambda_re=m_ssm_lambda_re, m_ssm_lambda_im=m_ssm_lambda_im, m_ssm_b_re=m_ssm_b_re, m_ssm_b_im=m_ssm_b_im, m_ssm_c_re=m_ssm_c_re, m_ssm_c_im=m_ssm_c_im, m_ssm_log_step=m_ssm_log_step, m_ssm_d=m_ssm_d, m_ssm_w_out=m_ssm_w_out, m_kv_norm=m_kv_norm, m_w_kv=m_w_kv, m_k_norm=m_k_norm, m_attn_w_q=m_attn_w_q, m_q_norm=m_q_norm, m_attn_sinks=m_attn_sinks, m_attn_w_o=m_attn_w_o, v_meta_tokens=v_meta_tokens, v_ffn1_norm=v_ffn1_norm, v_ffn1_w_gate_up=v_ffn1_w_gate_up, v_ffn1_w_down=v_ffn1_w_down, v_mix_norm=v_mix_norm, v_ffn2_norm=v_ffn2_norm, v_ffn2_w_gate_up=v_ffn2_w_gate_up, v_ffn2_w_down=v_ffn2_w_down, v_ssm_w_in=v_ssm_w_in, v_ssm_lambda_re=v_ssm_lambda_re, v_ssm_lambda_im=v_ssm_lambda_im, v_ssm_b_re=v_ssm_b_re, v_ssm_b_im=v_ssm_b_im, v_ssm_c_re=v_ssm_c_re, v_ssm_c_im=v_ssm_c_im, v_ssm_log_step=v_ssm_log_step, v_ssm_d=v_ssm_d, v_ssm_w_out=v_ssm_w_out, v_kv_norm=v_kv_norm, v_w_kv=v_w_kv, v_k_norm=v_k_norm, v_attn_w_q=v_attn_w_q, v_q_norm=v_q_norm, v_attn_sinks=v_attn_sinks, v_attn_w_o=v_attn_w_o)
    weights = {n: given[n] for n in TWIN_WEIGHTS}
    shared = {n: given[n] for n in SHARED_INPUTS}
    per_example = {n: given[n] for n in ['x']}
    grad_fn = _jax.value_and_grad(_loss, argnums=(0, 1))

    def one_microbatch(ex, loss_target):
        ex = dict(ex)
        diff = ex.pop(TWIN_DIFF_INPUT)
        return grad_fn(weights, diff, {**shared, **ex}, loss_target)

    if N_MICROBATCH == 1:
        loss, (grad_w, grad_x) = one_microbatch(per_example, given["loss_target"])
    else:
        def body(carry, xs):
            loss_sum, grad_sum = carry
            l_k, (gw_k, gx_k) = one_microbatch(xs[0], xs[1])
            with _jax.named_scope("update"):
                return (loss_sum + l_k, _jax.tree.map(_jnp.add, grad_sum, gw_k)), gx_k

        init = (_jnp.zeros((), _jnp.float32), _jax.tree.map(_jnp.zeros_like, weights))
        (loss, grad_w), grad_x = _jax.lax.scan(body, init, (per_example, given["loss_target"]))
    with _jax.named_scope("update"):
        delta_w, new_m, new_v = {}, {}, {}
        for n in TWIN_WEIGHTS:
            delta_w[n], new_m[n], new_v[n] = _adamw(weights[n], grad_w[n], given["m_" + n], given["v_" + n])
    return (loss, grad_x, *[grad_w[n] for n in TWIN_WEIGHTS], *[delta_w[n] for n in TWIN_WEIGHTS],
            *[new_m[n] for n in TWIN_WEIGHTS], *[new_v[n] for n in TWIN_WEIGHTS])
```

```python
import functools
import math

import jax
import jax.numpy as jnp
from jax import lax
from jax.experimental import pallas as pl
from jax.experimental.pallas import tpu as pltpu

F32 = jnp.float32
BF16 = jnp.bfloat16

N_META = 16
PAD = 128
META0 = PAD - N_META
HEAD_DIM = 64
N_KV_HEADS = 4
Q_PER_KV = 4
SSM_GROUP = 16
SSM_STATE = 64
EPS = 1e-6
NEG_INF = -1e30
ROPE_THETA = 10000.0
ADAM_LR, ADAM_B1, ADAM_B2, ADAM_EPS, ADAM_WD, ADAM_STEP = 0.001, 0.9, 0.999, 1e-08, 0.01, 10
LANES = 128
PACK_W = 1024
VMEM_LIMIT = 56 * 1024 * 1024
MESH_AXES = ("x", "y", "c")
N_DEV = 8


def _cparams(sem=None):
    return pltpu.CompilerParams(dimension_semantics=sem, vmem_limit_bytes=VMEM_LIMIT)


def _row_tile(rows):
    for tm in (384, 256, 128, 64, 32, 16, 8):
        if rows % tm == 0:
            return tm
    raise ValueError(rows)


def _dot(a, b):
    return jnp.dot(a.astype(BF16), b.astype(BF16), preferred_element_type=F32)


def _dot_nt(a, b):
    return lax.dot_general(a.astype(BF16), b.astype(BF16), (((1,), (1,)), ((), ())), preferred_element_type=F32)


def _dot_tn(a, b):
    return lax.dot_general(a.astype(BF16), b.astype(BF16), (((0,), (0,)), ((), ())), preferred_element_type=F32)


def _rms(x, g):
    rstd = lax.rsqrt(jnp.mean(x * x, axis=-1, keepdims=True) + EPS)
    y = x * rstd
    return y * g, y, rstd


def _rms_bwd(dhn, y, rstd, g):
    dyn = dhn * g
    dx = rstd * (dyn - y * jnp.mean(dyn * y, axis=-1, keepdims=True))
    return dx, jnp.sum(dhn * y, axis=0, keepdims=True)


def _sigmoid(x):
    return 1.0 / (1.0 + jnp.exp(-x))


_GELU_C = math.sqrt(2.0 / math.pi)


def _gelu(y):
    t = jnp.tanh(_GELU_C * (y + 0.044715 * y * y * y))
    return 0.5 * y * (1.0 + t), t


def _gelu_grad(y, t):
    return 0.5 * (1.0 + t) + 0.5 * y * (1.0 - t * t) * _GELU_C * (1.0 + 3.0 * 0.044715 * y * y)


def _rowcall(name, body, rows, row_ins, const_ins, row_outs, acc_outs=(), tm=None, row_in_maps=None):
    tm = tm or _row_tile(rows)
    steps = rows // tm
    in_specs = []
    for k, a in enumerate(row_ins):
        if row_in_maps is not None and row_in_maps[k] is not None:
            in_specs.append(pl.BlockSpec(*row_in_maps[k]))
        else:
            in_specs.append(pl.BlockSpec((tm, a.shape[1]), lambda i: (i, 0)))
    for a in const_ins:
        in_specs.append(pl.BlockSpec(a.shape, lambda i, nd=a.ndim: (0,) * nd, pipeline_mode=pl.Buffered(1)))
    out_shape, out_specs = [], []
    for w, dt in row_outs:
        out_shape.append(jax.ShapeDtypeStruct((rows, w), dt))
        out_specs.append(pl.BlockSpec((tm, w), lambda i: (i, 0)))
    for shp, dt in acc_outs:
        out_shape.append(jax.ShapeDtypeStruct(shp, dt))
        out_specs.append(pl.BlockSpec(shp, lambda i, nd=len(shp): (0,) * nd))

    def kern(*refs):
        body(pl.program_id(0), *refs)

    return pl.pallas_call(
        kern, name=name, grid=(steps,), in_specs=in_specs, out_specs=out_specs, out_shape=out_shape,
        compiler_params=_cparams(("arbitrary",)),
    )(*row_ins, *const_ins)


def _acc(step, ref, val):
    @pl.when(step == 0)
    def _():
        ref[...] = val

    @pl.when(step != 0)
    def _():
        ref[...] += val


def _embed(x, meta):
    bsz, seq, d = x.shape
    nb = seq // PAD + 1

    def kern(x_ref, m_ref, o_ref):
        i = pl.program_id(1)

        @pl.when(i == 0)
        def _():
            o_ref[0, 0:META0, :] = jnp.zeros((META0, d), F32)
            o_ref[0, META0:PAD, :] = m_ref[...]

        @pl.when(i != 0)
        def _():
            o_ref[0] = x_ref[0]

    return pl.pallas_call(
        kern, name="embed", grid=(bsz, nb),
        in_specs=[pl.BlockSpec((1, PAD, d), lambda b, i: (b, jnp.maximum(i - 1, 0), 0)),
                  pl.BlockSpec((N_META, d), lambda b, i: (0, 0))],
        out_specs=pl.BlockSpec((1, PAD, d), lambda b, i: (b, i, 0)),
        out_shape=jax.ShapeDtypeStruct((bsz, seq + PAD, d), F32),
        compiler_params=_cparams(("arbitrary", "arbitrary")),
    )(x, meta)


def _loss(h6, target):
    bsz, lp, d = h6.shape
    nb = lp // PAD

    def kern(h_ref, t_ref, l_ref, d_ref):
        b, i = pl.program_id(0), pl.program_id(1)

        @pl.when((b == 0) & (i == 0))
        def _():
            l_ref[...] = jnp.zeros_like(l_ref)

        @pl.when(i == 0)
        def _():
            d_ref[0] = jnp.zeros((PAD, d), F32)

        @pl.when(i != 0)
        def _():
            e = h_ref[0] - t_ref[0]
            d_ref[0] = e * (1.0 / d)
            l_ref[...] += 0.5 * jnp.sum(jnp.mean(e * e, axis=-1, keepdims=True))

    return pl.pallas_call(
        kern, name="loss", grid=(bsz, nb),
        in_specs=[pl.BlockSpec((1, PAD, d), lambda b, i: (b, i, 0)),
                  pl.BlockSpec((1, PAD, d), lambda b, i: (b, jnp.maximum(i - 1, 0), 0))],
        out_specs=[pl.BlockSpec((1, LANES), lambda b, i: (0, 0)),
                   pl.BlockSpec((1, PAD, d), lambda b, i: (b, i, 0))],
        out_shape=[jax.ShapeDtypeStruct((1, LANES), F32), jax.ShapeDtypeStruct((bsz, lp, d), F32)],
        compiler_params=_cparams(("arbitrary", "arbitrary")),
    )(h6, target)


def _meta_sum(dh0):
    bsz, lp, d = dh0.shape

    def kern(d_ref, o_ref):
        _acc(pl.program_id(0), o_ref, d_ref[0, META0:PAD, :])

    return pl.pallas_call(
        kern, name="meta_sum", grid=(bsz,),
        in_specs=[pl.BlockSpec((1, PAD, d), lambda b: (b, 0, 0))],
        out_specs=pl.BlockSpec((N_META, d), lambda b: (0, 0)),
        out_shape=jax.ShapeDtypeStruct((N_META, d), F32),
        compiler_params=_cparams(("arbitrary",)),
    )(dh0)


def _ffn_chunks(f):
    for n in (2, 4, 1, 11, 22):
        if f % n == 0 and (f // n) % LANES == 0:
            return n
    raise ValueError(f)


def _ffn_fwd(name, h, g, wgu, wd):
    rows, d = h.shape
    f = wd.shape[0]
    nf = _ffn_chunks(f)
    tf = f // nf

    def body(step, h_ref, g_ref, wgu_ref, wd_ref, o_ref):
        hx = h_ref[...]
        hb = _rms(hx, g_ref[...])[0].astype(BF16)
        acc = jnp.zeros(hx.shape, F32)
        for j in range(nf):
            a = _dot(hb, wgu_ref[:, j * tf:(j + 1) * tf])
            b = _dot(hb, wgu_ref[:, f + j * tf:f + (j + 1) * tf])
            acc = acc + _dot(a * _sigmoid(a) * b, wd_ref[j * tf:(j + 1) * tf, :])
        o_ref[...] = hx + 0.5 * acc

    return _rowcall(name, body, rows, [h], [g, wgu, wd], [(d, F32)])[0]


def _ffn_bwd(name, h, dout, g, wgu, wd):
    rows, d = h.shape
    f = wd.shape[0]
    nf = _ffn_chunks(f)
    tf = f // nf

    def body(step, h_ref, do_ref, g_ref, wgu_ref, wd_ref, dh_ref, hn_ref, dab_ref, act_ref, dg_ref):
        hx, dout_x, gx = h_ref[...], do_ref[...], g_ref[...]
        hn, y, rstd = _rms(hx, gx)
        hb = hn.astype(BF16)
        hn_ref[...] = hb
        dhalf = (0.5 * dout_x).astype(BF16)
        dhn = jnp.zeros(hx.shape, F32)
        for j in range(nf):
            ga, ua = slice(j * tf, (j + 1) * tf), slice(f + j * tf, f + (j + 1) * tf)
            a = _dot(hb, wgu_ref[:, ga])
            b = _dot(hb, wgu_ref[:, ua])
            s = _sigmoid(a)
            silu = a * s
            act_ref[:, ga] = (silu * b).astype(BF16)
            dact = _dot_nt(dhalf, wd_ref[ga, :])
            da = (dact * b * (s + silu * (1.0 - s))).astype(BF16)
            db = (dact * silu).astype(BF16)
            dab_ref[:, ga] = da
            dab_ref[:, ua] = db
            dhn = dhn + _dot_nt(da, wgu_ref[:, ga]) + _dot_nt(db, wgu_ref[:, ua])
        dx, dg = _rms_bwd(dhn, y, rstd, gx)
        dh_ref[...] = dout_x + dx
        _acc(step, dg_ref, dg)

    return _rowcall(name, body, rows, [h, dout], [g, wgu, wd],
                    [(d, F32), (d, BF16), (2 * f, BF16), (f, BF16)], [((1, d), F32)])


def _mm_tn(name, a, b, scale=1.0):
    rows, k1 = a.shape
    k2 = b.shape[1]
    tn = k2
    for cand in (512, 704, 1408, 1024):
        if k2 % cand == 0 and k1 * cand * 4 <= 6 * 1024 * 1024:
            tn = cand
    tm = _row_tile(rows)
    steps = rows // tm

    def kern(a_ref, b_ref, o_ref):
        bx = b_ref[...]
        if scale != 1.0:
            bx = bx * scale
        _acc(pl.program_id(1), o_ref, _dot_tn(a_ref[...], bx))

    return pl.pallas_call(
        kern, name=name, grid=(k2 // tn, steps),
        in_specs=[pl.BlockSpec((tm, k1), lambda j, i: (i, 0)), pl.BlockSpec((tm, tn), lambda j, i: (i, j))],
        out_specs=pl.BlockSpec((k1, tn), lambda j, i: (0, j)),
        out_shape=jax.ShapeDtypeStruct((k1, k2), F32),
        compiler_params=_cparams(("arbitrary", "arbitrary")),
    )(a, b)


def _proj_fwd(name, h, g, w):
    rows = h.shape[0]

    def body(step, h_ref, g_ref, w_ref, o_ref):
        o_ref[...] = _dot(_rms(h_ref[...], g_ref[...])[0], w_ref[...])

    return _rowcall(name, body, rows, [h], [g, w], [(w.shape[1], F32)])[0]


def _proj_bwd(name, h, g, w, dy, dres):
    rows, d = h.shape

    def body(step, h_ref, dy_ref, dr_ref, g_ref, w_ref, dh_ref, dg_ref, dw_ref):
        gx = g_ref[...]
        hn, y, rstd = _rms(h_ref[...], gx)
        dyx = dy_ref[...]
        dx, dg = _rms_bwd(_dot_nt(dyx, w_ref[...]), y, rstd, gx)
        dh_ref[...] = dr_ref[...] + dx
        _acc(step, dg_ref, dg)
        _acc(step, dw_ref, _dot_tn(hn, dyx))

    return _rowcall(name, body, rows, [h, dy, dres], [g, w], [(d, F32)], [((1, d), F32), (w.shape, F32)])


def _lin_res_fwd(name, a, w, res):
    rows = a.shape[0]

    def body(step, a_ref, r_ref, w_ref, o_ref):
        o_ref[...] = r_ref[...] + _dot(a_ref[...], w_ref[...])

    return _rowcall(name, body, rows, [a, res], [w], [(w.shape[1], F32)])[0]


def _lin_bwd(name, a, w, dy):
    rows, k = a.shape

    def body(step, a_ref, dy_ref, w_ref, da_ref, dw_ref):
        dyx = dy_ref[...]
        da_ref[...] = _dot_nt(dyx, w_ref[...])
        _acc(step, dw_ref, _dot_tn(a_ref[...], dyx))

    return _rowcall(name, body, rows, [a, dy], [w], [(k, F32)], [(w.shape, F32)])


def _s5_param_fn(lr, li, ls, brt, bit):
    step = jnp.exp(ls)
    mag = jnp.exp(lr * step)
    ar = mag * jnp.cos(li * step)
    ai = mag * jnp.sin(li * step)
    den = lr * lr + li * li
    nr, ni = ar - 1.0, ai
    cr = (nr * lr + ni * li) / den
    ci = (ni * lr - nr * li) / den
    return ar, ai, cr * brt - ci * bit, cr * bit + ci * brt


def _s5_params_fwd(lr, li, ls, brt, bit):
    def kern(lr_ref, li_ref, ls_ref, br_ref, bi_ref, ar_ref, ai_ref, bbr_ref, bbi_ref):
        ar, ai, bbr, bbi = _s5_param_fn(lr_ref[...], li_ref[...], ls_ref[...], br_ref[...], bi_ref[...])
        ar_ref[...], ai_ref[...], bbr_ref[...], bbi_ref[...] = ar, ai, bbr, bbi

    sd = jax.ShapeDtypeStruct
    return pl.pallas_call(
        kern, name="s5_params_fwd",
        out_shape=[sd(lr.shape, F32), sd(lr.shape, F32), sd(brt.shape, F32), sd(brt.shape, F32)],
    )(lr, li, ls, brt, bit)


def _s5_params_bwd(lr, li, ls, brt, bit, dar, dai, dbbr, dbbi):
    def kern(lr_ref, li_ref, ls_ref, br_ref, bi_ref, dar_ref, dai_ref, dbbr_ref, dbbi_ref,
             dlr_ref, dli_ref, dls_ref, dbr_ref, dbi_ref):
        _, vjp = jax.vjp(_s5_param_fn, lr_ref[...], li_ref[...], ls_ref[...], br_ref[...], bi_ref[...])
        dlr, dli, dls, dbr, dbi = vjp((dar_ref[...], dai_ref[...], dbbr_ref[...], dbbi_ref[...]))
        dlr_ref[...], dli_ref[...], dls_ref[...], dbr_ref[...], dbi_ref[...] = dlr, dli, dls, dbr, dbi

    sd = jax.ShapeDtypeStruct
    return pl.pallas_call(
        kern, name="s5_params_bwd",
        out_shape=[sd(lr.shape, F32), sd(lr.shape, F32), sd(ls.shape, F32), sd(brt.shape, F32), sd(brt.shape, F32)],
    )(lr, li, ls, brt, bit, dar, dai, dbbr, dbbi)


SCAN_LW = 512


def _scan_tables(a_ref, tab_ref, conj):
    ns = a_ref.shape[1]
    ar = jnp.broadcast_to(a_ref[0:1, :], (8, ns))
    ai = jnp.broadcast_to(a_ref[1:2, :], (8, ns))
    if conj:
        ai = -ai
    p1r, p1i = ar, ai
    p2r, p2i = p1r * p1r - p1i * p1i, 2.0 * p1r * p1i
    p4r, p4i = p2r * p2r - p2i * p2i, 2.0 * p2r * p2i
    row = lax.broadcasted_iota(jnp.int32, (8, ns), 0)
    e = row if not conj else 7 - row
    one, zero = jnp.ones((8, ns), F32), jnp.zeros((8, ns), F32)
    qr, qi = p1r, p1i
    for bit, (pr, pi) in ((1, (p1r, p1i)), (2, (p2r, p2i)), (4, (p4r, p4i))):
        sel = (e & bit) != 0
        fr, fi = jnp.where(sel, pr, one), jnp.where(sel, pi, zero)
        qr, qi = qr * fr - qi * fi, qr * fi + qi * fr
    for k, v in enumerate((p1r, p1i, p2r, p2i, p4r, p4i, qr, qi)):
        tab_ref[k] = v


def _scan_block(x_ref, tab_ref, carry_ref, t_rows, ns, reverse):
    ngrp = t_rows // 8
    row = lax.broadcasted_iota(jnp.int32, (8, SCAN_LW), 0)
    for lc in range(ns // SCAN_LW):
        lre = pl.ds(lc * SCAN_LW, SCAN_LW)
        lim = pl.ds(ns + lc * SCAN_LW, SCAN_LW)

        def group(k, carry, lre=lre, lim=lim):
            cr, ci = carry
            gi = (ngrp - 1 - k) if reverse else k
            rows = pl.ds(pl.multiple_of(gi * 8, 8), 8)
            vr, vi = x_ref[rows, lre], x_ref[rows, lim]
            for lvl, dsh in enumerate((1, 2, 4)):
                pr, pi = tab_ref[2 * lvl, :, lre], tab_ref[2 * lvl + 1, :, lre]
                if reverse:
                    keep = row < 8 - dsh
                    sr, si = pltpu.roll(vr, 8 - dsh, 0), pltpu.roll(vi, 8 - dsh, 0)
                else:
                    keep = row >= dsh
                    sr, si = pltpu.roll(vr, dsh, 0), pltpu.roll(vi, dsh, 0)
                sr, si = jnp.where(keep, sr, 0.0), jnp.where(keep, si, 0.0)
                vr, vi = vr + pr * sr - pi * si, vi + pr * si + pi * sr
            qr, qi = tab_ref[6, :, lre], tab_ref[7, :, lre]
            vr, vi = vr + qr * cr - qi * ci, vi + qr * ci + qi * cr
            x_ref[rows, lre], x_ref[rows, lim] = vr, vi
            edge = 0 if reverse else 7
            return (jnp.broadcast_to(vr[edge:edge + 1, :], (8, SCAN_LW)),
                    jnp.broadcast_to(vi[edge:edge + 1, :], (8, SCAN_LW)))

        cr, ci = lax.fori_loop(0, ngrp, group, (carry_ref[:, lre], carry_ref[:, lim]))
        carry_ref[:, lre], carry_ref[:, lim] = cr, ci


def _scan_rows(lp):
    for t in (384, 256, 128):
        if lp % t == 0:
            return t
    raise ValueError(lp)


def _s5_scan_fwd(u, bfull, cfull, a2, dvec, bsz):
    rows, hw = u.shape
    ns = a2.shape[1]
    lp = rows // bsz
    t_rows = _scan_rows(lp)
    nc = lp // t_rows

    def kern(u_ref, b_ref, c_ref, a_ref, d_ref, y_ref, x_ref, tab_ref, carry_ref):
        c = pl.program_id(1)

        @pl.when((pl.program_id(0) == 0) & (c == 0))
        def _():
            _scan_tables(a_ref, tab_ref, conj=False)

        @pl.when(c == 0)
        def _():
            carry_ref[...] = jnp.zeros_like(carry_ref)

        ux = u_ref[...]
        x_ref[...] = _dot(ux, b_ref[...])
        _scan_block(x_ref, tab_ref, carry_ref, t_rows, ns, reverse=False)
        y_ref[...] = _dot(x_ref[...], c_ref[...]) + d_ref[...] * ux

    const = lambda shp: pl.BlockSpec(shp, lambda b, c: (0,) * len(shp), pipeline_mode=pl.Buffered(1))
    return pl.pallas_call(
        kern, name="s5_scan_fwd", grid=(bsz, nc),
        in_specs=[pl.BlockSpec((t_rows, hw), lambda b, c: (b * nc + c, 0)),
                  const(bfull.shape), const(cfull.shape), const(a2.shape), const(dvec.shape)],
        out_specs=[pl.BlockSpec((t_rows, hw), lambda b, c: (b * nc + c, 0)),
                   pl.BlockSpec((t_rows, 2 * ns), lambda b, c: (b * nc + c, 0))],
        out_shape=[jax.ShapeDtypeStruct((rows, hw), F32), jax.ShapeDtypeStruct((rows, 2 * ns), F32)],
        scratch_shapes=[pltpu.VMEM((8, 8, ns), F32), pltpu.VMEM((8, 2 * ns), F32)],
        compiler_params=_cparams(("arbitrary", "arbitrary")),
    )(u, bfull, cfull, a2, dvec)


def _s5_scan_bwd(dy, u, xs, ctfull, btfull, a2, dvec, bsz):
    rows, hw = u.shape
    ns = a2.shape[1]
    lp = rows // bsz
    t_rows = _scan_rows(lp)
    nc = lp // t_rows
    blk = lambda b, c: (b * nc + (nc - 1 - c), 0)

    def prev8(b, c):
        first = (b * nc + (nc - 1 - c)) * (t_rows // 8)
        return (jnp.maximum(first - 1, 0), 0)

    def kern(dy_ref, u_ref, x_ref, xp_ref, ct_ref, bt_ref, a_ref, d_ref, du_ref, gx_ref, da_ref, dd_ref,
             tab_ref, carry_ref):
        b, c = pl.program_id(0), pl.program_id(1)
        first = (b == 0) & (c == 0)

        @pl.when(first)
        def _():
            _scan_tables(a_ref, tab_ref, conj=True)

        @pl.when(c == 0)
        def _():
            carry_ref[...] = jnp.zeros_like(carry_ref)

        dyx, ux = dy_ref[...], u_ref[...]
        gx_ref[...] = _dot(dyx, ct_ref[...])
        _scan_block(gx_ref, tab_ref, carry_ref, t_rows, ns, reverse=True)
        gx = gx_ref[...]
        du_ref[...] = _dot(gx, bt_ref[...]) + d_ref[...] * dyx
        xprev = pltpu.roll(x_ref[...], 1, 0)
        seq_start = c == nc - 1
        head = jnp.where(seq_start, 0.0, xp_ref[7:8, :])
        rid = lax.broadcasted_iota(jnp.int32, (t_rows, 1), 0)
        xprev = jnp.where(rid == 0, head, xprev)
        xr, xi, gr, gi = xprev[:, :ns], xprev[:, ns:], gx[:, :ns], gx[:, ns:]
        da = jnp.concatenate([jnp.sum(xr * gr + xi * gi, axis=0, keepdims=True),
                              jnp.sum(xr * gi - xi * gr, axis=0, keepdims=True)], axis=1)
        dd = jnp.sum(dyx * ux, axis=0, keepdims=True)

        @pl.when(first)
        def _():
            da_ref[...] = da
            dd_ref[...] = dd

        @pl.when(jnp.logical_not(first))
        def _():
            da_ref[...] += da
            dd_ref[...] += dd

    const = lambda shp: pl.BlockSpec(shp, lambda b, c: (0,) * len(shp), pipeline_mode=pl.Buffered(1))
    return pl.pallas_call(
        kern, name="s5_scan_bwd", grid=(bsz, nc),
        in_specs=[pl.BlockSpec((t_rows, hw), blk), pl.BlockSpec((t_rows, hw), blk),
                  pl.BlockSpec((t_rows, 2 * ns), blk), pl.BlockSpec((8, 2 * ns), prev8),
                  const(ctfull.shape), const(btfull.shape), const(a2.shape), const(dvec.shape)],
        out_specs=[pl.BlockSpec((t_rows, hw), blk), pl.BlockSpec((t_rows, 2 * ns), blk),
                   pl.BlockSpec((1, 2 * ns), lambda b, c: (0, 0)), pl.BlockSpec((1, hw), lambda b, c: (0, 0))],
        out_shape=[jax.ShapeDtypeStruct((rows, hw), F32), jax.ShapeDtypeStruct((rows, 2 * ns), F32),
                   jax.ShapeDtypeStruct((1, 2 * ns), F32), jax.ShapeDtypeStruct((1, hw), F32)],
        scratch_shapes=[pltpu.VMEM((8, 8, ns), F32), pltpu.VMEM((8, 2 * ns), F32)],
        compiler_params=_cparams(("arbitrary", "arbitrary")),
    )(dy, u, xs, xs, ctfull, btfull, a2, dvec)


def _glu_fwd(y, h1, wout):
    rows, d = h1.shape

    def body(step, y_ref, h_ref, w_ref, o_ref):
        z = _dot(_gelu(y_ref[...])[0], w_ref[...])
        o_ref[...] = h_ref[...] + z[:, :d] * _sigmoid(z[:, d:])

    return _rowcall("glu_fwd", body, rows, [y, h1], [wout], [(d, F32)])[0]


def _glu_bwd(y, dh2, wout):
    rows, d = dh2.shape
    hw = y.shape[1]

    def body(step, y_ref, dh_ref, w_ref, dy_ref, dw_ref):
        yx, dh = y_ref[...], dh_ref[...]
        gl, t = _gelu(yx)
        z = _dot(gl, w_ref[...])
        za, sg = z[:, :d], _sigmoid(z[:, d:])
        dza = dh * sg
        dzg = dh * za * sg * (1.0 - sg)
        dgl = _dot_nt(dza, w_ref[:, :d]) + _dot_nt(dzg, w_ref[:, d:])
        dy_ref[...] = dgl * _gelu_grad(yx, t)
        dw = jnp.concatenate([_dot_tn(gl, dza), _dot_tn(gl, dzg)], axis=1)
        _acc(step, dw_ref, dw)

    return _rowcall("glu_bwd", body, rows, [y, dh2], [wout], [(hw, F32)], [((hw, 2 * d), F32)])


def _gmean64(x2, gmat):
    hi = x2.astype(BF16)
    r1 = x2 - hi.astype(F32)
    mid = r1.astype(BF16)
    lo = (r1 - mid.astype(F32)).astype(BF16)
    outs = []
    for j in range(x2.shape[1] // LANES):
        sl = slice(j * LANES, (j + 1) * LANES)
        f = lambda p: jnp.dot(p[:, sl], gmat, preferred_element_type=F32)
        outs.append(f(hi) + f(mid) + f(lo))
    return outs[0] if len(outs) == 1 else jnp.concatenate(outs, axis=1)


def _swap32(x):
    w = x.shape[1]
    lane = lax.broadcasted_iota(jnp.int32, (1, w), 1)
    return jnp.where((lane & 32) == 0, pltpu.roll(x, w - 32, 1), pltpu.roll(x, 32, 1))


def _tile_lanes(t, w):
    reps = w // t.shape[1]
    return t if reps == 1 else jnp.concatenate([t] * reps, axis=1)


def _headrope_fwd(name, raw, w, gain, cos, sin, gmat, lp):
    rows = raw.shape[0]
    tm = _row_tile(lp)
    per = lp // tm

    def body(step, x_ref, c_ref, s_ref, g_ref, gm_ref, o_ref):
        x = x_ref[...]
        rstd = lax.rsqrt(_gmean64(x * x, gm_ref[...]) + EPS)
        z = x * rstd * g_ref[...]
        o_ref[...] = z * _tile_lanes(c_ref[...], w) + _swap32(z) * _tile_lanes(s_ref[...], w)

    maps = [((tm, w), lambda i: (i, 0)), ((tm, LANES), lambda i: (i % per, 0)), ((tm, LANES), lambda i: (i % per, 0))]
    return _rowcall(name, body, rows, [raw, cos, sin], [gain, gmat], [(w, F32)], tm=tm, row_in_maps=maps)[0]


def _headrope_bwd(name, raw, w, dout, gain, cos, sin, gmat, lp):
    rows = raw.shape[0]
    tm = _row_tile(lp)
    per = lp // tm

    def body(step, x_ref, do_ref, c_ref, s_ref, g_ref, gm_ref, dx_ref, dg_ref):
        x, dout_x, gx, gm = x_ref[...], do_ref[...], g_ref[...], gm_ref[...]
        rstd = lax.rsqrt(_gmean64(x * x, gm) + EPS)
        yn = x * rstd
        dz = dout_x * _tile_lanes(c_ref[...], w) + _swap32(dout_x * _tile_lanes(s_ref[...], w))
        dyn = dz * gx
        dx_ref[...] = rstd * (dyn - yn * _gmean64(dyn * yn, gm))
        dg = jnp.sum(dz * yn, axis=0, keepdims=True)
        sh = w // 2
        while sh >= HEAD_DIM:
            dg = dg + pltpu.roll(dg, sh, 1)
            sh //= 2
        _acc(step, dg_ref, dg)

    maps = [((tm, w), lambda i: (i, 0)), None, ((tm, LANES), lambda i: (i % per, 0)), ((tm, LANES), lambda i: (i % per, 0))]
    return _rowcall(name, body, rows, [raw, dout, cos, sin], [gain, gmat], [(w, F32)], [((1, w), F32)],
                    tm=tm, row_in_maps=maps)


KVW = N_KV_HEADS * HEAD_DIM
QB = 128


def _fold4(x):
    y = x + pltpu.roll(x, 128, 1)
    return y + pltpu.roll(y, 64, 1)


def _attn_scores(i, q_ref, k0_ref, kp_ref, kc_ref, sink_ref, h):
    lane = lax.broadcasted_iota(jnp.int32, (1, KVW), 1) // HEAD_DIM
    qh = q_ref[:, h * KVW:(h + 1) * KVW]
    qs = jnp.concatenate([jnp.where(lane == g, qh, 0.0) for g in range(Q_PER_KV)], axis=0).astype(BF16)
    hsel = lane == h
    kx = [_fold4(jnp.where(hsel, r[...], 0.0)).astype(BF16) for r in (k0_ref, kp_ref, kc_ref)]
    scale = HEAD_DIM ** -0.5
    s0, sp, sc = [_dot_nt(qs, k) * scale for k in kx]
    qi = lax.broadcasted_iota(jnp.int32, (Q_PER_KV * QB, QB), 0) % QB
    kj = lax.broadcasted_iota(jnp.int32, (Q_PER_KV * QB, QB), 1)
    s0 = jnp.where(kj >= META0, s0, NEG_INF)
    sp = jnp.where((kj > qi) & (i >= 2), sp, NEG_INF)
    sc = jnp.where(kj <= qi, sc, NEG_INF)
    rowg = lax.broadcasted_iota(jnp.int32, (Q_PER_KV * QB, 1), 0) // QB
    sink = jnp.zeros((Q_PER_KV * QB, 1), F32)
    for g in range(Q_PER_KV):
        sink = jnp.where(rowg == g, sink_ref[0, h * Q_PER_KV + g], sink)
    m = jnp.maximum(jnp.maximum(jnp.max(s0, axis=1, keepdims=True), jnp.max(sp, axis=1, keepdims=True)),
                    jnp.maximum(jnp.max(sc, axis=1, keepdims=True), sink))
    p0, pp, pc, ps = jnp.exp(s0 - m), jnp.exp(sp - m), jnp.exp(sc - m), jnp.exp(sink - m)
    den = jnp.sum(p0, axis=1, keepdims=True) + jnp.sum(pp, axis=1, keepdims=True) + jnp.sum(pc, axis=1, keepdims=True) + ps
    return qs, kx, (p0, pp, pc), ps, den, lane, hsel


def _unstack(x, lane):
    out = jnp.where(lane == 0, x[0:QB], 0.0)
    for g in range(1, Q_PER_KV):
        out = out + jnp.where(lane == g, x[g * QB:(g + 1) * QB], 0.0)
    return out


def _attn_specs(nb, d):
    qspec = pl.BlockSpec((None, QB, d), lambda b, i: (b, i, 0))
    k0 = pl.BlockSpec((None, QB, KVW), lambda b, i: (b, 0, 0))
    kp = pl.BlockSpec((None, QB, KVW), lambda b, i: (b, jnp.maximum(i - 1, 0), 0))
    kc = pl.BlockSpec((None, QB, KVW), lambda b, i: (b, i, 0))
    v0 = pl.BlockSpec((None, QB, KVW), lambda b, i: (b, 0, 1))
    vp = pl.BlockSpec((None, QB, KVW), lambda b, i: (b, jnp.maximum(i - 1, 0), 1))
    vc = pl.BlockSpec((None, QB, KVW), lambda b, i: (b, i, 1))
    sink = pl.BlockSpec(memory_space=pltpu.SMEM)
    return qspec, [k0, kp, kc], [v0, vp, vc], sink


def _attn_fwd(q, k, kv, sinks):
    bsz, lp, d = q.shape
    nb = lp // QB
    qspec, kspecs, vspecs, sspec = _attn_specs(nb, d)

    def kern(q_ref, k0_ref, kp_ref, kc_ref, v0_ref, vp_ref, vc_ref, sink_ref, o_ref):
        i = pl.program_id(1)
        for h in range(N_KV_HEADS):
            qs, kx, ps3, psink, den, lane, hsel = _attn_scores(i, q_ref, k0_ref, kp_ref, kc_ref, sink_ref, h)
            vx = [_fold4(jnp.where(hsel, r[...], 0.0)).astype(BF16) for r in (v0_ref, vp_ref, vc_ref)]
            o = _dot(ps3[0], vx[0]) + _dot(ps3[1], vx[1]) + _dot(ps3[2], vx[2])
            o_ref[:, h * KVW:(h + 1) * KVW] = _unstack(o / den, lane)

    return pl.pallas_call(
        kern, name="attn_fwd", grid=(bsz, nb),
        in_specs=[qspec] + kspecs + vspecs + [sspec],
        out_specs=qspec, out_shape=jax.ShapeDtypeStruct((bsz, lp, d), F32),
        compiler_params=_cparams(("arbitrary", "arbitrary")),
    )(q, k, k, k, kv, kv, kv, sinks)


def _attn_bwd(q, k, kv, sinks, o, do):
    bsz, lp, d = q.shape
    nb = lp // QB
    qspec, kspecs, vspecs, sspec = _attn_specs(nb, d)
    full = pl.BlockSpec((None, lp, KVW), lambda b, i: (b, 0, 0))

    def kern(q_ref, k0_ref, kp_ref, kc_ref, v0_ref, vp_ref, vc_ref, sink_ref, o_ref, do_ref,
             dq_ref, dk_ref, dv_ref, ds_ref):
        b, i = pl.program_id(0), pl.program_id(1)

        @pl.when(i == 0)
        def _():
            dk_ref[...] = jnp.zeros_like(dk_ref)
            dv_ref[...] = jnp.zeros_like(dv_ref)

        @pl.when((b == 0) & (i == 0))
        def _():
            ds_ref[...] = jnp.zeros_like(ds_ref)

        lane128 = lax.broadcasted_iota(jnp.int32, (1, LANES), 1)
        rowg = lax.broadcasted_iota(jnp.int32, (Q_PER_KV * QB, 1), 0) // QB
        dk_acc = [jnp.zeros((QB, KVW), F32) for _ in range(3)]
        dv_acc = [jnp.zeros((QB, KVW), F32) for _ in range(3)]
        dsink = jnp.zeros((1, LANES), F32)
        for h in range(N_KV_HEADS):
            qs, kx, ps3, psink, den, lane, hsel = _attn_scores(i, q_ref, k0_ref, kp_ref, kc_ref, sink_ref, h)
            vx = [_fold4(jnp.where(hsel, r[...], 0.0)).astype(BF16) for r in (v0_ref, vp_ref, vc_ref)]
            sl = slice(h * KVW, (h + 1) * KVW)
            doh, oh = do_ref[:, sl], o_ref[:, sl]
            dos = jnp.concatenate([jnp.where(lane == g, doh, 0.0) for g in range(Q_PER_KV)], axis=0)
            ost = jnp.concatenate([jnp.where(lane == g, oh, 0.0) for g in range(Q_PER_KV)], axis=0)
            delta = jnp.sum(dos * ost, axis=1, keepdims=True)
            inv = 1.0 / den
            dosb = dos.astype(BF16)
            dqs = jnp.zeros((Q_PER_KV * QB, KVW), F32)
            for n in range(3):
                pn = ps3[n] * inv
                ds = pn * (_dot_nt(dosb, vx[n]) - delta) * (HEAD_DIM ** -0.5)
                dqs = dqs + _dot(ds, kx[n])
                dk_acc[n] = dk_acc[n] + jnp.where(hsel, _fold4(_dot_tn(ds, qs)), 0.0)
                dv_acc[n] = dv_acc[n] + jnp.where(hsel, _fold4(_dot_tn(pn, dosb)), 0.0)
            dq_ref[:, sl] = _unstack(dqs, lane)
            dsk = -(psink * inv) * delta
            for g in range(Q_PER_KV):
                val = jnp.sum(jnp.where(rowg == g, dsk, 0.0), axis=0, keepdims=True)
                dsink = dsink + jnp.where(lane128 == h * Q_PER_KV + g, val, 0.0)
        ds_ref[...] += dsink
        r0 = pl.ds(0, QB)
        rp = pl.ds(pl.multiple_of(jnp.maximum(i - 1, 0) * QB, QB), QB)
        rc = pl.ds(pl.multiple_of(i * QB, QB), QB)
        for rows, n in ((r0, 0), (rp, 1), (rc, 2)):
            dk_ref[rows, :] += dk_acc[n]
            dv_ref[rows, :] += dv_acc[n]

    return pl.pallas_call(
        kern, name="attn_bwd", grid=(bsz, nb),
        in_specs=[qspec] + kspecs + vspecs + [sspec, qspec, qspec],
        out_specs=[qspec, full, full, pl.BlockSpec((1, LANES), lambda b, i: (0, 0))],
        out_shape=[jax.ShapeDtypeStruct((bsz, lp, d), F32), jax.ShapeDtypeStruct((bsz, lp, KVW), F32),
                   jax.ShapeDtypeStruct((bsz, lp, KVW), F32), jax.ShapeDtypeStruct((1, LANES), F32)],
        compiler_params=_cparams(("arbitrary", "arbitrary")),
    )(q, k, k, k, kv, kv, kv, sinks, o, do)


def _concat_cols(name, a, b):
    rows = a.shape[0]

    def body(step, a_ref, b_ref, o_ref):
        o_ref[...] = jnp.concatenate([a_ref[...], b_ref[...]], axis=1)

    return _rowcall(name, body, rows, [a, b], [], [(a.shape[1] + b.shape[1], F32)])[0]


def _adamw(name, w, m, v, parts):
    rows, wd = w.shape
    n = parts.shape[0]
    tm = _row_tile(rows)

    def kern(w_ref, m_ref, v_ref, p_ref, g_ref, d_ref, m2_ref, v2_ref):
        g = p_ref[0]
        for k in range(1, n):
            g = g + p_ref[k]
        m2 = ADAM_B1 * m_ref[...] + (1.0 - ADAM_B1) * g
        v2 = ADAM_B2 * v_ref[...] + (1.0 - ADAM_B2) * (g * g)
        mh = m2 / (1.0 - ADAM_B1 ** ADAM_STEP)
        vh = v2 / (1.0 - ADAM_B2 ** ADAM_STEP)
        g_ref[...] = g
        d_ref[...] = -ADAM_LR * (mh / (jnp.sqrt(vh) + ADAM_EPS) + ADAM_WD * w_ref[...])
        m2_ref[...] = m2
        v2_ref[...] = v2

    spec = pl.BlockSpec((tm, wd), lambda i: (i, 0))
    sd = jax.ShapeDtypeStruct((rows, wd), F32)
    return pl.pallas_call(
        kern, name=name, grid=(rows // tm,),
        in_specs=[spec, spec, spec, pl.BlockSpec((n, tm, wd), lambda i: (0, i, 0))],
        out_specs=[spec] * 4, out_shape=[sd] * 4,
        compiler_params=_cparams(("arbitrary",)),
    )(w, m, v, parts)


def _add4(name, mine, other):
    n, rows, wd = mine.shape
    tm = _row_tile(rows)
    spec = pl.BlockSpec((1, tm, wd), lambda k, i: (k, i, 0))

    def kern(a_ref, b_ref, o_ref):
        o_ref[...] = a_ref[...] + b_ref[...]

    return pl.pallas_call(
        kern, name=name, grid=(n, rows // tm), in_specs=[spec, spec], out_specs=spec,
        out_shape=jax.ShapeDtypeStruct(mine.shape, F32), compiler_params=_cparams(("arbitrary", "arbitrary")),
    )(mine, other)


MESH = pl.DeviceIdType.MESH
ANY = pl.BlockSpec(memory_space=pl.ANY)


def _allgather(name, shard):
    def body(x_ref, out_ref, send_sems, recv_sems, local_sem):
        x, y, c = lax.axis_index("x"), lax.axis_index("y"), lax.axis_index("c")
        me, sibling = (x, y, c), (x, y, 1 - c)
        chips = [(1 - x, y), (x, 1 - y), (1 - x, 1 - y)]

        def slot(px, py, pc):
            return out_ref.at[4 * px + 2 * py + pc]

        def copy(k, block, to, src=None):
            return pltpu.make_async_remote_copy(
                src_ref=slot(*block) if src is None else src, dst_ref=slot(*block),
                send_sem=send_sems.at[k], recv_sem=recv_sems.at[k], device_id=to, device_id_type=MESH)

        mine = pltpu.make_async_copy(x_ref, slot(*me), local_sem)
        mine.start()
        first = [copy(0, me, sibling, src=x_ref)]
        first += [copy(1 + j, me, (*chip, c), src=x_ref) for j, chip in enumerate(chips)]
        for cp in first:
            cp.start()
        passed = [copy(4 + j, (*chip, c), sibling) for j, chip in enumerate(chips)]
        for j, chip in enumerate(chips):
            copy(1 + j, (*chip, c), me).wait_recv()
            passed[j].start()
        copy(0, sibling, me).wait_recv()
        for j, chip in enumerate(chips):
            copy(4 + j, (*chip, 1 - c), me).wait_recv()
        for cp in first + passed:
            cp.wait_send()
        mine.wait()

    return pl.pallas_call(
        body, name=name, out_shape=jax.ShapeDtypeStruct((N_DEV,) + shard.shape, shard.dtype),
        in_specs=[ANY], out_specs=ANY,
        scratch_shapes=[pltpu.SemaphoreType.DMA((7,)), pltpu.SemaphoreType.DMA((7,)), pltpu.SemaphoreType.DMA],
    )(shard)


def _sibling_swap(name, parts):
    def body(p_ref, out_ref, send_sem, recv_sem):
        x, y, c = lax.axis_index("x"), lax.axis_index("y"), lax.axis_index("c")
        cp = pltpu.make_async_remote_copy(
            src_ref=p_ref.at[1 - c], dst_ref=out_ref, send_sem=send_sem, recv_sem=recv_sem,
            device_id=(x, y, 1 - c), device_id_type=MESH)
        cp.start()
        cp.wait()

    return pl.pallas_call(
        body, name=name, out_shape=jax.ShapeDtypeStruct(parts.shape[1:], parts.dtype),
        in_specs=[ANY], out_specs=ANY,
        scratch_shapes=[pltpu.SemaphoreType.DMA, pltpu.SemaphoreType.DMA],
    )(parts)


def _chip_scatter(name, sums):
    def body(s_ref, out_ref, send_sems, recv_sems, local_sem):
        x, y, c = lax.axis_index("x"), lax.axis_index("y"), lax.axis_index("c")
        mychip = 2 * x + y
        chips = [(1 - x, y), (x, 1 - y), (1 - x, 1 - y)]
        mine = pltpu.make_async_copy(s_ref.at[mychip], out_ref.at[mychip], local_sem)
        mine.start()
        copies = []
        for j, (px, py) in enumerate(chips):
            copies.append(pltpu.make_async_remote_copy(
                src_ref=s_ref.at[2 * px + py], dst_ref=out_ref.at[mychip],
                send_sem=send_sems.at[j], recv_sem=recv_sems.at[j], device_id=(px, py, c), device_id_type=MESH))
        for cp in copies:
            cp.start()
        for j, (px, py) in enumerate(chips):
            pltpu.make_async_remote_copy(
                src_ref=s_ref.at[mychip], dst_ref=out_ref.at[2 * px + py],
                send_sem=send_sems.at[j], recv_sem=recv_sems.at[j], device_id=(px, py, c),
                device_id_type=MESH).wait_recv()
        for cp in copies:
            cp.wait_send()
        mine.wait()

    return pl.pallas_call(
        body, name=name, out_shape=jax.ShapeDtypeStruct(sums.shape, sums.dtype),
        in_specs=[ANY], out_specs=ANY,
        scratch_shapes=[pltpu.SemaphoreType.DMA((3,)), pltpu.SemaphoreType.DMA((3,)), pltpu.SemaphoreType.DMA],
    )(sums)


BIG = (("ffn1_w_gate_up", 2), ("ffn1_w_down", 1), ("ffn2_w_gate_up", 2), ("ffn2_w_down", 1), ("ssm_w_in", 1),
       ("ssm_w_out", 2), ("w_kv", 0), ("attn_w_q", 1), ("attn_w_o", 1))
SMALL = ("ffn1_norm", "mix_norm", "ffn2_norm", "ssm_lambda_re", "ssm_lambda_im", "ssm_b_re", "ssm_b_im",
         "ssm_c_re", "ssm_c_im", "ssm_log_step", "kv_norm", "k_norm", "q_norm", "attn_sinks")
COLS = (("meta_tokens", 1), ("ssm_d", 1))
WEIGHTS = ("meta_tokens", "ffn1_norm", "ffn1_w_gate_up", "ffn1_w_down", "mix_norm", "ffn2_norm", "ffn2_w_gate_up",
           "ffn2_w_down", "ssm_w_in", "ssm_lambda_re", "ssm_lambda_im", "ssm_b_re", "ssm_b_im", "ssm_c_re",
           "ssm_c_im", "ssm_log_step", "ssm_d", "ssm_w_out", "kv_norm", "w_kv", "k_norm", "attn_w_q", "q_norm",
           "attn_sinks", "attn_w_o")


def _pack_rows(arrs, dtype):
    flat = jnp.concatenate([a.astype(dtype).reshape(-1) for a in arrs])
    pad = (-flat.shape[0]) % (8 * PACK_W)
    if pad:
        flat = jnp.concatenate([flat, jnp.zeros((pad,), dtype)])
    return flat.reshape(-1, PACK_W)


def _unpack_rows(buf, shapes, lead=()):
    flat = buf.reshape(lead + (-1,))
    out, off = [], 0
    for shp in shapes:
        n = math.prod(shp)
        out.append(flat[..., off:off + n].reshape(lead + tuple(shp)))
        off += n
    return out


def _unshard(g, axis):
    g = jnp.moveaxis(g, 0, axis)
    shp = g.shape
    return g.reshape(shp[:axis] + (shp[axis] * shp[axis + 1],) + shp[axis + 2:])


def _shard(full, axis):
    shp = full.shape
    g = full.reshape(shp[:axis] + (N_DEV, shp[axis] // N_DEV) + shp[axis + 1:])
    return jnp.moveaxis(g, axis, 0)


def _blockdiag(blocks):
    g, r, c = blocks.shape
    eye = jnp.eye(g, dtype=blocks.dtype)
    return (eye[:, None, :, None] * blocks[:, :, None, :]).reshape(g * r, g * c)


def _diagblocks(full, g):
    r, c = full.shape[0] // g, full.shape[1] // g
    f = full.reshape(g, r, g, c)
    idx = jnp.arange(g)
    return f[idx, :, idx, :]


def kernel(x, meta_tokens, ffn1_norm, ffn1_w_gate_up, ffn1_w_down, mix_norm, ffn2_norm, ffn2_w_gate_up, ffn2_w_down, ssm_w_in, ssm_lambda_re, ssm_lambda_im, ssm_b_re, ssm_b_im, ssm_c_re, ssm_c_im, ssm_log_step, ssm_d, ssm_w_out, kv_norm, w_kv, k_norm, attn_w_q, q_norm, attn_sinks, attn_w_o, loss_target, m_meta_tokens, m_ffn1_norm, m_ffn1_w_gate_up, m_ffn1_w_down, m_mix_norm, m_ffn2_norm, m_ffn2_w_gate_up, m_ffn2_w_down, m_ssm_w_in, m_ssm_lambda_re, m_ssm_lambda_im, m_ssm_b_re, m_ssm_b_im, m_ssm_c_re, m_ssm_c_im, m_ssm_log_step, m_ssm_d, m_ssm_w_out, m_kv_norm, m_w_kv, m_k_norm, m_attn_w_q, m_q_norm, m_attn_sinks, m_attn_w_o, v_meta_tokens, v_ffn1_norm, v_ffn1_w_gate_up, v_ffn1_w_down, v_mix_norm, v_ffn2_norm, v_ffn2_w_gate_up, v_ffn2_w_down, v_ssm_w_in, v_ssm_lambda_re, v_ssm_lambda_im, v_ssm_b_re, v_ssm_b_im, v_ssm_c_re, v_ssm_c_im, v_ssm_log_step, v_ssm_d, v_ssm_w_out, v_kv_norm, v_w_kv, v_k_norm, v_attn_w_q, v_q_norm, v_attn_sinks, v_attn_w_o):
    args = dict(locals())
    W = {n: args[n] for n in WEIGHTS}
    M = {n: args["m_" + n] for n in WEIGHTS}
    V = {n: args["v_" + n] for n in WEIGHTS}
    my_x, my_y, my_c = (lax.axis_index(a) for a in MESH_AXES)
    my_dev = 4 * my_x + 2 * my_y + my_c

    big_shapes = [W[n].shape for n, _ in BIG]
    gathered = _allgather("gather_weights", _pack_rows([W[n] for n, _ in BIG], BF16))
    full = {n: _unshard(g, ax)
            for (n, ax), g in zip(BIG, _unpack_rows(gathered, big_shapes, (N_DEV,)))}
    col_shapes = [W[n].shape for n, _ in COLS]
    gathered_f32 = _allgather("gather_cols", _pack_rows([W[n] for n, _ in COLS], F32))
    full.update({n: _unshard(g, ax) for (n, ax), g in zip(COLS, _unpack_rows(gathered_f32, col_shapes, (N_DEV,)))})

    grads = _local_step(x, loss_target, {**W, **full})
    loss = lax.psum(grads.pop("loss"), MESH_AXES)
    grad_x = grads.pop("x")

    parts = jnp.concatenate([_shard(grads[n], ax).reshape(N_DEV, -1) for n, ax in BIG], axis=1)
    pad = (-parts.shape[1]) % (8 * PACK_W)
    if pad:
        parts = jnp.concatenate([parts, jnp.zeros((N_DEV, pad), F32)], axis=1)
    rows_big = parts.shape[1] // PACK_W
    parts = parts.reshape(4, 2, rows_big, PACK_W).transpose(1, 0, 2, 3)
    theirs = _sibling_swap("grad_sibling_swap", parts)
    mine = lax.dynamic_index_in_dim(parts, my_c, 0, keepdims=False)
    summed = _chip_scatter("grad_chip_scatter", _add4("grad_pair_sum", mine, theirs))
    packw = lambda d: _pack_rows([d[n] for n, _ in BIG], F32)
    outs_big = _adamw("adamw_big", packw(W), packw(M), packw(V), summed)
    outs_big = [dict(zip([n for n, _ in BIG], _unpack_rows(o, big_shapes))) for o in outs_big]

    small_names = list(SMALL) + [n for n, _ in COLS]
    small_shapes = [grads[n].shape for n in small_names]
    small_parts = _allgather("gather_small_grads", _pack_rows([grads[n] for n in small_names], F32))
    zero_cols = {n: jnp.zeros(grads[n].shape, F32) for n, _ in COLS}
    packs = lambda d: _pack_rows([d[n] for n in SMALL] + [zero_cols[n] for n, _ in COLS], F32)
    outs_small = _adamw("adamw_small", packs(W), packs(M), packs(V), small_parts)
    outs_small = [dict(zip(small_names, _unpack_rows(o, small_shapes))) for o in outs_small]
    col_g = []
    for n, ax in COLS:
        gfull = outs_small[0][n]
        wcol = W[n].shape[ax]
        col_g.append(lax.dynamic_slice_in_dim(gfull, my_dev * wcol, wcol, axis=ax))
    packc = lambda d: _pack_rows([d[n] for n, _ in COLS], F32)
    outs_col = _adamw("adamw_cols", packc(W), packc(M), packc(V), _pack_rows(col_g, F32)[None])
    outs_col = [dict(zip([n for n, _ in COLS], _unpack_rows(o, col_shapes))) for o in outs_col]

    res = []
    for k in range(4):
        merged = {**outs_big[k], **outs_small[k], **outs_col[k]}
        res.append([merged[n] for n in WEIGHTS])
    return (loss, grad_x, *res[0], *res[1], *res[2], *res[3])


def _local_step(x, target, P):
    bsz, seq, d = x.shape
    lp = seq + PAD
    rows = bsz * lp
    depth = P["ffn1_norm"].shape[0]
    assert depth == 2
    f = P["ffn1_w_down"].shape[1]
    bf = lambda a: a.astype(BF16)
    row = lambda a: a.reshape(1, -1)

    pos = (jnp.arange(lp, dtype=F32) - float(META0))[:, None]
    half = HEAD_DIM // 2
    freqs = ROPE_THETA ** (-jnp.arange(0, half, dtype=F32) * 2.0 / HEAD_DIM)
    ang = pos * freqs[None, :]
    cos_t = jnp.tile(jnp.cos(ang), (1, LANES // half))
    sin_t = jnp.tile(jnp.concatenate([-jnp.sin(ang), jnp.sin(ang)], axis=1), (1, LANES // HEAD_DIM))
    gi = jnp.arange(LANES) // HEAD_DIM
    gmat = jnp.where(gi[:, None] == gi[None, :], 1.0 / HEAD_DIM, 0.0).astype(BF16)

    g_n, c_n, p_n = P["ssm_lambda_re"].shape[1], SSM_GROUP, SSM_STATE
    ns = g_n * p_n
    lr = P["ssm_lambda_re"][0].reshape(g_n, 1, p_n)
    li = P["ssm_lambda_im"][0].reshape(g_n, 1, p_n)
    ls = P["ssm_log_step"][0].reshape(g_n, 1, 1)
    brt = P["ssm_b_re"][0].transpose(0, 2, 1)
    bit = P["ssm_b_im"][0].transpose(0, 2, 1)
    ar, ai, bbr, bbi = _s5_params_fwd(lr, li, ls, brt, bit)
    a2 = jnp.concatenate([ar.reshape(1, ns), ai.reshape(1, ns)], axis=0)
    bfull = jnp.concatenate([_blockdiag(bbr), _blockdiag(bbi)], axis=1)
    cre_t = P["ssm_c_re"][0].transpose(0, 2, 1)
    cim_t = P["ssm_c_im"][0].transpose(0, 2, 1)
    cfull = jnp.concatenate([_blockdiag(cre_t), -_blockdiag(cim_t)], axis=0)
    dvec = P["ssm_d"].reshape(1, -1)

    ffn = lambda which, l: (row(P[which + "_norm"][l]), bf(P[which + "_w_gate_up"][l]), bf(P[which + "_w_down"][l]))
    w_in, w_out = bf(P["ssm_w_in"][0]), bf(P["ssm_w_out"][0])
    w_kv, w_q, w_o = bf(P["w_kv"]), bf(P["attn_w_q"][0]), bf(P["attn_w_o"][0])
    mix0, mix1, kvn = row(P["mix_norm"][0]), row(P["mix_norm"][1]), row(P["kv_norm"])
    kgain = jnp.tile(P["k_norm"].reshape(1, HEAD_DIM), (1, KVW // HEAD_DIM))
    qgain = jnp.tile(P["q_norm"].reshape(1, HEAD_DIM), (1, d // HEAD_DIM))
    sinks = P["attn_sinks"].reshape(1, -1)

    h0 = _embed(x, P["meta_tokens"]).reshape(rows, d)
    h1 = _ffn_fwd("ffn1_0_fwd", h0, *ffn("ffn1", 0))
    u = _proj_fwd("ssm_in_fwd", h1, mix0, w_in)
    y, xs = _s5_scan_fwd(u, bf(bfull), bf(cfull), a2, dvec, bsz)
    h2 = _glu_fwd(y, h1, w_out)
    h3 = _ffn_fwd("ffn2_0_fwd", h2, *ffn("ffn2", 0))
    kv = _proj_fwd("kv_fwd", h3, kvn, w_kv)
    k = _headrope_fwd("k_rope_fwd", kv, KVW, kgain, cos_t, sin_t, gmat, lp)
    h4 = _ffn_fwd("ffn1_1_fwd", h3, *ffn("ffn1", 1))
    q_raw = _proj_fwd("q_fwd", h4, mix1, w_q)
    q = _headrope_fwd("q_rope_fwd", q_raw, d, qgain, cos_t, sin_t, gmat, lp)
    r3 = lambda a: a.reshape(bsz, lp, a.shape[-1])
    o = _attn_fwd(r3(q), r3(k), r3(kv), sinks).reshape(rows, d)
    h5 = _lin_res_fwd("attn_out_fwd", o, w_o, h4)
    h6 = _ffn_fwd("ffn2_1_fwd", h5, *ffn("ffn2", 1))
    loss, dh6 = _loss(r3(h6), target)
    dh6 = dh6.reshape(rows, d)

    G = {"loss": loss[0, 0]}

    def ffn_back(name, which, l, h, dout):
        g, wgu, wd = ffn(which, l)
        dh, hn, dab, act, dg = _ffn_bwd(name, h, dout, g, wgu, wd)
        dwgu = _mm_tn(name + "_wgu", hn, dab)
        dwd = _mm_tn(name + "_wd", act, dout, scale=0.5)
        return dh, dg, dwgu, dwd

    dh5, dg_f2_1, dwgu_f2_1, dwd_f2_1 = ffn_back("ffn2_1_bwd", "ffn2", 1, h5, dh6)
    do, dw_o = _lin_bwd("attn_out_bwd", o, w_o, dh5)
    dq, dk, dv, dsinks = _attn_bwd(r3(q), r3(k), r3(kv), sinks, r3(o), r3(do))
    dq_raw, dqg = _headrope_bwd("q_rope_bwd", q_raw, d, dq.reshape(rows, d), qgain, cos_t, sin_t, gmat, lp)
    dh4, dg_mix1, dw_q = _proj_bwd("q_bwd", h4, mix1, w_q, dq_raw, dh5)
    dh3, dg_f1_1, dwgu_f1_1, dwd_f1_1 = ffn_back("ffn1_1_bwd", "ffn1", 1, h3, dh4)
    dk_raw, dkg = _headrope_bwd("k_rope_bwd", kv, KVW, dk.reshape(rows, KVW), kgain, cos_t, sin_t, gmat, lp)
    dkv = _concat_cols("dkv_concat", dk_raw, dv.reshape(rows, KVW))
    dh3, dg_kvn, dw_kv = _proj_bwd("kv_bwd", h3, kvn, w_kv, dkv, dh3)
    dh2, dg_f2_0, dwgu_f2_0, dwd_f2_0 = ffn_back("ffn2_0_bwd", "ffn2", 0, h2, dh3)
    dy, dw_out = _glu_bwd(y, dh2, w_out)
    ctfull = jnp.concatenate([_blockdiag(P["ssm_c_re"][0]), -_blockdiag(P["ssm_c_im"][0])], axis=1)
    btfull = jnp.concatenate([_blockdiag(bbr.transpose(0, 2, 1)), _blockdiag(bbi.transpose(0, 2, 1))], axis=0)
    du, gx, da, dd = _s5_scan_bwd(dy, u, xs, bf(ctfull), bf(btfull), a2, dvec, bsz)
    dbfull = _mm_tn("ssm_db", u, gx)
    dcfull = _mm_tn("ssm_dc", xs, dy)
    dh1, dg_mix0, dw_in = _proj_bwd("ssm_in_bwd", h1, mix0, w_in, du, dh2)
    dh0, dg_f1_0, dwgu_f1_0, dwd_f1_0 = ffn_back("ffn1_0_bwd", "ffn1", 0, h0, dh1)

    dbbr = _diagblocks(dbfull[:, :ns], g_n)
    dbbi = _diagblocks(dbfull[:, ns:], g_n)
    dlr, dli, dls, dbrt, dbit = _s5_params_bwd(lr, li, ls, brt, bit, da[:, :ns].reshape(g_n, 1, p_n),
                                               da[:, ns:].reshape(g_n, 1, p_n), dbbr, dbbi)
    dh0 = r3(dh0)
    G["x"] = dh0[:, PAD:, :]
    G["meta_tokens"] = _meta_sum(dh0)
    G["ffn1_norm"] = jnp.concatenate([dg_f1_0, dg_f1_1], axis=0)
    G["ffn2_norm"] = jnp.concatenate([dg_f2_0, dg_f2_1], axis=0)
    G["mix_norm"] = jnp.concatenate([dg_mix0, dg_mix1], axis=0)
    G["ffn1_w_gate_up"] = jnp.stack([dwgu_f1_0, dwgu_f1_1])
    G["ffn1_w_down"] = jnp.stack([dwd_f1_0, dwd_f1_1])
    G["ffn2_w_gate_up"] = jnp.stack([dwgu_f2_0, dwgu_f2_1])
    G["ffn2_w_down"] = jnp.stack([dwd_f2_0, dwd_f2_1])
    G["ssm_w_in"] = dw_in[None]
    G["ssm_lambda_re"] = dlr.reshape(1, g_n, p_n)
    G["ssm_lambda_im"] = dli.reshape(1, g_n, p_n)
    G["ssm_log_step"] = dls.reshape(1, g_n)
    G["ssm_b_re"] = dbrt.transpose(0, 2, 1)[None]
    G["ssm_b_im"] = dbit.transpose(0, 2, 1)[None]
    G["ssm_c_re"] = _diagblocks(dcfull[:ns], g_n).transpose(0, 2, 1)[None]
    G["ssm_c_im"] = -_diagblocks(dcfull[ns:], g_n).transpose(0, 2, 1)[None]
    G["ssm_d"] = dd
    G["ssm_w_out"] = dw_out[None]
    G["kv_norm"] = dg_kvn.reshape(-1)
    G["w_kv"] = dw_kv
    G["k_norm"] = dkg[0, :HEAD_DIM]
    G["attn_w_q"] = dw_q[None]
    G["q_norm"] = dqg[:, :HEAD_DIM]
    G["attn_sinks"] = dsinks[:, :N_KV_HEADS * Q_PER_KV]
    G["attn_w_o"] = dw_o[None]
    return G
```

```python
import functools
import math

import jax
import jax.numpy as jnp
from jax import lax
from jax.experimental import pallas as pl
from jax.experimental.pallas import tpu as pltpu

F32 = jnp.float32
BF16 = jnp.bfloat16

N_META = 16
PAD = 128
META0 = PAD - N_META
HEAD_DIM = 64
N_KV_HEADS = 4
Q_PER_KV = 4
SSM_GROUP = 16
SSM_STATE = 64
EPS = 1e-6
NEG_INF = -1e30
ROPE_THETA = 10000.0
ADAM_LR, ADAM_B1, ADAM_B2, ADAM_EPS, ADAM_WD, ADAM_STEP = 0.001, 0.9, 0.999, 1e-08, 0.01, 10
LANES = 128
PACK_W = 1024
VMEM_LIMIT = 56 * 1024 * 1024
MESH_AXES = ("x", "y", "c")
N_DEV = 8


def _cparams(sem=None):
    return pltpu.CompilerParams(dimension_semantics=sem, vmem_limit_bytes=VMEM_LIMIT)


def _row_tile(rows):
    for tm in (384, 256, 128, 64, 32, 16, 8):
        if rows % tm == 0:
            return tm
    raise ValueError(rows)


def _dot(a, b):
    return jnp.dot(a.astype(BF16), b.astype(BF16), preferred_element_type=F32)


def _dot_nt(a, b):
    return lax.dot_general(a.astype(BF16), b.astype(BF16), (((1,), (1,)), ((), ())), preferred_element_type=F32)


def _dot_tn(a, b):
    return lax.dot_general(a.astype(BF16), b.astype(BF16), (((0,), (0,)), ((), ())), preferred_element_type=F32)


def _rms(x, g):
    rstd = lax.rsqrt(jnp.mean(x * x, axis=-1, keepdims=True) + EPS)
    y = x * rstd
    return y * g, y, rstd


def _rms_bwd(dhn, y, rstd, g):
    dyn = dhn * g
    dx = rstd * (dyn - y * jnp.mean(dyn * y, axis=-1, keepdims=True))
    return dx, jnp.sum(dhn * y, axis=0, keepdims=True)


def _sigmoid(x):
    return 1.0 / (1.0 + jnp.exp(-x))


_GELU_C = math.sqrt(2.0 / math.pi)


def _gelu(y):
    t = jnp.tanh(_GELU_C * (y + 0.044715 * y * y * y))
    return 0.5 * y * (1.0 + t), t


def _gelu_grad(y, t):
    return 0.5 * (1.0 + t) + 0.5 * y * (1.0 - t * t) * _GELU_C * (1.0 + 3.0 * 0.044715 * y * y)


def _rowcall(name, body, rows, row_ins, const_ins, row_outs, acc_outs=(), tm=None, row_in_maps=None):
    tm = tm or _row_tile(rows)
    steps = rows // tm
    in_specs = []
    for k, a in enumerate(row_ins):
        if row_in_maps is not None and row_in_maps[k] is not None:
            in_specs.append(pl.BlockSpec(*row_in_maps[k]))
        else:
            in_specs.append(pl.BlockSpec((tm, a.shape[1]), lambda i: (i, 0)))
    for a in const_ins:
        in_specs.append(pl.BlockSpec(a.shape, lambda i, nd=a.ndim: (0,) * nd, pipeline_mode=pl.Buffered(1)))
    out_shape, out_specs = [], []
    for w, dt in row_outs:
        out_shape.append(jax.ShapeDtypeStruct((rows, w), dt))
        out_specs.append(pl.BlockSpec((tm, w), lambda i: (i, 0)))
    for shp, dt in acc_outs:
        out_shape.append(jax.ShapeDtypeStruct(shp, dt))
        out_specs.append(pl.BlockSpec(shp, lambda i, nd=len(shp): (0,) * nd))

    def kern(*refs):
        body(pl.program_id(0), *refs)

    return pl.pallas_call(
        kern, name=name, grid=(steps,), in_specs=in_specs, out_specs=out_specs, out_shape=out_shape,
        compiler_params=_cparams(("arbitrary",)),
    )(*row_ins, *const_ins)


def _acc(step, ref, val):
    @pl.when(step == 0)
    def _():
        ref[...] = val

    @pl.when(step != 0)
    def _():
        ref[...] += val


def _embed(x, meta):
    bsz, seq, d = x.shape
    nb = seq // PAD + 1

    def kern(x_ref, m_ref, o_ref):
        i = pl.program_id(1)

        @pl.when(i == 0)
        def _():
            o_ref[0, 0:META0, :] = jnp.zeros((META0, d), F32)
            o_ref[0, META0:PAD, :] = m_ref[...]

        @pl.when(i != 0)
        def _():
            o_ref[0] = x_ref[0]

    return pl.pallas_call(
        kern, name="embed", grid=(bsz, nb),
        in_specs=[pl.BlockSpec((1, PAD, d), lambda b, i: (b, jnp.maximum(i - 1, 0), 0)),
                  pl.BlockSpec((N_META, d), lambda b, i: (0, 0))],
        out_specs=pl.BlockSpec((1, PAD, d), lambda b, i: (b, i, 0)),
        out_shape=jax.ShapeDtypeStruct((bsz, seq + PAD, d), F32),
        compiler_params=_cparams(("arbitrary", "arbitrary")),
    )(x, meta)


def _loss(h6, target):
    bsz, lp, d = h6.shape
    nb = lp // PAD

    def kern(h_ref, t_ref, l_ref, d_ref):
        b, i = pl.program_id(0), pl.program_id(1)

        @pl.when((b == 0) & (i == 0))
        def _():
            l_ref[...] = jnp.zeros_like(l_ref)

        @pl.when(i == 0)
        def _():
            d_ref[0] = jnp.zeros((PAD, d), F32)

        @pl.when(i != 0)
        def _():
            e = h_ref[0] - t_ref[0]
            d_ref[0] = e * (1.0 / d)
            l_ref[...] += 0.5 * jnp.sum(jnp.mean(e * e, axis=-1, keepdims=True))

    return pl.pallas_call(
        kern, name="loss", grid=(bsz, nb),
        in_specs=[pl.BlockSpec((1, PAD, d), lambda b, i: (b, i, 0)),
                  pl.BlockSpec((1, PAD, d), lambda b, i: (b, jnp.maximum(i - 1, 0), 0))],
        out_specs=[pl.BlockSpec((1, LANES), lambda b, i: (0, 0)),
                   pl.BlockSpec((1, PAD, d), lambda b, i: (b, i, 0))],
        out_shape=[jax.ShapeDtypeStruct((1, LANES), F32), jax.ShapeDtypeStruct((bsz, lp, d), F32)],
        compiler_params=_cparams(("arbitrary", "arbitrary")),
    )(h6, target)


def _meta_sum(dh0):
    bsz, lp, d = dh0.shape

    def kern(d_ref, o_ref):
        _acc(pl.program_id(0), o_ref, d_ref[0, META0:PAD, :])

    return pl.pallas_call(
        kern, name="meta_sum", grid=(bsz,),
        in_specs=[pl.BlockSpec((1, PAD, d), lambda b: (b, 0, 0))],
        out_specs=pl.BlockSpec((N_META, d), lambda b: (0, 0)),
        out_shape=jax.ShapeDtypeStruct((N_META, d), F32),
        compiler_params=_cparams(("arbitrary",)),
    )(dh0)


def _ffn_chunks(f):
    for n in (2, 4, 1, 11, 22):
        if f % n == 0 and (f // n) % LANES == 0:
            return n
    raise ValueError(f)


def _ffn_fwd(name, h, g, wgu, wd):
    rows, d = h.shape
    f = wd.shape[0]
    nf = _ffn_chunks(f)
    tf = f // nf

    def body(step, h_ref, g_ref, wgu_ref, wd_ref, o_ref):
        hx = h_ref[...]
        hb = _rms(hx, g_ref[...])[0].astype(BF16)
        acc = jnp.zeros(hx.shape, F32)
        for j in range(nf):
            a = _dot(hb, wgu_ref[:, j * tf:(j + 1) * tf])
            b = _dot(hb, wgu_ref[:, f + j * tf:f + (j + 1) * tf])
            acc = acc + _dot(a * _sigmoid(a) * b, wd_ref[j * tf:(j + 1) * tf, :])
        o_ref[...] = hx + 0.5 * acc

    return _rowcall(name, body, rows, [h], [g, wgu, wd], [(d, F32)])[0]


def _ffn_bwd(name, h, dout, g, wgu, wd):
    rows, d = h.shape
    f = wd.shape[0]
    nf = _ffn_chunks(f)
    tf = f // nf

    def body(step, h_ref, do_ref, g_ref, wgu_ref, wd_ref, dh_ref, hn_ref, dab_ref, act_ref, dg_ref):
        hx, dout_x, gx = h_ref[...], do_ref[...], g_ref[...]
        hn, y, rstd = _rms(hx, gx)
        hb = hn.astype(BF16)
        hn_ref[...] = hb
        dhalf = (0.5 * dout_x).astype(BF16)
        dhn = jnp.zeros(hx.shape, F32)
        for j in range(nf):
            ga, ua = slice(j * tf, (j + 1) * tf), slice(f + j * tf, f + (j + 1) * tf)
            a = _dot(hb, wgu_ref[:, ga])
            b = _dot(hb, wgu_ref[:, ua])
            s = _sigmoid(a)
            silu = a * s
            act_ref[:, ga] = (silu * b).astype(BF16)
            dact = _dot_nt(dhalf, wd_ref[ga, :])
            da = (dact * b * (s + silu * (1.0 - s))).astype(BF16)
            db = (dact * silu).astype(BF16)
            dab_ref[:, ga] = da
            dab_ref[:, ua] = db
            dhn = dhn + _dot_nt(da, wgu_ref[:, ga]) + _dot_nt(db, wgu_ref[:, ua])
        dx, dg = _rms_bwd(dhn, y, rstd, gx)
        dh_ref[...] = dout_x + dx
        _acc(step, dg_ref, dg)

    return _rowcall(name, body, rows, [h, dout], [g, wgu, wd],
                    [(d, F32), (d, BF16), (2 * f, BF16), (f, BF16)], [((1, d), F32)])


def _mm_tn(name, a, b, scale=1.0):
    rows, k1 = a.shape
    k2 = b.shape[1]
    tn = k2
    for cand in (512, 704, 1408, 1024):
        if k2 % cand == 0 and k1 * cand * 4 <= 6 * 1024 * 1024:
            tn = cand
    tm = _row_tile(rows)
    steps = rows // tm

    def kern(a_ref, b_ref, o_ref):
        bx = b_ref[...]
        if scale != 1.0:
            bx = bx * scale
        _acc(pl.program_id(1), o_ref, _dot_tn(a_ref[...], bx))

    return pl.pallas_call(
        kern, name=name, grid=(k2 // tn, steps),
        in_specs=[pl.BlockSpec((tm, k1), lambda j, i: (i, 0)), pl.BlockSpec((tm, tn), lambda j, i: (i, j))],
        out_specs=pl.BlockSpec((k1, tn), lambda j, i: (0, j)),
        out_shape=jax.ShapeDtypeStruct((k1, k2), F32),
        compiler_params=_cparams(("arbitrary", "arbitrary")),
    )(a, b)


def _mm_tn_slots(name, a, b, scale, by_cols, layer, nlayers, into):
    rows, k1 = a.shape
    k2 = b.shape[1]
    tm = _row_tile(rows)
    steps = rows // tm
    if by_cols:
        tn = k2 // N_DEV
        shape = (N_DEV, nlayers, k1, tn)
        out_spec = pl.BlockSpec((None, None, k1, tn), lambda j, i: (j, layer, 0, 0))
    else:
        tn = 512 if k2 % 512 == 0 else k2
        sr = k1 // N_DEV
        shape = (N_DEV, nlayers, sr, k2)
        out_spec = pl.BlockSpec((N_DEV, None, sr, tn), lambda j, i: (0, layer, 0, j))

    def kern(*refs):
        a_ref, b_ref, o_ref = refs[0], refs[1], refs[-1]
        bx = b_ref[...]
        if scale != 1.0:
            bx = bx * scale
        res = _dot_tn(a_ref[...], bx)
        step = pl.program_id(1)
        if by_cols:
            _acc(step, o_ref, res)
        else:
            for s in range(N_DEV):
                _acc(step, o_ref.at[s], res[s * sr:(s + 1) * sr])

    in_specs = [pl.BlockSpec((tm, k1), lambda j, i: (i, 0)), pl.BlockSpec((tm, tn), lambda j, i: (i, j))]
    ins = [a, b]
    alias = {}
    if into is not None:
        in_specs.append(ANY)
        ins.append(into)
        alias = {2: 0}
    return pl.pallas_call(
        kern, name=name, grid=(k2 // tn, steps), in_specs=in_specs, out_specs=out_spec,
        out_shape=jax.ShapeDtypeStruct(shape, F32), input_output_aliases=alias,
        compiler_params=_cparams(("arbitrary", "arbitrary")),
    )(*ins)


def _proj_fwd(name, h, g, w):
    rows = h.shape[0]

    def body(step, h_ref, g_ref, w_ref, o_ref):
        o_ref[...] = _dot(_rms(h_ref[...], g_ref[...])[0], w_ref[...])

    return _rowcall(name, body, rows, [h], [g, w], [(w.shape[1], F32)])[0]


def _proj_bwd(name, h, g, w, dy, dres):
    rows, d = h.shape

    def body(step, h_ref, dy_ref, dr_ref, g_ref, w_ref, dh_ref, dg_ref, dw_ref):
        gx = g_ref[...]
        hn, y, rstd = _rms(h_ref[...], gx)
        dyx = dy_ref[...]
        dx, dg = _rms_bwd(_dot_nt(dyx, w_ref[...]), y, rstd, gx)
        dh_ref[...] = dr_ref[...] + dx
        _acc(step, dg_ref, dg)
        _acc(step, dw_ref, _dot_tn(hn, dyx))

    return _rowcall(name, body, rows, [h, dy, dres], [g, w], [(d, F32)], [((1, d), F32), (w.shape, F32)])


def _lin_res_fwd(name, a, w, res):
    rows = a.shape[0]

    def body(step, a_ref, r_ref, w_ref, o_ref):
        o_ref[...] = r_ref[...] + _dot(a_ref[...], w_ref[...])

    return _rowcall(name, body, rows, [a, res], [w], [(w.shape[1], F32)])[0]


def _lin_bwd(name, a, w, dy):
    rows, k = a.shape

    def body(step, a_ref, dy_ref, w_ref, da_ref, dw_ref):
        dyx = dy_ref[...]
        da_ref[...] = _dot_nt(dyx, w_ref[...])
        _acc(step, dw_ref, _dot_tn(a_ref[...], dyx))

    return _rowcall(name, body, rows, [a, dy], [w], [(k, F32)], [(w.shape, F32)])


def _s5_param_fn(lr, li, ls, brt, bit):
    step = jnp.exp(ls)
    mag = jnp.exp(lr * step)
    ar = mag * jnp.cos(li * step)
    ai = mag * jnp.sin(li * step)
    den = lr * lr + li * li
    nr, ni = ar - 1.0, ai
    cr = (nr * lr + ni * li) / den
    ci = (ni * lr - nr * li) / den
    return ar, ai, cr * brt - ci * bit, cr * bit + ci * brt


def _s5_params_fwd(lr, li, ls, brt, bit):
    def kern(lr_ref, li_ref, ls_ref, br_ref, bi_ref, ar_ref, ai_ref, bbr_ref, bbi_ref):
        ar, ai, bbr, bbi = _s5_param_fn(lr_ref[...], li_ref[...], ls_ref[...], br_ref[...], bi_ref[...])
        ar_ref[...], ai_ref[...], bbr_ref[...], bbi_ref[...] = ar, ai, bbr, bbi

    sd = jax.ShapeDtypeStruct
    return pl.pallas_call(
        kern, name="s5_params_fwd",
        out_shape=[sd(lr.shape, F32), sd(lr.shape, F32), sd(brt.shape, F32), sd(brt.shape, F32)],
    )(lr, li, ls, brt, bit)


def _s5_params_bwd(lr, li, ls, brt, bit, dar, dai, dbbr, dbbi):
    def kern(lr_ref, li_ref, ls_ref, br_ref, bi_ref, dar_ref, dai_ref, dbbr_ref, dbbi_ref,
             dlr_ref, dli_ref, dls_ref, dbr_ref, dbi_ref):
        _, vjp = jax.vjp(_s5_param_fn, lr_ref[...], li_ref[...], ls_ref[...], br_ref[...], bi_ref[...])
        dlr, dli, dls, dbr, dbi = vjp((dar_ref[...], dai_ref[...], dbbr_ref[...], dbbi_ref[...]))
        dlr_ref[...], dli_ref[...], dls_ref[...], dbr_ref[...], dbi_ref[...] = dlr, dli, dls, dbr, dbi

    sd = jax.ShapeDtypeStruct
    return pl.pallas_call(
        kern, name="s5_params_bwd",
        out_shape=[sd(lr.shape, F32), sd(lr.shape, F32), sd(ls.shape, F32), sd(brt.shape, F32), sd(brt.shape, F32)],
    )(lr, li, ls, brt, bit, dar, dai, dbbr, dbbi)


SCAN_LW = 512


def _scan_tables(a_ref, tab_ref, conj):
    ns = a_ref.shape[1]
    ar = jnp.broadcast_to(a_ref[0:1, :], (8, ns))
    ai = jnp.broadcast_to(a_ref[1:2, :], (8, ns))
    if conj:
        ai = -ai
    p1r, p1i = ar, ai
    p2r, p2i = p1r * p1r - p1i * p1i, 2.0 * p1r * p1i
    p4r, p4i = p2r * p2r - p2i * p2i, 2.0 * p2r * p2i
    row = lax.broadcasted_iota(jnp.int32, (8, ns), 0)
    e = row if not conj else 7 - row
    one, zero = jnp.ones((8, ns), F32), jnp.zeros((8, ns), F32)
    qr, qi = p1r, p1i
    for bit, (pr, pi) in ((1, (p1r, p1i)), (2, (p2r, p2i)), (4, (p4r, p4i))):
        sel = (e & bit) != 0
        fr, fi = jnp.where(sel, pr, one), jnp.where(sel, pi, zero)
        qr, qi = qr * fr - qi * fi, qr * fi + qi * fr
    for k, v in enumerate((p1r, p1i, p2r, p2i, p4r, p4i, qr, qi)):
        tab_ref[k] = v


def _scan_block(x_ref, tab_ref, carry_ref, t_rows, ns, reverse):
    ngrp = t_rows // 8
    row = lax.broadcasted_iota(jnp.int32, (8, SCAN_LW), 0)
    for lc in range(ns // SCAN_LW):
        lre = pl.ds(lc * SCAN_LW, SCAN_LW)
        lim = pl.ds(ns + lc * SCAN_LW, SCAN_LW)

        def group(k, carry, lre=lre, lim=lim):
            cr, ci = carry
            gi = (ngrp - 1 - k) if reverse else k
            rows = pl.ds(pl.multiple_of(gi * 8, 8), 8)
            vr, vi = x_ref[rows, lre], x_ref[rows, lim]
            for lvl, dsh in enumerate((1, 2, 4)):
                pr, pi = tab_ref[2 * lvl, :, lre], tab_ref[2 * lvl + 1, :, lre]
                if reverse:
                    keep = row < 8 - dsh
                    sr, si = pltpu.roll(vr, 8 - dsh, 0), pltpu.roll(vi, 8 - dsh, 0)
                else:
                    keep = row >= dsh
                    sr, si = pltpu.roll(vr, dsh, 0), pltpu.roll(vi, dsh, 0)
                sr, si = jnp.where(keep, sr, 0.0), jnp.where(keep, si, 0.0)
                vr, vi = vr + pr * sr - pi * si, vi + pr * si + pi * sr
            qr, qi = tab_ref[6, :, lre], tab_ref[7, :, lre]
            vr, vi = vr + qr * cr - qi * ci, vi + qr * ci + qi * cr
            x_ref[rows, lre], x_ref[rows, lim] = vr, vi
            edge = 0 if reverse else 7
            return (jnp.broadcast_to(vr[edge:edge + 1, :], (8, SCAN_LW)),
                    jnp.broadcast_to(vi[edge:edge + 1, :], (8, SCAN_LW)))

        cr, ci = lax.fori_loop(0, ngrp, group, (carry_ref[:, lre], carry_ref[:, lim]))
        carry_ref[:, lre], carry_ref[:, lim] = cr, ci


def _scan_rows(lp):
    for t in (384, 256, 128):
        if lp % t == 0:
            return t
    raise ValueError(lp)


def _s5_scan_fwd(u, bfull, cfull, a2, dvec, bsz):
    rows, hw = u.shape
    ns = a2.shape[1]
    lp = rows // bsz
    t_rows = _scan_rows(lp)
    nc = lp // t_rows

    def kern(u_ref, b_ref, c_ref, a_ref, d_ref, y_ref, x_ref, tab_ref, carry_ref):
        c = pl.program_id(1)

        @pl.when((pl.program_id(0) == 0) & (c == 0))
        def _():
            _scan_tables(a_ref, tab_ref, conj=False)

        @pl.when(c == 0)
        def _():
            carry_ref[...] = jnp.zeros_like(carry_ref)

        ux = u_ref[...]
        x_ref[...] = _dot(ux, b_ref[...])
        _scan_block(x_ref, tab_ref, carry_ref, t_rows, ns, reverse=False)
        y_ref[...] = _dot(x_ref[...], c_ref[...]) + d_ref[...] * ux

    const = lambda shp: pl.BlockSpec(shp, lambda b, c: (0,) * len(shp), pipeline_mode=pl.Buffered(1))
    return pl.pallas_call(
        kern, name="s5_scan_fwd", grid=(bsz, nc),
        in_specs=[pl.BlockSpec((t_rows, hw), lambda b, c: (b * nc + c, 0)),
                  const(bfull.shape), const(cfull.shape), const(a2.shape), const(dvec.shape)],
        out_specs=[pl.BlockSpec((t_rows, hw), lambda b, c: (b * nc + c, 0)),
                   pl.BlockSpec((t_rows, 2 * ns), lambda b, c: (b * nc + c, 0))],
        out_shape=[jax.ShapeDtypeStruct((rows, hw), F32), jax.ShapeDtypeStruct((rows, 2 * ns), F32)],
        scratch_shapes=[pltpu.VMEM((8, 8, ns), F32), pltpu.VMEM((8, 2 * ns), F32)],
        compiler_params=_cparams(("arbitrary", "arbitrary")),
    )(u, bfull, cfull, a2, dvec)


def _s5_scan_bwd(dy, u, xs, ctfull, btfull, a2, dvec, bsz):
    rows, hw = u.shape
    ns = a2.shape[1]
    lp = rows // bsz
    t_rows = _scan_rows(lp)
    nc = lp // t_rows
    blk = lambda b, c: (b * nc + (nc - 1 - c), 0)

    def prev8(b, c):
        first = (b * nc + (nc - 1 - c)) * (t_rows // 8)
        return (jnp.maximum(first - 1, 0), 0)

    def kern(dy_ref, u_ref, x_ref, xp_ref, ct_ref, bt_ref, a_ref, d_ref, du_ref, gx_ref, da_ref, dd_ref,
             tab_ref, carry_ref):
        b, c = pl.program_id(0), pl.program_id(1)
        first = (b == 0) & (c == 0)

        @pl.when(first)
        def _():
            _scan_tables(a_ref, tab_ref, conj=True)

        @pl.when(c == 0)
        def _():
            carry_ref[...] = jnp.zeros_like(carry_ref)

        dyx, ux = dy_ref[...], u_ref[...]
        gx_ref[...] = _dot(dyx, ct_ref[...])
        _scan_block(gx_ref, tab_ref, carry_ref, t_rows, ns, reverse=True)
        gx = gx_ref[...]
        du_ref[...] = _dot(gx, bt_ref[...]) + d_ref[...] * dyx
        xprev = pltpu.roll(x_ref[...], 1, 0)
        seq_start = c == nc - 1
        head = jnp.where(seq_start, 0.0, xp_ref[7:8, :])
        rid = lax.broadcasted_iota(jnp.int32, (t_rows, 1), 0)
        xprev = jnp.where(rid == 0, head, xprev)
        xr, xi, gr, gi = xprev[:, :ns], xprev[:, ns:], gx[:, :ns], gx[:, ns:]
        da = jnp.concatenate([jnp.sum(xr * gr + xi * gi, axis=0, keepdims=True),
                              jnp.sum(xr * gi - xi * gr, axis=0, keepdims=True)], axis=1)
        dd = jnp.sum(dyx * ux, axis=0, keepdims=True)

        @pl.when(first)
        def _():
            da_ref[...] = da
            dd_ref[...] = dd

        @pl.when(jnp.logical_not(first))
        def _():
            da_ref[...] += da
            dd_ref[...] += dd

    const = lambda shp: pl.BlockSpec(shp, lambda b, c: (0,) * len(shp), pipeline_mode=pl.Buffered(1))
    return pl.pallas_call(
        kern, name="s5_scan_bwd", grid=(bsz, nc),
        in_specs=[pl.BlockSpec((t_rows, hw), blk), pl.BlockSpec((t_rows, hw), blk),
                  pl.BlockSpec((t_rows, 2 * ns), blk), pl.BlockSpec((8, 2 * ns), prev8),
                  const(ctfull.shape), const(btfull.shape), const(a2.shape), const(dvec.shape)],
        out_specs=[pl.BlockSpec((t_rows, hw), blk), pl.BlockSpec((t_rows, 2 * ns), blk),
                   pl.BlockSpec((1, 2 * ns), lambda b, c: (0, 0)), pl.BlockSpec((1, hw), lambda b, c: (0, 0))],
        out_shape=[jax.ShapeDtypeStruct((rows, hw), F32), jax.ShapeDtypeStruct((rows, 2 * ns), F32),
                   jax.ShapeDtypeStruct((1, 2 * ns), F32), jax.ShapeDtypeStruct((1, hw), F32)],
        scratch_shapes=[pltpu.VMEM((8, 8, ns), F32), pltpu.VMEM((8, 2 * ns), F32)],
        compiler_params=_cparams(("arbitrary", "arbitrary")),
    )(dy, u, xs, xs, ctfull, btfull, a2, dvec)


def _glu_fwd(y, h1, wout):
    rows, d = h1.shape

    def body(step, y_ref, h_ref, w_ref, o_ref):
        z = _dot(_gelu(y_ref[...])[0], w_ref[...])
        o_ref[...] = h_ref[...] + z[:, :d] * _sigmoid(z[:, d:])

    return _rowcall("glu_fwd", body, rows, [y, h1], [wout], [(d, F32)])[0]


def _glu_bwd(y, dh2, wout):
    rows, d = dh2.shape
    hw = y.shape[1]

    def body(step, y_ref, dh_ref, w_ref, dy_ref, dw_ref):
        yx, dh = y_ref[...], dh_ref[...]
        gl, t = _gelu(yx)
        z = _dot(gl, w_ref[...])
        za, sg = z[:, :d], _sigmoid(z[:, d:])
        dza = dh * sg
        dzg = dh * za * sg * (1.0 - sg)
        dgl = _dot_nt(dza, w_ref[:, :d]) + _dot_nt(dzg, w_ref[:, d:])
        dy_ref[...] = dgl * _gelu_grad(yx, t)
        for half, dz in enumerate((dza, dzg)):
            dw = _dot_tn(gl, dz)
            for s in range(N_DEV // 2):
                _acc(step, dw_ref.at[half * (N_DEV // 2) + s], dw[:, s * cw:(s + 1) * cw])

    cw = 2 * d // N_DEV
    return _rowcall("glu_bwd", body, rows, [y, dh2], [wout], [(hw, F32)], [((N_DEV, hw, cw), F32)])


def _gmean64(x2, gmat):
    hi = x2.astype(BF16)
    r1 = x2 - hi.astype(F32)
    mid = r1.astype(BF16)
    lo = (r1 - mid.astype(F32)).astype(BF16)
    outs = []
    for j in range(x2.shape[1] // LANES):
        sl = slice(j * LANES, (j + 1) * LANES)
        f = lambda p: jnp.dot(p[:, sl], gmat, preferred_element_type=F32)
        outs.append(f(hi) + f(mid) + f(lo))
    return outs[0] if len(outs) == 1 else jnp.concatenate(outs, axis=1)


def _swap32(x):
    w = x.shape[1]
    lane = lax.broadcasted_iota(jnp.int32, (1, w), 1)
    return jnp.where((lane & 32) == 0, pltpu.roll(x, w - 32, 1), pltpu.roll(x, 32, 1))


def _tile_lanes(t, w):
    reps = w // t.shape[1]
    return t if reps == 1 else jnp.concatenate([t] * reps, axis=1)


def _headrope_fwd(name, raw, w, gain, cos, sin, gmat, lp):
    rows = raw.shape[0]
    tm = _row_tile(lp)
    per = lp // tm

    def body(step, x_ref, c_ref, s_ref, g_ref, gm_ref, o_ref):
        x = x_ref[...]
        rstd = lax.rsqrt(_gmean64(x * x, gm_ref[...]) + EPS)
        z = x * rstd * g_ref[...]
        o_ref[...] = z * _tile_lanes(c_ref[...], w) + _swap32(z) * _tile_lanes(s_ref[...], w)

    maps = [((tm, w), lambda i: (i, 0)), ((tm, LANES), lambda i: (i % per, 0)), ((tm, LANES), lambda i: (i % per, 0))]
    return _rowcall(name, body, rows, [raw, cos, sin], [gain, gmat], [(w, F32)], tm=tm, row_in_maps=maps)[0]


def _headrope_bwd(name, raw, w, dout, gain, cos, sin, gmat, lp):
    rows = raw.shape[0]
    tm = _row_tile(lp)
    per = lp // tm

    def body(step, x_ref, do_ref, c_ref, s_ref, g_ref, gm_ref, dx_ref, dg_ref):
        x, dout_x, gx, gm = x_ref[...], do_ref[...], g_ref[...], gm_ref[...]
        rstd = lax.rsqrt(_gmean64(x * x, gm) + EPS)
        yn = x * rstd
        dz = dout_x * _tile_lanes(c_ref[...], w) + _swap32(dout_x * _tile_lanes(s_ref[...], w))
        dyn = dz * gx
        dx_ref[...] = rstd * (dyn - yn * _gmean64(dyn * yn, gm))
        dg = jnp.sum(dz * yn, axis=0, keepdims=True)
        sh = w // 2
        while sh >= HEAD_DIM:
            dg = dg + pltpu.roll(dg, sh, 1)
            sh //= 2
        _acc(step, dg_ref, dg)

    maps = [((tm, w), lambda i: (i, 0)), None, ((tm, LANES), lambda i: (i % per, 0)), ((tm, LANES), lambda i: (i % per, 0))]
    return _rowcall(name, body, rows, [raw, dout, cos, sin], [gain, gmat], [(w, F32)], [((1, w), F32)],
                    tm=tm, row_in_maps=maps)


KVW = N_KV_HEADS * HEAD_DIM
QB = 128


def _fold4(x):
    y = x + pltpu.roll(x, 128, 1)
    return y + pltpu.roll(y, 64, 1)


def _attn_scores(i, q_ref, k0_ref, kp_ref, kc_ref, sink_ref, h):
    lane = lax.broadcasted_iota(jnp.int32, (1, KVW), 1) // HEAD_DIM
    qh = q_ref[:, h * KVW:(h + 1) * KVW]
    qs = jnp.concatenate([jnp.where(lane == g, qh, 0.0) for g in range(Q_PER_KV)], axis=0).astype(BF16)
    hsel = lane == h
    kx = [_fold4(jnp.where(hsel, r[...], 0.0)).astype(BF16) for r in (k0_ref, kp_ref, kc_ref)]
    scale = HEAD_DIM ** -0.5
    s0, sp, sc = [_dot_nt(qs, k) * scale for k in kx]
    qi = lax.broadcasted_iota(jnp.int32, (Q_PER_KV * QB, QB), 0) % QB
    kj = lax.broadcasted_iota(jnp.int32, (Q_PER_KV * QB, QB), 1)
    s0 = jnp.where(kj >= META0, s0, NEG_INF)
    sp = jnp.where((kj > qi) & (i >= 2), sp, NEG_INF)
    sc = jnp.where(kj <= qi, sc, NEG_INF)
    rowg = lax.broadcasted_iota(jnp.int32, (Q_PER_KV * QB, 1), 0) // QB
    sink = jnp.zeros((Q_PER_KV * QB, 1), F32)
    for g in range(Q_PER_KV):
        sink = jnp.where(rowg == g, sink_ref[0, h * Q_PER_KV + g], sink)
    m = jnp.maximum(jnp.maximum(jnp.max(s0, axis=1, keepdims=True), jnp.max(sp, axis=1, keepdims=True)),
                    jnp.maximum(jnp.max(sc, axis=1, keepdims=True), sink))
    p0, pp, pc, ps = jnp.exp(s0 - m), jnp.exp(sp - m), jnp.exp(sc - m), jnp.exp(sink - m)
    den = jnp.sum(p0, axis=1, keepdims=True) + jnp.sum(pp, axis=1, keepdims=True) + jnp.sum(pc, axis=1, keepdims=True) + ps
    return qs, kx, (p0, pp, pc), ps, den, lane, hsel


def _unstack(x, lane):
    out = jnp.where(lane == 0, x[0:QB], 0.0)
    for g in range(1, Q_PER_KV):
        out = out + jnp.where(lane == g, x[g * QB:(g + 1) * QB], 0.0)
    return out


def _attn_specs(nb, d):
    qspec = pl.BlockSpec((None, QB, d), lambda b, i: (b, i, 0))
    k0 = pl.BlockSpec((None, QB, KVW), lambda b, i: (b, 0, 0))
    kp = pl.BlockSpec((None, QB, KVW), lambda b, i: (b, jnp.maximum(i - 1, 0), 0))
    kc = pl.BlockSpec((None, QB, KVW), lambda b, i: (b, i, 0))
    v0 = pl.BlockSpec((None, QB, KVW), lambda b, i: (b, 0, 1))
    vp = pl.BlockSpec((None, QB, KVW), lambda b, i: (b, jnp.maximum(i - 1, 0), 1))
    vc = pl.BlockSpec((None, QB, KVW), lambda b, i: (b, i, 1))
    sink = pl.BlockSpec(memory_space=pltpu.SMEM)
    return qspec, [k0, kp, kc], [v0, vp, vc], sink


def _attn_fwd(q, k, kv, sinks):
    bsz, lp, d = q.shape
    nb = lp // QB
    qspec, kspecs, vspecs, sspec = _attn_specs(nb, d)

    def kern(q_ref, k0_ref, kp_ref, kc_ref, v0_ref, vp_ref, vc_ref, sink_ref, o_ref):
        i = pl.program_id(1)
        for h in range(N_KV_HEADS):
            qs, kx, ps3, psink, den, lane, hsel = _attn_scores(i, q_ref, k0_ref, kp_ref, kc_ref, sink_ref, h)
            vx = [_fold4(jnp.where(hsel, r[...], 0.0)).astype(BF16) for r in (v0_ref, vp_ref, vc_ref)]
            o = _dot(ps3[0], vx[0]) + _dot(ps3[1], vx[1]) + _dot(ps3[2], vx[2])
            o_ref[:, h * KVW:(h + 1) * KVW] = _unstack(o / den, lane)

    return pl.pallas_call(
        kern, name="attn_fwd", grid=(bsz, nb),
        in_specs=[qspec] + kspecs + vspecs + [sspec],
        out_specs=qspec, out_shape=jax.ShapeDtypeStruct((bsz, lp, d), F32),
        compiler_params=_cparams(("arbitrary", "arbitrary")),
    )(q, k, k, k, kv, kv, kv, sinks)


def _attn_bwd(q, k, kv, sinks, o, do):
    bsz, lp, d = q.shape
    nb = lp // QB
    qspec, kspecs, vspecs, sspec = _attn_specs(nb, d)
    full = pl.BlockSpec((None, lp, KVW), lambda b, i: (b, 0, 0))

    def kern(q_ref, k0_ref, kp_ref, kc_ref, v0_ref, vp_ref, vc_ref, sink_ref, o_ref, do_ref,
             dq_ref, dk_ref, dv_ref, ds_ref):
        b, i = pl.program_id(0), pl.program_id(1)

        @pl.when(i == 0)
        def _():
            dk_ref[...] = jnp.zeros_like(dk_ref)
            dv_ref[...] = jnp.zeros_like(dv_ref)

        @pl.when((b == 0) & (i == 0))
        def _():
            ds_ref[...] = jnp.zeros_like(ds_ref)

        lane128 = lax.broadcasted_iota(jnp.int32, (1, LANES), 1)
        rowg = lax.broadcasted_iota(jnp.int32, (Q_PER_KV * QB, 1), 0) // QB
        dk_acc = [jnp.zeros((QB, KVW), F32) for _ in range(3)]
        dv_acc = [jnp.zeros((QB, KVW), F32) for _ in range(3)]
        dsink = jnp.zeros((1, LANES), F32)
        for h in range(N_KV_HEADS):
            qs, kx, ps3, psink, den, lane, hsel = _attn_scores(i, q_ref, k0_ref, kp_ref, kc_ref, sink_ref, h)
            vx = [_fold4(jnp.where(hsel, r[...], 0.0)).astype(BF16) for r in (v0_ref, vp_ref, vc_ref)]
            sl = slice(h * KVW, (h + 1) * KVW)
            doh, oh = do_ref[:, sl], o_ref[:, sl]
            dos = jnp.concatenate([jnp.where(lane == g, doh, 0.0) for g in range(Q_PER_KV)], axis=0)
            ost = jnp.concatenate([jnp.where(lane == g, oh, 0.0) for g in range(Q_PER_KV)], axis=0)
            delta = jnp.sum(dos * ost, axis=1, keepdims=True)
            inv = 1.0 / den
            dosb = dos.astype(BF16)
            dqs = jnp.zeros((Q_PER_KV * QB, KVW), F32)
            for n in range(3):
                pn = ps3[n] * inv
                ds = pn * (_dot_nt(dosb, vx[n]) - delta) * (HEAD_DIM ** -0.5)
                dqs = dqs + _dot(ds, kx[n])
                dk_acc[n] = dk_acc[n] + jnp.where(hsel, _fold4(_dot_tn(ds, qs)), 0.0)
                dv_acc[n] = dv_acc[n] + jnp.where(hsel, _fold4(_dot_tn(pn, dosb)), 0.0)
            dq_ref[:, sl] = _unstack(dqs, lane)
            dsk = -(psink * inv) * delta
            for g in range(Q_PER_KV):
                val = jnp.sum(jnp.where(rowg == g, dsk, 0.0), axis=0, keepdims=True)
                dsink = dsink + jnp.where(lane128 == h * Q_PER_KV + g, val, 0.0)
        ds_ref[...] += dsink
        r0 = pl.ds(0, QB)
        rp = pl.ds(pl.multiple_of(jnp.maximum(i - 1, 0) * QB, QB), QB)
        rc = pl.ds(pl.multiple_of(i * QB, QB), QB)
        for rows, n in ((r0, 0), (rp, 1), (rc, 2)):
            dk_ref[rows, :] += dk_acc[n]
            dv_ref[rows, :] += dv_acc[n]

    return pl.pallas_call(
        kern, name="attn_bwd", grid=(bsz, nb),
        in_specs=[qspec] + kspecs + vspecs + [sspec, qspec, qspec],
        out_specs=[qspec, full, full, pl.BlockSpec((1, LANES), lambda b, i: (0, 0))],
        out_shape=[jax.ShapeDtypeStruct((bsz, lp, d), F32), jax.ShapeDtypeStruct((bsz, lp, KVW), F32),
                   jax.ShapeDtypeStruct((bsz, lp, KVW), F32), jax.ShapeDtypeStruct((1, LANES), F32)],
        compiler_params=_cparams(("arbitrary", "arbitrary")),
    )(q, k, k, k, kv, kv, kv, sinks, o, do)


def _concat_cols(name, a, b):
    rows = a.shape[0]

    def body(step, a_ref, b_ref, o_ref):
        o_ref[...] = jnp.concatenate([a_ref[...], b_ref[...]], axis=1)

    return _rowcall(name, body, rows, [a, b], [], [(a.shape[1] + b.shape[1], F32)])[0]


def _adamw(name, w, m, v, parts):
    rows, wd = w.shape
    n = parts.shape[0]
    tm = _row_tile(rows)

    def kern(w_ref, m_ref, v_ref, p_ref, g_ref, d_ref, m2_ref, v2_ref):
        g = p_ref[0].astype(F32)
        for k in range(1, n):
            g = g + p_ref[k].astype(F32)
        m2 = ADAM_B1 * m_ref[...] + (1.0 - ADAM_B1) * g
        v2 = ADAM_B2 * v_ref[...] + (1.0 - ADAM_B2) * (g * g)
        mh = m2 / (1.0 - ADAM_B1 ** ADAM_STEP)
        vh = v2 / (1.0 - ADAM_B2 ** ADAM_STEP)
        g_ref[...] = g
        d_ref[...] = -ADAM_LR * (mh / (jnp.sqrt(vh) + ADAM_EPS) + ADAM_WD * w_ref[...])
        m2_ref[...] = m2
        v2_ref[...] = v2

    spec = pl.BlockSpec((tm, wd), lambda i: (i, 0))
    sd = jax.ShapeDtypeStruct((rows, wd), F32)
    return pl.pallas_call(
        kern, name=name, grid=(rows // tm,),
        in_specs=[spec, spec, spec, pl.BlockSpec((n, tm, wd), lambda i: (0, i, 0))],
        out_specs=[spec] * 4, out_shape=[sd] * 4,
        compiler_params=_cparams(("arbitrary",)),
    )(w, m, v, parts)


def _pair_sum(name, parts, theirs, my_c):
    n, _, rows, wd = parts.shape
    tm = _row_tile(rows)

    def kern(c_ref, a_ref, b_ref, o_ref):
        o_ref[...] = (a_ref[...] + b_ref[...]).astype(BF16)

    return pl.pallas_call(
        kern, name=name,
        grid_spec=pltpu.PrefetchScalarGridSpec(
            num_scalar_prefetch=1, grid=(n, rows // tm),
            in_specs=[pl.BlockSpec((None, None, tm, wd), lambda k, i, c: (k, c[0], i, 0)),
                      pl.BlockSpec((None, tm, wd), lambda k, i, c: (k, i, 0))],
            out_specs=pl.BlockSpec((None, tm, wd), lambda k, i, c: (k, i, 0))),
        out_shape=jax.ShapeDtypeStruct((n, rows, wd), BF16), compiler_params=_cparams(("arbitrary", "arbitrary")),
    )(my_c, parts, theirs)


MESH = pl.DeviceIdType.MESH
ANY = pl.BlockSpec(memory_space=pl.ANY)


def _allgather(name, shards):
    n = len(shards)

    def body(*refs):
        x_refs, out_refs = refs[:n], refs[n:2 * n]
        send_sems, recv_sems, local_sems = refs[2 * n:]
        x, y, c = lax.axis_index("x"), lax.axis_index("y"), lax.axis_index("c")
        me, sibling = (x, y, c), (x, y, 1 - c)
        chips = [(1 - x, y), (x, 1 - y), (1 - x, 1 - y)]

        def copy(a, k, block, to, own=False):
            px, py, pc = block
            slot = out_refs[a].at[4 * px + 2 * py + pc]
            return pltpu.make_async_remote_copy(
                src_ref=x_refs[a] if own else slot, dst_ref=slot,
                send_sem=send_sems.at[a, k], recv_sem=recv_sems.at[a, k], device_id=to, device_id_type=MESH)

        mine = [pltpu.make_async_copy(x_refs[a], out_refs[a].at[4 * x + 2 * y + c], local_sems.at[a]) for a in range(n)]
        for cp in mine:
            cp.start()
        first = []
        for a in range(n):
            first.append(copy(a, 0, me, sibling, own=True))
            first += [copy(a, 1 + j, me, (*chip, c), own=True) for j, chip in enumerate(chips)]
        for cp in first:
            cp.start()
        passed = []
        for j, chip in enumerate(chips):
            for a in range(n):
                copy(a, 1 + j, (*chip, c), me).wait_recv()
                fwd = copy(a, 4 + j, (*chip, c), sibling)
                fwd.start()
                passed.append(fwd)
        for a in range(n):
            copy(a, 0, sibling, me).wait_recv()
            for j, chip in enumerate(chips):
                copy(a, 4 + j, (*chip, 1 - c), me).wait_recv()
        for cp in first + passed:
            cp.wait_send()
        for cp in mine:
            cp.wait()

    return pl.pallas_call(
        body, name=name, out_shape=[jax.ShapeDtypeStruct((N_DEV,) + s.shape, s.dtype) for s in shards],
        in_specs=[ANY] * n, out_specs=[ANY] * n,
        scratch_shapes=[pltpu.SemaphoreType.DMA((n, 7)), pltpu.SemaphoreType.DMA((n, 7)), pltpu.SemaphoreType.DMA((n,))],
    )(*shards)


def _sibling_swap(name, parts):
    n = len(parts)

    def body(*refs):
        p_refs, out_refs = refs[:n], refs[n:2 * n]
        send_sems, recv_sems = refs[2 * n:]
        x, y, c = lax.axis_index("x"), lax.axis_index("y"), lax.axis_index("c")
        copies = [pltpu.make_async_remote_copy(
            src_ref=p_refs[a].at[:, 1 - c], dst_ref=out_refs[a], send_sem=send_sems.at[a], recv_sem=recv_sems.at[a],
            device_id=(x, y, 1 - c), device_id_type=MESH) for a in range(n)]
        for cp in copies:
            cp.start()
        for cp in copies:
            cp.wait()

    return pl.pallas_call(
        body, name=name,
        out_shape=[jax.ShapeDtypeStruct((p.shape[0],) + p.shape[2:], p.dtype) for p in parts],
        in_specs=[ANY] * n, out_specs=[ANY] * n,
        scratch_shapes=[pltpu.SemaphoreType.DMA((n,)), pltpu.SemaphoreType.DMA((n,))],
    )(*parts)


def _chip_scatter(name, sums):
    n = len(sums)

    def body(*refs):
        s_refs, out_refs = refs[:n], refs[n:2 * n]
        send_sems, recv_sems, local_sems = refs[2 * n:]
        x, y, c = lax.axis_index("x"), lax.axis_index("y"), lax.axis_index("c")
        mychip = 2 * x + y
        chips = [(1 - x, y), (x, 1 - y), (1 - x, 1 - y)]
        mine = [pltpu.make_async_copy(s_refs[a].at[mychip], out_refs[a].at[mychip], local_sems.at[a]) for a in range(n)]
        for cp in mine:
            cp.start()

        def copy(a, j, block):
            px, py = chips[j]
            return pltpu.make_async_remote_copy(
                src_ref=s_refs[a].at[2 * px + py], dst_ref=out_refs[a].at[block],
                send_sem=send_sems.at[a, j], recv_sem=recv_sems.at[a, j], device_id=(px, py, c), device_id_type=MESH)

        copies = [copy(a, j, mychip) for j in range(3) for a in range(n)]
        for cp in copies:
            cp.start()
        for j, (px, py) in enumerate(chips):
            for a in range(n):
                copy(a, j, 2 * px + py).wait_recv()
        for cp in copies:
            cp.wait_send()
        for cp in mine:
            cp.wait()

    return pl.pallas_call(
        body, name=name, out_shape=[jax.ShapeDtypeStruct(s.shape, s.dtype) for s in sums],
        in_specs=[ANY] * n, out_specs=[ANY] * n,
        scratch_shapes=[pltpu.SemaphoreType.DMA((n, 3)), pltpu.SemaphoreType.DMA((n, 3)), pltpu.SemaphoreType.DMA((n,))],
    )(*sums)


BIG = (("ffn1_w_gate_up", 2), ("ffn1_w_down", 1), ("ffn2_w_gate_up", 2), ("ffn2_w_down", 1), ("ssm_w_in", 1),
       ("ssm_w_out", 2), ("w_kv", 0), ("attn_w_q", 1), ("attn_w_o", 1))
SMALL = ("ffn1_norm", "mix_norm", "ffn2_norm", "ssm_lambda_re", "ssm_lambda_im", "ssm_b_re", "ssm_b_im",
         "ssm_c_re", "ssm_c_im", "ssm_log_step", "kv_norm", "k_norm", "q_norm", "attn_sinks")
COLS = (("meta_tokens", 1), ("ssm_d", 1))
WEIGHTS = ("meta_tokens", "ffn1_norm", "ffn1_w_gate_up", "ffn1_w_down", "mix_norm", "ffn2_norm", "ffn2_w_gate_up",
           "ffn2_w_down", "ssm_w_in", "ssm_lambda_re", "ssm_lambda_im", "ssm_b_re", "ssm_b_im", "ssm_c_re",
           "ssm_c_im", "ssm_log_step", "ssm_d", "ssm_w_out", "kv_norm", "w_kv", "k_norm", "attn_w_q", "q_norm",
           "attn_sinks", "attn_w_o")


def _rows_of(a, width):
    n = math.prod(a.shape)
    if n % width == 0:
        return a.reshape(n // width, width)
    assert n < width
    return jnp.pad(a.reshape(1, n), ((0, 0), (0, width - n)))


def _pack_small(arrs, width):
    rows = jnp.concatenate([_rows_of(a.astype(F32), width) for a in arrs], axis=0)
    pad = (-rows.shape[0]) % 8
    return jnp.pad(rows, ((0, pad), (0, 0))) if pad else rows


def _unpack_small(buf, shapes, width):
    out, off = [], 0
    for shp in shapes:
        n = math.prod(shp)
        if n % width == 0:
            out.append(buf[off:off + n // width].reshape(shp))
            off += n // width
        else:
            out.append(buf[off, :n].reshape(shp))
            off += 1
    return out


def _shape2d(shp):
    return (math.prod(shp[:-1]), shp[-1])


def _unshard(g, axis):
    g = jnp.moveaxis(g, 0, axis)
    shp = g.shape
    return g.reshape(shp[:axis] + (shp[axis] * shp[axis + 1],) + shp[axis + 2:])


def _shard(full, axis):
    shp = full.shape
    g = full.reshape(shp[:axis] + (N_DEV, shp[axis] // N_DEV) + shp[axis + 1:])
    return jnp.moveaxis(g, axis, 0)


def _blockdiag(blocks):
    g, r, c = blocks.shape
    eye = jnp.eye(g, dtype=blocks.dtype)
    return (eye[:, None, :, None] * blocks[:, :, None, :]).reshape(g * r, g * c)


def _diagblocks(full, g):
    r, c = full.shape[0] // g, full.shape[1] // g
    f = full.reshape(g, r, g, c)
    idx = jnp.arange(g)
    return f[idx, :, idx, :]


def kernel(x, meta_tokens, ffn1_norm, ffn1_w_gate_up, ffn1_w_down, mix_norm, ffn2_norm, ffn2_w_gate_up, ffn2_w_down, ssm_w_in, ssm_lambda_re, ssm_lambda_im, ssm_b_re, ssm_b_im, ssm_c_re, ssm_c_im, ssm_log_step, ssm_d, ssm_w_out, kv_norm, w_kv, k_norm, attn_w_q, q_norm, attn_sinks, attn_w_o, loss_target, m_meta_tokens, m_ffn1_norm, m_ffn1_w_gate_up, m_ffn1_w_down, m_mix_norm, m_ffn2_norm, m_ffn2_w_gate_up, m_ffn2_w_down, m_ssm_w_in, m_ssm_lambda_re, m_ssm_lambda_im, m_ssm_b_re, m_ssm_b_im, m_ssm_c_re, m_ssm_c_im, m_ssm_log_step, m_ssm_d, m_ssm_w_out, m_kv_norm, m_w_kv, m_k_norm, m_attn_w_q, m_q_norm, m_attn_sinks, m_attn_w_o, v_meta_tokens, v_ffn1_norm, v_ffn1_w_gate_up, v_ffn1_w_down, v_mix_norm, v_ffn2_norm, v_ffn2_w_gate_up, v_ffn2_w_down, v_ssm_w_in, v_ssm_lambda_re, v_ssm_lambda_im, v_ssm_b_re, v_ssm_b_im, v_ssm_c_re, v_ssm_c_im, v_ssm_log_step, v_ssm_d, v_ssm_w_out, v_kv_norm, v_w_kv, v_k_norm, v_attn_w_q, v_q_norm, v_attn_sinks, v_attn_w_o):
    args = dict(locals())
    W = {n: args[n] for n in WEIGHTS}
    M = {n: args["m_" + n] for n in WEIGHTS}
    V = {n: args["v_" + n] for n in WEIGHTS}
    my_x, my_y, my_c = (lax.axis_index(a) for a in MESH_AXES)
    my_dev = 4 * my_x + 2 * my_y + my_c

    big_names = [n for n, _ in BIG]
    s2d = {n: _shape2d(W[n].shape) for n in big_names}
    widths = sorted({s2d[n][1] for n in big_names})
    by_width = [[n for n in big_names if s2d[n][1] == wd] for wd in widths]
    groups = [jnp.concatenate([W[n].reshape(s2d[n]).astype(BF16) for n in names], axis=0) for names in by_width]
    col_w = W["meta_tokens"].shape[1]
    cols = _pack_small([W["meta_tokens"], W["ssm_d"]], col_w)
    gathered = _allgather("gather_weights", groups + [cols])
    full = {}
    for names, g in zip(by_width, gathered):
        off = 0
        for n in names:
            r = s2d[n][0]
            full[n] = _unshard(g[:, off:off + r].reshape((N_DEV,) + W[n].shape), dict(BIG)[n])
            off += r
    gcols = gathered[-1]
    full["meta_tokens"] = _unshard(gcols[:, :N_META], 1)
    full["ssm_d"] = _unshard(gcols[:, N_META:N_META + 1, :W["ssm_d"].shape[1]], 1)

    grads = _local_step(x, loss_target, {**W, **full})
    loss = lax.psum(grads.pop("loss"), MESH_AXES)
    grad_x = grads.pop("x")

    parts = [grads[n].reshape((4, 2) + s2d[n]) for n in big_names]
    theirs = _sibling_swap("grad_sibling_swap", parts)
    c_arr = my_c.astype(jnp.int32).reshape(1)
    sums = [_pair_sum("pair_sum_" + n, p, t, c_arr) for n, p, t in zip(big_names, parts, theirs)]
    summed = _chip_scatter("grad_chip_scatter", sums)
    outs = [{}, {}, {}, {}]
    for n, sm in zip(big_names, summed):
        r4 = _adamw("adamw_" + n, W[n].reshape(s2d[n]), M[n].reshape(s2d[n]), V[n].reshape(s2d[n]), sm)
        for k in range(4):
            outs[k][n] = r4[k].reshape(W[n].shape)

    small_names = list(SMALL) + [n for n, _ in COLS]
    small_shapes = [grads[n].shape for n in small_names]
    small_parts = _allgather("gather_small_grads", [_pack_small([grads[n] for n in small_names], PACK_W)])[0]
    zero_cols = [jnp.zeros(grads[n].shape, F32) for n, _ in COLS]
    packs = lambda d: _pack_small([d[n] for n in SMALL] + zero_cols, PACK_W)
    r4 = _adamw("adamw_small", packs(W), packs(M), packs(V), small_parts)
    gsmall = None
    for k in range(4):
        un = dict(zip(small_names, _unpack_small(r4[k], small_shapes, PACK_W)))
        gsmall = un if k == 0 else gsmall
        outs[k].update({n: un[n] for n in SMALL})
    col_g = [lax.dynamic_slice_in_dim(gsmall[n], my_dev * W[n].shape[1], W[n].shape[1], axis=1) for n, _ in COLS]
    packc = lambda d: _pack_small([d[n] for n, _ in COLS], col_w)
    r4 = _adamw("adamw_cols", packc(W), packc(M), packc(V), _pack_small(col_g, col_w)[None])
    col_shapes = [W[n].shape for n, _ in COLS]
    for k in range(4):
        outs[k].update(dict(zip([n for n, _ in COLS], _unpack_small(r4[k], col_shapes, col_w))))

    res = [[outs[k][n] for n in WEIGHTS] for k in range(4)]
    return (loss, grad_x, *res[0], *res[1], *res[2], *res[3])


def _local_step(x, target, P):
    bsz, seq, d = x.shape
    lp = seq + PAD
    rows = bsz * lp
    depth = P["ffn1_norm"].shape[0]
    assert depth == 2
    f = P["ffn1_w_down"].shape[1]
    bf = lambda a: a.astype(BF16)
    row = lambda a: a.reshape(1, -1)

    pos = (jnp.arange(lp, dtype=F32) - float(META0))[:, None]
    half = HEAD_DIM // 2
    freqs = ROPE_THETA ** (-jnp.arange(0, half, dtype=F32) * 2.0 / HEAD_DIM)
    ang = pos * freqs[None, :]
    cos_t = jnp.tile(jnp.cos(ang), (1, LANES // half))
    sin_t = jnp.tile(jnp.concatenate([-jnp.sin(ang), jnp.sin(ang)], axis=1), (1, LANES // HEAD_DIM))
    gi = jnp.arange(LANES) // HEAD_DIM
    gmat = jnp.where(gi[:, None] == gi[None, :], 1.0 / HEAD_DIM, 0.0).astype(BF16)

    g_n, c_n, p_n = P["ssm_lambda_re"].shape[1], SSM_GROUP, SSM_STATE
    ns = g_n * p_n
    lr = P["ssm_lambda_re"][0].reshape(g_n, 1, p_n)
    li = P["ssm_lambda_im"][0].reshape(g_n, 1, p_n)
    ls = P["ssm_log_step"][0].reshape(g_n, 1, 1)
    brt = P["ssm_b_re"][0].transpose(0, 2, 1)
    bit = P["ssm_b_im"][0].transpose(0, 2, 1)
    ar, ai, bbr, bbi = _s5_params_fwd(lr, li, ls, brt, bit)
    a2 = jnp.concatenate([ar.reshape(1, ns), ai.reshape(1, ns)], axis=0)
    bfull = jnp.concatenate([_blockdiag(bbr), _blockdiag(bbi)], axis=1)
    cre_t = P["ssm_c_re"][0].transpose(0, 2, 1)
    cim_t = P["ssm_c_im"][0].transpose(0, 2, 1)
    cfull = jnp.concatenate([_blockdiag(cre_t), -_blockdiag(cim_t)], axis=0)
    dvec = P["ssm_d"].reshape(1, -1)

    ffn = lambda which, l: (row(P[which + "_norm"][l]), bf(P[which + "_w_gate_up"][l]), bf(P[which + "_w_down"][l]))
    w_in, w_out = bf(P["ssm_w_in"][0]), bf(P["ssm_w_out"][0])
    w_kv, w_q, w_o = bf(P["w_kv"]), bf(P["attn_w_q"][0]), bf(P["attn_w_o"][0])
    mix0, mix1, kvn = row(P["mix_norm"][0]), row(P["mix_norm"][1]), row(P["kv_norm"])
    kgain = jnp.tile(P["k_norm"].reshape(1, HEAD_DIM), (1, KVW // HEAD_DIM))
    qgain = jnp.tile(P["q_norm"].reshape(1, HEAD_DIM), (1, d // HEAD_DIM))
    sinks = P["attn_sinks"].reshape(1, -1)

    h0 = _embed(x, P["meta_tokens"]).reshape(rows, d)
    h1 = _ffn_fwd("ffn1_0_fwd", h0, *ffn("ffn1", 0))
    u = _proj_fwd("ssm_in_fwd", h1, mix0, w_in)
    y, xs = _s5_scan_fwd(u, bf(bfull), bf(cfull), a2, dvec, bsz)
    h2 = _glu_fwd(y, h1, w_out)
    h3 = _ffn_fwd("ffn2_0_fwd", h2, *ffn("ffn2", 0))
    kv = _proj_fwd("kv_fwd", h3, kvn, w_kv)
    k = _headrope_fwd("k_rope_fwd", kv, KVW, kgain, cos_t, sin_t, gmat, lp)
    h4 = _ffn_fwd("ffn1_1_fwd", h3, *ffn("ffn1", 1))
    q_raw = _proj_fwd("q_fwd", h4, mix1, w_q)
    q = _headrope_fwd("q_rope_fwd", q_raw, d, qgain, cos_t, sin_t, gmat, lp)
    r3 = lambda a: a.reshape(bsz, lp, a.shape[-1])
    o = _attn_fwd(r3(q), r3(k), r3(kv), sinks).reshape(rows, d)
    h5 = _lin_res_fwd("attn_out_fwd", o, w_o, h4)
    h6 = _ffn_fwd("ffn2_1_fwd", h5, *ffn("ffn2", 1))
    loss, dh6 = _loss(r3(h6), target)
    dh6 = dh6.reshape(rows, d)

    G = {"loss": loss[0, 0]}

    def ffn_back(name, which, l, h, dout):
        g, wgu, wd = ffn(which, l)
        dh, hn, dab, act, dg = _ffn_bwd(name, h, dout, g, wgu, wd)
        dwgu[which, l] = _mm_tn(name + "_wgu", hn, dab)
        G[which + "_w_down"] = _mm_tn_slots(name + "_wd", act, dout, 0.5, False, l, depth, G.get(which + "_w_down"))
        return dh, dg

    dwgu = {}

    dh5, dg_f2_1 = ffn_back("ffn2_1_bwd", "ffn2", 1, h5, dh6)
    do, dw_o = _lin_bwd("attn_out_bwd", o, w_o, dh5)
    dq, dk, dv, dsinks = _attn_bwd(r3(q), r3(k), r3(kv), sinks, r3(o), r3(do))
    dq_raw, dqg = _headrope_bwd("q_rope_bwd", q_raw, d, dq.reshape(rows, d), qgain, cos_t, sin_t, gmat, lp)
    dh4, dg_mix1, dw_q = _proj_bwd("q_bwd", h4, mix1, w_q, dq_raw, dh5)
    dh3, dg_f1_1 = ffn_back("ffn1_1_bwd", "ffn1", 1, h3, dh4)
    dk_raw, dkg = _headrope_bwd("k_rope_bwd", kv, KVW, dk.reshape(rows, KVW), kgain, cos_t, sin_t, gmat, lp)
    dkv = _concat_cols("dkv_concat", dk_raw, dv.reshape(rows, KVW))
    dh3, dg_kvn, dw_kv = _proj_bwd("kv_bwd", h3, kvn, w_kv, dkv, dh3)
    dh2, dg_f2_0 = ffn_back("ffn2_0_bwd", "ffn2", 0, h2, dh3)
    dy, dw_out = _glu_bwd(y, dh2, w_out)
    ctfull = jnp.concatenate([_blockdiag(P["ssm_c_re"][0]), -_blockdiag(P["ssm_c_im"][0])], axis=1)
    btfull = jnp.concatenate([_blockdiag(bbr.transpose(0, 2, 1)), _blockdiag(bbi.transpose(0, 2, 1))], axis=0)
    du, gx, da, dd = _s5_scan_bwd(dy, u, xs, bf(ctfull), bf(btfull), a2, dvec, bsz)
    dbfull = _mm_tn("ssm_db", u, gx)
    dcfull = _mm_tn("ssm_dc", xs, dy)
    dh1, dg_mix0, dw_in = _proj_bwd("ssm_in_bwd", h1, mix0, w_in, du, dh2)
    dh0, dg_f1_0 = ffn_back("ffn1_0_bwd", "ffn1", 0, h0, dh1)

    dbbr = _diagblocks(dbfull[:, :ns], g_n)
    dbbi = _diagblocks(dbfull[:, ns:], g_n)
    dlr, dli, dls, dbrt, dbit = _s5_params_bwd(lr, li, ls, brt, bit, da[:, :ns].reshape(g_n, 1, p_n),
                                               da[:, ns:].reshape(g_n, 1, p_n), dbbr, dbbi)
    dh0 = r3(dh0)
    G["x"] = dh0[:, PAD:, :]
    G["meta_tokens"] = _meta_sum(dh0)
    G["ffn1_norm"] = jnp.concatenate([dg_f1_0, dg_f1_1], axis=0)
    G["ffn2_norm"] = jnp.concatenate([dg_f2_0, dg_f2_1], axis=0)
    G["mix_norm"] = jnp.concatenate([dg_mix0, dg_mix1], axis=0)
    for which in ("ffn1", "ffn2"):
        G[which + "_w_gate_up"] = _shard(jnp.stack([dwgu[which, l] for l in range(depth)]), 2)
    G["ssm_w_in"] = dw_in
    G["ssm_lambda_re"] = dlr.reshape(1, g_n, p_n)
    G["ssm_lambda_im"] = dli.reshape(1, g_n, p_n)
    G["ssm_log_step"] = dls.reshape(1, g_n)
    G["ssm_b_re"] = dbrt.transpose(0, 2, 1)[None]
    G["ssm_b_im"] = dbit.transpose(0, 2, 1)[None]
    G["ssm_c_re"] = _diagblocks(dcfull[:ns], g_n).transpose(0, 2, 1)[None]
    G["ssm_c_im"] = -_diagblocks(dcfull[ns:], g_n).transpose(0, 2, 1)[None]
    G["ssm_d"] = dd
    G["ssm_w_out"] = dw_out
    G["kv_norm"] = dg_kvn.reshape(-1)
    G["w_kv"] = dw_kv
    G["k_norm"] = dkg[0, :HEAD_DIM]
    G["attn_w_q"] = dw_q[None]
    G["q_norm"] = dqg[:, :HEAD_DIM]
    G["attn_sinks"] = dsinks[:, :N_KV_HEADS * Q_PER_KV]
    G["attn_w_o"] = dw_o[None]
    return G
```

```python
import functools
import math

import jax
import jax.numpy as jnp
from jax import lax
from jax.experimental import pallas as pl
from jax.experimental.pallas import tpu as pltpu

F32 = jnp.float32
BF16 = jnp.bfloat16

N_META = 16
PAD = 128
META0 = PAD - N_META
HEAD_DIM = 64
N_KV_HEADS = 4
Q_PER_KV = 4
SSM_GROUP = 16
SSM_STATE = 64
EPS = 1e-6
NEG_INF = -1e30
ROPE_THETA = 10000.0
ADAM_LR, ADAM_B1, ADAM_B2, ADAM_EPS, ADAM_WD, ADAM_STEP = 0.001, 0.9, 0.999, 1e-08, 0.01, 10
LANES = 128
PACK_W = 1024
VMEM_LIMIT = 56 * 1024 * 1024
MESH_AXES = ("x", "y", "c")
N_DEV = 8


def _cparams(sem=None):
    return pltpu.CompilerParams(dimension_semantics=sem, vmem_limit_bytes=VMEM_LIMIT)


def _row_tile(rows):
    for tm in (384, 256, 128, 64, 32, 16, 8):
        if rows % tm == 0:
            return tm
    raise ValueError(rows)


TN_BUDGET = 44 * 1024 * 1024


def _tn_tile(rows, a, b, k1, tn):
    sa, sb = a.dtype.itemsize, b.dtype.itemsize
    for tm in (2112, 1056, 768, 528, 384, 256, 128, 64, 32, 16, 8):
        need = 2 * tm * (k1 * sa + tn * sb) + 3 * k1 * tn * 4 + tm * (k1 + tn) * 2
        if rows % tm == 0 and need <= TN_BUDGET:
            return tm
    raise ValueError(rows)


def _dot(a, b):
    return jnp.dot(a.astype(BF16), b.astype(BF16), preferred_element_type=F32)


def _dot_nt(a, b):
    return lax.dot_general(a.astype(BF16), b.astype(BF16), (((1,), (1,)), ((), ())), preferred_element_type=F32)


def _dot_tn(a, b):
    return lax.dot_general(a.astype(BF16), b.astype(BF16), (((0,), (0,)), ((), ())), preferred_element_type=F32)


def _rms(x, g):
    rstd = lax.rsqrt(jnp.mean(x * x, axis=-1, keepdims=True) + EPS)
    y = x * rstd
    return y * g, y, rstd


def _rms_bwd(dhn, y, rstd, g):
    dyn = dhn * g
    dx = rstd * (dyn - y * jnp.mean(dyn * y, axis=-1, keepdims=True))
    return dx, jnp.sum(dhn * y, axis=0, keepdims=True)


def _sigmoid(x):
    return 1.0 / (1.0 + jnp.exp(-x))


_GELU_C = math.sqrt(2.0 / math.pi)


def _gelu(y):
    t = jnp.tanh(_GELU_C * (y + 0.044715 * y * y * y))
    return 0.5 * y * (1.0 + t), t


def _gelu_grad(y, t):
    return 0.5 * (1.0 + t) + 0.5 * y * (1.0 - t * t) * _GELU_C * (1.0 + 3.0 * 0.044715 * y * y)


def _rowcall(name, body, rows, row_ins, const_ins, row_outs, acc_outs=(), tm=None, row_in_maps=None):
    tm = tm or _row_tile(rows)
    steps = rows // tm
    in_specs = []
    for k, a in enumerate(row_ins):
        if row_in_maps is not None and row_in_maps[k] is not None:
            in_specs.append(pl.BlockSpec(*row_in_maps[k]))
        else:
            in_specs.append(pl.BlockSpec((tm, a.shape[1]), lambda i: (i, 0)))
    for a in const_ins:
        in_specs.append(pl.BlockSpec(a.shape, lambda i, nd=a.ndim: (0,) * nd, pipeline_mode=pl.Buffered(1)))
    out_shape, out_specs = [], []
    for w, dt in row_outs:
        out_shape.append(jax.ShapeDtypeStruct((rows, w), dt))
        out_specs.append(pl.BlockSpec((tm, w), lambda i: (i, 0)))
    for shp, dt in acc_outs:
        out_shape.append(jax.ShapeDtypeStruct(shp, dt))
        out_specs.append(pl.BlockSpec(shp, lambda i, nd=len(shp): (0,) * nd))

    def kern(*refs):
        body(pl.program_id(0), *refs)

    return pl.pallas_call(
        kern, name=name, grid=(steps,), in_specs=in_specs, out_specs=out_specs, out_shape=out_shape,
        compiler_params=_cparams(("arbitrary",)),
    )(*row_ins, *const_ins)


def _acc(step, ref, val):
    @pl.when(step == 0)
    def _():
        ref[...] = val

    @pl.when(step != 0)
    def _():
        ref[...] += val


def _embed(x, meta):
    bsz, seq, d = x.shape
    nb = seq // PAD + 1

    def kern(x_ref, m_ref, o_ref):
        i = pl.program_id(1)

        @pl.when(i == 0)
        def _():
            o_ref[0, 0:META0, :] = jnp.zeros((META0, d), F32)
            o_ref[0, META0:PAD, :] = m_ref[...]

        @pl.when(i != 0)
        def _():
            o_ref[0] = x_ref[0]

    return pl.pallas_call(
        kern, name="embed", grid=(bsz, nb),
        in_specs=[pl.BlockSpec((1, PAD, d), lambda b, i: (b, jnp.maximum(i - 1, 0), 0)),
                  pl.BlockSpec((N_META, d), lambda b, i: (0, 0))],
        out_specs=pl.BlockSpec((1, PAD, d), lambda b, i: (b, i, 0)),
        out_shape=jax.ShapeDtypeStruct((bsz, seq + PAD, d), F32),
        compiler_params=_cparams(("arbitrary", "arbitrary")),
    )(x, meta)


def _loss(h6, target):
    bsz, lp, d = h6.shape
    nb = lp // PAD

    def kern(h_ref, t_ref, l_ref, d_ref):
        b, i = pl.program_id(0), pl.program_id(1)

        @pl.when((b == 0) & (i == 0))
        def _():
            l_ref[...] = jnp.zeros_like(l_ref)

        @pl.when(i == 0)
        def _():
            d_ref[0] = jnp.zeros((PAD, d), F32)

        @pl.when(i != 0)
        def _():
            e = h_ref[0] - t_ref[0]
            d_ref[0] = e * (1.0 / d)
            l_ref[...] += 0.5 * jnp.sum(jnp.mean(e * e, axis=-1, keepdims=True))

    return pl.pallas_call(
        kern, name="loss", grid=(bsz, nb),
        in_specs=[pl.BlockSpec((1, PAD, d), lambda b, i: (b, i, 0)),
                  pl.BlockSpec((1, PAD, d), lambda b, i: (b, jnp.maximum(i - 1, 0), 0))],
        out_specs=[pl.BlockSpec((1, LANES), lambda b, i: (0, 0)),
                   pl.BlockSpec((1, PAD, d), lambda b, i: (b, i, 0))],
        out_shape=[jax.ShapeDtypeStruct((1, LANES), F32), jax.ShapeDtypeStruct((bsz, lp, d), F32)],
        compiler_params=_cparams(("arbitrary", "arbitrary")),
    )(h6, target)


def _meta_sum(dh0):
    bsz, lp, d = dh0.shape

    def kern(d_ref, o_ref):
        _acc(pl.program_id(0), o_ref, d_ref[0, META0:PAD, :])

    return pl.pallas_call(
        kern, name="meta_sum", grid=(bsz,),
        in_specs=[pl.BlockSpec((1, PAD, d), lambda b: (b, 0, 0))],
        out_specs=pl.BlockSpec((N_META, d), lambda b: (0, 0)),
        out_shape=jax.ShapeDtypeStruct((N_META, d), F32),
        compiler_params=_cparams(("arbitrary",)),
    )(dh0)


def _ffn_chunks(f):
    for n in (2, 4, 1, 11, 22):
        if f % n == 0 and (f // n) % LANES == 0:
            return n
    raise ValueError(f)


def _ffn_fwd(name, h, g, wgu, wd):
    rows, d = h.shape
    f = wd.shape[0]
    nf = _ffn_chunks(f)
    tf = f // nf

    def body(step, h_ref, g_ref, wgu_ref, wd_ref, o_ref, ab_ref):
        hx = h_ref[...]
        hb = _rms(hx, g_ref[...])[0].astype(BF16)
        acc = jnp.zeros(hx.shape, F32)
        for j in range(nf):
            ga, ua = slice(j * tf, (j + 1) * tf), slice(f + j * tf, f + (j + 1) * tf)
            a = _dot(hb, wgu_ref[:, ga])
            b = _dot(hb, wgu_ref[:, ua])
            ab_ref[:, ga] = a.astype(BF16)
            ab_ref[:, ua] = b.astype(BF16)
            acc = acc + _dot(a * _sigmoid(a) * b, wd_ref[ga, :])
        o_ref[...] = hx + 0.5 * acc

    return _rowcall(name, body, rows, [h], [g, wgu, wd], [(d, F32), (2 * f, BF16)])


def _ffn_bwd(name, h, ab, dout, g, wgu, wd):
    rows, d = h.shape
    f = wd.shape[0]
    nf = _ffn_chunks(f)
    tf = f // nf

    def body(step, h_ref, ab_ref, do_ref, g_ref, wgu_ref, wd_ref, dh_ref, hn_ref, dab_ref, act_ref, dg_ref):
        hx, dout_x, gx = h_ref[...], do_ref[...], g_ref[...]
        hn, y, rstd = _rms(hx, gx)
        hn_ref[...] = hn.astype(BF16)
        dhalf = (0.5 * dout_x).astype(BF16)
        dhn = jnp.zeros(hx.shape, F32)
        for j in range(nf):
            ga, ua = slice(j * tf, (j + 1) * tf), slice(f + j * tf, f + (j + 1) * tf)
            a = ab_ref[:, ga].astype(F32)
            b = ab_ref[:, ua].astype(F32)
            s = _sigmoid(a)
            silu = a * s
            act_ref[:, ga] = (silu * b).astype(BF16)
            dact = _dot_nt(dhalf, wd_ref[ga, :])
            da = (dact * b * (s + silu * (1.0 - s))).astype(BF16)
            db = (dact * silu).astype(BF16)
            dab_ref[:, ga] = da
            dab_ref[:, ua] = db
            dhn = dhn + _dot_nt(da, wgu_ref[:, ga]) + _dot_nt(db, wgu_ref[:, ua])
        dx, dg = _rms_bwd(dhn, y, rstd, gx)
        dh_ref[...] = dout_x + dx
        _acc(step, dg_ref, dg)

    return _rowcall(name, body, rows, [h, ab, dout], [g, wgu, wd],
                    [(d, F32), (d, BF16), (2 * f, BF16), (f, BF16)], [((1, d), F32)])


def _mm_tn(name, a, b, scale=1.0):
    rows, k1 = a.shape
    k2 = b.shape[1]
    tn = k2
    for cand in (512, 704, 1408, 1024):
        if k2 % cand == 0 and k1 * cand * 4 <= 6 * 1024 * 1024:
            tn = cand
    tm = _tn_tile(rows, a, b, k1, tn)
    steps = rows // tm

    def kern(a_ref, b_ref, o_ref):
        bx = b_ref[...]
        if scale != 1.0:
            bx = bx * scale
        _acc(pl.program_id(1), o_ref, _dot_tn(a_ref[...], bx))

    return pl.pallas_call(
        kern, name=name, grid=(k2 // tn, steps),
        in_specs=[pl.BlockSpec((tm, k1), lambda j, i: (i, 0)), pl.BlockSpec((tm, tn), lambda j, i: (i, j))],
        out_specs=pl.BlockSpec((k1, tn), lambda j, i: (0, j)),
        out_shape=jax.ShapeDtypeStruct((k1, k2), F32),
        compiler_params=_cparams(("arbitrary", "arbitrary")),
    )(a, b)


def _mm_tn_slots(name, a, b, scale, layer, nlayers, into):
    rows, k1 = a.shape
    k2 = b.shape[1]
    tn = 512 if k2 % 512 == 0 else k2
    sr = k1 // N_DEV
    shape = (N_DEV, nlayers, sr, k2)
    out_spec = pl.BlockSpec((N_DEV, None, sr, tn), lambda j, i: (0, layer, 0, j))
    tm = _tn_tile(rows, a, b, k1, tn)
    steps = rows // tm

    def kern(*refs):
        a_ref, b_ref, o_ref = refs[0], refs[1], refs[-1]
        bx = b_ref[...]
        if scale != 1.0:
            bx = bx * scale
        res = _dot_tn(a_ref[...], bx)
        step = pl.program_id(1)
        for s in range(N_DEV):
            _acc(step, o_ref.at[s], res[s * sr:(s + 1) * sr])

    in_specs = [pl.BlockSpec((tm, k1), lambda j, i: (i, 0)), pl.BlockSpec((tm, tn), lambda j, i: (i, j))]
    ins = [a, b]
    alias = {}
    if into is not None:
        in_specs.append(ANY)
        ins.append(into)
        alias = {2: 0}
    return pl.pallas_call(
        kern, name=name, grid=(k2 // tn, steps), in_specs=in_specs, out_specs=out_spec,
        out_shape=jax.ShapeDtypeStruct(shape, F32), input_output_aliases=alias,
        compiler_params=_cparams(("arbitrary", "arbitrary")),
    )(*ins)


def _proj_fwd(name, h, g, w):
    rows = h.shape[0]

    def body(step, h_ref, g_ref, w_ref, o_ref):
        o_ref[...] = _dot(_rms(h_ref[...], g_ref[...])[0], w_ref[...])

    return _rowcall(name, body, rows, [h], [g, w], [(w.shape[1], F32)])[0]


def _proj_bwd(name, h, g, w, dy, dres):
    rows, d = h.shape

    def body(step, h_ref, dy_ref, dr_ref, g_ref, w_ref, dh_ref, dg_ref, dw_ref):
        gx = g_ref[...]
        hn, y, rstd = _rms(h_ref[...], gx)
        dyx = dy_ref[...]
        dx, dg = _rms_bwd(_dot_nt(dyx, w_ref[...]), y, rstd, gx)
        dh_ref[...] = dr_ref[...] + dx
        _acc(step, dg_ref, dg)
        _acc(step, dw_ref, _dot_tn(hn, dyx))

    return _rowcall(name, body, rows, [h, dy, dres], [g, w], [(d, F32)], [((1, d), F32), (w.shape, F32)])


def _lin_res_fwd(name, a, w, res):
    rows = a.shape[0]

    def body(step, a_ref, r_ref, w_ref, o_ref):
        o_ref[...] = r_ref[...] + _dot(a_ref[...], w_ref[...])

    return _rowcall(name, body, rows, [a, res], [w], [(w.shape[1], F32)])[0]


def _lin_bwd(name, a, w, dy):
    rows, k = a.shape

    def body(step, a_ref, dy_ref, w_ref, da_ref, dw_ref):
        dyx = dy_ref[...]
        da_ref[...] = _dot_nt(dyx, w_ref[...])
        _acc(step, dw_ref, _dot_tn(a_ref[...], dyx))

    return _rowcall(name, body, rows, [a, dy], [w], [(k, F32)], [(w.shape, F32)])


def _s5_param_fn(lr, li, ls, brt, bit):
    step = jnp.exp(ls)
    mag = jnp.exp(lr * step)
    ar = mag * jnp.cos(li * step)
    ai = mag * jnp.sin(li * step)
    den = lr * lr + li * li
    nr, ni = ar - 1.0, ai
    cr = (nr * lr + ni * li) / den
    ci = (ni * lr - nr * li) / den
    return ar, ai, cr * brt - ci * bit, cr * bit + ci * brt


def _s5_params_fwd(lr, li, ls, brt, bit):
    def kern(lr_ref, li_ref, ls_ref, br_ref, bi_ref, ar_ref, ai_ref, bbr_ref, bbi_ref):
        ar, ai, bbr, bbi = _s5_param_fn(lr_ref[...], li_ref[...], ls_ref[...], br_ref[...], bi_ref[...])
        ar_ref[...], ai_ref[...], bbr_ref[...], bbi_ref[...] = ar, ai, bbr, bbi

    sd = jax.ShapeDtypeStruct
    return pl.pallas_call(
        kern, name="s5_params_fwd",
        out_shape=[sd(lr.shape, F32), sd(lr.shape, F32), sd(brt.shape, F32), sd(brt.shape, F32)],
    )(lr, li, ls, brt, bit)


def _s5_params_bwd(lr, li, ls, brt, bit, dar, dai, dbbr, dbbi):
    def kern(lr_ref, li_ref, ls_ref, br_ref, bi_ref, dar_ref, dai_ref, dbbr_ref, dbbi_ref,
             dlr_ref, dli_ref, dls_ref, dbr_ref, dbi_ref):
        _, vjp = jax.vjp(_s5_param_fn, lr_ref[...], li_ref[...], ls_ref[...], br_ref[...], bi_ref[...])
        dlr, dli, dls, dbr, dbi = vjp((dar_ref[...], dai_ref[...], dbbr_ref[...], dbbi_ref[...]))
        dlr_ref[...], dli_ref[...], dls_ref[...], dbr_ref[...], dbi_ref[...] = dlr, dli, dls, dbr, dbi

    sd = jax.ShapeDtypeStruct
    return pl.pallas_call(
        kern, name="s5_params_bwd",
        out_shape=[sd(lr.shape, F32), sd(lr.shape, F32), sd(ls.shape, F32), sd(brt.shape, F32), sd(brt.shape, F32)],
    )(lr, li, ls, brt, bit, dar, dai, dbbr, dbbi)


SCAN_LW = 512


def _scan_tables(a_ref, tab_ref, conj):
    ns = a_ref.shape[1]
    ar = jnp.broadcast_to(a_ref[0:1, :], (8, ns))
    ai = jnp.broadcast_to(a_ref[1:2, :], (8, ns))
    if conj:
        ai = -ai
    p1r, p1i = ar, ai
    p2r, p2i = p1r * p1r - p1i * p1i, 2.0 * p1r * p1i
    p4r, p4i = p2r * p2r - p2i * p2i, 2.0 * p2r * p2i
    row = lax.broadcasted_iota(jnp.int32, (8, ns), 0)
    e = row if not conj else 7 - row
    one, zero = jnp.ones((8, ns), F32), jnp.zeros((8, ns), F32)
    qr, qi = p1r, p1i
    for bit, (pr, pi) in ((1, (p1r, p1i)), (2, (p2r, p2i)), (4, (p4r, p4i))):
        sel = (e & bit) != 0
        fr, fi = jnp.where(sel, pr, one), jnp.where(sel, pi, zero)
        qr, qi = qr * fr - qi * fi, qr * fi + qi * fr
    for k, v in enumerate((p1r, p1i, p2r, p2i, p4r, p4i, qr, qi)):
        tab_ref[k] = v


def _scan_block(x_ref, tab_ref, carry_ref, t_rows, ns, reverse):
    ngrp = t_rows // 8
    row = lax.broadcasted_iota(jnp.int32, (8, SCAN_LW), 0)
    for lc in range(ns // SCAN_LW):
        lre = pl.ds(lc * SCAN_LW, SCAN_LW)
        lim = pl.ds(ns + lc * SCAN_LW, SCAN_LW)

        def group(k, carry, lre=lre, lim=lim):
            cr, ci = carry
            gi = (ngrp - 1 - k) if reverse else k
            rows = pl.ds(pl.multiple_of(gi * 8, 8), 8)
            vr, vi = x_ref[rows, lre], x_ref[rows, lim]
            for lvl, dsh in enumerate((1, 2, 4)):
                pr, pi = tab_ref[2 * lvl, :, lre], tab_ref[2 * lvl + 1, :, lre]
                if reverse:
                    keep = row < 8 - dsh
                    sr, si = pltpu.roll(vr, 8 - dsh, 0), pltpu.roll(vi, 8 - dsh, 0)
                else:
                    keep = row >= dsh
                    sr, si = pltpu.roll(vr, dsh, 0), pltpu.roll(vi, dsh, 0)
                sr, si = jnp.where(keep, sr, 0.0), jnp.where(keep, si, 0.0)
                vr, vi = vr + pr * sr - pi * si, vi + pr * si + pi * sr
            qr, qi = tab_ref[6, :, lre], tab_ref[7, :, lre]
            vr, vi = vr + qr * cr - qi * ci, vi + qr * ci + qi * cr
            x_ref[rows, lre], x_ref[rows, lim] = vr, vi
            edge = 0 if reverse else 7
            return (jnp.broadcast_to(vr[edge:edge + 1, :], (8, SCAN_LW)),
                    jnp.broadcast_to(vi[edge:edge + 1, :], (8, SCAN_LW)))

        cr, ci = lax.fori_loop(0, ngrp, group, (carry_ref[:, lre], carry_ref[:, lim]))
        carry_ref[:, lre], carry_ref[:, lim] = cr, ci


def _scan_rows(lp):
    for t in (384, 256, 128):
        if lp % t == 0:
            return t
    raise ValueError(lp)


def _s5_scan_fwd(u, bfull, cfull, a2, dvec, bsz):
    rows, hw = u.shape
    ns = a2.shape[1]
    lp = rows // bsz
    t_rows = _scan_rows(lp)
    nc = lp // t_rows

    def kern(u_ref, b_ref, c_ref, a_ref, d_ref, y_ref, x_ref, tab_ref, carry_ref):
        c = pl.program_id(1)

        @pl.when((pl.program_id(0) == 0) & (c == 0))
        def _():
            _scan_tables(a_ref, tab_ref, conj=False)

        @pl.when(c == 0)
        def _():
            carry_ref[...] = jnp.zeros_like(carry_ref)

        ux = u_ref[...]
        x_ref[...] = _dot(ux, b_ref[...])
        _scan_block(x_ref, tab_ref, carry_ref, t_rows, ns, reverse=False)
        y_ref[...] = _dot(x_ref[...], c_ref[...]) + d_ref[...] * ux

    const = lambda shp: pl.BlockSpec(shp, lambda b, c: (0,) * len(shp), pipeline_mode=pl.Buffered(1))
    return pl.pallas_call(
        kern, name="s5_scan_fwd", grid=(bsz, nc),
        in_specs=[pl.BlockSpec((t_rows, hw), lambda b, c: (b * nc + c, 0)),
                  const(bfull.shape), const(cfull.shape), const(a2.shape), const(dvec.shape)],
        out_specs=[pl.BlockSpec((t_rows, hw), lambda b, c: (b * nc + c, 0)),
                   pl.BlockSpec((t_rows, 2 * ns), lambda b, c: (b * nc + c, 0))],
        out_shape=[jax.ShapeDtypeStruct((rows, hw), F32), jax.ShapeDtypeStruct((rows, 2 * ns), F32)],
        scratch_shapes=[pltpu.VMEM((8, 8, ns), F32), pltpu.VMEM((8, 2 * ns), F32)],
        compiler_params=_cparams(("arbitrary", "arbitrary")),
    )(u, bfull, cfull, a2, dvec)


def _s5_scan_bwd(dy, u, xs, ctfull, btfull, a2, dvec, bsz):
    rows, hw = u.shape
    ns = a2.shape[1]
    lp = rows // bsz
    t_rows = _scan_rows(lp)
    nc = lp // t_rows
    blk = lambda b, c: (b * nc + (nc - 1 - c), 0)

    def prev8(b, c):
        first = (b * nc + (nc - 1 - c)) * (t_rows // 8)
        return (jnp.maximum(first - 1, 0), 0)

    def kern(dy_ref, u_ref, x_ref, xp_ref, ct_ref, bt_ref, a_ref, d_ref, du_ref, gx_ref, da_ref, dd_ref,
             tab_ref, carry_ref):
        b, c = pl.program_id(0), pl.program_id(1)
        first = (b == 0) & (c == 0)

        @pl.when(first)
        def _():
            _scan_tables(a_ref, tab_ref, conj=True)

        @pl.when(c == 0)
        def _():
            carry_ref[...] = jnp.zeros_like(carry_ref)

        dyx, ux = dy_ref[...], u_ref[...]
        gx_ref[...] = _dot(dyx, ct_ref[...])
        _scan_block(gx_ref, tab_ref, carry_ref, t_rows, ns, reverse=True)
        gx = gx_ref[...]
        du_ref[...] = _dot(gx, bt_ref[...]) + d_ref[...] * dyx
        xprev = pltpu.roll(x_ref[...], 1, 0)
        seq_start = c == nc - 1
        head = jnp.where(seq_start, 0.0, xp_ref[7:8, :])
        rid = lax.broadcasted_iota(jnp.int32, (t_rows, 1), 0)
        xprev = jnp.where(rid == 0, head, xprev)
        xr, xi, gr, gi = xprev[:, :ns], xprev[:, ns:], gx[:, :ns], gx[:, ns:]
        da = jnp.concatenate([jnp.sum(xr * gr + xi * gi, axis=0, keepdims=True),
                              jnp.sum(xr * gi - xi * gr, axis=0, keepdims=True)], axis=1)
        dd = jnp.sum(dyx * ux, axis=0, keepdims=True)

        @pl.when(first)
        def _():
            da_ref[...] = da
            dd_ref[...] = dd

        @pl.when(jnp.logical_not(first))
        def _():
            da_ref[...] += da
            dd_ref[...] += dd

    const = lambda shp: pl.BlockSpec(shp, lambda b, c: (0,) * len(shp), pipeline_mode=pl.Buffered(1))
    return pl.pallas_call(
        kern, name="s5_scan_bwd", grid=(bsz, nc),
        in_specs=[pl.BlockSpec((t_rows, hw), blk), pl.BlockSpec((t_rows, hw), blk),
                  pl.BlockSpec((t_rows, 2 * ns), blk), pl.BlockSpec((8, 2 * ns), prev8),
                  const(ctfull.shape), const(btfull.shape), const(a2.shape), const(dvec.shape)],
        out_specs=[pl.BlockSpec((t_rows, hw), blk), pl.BlockSpec((t_rows, 2 * ns), blk),
                   pl.BlockSpec((1, 2 * ns), lambda b, c: (0, 0)), pl.BlockSpec((1, hw), lambda b, c: (0, 0))],
        out_shape=[jax.ShapeDtypeStruct((rows, hw), F32), jax.ShapeDtypeStruct((rows, 2 * ns), F32),
                   jax.ShapeDtypeStruct((1, 2 * ns), F32), jax.ShapeDtypeStruct((1, hw), F32)],
        scratch_shapes=[pltpu.VMEM((8, 8, ns), F32), pltpu.VMEM((8, 2 * ns), F32)],
        compiler_params=_cparams(("arbitrary", "arbitrary")),
    )(dy, u, xs, xs, ctfull, btfull, a2, dvec)


def _glu_fwd(y, h1, wout):
    rows, d = h1.shape

    def body(step, y_ref, h_ref, w_ref, o_ref):
        z = _dot(_gelu(y_ref[...])[0], w_ref[...])
        o_ref[...] = h_ref[...] + z[:, :d] * _sigmoid(z[:, d:])

    return _rowcall("glu_fwd", body, rows, [y, h1], [wout], [(d, F32)])[0]


def _glu_bwd(y, dh2, wout):
    rows, d = dh2.shape
    hw = y.shape[1]

    def body(step, y_ref, dh_ref, w_ref, dy_ref, dw_ref):
        yx, dh = y_ref[...], dh_ref[...]
        gl, t = _gelu(yx)
        z = _dot(gl, w_ref[...])
        za, sg = z[:, :d], _sigmoid(z[:, d:])
        dza = dh * sg
        dzg = dh * za * sg * (1.0 - sg)
        dgl = _dot_nt(dza, w_ref[:, :d]) + _dot_nt(dzg, w_ref[:, d:])
        dy_ref[...] = dgl * _gelu_grad(yx, t)
        for half, dz in enumerate((dza, dzg)):
            dw = _dot_tn(gl, dz)
            for s in range(N_DEV // 2):
                _acc(step, dw_ref.at[half * (N_DEV // 2) + s], dw[:, s * cw:(s + 1) * cw])

    cw = 2 * d // N_DEV
    return _rowcall("glu_bwd", body, rows, [y, dh2], [wout], [(hw, F32)], [((N_DEV, hw, cw), F32)])


def _gmean64(x2, gmat):
    hi = x2.astype(BF16)
    r1 = x2 - hi.astype(F32)
    mid = r1.astype(BF16)
    lo = (r1 - mid.astype(F32)).astype(BF16)
    outs = []
    for j in range(x2.shape[1] // LANES):
        sl = slice(j * LANES, (j + 1) * LANES)
        f = lambda p: jnp.dot(p[:, sl], gmat, preferred_element_type=F32)
        outs.append(f(hi) + f(mid) + f(lo))
    return outs[0] if len(outs) == 1 else jnp.concatenate(outs, axis=1)


def _swap32(x):
    w = x.shape[1]
    lane = lax.broadcasted_iota(jnp.int32, (1, w), 1)
    return jnp.where((lane & 32) == 0, pltpu.roll(x, w - 32, 1), pltpu.roll(x, 32, 1))


def _tile_lanes(t, w):
    reps = w // t.shape[1]
    return t if reps == 1 else jnp.concatenate([t] * reps, axis=1)


def _headrope_fwd(name, raw, w, gain, cos, sin, gmat, lp):
    rows = raw.shape[0]
    tm = _row_tile(lp)
    per = lp // tm

    def body(step, x_ref, c_ref, s_ref, g_ref, gm_ref, o_ref):
        x = x_ref[...]
        rstd = lax.rsqrt(_gmean64(x * x, gm_ref[...]) + EPS)
        z = x * rstd * g_ref[...]
        o_ref[...] = z * _tile_lanes(c_ref[...], w) + _swap32(z) * _tile_lanes(s_ref[...], w)

    maps = [((tm, w), lambda i: (i, 0)), ((tm, LANES), lambda i: (i % per, 0)), ((tm, LANES), lambda i: (i % per, 0))]
    return _rowcall(name, body, rows, [raw, cos, sin], [gain, gmat], [(w, F32)], tm=tm, row_in_maps=maps)[0]


def _headrope_bwd(name, raw, w, dout, gain, cos, sin, gmat, lp):
    rows = raw.shape[0]
    tm = _row_tile(lp)
    per = lp // tm

    def body(step, x_ref, do_ref, c_ref, s_ref, g_ref, gm_ref, dx_ref, dg_ref):
        x, dout_x, gx, gm = x_ref[...], do_ref[...], g_ref[...], gm_ref[...]
        rstd = lax.rsqrt(_gmean64(x * x, gm) + EPS)
        yn = x * rstd
        dz = dout_x * _tile_lanes(c_ref[...], w) + _swap32(dout_x * _tile_lanes(s_ref[...], w))
        dyn = dz * gx
        dx_ref[...] = rstd * (dyn - yn * _gmean64(dyn * yn, gm))
        dg = jnp.sum(dz * yn, axis=0, keepdims=True)
        sh = w // 2
        while sh >= HEAD_DIM:
            dg = dg + pltpu.roll(dg, sh, 1)
            sh //= 2
        _acc(step, dg_ref, dg)

    maps = [((tm, w), lambda i: (i, 0)), None, ((tm, LANES), lambda i: (i % per, 0)), ((tm, LANES), lambda i: (i % per, 0))]
    return _rowcall(name, body, rows, [raw, dout, cos, sin], [gain, gmat], [(w, F32)], [((1, w), F32)],
                    tm=tm, row_in_maps=maps)


KVW = N_KV_HEADS * HEAD_DIM
QB = 128


def _fold4(x):
    y = x + pltpu.roll(x, 128, 1)
    return y + pltpu.roll(y, 64, 1)


def _attn_scores(i, q_ref, k0_ref, kp_ref, kc_ref, sink_ref, h):
    lane = lax.broadcasted_iota(jnp.int32, (1, KVW), 1) // HEAD_DIM
    qh = q_ref[:, h * KVW:(h + 1) * KVW]
    qs = jnp.concatenate([jnp.where(lane == g, qh, 0.0) for g in range(Q_PER_KV)], axis=0).astype(BF16)
    hsel = lane == h
    kx = [_fold4(jnp.where(hsel, r[...], 0.0)).astype(BF16) for r in (k0_ref, kp_ref, kc_ref)]
    scale = HEAD_DIM ** -0.5
    s0, sp, sc = [_dot_nt(qs, k) * scale for k in kx]
    qi = lax.broadcasted_iota(jnp.int32, (Q_PER_KV * QB, QB), 0) % QB
    kj = lax.broadcasted_iota(jnp.int32, (Q_PER_KV * QB, QB), 1)
    s0 = jnp.where(kj >= META0, s0, NEG_INF)
    sp = jnp.where((kj > qi) & (i >= 2), sp, NEG_INF)
    sc = jnp.where(kj <= qi, sc, NEG_INF)
    rowg = lax.broadcasted_iota(jnp.int32, (Q_PER_KV * QB, 1), 0) // QB
    sink = jnp.zeros((Q_PER_KV * QB, 1), F32)
    for g in range(Q_PER_KV):
        sink = jnp.where(rowg == g, sink_ref[0, h * Q_PER_KV + g], sink)
    m = jnp.maximum(jnp.maximum(jnp.max(s0, axis=1, keepdims=True), jnp.max(sp, axis=1, keepdims=True)),
                    jnp.maximum(jnp.max(sc, axis=1, keepdims=True), sink))
    p0, pp, pc, ps = jnp.exp(s0 - m), jnp.exp(sp - m), jnp.exp(sc - m), jnp.exp(sink - m)
    den = jnp.sum(p0, axis=1, keepdims=True) + jnp.sum(pp, axis=1, keepdims=True) + jnp.sum(pc, axis=1, keepdims=True) + ps
    return qs, kx, (p0, pp, pc), ps, den, lane, hsel


def _unstack(x, lane):
    out = jnp.where(lane == 0, x[0:QB], 0.0)
    for g in range(1, Q_PER_KV):
        out = out + jnp.where(lane == g, x[g * QB:(g + 1) * QB], 0.0)
    return out


def _attn_specs(nb, d):
    qspec = pl.BlockSpec((None, QB, d), lambda b, i: (b, i, 0))
    k0 = pl.BlockSpec((None, QB, KVW), lambda b, i: (b, 0, 0))
    kp = pl.BlockSpec((None, QB, KVW), lambda b, i: (b, jnp.maximum(i - 1, 0), 0))
    kc = pl.BlockSpec((None, QB, KVW), lambda b, i: (b, i, 0))
    v0 = pl.BlockSpec((None, QB, KVW), lambda b, i: (b, 0, 1))
    vp = pl.BlockSpec((None, QB, KVW), lambda b, i: (b, jnp.maximum(i - 1, 0), 1))
    vc = pl.BlockSpec((None, QB, KVW), lambda b, i: (b, i, 1))
    sink = pl.BlockSpec(memory_space=pltpu.SMEM)
    return qspec, [k0, kp, kc], [v0, vp, vc], sink


def _attn_fwd(q, k, kv, sinks):
    bsz, lp, d = q.shape
    nb = lp // QB
    qspec, kspecs, vspecs, sspec = _attn_specs(nb, d)

    def kern(q_ref, k0_ref, kp_ref, kc_ref, v0_ref, vp_ref, vc_ref, sink_ref, o_ref):
        i = pl.program_id(1)
        for h in range(N_KV_HEADS):
            qs, kx, ps3, psink, den, lane, hsel = _attn_scores(i, q_ref, k0_ref, kp_ref, kc_ref, sink_ref, h)
            vx = [_fold4(jnp.where(hsel, r[...], 0.0)).astype(BF16) for r in (v0_ref, vp_ref, vc_ref)]
            o = _dot(ps3[0], vx[0]) + _dot(ps3[1], vx[1]) + _dot(ps3[2], vx[2])
            o_ref[:, h * KVW:(h + 1) * KVW] = _unstack(o / den, lane)

    return pl.pallas_call(
        kern, name="attn_fwd", grid=(bsz, nb),
        in_specs=[qspec] + kspecs + vspecs + [sspec],
        out_specs=qspec, out_shape=jax.ShapeDtypeStruct((bsz, lp, d), F32),
        compiler_params=_cparams(("arbitrary", "arbitrary")),
    )(q, k, k, k, kv, kv, kv, sinks)


def _attn_bwd(q, k, kv, sinks, o, do):
    bsz, lp, d = q.shape
    nb = lp // QB
    qspec, kspecs, vspecs, sspec = _attn_specs(nb, d)
    full = pl.BlockSpec((None, lp, KVW), lambda b, i: (b, 0, 0))

    def kern(q_ref, k0_ref, kp_ref, kc_ref, v0_ref, vp_ref, vc_ref, sink_ref, o_ref, do_ref,
             dq_ref, dk_ref, dv_ref, ds_ref):
        b, i = pl.program_id(0), pl.program_id(1)

        @pl.when(i == 0)
        def _():
            dk_ref[...] = jnp.zeros_like(dk_ref)
            dv_ref[...] = jnp.zeros_like(dv_ref)

        @pl.when((b == 0) & (i == 0))
        def _():
            ds_ref[...] = jnp.zeros_like(ds_ref)

        lane128 = lax.broadcasted_iota(jnp.int32, (1, LANES), 1)
        rowg = lax.broadcasted_iota(jnp.int32, (Q_PER_KV * QB, 1), 0) // QB
        dk_acc = [jnp.zeros((QB, KVW), F32) for _ in range(3)]
        dv_acc = [jnp.zeros((QB, KVW), F32) for _ in range(3)]
        dsink = jnp.zeros((1, LANES), F32)
        for h in range(N_KV_HEADS):
            qs, kx, ps3, psink, den, lane, hsel = _attn_scores(i, q_ref, k0_ref, kp_ref, kc_ref, sink_ref, h)
            vx = [_fold4(jnp.where(hsel, r[...], 0.0)).astype(BF16) for r in (v0_ref, vp_ref, vc_ref)]
            sl = slice(h * KVW, (h + 1) * KVW)
            doh, oh = do_ref[:, sl], o_ref[:, sl]
            dos = jnp.concatenate([jnp.where(lane == g, doh, 0.0) for g in range(Q_PER_KV)], axis=0)
            ost = jnp.concatenate([jnp.where(lane == g, oh, 0.0) for g in range(Q_PER_KV)], axis=0)
            delta = jnp.sum(dos * ost, axis=1, keepdims=True)
            inv = 1.0 / den
            dosb = dos.astype(BF16)
            dqs = jnp.zeros((Q_PER_KV * QB, KVW), F32)
            for n in range(3):
                pn = ps3[n] * inv
                ds = pn * (_dot_nt(dosb, vx[n]) - delta) * (HEAD_DIM ** -0.5)
                dqs = dqs + _dot(ds, kx[n])
                dk_acc[n] = dk_acc[n] + jnp.where(hsel, _fold4(_dot_tn(ds, qs)), 0.0)
                dv_acc[n] = dv_acc[n] + jnp.where(hsel, _fold4(_dot_tn(pn, dosb)), 0.0)
            dq_ref[:, sl] = _unstack(dqs, lane)
            dsk = -(psink * inv) * delta
            for g in range(Q_PER_KV):
                val = jnp.sum(jnp.where(rowg == g, dsk, 0.0), axis=0, keepdims=True)
                dsink = dsink + jnp.where(lane128 == h * Q_PER_KV + g, val, 0.0)
        ds_ref[...] += dsink
        r0 = pl.ds(0, QB)
        rp = pl.ds(pl.multiple_of(jnp.maximum(i - 1, 0) * QB, QB), QB)
        rc = pl.ds(pl.multiple_of(i * QB, QB), QB)
        for rows, n in ((r0, 0), (rp, 1), (rc, 2)):
            dk_ref[rows, :] += dk_acc[n]
            dv_ref[rows, :] += dv_acc[n]

    return pl.pallas_call(
        kern, name="attn_bwd", grid=(bsz, nb),
        in_specs=[qspec] + kspecs + vspecs + [sspec, qspec, qspec],
        out_specs=[qspec, full, full, pl.BlockSpec((1, LANES), lambda b, i: (0, 0))],
        out_shape=[jax.ShapeDtypeStruct((bsz, lp, d), F32), jax.ShapeDtypeStruct((bsz, lp, KVW), F32),
                   jax.ShapeDtypeStruct((bsz, lp, KVW), F32), jax.ShapeDtypeStruct((1, LANES), F32)],
        compiler_params=_cparams(("arbitrary", "arbitrary")),
    )(q, k, k, k, kv, kv, kv, sinks, o, do)


def _concat_cols(name, a, b):
    rows = a.shape[0]

    def body(step, a_ref, b_ref, o_ref):
        o_ref[...] = jnp.concatenate([a_ref[...], b_ref[...]], axis=1)

    return _rowcall(name, body, rows, [a, b], [], [(a.shape[1] + b.shape[1], F32)])[0]


def _adamw(name, w, m, v, parts):
    rows, wd = w.shape
    n = parts.shape[0]
    tm = _row_tile(rows)

    def kern(w_ref, m_ref, v_ref, p_ref, g_ref, d_ref, m2_ref, v2_ref):
        g = p_ref[0].astype(F32)
        for k in range(1, n):
            g = g + p_ref[k].astype(F32)
        m2 = ADAM_B1 * m_ref[...] + (1.0 - ADAM_B1) * g
        v2 = ADAM_B2 * v_ref[...] + (1.0 - ADAM_B2) * (g * g)
        mh = m2 / (1.0 - ADAM_B1 ** ADAM_STEP)
        vh = v2 / (1.0 - ADAM_B2 ** ADAM_STEP)
        g_ref[...] = g
        d_ref[...] = -ADAM_LR * (mh / (jnp.sqrt(vh) + ADAM_EPS) + ADAM_WD * w_ref[...])
        m2_ref[...] = m2
        v2_ref[...] = v2

    spec = pl.BlockSpec((tm, wd), lambda i: (i, 0))
    sd = jax.ShapeDtypeStruct((rows, wd), F32)
    return pl.pallas_call(
        kern, name=name, grid=(rows // tm,),
        in_specs=[spec, spec, spec, pl.BlockSpec((n, tm, wd), lambda i: (0, i, 0))],
        out_specs=[spec] * 4, out_shape=[sd] * 4,
        compiler_params=_cparams(("arbitrary",)),
    )(w, m, v, parts)


def _pair_sum(name, parts, theirs, my_c):
    n, _, rows, wd = parts.shape
    tm = _row_tile(rows)

    def kern(c_ref, a_ref, b_ref, o_ref):
        o_ref[...] = (a_ref[...] + b_ref[...]).astype(BF16)

    return pl.pallas_call(
        kern, name=name,
        grid_spec=pltpu.PrefetchScalarGridSpec(
            num_scalar_prefetch=1, grid=(n, rows // tm),
            in_specs=[pl.BlockSpec((None, None, tm, wd), lambda k, i, c: (k, c[0], i, 0)),
                      pl.BlockSpec((None, tm, wd), lambda k, i, c: (k, i, 0))],
            out_specs=pl.BlockSpec((None, tm, wd), lambda k, i, c: (k, i, 0))),
        out_shape=jax.ShapeDtypeStruct((n, rows, wd), BF16), compiler_params=_cparams(("arbitrary", "arbitrary")),
    )(my_c, parts, theirs)


MESH = pl.DeviceIdType.MESH
ANY = pl.BlockSpec(memory_space=pl.ANY)


def _allgather(name, shards):
    n = len(shards)

    def body(*refs):
        x_refs, out_refs = refs[:n], refs[n:2 * n]
        send_sems, recv_sems, local_sems = refs[2 * n:]
        x, y, c = lax.axis_index("x"), lax.axis_index("y"), lax.axis_index("c")
        me, sibling = (x, y, c), (x, y, 1 - c)
        chips = [(1 - x, y), (x, 1 - y), (1 - x, 1 - y)]

        def copy(a, k, block, to, own=False):
            px, py, pc = block
            slot = out_refs[a].at[4 * px + 2 * py + pc]
            return pltpu.make_async_remote_copy(
                src_ref=x_refs[a] if own else slot, dst_ref=slot,
                send_sem=send_sems.at[a, k], recv_sem=recv_sems.at[a, k], device_id=to, device_id_type=MESH)

        mine = [pltpu.make_async_copy(x_refs[a], out_refs[a].at[4 * x + 2 * y + c], local_sems.at[a]) for a in range(n)]
        for cp in mine:
            cp.start()
        first = []
        for a in range(n):
            first.append(copy(a, 0, me, sibling, own=True))
            first += [copy(a, 1 + j, me, (*chip, c), own=True) for j, chip in enumerate(chips)]
        for cp in first:
            cp.start()
        passed = []
        for j, chip in enumerate(chips):
            for a in range(n):
                copy(a, 1 + j, (*chip, c), me).wait_recv()
                fwd = copy(a, 4 + j, (*chip, c), sibling)
                fwd.start()
                passed.append(fwd)
        for a in range(n):
            copy(a, 0, sibling, me).wait_recv()
            for j, chip in enumerate(chips):
                copy(a, 4 + j, (*chip, 1 - c), me).wait_recv()
        for cp in first + passed:
            cp.wait_send()
        for cp in mine:
            cp.wait()

    return pl.pallas_call(
        body, name=name, out_shape=[jax.ShapeDtypeStruct((N_DEV,) + s.shape, s.dtype) for s in shards],
        in_specs=[ANY] * n, out_specs=[ANY] * n,
        scratch_shapes=[pltpu.SemaphoreType.DMA((n, 7)), pltpu.SemaphoreType.DMA((n, 7)), pltpu.SemaphoreType.DMA((n,))],
    )(*shards)


def _sibling_swap(name, parts):
    n = len(parts)

    def body(*refs):
        p_refs, out_refs = refs[:n], refs[n:2 * n]
        send_sems, recv_sems = refs[2 * n:]
        x, y, c = lax.axis_index("x"), lax.axis_index("y"), lax.axis_index("c")
        copies = [pltpu.make_async_remote_copy(
            src_ref=p_refs[a].at[:, 1 - c], dst_ref=out_refs[a], send_sem=send_sems.at[a], recv_sem=recv_sems.at[a],
            device_id=(x, y, 1 - c), device_id_type=MESH) for a in range(n)]
        for cp in copies:
            cp.start()
        for cp in copies:
            cp.wait()

    return pl.pallas_call(
        body, name=name,
        out_shape=[jax.ShapeDtypeStruct((p.shape[0],) + p.shape[2:], p.dtype) for p in parts],
        in_specs=[ANY] * n, out_specs=[ANY] * n,
        scratch_shapes=[pltpu.SemaphoreType.DMA((n,)), pltpu.SemaphoreType.DMA((n,))],
    )(*parts)


def _chip_scatter(name, sums):
    n = len(sums)

    def body(*refs):
        s_refs, out_refs = refs[:n], refs[n:2 * n]
        send_sems, recv_sems, local_sems = refs[2 * n:]
        x, y, c = lax.axis_index("x"), lax.axis_index("y"), lax.axis_index("c")
        mychip = 2 * x + y
        chips = [(1 - x, y), (x, 1 - y), (1 - x, 1 - y)]
        mine = [pltpu.make_async_copy(s_refs[a].at[mychip], out_refs[a].at[mychip], local_sems.at[a]) for a in range(n)]
        for cp in mine:
            cp.start()

        def copy(a, j, block):
            px, py = chips[j]
            return pltpu.make_async_remote_copy(
                src_ref=s_refs[a].at[2 * px + py], dst_ref=out_refs[a].at[block],
                send_sem=send_sems.at[a, j], recv_sem=recv_sems.at[a, j], device_id=(px, py, c), device_id_type=MESH)

        copies = [copy(a, j, mychip) for j in range(3) for a in range(n)]
        for cp in copies:
            cp.start()
        for j, (px, py) in enumerate(chips):
            for a in range(n):
                copy(a, j, 2 * px + py).wait_recv()
        for cp in copies:
            cp.wait_send()
        for cp in mine:
            cp.wait()

    return pl.pallas_call(
        body, name=name, out_shape=[jax.ShapeDtypeStruct(s.shape, s.dtype) for s in sums],
        in_specs=[ANY] * n, out_specs=[ANY] * n,
        scratch_shapes=[pltpu.SemaphoreType.DMA((n, 3)), pltpu.SemaphoreType.DMA((n, 3)), pltpu.SemaphoreType.DMA((n,))],
    )(*sums)


BIG = (("ffn1_w_gate_up", 2), ("ffn1_w_down", 1), ("ffn2_w_gate_up", 2), ("ffn2_w_down", 1), ("ssm_w_in", 1),
       ("ssm_w_out", 2), ("w_kv", 0), ("attn_w_q", 1), ("attn_w_o", 1))
SMALL = ("ffn1_norm", "mix_norm", "ffn2_norm", "ssm_lambda_re", "ssm_lambda_im", "ssm_b_re", "ssm_b_im",
         "ssm_c_re", "ssm_c_im", "ssm_log_step", "kv_norm", "k_norm", "q_norm", "attn_sinks")
COLS = (("meta_tokens", 1), ("ssm_d", 1))
WEIGHTS = ("meta_tokens", "ffn1_norm", "ffn1_w_gate_up", "ffn1_w_down", "mix_norm", "ffn2_norm", "ffn2_w_gate_up",
           "ffn2_w_down", "ssm_w_in", "ssm_lambda_re", "ssm_lambda_im", "ssm_b_re", "ssm_b_im", "ssm_c_re",
           "ssm_c_im", "ssm_log_step", "ssm_d", "ssm_w_out", "kv_norm", "w_kv", "k_norm", "attn_w_q", "q_norm",
           "attn_sinks", "attn_w_o")


def _rows_of(a, width):
    n = math.prod(a.shape)
    if n % width == 0:
        r = a.reshape(n // width, width)
    else:
        assert n < width
        r = jnp.pad(a.reshape(1, n), ((0, 0), (0, width - n)))
    return jnp.pad(r, ((0, (-r.shape[0]) % 8), (0, 0)))


def _pack_small(arrs, width):
    return jnp.concatenate([_rows_of(a.astype(F32), width) for a in arrs], axis=0)


def _unpack_small(buf, shapes, width):
    out, off = [], 0
    for shp in shapes:
        n = math.prod(shp)
        r = max(n // width, 1)
        out.append(buf[off:off + r].reshape(shp) if n % width == 0 else buf[off, :n].reshape(shp))
        off += r + (-r) % 8
    return out


def _shape2d(shp):
    return (math.prod(shp[:-1]), shp[-1])


def _unshard(g, axis):
    g = jnp.moveaxis(g, 0, axis)
    shp = g.shape
    return g.reshape(shp[:axis] + (shp[axis] * shp[axis + 1],) + shp[axis + 2:])


def _shard(full, axis):
    shp = full.shape
    g = full.reshape(shp[:axis] + (N_DEV, shp[axis] // N_DEV) + shp[axis + 1:])
    return jnp.moveaxis(g, axis, 0)


def _blockdiag(blocks):
    g, r, c = blocks.shape
    eye = jnp.eye(g, dtype=blocks.dtype)
    return (eye[:, None, :, None] * blocks[:, :, None, :]).reshape(g * r, g * c)


def _diagblocks(full, g):
    r, c = full.shape[0] // g, full.shape[1] // g
    f = full.reshape(g, r, g, c)
    idx = jnp.arange(g)
    return f[idx, :, idx, :]


def kernel(x, meta_tokens, ffn1_norm, ffn1_w_gate_up, ffn1_w_down, mix_norm, ffn2_norm, ffn2_w_gate_up, ffn2_w_down, ssm_w_in, ssm_lambda_re, ssm_lambda_im, ssm_b_re, ssm_b_im, ssm_c_re, ssm_c_im, ssm_log_step, ssm_d, ssm_w_out, kv_norm, w_kv, k_norm, attn_w_q, q_norm, attn_sinks, attn_w_o, loss_target, m_meta_tokens, m_ffn1_norm, m_ffn1_w_gate_up, m_ffn1_w_down, m_mix_norm, m_ffn2_norm, m_ffn2_w_gate_up, m_ffn2_w_down, m_ssm_w_in, m_ssm_lambda_re, m_ssm_lambda_im, m_ssm_b_re, m_ssm_b_im, m_ssm_c_re, m_ssm_c_im, m_ssm_log_step, m_ssm_d, m_ssm_w_out, m_kv_norm, m_w_kv, m_k_norm, m_attn_w_q, m_q_norm, m_attn_sinks, m_attn_w_o, v_meta_tokens, v_ffn1_norm, v_ffn1_w_gate_up, v_ffn1_w_down, v_mix_norm, v_ffn2_norm, v_ffn2_w_gate_up, v_ffn2_w_down, v_ssm_w_in, v_ssm_lambda_re, v_ssm_lambda_im, v_ssm_b_re, v_ssm_b_im, v_ssm_c_re, v_ssm_c_im, v_ssm_log_step, v_ssm_d, v_ssm_w_out, v_kv_norm, v_w_kv, v_k_norm, v_attn_w_q, v_q_norm, v_attn_sinks, v_attn_w_o):
    args = dict(locals())
    W = {n: args[n] for n in WEIGHTS}
    M = {n: args["m_" + n] for n in WEIGHTS}
    V = {n: args["v_" + n] for n in WEIGHTS}
    my_x, my_y, my_c = (lax.axis_index(a) for a in MESH_AXES)
    my_dev = 4 * my_x + 2 * my_y + my_c

    big_names = [n for n, _ in BIG]
    s2d = {n: _shape2d(W[n].shape) for n in big_names}
    widths = sorted({s2d[n][1] for n in big_names})
    by_width = [[n for n in big_names if s2d[n][1] == wd] for wd in widths]
    groups = [jnp.concatenate([W[n].reshape(s2d[n]).astype(BF16) for n in names], axis=0) for names in by_width]
    col_w = W["meta_tokens"].shape[1]
    cols = _pack_small([W["meta_tokens"], W["ssm_d"]], col_w)
    gathered = _allgather("gather_weights", groups + [cols])
    full = {}
    for names, g in zip(by_width, gathered):
        off = 0
        for n in names:
            r = s2d[n][0]
            full[n] = _unshard(g[:, off:off + r].reshape((N_DEV,) + W[n].shape), dict(BIG)[n])
            off += r
    gcols = gathered[-1]
    full["meta_tokens"] = _unshard(gcols[:, :N_META], 1)
    full["ssm_d"] = _unshard(gcols[:, N_META:N_META + 1, :W["ssm_d"].shape[1]], 1)

    grads = _local_step(x, loss_target, {**W, **full})
    loss = lax.psum(grads.pop("loss"), MESH_AXES)
    grad_x = grads.pop("x")

    parts = [grads[n].reshape((4, 2) + s2d[n]) for n in big_names]
    theirs = _sibling_swap("grad_sibling_swap", parts)
    c_arr = my_c.astype(jnp.int32).reshape(1)
    sums = [_pair_sum("pair_sum_" + n, p, t, c_arr) for n, p, t in zip(big_names, parts, theirs)]
    summed = _chip_scatter("grad_chip_scatter", sums)
    outs = [{}, {}, {}, {}]
    for n, sm in zip(big_names, summed):
        r4 = _adamw("adamw_" + n, W[n].reshape(s2d[n]), M[n].reshape(s2d[n]), V[n].reshape(s2d[n]), sm)
        for k in range(4):
            outs[k][n] = r4[k].reshape(W[n].shape)

    small_names = list(SMALL) + [n for n, _ in COLS]
    small_shapes = [grads[n].shape for n in small_names]
    small_parts = _allgather("gather_small_grads", [_pack_small([grads[n] for n in small_names], PACK_W)])[0]
    zero_cols = [jnp.zeros(grads[n].shape, F32) for n, _ in COLS]
    packs = lambda d: _pack_small([d[n] for n in SMALL] + zero_cols, PACK_W)
    r4 = _adamw("adamw_small", packs(W), packs(M), packs(V), small_parts)
    gsmall = None
    for k in range(4):
        un = dict(zip(small_names, _unpack_small(r4[k], small_shapes, PACK_W)))
        gsmall = un if k == 0 else gsmall
        outs[k].update({n: un[n] for n in SMALL})
    col_g = [lax.dynamic_slice_in_dim(gsmall[n], my_dev * W[n].shape[1], W[n].shape[1], axis=1) for n, _ in COLS]
    packc = lambda d: _pack_small([d[n] for n, _ in COLS], col_w)
    r4 = _adamw("adamw_cols", packc(W), packc(M), packc(V), _pack_small(col_g, col_w)[None])
    col_shapes = [W[n].shape for n, _ in COLS]
    for k in range(4):
        outs[k].update(dict(zip([n for n, _ in COLS], _unpack_small(r4[k], col_shapes, col_w))))

    res = [[outs[k][n] for n in WEIGHTS] for k in range(4)]
    return (loss, grad_x, *res[0], *res[1], *res[2], *res[3])


def _local_step(x, target, P):
    bsz, seq, d = x.shape
    lp = seq + PAD
    rows = bsz * lp
    depth = P["ffn1_norm"].shape[0]
    assert depth == 2
    f = P["ffn1_w_down"].shape[1]
    bf = lambda a: a.astype(BF16)
    row = lambda a: a.reshape(1, -1)

    pos = (jnp.arange(lp, dtype=F32) - float(META0))[:, None]
    half = HEAD_DIM // 2
    freqs = ROPE_THETA ** (-jnp.arange(0, half, dtype=F32) * 2.0 / HEAD_DIM)
    ang = pos * freqs[None, :]
    cos_t = jnp.tile(jnp.cos(ang), (1, LANES // half))
    sin_t = jnp.tile(jnp.concatenate([-jnp.sin(ang), jnp.sin(ang)], axis=1), (1, LANES // HEAD_DIM))
    gi = jnp.arange(LANES) // HEAD_DIM
    gmat = jnp.where(gi[:, None] == gi[None, :], 1.0 / HEAD_DIM, 0.0).astype(BF16)

    g_n, c_n, p_n = P["ssm_lambda_re"].shape[1], SSM_GROUP, SSM_STATE
    ns = g_n * p_n
    lr = P["ssm_lambda_re"][0].reshape(g_n, 1, p_n)
    li = P["ssm_lambda_im"][0].reshape(g_n, 1, p_n)
    ls = P["ssm_log_step"][0].reshape(g_n, 1, 1)
    brt = P["ssm_b_re"][0].transpose(0, 2, 1)
    bit = P["ssm_b_im"][0].transpose(0, 2, 1)
    ar, ai, bbr, bbi = _s5_params_fwd(lr, li, ls, brt, bit)
    a2 = jnp.concatenate([ar.reshape(1, ns), ai.reshape(1, ns)], axis=0)
    bfull = jnp.concatenate([_blockdiag(bbr), _blockdiag(bbi)], axis=1)
    cre_t = P["ssm_c_re"][0].transpose(0, 2, 1)
    cim_t = P["ssm_c_im"][0].transpose(0, 2, 1)
    cfull = jnp.concatenate([_blockdiag(cre_t), -_blockdiag(cim_t)], axis=0)
    dvec = P["ssm_d"].reshape(1, -1)

    ffn = lambda which, l: (row(P[which + "_norm"][l]), bf(P[which + "_w_gate_up"][l]), bf(P[which + "_w_down"][l]))
    w_in, w_out = bf(P["ssm_w_in"][0]), bf(P["ssm_w_out"][0])
    w_kv, w_q, w_o = bf(P["w_kv"]), bf(P["attn_w_q"][0]), bf(P["attn_w_o"][0])
    mix0, mix1, kvn = row(P["mix_norm"][0]), row(P["mix_norm"][1]), row(P["kv_norm"])
    kgain = jnp.tile(P["k_norm"].reshape(1, HEAD_DIM), (1, KVW // HEAD_DIM))
    qgain = jnp.tile(P["q_norm"].reshape(1, HEAD_DIM), (1, d // HEAD_DIM))
    sinks = P["attn_sinks"].reshape(1, -1)

    h0 = _embed(x, P["meta_tokens"]).reshape(rows, d)
    h1, ab_f1_0 = _ffn_fwd("ffn1_0_fwd", h0, *ffn("ffn1", 0))
    u = _proj_fwd("ssm_in_fwd", h1, mix0, w_in)
    y, xs = _s5_scan_fwd(u, bf(bfull), bf(cfull), a2, dvec, bsz)
    h2 = _glu_fwd(y, h1, w_out)
    h3, ab_f2_0 = _ffn_fwd("ffn2_0_fwd", h2, *ffn("ffn2", 0))
    kv = _proj_fwd("kv_fwd", h3, kvn, w_kv)
    k = _headrope_fwd("k_rope_fwd", kv, KVW, kgain, cos_t, sin_t, gmat, lp)
    h4, ab_f1_1 = _ffn_fwd("ffn1_1_fwd", h3, *ffn("ffn1", 1))
    q_raw = _proj_fwd("q_fwd", h4, mix1, w_q)
    q = _headrope_fwd("q_rope_fwd", q_raw, d, qgain, cos_t, sin_t, gmat, lp)
    r3 = lambda a: a.reshape(bsz, lp, a.shape[-1])
    o = _attn_fwd(r3(q), r3(k), r3(kv), sinks).reshape(rows, d)
    h5 = _lin_res_fwd("attn_out_fwd", o, w_o, h4)
    h6, ab_f2_1 = _ffn_fwd("ffn2_1_fwd", h5, *ffn("ffn2", 1))
    loss, dh6 = _loss(r3(h6), target)
    dh6 = dh6.reshape(rows, d)

    G = {"loss": loss[0, 0]}

    def ffn_back(name, which, l, h, ab, dout):
        g, wgu, wd = ffn(which, l)
        dh, hn, dab, act, dg = _ffn_bwd(name, h, ab, dout, g, wgu, wd)
        dwgu[which, l] = _mm_tn(name + "_wgu", hn, dab)
        G[which + "_w_down"] = _mm_tn_slots(name + "_wd", act, dout, 0.5, l, depth, G.get(which + "_w_down"))
        return dh, dg

    dwgu = {}

    dh5, dg_f2_1 = ffn_back("ffn2_1_bwd", "ffn2", 1, h5, ab_f2_1, dh6)
    do, dw_o = _lin_bwd("attn_out_bwd", o, w_o, dh5)
    dq, dk, dv, dsinks = _attn_bwd(r3(q), r3(k), r3(kv), sinks, r3(o), r3(do))
    dq_raw, dqg = _headrope_bwd("q_rope_bwd", q_raw, d, dq.reshape(rows, d), qgain, cos_t, sin_t, gmat, lp)
    dh4, dg_mix1, dw_q = _proj_bwd("q_bwd", h4, mix1, w_q, dq_raw, dh5)
    dh3, dg_f1_1 = ffn_back("ffn1_1_bwd", "ffn1", 1, h3, ab_f1_1, dh4)
    dk_raw, dkg = _headrope_bwd("k_rope_bwd", kv, KVW, dk.reshape(rows, KVW), kgain, cos_t, sin_t, gmat, lp)
    dkv = _concat_cols("dkv_concat", dk_raw, dv.reshape(rows, KVW))
    dh3, dg_kvn, dw_kv = _proj_bwd("kv_bwd", h3, kvn, w_kv, dkv, dh3)
    dh2, dg_f2_0 = ffn_back("ffn2_0_bwd", "ffn2", 0, h2, ab_f2_0, dh3)
    dy, dw_out = _glu_bwd(y, dh2, w_out)
    ctfull = jnp.concatenate([_blockdiag(P["ssm_c_re"][0]), -_blockdiag(P["ssm_c_im"][0])], axis=1)
    btfull = jnp.concatenate([_blockdiag(bbr.transpose(0, 2, 1)), _blockdiag(bbi.transpose(0, 2, 1))], axis=0)
    du, gx, da, dd = _s5_scan_bwd(dy, u, xs, bf(ctfull), bf(btfull), a2, dvec, bsz)
    dbfull = _mm_tn("ssm_db", u, gx)
    dcfull = _mm_tn("ssm_dc", xs, dy)
    dh1, dg_mix0, dw_in = _proj_bwd("ssm_in_bwd", h1, mix0, w_in, du, dh2)
    dh0, dg_f1_0 = ffn_back("ffn1_0_bwd", "ffn1", 0, h0, ab_f1_0, dh1)

    dbbr = _diagblocks(dbfull[:, :ns], g_n)
    dbbi = _diagblocks(dbfull[:, ns:], g_n)
    dlr, dli, dls, dbrt, dbit = _s5_params_bwd(lr, li, ls, brt, bit, da[:, :ns].reshape(g_n, 1, p_n),
                                               da[:, ns:].reshape(g_n, 1, p_n), dbbr, dbbi)
    dh0 = r3(dh0)
    G["x"] = dh0[:, PAD:, :]
    G["meta_tokens"] = _meta_sum(dh0)
    G["ffn1_norm"] = jnp.concatenate([dg_f1_0, dg_f1_1], axis=0)
    G["ffn2_norm"] = jnp.concatenate([dg_f2_0, dg_f2_1], axis=0)
    G["mix_norm"] = jnp.concatenate([dg_mix0, dg_mix1], axis=0)
    for which in ("ffn1", "ffn2"):
        G[which + "_w_gate_up"] = _shard(jnp.stack([dwgu[which, l] for l in range(depth)]), 2)
    G["ssm_w_in"] = dw_in
    G["ssm_lambda_re"] = dlr.reshape(1, g_n, p_n)
    G["ssm_lambda_im"] = dli.reshape(1, g_n, p_n)
    G["ssm_log_step"] = dls.reshape(1, g_n)
    G["ssm_b_re"] = dbrt.transpose(0, 2, 1)[None]
    G["ssm_b_im"] = dbit.transpose(0, 2, 1)[None]
    G["ssm_c_re"] = _diagblocks(dcfull[:ns], g_n).transpose(0, 2, 1)[None]
    G["ssm_c_im"] = -_diagblocks(dcfull[ns:], g_n).transpose(0, 2, 1)[None]
    G["ssm_d"] = dd
    G["ssm_w_out"] = dw_out
    G["kv_norm"] = dg_kvn.reshape(-1)
    G["w_kv"] = dw_kv
    G["k_norm"] = dkg[0, :HEAD_DIM]
    G["attn_w_q"] = dw_q[None]
    G["q_norm"] = dqg[:, :HEAD_DIM]
    G["attn_sinks"] = dsinks[:, :N_KV_HEADS * Q_PER_KV]
    G["attn_w_o"] = dw_o[None]
    return G
```

```python
import functools
import math

import jax
import jax.numpy as jnp
from jax import lax
from jax.experimental import pallas as pl
from jax.experimental.pallas import tpu as pltpu

F32 = jnp.float32
BF16 = jnp.bfloat16

N_META = 16
PAD = 128
META0 = PAD - N_META
HEAD_DIM = 64
N_KV_HEADS = 4
Q_PER_KV = 4
SSM_GROUP = 16
SSM_STATE = 64
EPS = 1e-6
NEG_INF = -1e30
ROPE_THETA = 10000.0
ADAM_LR, ADAM_B1, ADAM_B2, ADAM_EPS, ADAM_WD, ADAM_STEP = 0.001, 0.9, 0.999, 1e-08, 0.01, 10
LANES = 128
PACK_W = 1024
VMEM_LIMIT = 56 * 1024 * 1024
MESH_AXES = ("x", "y", "c")
N_DEV = 8


def _cparams(sem=None):
    return pltpu.CompilerParams(dimension_semantics=sem, vmem_limit_bytes=VMEM_LIMIT)


def _row_tile(rows):
    for tm in (384, 256, 128, 64, 32, 16, 8):
        if rows % tm == 0:
            return tm
    raise ValueError(rows)


TN_BUDGET = 44 * 1024 * 1024


def _tn_tile(rows, a, b, k1, tn):
    sa, sb = a.dtype.itemsize, b.dtype.itemsize
    for tm in (2112, 1056, 768, 528, 384, 256, 128, 64, 32, 16, 8):
        need = 2 * tm * (k1 * sa + tn * sb) + 3 * k1 * tn * 4 + tm * (k1 + tn) * 2
        if rows % tm == 0 and need <= TN_BUDGET:
            return tm
    raise ValueError(rows)


def _dot(a, b):
    return jnp.dot(a.astype(BF16), b.astype(BF16), preferred_element_type=F32)


def _dot_nt(a, b):
    return lax.dot_general(a.astype(BF16), b.astype(BF16), (((1,), (1,)), ((), ())), preferred_element_type=F32)


def _dot_tn(a, b):
    return lax.dot_general(a.astype(BF16), b.astype(BF16), (((0,), (0,)), ((), ())), preferred_element_type=F32)


def _rms(x, g):
    rstd = lax.rsqrt(jnp.mean(x * x, axis=-1, keepdims=True) + EPS)
    y = x * rstd
    return y * g, y, rstd


def _rms_bwd(dhn, y, rstd, g):
    dyn = dhn * g
    dx = rstd * (dyn - y * jnp.mean(dyn * y, axis=-1, keepdims=True))
    return dx, jnp.sum(dhn * y, axis=0, keepdims=True)


def _sigmoid(x):
    return 1.0 / (1.0 + jnp.exp(-x))


_GELU_C = math.sqrt(2.0 / math.pi)


def _gelu(y):
    t = jnp.tanh(_GELU_C * (y + 0.044715 * y * y * y))
    return 0.5 * y * (1.0 + t), t


def _gelu_grad(y, t):
    return 0.5 * (1.0 + t) + 0.5 * y * (1.0 - t * t) * _GELU_C * (1.0 + 3.0 * 0.044715 * y * y)


class _Rider:
    def __init__(self, ins, outs, sems, start, mid, finish):
        self.ins, self.outs, self.sems, self.start, self.mid, self.finish = ins, outs, sems, start, mid, finish


def _run_rider(name, rider):
    def kern(*refs):
        ni, no = len(rider.ins), len(rider.outs)
        parts = refs[:ni], refs[ni:ni + no], refs[ni + no:]
        rider.start(*parts)
        if rider.mid is not None:
            rider.mid(*parts)
        rider.finish(*parts)

    return pl.pallas_call(
        kern, name=name, out_shape=list(rider.outs), in_specs=[ANY] * len(rider.ins),
        out_specs=[ANY] * len(rider.outs), scratch_shapes=list(rider.sems),
    )(*rider.ins)


def _rowcall(name, body, rows, row_ins, const_ins, row_outs, acc_outs=(), tm=None, row_in_maps=None, rider=None):
    tm = tm or _row_tile(rows)
    steps = rows // tm
    in_specs = []
    for k, a in enumerate(row_ins):
        if row_in_maps is not None and row_in_maps[k] is not None:
            in_specs.append(pl.BlockSpec(*row_in_maps[k]))
        else:
            in_specs.append(pl.BlockSpec((tm, a.shape[1]), lambda i: (i, 0)))
    for a in const_ins:
        in_specs.append(pl.BlockSpec(a.shape, lambda i, nd=a.ndim: (0,) * nd, pipeline_mode=pl.Buffered(1)))
    out_shape, out_specs = [], []
    for w, dt in row_outs:
        out_shape.append(jax.ShapeDtypeStruct((rows, w), dt))
        out_specs.append(pl.BlockSpec((tm, w), lambda i: (i, 0)))
    for shp, dt in acc_outs:
        out_shape.append(jax.ShapeDtypeStruct(shp, dt))
        out_specs.append(pl.BlockSpec(shp, lambda i, nd=len(shp): (0,) * nd))

    if rider is None:
        def kern(*refs):
            body(pl.program_id(0), *refs)

        return pl.pallas_call(
            kern, name=name, grid=(steps,), in_specs=in_specs, out_specs=out_specs, out_shape=out_shape,
            compiler_params=_cparams(("arbitrary",)),
        )(*row_ins, *const_ins)

    n_in, n_out = len(in_specs), len(out_specs)
    r_in, r_out = len(rider.ins), len(rider.outs)

    def kern_r(*refs):
        step = pl.program_id(0)
        ins, rins = refs[:n_in], refs[n_in:n_in + r_in]
        outs = refs[n_in + r_in:n_in + r_in + n_out]
        routs = refs[n_in + r_in + n_out:n_in + r_in + n_out + r_out]
        sems = refs[n_in + r_in + n_out + r_out:]

        @pl.when(step == 0)
        def _():
            rider.start(rins, routs, sems)

        if rider.mid is not None:
            @pl.when(step == steps // 2)
            def _():
                rider.mid(rins, routs, sems)

        body(step, *ins, *outs)

        @pl.when(step == steps - 1)
        def _():
            rider.finish(rins, routs, sems)

    return pl.pallas_call(
        kern_r, name=name, grid=(steps,), in_specs=in_specs + [ANY] * r_in, out_specs=out_specs + [ANY] * r_out,
        out_shape=out_shape + list(rider.outs), scratch_shapes=list(rider.sems),
        compiler_params=_cparams(("arbitrary",)),
    )(*row_ins, *const_ins, *rider.ins)


def _acc(step, ref, val):
    @pl.when(step == 0)
    def _():
        ref[...] = val

    @pl.when(step != 0)
    def _():
        ref[...] += val


def _embed(x, meta):
    bsz, seq, d = x.shape
    nb = seq // PAD + 1

    def kern(x_ref, m_ref, o_ref):
        i = pl.program_id(1)

        @pl.when(i == 0)
        def _():
            o_ref[0, 0:META0, :] = jnp.zeros((META0, d), F32)
            o_ref[0, META0:PAD, :] = m_ref[...]

        @pl.when(i != 0)
        def _():
            o_ref[0] = x_ref[0]

    return pl.pallas_call(
        kern, name="embed", grid=(bsz, nb),
        in_specs=[pl.BlockSpec((1, PAD, d), lambda b, i: (b, jnp.maximum(i - 1, 0), 0)),
                  pl.BlockSpec((N_META, d), lambda b, i: (0, 0))],
        out_specs=pl.BlockSpec((1, PAD, d), lambda b, i: (b, i, 0)),
        out_shape=jax.ShapeDtypeStruct((bsz, seq + PAD, d), F32),
        compiler_params=_cparams(("arbitrary", "arbitrary")),
    )(x, meta)


def _loss(h6, target):
    bsz, lp, d = h6.shape
    nb = lp // PAD

    def kern(h_ref, t_ref, l_ref, d_ref):
        b, i = pl.program_id(0), pl.program_id(1)

        @pl.when((b == 0) & (i == 0))
        def _():
            l_ref[...] = jnp.zeros_like(l_ref)

        @pl.when(i == 0)
        def _():
            d_ref[0] = jnp.zeros((PAD, d), F32)

        @pl.when(i != 0)
        def _():
            e = h_ref[0] - t_ref[0]
            d_ref[0] = e * (1.0 / d)
            l_ref[...] += 0.5 * jnp.sum(jnp.mean(e * e, axis=-1, keepdims=True))

    return pl.pallas_call(
        kern, name="loss", grid=(bsz, nb),
        in_specs=[pl.BlockSpec((1, PAD, d), lambda b, i: (b, i, 0)),
                  pl.BlockSpec((1, PAD, d), lambda b, i: (b, jnp.maximum(i - 1, 0), 0))],
        out_specs=[pl.BlockSpec((1, LANES), lambda b, i: (0, 0)),
                   pl.BlockSpec((1, PAD, d), lambda b, i: (b, i, 0))],
        out_shape=[jax.ShapeDtypeStruct((1, LANES), F32), jax.ShapeDtypeStruct((bsz, lp, d), F32)],
        compiler_params=_cparams(("arbitrary", "arbitrary")),
    )(h6, target)


def _meta_sum(dh0):
    bsz, lp, d = dh0.shape

    def kern(d_ref, o_ref):
        _acc(pl.program_id(0), o_ref, d_ref[0, META0:PAD, :])

    return pl.pallas_call(
        kern, name="meta_sum", grid=(bsz,),
        in_specs=[pl.BlockSpec((1, PAD, d), lambda b: (b, 0, 0))],
        out_specs=pl.BlockSpec((N_META, d), lambda b: (0, 0)),
        out_shape=jax.ShapeDtypeStruct((N_META, d), F32),
        compiler_params=_cparams(("arbitrary",)),
    )(dh0)


def _ffn_chunks(f):
    for n in (2, 4, 1, 11, 22):
        if f % n == 0 and (f // n) % LANES == 0:
            return n
    raise ValueError(f)


def _ffn_fwd(name, h, g, wgu, wd, rider=None):
    rows, d = h.shape
    f = wd.shape[0]
    nf = _ffn_chunks(f)
    tf = f // nf

    def body(step, h_ref, g_ref, wgu_ref, wd_ref, o_ref, ab_ref):
        hx = h_ref[...]
        hb = _rms(hx, g_ref[...])[0].astype(BF16)
        acc = jnp.zeros(hx.shape, F32)
        for j in range(nf):
            ga, ua = slice(j * tf, (j + 1) * tf), slice(f + j * tf, f + (j + 1) * tf)
            a = _dot(hb, wgu_ref[:, ga])
            b = _dot(hb, wgu_ref[:, ua])
            ab_ref[:, ga] = a.astype(BF16)
            ab_ref[:, ua] = b.astype(BF16)
            acc = acc + _dot(a * _sigmoid(a) * b, wd_ref[ga, :])
        o_ref[...] = hx + 0.5 * acc

    return _rowcall(name, body, rows, [h], [g, wgu, wd], [(d, F32), (2 * f, BF16)], rider=rider)


def _ffn_bwd(name, h, ab, dout, g, wgu, wd, rider=None):
    rows, d = h.shape
    f = wd.shape[0]
    nf = _ffn_chunks(f)
    tf = f // nf

    def body(step, h_ref, ab_ref, do_ref, g_ref, wgu_ref, wd_ref, dh_ref, hn_ref, dab_ref, act_ref, dg_ref):
        hx, dout_x, gx = h_ref[...], do_ref[...], g_ref[...]
        hn, y, rstd = _rms(hx, gx)
        hn_ref[...] = hn.astype(BF16)
        dhalf = (0.5 * dout_x).astype(BF16)
        dhn = jnp.zeros(hx.shape, F32)
        for j in range(nf):
            ga, ua = slice(j * tf, (j + 1) * tf), slice(f + j * tf, f + (j + 1) * tf)
            a = ab_ref[:, ga].astype(F32)
            b = ab_ref[:, ua].astype(F32)
            s = _sigmoid(a)
            silu = a * s
            act_ref[:, ga] = (silu * b).astype(BF16)
            dact = _dot_nt(dhalf, wd_ref[ga, :])
            da = (dact * b * (s + silu * (1.0 - s))).astype(BF16)
            db = (dact * silu).astype(BF16)
            dab_ref[:, ga] = da
            dab_ref[:, ua] = db
            dhn = dhn + _dot_nt(da, wgu_ref[:, ga]) + _dot_nt(db, wgu_ref[:, ua])
        dx, dg = _rms_bwd(dhn, y, rstd, gx)
        dh_ref[...] = dout_x + dx
        _acc(step, dg_ref, dg)

    return _rowcall(name, body, rows, [h, ab, dout], [g, wgu, wd],
                    [(d, F32), (d, BF16), (2 * f, BF16), (f, BF16)], [((1, d), F32)], rider=rider)


def _mm_tn(name, a, b, scale=1.0):
    rows, k1 = a.shape
    k2 = b.shape[1]
    tn = k2
    for cand in (512, 704, 1408, 1024):
        if k2 % cand == 0 and k1 * cand * 4 <= 6 * 1024 * 1024:
            tn = cand
    tm = _tn_tile(rows, a, b, k1, tn)
    steps = rows // tm

    def kern(a_ref, b_ref, o_ref):
        bx = b_ref[...]
        if scale != 1.0:
            bx = bx * scale
        _acc(pl.program_id(1), o_ref, _dot_tn(a_ref[...], bx))

    return pl.pallas_call(
        kern, name=name, grid=(k2 // tn, steps),
        in_specs=[pl.BlockSpec((tm, k1), lambda j, i: (i, 0)), pl.BlockSpec((tm, tn), lambda j, i: (i, j))],
        out_specs=pl.BlockSpec((k1, tn), lambda j, i: (0, j)),
        out_shape=jax.ShapeDtypeStruct((k1, k2), F32),
        compiler_params=_cparams(("arbitrary", "arbitrary")),
    )(a, b)


def _mm_tn_slots(name, a, b, scale):
    rows, k1 = a.shape
    k2 = b.shape[1]
    tn = 512 if k2 % 512 == 0 else k2
    sr = k1 // N_DEV
    tm = _tn_tile(rows, a, b, k1, tn)
    steps = rows // tm

    def kern(a_ref, b_ref, o_ref):
        bx = b_ref[...]
        if scale != 1.0:
            bx = bx * scale
        res = _dot_tn(a_ref[...], bx)
        step = pl.program_id(1)
        for s in range(N_DEV):
            _acc(step, o_ref.at[s], res[s * sr:(s + 1) * sr])

    return pl.pallas_call(
        kern, name=name, grid=(k2 // tn, steps),
        in_specs=[pl.BlockSpec((tm, k1), lambda j, i: (i, 0)), pl.BlockSpec((tm, tn), lambda j, i: (i, j))],
        out_specs=pl.BlockSpec((N_DEV, sr, tn), lambda j, i: (0, 0, j)),
        out_shape=jax.ShapeDtypeStruct((N_DEV, sr, k2), F32),
        compiler_params=_cparams(("arbitrary", "arbitrary")),
    )(a, b)


def _proj_fwd(name, h, g, w):
    rows = h.shape[0]

    def body(step, h_ref, g_ref, w_ref, o_ref):
        o_ref[...] = _dot(_rms(h_ref[...], g_ref[...])[0], w_ref[...])

    return _rowcall(name, body, rows, [h], [g, w], [(w.shape[1], F32)])[0]


def _proj_bwd(name, h, g, w, dy, dres, rider=None):
    rows, d = h.shape

    def body(step, h_ref, dy_ref, dr_ref, g_ref, w_ref, dh_ref, dg_ref, dw_ref):
        gx = g_ref[...]
        hn, y, rstd = _rms(h_ref[...], gx)
        dyx = dy_ref[...]
        dx, dg = _rms_bwd(_dot_nt(dyx, w_ref[...]), y, rstd, gx)
        dh_ref[...] = dr_ref[...] + dx
        _acc(step, dg_ref, dg)
        _acc(step, dw_ref, _dot_tn(hn, dyx))

    return _rowcall(name, body, rows, [h, dy, dres], [g, w], [(d, F32)], [((1, d), F32), (w.shape, F32)],
                    rider=rider)


def _lin_res_fwd(name, a, w, res):
    rows = a.shape[0]

    def body(step, a_ref, r_ref, w_ref, o_ref):
        o_ref[...] = r_ref[...] + _dot(a_ref[...], w_ref[...])

    return _rowcall(name, body, rows, [a, res], [w], [(w.shape[1], F32)])[0]


def _lin_bwd(name, a, w, dy, rider=None):
    rows, k = a.shape

    def body(step, a_ref, dy_ref, w_ref, da_ref, dw_ref):
        dyx = dy_ref[...]
        da_ref[...] = _dot_nt(dyx, w_ref[...])
        _acc(step, dw_ref, _dot_tn(a_ref[...], dyx))

    return _rowcall(name, body, rows, [a, dy], [w], [(k, F32)], [(w.shape, F32)], rider=rider)


def _s5_param_fn(lr, li, ls, brt, bit):
    step = jnp.exp(ls)
    mag = jnp.exp(lr * step)
    ar = mag * jnp.cos(li * step)
    ai = mag * jnp.sin(li * step)
    den = lr * lr + li * li
    nr, ni = ar - 1.0, ai
    cr = (nr * lr + ni * li) / den
    ci = (ni * lr - nr * li) / den
    return ar, ai, cr * brt - ci * bit, cr * bit + ci * brt


def _s5_params_fwd(lr, li, ls, brt, bit):
    def kern(lr_ref, li_ref, ls_ref, br_ref, bi_ref, ar_ref, ai_ref, bbr_ref, bbi_ref):
        ar, ai, bbr, bbi = _s5_param_fn(lr_ref[...], li_ref[...], ls_ref[...], br_ref[...], bi_ref[...])
        ar_ref[...], ai_ref[...], bbr_ref[...], bbi_ref[...] = ar, ai, bbr, bbi

    sd = jax.ShapeDtypeStruct
    return pl.pallas_call(
        kern, name="s5_params_fwd",
        out_shape=[sd(lr.shape, F32), sd(lr.shape, F32), sd(brt.shape, F32), sd(brt.shape, F32)],
    )(lr, li, ls, brt, bit)


def _s5_params_bwd(lr, li, ls, brt, bit, dar, dai, dbbr, dbbi):
    def kern(lr_ref, li_ref, ls_ref, br_ref, bi_ref, dar_ref, dai_ref, dbbr_ref, dbbi_ref,
             dlr_ref, dli_ref, dls_ref, dbr_ref, dbi_ref):
        _, vjp = jax.vjp(_s5_param_fn, lr_ref[...], li_ref[...], ls_ref[...], br_ref[...], bi_ref[...])
        dlr, dli, dls, dbr, dbi = vjp((dar_ref[...], dai_ref[...], dbbr_ref[...], dbbi_ref[...]))
        dlr_ref[...], dli_ref[...], dls_ref[...], dbr_ref[...], dbi_ref[...] = dlr, dli, dls, dbr, dbi

    sd = jax.ShapeDtypeStruct
    return pl.pallas_call(
        kern, name="s5_params_bwd",
        out_shape=[sd(lr.shape, F32), sd(lr.shape, F32), sd(ls.shape, F32), sd(brt.shape, F32), sd(brt.shape, F32)],
    )(lr, li, ls, brt, bit, dar, dai, dbbr, dbbi)


SCAN_LW = 512


def _scan_tables(a_ref, tab_ref, conj):
    ns = a_ref.shape[1]
    ar = jnp.broadcast_to(a_ref[0:1, :], (8, ns))
    ai = jnp.broadcast_to(a_ref[1:2, :], (8, ns))
    if conj:
        ai = -ai
    p1r, p1i = ar, ai
    p2r, p2i = p1r * p1r - p1i * p1i, 2.0 * p1r * p1i
    p4r, p4i = p2r * p2r - p2i * p2i, 2.0 * p2r * p2i
    row = lax.broadcasted_iota(jnp.int32, (8, ns), 0)
    e = row if not conj else 7 - row
    one, zero = jnp.ones((8, ns), F32), jnp.zeros((8, ns), F32)
    qr, qi = p1r, p1i
    for bit, (pr, pi) in ((1, (p1r, p1i)), (2, (p2r, p2i)), (4, (p4r, p4i))):
        sel = (e & bit) != 0
        fr, fi = jnp.where(sel, pr, one), jnp.where(sel, pi, zero)
        qr, qi = qr * fr - qi * fi, qr * fi + qi * fr
    for k, v in enumerate((p1r, p1i, p2r, p2i, p4r, p4i, qr, qi)):
        tab_ref[k] = v


def _scan_block(x_ref, tab_ref, carry_ref, t_rows, ns, reverse):
    ngrp = t_rows // 8
    row = lax.broadcasted_iota(jnp.int32, (8, SCAN_LW), 0)
    for lc in range(ns // SCAN_LW):
        lre = pl.ds(lc * SCAN_LW, SCAN_LW)
        lim = pl.ds(ns + lc * SCAN_LW, SCAN_LW)

        def group(k, carry, lre=lre, lim=lim):
            cr, ci = carry
            gi = (ngrp - 1 - k) if reverse else k
            rows = pl.ds(pl.multiple_of(gi * 8, 8), 8)
            vr, vi = x_ref[rows, lre], x_ref[rows, lim]
            for lvl, dsh in enumerate((1, 2, 4)):
                pr, pi = tab_ref[2 * lvl, :, lre], tab_ref[2 * lvl + 1, :, lre]
                if reverse:
                    keep = row < 8 - dsh
                    sr, si = pltpu.roll(vr, 8 - dsh, 0), pltpu.roll(vi, 8 - dsh, 0)
                else:
                    keep = row >= dsh
                    sr, si = pltpu.roll(vr, dsh, 0), pltpu.roll(vi, dsh, 0)
                sr, si = jnp.where(keep, sr, 0.0), jnp.where(keep, si, 0.0)
                vr, vi = vr + pr * sr - pi * si, vi + pr * si + pi * sr
            qr, qi = tab_ref[6, :, lre], tab_ref[7, :, lre]
            vr, vi = vr + qr * cr - qi * ci, vi + qr * ci + qi * cr
            x_ref[rows, lre], x_ref[rows, lim] = vr, vi
            edge = 0 if reverse else 7
            return (jnp.broadcast_to(vr[edge:edge + 1, :], (8, SCAN_LW)),
                    jnp.broadcast_to(vi[edge:edge + 1, :], (8, SCAN_LW)))

        cr, ci = lax.fori_loop(0, ngrp, group, (carry_ref[:, lre], carry_ref[:, lim]))
        carry_ref[:, lre], carry_ref[:, lim] = cr, ci


def _scan_rows(lp):
    for t in (384, 256, 128):
        if lp % t == 0:
            return t
    raise ValueError(lp)


def _s5_scan_fwd(u, bfull, cfull, a2, dvec, bsz):
    rows, hw = u.shape
    ns = a2.shape[1]
    lp = rows // bsz
    t_rows = _scan_rows(lp)
    nc = lp // t_rows

    def kern(u_ref, b_ref, c_ref, a_ref, d_ref, y_ref, x_ref, tab_ref, carry_ref):
        c = pl.program_id(1)

        @pl.when((pl.program_id(0) == 0) & (c == 0))
        def _():
            _scan_tables(a_ref, tab_ref, conj=False)

        @pl.when(c == 0)
        def _():
            carry_ref[...] = jnp.zeros_like(carry_ref)

        ux = u_ref[...]
        x_ref[...] = _dot(ux, b_ref[...])
        _scan_block(x_ref, tab_ref, carry_ref, t_rows, ns, reverse=False)
        y_ref[...] = _dot(x_ref[...], c_ref[...]) + d_ref[...] * ux

    const = lambda shp: pl.BlockSpec(shp, lambda b, c: (0,) * len(shp), pipeline_mode=pl.Buffered(1))
    return pl.pallas_call(
        kern, name="s5_scan_fwd", grid=(bsz, nc),
        in_specs=[pl.BlockSpec((t_rows, hw), lambda b, c: (b * nc + c, 0)),
                  const(bfull.shape), const(cfull.shape), const(a2.shape), const(dvec.shape)],
        out_specs=[pl.BlockSpec((t_rows, hw), lambda b, c: (b * nc + c, 0)),
                   pl.BlockSpec((t_rows, 2 * ns), lambda b, c: (b * nc + c, 0))],
        out_shape=[jax.ShapeDtypeStruct((rows, hw), F32), jax.ShapeDtypeStruct((rows, 2 * ns), F32)],
        scratch_shapes=[pltpu.VMEM((8, 8, ns), F32), pltpu.VMEM((8, 2 * ns), F32)],
        compiler_params=_cparams(("arbitrary", "arbitrary")),
    )(u, bfull, cfull, a2, dvec)


def _s5_scan_bwd(dy, u, xs, ctfull, btfull, a2, dvec, bsz):
    rows, hw = u.shape
    ns = a2.shape[1]
    lp = rows // bsz
    t_rows = _scan_rows(lp)
    nc = lp // t_rows
    blk = lambda b, c: (b * nc + (nc - 1 - c), 0)

    def prev8(b, c):
        first = (b * nc + (nc - 1 - c)) * (t_rows // 8)
        return (jnp.maximum(first - 1, 0), 0)

    def kern(dy_ref, u_ref, x_ref, xp_ref, ct_ref, bt_ref, a_ref, d_ref, du_ref, gx_ref, da_ref, dd_ref,
             tab_ref, carry_ref):
        b, c = pl.program_id(0), pl.program_id(1)
        first = (b == 0) & (c == 0)

        @pl.when(first)
        def _():
            _scan_tables(a_ref, tab_ref, conj=True)

        @pl.when(c == 0)
        def _():
            carry_ref[...] = jnp.zeros_like(carry_ref)

        dyx, ux = dy_ref[...], u_ref[...]
        gx_ref[...] = _dot(dyx, ct_ref[...])
        _scan_block(gx_ref, tab_ref, carry_ref, t_rows, ns, reverse=True)
        gx = gx_ref[...]
        du_ref[...] = _dot(gx, bt_ref[...]) + d_ref[...] * dyx
        xprev = pltpu.roll(x_ref[...], 1, 0)
        seq_start = c == nc - 1
        head = jnp.where(seq_start, 0.0, xp_ref[7:8, :])
        rid = lax.broadcasted_iota(jnp.int32, (t_rows, 1), 0)
        xprev = jnp.where(rid == 0, head, xprev)
        xr, xi, gr, gi = xprev[:, :ns], xprev[:, ns:], gx[:, :ns], gx[:, ns:]
        da = jnp.concatenate([jnp.sum(xr * gr + xi * gi, axis=0, keepdims=True),
                              jnp.sum(xr * gi - xi * gr, axis=0, keepdims=True)], axis=1)
        dd = jnp.sum(dyx * ux, axis=0, keepdims=True)

        @pl.when(first)
        def _():
            da_ref[...] = da
            dd_ref[...] = dd

        @pl.when(jnp.logical_not(first))
        def _():
            da_ref[...] += da
            dd_ref[...] += dd

    const = lambda shp: pl.BlockSpec(shp, lambda b, c: (0,) * len(shp), pipeline_mode=pl.Buffered(1))
    return pl.pallas_call(
        kern, name="s5_scan_bwd", grid=(bsz, nc),
        in_specs=[pl.BlockSpec((t_rows, hw), blk), pl.BlockSpec((t_rows, hw), blk),
                  pl.BlockSpec((t_rows, 2 * ns), blk), pl.BlockSpec((8, 2 * ns), prev8),
                  const(ctfull.shape), const(btfull.shape), const(a2.shape), const(dvec.shape)],
        out_specs=[pl.BlockSpec((t_rows, hw), blk), pl.BlockSpec((t_rows, 2 * ns), blk),
                   pl.BlockSpec((1, 2 * ns), lambda b, c: (0, 0)), pl.BlockSpec((1, hw), lambda b, c: (0, 0))],
        out_shape=[jax.ShapeDtypeStruct((rows, hw), F32), jax.ShapeDtypeStruct((rows, 2 * ns), F32),
                   jax.ShapeDtypeStruct((1, 2 * ns), F32), jax.ShapeDtypeStruct((1, hw), F32)],
        scratch_shapes=[pltpu.VMEM((8, 8, ns), F32), pltpu.VMEM((8, 2 * ns), F32)],
        compiler_params=_cparams(("arbitrary", "arbitrary")),
    )(dy, u, xs, xs, ctfull, btfull, a2, dvec)


def _glu_fwd(y, h1, wout):
    rows, d = h1.shape

    def body(step, y_ref, h_ref, w_ref, o_ref):
        z = _dot(_gelu(y_ref[...])[0], w_ref[...])
        o_ref[...] = h_ref[...] + z[:, :d] * _sigmoid(z[:, d:])

    return _rowcall("glu_fwd", body, rows, [y, h1], [wout], [(d, F32)])[0]


def _glu_bwd(y, dh2, wout, rider=None):
    rows, d = dh2.shape
    hw = y.shape[1]

    def body(step, y_ref, dh_ref, w_ref, dy_ref, dw_ref):
        yx, dh = y_ref[...], dh_ref[...]
        gl, t = _gelu(yx)
        z = _dot(gl, w_ref[...])
        za, sg = z[:, :d], _sigmoid(z[:, d:])
        dza = dh * sg
        dzg = dh * za * sg * (1.0 - sg)
        dgl = _dot_nt(dza, w_ref[:, :d]) + _dot_nt(dzg, w_ref[:, d:])
        dy_ref[...] = dgl * _gelu_grad(yx, t)
        for half, dz in enumerate((dza, dzg)):
            dw = _dot_tn(gl, dz)
            for s in range(N_DEV // 2):
                _acc(step, dw_ref.at[half * (N_DEV // 2) + s], dw[:, s * cw:(s + 1) * cw])

    cw = 2 * d // N_DEV
    return _rowcall("glu_bwd", body, rows, [y, dh2], [wout], [(hw, F32)], [((N_DEV, hw, cw), F32)], rider=rider)


def _gmean64(x2, gmat):
    hi = x2.astype(BF16)
    r1 = x2 - hi.astype(F32)
    mid = r1.astype(BF16)
    lo = (r1 - mid.astype(F32)).astype(BF16)
    outs = []
    for j in range(x2.shape[1] // LANES):
        sl = slice(j * LANES, (j + 1) * LANES)
        f = lambda p: jnp.dot(p[:, sl], gmat, preferred_element_type=F32)
        outs.append(f(hi) + f(mid) + f(lo))
    return outs[0] if len(outs) == 1 else jnp.concatenate(outs, axis=1)


def _swap32(x):
    w = x.shape[1]
    lane = lax.broadcasted_iota(jnp.int32, (1, w), 1)
    return jnp.where((lane & 32) == 0, pltpu.roll(x, w - 32, 1), pltpu.roll(x, 32, 1))


def _tile_lanes(t, w):
    reps = w // t.shape[1]
    return t if reps == 1 else jnp.concatenate([t] * reps, axis=1)


def _headrope_fwd(name, raw, w, gain, cos, sin, gmat, lp):
    rows = raw.shape[0]
    tm = _row_tile(lp)
    per = lp // tm

    def body(step, x_ref, c_ref, s_ref, g_ref, gm_ref, o_ref):
        x = x_ref[...]
        rstd = lax.rsqrt(_gmean64(x * x, gm_ref[...]) + EPS)
        z = x * rstd * g_ref[...]
        o_ref[...] = z * _tile_lanes(c_ref[...], w) + _swap32(z) * _tile_lanes(s_ref[...], w)

    maps = [((tm, w), lambda i: (i, 0)), ((tm, LANES), lambda i: (i % per, 0)), ((tm, LANES), lambda i: (i % per, 0))]
    return _rowcall(name, body, rows, [raw, cos, sin], [gain, gmat], [(w, F32)], tm=tm, row_in_maps=maps)[0]


def _headrope_bwd(name, raw, w, dout, gain, cos, sin, gmat, lp):
    rows = raw.shape[0]
    tm = _row_tile(lp)
    per = lp // tm

    def body(step, x_ref, do_ref, c_ref, s_ref, g_ref, gm_ref, dx_ref, dg_ref):
        x, dout_x, gx, gm = x_ref[...], do_ref[...], g_ref[...], gm_ref[...]
        rstd = lax.rsqrt(_gmean64(x * x, gm) + EPS)
        yn = x * rstd
        dz = dout_x * _tile_lanes(c_ref[...], w) + _swap32(dout_x * _tile_lanes(s_ref[...], w))
        dyn = dz * gx
        dx_ref[...] = rstd * (dyn - yn * _gmean64(dyn * yn, gm))
        dg = jnp.sum(dz * yn, axis=0, keepdims=True)
        sh = w // 2
        while sh >= HEAD_DIM:
            dg = dg + pltpu.roll(dg, sh, 1)
            sh //= 2
        _acc(step, dg_ref, dg)

    maps = [((tm, w), lambda i: (i, 0)), None, ((tm, LANES), lambda i: (i % per, 0)), ((tm, LANES), lambda i: (i % per, 0))]
    return _rowcall(name, body, rows, [raw, dout, cos, sin], [gain, gmat], [(w, F32)], [((1, w), F32)],
                    tm=tm, row_in_maps=maps)


KVW = N_KV_HEADS * HEAD_DIM
QB = 128


def _fold4(x):
    y = x + pltpu.roll(x, 128, 1)
    return y + pltpu.roll(y, 64, 1)


def _attn_scores(i, q_ref, k0_ref, kp_ref, kc_ref, sink_ref, h):
    lane = lax.broadcasted_iota(jnp.int32, (1, KVW), 1) // HEAD_DIM
    qh = q_ref[:, h * KVW:(h + 1) * KVW]
    qs = jnp.concatenate([jnp.where(lane == g, qh, 0.0) for g in range(Q_PER_KV)], axis=0).astype(BF16)
    hsel = lane == h
    kx = [_fold4(jnp.where(hsel, r[...], 0.0)).astype(BF16) for r in (k0_ref, kp_ref, kc_ref)]
    scale = HEAD_DIM ** -0.5
    s0, sp, sc = [_dot_nt(qs, k) * scale for k in kx]
    qi = lax.broadcasted_iota(jnp.int32, (Q_PER_KV * QB, QB), 0) % QB
    kj = lax.broadcasted_iota(jnp.int32, (Q_PER_KV * QB, QB), 1)
    s0 = jnp.where(kj >= META0, s0, NEG_INF)
    sp = jnp.where((kj > qi) & (i >= 2), sp, NEG_INF)
    sc = jnp.where(kj <= qi, sc, NEG_INF)
    rowg = lax.broadcasted_iota(jnp.int32, (Q_PER_KV * QB, 1), 0) // QB
    sink = jnp.zeros((Q_PER_KV * QB, 1), F32)
    for g in range(Q_PER_KV):
        sink = jnp.where(rowg == g, sink_ref[0, h * Q_PER_KV + g], sink)
    m = jnp.maximum(jnp.maximum(jnp.max(s0, axis=1, keepdims=True), jnp.max(sp, axis=1, keepdims=True)),
                    jnp.maximum(jnp.max(sc, axis=1, keepdims=True), sink))
    p0, pp, pc, ps = jnp.exp(s0 - m), jnp.exp(sp - m), jnp.exp(sc - m), jnp.exp(sink - m)
    den = jnp.sum(p0, axis=1, keepdims=True) + jnp.sum(pp, axis=1, keepdims=True) + jnp.sum(pc, axis=1, keepdims=True) + ps
    return qs, kx, (p0, pp, pc), ps, den, lane, hsel


def _unstack(x, lane):
    out = jnp.where(lane == 0, x[0:QB], 0.0)
    for g in range(1, Q_PER_KV):
        out = out + jnp.where(lane == g, x[g * QB:(g + 1) * QB], 0.0)
    return out


def _attn_specs(nb, d):
    qspec = pl.BlockSpec((None, QB, d), lambda b, i: (b, i, 0))
    k0 = pl.BlockSpec((None, QB, KVW), lambda b, i: (b, 0, 0))
    kp = pl.BlockSpec((None, QB, KVW), lambda b, i: (b, jnp.maximum(i - 1, 0), 0))
    kc = pl.BlockSpec((None, QB, KVW), lambda b, i: (b, i, 0))
    v0 = pl.BlockSpec((None, QB, KVW), lambda b, i: (b, 0, 1))
    vp = pl.BlockSpec((None, QB, KVW), lambda b, i: (b, jnp.maximum(i - 1, 0), 1))
    vc = pl.BlockSpec((None, QB, KVW), lambda b, i: (b, i, 1))
    sink = pl.BlockSpec(memory_space=pltpu.SMEM)
    return qspec, [k0, kp, kc], [v0, vp, vc], sink


def _attn_fwd(q, k, kv, sinks):
    bsz, lp, d = q.shape
    nb = lp // QB
    qspec, kspecs, vspecs, sspec = _attn_specs(nb, d)

    def kern(q_ref, k0_ref, kp_ref, kc_ref, v0_ref, vp_ref, vc_ref, sink_ref, o_ref):
        i = pl.program_id(1)
        for h in range(N_KV_HEADS):
            qs, kx, ps3, psink, den, lane, hsel = _attn_scores(i, q_ref, k0_ref, kp_ref, kc_ref, sink_ref, h)
            vx = [_fold4(jnp.where(hsel, r[...], 0.0)).astype(BF16) for r in (v0_ref, vp_ref, vc_ref)]
            o = _dot(ps3[0], vx[0]) + _dot(ps3[1], vx[1]) + _dot(ps3[2], vx[2])
            o_ref[:, h * KVW:(h + 1) * KVW] = _unstack(o / den, lane)

    return pl.pallas_call(
        kern, name="attn_fwd", grid=(bsz, nb),
        in_specs=[qspec] + kspecs + vspecs + [sspec],
        out_specs=qspec, out_shape=jax.ShapeDtypeStruct((bsz, lp, d), F32),
        compiler_params=_cparams(("arbitrary", "arbitrary")),
    )(q, k, k, k, kv, kv, kv, sinks)


def _attn_bwd(q, k, kv, sinks, o, do):
    bsz, lp, d = q.shape
    nb = lp // QB
    qspec, kspecs, vspecs, sspec = _attn_specs(nb, d)
    full = pl.BlockSpec((None, lp, KVW), lambda b, i: (b, 0, 0))

    def kern(q_ref, k0_ref, kp_ref, kc_ref, v0_ref, vp_ref, vc_ref, sink_ref, o_ref, do_ref,
             dq_ref, dk_ref, dv_ref, ds_ref):
        b, i = pl.program_id(0), pl.program_id(1)

        @pl.when(i == 0)
        def _():
            dk_ref[...] = jnp.zeros_like(dk_ref)
            dv_ref[...] = jnp.zeros_like(dv_ref)

        @pl.when((b == 0) & (i == 0))
        def _():
            ds_ref[...] = jnp.zeros_like(ds_ref)

        lane128 = lax.broadcasted_iota(jnp.int32, (1, LANES), 1)
        rowg = lax.broadcasted_iota(jnp.int32, (Q_PER_KV * QB, 1), 0) // QB
        dk_acc = [jnp.zeros((QB, KVW), F32) for _ in range(3)]
        dv_acc = [jnp.zeros((QB, KVW), F32) for _ in range(3)]
        dsink = jnp.zeros((1, LANES), F32)
        for h in range(N_KV_HEADS):
            qs, kx, ps3, psink, den, lane, hsel = _attn_scores(i, q_ref, k0_ref, kp_ref, kc_ref, sink_ref, h)
            vx = [_fold4(jnp.where(hsel, r[...], 0.0)).astype(BF16) for r in (v0_ref, vp_ref, vc_ref)]
            sl = slice(h * KVW, (h + 1) * KVW)
            doh, oh = do_ref[:, sl], o_ref[:, sl]
            dos = jnp.concatenate([jnp.where(lane == g, doh, 0.0) for g in range(Q_PER_KV)], axis=0)
            ost = jnp.concatenate([jnp.where(lane == g, oh, 0.0) for g in range(Q_PER_KV)], axis=0)
            delta = jnp.sum(dos * ost, axis=1, keepdims=True)
            inv = 1.0 / den
            dosb = dos.astype(BF16)
            dqs = jnp.zeros((Q_PER_KV * QB, KVW), F32)
            for n in range(3):
                pn = ps3[n] * inv
                ds = pn * (_dot_nt(dosb, vx[n]) - delta) * (HEAD_DIM ** -0.5)
                dqs = dqs + _dot(ds, kx[n])
                dk_acc[n] = dk_acc[n] + jnp.where(hsel, _fold4(_dot_tn(ds, qs)), 0.0)
                dv_acc[n] = dv_acc[n] + jnp.where(hsel, _fold4(_dot_tn(pn, dosb)), 0.0)
            dq_ref[:, sl] = _unstack(dqs, lane)
            dsk = -(psink * inv) * delta
            for g in range(Q_PER_KV):
                val = jnp.sum(jnp.where(rowg == g, dsk, 0.0), axis=0, keepdims=True)
                dsink = dsink + jnp.where(lane128 == h * Q_PER_KV + g, val, 0.0)
        ds_ref[...] += dsink
        r0 = pl.ds(0, QB)
        rp = pl.ds(pl.multiple_of(jnp.maximum(i - 1, 0) * QB, QB), QB)
        rc = pl.ds(pl.multiple_of(i * QB, QB), QB)
        for rows, n in ((r0, 0), (rp, 1), (rc, 2)):
            dk_ref[rows, :] += dk_acc[n]
            dv_ref[rows, :] += dv_acc[n]

    return pl.pallas_call(
        kern, name="attn_bwd", grid=(bsz, nb),
        in_specs=[qspec] + kspecs + vspecs + [sspec, qspec, qspec],
        out_specs=[qspec, full, full, pl.BlockSpec((1, LANES), lambda b, i: (0, 0))],
        out_shape=[jax.ShapeDtypeStruct((bsz, lp, d), F32), jax.ShapeDtypeStruct((bsz, lp, KVW), F32),
                   jax.ShapeDtypeStruct((bsz, lp, KVW), F32), jax.ShapeDtypeStruct((1, LANES), F32)],
        compiler_params=_cparams(("arbitrary", "arbitrary")),
    )(q, k, k, k, kv, kv, kv, sinks, o, do)


def _concat_cols(name, a, b):
    rows = a.shape[0]

    def body(step, a_ref, b_ref, o_ref):
        o_ref[...] = jnp.concatenate([a_ref[...], b_ref[...]], axis=1)

    return _rowcall(name, body, rows, [a, b], [], [(a.shape[1] + b.shape[1], F32)])[0]


def _adamw(name, w, m, v, parts):
    rows, wd = w.shape
    n = parts.shape[0]
    tm = _row_tile(rows)

    def kern(w_ref, m_ref, v_ref, p_ref, g_ref, d_ref, m2_ref, v2_ref):
        g = p_ref[0].astype(F32)
        for k in range(1, n):
            g = g + p_ref[k].astype(F32)
        m2 = ADAM_B1 * m_ref[...] + (1.0 - ADAM_B1) * g
        v2 = ADAM_B2 * v_ref[...] + (1.0 - ADAM_B2) * (g * g)
        mh = m2 / (1.0 - ADAM_B1 ** ADAM_STEP)
        vh = v2 / (1.0 - ADAM_B2 ** ADAM_STEP)
        g_ref[...] = g
        d_ref[...] = -ADAM_LR * (mh / (jnp.sqrt(vh) + ADAM_EPS) + ADAM_WD * w_ref[...])
        m2_ref[...] = m2
        v2_ref[...] = v2

    spec = pl.BlockSpec((tm, wd), lambda i: (i, 0))
    sd = jax.ShapeDtypeStruct((rows, wd), F32)
    return pl.pallas_call(
        kern, name=name, grid=(rows // tm,),
        in_specs=[spec, spec, spec, pl.BlockSpec((n, tm, wd), lambda i: (0, i, 0))],
        out_specs=[spec] * 4, out_shape=[sd] * 4,
        compiler_params=_cparams(("arbitrary",)),
    )(w, m, v, parts)


def _pair_sum(name, parts, theirs, my_c):
    n, _, rows, wd = parts.shape
    tm = _row_tile(rows)

    def kern(c_ref, a_ref, b_ref, o_ref):
        o_ref[...] = (a_ref[...] + b_ref[...]).astype(BF16)

    return pl.pallas_call(
        kern, name=name,
        grid_spec=pltpu.PrefetchScalarGridSpec(
            num_scalar_prefetch=1, grid=(n, rows // tm),
            in_specs=[pl.BlockSpec((None, None, tm, wd), lambda k, i, c: (k, c[0], i, 0)),
                      pl.BlockSpec((None, tm, wd), lambda k, i, c: (k, i, 0))],
            out_specs=pl.BlockSpec((None, tm, wd), lambda k, i, c: (k, i, 0))),
        out_shape=jax.ShapeDtypeStruct((n, rows, wd), BF16), compiler_params=_cparams(("arbitrary", "arbitrary")),
    )(my_c, parts, theirs)


MESH = pl.DeviceIdType.MESH
ANY = pl.BlockSpec(memory_space=pl.ANY)


def _place():
    x, y, c = lax.axis_index("x"), lax.axis_index("y"), lax.axis_index("c")
    return x, y, c, [(1 - x, y), (x, 1 - y), (1 - x, 1 - y)]


def _gather_rider(shards):
    n = len(shards)

    def copy(refs, a, k, block, to, own=False):
        x_refs, out_refs, (send_sems, recv_sems, _) = refs
        px, py, pc = block
        slot = out_refs[a].at[4 * px + 2 * py + pc]
        return pltpu.make_async_remote_copy(
            src_ref=x_refs[a] if own else slot, dst_ref=slot,
            send_sem=send_sems.at[a, k], recv_sem=recv_sems.at[a, k], device_id=to, device_id_type=MESH)

    def local(refs, a):
        x, y, c, _ = _place()
        return pltpu.make_async_copy(refs[0][a], refs[1][a].at[4 * x + 2 * y + c], refs[2][2].at[a])

    def first(refs):
        x, y, c, chips = _place()
        out = []
        for a in range(n):
            out.append(copy(refs, a, 0, (x, y, c), (x, y, 1 - c), own=True))
            out += [copy(refs, a, 1 + j, (x, y, c), (*chip, c), own=True) for j, chip in enumerate(chips)]
        return out

    def passed(refs):
        x, y, c, chips = _place()
        return [copy(refs, a, 4 + j, (*chip, c), (x, y, 1 - c)) for j, chip in enumerate(chips) for a in range(n)]

    def start(*refs):
        for a in range(n):
            local(refs, a).start()
        for cp in first(refs):
            cp.start()

    def mid(*refs):
        x, y, c, chips = _place()
        fwd = passed(refs)
        for j, chip in enumerate(chips):
            for a in range(n):
                copy(refs, a, 1 + j, (*chip, c), (x, y, c)).wait_recv()
                fwd[j * n + a].start()

    def finish(*refs):
        x, y, c, chips = _place()
        for a in range(n):
            copy(refs, a, 0, (x, y, 1 - c), (x, y, c)).wait_recv()
            for j, chip in enumerate(chips):
                copy(refs, a, 4 + j, (*chip, 1 - c), (x, y, c)).wait_recv()
        for cp in first(refs) + passed(refs):
            cp.wait_send()
        for a in range(n):
            local(refs, a).wait()

    return _Rider(list(shards), [jax.ShapeDtypeStruct((N_DEV,) + s.shape, s.dtype) for s in shards],
                  [pltpu.SemaphoreType.DMA((n, 7)), pltpu.SemaphoreType.DMA((n, 7)), pltpu.SemaphoreType.DMA((n,))],
                  start, mid, finish)


def _swap_rider(parts):
    n = len(parts)

    def copies(p_refs, out_refs, sems):
        x, y, c, _ = _place()
        return [pltpu.make_async_remote_copy(
            src_ref=p_refs[a].at[:, 1 - c], dst_ref=out_refs[a], send_sem=sems[0].at[a], recv_sem=sems[1].at[a],
            device_id=(x, y, 1 - c), device_id_type=MESH) for a in range(n)]

    def start(*refs):
        for cp in copies(*refs):
            cp.start()

    def finish(*refs):
        for cp in copies(*refs):
            cp.wait()

    return _Rider(list(parts), [jax.ShapeDtypeStruct((p.shape[0],) + p.shape[2:], p.dtype) for p in parts],
                  [pltpu.SemaphoreType.DMA((n,)), pltpu.SemaphoreType.DMA((n,))], start, None, finish)


def _scatter_rider(sums):
    n = len(sums)

    def copy(refs, a, j, block):
        s_refs, out_refs, (send_sems, recv_sems, _) = refs
        x, y, c, chips = _place()
        px, py = chips[j]
        return pltpu.make_async_remote_copy(
            src_ref=s_refs[a].at[2 * px + py], dst_ref=out_refs[a].at[block],
            send_sem=send_sems.at[a, j], recv_sem=recv_sems.at[a, j], device_id=(px, py, c), device_id_type=MESH)

    def local(refs, a):
        x, y, c, _ = _place()
        return pltpu.make_async_copy(refs[0][a].at[2 * x + y], refs[1][a].at[2 * x + y], refs[2][2].at[a])

    def sends(refs):
        x, y, c, _ = _place()
        return [copy(refs, a, j, 2 * x + y) for j in range(3) for a in range(n)]

    def start(*refs):
        for a in range(n):
            local(refs, a).start()
        for cp in sends(refs):
            cp.start()

    def finish(*refs):
        x, y, c, chips = _place()
        for j, (px, py) in enumerate(chips):
            for a in range(n):
                copy(refs, a, j, 2 * px + py).wait_recv()
        for cp in sends(refs):
            cp.wait_send()
        for a in range(n):
            local(refs, a).wait()

    return _Rider(list(sums), [jax.ShapeDtypeStruct(s.shape, s.dtype) for s in sums],
                  [pltpu.SemaphoreType.DMA((n, 3)), pltpu.SemaphoreType.DMA((n, 3)), pltpu.SemaphoreType.DMA((n,))],
                  start, None, finish)


BIG = (("ffn1_w_gate_up", 2), ("ffn1_w_down", 1), ("ffn2_w_gate_up", 2), ("ffn2_w_down", 1), ("ssm_w_in", 1),
       ("ssm_w_out", 2), ("w_kv", 0), ("attn_w_q", 1), ("attn_w_o", 1))
SMALL = ("ffn1_norm", "mix_norm", "ffn2_norm", "ssm_lambda_re", "ssm_lambda_im", "ssm_b_re", "ssm_b_im",
         "ssm_c_re", "ssm_c_im", "ssm_log_step", "kv_norm", "k_norm", "q_norm", "attn_sinks")
COLS = (("meta_tokens", 1), ("ssm_d", 1))
WEIGHTS = ("meta_tokens", "ffn1_norm", "ffn1_w_gate_up", "ffn1_w_down", "mix_norm", "ffn2_norm", "ffn2_w_gate_up",
           "ffn2_w_down", "ssm_w_in", "ssm_lambda_re", "ssm_lambda_im", "ssm_b_re", "ssm_b_im", "ssm_c_re",
           "ssm_c_im", "ssm_log_step", "ssm_d", "ssm_w_out", "kv_norm", "w_kv", "k_norm", "attn_w_q", "q_norm",
           "attn_sinks", "attn_w_o")


def _rows_of(a, width):
    n = math.prod(a.shape)
    if n % width == 0:
        r = a.reshape(n // width, width)
    else:
        assert n < width
        r = jnp.pad(a.reshape(1, n), ((0, 0), (0, width - n)))
    return jnp.pad(r, ((0, (-r.shape[0]) % 8), (0, 0)))


def _pack_small(arrs, width):
    return jnp.concatenate([_rows_of(a.astype(F32), width) for a in arrs], axis=0)


def _unpack_small(buf, shapes, width):
    out, off = [], 0
    for shp in shapes:
        n = math.prod(shp)
        r = max(n // width, 1)
        out.append(buf[off:off + r].reshape(shp) if n % width == 0 else buf[off, :n].reshape(shp))
        off += r + (-r) % 8
    return out


def _shape2d(shp):
    return (math.prod(shp[:-1]), shp[-1])


def _unshard(g, axis):
    g = jnp.moveaxis(g, 0, axis)
    shp = g.shape
    return g.reshape(shp[:axis] + (shp[axis] * shp[axis + 1],) + shp[axis + 2:])


def _shard(full, axis):
    shp = full.shape
    g = full.reshape(shp[:axis] + (N_DEV, shp[axis] // N_DEV) + shp[axis + 1:])
    return jnp.moveaxis(g, axis, 0)


def _blockdiag(blocks):
    g, r, c = blocks.shape
    eye = jnp.eye(g, dtype=blocks.dtype)
    return (eye[:, None, :, None] * blocks[:, :, None, :]).reshape(g * r, g * c)


def _diagblocks(full, g):
    r, c = full.shape[0] // g, full.shape[1] // g
    f = full.reshape(g, r, g, c)
    idx = jnp.arange(g)
    return f[idx, :, idx, :]


def kernel(x, meta_tokens, ffn1_norm, ffn1_w_gate_up, ffn1_w_down, mix_norm, ffn2_norm, ffn2_w_gate_up, ffn2_w_down, ssm_w_in, ssm_lambda_re, ssm_lambda_im, ssm_b_re, ssm_b_im, ssm_c_re, ssm_c_im, ssm_log_step, ssm_d, ssm_w_out, kv_norm, w_kv, k_norm, attn_w_q, q_norm, attn_sinks, attn_w_o, loss_target, m_meta_tokens, m_ffn1_norm, m_ffn1_w_gate_up, m_ffn1_w_down, m_mix_norm, m_ffn2_norm, m_ffn2_w_gate_up, m_ffn2_w_down, m_ssm_w_in, m_ssm_lambda_re, m_ssm_lambda_im, m_ssm_b_re, m_ssm_b_im, m_ssm_c_re, m_ssm_c_im, m_ssm_log_step, m_ssm_d, m_ssm_w_out, m_kv_norm, m_w_kv, m_k_norm, m_attn_w_q, m_q_norm, m_attn_sinks, m_attn_w_o, v_meta_tokens, v_ffn1_norm, v_ffn1_w_gate_up, v_ffn1_w_down, v_mix_norm, v_ffn2_norm, v_ffn2_w_gate_up, v_ffn2_w_down, v_ssm_w_in, v_ssm_lambda_re, v_ssm_lambda_im, v_ssm_b_re, v_ssm_b_im, v_ssm_c_re, v_ssm_c_im, v_ssm_log_step, v_ssm_d, v_ssm_w_out, v_kv_norm, v_w_kv, v_k_norm, v_attn_w_q, v_q_norm, v_attn_sinks, v_attn_w_o):
    args = dict(locals())
    W = {n: args[n] for n in WEIGHTS}
    M = {n: args["m_" + n] for n in WEIGHTS}
    V = {n: args["v_" + n] for n in WEIGHTS}
    my_x, my_y, my_c = (lax.axis_index(a) for a in MESH_AXES)
    my_dev = 4 * my_x + 2 * my_y + my_c

    big_names = [n for n, _ in BIG]
    s2d = {n: _shape2d(W[n].shape) for n in big_names}
    col_w = W["meta_tokens"].shape[1]

    grads, summed = _local_step(x, loss_target, W, my_c.astype(jnp.int32).reshape(1))
    loss = lax.psum(grads.pop("loss"), MESH_AXES)
    grad_x = grads.pop("x")

    outs = [{}, {}, {}, {}]
    for n in big_names:
        r4 = _adamw("adamw_" + n, W[n].reshape(s2d[n]), M[n].reshape(s2d[n]), V[n].reshape(s2d[n]), summed[n])
        for k in range(4):
            outs[k][n] = r4[k].reshape(W[n].shape)

    small_names = list(SMALL) + [n for n, _ in COLS]
    small_shapes = [grads[n].shape for n in small_names]
    small_parts = _run_rider("gather_small_grads",
                             _gather_rider([_pack_small([grads[n] for n in small_names], PACK_W)]))[0]
    zero_cols = [jnp.zeros(grads[n].shape, F32) for n, _ in COLS]
    packs = lambda d: _pack_small([d[n] for n in SMALL] + zero_cols, PACK_W)
    r4 = _adamw("adamw_small", packs(W), packs(M), packs(V), small_parts)
    gsmall = None
    for k in range(4):
        un = dict(zip(small_names, _unpack_small(r4[k], small_shapes, PACK_W)))
        gsmall = un if k == 0 else gsmall
        outs[k].update({n: un[n] for n in SMALL})
    col_g = [lax.dynamic_slice_in_dim(gsmall[n], my_dev * W[n].shape[1], W[n].shape[1], axis=1) for n, _ in COLS]
    packc = lambda d: _pack_small([d[n] for n, _ in COLS], col_w)
    r4 = _adamw("adamw_cols", packc(W), packc(M), packc(V), _pack_small(col_g, col_w)[None])
    col_shapes = [W[n].shape for n, _ in COLS]
    for k in range(4):
        outs[k].update(dict(zip([n for n, _ in COLS], _unpack_small(r4[k], col_shapes, col_w))))

    res = [[outs[k][n] for n in WEIGHTS] for k in range(4)]
    return (loss, grad_x, *res[0], *res[1], *res[2], *res[3])


def _local_step(x, target, P, c_arr):
    bsz, seq, d = x.shape
    lp = seq + PAD
    rows = bsz * lp
    depth = P["ffn1_norm"].shape[0]
    assert depth == 2
    bf = lambda a: a.astype(BF16)
    row = lambda a: a.reshape(1, -1)

    def shard(n, l=None):
        a = P[n] if l is None else P[n][l]
        return bf(a.reshape(_shape2d(a.shape)))

    rowsharded = lambda g: g.reshape((g.shape[0] * g.shape[1],) + g.shape[2:])
    colsharded = lambda g: _unshard(g, 1)
    col_w = P["meta_tokens"].shape[1]
    g0 = _run_rider("gather_first", _gather_rider(
        [shard("ffn1_w_gate_up", 0), shard("ffn1_w_down", 0), shard("ssm_w_in", 0),
         _pack_small([P["meta_tokens"], P["ssm_d"]], col_w)]))
    ffn_w = {("ffn1", 0): (colsharded(g0[0]), rowsharded(g0[1]))}
    w_in = rowsharded(g0[2])
    meta_full = _unshard(g0[3][:, :N_META], 1)
    dvec = _unshard(g0[3][:, N_META:N_META + 1, :P["ssm_d"].shape[1]], 1)

    pos = (jnp.arange(lp, dtype=F32) - float(META0))[:, None]
    half = HEAD_DIM // 2
    freqs = ROPE_THETA ** (-jnp.arange(0, half, dtype=F32) * 2.0 / HEAD_DIM)
    ang = pos * freqs[None, :]
    cos_t = jnp.tile(jnp.cos(ang), (1, LANES // half))
    sin_t = jnp.tile(jnp.concatenate([-jnp.sin(ang), jnp.sin(ang)], axis=1), (1, LANES // HEAD_DIM))
    gi = jnp.arange(LANES) // HEAD_DIM
    gmat = jnp.where(gi[:, None] == gi[None, :], 1.0 / HEAD_DIM, 0.0).astype(BF16)

    g_n, c_n, p_n = P["ssm_lambda_re"].shape[1], SSM_GROUP, SSM_STATE
    ns = g_n * p_n
    lr = P["ssm_lambda_re"][0].reshape(g_n, 1, p_n)
    li = P["ssm_lambda_im"][0].reshape(g_n, 1, p_n)
    ls = P["ssm_log_step"][0].reshape(g_n, 1, 1)
    brt = P["ssm_b_re"][0].transpose(0, 2, 1)
    bit = P["ssm_b_im"][0].transpose(0, 2, 1)
    ar, ai, bbr, bbi = _s5_params_fwd(lr, li, ls, brt, bit)
    a2 = jnp.concatenate([ar.reshape(1, ns), ai.reshape(1, ns)], axis=0)
    bfull = jnp.concatenate([_blockdiag(bbr), _blockdiag(bbi)], axis=1)
    cre_t = P["ssm_c_re"][0].transpose(0, 2, 1)
    cim_t = P["ssm_c_im"][0].transpose(0, 2, 1)
    cfull = jnp.concatenate([_blockdiag(cre_t), -_blockdiag(cim_t)], axis=0)

    ffn = lambda which, l: (row(P[which + "_norm"][l]),) + ffn_w[which, l]
    mix0, mix1, kvn = row(P["mix_norm"][0]), row(P["mix_norm"][1]), row(P["kv_norm"])
    kgain = jnp.tile(P["k_norm"].reshape(1, HEAD_DIM), (1, KVW // HEAD_DIM))
    qgain = jnp.tile(P["q_norm"].reshape(1, HEAD_DIM), (1, d // HEAD_DIM))
    sinks = P["attn_sinks"].reshape(1, -1)

    h0 = _embed(x, meta_full).reshape(rows, d)
    h1, ab_f1_0, g_wout, g_gu, g_d, g_kv = _ffn_fwd("ffn1_0_fwd", h0, *ffn("ffn1", 0), rider=_gather_rider(
        [shard("ssm_w_out", 0), shard("ffn2_w_gate_up", 0), shard("ffn2_w_down", 0), shard("w_kv")]))
    w_out, w_kv = colsharded(g_wout), rowsharded(g_kv)
    ffn_w["ffn2", 0] = (colsharded(g_gu), rowsharded(g_d))
    u = _proj_fwd("ssm_in_fwd", h1, mix0, w_in)
    y, xs = _s5_scan_fwd(u, bf(bfull), bf(cfull), a2, dvec, bsz)
    h2 = _glu_fwd(y, h1, w_out)
    h3, ab_f2_0, g_gu, g_d, g_q, g_o = _ffn_fwd("ffn2_0_fwd", h2, *ffn("ffn2", 0), rider=_gather_rider(
        [shard("ffn1_w_gate_up", 1), shard("ffn1_w_down", 1), shard("attn_w_q", 0), shard("attn_w_o", 0)]))
    w_q, w_o = rowsharded(g_q), rowsharded(g_o)
    ffn_w["ffn1", 1] = (colsharded(g_gu), rowsharded(g_d))
    kv = _proj_fwd("kv_fwd", h3, kvn, w_kv)
    k = _headrope_fwd("k_rope_fwd", kv, KVW, kgain, cos_t, sin_t, gmat, lp)
    h4, ab_f1_1, g_gu, g_d = _ffn_fwd("ffn1_1_fwd", h3, *ffn("ffn1", 1), rider=_gather_rider(
        [shard("ffn2_w_gate_up", 1), shard("ffn2_w_down", 1)]))
    ffn_w["ffn2", 1] = (colsharded(g_gu), rowsharded(g_d))
    q_raw = _proj_fwd("q_fwd", h4, mix1, w_q)
    q = _headrope_fwd("q_rope_fwd", q_raw, d, qgain, cos_t, sin_t, gmat, lp)
    r3 = lambda a: a.reshape(bsz, lp, a.shape[-1])
    o = _attn_fwd(r3(q), r3(k), r3(kv), sinks).reshape(rows, d)
    h5 = _lin_res_fwd("attn_out_fwd", o, w_o, h4)
    h6, ab_f2_1 = _ffn_fwd("ffn2_1_fwd", h5, *ffn("ffn2", 1))
    loss, dh6 = _loss(r3(h6), target)
    dh6 = dh6.reshape(rows, d)

    G = {"loss": loss[0, 0]}

    def ffn_back(name, which, l, h, ab, dout, rider=None):
        g, wgu, wd = ffn(which, l)
        dh, hn, dab, act, dg, *rode = _ffn_bwd(name, h, ab, dout, g, wgu, wd, rider=rider)
        parts = [_shard(_mm_tn(name + "_wgu", hn, dab), 1), _mm_tn_slots(name + "_wd", act, dout, 0.5)]
        return dh, dg, parts, rode

    swap_of = lambda parts: _swap_rider([p.reshape((4, 2) + p.shape[1:]) for p in parts])

    def pair_sums(tag, parts, theirs):
        return [_pair_sum("pair_sum_%s_%d" % (tag, k), p.reshape((4, 2) + p.shape[1:]), t, c_arr)
                for k, (p, t) in enumerate(zip(parts, theirs))]

    dh5, dg_f2_1, parts_a, _ = ffn_back("ffn2_1_bwd", "ffn2", 1, h5, ab_f2_1, dh6)
    do, dw_o, *theirs = _lin_bwd("attn_out_bwd", o, w_o, dh5, rider=swap_of(parts_a))
    sums_a = pair_sums("ffn2_1", parts_a, theirs)
    dq, dk, dv, dsinks = _attn_bwd(r3(q), r3(k), r3(kv), sinks, r3(o), r3(do))
    dq_raw, dqg = _headrope_bwd("q_rope_bwd", q_raw, d, dq.reshape(rows, d), qgain, cos_t, sin_t, gmat, lp)
    dh4, dg_mix1, dw_q = _proj_bwd("q_bwd", h4, mix1, w_q, dq_raw, dh5)
    dh3, dg_f1_1, parts_b, red_a = ffn_back("ffn1_1_bwd", "ffn1", 1, h3, ab_f1_1, dh4, rider=_scatter_rider(sums_a))
    dk_raw, dkg = _headrope_bwd("k_rope_bwd", kv, KVW, dk.reshape(rows, KVW), kgain, cos_t, sin_t, gmat, lp)
    dkv = _concat_cols("dkv_concat", dk_raw, dv.reshape(rows, KVW))
    dh3, dg_kvn, dw_kv, *theirs = _proj_bwd("kv_bwd", h3, kvn, w_kv, dkv, dh3, rider=swap_of(parts_b))
    sums_b = pair_sums("ffn1_1", parts_b, theirs)
    dh2, dg_f2_0, parts_c, red_b = ffn_back("ffn2_0_bwd", "ffn2", 0, h2, ab_f2_0, dh3, rider=_scatter_rider(sums_b))
    dy, dw_out, *theirs = _glu_bwd(y, dh2, w_out, rider=swap_of(parts_c))
    sums_c = pair_sums("ffn2_0", parts_c, theirs)
    ctfull = jnp.concatenate([_blockdiag(P["ssm_c_re"][0]), -_blockdiag(P["ssm_c_im"][0])], axis=1)
    btfull = jnp.concatenate([_blockdiag(bbr.transpose(0, 2, 1)), _blockdiag(bbi.transpose(0, 2, 1))], axis=0)
    du, gx, da, dd = _s5_scan_bwd(dy, u, xs, bf(ctfull), bf(btfull), a2, dvec, bsz)
    dbfull = _mm_tn("ssm_db", u, gx)
    dcfull = _mm_tn("ssm_dc", xs, dy)
    dh1, dg_mix0, dw_in = _proj_bwd("ssm_in_bwd", h1, mix0, w_in, du, dh2)
    dh0, dg_f1_0, parts_d, red_c = ffn_back("ffn1_0_bwd", "ffn1", 0, h0, ab_f1_0, dh1, rider=_scatter_rider(sums_c))
    slots = lambda g: g.reshape((N_DEV, g.shape[0] // N_DEV) + g.shape[1:])
    parts_d = parts_d + [slots(dw_in), dw_out, slots(dw_kv), slots(dw_q), slots(dw_o)]
    theirs = _run_rider("grad_swap_last", swap_of(parts_d))
    red_d = _run_rider("grad_scatter_last", _scatter_rider(pair_sums("last", parts_d, theirs)))
    both = lambda lo, hi: jnp.concatenate([lo, hi], axis=1)
    summed = {"ffn1_w_gate_up": both(red_d[0], red_b[0]), "ffn1_w_down": both(red_d[1], red_b[1]),
              "ffn2_w_gate_up": both(red_c[0], red_a[0]), "ffn2_w_down": both(red_c[1], red_a[1]),
              "ssm_w_in": red_d[2], "ssm_w_out": red_d[3], "w_kv": red_d[4], "attn_w_q": red_d[5],
              "attn_w_o": red_d[6]}

    dbbr = _diagblocks(dbfull[:, :ns], g_n)
    dbbi = _diagblocks(dbfull[:, ns:], g_n)
    dlr, dli, dls, dbrt, dbit = _s5_params_bwd(lr, li, ls, brt, bit, da[:, :ns].reshape(g_n, 1, p_n),
                                               da[:, ns:].reshape(g_n, 1, p_n), dbbr, dbbi)
    dh0 = r3(dh0)
    G["x"] = dh0[:, PAD:, :]
    G["meta_tokens"] = _meta_sum(dh0)
    G["ffn1_norm"] = jnp.concatenate([dg_f1_0, dg_f1_1], axis=0)
    G["ffn2_norm"] = jnp.concatenate([dg_f2_0, dg_f2_1], axis=0)
    G["mix_norm"] = jnp.concatenate([dg_mix0, dg_mix1], axis=0)
    G["ssm_lambda_re"] = dlr.reshape(1, g_n, p_n)
    G["ssm_lambda_im"] = dli.reshape(1, g_n, p_n)
    G["ssm_log_step"] = dls.reshape(1, g_n)
    G["ssm_b_re"] = dbrt.transpose(0, 2, 1)[None]
    G["ssm_b_im"] = dbit.transpose(0, 2, 1)[None]
    G["ssm_c_re"] = _diagblocks(dcfull[:ns], g_n).transpose(0, 2, 1)[None]
    G["ssm_c_im"] = -_diagblocks(dcfull[ns:], g_n).transpose(0, 2, 1)[None]
    G["ssm_d"] = dd
    G["kv_norm"] = dg_kvn.reshape(-1)
    G["k_norm"] = dkg[0, :HEAD_DIM]
    G["q_norm"] = dqg[:, :HEAD_DIM]
    G["attn_sinks"] = dsinks[:, :N_KV_HEADS * Q_PER_KV]
    return G, summed
```

```python
import functools
import math

import jax
import jax.numpy as jnp
from jax import lax
from jax.experimental import pallas as pl
from jax.experimental.pallas import tpu as pltpu

F32 = jnp.float32
BF16 = jnp.bfloat16

N_META = 16
PAD = 128
META0 = PAD - N_META
HEAD_DIM = 64
N_KV_HEADS = 4
Q_PER_KV = 4
SSM_GROUP = 16
SSM_STATE = 64
EPS = 1e-6
NEG_INF = -1e30
ROPE_THETA = 10000.0
ADAM_LR, ADAM_B1, ADAM_B2, ADAM_EPS, ADAM_WD, ADAM_STEP = 0.001, 0.9, 0.999, 1e-08, 0.01, 10
LANES = 128
PACK_W = 1024
VMEM_LIMIT = 56 * 1024 * 1024
MESH_AXES = ("x", "y", "c")
N_DEV = 8


def _cparams(sem=None):
    return pltpu.CompilerParams(dimension_semantics=sem, vmem_limit_bytes=VMEM_LIMIT)


def _row_tile(rows):
    for tm in (384, 256, 128, 64, 32, 16, 8):
        if rows % tm == 0:
            return tm
    raise ValueError(rows)


STREAM_BUDGET = 32 * 1024 * 1024


def _stream_tile(rows, bytes_per_row):
    for tm in range(rows, 0, -1):
        if rows % tm == 0 and (tm % 16 == 0 or tm == rows) and 2 * tm * bytes_per_row <= STREAM_BUDGET:
            return tm
    raise ValueError(rows)


TN_BUDGET = 44 * 1024 * 1024


def _tn_tile(rows, a, b, k1, tn):
    sa, sb = a.dtype.itemsize, b.dtype.itemsize
    for tm in (2112, 1056, 768, 528, 384, 256, 128, 64, 32, 16, 8):
        need = 2 * tm * (k1 * sa + tn * sb) + 3 * k1 * tn * 4 + tm * (k1 + tn) * 2
        if rows % tm == 0 and need <= TN_BUDGET:
            return tm
    raise ValueError(rows)


def _dot(a, b):
    return jnp.dot(a.astype(BF16), b.astype(BF16), preferred_element_type=F32)


def _dot_nt(a, b):
    return lax.dot_general(a.astype(BF16), b.astype(BF16), (((1,), (1,)), ((), ())), preferred_element_type=F32)


def _dot_tn(a, b):
    return lax.dot_general(a.astype(BF16), b.astype(BF16), (((0,), (0,)), ((), ())), preferred_element_type=F32)


def _rms(x, g):
    rstd = lax.rsqrt(jnp.mean(x * x, axis=-1, keepdims=True) + EPS)
    y = x * rstd
    return y * g, y, rstd


def _rms_bwd(dhn, y, rstd, g):
    dyn = dhn * g
    dx = rstd * (dyn - y * jnp.mean(dyn * y, axis=-1, keepdims=True))
    return dx, jnp.sum(dhn * y, axis=0, keepdims=True)


def _sigmoid(x):
    return 1.0 / (1.0 + jnp.exp(-x))


_GELU_C = math.sqrt(2.0 / math.pi)


def _gelu(y):
    t = jnp.tanh(_GELU_C * (y + 0.044715 * y * y * y))
    return 0.5 * y * (1.0 + t), t


def _gelu_grad(y, t):
    return 0.5 * (1.0 + t) + 0.5 * y * (1.0 - t * t) * _GELU_C * (1.0 + 3.0 * 0.044715 * y * y)


class _Rider:
    def __init__(self, ins, outs, sems, start, mid, finish):
        self.ins, self.outs, self.sems, self.start, self.mid, self.finish = ins, outs, sems, start, mid, finish


def _run_rider(name, rider):
    def kern(*refs):
        ni, no = len(rider.ins), len(rider.outs)
        parts = refs[:ni], refs[ni:ni + no], refs[ni + no:]
        rider.start(*parts)
        if rider.mid is not None:
            rider.mid(*parts)
        rider.finish(*parts)

    return pl.pallas_call(
        kern, name=name, out_shape=list(rider.outs), in_specs=[ANY] * len(rider.ins),
        out_specs=[ANY] * len(rider.outs), scratch_shapes=list(rider.sems),
    )(*rider.ins)


def _rowcall(name, body, rows, row_ins, const_ins, row_outs, acc_outs=(), tm=None, row_in_maps=None, rider=None):
    tm = tm or _row_tile(rows)
    steps = rows // tm
    in_specs = []
    for k, a in enumerate(row_ins):
        if row_in_maps is not None and row_in_maps[k] is not None:
            in_specs.append(pl.BlockSpec(*row_in_maps[k]))
        else:
            in_specs.append(pl.BlockSpec((tm, a.shape[1]), lambda i: (i, 0)))
    for a in const_ins:
        in_specs.append(pl.BlockSpec(a.shape, lambda i, nd=a.ndim: (0,) * nd, pipeline_mode=pl.Buffered(1)))
    out_shape, out_specs = [], []
    for w, dt in row_outs:
        out_shape.append(jax.ShapeDtypeStruct((rows, w), dt))
        out_specs.append(pl.BlockSpec((tm, w), lambda i: (i, 0)))
    for shp, dt in acc_outs:
        out_shape.append(jax.ShapeDtypeStruct(shp, dt))
        out_specs.append(pl.BlockSpec(shp, lambda i, nd=len(shp): (0,) * nd))

    if rider is None:
        def kern(*refs):
            body(pl.program_id(0), *refs)

        return pl.pallas_call(
            kern, name=name, grid=(steps,), in_specs=in_specs, out_specs=out_specs, out_shape=out_shape,
            compiler_params=_cparams(("arbitrary",)),
        )(*row_ins, *const_ins)

    n_in, n_out = len(in_specs), len(out_specs)
    r_in, r_out = len(rider.ins), len(rider.outs)

    def kern_r(*refs):
        step = pl.program_id(0)
        ins, rins = refs[:n_in], refs[n_in:n_in + r_in]
        outs = refs[n_in + r_in:n_in + r_in + n_out]
        routs = refs[n_in + r_in + n_out:n_in + r_in + n_out + r_out]
        sems = refs[n_in + r_in + n_out + r_out:]

        @pl.when(step == 0)
        def _():
            rider.start(rins, routs, sems)

        if rider.mid is not None:
            @pl.when(step == (3 * steps) // 4)
            def _():
                rider.mid(rins, routs, sems)

        body(step, *ins, *outs)

        @pl.when(step == steps - 1)
        def _():
            rider.finish(rins, routs, sems)

    return pl.pallas_call(
        kern_r, name=name, grid=(steps,), in_specs=in_specs + [ANY] * r_in, out_specs=out_specs + [ANY] * r_out,
        out_shape=out_shape + list(rider.outs), scratch_shapes=list(rider.sems),
        compiler_params=_cparams(("arbitrary",)),
    )(*row_ins, *const_ins, *rider.ins)


def _acc(step, ref, val):
    @pl.when(step == 0)
    def _():
        ref[...] = val

    @pl.when(step != 0)
    def _():
        ref[...] += val


def _embed(x, meta):
    bsz, seq, d = x.shape
    nb = seq // PAD + 1

    def kern(x_ref, m_ref, o_ref):
        i = pl.program_id(1)

        @pl.when(i == 0)
        def _():
            o_ref[0, 0:META0, :] = jnp.zeros((META0, d), F32)
            o_ref[0, META0:PAD, :] = m_ref[...]

        @pl.when(i != 0)
        def _():
            o_ref[0] = x_ref[0]

    return pl.pallas_call(
        kern, name="embed", grid=(bsz, nb),
        in_specs=[pl.BlockSpec((1, PAD, d), lambda b, i: (b, jnp.maximum(i - 1, 0), 0)),
                  pl.BlockSpec((N_META, d), lambda b, i: (0, 0))],
        out_specs=pl.BlockSpec((1, PAD, d), lambda b, i: (b, i, 0)),
        out_shape=jax.ShapeDtypeStruct((bsz, seq + PAD, d), F32),
        compiler_params=_cparams(("arbitrary", "arbitrary")),
    )(x, meta)


def _loss(h6, target):
    bsz, lp, d = h6.shape
    nb = lp // PAD

    def kern(h_ref, t_ref, l_ref, d_ref):
        b, i = pl.program_id(0), pl.program_id(1)

        @pl.when((b == 0) & (i == 0))
        def _():
            l_ref[...] = jnp.zeros_like(l_ref)

        @pl.when(i == 0)
        def _():
            d_ref[0] = jnp.zeros((PAD, d), F32)

        @pl.when(i != 0)
        def _():
            e = h_ref[0] - t_ref[0]
            d_ref[0] = e * (1.0 / d)
            l_ref[...] += 0.5 * jnp.sum(jnp.mean(e * e, axis=-1, keepdims=True))

    return pl.pallas_call(
        kern, name="loss", grid=(bsz, nb),
        in_specs=[pl.BlockSpec((1, PAD, d), lambda b, i: (b, i, 0)),
                  pl.BlockSpec((1, PAD, d), lambda b, i: (b, jnp.maximum(i - 1, 0), 0))],
        out_specs=[pl.BlockSpec((1, LANES), lambda b, i: (0, 0)),
                   pl.BlockSpec((1, PAD, d), lambda b, i: (b, i, 0))],
        out_shape=[jax.ShapeDtypeStruct((1, LANES), F32), jax.ShapeDtypeStruct((bsz, lp, d), F32)],
        compiler_params=_cparams(("arbitrary", "arbitrary")),
    )(h6, target)


def _meta_sum(dh0):
    bsz, lp, d = dh0.shape

    def kern(d_ref, o_ref):
        _acc(pl.program_id(0), o_ref, d_ref[0, META0:PAD, :])

    return pl.pallas_call(
        kern, name="meta_sum", grid=(bsz,),
        in_specs=[pl.BlockSpec((1, PAD, d), lambda b: (b, 0, 0))],
        out_specs=pl.BlockSpec((N_META, d), lambda b: (0, 0)),
        out_shape=jax.ShapeDtypeStruct((N_META, d), F32),
        compiler_params=_cparams(("arbitrary",)),
    )(dh0)


def _ffn_chunks(f):
    for n in (2, 4, 1, 11, 22):
        if f % n == 0 and (f // n) % LANES == 0:
            return n
    raise ValueError(f)


def _ffn_fwd(name, h, g, wgu, wd, rider=None):
    rows, d = h.shape
    f = wd.shape[0]
    nf = _ffn_chunks(f)
    tf = f // nf

    def body(step, h_ref, g_ref, wgu_ref, wd_ref, o_ref, ab_ref):
        hx = h_ref[...]
        hb = _rms(hx, g_ref[...])[0].astype(BF16)
        acc = jnp.zeros(hx.shape, F32)
        for j in range(nf):
            ga, ua = slice(j * tf, (j + 1) * tf), slice(f + j * tf, f + (j + 1) * tf)
            a = _dot(hb, wgu_ref[:, ga])
            b = _dot(hb, wgu_ref[:, ua])
            ab_ref[:, ga] = a.astype(BF16)
            ab_ref[:, ua] = b.astype(BF16)
            acc = acc + _dot(a * _sigmoid(a) * b, wd_ref[ga, :])
        o_ref[...] = hx + 0.5 * acc

    return _rowcall(name, body, rows, [h], [g, wgu, wd], [(d, F32), (2 * f, BF16)], rider=rider)


def _ffn_bwd(name, h, ab, dout, g, wgu, wd, rider=None):
    rows, d = h.shape
    f = wd.shape[0]
    nf = _ffn_chunks(f)
    tf = f // nf

    def body(step, h_ref, ab_ref, do_ref, g_ref, wgu_ref, wd_ref, dh_ref, hn_ref, dab_ref, act_ref, dg_ref):
        hx, dout_x, gx = h_ref[...], do_ref[...], g_ref[...]
        hn, y, rstd = _rms(hx, gx)
        hn_ref[...] = hn.astype(BF16)
        dhalf = (0.5 * dout_x).astype(BF16)
        dhn = jnp.zeros(hx.shape, F32)
        for j in range(nf):
            ga, ua = slice(j * tf, (j + 1) * tf), slice(f + j * tf, f + (j + 1) * tf)
            a = ab_ref[:, ga].astype(F32)
            b = ab_ref[:, ua].astype(F32)
            s = _sigmoid(a)
            silu = a * s
            act_ref[:, ga] = (silu * b).astype(BF16)
            dact = _dot_nt(dhalf, wd_ref[ga, :])
            da = (dact * b * (s + silu * (1.0 - s))).astype(BF16)
            db = (dact * silu).astype(BF16)
            dab_ref[:, ga] = da
            dab_ref[:, ua] = db
            dhn = dhn + _dot_nt(da, wgu_ref[:, ga]) + _dot_nt(db, wgu_ref[:, ua])
        dx, dg = _rms_bwd(dhn, y, rstd, gx)
        dh_ref[...] = dout_x + dx
        _acc(step, dg_ref, dg)

    return _rowcall(name, body, rows, [h, ab, dout], [g, wgu, wd],
                    [(d, F32), (d, BF16), (2 * f, BF16), (f, BF16)], [((1, d), F32)], rider=rider)


def _mm_tn(name, a, b, scale=1.0):
    rows, k1 = a.shape
    k2 = b.shape[1]
    tn = k2
    for cand in (512, 704, 1408, 1024):
        if k2 % cand == 0 and k1 * cand * 4 <= 6 * 1024 * 1024:
            tn = cand
    tm = _tn_tile(rows, a, b, k1, tn)
    steps = rows // tm

    def kern(a_ref, b_ref, o_ref):
        bx = b_ref[...]
        if scale != 1.0:
            bx = bx * scale
        _acc(pl.program_id(1), o_ref, _dot_tn(a_ref[...], bx))

    return pl.pallas_call(
        kern, name=name, grid=(k2 // tn, steps),
        in_specs=[pl.BlockSpec((tm, k1), lambda j, i: (i, 0)), pl.BlockSpec((tm, tn), lambda j, i: (i, j))],
        out_specs=pl.BlockSpec((k1, tn), lambda j, i: (0, j)),
        out_shape=jax.ShapeDtypeStruct((k1, k2), F32),
        compiler_params=_cparams(("arbitrary", "arbitrary")),
    )(a, b)


def _mm_tn_blockdiag(name, a, b, states_first):
    rows = a.shape[0]
    ka, kb = a.shape[1], b.shape[1]
    qa, qb = (ka // 4, kb // 2) if states_first else (ka // 2, kb // 4)
    tm = _tn_tile(rows, a, b, qa, qb)
    steps = rows // tm
    wide = lambda part, k: 2 * part + k
    amap = (lambda p, k, i: (i, wide(p, k))) if states_first else (lambda p, k, i: (i, k))
    bmap = (lambda p, k, i: (i, k)) if states_first else (lambda p, k, i: (i, wide(p, k)))
    omap = (lambda p, k, i: (wide(p, k), k)) if states_first else (lambda p, k, i: (k, wide(p, k)))

    def kern(a_ref, b_ref, o_ref):
        _acc(pl.program_id(2), o_ref, _dot_tn(a_ref[...], b_ref[...]))

    return pl.pallas_call(
        kern, name=name, grid=(2, 2, steps),
        in_specs=[pl.BlockSpec((tm, qa), amap), pl.BlockSpec((tm, qb), bmap)],
        out_specs=pl.BlockSpec((qa, qb), omap), out_shape=jax.ShapeDtypeStruct((ka, kb), F32),
        compiler_params=_cparams(("arbitrary", "arbitrary", "arbitrary")),
    )(a, b)


def _mm_tn_slots(name, a, b, scale):
    rows, k1 = a.shape
    k2 = b.shape[1]
    tn = 512 if k2 % 512 == 0 else k2
    sr = k1 // N_DEV
    tm = _tn_tile(rows, a, b, k1, tn)
    steps = rows // tm

    def kern(a_ref, b_ref, o_ref):
        bx = b_ref[...]
        if scale != 1.0:
            bx = bx * scale
        res = _dot_tn(a_ref[...], bx)
        step = pl.program_id(1)
        for s in range(N_DEV):
            _acc(step, o_ref.at[s], res[s * sr:(s + 1) * sr])

    return pl.pallas_call(
        kern, name=name, grid=(k2 // tn, steps),
        in_specs=[pl.BlockSpec((tm, k1), lambda j, i: (i, 0)), pl.BlockSpec((tm, tn), lambda j, i: (i, j))],
        out_specs=pl.BlockSpec((N_DEV, sr, tn), lambda j, i: (0, 0, j)),
        out_shape=jax.ShapeDtypeStruct((N_DEV, sr, k2), F32),
        compiler_params=_cparams(("arbitrary", "arbitrary")),
    )(a, b)


def _proj_fwd(name, h, g, w):
    rows = h.shape[0]

    def body(step, h_ref, g_ref, w_ref, o_ref):
        o_ref[...] = _dot(_rms(h_ref[...], g_ref[...])[0], w_ref[...])

    return _rowcall(name, body, rows, [h], [g, w], [(w.shape[1], F32)])[0]


def _proj_bwd(name, h, g, w, dy, dres, rider=None):
    rows, d = h.shape

    def body(step, h_ref, dy_ref, dr_ref, g_ref, w_ref, dh_ref, dg_ref, dw_ref):
        gx = g_ref[...]
        hn, y, rstd = _rms(h_ref[...], gx)
        dyx = dy_ref[...]
        dx, dg = _rms_bwd(_dot_nt(dyx, w_ref[...]), y, rstd, gx)
        dh_ref[...] = dr_ref[...] + dx
        _acc(step, dg_ref, dg)
        _acc(step, dw_ref, _dot_tn(hn, dyx))

    return _rowcall(name, body, rows, [h, dy, dres], [g, w], [(d, F32)], [((1, d), F32), (w.shape, F32)],
                    rider=rider)


def _lin_res_fwd(name, a, w, res):
    rows = a.shape[0]

    def body(step, a_ref, r_ref, w_ref, o_ref):
        o_ref[...] = r_ref[...] + _dot(a_ref[...], w_ref[...])

    return _rowcall(name, body, rows, [a, res], [w], [(w.shape[1], F32)])[0]


def _lin_bwd(name, a, w, dy, rider=None):
    rows, k = a.shape

    def body(step, a_ref, dy_ref, w_ref, da_ref, dw_ref):
        dyx = dy_ref[...]
        da_ref[...] = _dot_nt(dyx, w_ref[...])
        _acc(step, dw_ref, _dot_tn(a_ref[...], dyx))

    return _rowcall(name, body, rows, [a, dy], [w], [(k, F32)], [(w.shape, F32)], rider=rider)


def _s5_param_fn(lr, li, ls, brt, bit):
    step = jnp.exp(ls)
    mag = jnp.exp(lr * step)
    ar = mag * jnp.cos(li * step)
    ai = mag * jnp.sin(li * step)
    den = lr * lr + li * li
    nr, ni = ar - 1.0, ai
    cr = (nr * lr + ni * li) / den
    ci = (ni * lr - nr * li) / den
    return ar, ai, cr * brt - ci * bit, cr * bit + ci * brt


def _s5_params_fwd(lr, li, ls, brt, bit):
    def kern(lr_ref, li_ref, ls_ref, br_ref, bi_ref, ar_ref, ai_ref, bbr_ref, bbi_ref):
        ar, ai, bbr, bbi = _s5_param_fn(lr_ref[...], li_ref[...], ls_ref[...], br_ref[...], bi_ref[...])
        ar_ref[...], ai_ref[...], bbr_ref[...], bbi_ref[...] = ar, ai, bbr, bbi

    sd = jax.ShapeDtypeStruct
    return pl.pallas_call(
        kern, name="s5_params_fwd",
        out_shape=[sd(lr.shape, F32), sd(lr.shape, F32), sd(brt.shape, F32), sd(brt.shape, F32)],
    )(lr, li, ls, brt, bit)


def _s5_params_bwd(lr, li, ls, brt, bit, dar, dai, dbbr, dbbi):
    def kern(lr_ref, li_ref, ls_ref, br_ref, bi_ref, dar_ref, dai_ref, dbbr_ref, dbbi_ref,
             dlr_ref, dli_ref, dls_ref, dbr_ref, dbi_ref):
        _, vjp = jax.vjp(_s5_param_fn, lr_ref[...], li_ref[...], ls_ref[...], br_ref[...], bi_ref[...])
        dlr, dli, dls, dbr, dbi = vjp((dar_ref[...], dai_ref[...], dbbr_ref[...], dbbi_ref[...]))
        dlr_ref[...], dli_ref[...], dls_ref[...], dbr_ref[...], dbi_ref[...] = dlr, dli, dls, dbr, dbi

    sd = jax.ShapeDtypeStruct
    return pl.pallas_call(
        kern, name="s5_params_bwd",
        out_shape=[sd(lr.shape, F32), sd(lr.shape, F32), sd(ls.shape, F32), sd(brt.shape, F32), sd(brt.shape, F32)],
    )(lr, li, ls, brt, bit, dar, dai, dbbr, dbbi)


SCAN_LW = 512


def _scan_tables(a_ref, tab_ref, conj):
    ns = a_ref.shape[1]
    ar = jnp.broadcast_to(a_ref[0:1, :], (8, ns))
    ai = jnp.broadcast_to(a_ref[1:2, :], (8, ns))
    if conj:
        ai = -ai
    p1r, p1i = ar, ai
    p2r, p2i = p1r * p1r - p1i * p1i, 2.0 * p1r * p1i
    p4r, p4i = p2r * p2r - p2i * p2i, 2.0 * p2r * p2i
    row = lax.broadcasted_iota(jnp.int32, (8, ns), 0)
    e = row if not conj else 7 - row
    one, zero = jnp.ones((8, ns), F32), jnp.zeros((8, ns), F32)
    qr, qi = p1r, p1i
    for bit, (pr, pi) in ((1, (p1r, p1i)), (2, (p2r, p2i)), (4, (p4r, p4i))):
        sel = (e & bit) != 0
        fr, fi = jnp.where(sel, pr, one), jnp.where(sel, pi, zero)
        qr, qi = qr * fr - qi * fi, qr * fi + qi * fr
    for k, v in enumerate((p1r, p1i, p2r, p2i, p4r, p4i, qr, qi)):
        tab_ref[k] = v


def _scan_block(x_ref, tab_ref, carry_ref, t_rows, ns, reverse):
    ngrp = t_rows // 8
    row = lax.broadcasted_iota(jnp.int32, (8, SCAN_LW), 0)
    for lc in range(ns // SCAN_LW):
        lre = pl.ds(lc * SCAN_LW, SCAN_LW)
        lim = pl.ds(ns + lc * SCAN_LW, SCAN_LW)

        def group(k, carry, lre=lre, lim=lim):
            cr, ci = carry
            gi = (ngrp - 1 - k) if reverse else k
            rows = pl.ds(pl.multiple_of(gi * 8, 8), 8)
            vr, vi = x_ref[rows, lre], x_ref[rows, lim]
            for lvl, dsh in enumerate((1, 2, 4)):
                pr, pi = tab_ref[2 * lvl, :, lre], tab_ref[2 * lvl + 1, :, lre]
                if reverse:
                    keep = row < 8 - dsh
                    sr, si = pltpu.roll(vr, 8 - dsh, 0), pltpu.roll(vi, 8 - dsh, 0)
                else:
                    keep = row >= dsh
                    sr, si = pltpu.roll(vr, dsh, 0), pltpu.roll(vi, dsh, 0)
                sr, si = jnp.where(keep, sr, 0.0), jnp.where(keep, si, 0.0)
                vr, vi = vr + pr * sr - pi * si, vi + pr * si + pi * sr
            qr, qi = tab_ref[6, :, lre], tab_ref[7, :, lre]
            vr, vi = vr + qr * cr - qi * ci, vi + qr * ci + qi * cr
            x_ref[rows, lre], x_ref[rows, lim] = vr, vi
            edge = 0 if reverse else 7
            return (jnp.broadcast_to(vr[edge:edge + 1, :], (8, SCAN_LW)),
                    jnp.broadcast_to(vi[edge:edge + 1, :], (8, SCAN_LW)))

        cr, ci = lax.fori_loop(0, ngrp, group, (carry_ref[:, lre], carry_ref[:, lim]))
        carry_ref[:, lre], carry_ref[:, lim] = cr, ci


def _bd_expand(u, w_ref, x_ref, ns):
    hh, sh = u.shape[1] // 2, ns // 2
    ub = u.astype(BF16)
    for part in range(2):
        for k in range(2):
            cols = slice(part * ns + k * sh, part * ns + (k + 1) * sh)
            x_ref[:, cols] = jnp.dot(ub[:, k * hh:(k + 1) * hh], w_ref[k * hh:(k + 1) * hh, cols],
                                     preferred_element_type=F32)


def _bd_contract(x_ref, w_ref, ns):
    hh, sh = w_ref.shape[1] // 2, ns // 2
    halves = []
    for k in range(2):
        acc = None
        for part in range(2):
            rows = slice(part * ns + k * sh, part * ns + (k + 1) * sh)
            t = jnp.dot(x_ref[:, rows].astype(BF16), w_ref[rows, k * hh:(k + 1) * hh], preferred_element_type=F32)
            acc = t if acc is None else acc + t
        halves.append(acc)
    return jnp.concatenate(halves, axis=1)


def _scan_rows(lp):
    for t in (384, 256, 128):
        if lp % t == 0:
            return t
    raise ValueError(lp)


def _s5_scan_fwd(u, bfull, cfull, a2, dvec, bsz):
    rows, hw = u.shape
    ns = a2.shape[1]
    lp = rows // bsz
    t_rows = _scan_rows(lp)
    nc = lp // t_rows

    def kern(u_ref, b_ref, c_ref, a_ref, d_ref, y_ref, x_ref, tab_ref, carry_ref):
        c = pl.program_id(1)

        @pl.when((pl.program_id(0) == 0) & (c == 0))
        def _():
            _scan_tables(a_ref, tab_ref, conj=False)

        @pl.when(c == 0)
        def _():
            carry_ref[...] = jnp.zeros_like(carry_ref)

        ux = u_ref[...]
        _bd_expand(ux, b_ref, x_ref, ns)
        _scan_block(x_ref, tab_ref, carry_ref, t_rows, ns, reverse=False)
        y_ref[...] = _bd_contract(x_ref, c_ref, ns) + d_ref[...] * ux

    const = lambda shp: pl.BlockSpec(shp, lambda b, c: (0,) * len(shp), pipeline_mode=pl.Buffered(1))
    return pl.pallas_call(
        kern, name="s5_scan_fwd", grid=(bsz, nc),
        in_specs=[pl.BlockSpec((t_rows, hw), lambda b, c: (b * nc + c, 0)),
                  const(bfull.shape), const(cfull.shape), const(a2.shape), const(dvec.shape)],
        out_specs=[pl.BlockSpec((t_rows, hw), lambda b, c: (b * nc + c, 0)),
                   pl.BlockSpec((t_rows, 2 * ns), lambda b, c: (b * nc + c, 0))],
        out_shape=[jax.ShapeDtypeStruct((rows, hw), F32), jax.ShapeDtypeStruct((rows, 2 * ns), F32)],
        scratch_shapes=[pltpu.VMEM((8, 8, ns), F32), pltpu.VMEM((8, 2 * ns), F32)],
        compiler_params=_cparams(("arbitrary", "arbitrary")),
    )(u, bfull, cfull, a2, dvec)


def _s5_scan_bwd(dy, u, xs, ctfull, btfull, a2, dvec, bsz):
    rows, hw = u.shape
    ns = a2.shape[1]
    lp = rows // bsz
    t_rows = _scan_rows(lp)
    nc = lp // t_rows
    blk = lambda b, c: (b * nc + (nc - 1 - c), 0)

    def prev8(b, c):
        first = (b * nc + (nc - 1 - c)) * (t_rows // 8)
        return (jnp.maximum(first - 1, 0), 0)

    def kern(dy_ref, u_ref, x_ref, xp_ref, ct_ref, bt_ref, a_ref, d_ref, du_ref, gx_ref, da_ref, dd_ref,
             tab_ref, carry_ref):
        b, c = pl.program_id(0), pl.program_id(1)
        first = (b == 0) & (c == 0)

        @pl.when(first)
        def _():
            _scan_tables(a_ref, tab_ref, conj=True)

        @pl.when(c == 0)
        def _():
            carry_ref[...] = jnp.zeros_like(carry_ref)

        dyx, ux = dy_ref[...], u_ref[...]
        _bd_expand(dyx, ct_ref, gx_ref, ns)
        _scan_block(gx_ref, tab_ref, carry_ref, t_rows, ns, reverse=True)
        gx = gx_ref[...]
        du_ref[...] = _bd_contract(gx_ref, bt_ref, ns) + d_ref[...] * dyx
        xprev = pltpu.roll(x_ref[...], 1, 0)
        seq_start = c == nc - 1
        head = jnp.where(seq_start, 0.0, xp_ref[7:8, :])
        rid = lax.broadcasted_iota(jnp.int32, (t_rows, 1), 0)
        xprev = jnp.where(rid == 0, head, xprev)
        xr, xi, gr, gi = xprev[:, :ns], xprev[:, ns:], gx[:, :ns], gx[:, ns:]
        da = jnp.concatenate([jnp.sum(xr * gr + xi * gi, axis=0, keepdims=True),
                              jnp.sum(xr * gi - xi * gr, axis=0, keepdims=True)], axis=1)
        dd = jnp.sum(dyx * ux, axis=0, keepdims=True)

        @pl.when(first)
        def _():
            da_ref[...] = da
            dd_ref[...] = dd

        @pl.when(jnp.logical_not(first))
        def _():
            da_ref[...] += da
            dd_ref[...] += dd

    const = lambda shp: pl.BlockSpec(shp, lambda b, c: (0,) * len(shp), pipeline_mode=pl.Buffered(1))
    return pl.pallas_call(
        kern, name="s5_scan_bwd", grid=(bsz, nc),
        in_specs=[pl.BlockSpec((t_rows, hw), blk), pl.BlockSpec((t_rows, hw), blk),
                  pl.BlockSpec((t_rows, 2 * ns), blk), pl.BlockSpec((8, 2 * ns), prev8),
                  const(ctfull.shape), const(btfull.shape), const(a2.shape), const(dvec.shape)],
        out_specs=[pl.BlockSpec((t_rows, hw), blk), pl.BlockSpec((t_rows, 2 * ns), blk),
                   pl.BlockSpec((1, 2 * ns), lambda b, c: (0, 0)), pl.BlockSpec((1, hw), lambda b, c: (0, 0))],
        out_shape=[jax.ShapeDtypeStruct((rows, hw), F32), jax.ShapeDtypeStruct((rows, 2 * ns), F32),
                   jax.ShapeDtypeStruct((1, 2 * ns), F32), jax.ShapeDtypeStruct((1, hw), F32)],
        scratch_shapes=[pltpu.VMEM((8, 8, ns), F32), pltpu.VMEM((8, 2 * ns), F32)],
        compiler_params=_cparams(("arbitrary", "arbitrary")),
    )(dy, u, xs, xs, ctfull, btfull, a2, dvec)


def _glu_fwd(y, h1, wout):
    rows, d = h1.shape

    def body(step, y_ref, h_ref, w_ref, o_ref):
        z = _dot(_gelu(y_ref[...])[0], w_ref[...])
        o_ref[...] = h_ref[...] + z[:, :d] * _sigmoid(z[:, d:])

    return _rowcall("glu_fwd", body, rows, [y, h1], [wout], [(d, F32)])[0]


def _glu_bwd(y, dh2, wout, rider=None):
    rows, d = dh2.shape
    hw = y.shape[1]

    def body(step, y_ref, dh_ref, w_ref, dy_ref, dw_ref):
        yx, dh = y_ref[...], dh_ref[...]
        gl, t = _gelu(yx)
        z = _dot(gl, w_ref[...])
        za, sg = z[:, :d], _sigmoid(z[:, d:])
        dza = dh * sg
        dzg = dh * za * sg * (1.0 - sg)
        dgl = _dot_nt(dza, w_ref[:, :d]) + _dot_nt(dzg, w_ref[:, d:])
        dy_ref[...] = dgl * _gelu_grad(yx, t)
        for half, dz in enumerate((dza, dzg)):
            dw = _dot_tn(gl, dz)
            for s in range(N_DEV // 2):
                _acc(step, dw_ref.at[half * (N_DEV // 2) + s], dw[:, s * cw:(s + 1) * cw])

    cw = 2 * d // N_DEV
    return _rowcall("glu_bwd", body, rows, [y, dh2], [wout], [(hw, F32)], [((N_DEV, hw, cw), F32)], rider=rider)


def _gmean64(x2, gmat):
    hi = x2.astype(BF16)
    r1 = x2 - hi.astype(F32)
    mid = r1.astype(BF16)
    lo = (r1 - mid.astype(F32)).astype(BF16)
    outs = []
    for j in range(x2.shape[1] // LANES):
        sl = slice(j * LANES, (j + 1) * LANES)
        f = lambda p: jnp.dot(p[:, sl], gmat, preferred_element_type=F32)
        outs.append(f(hi) + f(mid) + f(lo))
    return outs[0] if len(outs) == 1 else jnp.concatenate(outs, axis=1)


def _swap32(x):
    w = x.shape[1]
    lane = lax.broadcasted_iota(jnp.int32, (1, w), 1)
    return jnp.where((lane & 32) == 0, pltpu.roll(x, w - 32, 1), pltpu.roll(x, 32, 1))


def _tile_lanes(t, w):
    reps = w // t.shape[1]
    return t if reps == 1 else jnp.concatenate([t] * reps, axis=1)


def _headrope_fwd(name, raw, w, gain, cos, sin, gmat, lp):
    rows = raw.shape[0]
    tm = _row_tile(lp)
    per = lp // tm

    def body(step, x_ref, c_ref, s_ref, g_ref, gm_ref, o_ref):
        x = x_ref[...]
        rstd = lax.rsqrt(_gmean64(x * x, gm_ref[...]) + EPS)
        z = x * rstd * g_ref[...]
        o_ref[...] = z * _tile_lanes(c_ref[...], w) + _swap32(z) * _tile_lanes(s_ref[...], w)

    maps = [((tm, w), lambda i: (i, 0)), ((tm, LANES), lambda i: (i % per, 0)), ((tm, LANES), lambda i: (i % per, 0))]
    return _rowcall(name, body, rows, [raw, cos, sin], [gain, gmat], [(w, F32)], tm=tm, row_in_maps=maps)[0]


def _headrope_bwd(name, raw, w, dout, gain, cos, sin, gmat, lp):
    rows = raw.shape[0]
    tm = _row_tile(lp)
    per = lp // tm

    def body(step, x_ref, do_ref, c_ref, s_ref, g_ref, gm_ref, dx_ref, dg_ref):
        x, dout_x, gx, gm = x_ref[...], do_ref[...], g_ref[...], gm_ref[...]
        rstd = lax.rsqrt(_gmean64(x * x, gm) + EPS)
        yn = x * rstd
        dz = dout_x * _tile_lanes(c_ref[...], w) + _swap32(dout_x * _tile_lanes(s_ref[...], w))
        dyn = dz * gx
        dx_ref[...] = rstd * (dyn - yn * _gmean64(dyn * yn, gm))
        dg = jnp.sum(dz * yn, axis=0, keepdims=True)
        sh = w // 2
        while sh >= HEAD_DIM:
            dg = dg + pltpu.roll(dg, sh, 1)
            sh //= 2
        _acc(step, dg_ref, dg)

    maps = [((tm, w), lambda i: (i, 0)), None, ((tm, LANES), lambda i: (i % per, 0)), ((tm, LANES), lambda i: (i % per, 0))]
    return _rowcall(name, body, rows, [raw, dout, cos, sin], [gain, gmat], [(w, F32)], [((1, w), F32)],
                    tm=tm, row_in_maps=maps)


KVW = N_KV_HEADS * HEAD_DIM
QB = 128


def _fold4(x):
    y = x + pltpu.roll(x, 128, 1)
    return y + pltpu.roll(y, 64, 1)


def _attn_scores(i, q_ref, k0_ref, kp_ref, kc_ref, sink_ref, h):
    lane = lax.broadcasted_iota(jnp.int32, (1, KVW), 1) // HEAD_DIM
    qh = q_ref[:, h * KVW:(h + 1) * KVW]
    qs = jnp.concatenate([jnp.where(lane == g, qh, 0.0) for g in range(Q_PER_KV)], axis=0).astype(BF16)
    hsel = lane == h
    kx = _expand_kv((k0_ref, kp_ref, kc_ref), hsel)
    scale = HEAD_DIM ** -0.5
    s0, sb = [_dot_nt(qs, k) * scale for k in kx]
    k0j = lax.broadcasted_iota(jnp.int32, (Q_PER_KV * QB, QB), 1)
    s0 = jnp.where(k0j >= META0, s0, NEG_INF)
    qi = lax.broadcasted_iota(jnp.int32, (Q_PER_KV * QB, 2 * QB), 0) % QB
    kj = lax.broadcasted_iota(jnp.int32, (Q_PER_KV * QB, 2 * QB), 1)
    in_prev = (kj < QB) & (kj > qi) & (i >= 2)
    in_cur = (kj >= QB) & (kj - QB <= qi)
    sb = jnp.where(in_prev | in_cur, sb, NEG_INF)
    rowg = lax.broadcasted_iota(jnp.int32, (Q_PER_KV * QB, 1), 0) // QB
    sink = jnp.zeros((Q_PER_KV * QB, 1), F32)
    for g in range(Q_PER_KV):
        sink = jnp.where(rowg == g, sink_ref[0, h * Q_PER_KV + g], sink)
    m = jnp.maximum(jnp.maximum(jnp.max(s0, axis=1, keepdims=True), jnp.max(sb, axis=1, keepdims=True)), sink)
    p0, pb, ps = jnp.exp(s0 - m), jnp.exp(sb - m), jnp.exp(sink - m)
    den = jnp.sum(p0, axis=1, keepdims=True) + jnp.sum(pb, axis=1, keepdims=True) + ps
    return qs, kx, (p0, pb), ps, den, lane, hsel


def _expand_kv(refs, hsel):
    x0, xp, xc = [_fold4(jnp.where(hsel, r[...], 0.0)).astype(BF16) for r in refs]
    return [x0, jnp.concatenate([xp, xc], axis=0)]


def _unstack(x, lane):
    out = jnp.where(lane == 0, x[0:QB], 0.0)
    for g in range(1, Q_PER_KV):
        out = out + jnp.where(lane == g, x[g * QB:(g + 1) * QB], 0.0)
    return out


def _attn_specs(nb, d):
    qspec = pl.BlockSpec((None, QB, d), lambda b, i: (b, i, 0))
    k0 = pl.BlockSpec((None, QB, KVW), lambda b, i: (b, 0, 0))
    kp = pl.BlockSpec((None, QB, KVW), lambda b, i: (b, jnp.maximum(i - 1, 0), 0))
    kc = pl.BlockSpec((None, QB, KVW), lambda b, i: (b, i, 0))
    v0 = pl.BlockSpec((None, QB, KVW), lambda b, i: (b, 0, 1))
    vp = pl.BlockSpec((None, QB, KVW), lambda b, i: (b, jnp.maximum(i - 1, 0), 1))
    vc = pl.BlockSpec((None, QB, KVW), lambda b, i: (b, i, 1))
    sink = pl.BlockSpec(memory_space=pltpu.SMEM)
    return qspec, [k0, kp, kc], [v0, vp, vc], sink


def _attn_fwd(q, k, kv, sinks):
    bsz, lp, d = q.shape
    nb = lp // QB
    qspec, kspecs, vspecs, sspec = _attn_specs(nb, d)

    def kern(q_ref, k0_ref, kp_ref, kc_ref, v0_ref, vp_ref, vc_ref, sink_ref, o_ref):
        i = pl.program_id(1)
        for h in range(N_KV_HEADS):
            qs, kx, ps3, psink, den, lane, hsel = _attn_scores(i, q_ref, k0_ref, kp_ref, kc_ref, sink_ref, h)
            vx = _expand_kv((v0_ref, vp_ref, vc_ref), hsel)
            o = _dot(ps3[0], vx[0]) + _dot(ps3[1], vx[1])
            o_ref[:, h * KVW:(h + 1) * KVW] = _unstack(o / den, lane)

    return pl.pallas_call(
        kern, name="attn_fwd", grid=(bsz, nb),
        in_specs=[qspec] + kspecs + vspecs + [sspec],
        out_specs=qspec, out_shape=jax.ShapeDtypeStruct((bsz, lp, d), F32),
        compiler_params=_cparams(("arbitrary", "arbitrary")),
    )(q, k, k, k, kv, kv, kv, sinks)


def _attn_bwd(q, k, kv, sinks, o, do):
    bsz, lp, d = q.shape
    nb = lp // QB
    qspec, kspecs, vspecs, sspec = _attn_specs(nb, d)
    full = pl.BlockSpec((None, lp, KVW), lambda b, i: (b, 0, 0))

    def kern(q_ref, k0_ref, kp_ref, kc_ref, v0_ref, vp_ref, vc_ref, sink_ref, o_ref, do_ref,
             dq_ref, dk_ref, dv_ref, ds_ref):
        b, i = pl.program_id(0), pl.program_id(1)

        @pl.when(i == 0)
        def _():
            dk_ref[...] = jnp.zeros_like(dk_ref)
            dv_ref[...] = jnp.zeros_like(dv_ref)

        @pl.when((b == 0) & (i == 0))
        def _():
            ds_ref[...] = jnp.zeros_like(ds_ref)

        lane128 = lax.broadcasted_iota(jnp.int32, (1, LANES), 1)
        rowg = lax.broadcasted_iota(jnp.int32, (Q_PER_KV * QB, 1), 0) // QB
        dk_acc = [jnp.zeros((QB, KVW), F32), jnp.zeros((2 * QB, KVW), F32)]
        dv_acc = [jnp.zeros((QB, KVW), F32), jnp.zeros((2 * QB, KVW), F32)]
        dsink = jnp.zeros((1, LANES), F32)
        for h in range(N_KV_HEADS):
            qs, kx, ps3, psink, den, lane, hsel = _attn_scores(i, q_ref, k0_ref, kp_ref, kc_ref, sink_ref, h)
            vx = _expand_kv((v0_ref, vp_ref, vc_ref), hsel)
            sl = slice(h * KVW, (h + 1) * KVW)
            doh, oh = do_ref[:, sl], o_ref[:, sl]
            dos = jnp.concatenate([jnp.where(lane == g, doh, 0.0) for g in range(Q_PER_KV)], axis=0)
            ost = jnp.concatenate([jnp.where(lane == g, oh, 0.0) for g in range(Q_PER_KV)], axis=0)
            delta = jnp.sum(dos * ost, axis=1, keepdims=True)
            inv = 1.0 / den
            dosb = dos.astype(BF16)
            dqs = jnp.zeros((Q_PER_KV * QB, KVW), F32)
            for n in range(2):
                pn = ps3[n] * inv
                ds = pn * (_dot_nt(dosb, vx[n]) - delta) * (HEAD_DIM ** -0.5)
                dqs = dqs + _dot(ds, kx[n])
                dk_acc[n] = dk_acc[n] + jnp.where(hsel, _fold4(_dot_tn(ds, qs)), 0.0)
                dv_acc[n] = dv_acc[n] + jnp.where(hsel, _fold4(_dot_tn(pn, dosb)), 0.0)
            dq_ref[:, sl] = _unstack(dqs, lane)
            dsk = -(psink * inv) * delta
            for g in range(Q_PER_KV):
                val = jnp.sum(jnp.where(rowg == g, dsk, 0.0), axis=0, keepdims=True)
                dsink = dsink + jnp.where(lane128 == h * Q_PER_KV + g, val, 0.0)
        ds_ref[...] += dsink
        r0 = pl.ds(0, QB)
        rp = pl.ds(pl.multiple_of(jnp.maximum(i - 1, 0) * QB, QB), QB)
        rc = pl.ds(pl.multiple_of(i * QB, QB), QB)
        for acc, ref in ((dk_acc, dk_ref), (dv_acc, dv_ref)):
            ref[r0, :] += acc[0]
            ref[rp, :] += acc[1][:QB]
            ref[rc, :] += acc[1][QB:]

    return pl.pallas_call(
        kern, name="attn_bwd", grid=(bsz, nb),
        in_specs=[qspec] + kspecs + vspecs + [sspec, qspec, qspec],
        out_specs=[qspec, full, full, pl.BlockSpec((1, LANES), lambda b, i: (0, 0))],
        out_shape=[jax.ShapeDtypeStruct((bsz, lp, d), F32), jax.ShapeDtypeStruct((bsz, lp, KVW), F32),
                   jax.ShapeDtypeStruct((bsz, lp, KVW), F32), jax.ShapeDtypeStruct((1, LANES), F32)],
        compiler_params=_cparams(("arbitrary", "arbitrary")),
    )(q, k, k, k, kv, kv, kv, sinks, o, do)


def _concat_cols(name, a, b):
    rows = a.shape[0]

    def body(step, a_ref, b_ref, o_ref):
        o_ref[...] = jnp.concatenate([a_ref[...], b_ref[...]], axis=1)

    return _rowcall(name, body, rows, [a, b], [], [(a.shape[1] + b.shape[1], F32)])[0]


def _adamw(name, w, m, v, parts):
    rows, wd = w.shape
    n = parts.shape[0]
    tm = _stream_tile(rows, wd * (7 * 4 + n * parts.dtype.itemsize))

    def kern(w_ref, m_ref, v_ref, p_ref, g_ref, d_ref, m2_ref, v2_ref):
        g = p_ref[0].astype(F32)
        for k in range(1, n):
            g = g + p_ref[k].astype(F32)
        m2 = ADAM_B1 * m_ref[...] + (1.0 - ADAM_B1) * g
        v2 = ADAM_B2 * v_ref[...] + (1.0 - ADAM_B2) * (g * g)
        mh = m2 / (1.0 - ADAM_B1 ** ADAM_STEP)
        vh = v2 / (1.0 - ADAM_B2 ** ADAM_STEP)
        g_ref[...] = g
        d_ref[...] = -ADAM_LR * (mh / (jnp.sqrt(vh) + ADAM_EPS) + ADAM_WD * w_ref[...])
        m2_ref[...] = m2
        v2_ref[...] = v2

    spec = pl.BlockSpec((tm, wd), lambda i: (i, 0))
    sd = jax.ShapeDtypeStruct((rows, wd), F32)
    return pl.pallas_call(
        kern, name=name, grid=(rows // tm,),
        in_specs=[spec, spec, spec, pl.BlockSpec((n, tm, wd), lambda i: (0, i, 0))],
        out_specs=[spec] * 4, out_shape=[sd] * 4,
        compiler_params=_cparams(("arbitrary",)),
    )(w, m, v, parts)


def _pair_sum(name, parts, theirs, my_c):
    n, _, rows, wd = parts.shape
    tm = _stream_tile(rows, wd * (4 + 4 + 2))

    def kern(c_ref, a_ref, b_ref, o_ref):
        o_ref[...] = (a_ref[...] + b_ref[...]).astype(BF16)

    return pl.pallas_call(
        kern, name=name,
        grid_spec=pltpu.PrefetchScalarGridSpec(
            num_scalar_prefetch=1, grid=(n, rows // tm),
            in_specs=[pl.BlockSpec((None, None, tm, wd), lambda k, i, c: (k, c[0], i, 0)),
                      pl.BlockSpec((None, tm, wd), lambda k, i, c: (k, i, 0))],
            out_specs=pl.BlockSpec((None, tm, wd), lambda k, i, c: (k, i, 0))),
        out_shape=jax.ShapeDtypeStruct((n, rows, wd), BF16), compiler_params=_cparams(("arbitrary", "arbitrary")),
    )(my_c, parts, theirs)


MESH = pl.DeviceIdType.MESH
ANY = pl.BlockSpec(memory_space=pl.ANY)


def _place():
    x, y, c = lax.axis_index("x"), lax.axis_index("y"), lax.axis_index("c")
    return x, y, c, [(1 - x, y), (x, 1 - y), (1 - x, 1 - y)]


def _gather_rider(shards):
    n = len(shards)

    def copy(refs, a, k, block, to, own=False):
        x_refs, out_refs, (send_sems, recv_sems, _) = refs
        px, py, pc = block
        slot = out_refs[a].at[4 * px + 2 * py + pc]
        return pltpu.make_async_remote_copy(
            src_ref=x_refs[a] if own else slot, dst_ref=slot,
            send_sem=send_sems.at[a, k], recv_sem=recv_sems.at[a, k], device_id=to, device_id_type=MESH)

    def local(refs, a):
        x, y, c, _ = _place()
        return pltpu.make_async_copy(refs[0][a], refs[1][a].at[4 * x + 2 * y + c], refs[2][2].at[a])

    def first(refs):
        x, y, c, chips = _place()
        out = []
        for a in range(n):
            out.append(copy(refs, a, 0, (x, y, c), (x, y, 1 - c), own=True))
            out += [copy(refs, a, 1 + j, (x, y, c), (*chip, c), own=True) for j, chip in enumerate(chips)]
        return out

    def passed(refs):
        x, y, c, chips = _place()
        return [copy(refs, a, 4 + j, (*chip, c), (x, y, 1 - c)) for j, chip in enumerate(chips) for a in range(n)]

    def start(*refs):
        for a in range(n):
            local(refs, a).start()
        for cp in first(refs):
            cp.start()

    def mid(*refs):
        x, y, c, chips = _place()
        fwd = passed(refs)
        for j, chip in enumerate(chips):
            for a in range(n):
                copy(refs, a, 1 + j, (*chip, c), (x, y, c)).wait_recv()
                fwd[j * n + a].start()

    def finish(*refs):
        x, y, c, chips = _place()
        for a in range(n):
            copy(refs, a, 0, (x, y, 1 - c), (x, y, c)).wait_recv()
            for j, chip in enumerate(chips):
                copy(refs, a, 4 + j, (*chip, 1 - c), (x, y, c)).wait_recv()
        for cp in first(refs) + passed(refs):
            cp.wait_send()
        for a in range(n):
            local(refs, a).wait()

    return _Rider(list(shards), [jax.ShapeDtypeStruct((N_DEV,) + s.shape, s.dtype) for s in shards],
                  [pltpu.SemaphoreType.DMA((n, 7)), pltpu.SemaphoreType.DMA((n, 7)), pltpu.SemaphoreType.DMA((n,))],
                  start, mid, finish)


def _swap_rider(parts):
    n = len(parts)

    def copies(p_refs, out_refs, sems):
        x, y, c, _ = _place()
        return [pltpu.make_async_remote_copy(
            src_ref=p_refs[a].at[:, 1 - c], dst_ref=out_refs[a], send_sem=sems[0].at[a], recv_sem=sems[1].at[a],
            device_id=(x, y, 1 - c), device_id_type=MESH) for a in range(n)]

    def start(*refs):
        for cp in copies(*refs):
            cp.start()

    def finish(*refs):
        for cp in copies(*refs):
            cp.wait()

    return _Rider(list(parts), [jax.ShapeDtypeStruct((p.shape[0],) + p.shape[2:], p.dtype) for p in parts],
                  [pltpu.SemaphoreType.DMA((n,)), pltpu.SemaphoreType.DMA((n,))], start, None, finish)


def _scatter_rider(sums):
    n = len(sums)

    def copy(refs, a, j, block):
        s_refs, out_refs, (send_sems, recv_sems, _) = refs
        x, y, c, chips = _place()
        px, py = chips[j]
        return pltpu.make_async_remote_copy(
            src_ref=s_refs[a].at[2 * px + py], dst_ref=out_refs[a].at[block],
            send_sem=send_sems.at[a, j], recv_sem=recv_sems.at[a, j], device_id=(px, py, c), device_id_type=MESH)

    def local(refs, a):
        x, y, c, _ = _place()
        return pltpu.make_async_copy(refs[0][a].at[2 * x + y], refs[1][a].at[2 * x + y], refs[2][2].at[a])

    def sends(refs):
        x, y, c, _ = _place()
        return [copy(refs, a, j, 2 * x + y) for j in range(3) for a in range(n)]

    def start(*refs):
        for a in range(n):
            local(refs, a).start()
        for cp in sends(refs):
            cp.start()

    def finish(*refs):
        x, y, c, chips = _place()
        for j, (px, py) in enumerate(chips):
            for a in range(n):
                copy(refs, a, j, 2 * px + py).wait_recv()
        for cp in sends(refs):
            cp.wait_send()
        for a in range(n):
            local(refs, a).wait()

    return _Rider(list(sums), [jax.ShapeDtypeStruct(s.shape, s.dtype) for s in sums],
                  [pltpu.SemaphoreType.DMA((n, 3)), pltpu.SemaphoreType.DMA((n, 3)), pltpu.SemaphoreType.DMA((n,))],
                  start, None, finish)


BIG = (("ffn1_w_gate_up", 2), ("ffn1_w_down", 1), ("ffn2_w_gate_up", 2), ("ffn2_w_down", 1), ("ssm_w_in", 1),
       ("ssm_w_out", 2), ("w_kv", 0), ("attn_w_q", 1), ("attn_w_o", 1))
SMALL = ("ffn1_norm", "mix_norm", "ffn2_norm", "ssm_lambda_re", "ssm_lambda_im", "ssm_b_re", "ssm_b_im",
         "ssm_c_re", "ssm_c_im", "ssm_log_step", "kv_norm", "k_norm", "q_norm", "attn_sinks")
COLS = (("meta_tokens", 1), ("ssm_d", 1))
WEIGHTS = ("meta_tokens", "ffn1_norm", "ffn1_w_gate_up", "ffn1_w_down", "mix_norm", "ffn2_norm", "ffn2_w_gate_up",
           "ffn2_w_down", "ssm_w_in", "ssm_lambda_re", "ssm_lambda_im", "ssm_b_re", "ssm_b_im", "ssm_c_re",
           "ssm_c_im", "ssm_log_step", "ssm_d", "ssm_w_out", "kv_norm", "w_kv", "k_norm", "attn_w_q", "q_norm",
           "attn_sinks", "attn_w_o")


def _rows_of(a, width):
    n = math.prod(a.shape)
    if n % width == 0:
        r = a.reshape(n // width, width)
    else:
        assert n < width
        r = jnp.pad(a.reshape(1, n), ((0, 0), (0, width - n)))
    return jnp.pad(r, ((0, (-r.shape[0]) % 8), (0, 0)))


def _pack_small(arrs, width):
    return jnp.concatenate([_rows_of(a.astype(F32), width) for a in arrs], axis=0)


def _unpack_small(buf, shapes, width):
    out, off = [], 0
    for shp in shapes:
        n = math.prod(shp)
        r = max(n // width, 1)
        out.append(buf[off:off + r].reshape(shp) if n % width == 0 else buf[off, :n].reshape(shp))
        off += r + (-r) % 8
    return out


def _shape2d(shp):
    return (math.prod(shp[:-1]), shp[-1])


def _unshard(g, axis):
    g = jnp.moveaxis(g, 0, axis)
    shp = g.shape
    return g.reshape(shp[:axis] + (shp[axis] * shp[axis + 1],) + shp[axis + 2:])


def _shard(full, axis):
    shp = full.shape
    g = full.reshape(shp[:axis] + (N_DEV, shp[axis] // N_DEV) + shp[axis + 1:])
    return jnp.moveaxis(g, axis, 0)


def _blockdiag(blocks):
    g, r, c = blocks.shape
    eye = jnp.eye(g, dtype=blocks.dtype)
    return (eye[:, None, :, None] * blocks[:, :, None, :]).reshape(g * r, g * c)


def _diagblocks(full, g):
    r, c = full.shape[0] // g, full.shape[1] // g
    f = full.reshape(g, r, g, c)
    idx = jnp.arange(g)
    return f[idx, :, idx, :]


def kernel(x, meta_tokens, ffn1_norm, ffn1_w_gate_up, ffn1_w_down, mix_norm, ffn2_norm, ffn2_w_gate_up, ffn2_w_down, ssm_w_in, ssm_lambda_re, ssm_lambda_im, ssm_b_re, ssm_b_im, ssm_c_re, ssm_c_im, ssm_log_step, ssm_d, ssm_w_out, kv_norm, w_kv, k_norm, attn_w_q, q_norm, attn_sinks, attn_w_o, loss_target, m_meta_tokens, m_ffn1_norm, m_ffn1_w_gate_up, m_ffn1_w_down, m_mix_norm, m_ffn2_norm, m_ffn2_w_gate_up, m_ffn2_w_down, m_ssm_w_in, m_ssm_lambda_re, m_ssm_lambda_im, m_ssm_b_re, m_ssm_b_im, m_ssm_c_re, m_ssm_c_im, m_ssm_log_step, m_ssm_d, m_ssm_w_out, m_kv_norm, m_w_kv, m_k_norm, m_attn_w_q, m_q_norm, m_attn_sinks, m_attn_w_o, v_meta_tokens, v_ffn1_norm, v_ffn1_w_gate_up, v_ffn1_w_down, v_mix_norm, v_ffn2_norm, v_ffn2_w_gate_up, v_ffn2_w_down, v_ssm_w_in, v_ssm_lambda_re, v_ssm_lambda_im, v_ssm_b_re, v_ssm_b_im, v_ssm_c_re, v_ssm_c_im, v_ssm_log_step, v_ssm_d, v_ssm_w_out, v_kv_norm, v_w_kv, v_k_norm, v_attn_w_q, v_q_norm, v_attn_sinks, v_attn_w_o):
    args = dict(locals())
    W = {n: args[n] for n in WEIGHTS}
    M = {n: args["m_" + n] for n in WEIGHTS}
    V = {n: args["v_" + n] for n in WEIGHTS}
    my_x, my_y, my_c = (lax.axis_index(a) for a in MESH_AXES)
    my_dev = 4 * my_x + 2 * my_y + my_c

    big_names = [n for n, _ in BIG]
    s2d = {n: _shape2d(W[n].shape) for n in big_names}
    col_w = W["meta_tokens"].shape[1]

    grads, summed = _local_step(x, loss_target, W, my_c.astype(jnp.int32).reshape(1))
    loss = lax.psum(grads.pop("loss"), MESH_AXES)
    grad_x = grads.pop("x")

    outs = [{}, {}, {}, {}]
    for n in big_names:
        r4 = _adamw("adamw_" + n, W[n].reshape(s2d[n]), M[n].reshape(s2d[n]), V[n].reshape(s2d[n]), summed[n])
        for k in range(4):
            outs[k][n] = r4[k].reshape(W[n].shape)

    small_names = list(SMALL) + [n for n, _ in COLS]
    small_shapes = [grads[n].shape for n in small_names]
    small_parts = _run_rider("gather_small_grads",
                             _gather_rider([_pack_small([grads[n] for n in small_names], PACK_W)]))[0]
    zero_cols = [jnp.zeros(grads[n].shape, F32) for n, _ in COLS]
    packs = lambda d: _pack_small([d[n] for n in SMALL] + zero_cols, PACK_W)
    r4 = _adamw("adamw_small", packs(W), packs(M), packs(V), small_parts)
    gsmall = None
    for k in range(4):
        un = dict(zip(small_names, _unpack_small(r4[k], small_shapes, PACK_W)))
        gsmall = un if k == 0 else gsmall
        outs[k].update({n: un[n] for n in SMALL})
    col_g = [lax.dynamic_slice_in_dim(gsmall[n], my_dev * W[n].shape[1], W[n].shape[1], axis=1) for n, _ in COLS]
    packc = lambda d: _pack_small([d[n] for n, _ in COLS], col_w)
    r4 = _adamw("adamw_cols", packc(W), packc(M), packc(V), _pack_small(col_g, col_w)[None])
    col_shapes = [W[n].shape for n, _ in COLS]
    for k in range(4):
        outs[k].update(dict(zip([n for n, _ in COLS], _unpack_small(r4[k], col_shapes, col_w))))

    res = [[outs[k][n] for n in WEIGHTS] for k in range(4)]
    return (loss, grad_x, *res[0], *res[1], *res[2], *res[3])


def _local_step(x, target, P, c_arr):
    bsz, seq, d = x.shape
    lp = seq + PAD
    rows = bsz * lp
    depth = P["ffn1_norm"].shape[0]
    assert depth == 2
    bf = lambda a: a.astype(BF16)
    row = lambda a: a.reshape(1, -1)

    def shard(n, l=None):
        a = P[n] if l is None else P[n][l]
        return bf(a.reshape(_shape2d(a.shape)))

    rowsharded = lambda g: g.reshape((g.shape[0] * g.shape[1],) + g.shape[2:])
    colsharded = lambda g: _unshard(g, 1)
    col_w = P["meta_tokens"].shape[1]
    g0 = _run_rider("gather_first", _gather_rider(
        [shard("ffn1_w_gate_up", 0), shard("ffn1_w_down", 0), shard("ssm_w_in", 0),
         _pack_small([P["meta_tokens"], P["ssm_d"]], col_w)]))
    ffn_w = {("ffn1", 0): (colsharded(g0[0]), rowsharded(g0[1]))}
    w_in = rowsharded(g0[2])
    meta_full = _unshard(g0[3][:, :N_META], 1)
    dvec = _unshard(g0[3][:, N_META:N_META + 1, :P["ssm_d"].shape[1]], 1)

    pos = (jnp.arange(lp, dtype=F32) - float(META0))[:, None]
    half = HEAD_DIM // 2
    freqs = ROPE_THETA ** (-jnp.arange(0, half, dtype=F32) * 2.0 / HEAD_DIM)
    ang = pos * freqs[None, :]
    cos_t = jnp.tile(jnp.cos(ang), (1, LANES // half))
    sin_t = jnp.tile(jnp.concatenate([-jnp.sin(ang), jnp.sin(ang)], axis=1), (1, LANES // HEAD_DIM))
    gi = jnp.arange(LANES) // HEAD_DIM
    gmat = jnp.where(gi[:, None] == gi[None, :], 1.0 / HEAD_DIM, 0.0).astype(BF16)

    g_n, c_n, p_n = P["ssm_lambda_re"].shape[1], SSM_GROUP, SSM_STATE
    ns = g_n * p_n
    lr = P["ssm_lambda_re"][0].reshape(g_n, 1, p_n)
    li = P["ssm_lambda_im"][0].reshape(g_n, 1, p_n)
    ls = P["ssm_log_step"][0].reshape(g_n, 1, 1)
    brt = P["ssm_b_re"][0].transpose(0, 2, 1)
    bit = P["ssm_b_im"][0].transpose(0, 2, 1)
    ar, ai, bbr, bbi = _s5_params_fwd(lr, li, ls, brt, bit)
    a2 = jnp.concatenate([ar.reshape(1, ns), ai.reshape(1, ns)], axis=0)
    bfull = jnp.concatenate([_blockdiag(bbr), _blockdiag(bbi)], axis=1)
    cre_t = P["ssm_c_re"][0].transpose(0, 2, 1)
    cim_t = P["ssm_c_im"][0].transpose(0, 2, 1)
    cfull = jnp.concatenate([_blockdiag(cre_t), -_blockdiag(cim_t)], axis=0)

    ffn = lambda which, l: (row(P[which + "_norm"][l]),) + ffn_w[which, l]
    mix0, mix1, kvn = row(P["mix_norm"][0]), row(P["mix_norm"][1]), row(P["kv_norm"])
    kgain = jnp.tile(P["k_norm"].reshape(1, HEAD_DIM), (1, KVW // HEAD_DIM))
    qgain = jnp.tile(P["q_norm"].reshape(1, HEAD_DIM), (1, d // HEAD_DIM))
    sinks = P["attn_sinks"].reshape(1, -1)

    h0 = _embed(x, meta_full).reshape(rows, d)
    h1, ab_f1_0, g_wout, g_gu, g_d, g_kv = _ffn_fwd("ffn1_0_fwd", h0, *ffn("ffn1", 0), rider=_gather_rider(
        [shard("ssm_w_out", 0), shard("ffn2_w_gate_up", 0), shard("ffn2_w_down", 0), shard("w_kv")]))
    w_out, w_kv = colsharded(g_wout), rowsharded(g_kv)
    ffn_w["ffn2", 0] = (colsharded(g_gu), rowsharded(g_d))
    u = _proj_fwd("ssm_in_fwd", h1, mix0, w_in)
    y, xs = _s5_scan_fwd(u, bf(bfull), bf(cfull), a2, dvec, bsz)
    h2 = _glu_fwd(y, h1, w_out)
    h3, ab_f2_0, g_gu, g_d, g_q, g_o = _ffn_fwd("ffn2_0_fwd", h2, *ffn("ffn2", 0), rider=_gather_rider(
        [shard("ffn1_w_gate_up", 1), shard("ffn1_w_down", 1), shard("attn_w_q", 0), shard("attn_w_o", 0)]))
    w_q, w_o = rowsharded(g_q), rowsharded(g_o)
    ffn_w["ffn1", 1] = (colsharded(g_gu), rowsharded(g_d))
    kv = _proj_fwd("kv_fwd", h3, kvn, w_kv)
    k = _headrope_fwd("k_rope_fwd", kv, KVW, kgain, cos_t, sin_t, gmat, lp)
    h4, ab_f1_1, g_gu, g_d = _ffn_fwd("ffn1_1_fwd", h3, *ffn("ffn1", 1), rider=_gather_rider(
        [shard("ffn2_w_gate_up", 1), shard("ffn2_w_down", 1)]))
    ffn_w["ffn2", 1] = (colsharded(g_gu), rowsharded(g_d))
    q_raw = _proj_fwd("q_fwd", h4, mix1, w_q)
    q = _headrope_fwd("q_rope_fwd", q_raw, d, qgain, cos_t, sin_t, gmat, lp)
    r3 = lambda a: a.reshape(bsz, lp, a.shape[-1])
    o = _attn_fwd(r3(q), r3(k), r3(kv), sinks).reshape(rows, d)
    h5 = _lin_res_fwd("attn_out_fwd", o, w_o, h4)
    h6, ab_f2_1 = _ffn_fwd("ffn2_1_fwd", h5, *ffn("ffn2", 1))
    loss, dh6 = _loss(r3(h6), target)
    dh6 = dh6.reshape(rows, d)

    G = {"loss": loss[0, 0]}

    def ffn_back(name, which, l, h, ab, dout, rider=None):
        g, wgu, wd = ffn(which, l)
        dh, hn, dab, act, dg, *rode = _ffn_bwd(name, h, ab, dout, g, wgu, wd, rider=rider)
        parts = [_shard(_mm_tn(name + "_wgu", hn, dab), 1), _mm_tn_slots(name + "_wd", act, dout, 0.5)]
        return dh, dg, parts, rode

    swap_of = lambda parts: _swap_rider([p.reshape((4, 2) + p.shape[1:]) for p in parts])

    def pair_sums(tag, parts, theirs):
        return [_pair_sum("pair_sum_%s_%d" % (tag, k), p.reshape((4, 2) + p.shape[1:]), t, c_arr)
                for k, (p, t) in enumerate(zip(parts, theirs))]

    dh5, dg_f2_1, parts_a, _ = ffn_back("ffn2_1_bwd", "ffn2", 1, h5, ab_f2_1, dh6)
    do, dw_o, *theirs = _lin_bwd("attn_out_bwd", o, w_o, dh5, rider=swap_of(parts_a))
    sums_a = pair_sums("ffn2_1", parts_a, theirs)
    dq, dk, dv, dsinks = _attn_bwd(r3(q), r3(k), r3(kv), sinks, r3(o), r3(do))
    dq_raw, dqg = _headrope_bwd("q_rope_bwd", q_raw, d, dq.reshape(rows, d), qgain, cos_t, sin_t, gmat, lp)
    dh4, dg_mix1, dw_q = _proj_bwd("q_bwd", h4, mix1, w_q, dq_raw, dh5)
    dh3, dg_f1_1, parts_b, red_a = ffn_back("ffn1_1_bwd", "ffn1", 1, h3, ab_f1_1, dh4, rider=_scatter_rider(sums_a))
    dk_raw, dkg = _headrope_bwd("k_rope_bwd", kv, KVW, dk.reshape(rows, KVW), kgain, cos_t, sin_t, gmat, lp)
    dkv = _concat_cols("dkv_concat", dk_raw, dv.reshape(rows, KVW))
    dh3, dg_kvn, dw_kv, *theirs = _proj_bwd("kv_bwd", h3, kvn, w_kv, dkv, dh3, rider=swap_of(parts_b))
    sums_b = pair_sums("ffn1_1", parts_b, theirs)
    dh2, dg_f2_0, parts_c, red_b = ffn_back("ffn2_0_bwd", "ffn2", 0, h2, ab_f2_0, dh3, rider=_scatter_rider(sums_b))
    dy, dw_out, *theirs = _glu_bwd(y, dh2, w_out, rider=swap_of(parts_c))
    sums_c = pair_sums("ffn2_0", parts_c, theirs)
    ctfull = jnp.concatenate([_blockdiag(P["ssm_c_re"][0]), -_blockdiag(P["ssm_c_im"][0])], axis=1)
    btfull = jnp.concatenate([_blockdiag(bbr.transpose(0, 2, 1)), _blockdiag(bbi.transpose(0, 2, 1))], axis=0)
    du, gx, da, dd = _s5_scan_bwd(dy, u, xs, bf(ctfull), bf(btfull), a2, dvec, bsz)
    dbfull = _mm_tn_blockdiag("ssm_db", u, gx, False)
    dcfull = _mm_tn_blockdiag("ssm_dc", xs, dy, True)
    dh1, dg_mix0, dw_in = _proj_bwd("ssm_in_bwd", h1, mix0, w_in, du, dh2)
    dh0, dg_f1_0, parts_d, red_c = ffn_back("ffn1_0_bwd", "ffn1", 0, h0, ab_f1_0, dh1, rider=_scatter_rider(sums_c))
    slots = lambda g: g.reshape((N_DEV, g.shape[0] // N_DEV) + g.shape[1:])
    parts_d = parts_d + [slots(dw_in), dw_out, slots(dw_kv), slots(dw_q), slots(dw_o)]
    theirs = _run_rider("grad_swap_last", swap_of(parts_d))
    red_d = _run_rider("grad_scatter_last", _scatter_rider(pair_sums("last", parts_d, theirs)))
    both = lambda lo, hi: jnp.concatenate([lo, hi], axis=1)
    summed = {"ffn1_w_gate_up": both(red_d[0], red_b[0]), "ffn1_w_down": both(red_d[1], red_b[1]),
              "ffn2_w_gate_up": both(red_c[0], red_a[0]), "ffn2_w_down": both(red_c[1], red_a[1]),
              "ssm_w_in": red_d[2], "ssm_w_out": red_d[3], "w_kv": red_d[4], "attn_w_q": red_d[5],
              "attn_w_o": red_d[6]}

    dbbr = _diagblocks(dbfull[:, :ns], g_n)
    dbbi = _diagblocks(dbfull[:, ns:], g_n)
    dlr, dli, dls, dbrt, dbit = _s5_params_bwd(lr, li, ls, brt, bit, da[:, :ns].reshape(g_n, 1, p_n),
                                               da[:, ns:].reshape(g_n, 1, p_n), dbbr, dbbi)
    dh0 = r3(dh0)
    G["x"] = dh0[:, PAD:, :]
    G["meta_tokens"] = _meta_sum(dh0)
    G["ffn1_norm"] = jnp.concatenate([dg_f1_0, dg_f1_1], axis=0)
    G["ffn2_norm"] = jnp.concatenate([dg_f2_0, dg_f2_1], axis=0)
    G["mix_norm"] = jnp.concatenate([dg_mix0, dg_mix1], axis=0)
    G["ssm_lambda_re"] = dlr.reshape(1, g_n, p_n)
    G["ssm_lambda_im"] = dli.reshape(1, g_n, p_n)
    G["ssm_log_step"] = dls.reshape(1, g_n)
    G["ssm_b_re"] = dbrt.transpose(0, 2, 1)[None]
    G["ssm_b_im"] = dbit.transpose(0, 2, 1)[None]
    G["ssm_c_re"] = _diagblocks(dcfull[:ns], g_n).transpose(0, 2, 1)[None]
    G["ssm_c_im"] = -_diagblocks(dcfull[ns:], g_n).transpose(0, 2, 1)[None]
    G["ssm_d"] = dd
    G["kv_norm"] = dg_kvn.reshape(-1)
    G["k_norm"] = dkg[0, :HEAD_DIM]
    G["q_norm"] = dqg[:, :HEAD_DIM]
    G["attn_sinks"] = dsinks[:, :N_KV_HEADS * Q_PER_KV]
    return G, summed
```

```python
import functools
import math

import jax
import jax.numpy as jnp
from jax import lax
from jax.experimental import pallas as pl
from jax.experimental.pallas import tpu as pltpu

F32 = jnp.float32
BF16 = jnp.bfloat16

N_META = 16
PAD = 128
META0 = PAD - N_META
HEAD_DIM = 64
N_KV_HEADS = 4
Q_PER_KV = 4
SSM_GROUP = 16
SSM_STATE = 64
EPS = 1e-6
NEG_INF = -1e30
ROPE_THETA = 10000.0
ADAM_LR, ADAM_B1, ADAM_B2, ADAM_EPS, ADAM_WD, ADAM_STEP = 0.001, 0.9, 0.999, 1e-08, 0.01, 10
LANES = 128
PACK_W = 1024
VMEM_LIMIT = 56 * 1024 * 1024
MESH_AXES = ("x", "y", "c")
N_DEV = 8


def _cparams(sem=None):
    return pltpu.CompilerParams(dimension_semantics=sem, vmem_limit_bytes=VMEM_LIMIT)


def _row_tile(rows):
    for tm in (384, 256, 128, 64, 32, 16, 8):
        if rows % tm == 0:
            return tm
    raise ValueError(rows)


STREAM_BUDGET = 32 * 1024 * 1024


def _stream_tile(rows, bytes_per_row):
    for tm in range(rows, 0, -1):
        if rows % tm == 0 and (tm % 16 == 0 or tm == rows) and 2 * tm * bytes_per_row <= STREAM_BUDGET:
            return tm
    raise ValueError(rows)


TN_BUDGET = 44 * 1024 * 1024


def _tn_tile(rows, a, b, k1, tn):
    sa, sb = a.dtype.itemsize, b.dtype.itemsize
    fits = lambda tm: 2 * tm * (k1 * sa + tn * sb) + 3 * k1 * tn * 4 + tm * (k1 + tn) * 2 <= TN_BUDGET
    divisors = [tm for tm in range(rows, 7, -8) if rows % tm == 0 and fits(tm)]
    whole = [tm for tm in divisors if tm % MXU_DIM == 0]
    if whole or divisors:
        return (whole or divisors)[0]
    raise ValueError(rows)


def _dot(a, b):
    return jnp.dot(a.astype(BF16), b.astype(BF16), preferred_element_type=F32)


def _dot_nt(a, b):
    return lax.dot_general(a.astype(BF16), b.astype(BF16), (((1,), (1,)), ((), ())), preferred_element_type=F32)


def _dot_tn(a, b):
    return lax.dot_general(a.astype(BF16), b.astype(BF16), (((0,), (0,)), ((), ())), preferred_element_type=F32)


def _rms(x, g):
    rstd = lax.rsqrt(jnp.mean(x * x, axis=-1, keepdims=True) + EPS)
    y = x * rstd
    return y * g, y, rstd


def _rms_bwd(dhn, y, rstd, g):
    dyn = dhn * g
    dx = rstd * (dyn - y * jnp.mean(dyn * y, axis=-1, keepdims=True))
    return dx, jnp.sum(dhn * y, axis=0, keepdims=True)


def _sigmoid(x):
    return 1.0 / (1.0 + jnp.exp(-x))


_GELU_C = math.sqrt(2.0 / math.pi)


def _gelu(y):
    t = jnp.tanh(_GELU_C * (y + 0.044715 * y * y * y))
    return 0.5 * y * (1.0 + t), t


def _gelu_grad(y, t):
    return 0.5 * (1.0 + t) + 0.5 * y * (1.0 - t * t) * _GELU_C * (1.0 + 3.0 * 0.044715 * y * y)


class _Rider:
    def __init__(self, ins, outs, sems, start, mid, finish):
        self.ins, self.outs, self.sems, self.start, self.mid, self.finish = ins, outs, sems, start, mid, finish


def _join_riders(r1, r2):
    ni, no, ns = len(r1.ins), len(r1.outs), len(r1.sems)

    def both(f1, f2):
        def phase(ins, outs, sems):
            if f1 is not None:
                f1(ins[:ni], outs[:no], sems[:ns])
            if f2 is not None:
                f2(ins[ni:], outs[no:], sems[ns:])
        return phase

    mid = both(r1.mid, r2.mid) if (r1.mid is not None or r2.mid is not None) else None
    return _Rider(r1.ins + r2.ins, r1.outs + r2.outs, r1.sems + r2.sems,
                  both(r1.start, r2.start), mid, both(r1.finish, r2.finish))


def _run_rider(name, rider):
    def kern(*refs):
        ni, no = len(rider.ins), len(rider.outs)
        parts = refs[:ni], refs[ni:ni + no], refs[ni + no:]
        rider.start(*parts)
        if rider.mid is not None:
            rider.mid(*parts)
        rider.finish(*parts)

    return pl.pallas_call(
        kern, name=name, out_shape=list(rider.outs), in_specs=[ANY] * len(rider.ins),
        out_specs=[ANY] * len(rider.outs), scratch_shapes=list(rider.sems),
    )(*rider.ins)


def _rowcall(name, body, rows, row_ins, const_ins, row_outs, acc_outs=(), tm=None, row_in_maps=None, rider=None):
    tm = tm or _row_tile(rows)
    steps = rows // tm
    in_specs = []
    for k, a in enumerate(row_ins):
        if row_in_maps is not None and row_in_maps[k] is not None:
            in_specs.append(pl.BlockSpec(*row_in_maps[k]))
        else:
            in_specs.append(pl.BlockSpec((tm, a.shape[1]), lambda i: (i, 0)))
    for a in const_ins:
        in_specs.append(pl.BlockSpec(a.shape, lambda i, nd=a.ndim: (0,) * nd, pipeline_mode=pl.Buffered(1)))
    out_shape, out_specs = [], []
    for w, dt in row_outs:
        out_shape.append(jax.ShapeDtypeStruct((rows, w), dt))
        out_specs.append(pl.BlockSpec((tm, w), lambda i: (i, 0)))
    for shp, dt in acc_outs:
        out_shape.append(jax.ShapeDtypeStruct(shp, dt))
        out_specs.append(pl.BlockSpec(shp, lambda i, nd=len(shp): (0,) * nd))

    if rider is None:
        def kern(*refs):
            body(pl.program_id(0), *refs)

        return pl.pallas_call(
            kern, name=name, grid=(steps,), in_specs=in_specs, out_specs=out_specs, out_shape=out_shape,
            compiler_params=_cparams(("arbitrary",)),
        )(*row_ins, *const_ins)

    n_in, n_out = len(in_specs), len(out_specs)
    r_in, r_out = len(rider.ins), len(rider.outs)

    def kern_r(*refs):
        step = pl.program_id(0)
        ins, rins = refs[:n_in], refs[n_in:n_in + r_in]
        outs = refs[n_in + r_in:n_in + r_in + n_out]
        routs = refs[n_in + r_in + n_out:n_in + r_in + n_out + r_out]
        sems = refs[n_in + r_in + n_out + r_out:]

        @pl.when(step == 0)
        def _():
            rider.start(rins, routs, sems)

        if rider.mid is not None:
            @pl.when(step == (3 * steps) // 4)
            def _():
                rider.mid(rins, routs, sems)

        body(step, *ins, *outs)

        @pl.when(step == steps - 1)
        def _():
            rider.finish(rins, routs, sems)

    return pl.pallas_call(
        kern_r, name=name, grid=(steps,), in_specs=in_specs + [ANY] * r_in, out_specs=out_specs + [ANY] * r_out,
        out_shape=out_shape + list(rider.outs), scratch_shapes=list(rider.sems),
        compiler_params=_cparams(("arbitrary",)),
    )(*row_ins, *const_ins, *rider.ins)


def _acc(step, ref, val):
    @pl.when(step == 0)
    def _():
        ref[...] = val

    @pl.when(step != 0)
    def _():
        ref[...] += val


def _embed(x, meta):
    bsz, seq, d = x.shape
    nb = seq // PAD + 1

    def kern(x_ref, m_ref, o_ref):
        i = pl.program_id(1)

        @pl.when(i == 0)
        def _():
            o_ref[0, 0:META0, :] = jnp.zeros((META0, d), F32)
            o_ref[0, META0:PAD, :] = m_ref[...]

        @pl.when(i != 0)
        def _():
            o_ref[0] = x_ref[0]

    return pl.pallas_call(
        kern, name="embed", grid=(bsz, nb),
        in_specs=[pl.BlockSpec((1, PAD, d), lambda b, i: (b, jnp.maximum(i - 1, 0), 0)),
                  pl.BlockSpec((N_META, d), lambda b, i: (0, 0))],
        out_specs=pl.BlockSpec((1, PAD, d), lambda b, i: (b, i, 0)),
        out_shape=jax.ShapeDtypeStruct((bsz, seq + PAD, d), F32),
        compiler_params=_cparams(("arbitrary", "arbitrary")),
    )(x, meta)


def _loss(h6, target):
    bsz, lp, d = h6.shape
    nb = lp // PAD

    def kern(h_ref, t_ref, l_ref, d_ref):
        b, i = pl.program_id(0), pl.program_id(1)

        @pl.when((b == 0) & (i == 0))
        def _():
            l_ref[...] = jnp.zeros_like(l_ref)

        @pl.when(i == 0)
        def _():
            d_ref[0] = jnp.zeros((PAD, d), F32)

        @pl.when(i != 0)
        def _():
            e = h_ref[0] - t_ref[0]
            d_ref[0] = e * (1.0 / d)
            l_ref[...] += 0.5 * jnp.sum(jnp.mean(e * e, axis=-1, keepdims=True))

    return pl.pallas_call(
        kern, name="loss", grid=(bsz, nb),
        in_specs=[pl.BlockSpec((1, PAD, d), lambda b, i: (b, i, 0)),
                  pl.BlockSpec((1, PAD, d), lambda b, i: (b, jnp.maximum(i - 1, 0), 0))],
        out_specs=[pl.BlockSpec((1, LANES), lambda b, i: (0, 0)),
                   pl.BlockSpec((1, PAD, d), lambda b, i: (b, i, 0))],
        out_shape=[jax.ShapeDtypeStruct((1, LANES), F32), jax.ShapeDtypeStruct((bsz, lp, d), F32)],
        compiler_params=_cparams(("arbitrary", "arbitrary")),
    )(h6, target)


def _meta_sum(dh0):
    bsz, lp, d = dh0.shape

    def kern(d_ref, o_ref):
        _acc(pl.program_id(0), o_ref, d_ref[0, META0:PAD, :])

    return pl.pallas_call(
        kern, name="meta_sum", grid=(bsz,),
        in_specs=[pl.BlockSpec((1, PAD, d), lambda b: (b, 0, 0))],
        out_specs=pl.BlockSpec((N_META, d), lambda b: (0, 0)),
        out_shape=jax.ShapeDtypeStruct((N_META, d), F32),
        compiler_params=_cparams(("arbitrary",)),
    )(dh0)


MXU_DIM = 256


def _ffn_chunks(f):
    unit = MXU_DIM if f % MXU_DIM == 0 else LANES
    assert f % unit == 0
    first = (f // unit + 1) // 2 * unit
    return [(0, first), (first, f)] if first < f else [(0, f)]


def _ffn_fwd(name, h, g, wgu, wd, rider=None):
    rows, d = h.shape
    f = wd.shape[0]
    chunks = _ffn_chunks(f)

    def body(step, h_ref, g_ref, wgu_ref, wd_ref, o_ref, ab_ref):
        hx = h_ref[...]
        hb = _rms(hx, g_ref[...])[0].astype(BF16)
        acc = jnp.zeros(hx.shape, F32)
        for lo, hi in chunks:
            ga, ua = slice(lo, hi), slice(f + lo, f + hi)
            a = _dot(hb, wgu_ref[:, ga])
            b = _dot(hb, wgu_ref[:, ua])
            ab_ref[:, ga] = a.astype(BF16)
            ab_ref[:, ua] = b.astype(BF16)
            acc = acc + _dot(a * _sigmoid(a) * b, wd_ref[ga, :])
        o_ref[...] = hx + 0.5 * acc

    return _rowcall(name, body, rows, [h], [g, wgu, wd], [(d, F32), (2 * f, BF16)], rider=rider)


def _ffn_bwd(name, h, ab, dout, g, wgu, wd, rider=None):
    rows, d = h.shape
    f = wd.shape[0]
    chunks = _ffn_chunks(f)

    def body(step, h_ref, ab_ref, do_ref, g_ref, wgu_ref, wd_ref, dh_ref, hn_ref, dab_ref, act_ref, dg_ref):
        hx, dout_x, gx = h_ref[...], do_ref[...], g_ref[...]
        hn, y, rstd = _rms(hx, gx)
        hn_ref[...] = hn.astype(BF16)
        dhalf = (0.5 * dout_x).astype(BF16)
        dhn = jnp.zeros(hx.shape, F32)
        for lo, hi in chunks:
            ga, ua = slice(lo, hi), slice(f + lo, f + hi)
            a = ab_ref[:, ga].astype(F32)
            b = ab_ref[:, ua].astype(F32)
            s = _sigmoid(a)
            silu = a * s
            act_ref[:, ga] = (silu * b).astype(BF16)
            dact = _dot_nt(dhalf, wd_ref[ga, :])
            da = (dact * b * (s + silu * (1.0 - s))).astype(BF16)
            db = (dact * silu).astype(BF16)
            dab_ref[:, ga] = da
            dab_ref[:, ua] = db
            dhn = dhn + _dot_nt(da, wgu_ref[:, ga]) + _dot_nt(db, wgu_ref[:, ua])
        dx, dg = _rms_bwd(dhn, y, rstd, gx)
        dh_ref[...] = dout_x + dx
        _acc(step, dg_ref, dg)

    return _rowcall(name, body, rows, [h, ab, dout], [g, wgu, wd],
                    [(d, F32), (d, BF16), (2 * f, BF16), (f, BF16)], [((1, d), F32)], rider=rider)


def _mm_tn(name, a, b, scale=1.0):
    rows, k1 = a.shape
    k2 = b.shape[1]
    tn = k2
    for cand in (512, 704, 1408, 1024):
        if k2 % cand == 0 and k1 * cand * 4 <= 6 * 1024 * 1024:
            tn = cand
    tm = _tn_tile(rows, a, b, k1, tn)
    steps = rows // tm

    def kern(a_ref, b_ref, o_ref):
        bx = b_ref[...]
        if scale != 1.0:
            bx = bx * scale
        _acc(pl.program_id(1), o_ref, _dot_tn(a_ref[...], bx))

    return pl.pallas_call(
        kern, name=name, grid=(k2 // tn, steps),
        in_specs=[pl.BlockSpec((tm, k1), lambda j, i: (i, 0)), pl.BlockSpec((tm, tn), lambda j, i: (i, j))],
        out_specs=pl.BlockSpec((k1, tn), lambda j, i: (0, j)),
        out_shape=jax.ShapeDtypeStruct((k1, k2), F32),
        compiler_params=_cparams(("arbitrary", "arbitrary")),
    )(a, b)


def _mm_tn_blockdiag(name, a, b, states_first):
    rows = a.shape[0]
    ka, kb = a.shape[1], b.shape[1]
    qa, qb = (ka // 4, kb // 2) if states_first else (ka // 2, kb // 4)
    tm = _tn_tile(rows, a, b, qa, qb)
    steps = rows // tm
    wide = lambda part, k: 2 * part + k
    amap = (lambda p, k, i: (i, wide(p, k))) if states_first else (lambda p, k, i: (i, k))
    bmap = (lambda p, k, i: (i, k)) if states_first else (lambda p, k, i: (i, wide(p, k)))
    omap = (lambda p, k, i: (wide(p, k), k)) if states_first else (lambda p, k, i: (k, wide(p, k)))

    def kern(a_ref, b_ref, o_ref):
        _acc(pl.program_id(2), o_ref, _dot_tn(a_ref[...], b_ref[...]))

    return pl.pallas_call(
        kern, name=name, grid=(2, 2, steps),
        in_specs=[pl.BlockSpec((tm, qa), amap), pl.BlockSpec((tm, qb), bmap)],
        out_specs=pl.BlockSpec((qa, qb), omap), out_shape=jax.ShapeDtypeStruct((ka, kb), F32),
        compiler_params=_cparams(("arbitrary", "arbitrary", "arbitrary")),
    )(a, b)


def _mm_tn_slots(name, a, b, scale):
    rows, k1 = a.shape
    k2 = b.shape[1]
    tn = 512 if k2 % 512 == 0 else k2
    sr = k1 // N_DEV
    tm = _tn_tile(rows, a, b, k1, tn)
    steps = rows // tm

    def kern(a_ref, b_ref, o_ref):
        bx = b_ref[...]
        if scale != 1.0:
            bx = bx * scale
        res = _dot_tn(a_ref[...], bx)
        step = pl.program_id(1)
        for s in range(N_DEV):
            _acc(step, o_ref.at[s], res[s * sr:(s + 1) * sr])

    return pl.pallas_call(
        kern, name=name, grid=(k2 // tn, steps),
        in_specs=[pl.BlockSpec((tm, k1), lambda j, i: (i, 0)), pl.BlockSpec((tm, tn), lambda j, i: (i, j))],
        out_specs=pl.BlockSpec((N_DEV, sr, tn), lambda j, i: (0, 0, j)),
        out_shape=jax.ShapeDtypeStruct((N_DEV, sr, k2), F32),
        compiler_params=_cparams(("arbitrary", "arbitrary")),
    )(a, b)


def _proj_fwd(name, h, g, w):
    rows = h.shape[0]

    def body(step, h_ref, g_ref, w_ref, o_ref):
        o_ref[...] = _dot(_rms(h_ref[...], g_ref[...])[0], w_ref[...])

    return _rowcall(name, body, rows, [h], [g, w], [(w.shape[1], F32)])[0]


def _proj_bwd(name, h, g, w, dy, dres, rider=None):
    rows, d = h.shape

    def body(step, h_ref, dy_ref, dr_ref, g_ref, w_ref, dh_ref, dg_ref, dw_ref):
        gx = g_ref[...]
        hn, y, rstd = _rms(h_ref[...], gx)
        dyx = dy_ref[...]
        dx, dg = _rms_bwd(_dot_nt(dyx, w_ref[...]), y, rstd, gx)
        dh_ref[...] = dr_ref[...] + dx
        _acc(step, dg_ref, dg)
        _acc(step, dw_ref, _dot_tn(hn, dyx))

    return _rowcall(name, body, rows, [h, dy, dres], [g, w], [(d, F32)], [((1, d), F32), (w.shape, F32)],
                    rider=rider)


def _lin_res_fwd(name, a, w, res):
    rows = a.shape[0]

    def body(step, a_ref, r_ref, w_ref, o_ref):
        o_ref[...] = r_ref[...] + _dot(a_ref[...], w_ref[...])

    return _rowcall(name, body, rows, [a, res], [w], [(w.shape[1], F32)])[0]


def _lin_bwd(name, a, w, dy, rider=None):
    rows, k = a.shape

    def body(step, a_ref, dy_ref, w_ref, da_ref, dw_ref):
        dyx = dy_ref[...]
        da_ref[...] = _dot_nt(dyx, w_ref[...])
        _acc(step, dw_ref, _dot_tn(a_ref[...], dyx))

    return _rowcall(name, body, rows, [a, dy], [w], [(k, F32)], [(w.shape, F32)], rider=rider)


def _s5_param_fn(lr, li, ls, brt, bit):
    step = jnp.exp(ls)
    mag = jnp.exp(lr * step)
    ar = mag * jnp.cos(li * step)
    ai = mag * jnp.sin(li * step)
    den = lr * lr + li * li
    nr, ni = ar - 1.0, ai
    cr = (nr * lr + ni * li) / den
    ci = (ni * lr - nr * li) / den
    return ar, ai, cr * brt - ci * bit, cr * bit + ci * brt


def _s5_params_fwd(lr, li, ls, brt, bit):
    def kern(lr_ref, li_ref, ls_ref, br_ref, bi_ref, ar_ref, ai_ref, bbr_ref, bbi_ref):
        ar, ai, bbr, bbi = _s5_param_fn(lr_ref[...], li_ref[...], ls_ref[...], br_ref[...], bi_ref[...])
        ar_ref[...], ai_ref[...], bbr_ref[...], bbi_ref[...] = ar, ai, bbr, bbi

    sd = jax.ShapeDtypeStruct
    return pl.pallas_call(
        kern, name="s5_params_fwd",
        out_shape=[sd(lr.shape, F32), sd(lr.shape, F32), sd(brt.shape, F32), sd(brt.shape, F32)],
    )(lr, li, ls, brt, bit)


def _s5_params_bwd(lr, li, ls, brt, bit, dar, dai, dbbr, dbbi):
    def kern(lr_ref, li_ref, ls_ref, br_ref, bi_ref, dar_ref, dai_ref, dbbr_ref, dbbi_ref,
             dlr_ref, dli_ref, dls_ref, dbr_ref, dbi_ref):
        _, vjp = jax.vjp(_s5_param_fn, lr_ref[...], li_ref[...], ls_ref[...], br_ref[...], bi_ref[...])
        dlr, dli, dls, dbr, dbi = vjp((dar_ref[...], dai_ref[...], dbbr_ref[...], dbbi_ref[...]))
        dlr_ref[...], dli_ref[...], dls_ref[...], dbr_ref[...], dbi_ref[...] = dlr, dli, dls, dbr, dbi

    sd = jax.ShapeDtypeStruct
    return pl.pallas_call(
        kern, name="s5_params_bwd",
        out_shape=[sd(lr.shape, F32), sd(lr.shape, F32), sd(ls.shape, F32), sd(brt.shape, F32), sd(brt.shape, F32)],
    )(lr, li, ls, brt, bit, dar, dai, dbbr, dbbi)


SCAN_LW = 512


def _scan_tables(a_ref, tab_ref, conj):
    ns = a_ref.shape[1]
    ar = jnp.broadcast_to(a_ref[0:1, :], (8, ns))
    ai = jnp.broadcast_to(a_ref[1:2, :], (8, ns))
    if conj:
        ai = -ai
    p1r, p1i = ar, ai
    p2r, p2i = p1r * p1r - p1i * p1i, 2.0 * p1r * p1i
    p4r, p4i = p2r * p2r - p2i * p2i, 2.0 * p2r * p2i
    row = lax.broadcasted_iota(jnp.int32, (8, ns), 0)
    e = row if not conj else 7 - row
    one, zero = jnp.ones((8, ns), F32), jnp.zeros((8, ns), F32)
    qr, qi = p1r, p1i
    for bit, (pr, pi) in ((1, (p1r, p1i)), (2, (p2r, p2i)), (4, (p4r, p4i))):
        sel = (e & bit) != 0
        fr, fi = jnp.where(sel, pr, one), jnp.where(sel, pi, zero)
        qr, qi = qr * fr - qi * fi, qr * fi + qi * fr
    for k, v in enumerate((p1r, p1i, p2r, p2i, p4r, p4i, qr, qi)):
        tab_ref[k] = v


def _scan_block(x_ref, tab_ref, carry_ref, t_rows, ns, reverse):
    ngrp = t_rows // 8
    row = lax.broadcasted_iota(jnp.int32, (8, SCAN_LW), 0)
    for lc in range(ns // SCAN_LW):
        lre = pl.ds(lc * SCAN_LW, SCAN_LW)
        lim = pl.ds(ns + lc * SCAN_LW, SCAN_LW)

        def group(k, carry, lre=lre, lim=lim):
            cr, ci = carry
            gi = (ngrp - 1 - k) if reverse else k
            rows = pl.ds(pl.multiple_of(gi * 8, 8), 8)
            vr, vi = x_ref[rows, lre], x_ref[rows, lim]
            for lvl, dsh in enumerate((1, 2, 4)):
                pr, pi = tab_ref[2 * lvl, :, lre], tab_ref[2 * lvl + 1, :, lre]
                if reverse:
                    keep = row < 8 - dsh
                    sr, si = pltpu.roll(vr, 8 - dsh, 0), pltpu.roll(vi, 8 - dsh, 0)
                else:
                    keep = row >= dsh
                    sr, si = pltpu.roll(vr, dsh, 0), pltpu.roll(vi, dsh, 0)
                sr, si = jnp.where(keep, sr, 0.0), jnp.where(keep, si, 0.0)
                vr, vi = vr + pr * sr - pi * si, vi + pr * si + pi * sr
            qr, qi = tab_ref[6, :, lre], tab_ref[7, :, lre]
            vr, vi = vr + qr * cr - qi * ci, vi + qr * ci + qi * cr
            x_ref[rows, lre], x_ref[rows, lim] = vr, vi
            edge = 0 if reverse else 7
            return (jnp.broadcast_to(vr[edge:edge + 1, :], (8, SCAN_LW)),
                    jnp.broadcast_to(vi[edge:edge + 1, :], (8, SCAN_LW)))

        cr, ci = lax.fori_loop(0, ngrp, group, (carry_ref[:, lre], carry_ref[:, lim]))
        carry_ref[:, lre], carry_ref[:, lim] = cr, ci


def _bd_expand(u, w_ref, x_ref, ns):
    hh, sh = u.shape[1] // 2, ns // 2
    ub = u.astype(BF16)
    for part in range(2):
        for k in range(2):
            cols = slice(part * ns + k * sh, part * ns + (k + 1) * sh)
            x_ref[:, cols] = jnp.dot(ub[:, k * hh:(k + 1) * hh], w_ref[k * hh:(k + 1) * hh, cols],
                                     preferred_element_type=F32)


def _bd_contract(x_ref, w_ref, ns):
    hh, sh = w_ref.shape[1] // 2, ns // 2
    halves = []
    for k in range(2):
        acc = None
        for part in range(2):
            rows = slice(part * ns + k * sh, part * ns + (k + 1) * sh)
            t = jnp.dot(x_ref[:, rows].astype(BF16), w_ref[rows, k * hh:(k + 1) * hh], preferred_element_type=F32)
            acc = t if acc is None else acc + t
        halves.append(acc)
    return jnp.concatenate(halves, axis=1)


def _scan_rows(lp):
    for t in (384, 256, 128):
        if lp % t == 0:
            return t
    raise ValueError(lp)


def _s5_scan_fwd(u, bfull, cfull, a2, dvec, bsz):
    rows, hw = u.shape
    ns = a2.shape[1]
    lp = rows // bsz
    t_rows = _scan_rows(lp)
    nc = lp // t_rows

    def kern(u_ref, b_ref, c_ref, a_ref, d_ref, y_ref, x_ref, tab_ref, carry_ref):
        c = pl.program_id(1)

        @pl.when((pl.program_id(0) == 0) & (c == 0))
        def _():
            _scan_tables(a_ref, tab_ref, conj=False)

        @pl.when(c == 0)
        def _():
            carry_ref[...] = jnp.zeros_like(carry_ref)

        ux = u_ref[...]
        _bd_expand(ux, b_ref, x_ref, ns)
        _scan_block(x_ref, tab_ref, carry_ref, t_rows, ns, reverse=False)
        y_ref[...] = _bd_contract(x_ref, c_ref, ns) + d_ref[...] * ux

    const = lambda shp: pl.BlockSpec(shp, lambda b, c: (0,) * len(shp), pipeline_mode=pl.Buffered(1))
    return pl.pallas_call(
        kern, name="s5_scan_fwd", grid=(bsz, nc),
        in_specs=[pl.BlockSpec((t_rows, hw), lambda b, c: (b * nc + c, 0)),
                  const(bfull.shape), const(cfull.shape), const(a2.shape), const(dvec.shape)],
        out_specs=[pl.BlockSpec((t_rows, hw), lambda b, c: (b * nc + c, 0)),
                   pl.BlockSpec((t_rows, 2 * ns), lambda b, c: (b * nc + c, 0))],
        out_shape=[jax.ShapeDtypeStruct((rows, hw), F32), jax.ShapeDtypeStruct((rows, 2 * ns), F32)],
        scratch_shapes=[pltpu.VMEM((8, 8, ns), F32), pltpu.VMEM((8, 2 * ns), F32)],
        compiler_params=_cparams(("arbitrary", "arbitrary")),
    )(u, bfull, cfull, a2, dvec)


def _s5_scan_bwd(dy, u, xs, ctfull, btfull, a2, dvec, bsz):
    rows, hw = u.shape
    ns = a2.shape[1]
    lp = rows // bsz
    t_rows = _scan_rows(lp)
    nc = lp // t_rows
    blk = lambda b, c: (b * nc + (nc - 1 - c), 0)

    def prev8(b, c):
        first = (b * nc + (nc - 1 - c)) * (t_rows // 8)
        return (jnp.maximum(first - 1, 0), 0)

    def kern(dy_ref, u_ref, x_ref, xp_ref, ct_ref, bt_ref, a_ref, d_ref, du_ref, gx_ref, da_ref, dd_ref,
             tab_ref, carry_ref):
        b, c = pl.program_id(0), pl.program_id(1)
        first = (b == 0) & (c == 0)

        @pl.when(first)
        def _():
            _scan_tables(a_ref, tab_ref, conj=True)

        @pl.when(c == 0)
        def _():
            carry_ref[...] = jnp.zeros_like(carry_ref)

        dyx, ux = dy_ref[...], u_ref[...]
        _bd_expand(dyx, ct_ref, gx_ref, ns)
        _scan_block(gx_ref, tab_ref, carry_ref, t_rows, ns, reverse=True)
        gx = gx_ref[...]
        du_ref[...] = _bd_contract(gx_ref, bt_ref, ns) + d_ref[...] * dyx
        xprev = pltpu.roll(x_ref[...], 1, 0)
        seq_start = c == nc - 1
        head = jnp.where(seq_start, 0.0, xp_ref[7:8, :])
        rid = lax.broadcasted_iota(jnp.int32, (t_rows, 1), 0)
        xprev = jnp.where(rid == 0, head, xprev)
        xr, xi, gr, gi = xprev[:, :ns], xprev[:, ns:], gx[:, :ns], gx[:, ns:]
        da = jnp.concatenate([jnp.sum(xr * gr + xi * gi, axis=0, keepdims=True),
                              jnp.sum(xr * gi - xi * gr, axis=0, keepdims=True)], axis=1)
        dd = jnp.sum(dyx * ux, axis=0, keepdims=True)

        @pl.when(first)
        def _():
            da_ref[...] = da
            dd_ref[...] = dd

        @pl.when(jnp.logical_not(first))
        def _():
            da_ref[...] += da
            dd_ref[...] += dd

    const = lambda shp: pl.BlockSpec(shp, lambda b, c: (0,) * len(shp), pipeline_mode=pl.Buffered(1))
    return pl.pallas_call(
        kern, name="s5_scan_bwd", grid=(bsz, nc),
        in_specs=[pl.BlockSpec((t_rows, hw), blk), pl.BlockSpec((t_rows, hw), blk),
                  pl.BlockSpec((t_rows, 2 * ns), blk), pl.BlockSpec((8, 2 * ns), prev8),
                  const(ctfull.shape), const(btfull.shape), const(a2.shape), const(dvec.shape)],
        out_specs=[pl.BlockSpec((t_rows, hw), blk), pl.BlockSpec((t_rows, 2 * ns), blk),
                   pl.BlockSpec((1, 2 * ns), lambda b, c: (0, 0)), pl.BlockSpec((1, hw), lambda b, c: (0, 0))],
        out_shape=[jax.ShapeDtypeStruct((rows, hw), F32), jax.ShapeDtypeStruct((rows, 2 * ns), F32),
                   jax.ShapeDtypeStruct((1, 2 * ns), F32), jax.ShapeDtypeStruct((1, hw), F32)],
        scratch_shapes=[pltpu.VMEM((8, 8, ns), F32), pltpu.VMEM((8, 2 * ns), F32)],
        compiler_params=_cparams(("arbitrary", "arbitrary")),
    )(dy, u, xs, xs, ctfull, btfull, a2, dvec)


def _glu_fwd(y, h1, wout):
    rows, d = h1.shape

    def body(step, y_ref, h_ref, w_ref, o_ref):
        z = _dot(_gelu(y_ref[...])[0], w_ref[...])
        o_ref[...] = h_ref[...] + z[:, :d] * _sigmoid(z[:, d:])

    return _rowcall("glu_fwd", body, rows, [y, h1], [wout], [(d, F32)])[0]


def _glu_bwd(y, dh2, wout, rider=None):
    rows, d = dh2.shape
    hw = y.shape[1]

    def body(step, y_ref, dh_ref, w_ref, dy_ref, dw_ref):
        yx, dh = y_ref[...], dh_ref[...]
        gl, t = _gelu(yx)
        z = _dot(gl, w_ref[...])
        za, sg = z[:, :d], _sigmoid(z[:, d:])
        dza = dh * sg
        dzg = dh * za * sg * (1.0 - sg)
        dgl = _dot_nt(dza, w_ref[:, :d]) + _dot_nt(dzg, w_ref[:, d:])
        dy_ref[...] = dgl * _gelu_grad(yx, t)
        for half, dz in enumerate((dza, dzg)):
            dw = _dot_tn(gl, dz)
            for s in range(N_DEV // 2):
                _acc(step, dw_ref.at[half * (N_DEV // 2) + s], dw[:, s * cw:(s + 1) * cw])

    cw = 2 * d // N_DEV
    return _rowcall("glu_bwd", body, rows, [y, dh2], [wout], [(hw, F32)], [((N_DEV, hw, cw), F32)], rider=rider)


def _gmean64(x2, gmat):
    hi = x2.astype(BF16)
    r1 = x2 - hi.astype(F32)
    mid = r1.astype(BF16)
    lo = (r1 - mid.astype(F32)).astype(BF16)
    outs = []
    for j in range(x2.shape[1] // LANES):
        sl = slice(j * LANES, (j + 1) * LANES)
        f = lambda p: jnp.dot(p[:, sl], gmat, preferred_element_type=F32)
        outs.append(f(hi) + f(mid) + f(lo))
    return outs[0] if len(outs) == 1 else jnp.concatenate(outs, axis=1)


def _swap32(x):
    w = x.shape[1]
    lane = lax.broadcasted_iota(jnp.int32, (1, w), 1)
    return jnp.where((lane & 32) == 0, pltpu.roll(x, w - 32, 1), pltpu.roll(x, 32, 1))


def _tile_lanes(t, w):
    reps = w // t.shape[1]
    return t if reps == 1 else jnp.concatenate([t] * reps, axis=1)


def _headrope_fwd(name, raw, w, gain, cos, sin, gmat, lp):
    rows = raw.shape[0]
    tm = _row_tile(lp)
    per = lp // tm

    def body(step, x_ref, c_ref, s_ref, g_ref, gm_ref, o_ref):
        x = x_ref[...]
        rstd = lax.rsqrt(_gmean64(x * x, gm_ref[...]) + EPS)
        z = x * rstd * g_ref[...]
        o_ref[...] = z * _tile_lanes(c_ref[...], w) + _swap32(z) * _tile_lanes(s_ref[...], w)

    maps = [((tm, w), lambda i: (i, 0)), ((tm, LANES), lambda i: (i % per, 0)), ((tm, LANES), lambda i: (i % per, 0))]
    return _rowcall(name, body, rows, [raw, cos, sin], [gain, gmat], [(w, F32)], tm=tm, row_in_maps=maps)[0]


def _headrope_bwd(name, raw, w, dout, gain, cos, sin, gmat, lp):
    rows = raw.shape[0]
    tm = _row_tile(lp)
    per = lp // tm

    def body(step, x_ref, do_ref, c_ref, s_ref, g_ref, gm_ref, dx_ref, dg_ref):
        x, dout_x, gx, gm = x_ref[...], do_ref[...], g_ref[...], gm_ref[...]
        rstd = lax.rsqrt(_gmean64(x * x, gm) + EPS)
        yn = x * rstd
        dz = dout_x * _tile_lanes(c_ref[...], w) + _swap32(dout_x * _tile_lanes(s_ref[...], w))
        dyn = dz * gx
        dx_ref[...] = rstd * (dyn - yn * _gmean64(dyn * yn, gm))
        dg = jnp.sum(dz * yn, axis=0, keepdims=True)
        sh = w // 2
        while sh >= HEAD_DIM:
            dg = dg + pltpu.roll(dg, sh, 1)
            sh //= 2
        _acc(step, dg_ref, dg)

    maps = [((tm, w), lambda i: (i, 0)), None, ((tm, LANES), lambda i: (i % per, 0)), ((tm, LANES), lambda i: (i % per, 0))]
    return _rowcall(name, body, rows, [raw, dout, cos, sin], [gain, gmat], [(w, F32)], [((1, w), F32)],
                    tm=tm, row_in_maps=maps)


KVW = N_KV_HEADS * HEAD_DIM
QB = 128


def _fold4(x):
    y = x + pltpu.roll(x, 128, 1)
    return y + pltpu.roll(y, 64, 1)


def _attn_scores(i, q_ref, k0_ref, kp_ref, kc_ref, sink_ref, h):
    lane = lax.broadcasted_iota(jnp.int32, (1, KVW), 1) // HEAD_DIM
    qh = q_ref[:, h * KVW:(h + 1) * KVW]
    qs = jnp.concatenate([jnp.where(lane == g, qh, 0.0) for g in range(Q_PER_KV)], axis=0).astype(BF16)
    hsel = lane == h
    kx = _expand_kv((k0_ref, kp_ref, kc_ref), hsel)
    scale = HEAD_DIM ** -0.5
    s0, sb = [_dot_nt(qs, k) * scale for k in kx]
    k0j = lax.broadcasted_iota(jnp.int32, (Q_PER_KV * QB, QB), 1)
    s0 = jnp.where(k0j >= META0, s0, NEG_INF)
    qi = lax.broadcasted_iota(jnp.int32, (Q_PER_KV * QB, 2 * QB), 0) % QB
    kj = lax.broadcasted_iota(jnp.int32, (Q_PER_KV * QB, 2 * QB), 1)
    in_prev = (kj < QB) & (kj > qi) & (i >= 2)
    in_cur = (kj >= QB) & (kj - QB <= qi)
    sb = jnp.where(in_prev | in_cur, sb, NEG_INF)
    rowg = lax.broadcasted_iota(jnp.int32, (Q_PER_KV * QB, 1), 0) // QB
    sink = jnp.zeros((Q_PER_KV * QB, 1), F32)
    for g in range(Q_PER_KV):
        sink = jnp.where(rowg == g, sink_ref[0, h * Q_PER_KV + g], sink)
    m = jnp.maximum(jnp.maximum(jnp.max(s0, axis=1, keepdims=True), jnp.max(sb, axis=1, keepdims=True)), sink)
    p0, pb, ps = jnp.exp(s0 - m), jnp.exp(sb - m), jnp.exp(sink - m)
    den = jnp.sum(p0, axis=1, keepdims=True) + jnp.sum(pb, axis=1, keepdims=True) + ps
    return qs, kx, (p0, pb), ps, den, lane, hsel


def _expand_kv(refs, hsel):
    x0, xp, xc = [_fold4(jnp.where(hsel, r[...], 0.0)).astype(BF16) for r in refs]
    return [x0, jnp.concatenate([xp, xc], axis=0)]


def _unstack(x, lane):
    out = jnp.where(lane == 0, x[0:QB], 0.0)
    for g in range(1, Q_PER_KV):
        out = out + jnp.where(lane == g, x[g * QB:(g + 1) * QB], 0.0)
    return out


def _attn_specs(nb, d):
    qspec = pl.BlockSpec((None, QB, d), lambda b, i: (b, i, 0))
    k0 = pl.BlockSpec((None, QB, KVW), lambda b, i: (b, 0, 0))
    kp = pl.BlockSpec((None, QB, KVW), lambda b, i: (b, jnp.maximum(i - 1, 0), 0))
    kc = pl.BlockSpec((None, QB, KVW), lambda b, i: (b, i, 0))
    v0 = pl.BlockSpec((None, QB, KVW), lambda b, i: (b, 0, 1))
    vp = pl.BlockSpec((None, QB, KVW), lambda b, i: (b, jnp.maximum(i - 1, 0), 1))
    vc = pl.BlockSpec((None, QB, KVW), lambda b, i: (b, i, 1))
    sink = pl.BlockSpec(memory_space=pltpu.SMEM)
    return qspec, [k0, kp, kc], [v0, vp, vc], sink


def _attn_fwd(q, k, kv, sinks):
    bsz, lp, d = q.shape
    nb = lp // QB
    qspec, kspecs, vspecs, sspec = _attn_specs(nb, d)

    def kern(q_ref, k0_ref, kp_ref, kc_ref, v0_ref, vp_ref, vc_ref, sink_ref, o_ref):
        i = pl.program_id(1)
        for h in range(N_KV_HEADS):
            qs, kx, ps3, psink, den, lane, hsel = _attn_scores(i, q_ref, k0_ref, kp_ref, kc_ref, sink_ref, h)
            vx = _expand_kv((v0_ref, vp_ref, vc_ref), hsel)
            o = _dot(ps3[0], vx[0]) + _dot(ps3[1], vx[1])
            o_ref[:, h * KVW:(h + 1) * KVW] = _unstack(o / den, lane)

    return pl.pallas_call(
        kern, name="attn_fwd", grid=(bsz, nb),
        in_specs=[qspec] + kspecs + vspecs + [sspec],
        out_specs=qspec, out_shape=jax.ShapeDtypeStruct((bsz, lp, d), F32),
        compiler_params=_cparams(("arbitrary", "arbitrary")),
    )(q, k, k, k, kv, kv, kv, sinks)


def _attn_bwd(q, k, kv, sinks, o, do):
    bsz, lp, d = q.shape
    nb = lp // QB
    qspec, kspecs, vspecs, sspec = _attn_specs(nb, d)
    full = pl.BlockSpec((None, lp, KVW), lambda b, i: (b, 0, 0))

    def kern(q_ref, k0_ref, kp_ref, kc_ref, v0_ref, vp_ref, vc_ref, sink_ref, o_ref, do_ref,
             dq_ref, dk_ref, dv_ref, ds_ref):
        b, i = pl.program_id(0), pl.program_id(1)

        @pl.when(i == 0)
        def _():
            dk_ref[...] = jnp.zeros_like(dk_ref)
            dv_ref[...] = jnp.zeros_like(dv_ref)

        @pl.when((b == 0) & (i == 0))
        def _():
            ds_ref[...] = jnp.zeros_like(ds_ref)

        lane128 = lax.broadcasted_iota(jnp.int32, (1, LANES), 1)
        rowg = lax.broadcasted_iota(jnp.int32, (Q_PER_KV * QB, 1), 0) // QB
        dk_acc = [jnp.zeros((QB, KVW), F32), jnp.zeros((2 * QB, KVW), F32)]
        dv_acc = [jnp.zeros((QB, KVW), F32), jnp.zeros((2 * QB, KVW), F32)]
        dsink = jnp.zeros((1, LANES), F32)
        for h in range(N_KV_HEADS):
            qs, kx, ps3, psink, den, lane, hsel = _attn_scores(i, q_ref, k0_ref, kp_ref, kc_ref, sink_ref, h)
            vx = _expand_kv((v0_ref, vp_ref, vc_ref), hsel)
            sl = slice(h * KVW, (h + 1) * KVW)
            doh, oh = do_ref[:, sl], o_ref[:, sl]
            dos = jnp.concatenate([jnp.where(lane == g, doh, 0.0) for g in range(Q_PER_KV)], axis=0)
            ost = jnp.concatenate([jnp.where(lane == g, oh, 0.0) for g in range(Q_PER_KV)], axis=0)
            delta = jnp.sum(dos * ost, axis=1, keepdims=True)
            inv = 1.0 / den
            dosb = dos.astype(BF16)
            dqs = jnp.zeros((Q_PER_KV * QB, KVW), F32)
            for n in range(2):
                pn = ps3[n] * inv
                ds = pn * (_dot_nt(dosb, vx[n]) - delta) * (HEAD_DIM ** -0.5)
                dqs = dqs + _dot(ds, kx[n])
                dk_acc[n] = dk_acc[n] + jnp.where(hsel, _fold4(_dot_tn(ds, qs)), 0.0)
                dv_acc[n] = dv_acc[n] + jnp.where(hsel, _fold4(_dot_tn(pn, dosb)), 0.0)
            dq_ref[:, sl] = _unstack(dqs, lane)
            dsk = -(psink * inv) * delta
            for g in range(Q_PER_KV):
                val = jnp.sum(jnp.where(rowg == g, dsk, 0.0), axis=0, keepdims=True)
                dsink = dsink + jnp.where(lane128 == h * Q_PER_KV + g, val, 0.0)
        ds_ref[...] += dsink
        r0 = pl.ds(0, QB)
        rp = pl.ds(pl.multiple_of(jnp.maximum(i - 1, 0) * QB, QB), QB)
        rc = pl.ds(pl.multiple_of(i * QB, QB), QB)
        for acc, ref in ((dk_acc, dk_ref), (dv_acc, dv_ref)):
            ref[r0, :] += acc[0]
            ref[rp, :] += acc[1][:QB]
            ref[rc, :] += acc[1][QB:]

    return pl.pallas_call(
        kern, name="attn_bwd", grid=(bsz, nb),
        in_specs=[qspec] + kspecs + vspecs + [sspec, qspec, qspec],
        out_specs=[qspec, full, full, pl.BlockSpec((1, LANES), lambda b, i: (0, 0))],
        out_shape=[jax.ShapeDtypeStruct((bsz, lp, d), F32), jax.ShapeDtypeStruct((bsz, lp, KVW), F32),
                   jax.ShapeDtypeStruct((bsz, lp, KVW), F32), jax.ShapeDtypeStruct((1, LANES), F32)],
        compiler_params=_cparams(("arbitrary", "arbitrary")),
    )(q, k, k, k, kv, kv, kv, sinks, o, do)


def _concat_cols(name, a, b):
    rows = a.shape[0]

    def body(step, a_ref, b_ref, o_ref):
        o_ref[...] = jnp.concatenate([a_ref[...], b_ref[...]], axis=1)

    return _rowcall(name, body, rows, [a, b], [], [(a.shape[1] + b.shape[1], F32)])[0]


def _adamw(name, w, m, v, parts):
    rows, wd = w.shape
    n = parts.shape[0]
    tm = _stream_tile(rows, wd * (7 * 4 + n * parts.dtype.itemsize))

    def kern(w_ref, m_ref, v_ref, p_ref, g_ref, d_ref, m2_ref, v2_ref):
        g = p_ref[0].astype(F32)
        for k in range(1, n):
            g = g + p_ref[k].astype(F32)
        m2 = ADAM_B1 * m_ref[...] + (1.0 - ADAM_B1) * g
        v2 = ADAM_B2 * v_ref[...] + (1.0 - ADAM_B2) * (g * g)
        mh = m2 / (1.0 - ADAM_B1 ** ADAM_STEP)
        vh = v2 / (1.0 - ADAM_B2 ** ADAM_STEP)
        g_ref[...] = g
        d_ref[...] = -ADAM_LR * (mh / (jnp.sqrt(vh) + ADAM_EPS) + ADAM_WD * w_ref[...])
        m2_ref[...] = m2
        v2_ref[...] = v2

    spec = pl.BlockSpec((tm, wd), lambda i: (i, 0))
    sd = jax.ShapeDtypeStruct((rows, wd), F32)
    return pl.pallas_call(
        kern, name=name, grid=(rows // tm,),
        in_specs=[spec, spec, spec, pl.BlockSpec((n, tm, wd), lambda i: (0, i, 0))],
        out_specs=[spec] * 4, out_shape=[sd] * 4,
        compiler_params=_cparams(("arbitrary",)),
    )(w, m, v, parts)


def _pair_sum(name, parts, theirs, my_c):
    n, _, rows, wd = parts.shape
    tm = _stream_tile(rows, wd * (4 + 4 + 2))

    def kern(c_ref, a_ref, b_ref, o_ref):
        o_ref[...] = (a_ref[...] + b_ref[...]).astype(BF16)

    return pl.pallas_call(
        kern, name=name,
        grid_spec=pltpu.PrefetchScalarGridSpec(
            num_scalar_prefetch=1, grid=(n, rows // tm),
            in_specs=[pl.BlockSpec((None, None, tm, wd), lambda k, i, c: (k, c[0], i, 0)),
                      pl.BlockSpec((None, tm, wd), lambda k, i, c: (k, i, 0))],
            out_specs=pl.BlockSpec((None, tm, wd), lambda k, i, c: (k, i, 0))),
        out_shape=jax.ShapeDtypeStruct((n, rows, wd), BF16), compiler_params=_cparams(("arbitrary", "arbitrary")),
    )(my_c, parts, theirs)


MESH = pl.DeviceIdType.MESH
ANY = pl.BlockSpec(memory_space=pl.ANY)


def _place():
    x, y, c = lax.axis_index("x"), lax.axis_index("y"), lax.axis_index("c")
    return x, y, c, [(1 - x, y), (x, 1 - y), (1 - x, 1 - y)]


def _gather_rider(shards):
    n = len(shards)

    def copy(refs, a, k, block, to, own=False):
        x_refs, out_refs, (send_sems, recv_sems, _) = refs
        px, py, pc = block
        slot = out_refs[a].at[4 * px + 2 * py + pc]
        return pltpu.make_async_remote_copy(
            src_ref=x_refs[a] if own else slot, dst_ref=slot,
            send_sem=send_sems.at[a, k], recv_sem=recv_sems.at[a, k], device_id=to, device_id_type=MESH)

    def local(refs, a):
        x, y, c, _ = _place()
        return pltpu.make_async_copy(refs[0][a], refs[1][a].at[4 * x + 2 * y + c], refs[2][2].at[a])

    def first(refs):
        x, y, c, chips = _place()
        out = []
        for a in range(n):
            out.append(copy(refs, a, 0, (x, y, c), (x, y, 1 - c), own=True))
            out += [copy(refs, a, 1 + j, (x, y, c), (*chip, c), own=True) for j, chip in enumerate(chips)]
        return out

    def passed(refs):
        x, y, c, chips = _place()
        return [copy(refs, a, 4 + j, (*chip, c), (x, y, 1 - c)) for j, chip in enumerate(chips) for a in range(n)]

    def start(*refs):
        for a in range(n):
            local(refs, a).start()
        for cp in first(refs):
            cp.start()

    def mid(*refs):
        x, y, c, chips = _place()
        fwd = passed(refs)
        for j, chip in enumerate(chips):
            for a in range(n):
                copy(refs, a, 1 + j, (*chip, c), (x, y, c)).wait_recv()
                fwd[j * n + a].start()

    def finish(*refs):
        x, y, c, chips = _place()
        for a in range(n):
            copy(refs, a, 0, (x, y, 1 - c), (x, y, c)).wait_recv()
            for j, chip in enumerate(chips):
                copy(refs, a, 4 + j, (*chip, 1 - c), (x, y, c)).wait_recv()
        for cp in first(refs) + passed(refs):
            cp.wait_send()
        for a in range(n):
            local(refs, a).wait()

    return _Rider(list(shards), [jax.ShapeDtypeStruct((N_DEV,) + s.shape, s.dtype) for s in shards],
                  [pltpu.SemaphoreType.DMA((n, 7)), pltpu.SemaphoreType.DMA((n, 7)), pltpu.SemaphoreType.DMA((n,))],
                  start, mid, finish)


def _swap_rider(parts):
    n = len(parts)

    def copies(p_refs, out_refs, sems):
        x, y, c, _ = _place()
        return [pltpu.make_async_remote_copy(
            src_ref=p_refs[a].at[:, 1 - c], dst_ref=out_refs[a], send_sem=sems[0].at[a], recv_sem=sems[1].at[a],
            device_id=(x, y, 1 - c), device_id_type=MESH) for a in range(n)]

    def start(*refs):
        for cp in copies(*refs):
            cp.start()

    def finish(*refs):
        for cp in copies(*refs):
            cp.wait()

    return _Rider(list(parts), [jax.ShapeDtypeStruct((p.shape[0],) + p.shape[2:], p.dtype) for p in parts],
                  [pltpu.SemaphoreType.DMA((n,)), pltpu.SemaphoreType.DMA((n,))], start, None, finish)


def _scatter_rider(sums):
    n = len(sums)

    def copy(refs, a, j, block):
        s_refs, out_refs, (send_sems, recv_sems, _) = refs
        x, y, c, chips = _place()
        px, py = chips[j]
        return pltpu.make_async_remote_copy(
            src_ref=s_refs[a].at[2 * px + py], dst_ref=out_refs[a].at[block],
            send_sem=send_sems.at[a, j], recv_sem=recv_sems.at[a, j], device_id=(px, py, c), device_id_type=MESH)

    def local(refs, a):
        x, y, c, _ = _place()
        return pltpu.make_async_copy(refs[0][a].at[2 * x + y], refs[1][a].at[2 * x + y], refs[2][2].at[a])

    def sends(refs):
        x, y, c, _ = _place()
        return [copy(refs, a, j, 2 * x + y) for j in range(3) for a in range(n)]

    def start(*refs):
        for a in range(n):
            local(refs, a).start()
        for cp in sends(refs):
            cp.start()

    def finish(*refs):
        x, y, c, chips = _place()
        for j, (px, py) in enumerate(chips):
            for a in range(n):
                copy(refs, a, j, 2 * px + py).wait_recv()
        for cp in sends(refs):
            cp.wait_send()
        for a in range(n):
            local(refs, a).wait()

    return _Rider(list(sums), [jax.ShapeDtypeStruct(s.shape, s.dtype) for s in sums],
                  [pltpu.SemaphoreType.DMA((n, 3)), pltpu.SemaphoreType.DMA((n, 3)), pltpu.SemaphoreType.DMA((n,))],
                  start, None, finish)


BIG = (("ffn1_w_gate_up", 2), ("ffn1_w_down", 1), ("ffn2_w_gate_up", 2), ("ffn2_w_down", 1), ("ssm_w_in", 1),
       ("ssm_w_out", 2), ("w_kv", 0), ("attn_w_q", 1), ("attn_w_o", 1))
SMALL = ("ffn1_norm", "mix_norm", "ffn2_norm", "ssm_lambda_re", "ssm_lambda_im", "ssm_b_re", "ssm_b_im",
         "ssm_c_re", "ssm_c_im", "ssm_log_step", "kv_norm", "k_norm", "q_norm", "attn_sinks")
COLS = (("meta_tokens", 1), ("ssm_d", 1))
WEIGHTS = ("meta_tokens", "ffn1_norm", "ffn1_w_gate_up", "ffn1_w_down", "mix_norm", "ffn2_norm", "ffn2_w_gate_up",
           "ffn2_w_down", "ssm_w_in", "ssm_lambda_re", "ssm_lambda_im", "ssm_b_re", "ssm_b_im", "ssm_c_re",
           "ssm_c_im", "ssm_log_step", "ssm_d", "ssm_w_out", "kv_norm", "w_kv", "k_norm", "attn_w_q", "q_norm",
           "attn_sinks", "attn_w_o")


def _rows_of(a, width):
    n = math.prod(a.shape)
    if n % width == 0:
        r = a.reshape(n // width, width)
    else:
        assert n < width
        r = jnp.pad(a.reshape(1, n), ((0, 0), (0, width - n)))
    return jnp.pad(r, ((0, (-r.shape[0]) % 8), (0, 0)))


def _pack_small(arrs, width):
    return jnp.concatenate([_rows_of(a.astype(F32), width) for a in arrs], axis=0)


def _unpack_small(buf, shapes, width):
    out, off = [], 0
    for shp in shapes:
        n = math.prod(shp)
        r = max(n // width, 1)
        out.append(buf[off:off + r].reshape(shp) if n % width == 0 else buf[off, :n].reshape(shp))
        off += r + (-r) % 8
    return out


def _shape2d(shp):
    return (math.prod(shp[:-1]), shp[-1])


def _unshard(g, axis):
    g = jnp.moveaxis(g, 0, axis)
    shp = g.shape
    return g.reshape(shp[:axis] + (shp[axis] * shp[axis + 1],) + shp[axis + 2:])


def _shard(full, axis):
    shp = full.shape
    g = full.reshape(shp[:axis] + (N_DEV, shp[axis] // N_DEV) + shp[axis + 1:])
    return jnp.moveaxis(g, axis, 0)


def _blockdiag(blocks):
    g, r, c = blocks.shape
    eye = jnp.eye(g, dtype=blocks.dtype)
    return (eye[:, None, :, None] * blocks[:, :, None, :]).reshape(g * r, g * c)


def _diagblocks(full, g):
    r, c = full.shape[0] // g, full.shape[1] // g
    f = full.reshape(g, r, g, c)
    idx = jnp.arange(g)
    return f[idx, :, idx, :]


def kernel(x, meta_tokens, ffn1_norm, ffn1_w_gate_up, ffn1_w_down, mix_norm, ffn2_norm, ffn2_w_gate_up, ffn2_w_down, ssm_w_in, ssm_lambda_re, ssm_lambda_im, ssm_b_re, ssm_b_im, ssm_c_re, ssm_c_im, ssm_log_step, ssm_d, ssm_w_out, kv_norm, w_kv, k_norm, attn_w_q, q_norm, attn_sinks, attn_w_o, loss_target, m_meta_tokens, m_ffn1_norm, m_ffn1_w_gate_up, m_ffn1_w_down, m_mix_norm, m_ffn2_norm, m_ffn2_w_gate_up, m_ffn2_w_down, m_ssm_w_in, m_ssm_lambda_re, m_ssm_lambda_im, m_ssm_b_re, m_ssm_b_im, m_ssm_c_re, m_ssm_c_im, m_ssm_log_step, m_ssm_d, m_ssm_w_out, m_kv_norm, m_w_kv, m_k_norm, m_attn_w_q, m_q_norm, m_attn_sinks, m_attn_w_o, v_meta_tokens, v_ffn1_norm, v_ffn1_w_gate_up, v_ffn1_w_down, v_mix_norm, v_ffn2_norm, v_ffn2_w_gate_up, v_ffn2_w_down, v_ssm_w_in, v_ssm_lambda_re, v_ssm_lambda_im, v_ssm_b_re, v_ssm_b_im, v_ssm_c_re, v_ssm_c_im, v_ssm_log_step, v_ssm_d, v_ssm_w_out, v_kv_norm, v_w_kv, v_k_norm, v_attn_w_q, v_q_norm, v_attn_sinks, v_attn_w_o):
    args = dict(locals())
    W = {n: args[n] for n in WEIGHTS}
    M = {n: args["m_" + n] for n in WEIGHTS}
    V = {n: args["v_" + n] for n in WEIGHTS}
    my_x, my_y, my_c = (lax.axis_index(a) for a in MESH_AXES)
    my_dev = 4 * my_x + 2 * my_y + my_c

    big_names = [n for n, _ in BIG]
    s2d = {n: _shape2d(W[n].shape) for n in big_names}
    col_w = W["meta_tokens"].shape[1]

    grads, summed, small_parts = _local_step(x, loss_target, W, my_c.astype(jnp.int32).reshape(1))
    loss = lax.psum(grads.pop("loss"), MESH_AXES)
    grad_x = grads.pop("x")

    outs = [{}, {}, {}, {}]
    for n in big_names:
        r4 = _adamw("adamw_" + n, W[n].reshape(s2d[n]), M[n].reshape(s2d[n]), V[n].reshape(s2d[n]), summed[n])
        for k in range(4):
            outs[k][n] = r4[k].reshape(W[n].shape)

    small_names = list(SMALL) + [n for n, _ in COLS]
    small_shapes = [grads[n].shape for n in small_names]
    zero_cols = [jnp.zeros(grads[n].shape, F32) for n, _ in COLS]
    packs = lambda d: _pack_small([d[n] for n in SMALL] + zero_cols, PACK_W)
    r4 = _adamw("adamw_small", packs(W), packs(M), packs(V), small_parts)
    gsmall = None
    for k in range(4):
        un = dict(zip(small_names, _unpack_small(r4[k], small_shapes, PACK_W)))
        gsmall = un if k == 0 else gsmall
        outs[k].update({n: un[n] for n in SMALL})
    col_g = [lax.dynamic_slice_in_dim(gsmall[n], my_dev * W[n].shape[1], W[n].shape[1], axis=1) for n, _ in COLS]
    packc = lambda d: _pack_small([d[n] for n, _ in COLS], col_w)
    r4 = _adamw("adamw_cols", packc(W), packc(M), packc(V), _pack_small(col_g, col_w)[None])
    col_shapes = [W[n].shape for n, _ in COLS]
    for k in range(4):
        outs[k].update(dict(zip([n for n, _ in COLS], _unpack_small(r4[k], col_shapes, col_w))))

    res = [[outs[k][n] for n in WEIGHTS] for k in range(4)]
    return (loss, grad_x, *res[0], *res[1], *res[2], *res[3])


def _local_step(x, target, P, c_arr):
    bsz, seq, d = x.shape
    lp = seq + PAD
    rows = bsz * lp
    depth = P["ffn1_norm"].shape[0]
    assert depth == 2
    bf = lambda a: a.astype(BF16)
    row = lambda a: a.reshape(1, -1)

    def shard(n, l=None):
        a = P[n] if l is None else P[n][l]
        return bf(a.reshape(_shape2d(a.shape)))

    rowsharded = lambda g: g.reshape((g.shape[0] * g.shape[1],) + g.shape[2:])
    colsharded = lambda g: _unshard(g, 1)
    col_w = P["meta_tokens"].shape[1]
    g0 = _run_rider("gather_first", _gather_rider(
        [shard("ffn1_w_gate_up", 0), shard("ffn1_w_down", 0), shard("ssm_w_in", 0),
         _pack_small([P["meta_tokens"], P["ssm_d"]], col_w)]))
    ffn_w = {("ffn1", 0): (colsharded(g0[0]), rowsharded(g0[1]))}
    w_in = rowsharded(g0[2])
    meta_full = _unshard(g0[3][:, :N_META], 1)
    dvec = _unshard(g0[3][:, N_META:N_META + 1, :P["ssm_d"].shape[1]], 1)

    pos = (jnp.arange(lp, dtype=F32) - float(META0))[:, None]
    half = HEAD_DIM // 2
    freqs = ROPE_THETA ** (-jnp.arange(0, half, dtype=F32) * 2.0 / HEAD_DIM)
    ang = pos * freqs[None, :]
    cos_t = jnp.tile(jnp.cos(ang), (1, LANES // half))
    sin_t = jnp.tile(jnp.concatenate([-jnp.sin(ang), jnp.sin(ang)], axis=1), (1, LANES // HEAD_DIM))
    gi = jnp.arange(LANES) // HEAD_DIM
    gmat = jnp.where(gi[:, None] == gi[None, :], 1.0 / HEAD_DIM, 0.0).astype(BF16)

    g_n, c_n, p_n = P["ssm_lambda_re"].shape[1], SSM_GROUP, SSM_STATE
    ns = g_n * p_n
    lr = P["ssm_lambda_re"][0].reshape(g_n, 1, p_n)
    li = P["ssm_lambda_im"][0].reshape(g_n, 1, p_n)
    ls = P["ssm_log_step"][0].reshape(g_n, 1, 1)
    brt = P["ssm_b_re"][0].transpose(0, 2, 1)
    bit = P["ssm_b_im"][0].transpose(0, 2, 1)
    ar, ai, bbr, bbi = _s5_params_fwd(lr, li, ls, brt, bit)
    a2 = jnp.concatenate([ar.reshape(1, ns), ai.reshape(1, ns)], axis=0)
    bfull = jnp.concatenate([_blockdiag(bbr), _blockdiag(bbi)], axis=1)
    cre_t = P["ssm_c_re"][0].transpose(0, 2, 1)
    cim_t = P["ssm_c_im"][0].transpose(0, 2, 1)
    cfull = jnp.concatenate([_blockdiag(cre_t), -_blockdiag(cim_t)], axis=0)

    ffn = lambda which, l: (row(P[which + "_norm"][l]),) + ffn_w[which, l]
    mix0, mix1, kvn = row(P["mix_norm"][0]), row(P["mix_norm"][1]), row(P["kv_norm"])
    kgain = jnp.tile(P["k_norm"].reshape(1, HEAD_DIM), (1, KVW // HEAD_DIM))
    qgain = jnp.tile(P["q_norm"].reshape(1, HEAD_DIM), (1, d // HEAD_DIM))
    sinks = P["attn_sinks"].reshape(1, -1)

    h0 = _embed(x, meta_full).reshape(rows, d)
    h1, ab_f1_0, g_wout, g_gu, g_d, g_kv = _ffn_fwd("ffn1_0_fwd", h0, *ffn("ffn1", 0), rider=_gather_rider(
        [shard("ssm_w_out", 0), shard("ffn2_w_gate_up", 0), shard("ffn2_w_down", 0), shard("w_kv")]))
    w_out, w_kv = colsharded(g_wout), rowsharded(g_kv)
    ffn_w["ffn2", 0] = (colsharded(g_gu), rowsharded(g_d))
    u = _proj_fwd("ssm_in_fwd", h1, mix0, w_in)
    y, xs = _s5_scan_fwd(u, bf(bfull), bf(cfull), a2, dvec, bsz)
    h2 = _glu_fwd(y, h1, w_out)
    h3, ab_f2_0, g_gu, g_d, g_q, g_o = _ffn_fwd("ffn2_0_fwd", h2, *ffn("ffn2", 0), rider=_gather_rider(
        [shard("ffn1_w_gate_up", 1), shard("ffn1_w_down", 1), shard("attn_w_q", 0), shard("attn_w_o", 0)]))
    w_q, w_o = rowsharded(g_q), rowsharded(g_o)
    ffn_w["ffn1", 1] = (colsharded(g_gu), rowsharded(g_d))
    kv = _proj_fwd("kv_fwd", h3, kvn, w_kv)
    k = _headrope_fwd("k_rope_fwd", kv, KVW, kgain, cos_t, sin_t, gmat, lp)
    h4, ab_f1_1, g_gu, g_d = _ffn_fwd("ffn1_1_fwd", h3, *ffn("ffn1", 1), rider=_gather_rider(
        [shard("ffn2_w_gate_up", 1), shard("ffn2_w_down", 1)]))
    ffn_w["ffn2", 1] = (colsharded(g_gu), rowsharded(g_d))
    q_raw = _proj_fwd("q_fwd", h4, mix1, w_q)
    q = _headrope_fwd("q_rope_fwd", q_raw, d, qgain, cos_t, sin_t, gmat, lp)
    r3 = lambda a: a.reshape(bsz, lp, a.shape[-1])
    o = _attn_fwd(r3(q), r3(k), r3(kv), sinks).reshape(rows, d)
    h5 = _lin_res_fwd("attn_out_fwd", o, w_o, h4)
    h6, ab_f2_1 = _ffn_fwd("ffn2_1_fwd", h5, *ffn("ffn2", 1))
    loss, dh6 = _loss(r3(h6), target)
    dh6 = dh6.reshape(rows, d)

    G = {"loss": loss[0, 0]}

    def ffn_back(name, which, l, h, ab, dout, rider=None):
        g, wgu, wd = ffn(which, l)
        dh, hn, dab, act, dg, *rode = _ffn_bwd(name, h, ab, dout, g, wgu, wd, rider=rider)
        parts = [_shard(_mm_tn(name + "_wgu", hn, dab), 1), _mm_tn_slots(name + "_wd", act, dout, 0.5)]
        return dh, dg, parts, rode

    swap_of = lambda parts: _swap_rider([p.reshape((4, 2) + p.shape[1:]) for p in parts])

    def pair_sums(tag, parts, theirs):
        return [_pair_sum("pair_sum_%s_%d" % (tag, k), p.reshape((4, 2) + p.shape[1:]), t, c_arr)
                for k, (p, t) in enumerate(zip(parts, theirs))]

    dh5, dg_f2_1, parts_a, _ = ffn_back("ffn2_1_bwd", "ffn2", 1, h5, ab_f2_1, dh6)
    do, dw_o, *theirs = _lin_bwd("attn_out_bwd", o, w_o, dh5, rider=swap_of(parts_a))
    sums_a = pair_sums("ffn2_1", parts_a, theirs)
    dq, dk, dv, dsinks = _attn_bwd(r3(q), r3(k), r3(kv), sinks, r3(o), r3(do))
    dq_raw, dqg = _headrope_bwd("q_rope_bwd", q_raw, d, dq.reshape(rows, d), qgain, cos_t, sin_t, gmat, lp)
    dh4, dg_mix1, dw_q = _proj_bwd("q_bwd", h4, mix1, w_q, dq_raw, dh5)
    dh3, dg_f1_1, parts_b, red_a = ffn_back("ffn1_1_bwd", "ffn1", 1, h3, ab_f1_1, dh4, rider=_scatter_rider(sums_a))
    dk_raw, dkg = _headrope_bwd("k_rope_bwd", kv, KVW, dk.reshape(rows, KVW), kgain, cos_t, sin_t, gmat, lp)
    dkv = _concat_cols("dkv_concat", dk_raw, dv.reshape(rows, KVW))
    dh3, dg_kvn, dw_kv, *theirs = _proj_bwd("kv_bwd", h3, kvn, w_kv, dkv, dh3, rider=swap_of(parts_b))
    sums_b = pair_sums("ffn1_1", parts_b, theirs)
    dh2, dg_f2_0, parts_c, red_b = ffn_back("ffn2_0_bwd", "ffn2", 0, h2, ab_f2_0, dh3, rider=_scatter_rider(sums_b))
    dy, dw_out, *theirs = _glu_bwd(y, dh2, w_out, rider=swap_of(parts_c))
    sums_c = pair_sums("ffn2_0", parts_c, theirs)
    ctfull = jnp.concatenate([_blockdiag(P["ssm_c_re"][0]), -_blockdiag(P["ssm_c_im"][0])], axis=1)
    btfull = jnp.concatenate([_blockdiag(bbr.transpose(0, 2, 1)), _blockdiag(bbi.transpose(0, 2, 1))], axis=0)
    du, gx, da, dd = _s5_scan_bwd(dy, u, xs, bf(ctfull), bf(btfull), a2, dvec, bsz)
    dbfull = _mm_tn_blockdiag("ssm_db", u, gx, False)
    dcfull = _mm_tn_blockdiag("ssm_dc", xs, dy, True)
    dh1, dg_mix0, dw_in = _proj_bwd("ssm_in_bwd", h1, mix0, w_in, du, dh2)
    dh0, dg_f1_0, parts_d, red_c = ffn_back("ffn1_0_bwd", "ffn1", 0, h0, ab_f1_0, dh1, rider=_scatter_rider(sums_c))
    dbbr = _diagblocks(dbfull[:, :ns], g_n)
    dbbi = _diagblocks(dbfull[:, ns:], g_n)
    dlr, dli, dls, dbrt, dbit = _s5_params_bwd(lr, li, ls, brt, bit, da[:, :ns].reshape(g_n, 1, p_n),
                                               da[:, ns:].reshape(g_n, 1, p_n), dbbr, dbbi)
    dh0 = r3(dh0)
    G["x"] = dh0[:, PAD:, :]
    G["meta_tokens"] = _meta_sum(dh0)
    G["ffn1_norm"] = jnp.concatenate([dg_f1_0, dg_f1_1], axis=0)
    G["ffn2_norm"] = jnp.concatenate([dg_f2_0, dg_f2_1], axis=0)
    G["mix_norm"] = jnp.concatenate([dg_mix0, dg_mix1], axis=0)
    G["ssm_lambda_re"] = dlr.reshape(1, g_n, p_n)
    G["ssm_lambda_im"] = dli.reshape(1, g_n, p_n)
    G["ssm_log_step"] = dls.reshape(1, g_n)
    G["ssm_b_re"] = dbrt.transpose(0, 2, 1)[None]
    G["ssm_b_im"] = dbit.transpose(0, 2, 1)[None]
    G["ssm_c_re"] = _diagblocks(dcfull[:ns], g_n).transpose(0, 2, 1)[None]
    G["ssm_c_im"] = -_diagblocks(dcfull[ns:], g_n).transpose(0, 2, 1)[None]
    G["ssm_d"] = dd
    G["kv_norm"] = dg_kvn.reshape(-1)
    G["k_norm"] = dkg[0, :HEAD_DIM]
    G["q_norm"] = dqg[:, :HEAD_DIM]
    G["attn_sinks"] = dsinks[:, :N_KV_HEADS * Q_PER_KV]

    slots = lambda g: g.reshape((N_DEV, g.shape[0] // N_DEV) + g.shape[1:])
    parts_d = parts_d + [slots(dw_in), dw_out, slots(dw_kv), slots(dw_q), slots(dw_o)]
    small_pack = _pack_small([G[n] for n in list(SMALL) + [n for n, _ in COLS]], PACK_W)
    *theirs, small_parts = _run_rider("grad_swap_last", _join_riders(swap_of(parts_d), _gather_rider([small_pack])))
    red_d = _run_rider("grad_scatter_last", _scatter_rider(pair_sums("last", parts_d, theirs)))
    both = lambda lo, hi: jnp.concatenate([lo, hi], axis=1)
    summed = {"ffn1_w_gate_up": both(red_d[0], red_b[0]), "ffn1_w_down": both(red_d[1], red_b[1]),
              "ffn2_w_gate_up": both(red_c[0], red_a[0]), "ffn2_w_down": both(red_c[1], red_a[1]),
              "ssm_w_in": red_d[2], "ssm_w_out": red_d[3], "w_kv": red_d[4], "attn_w_q": red_d[5],
              "attn_w_o": red_d[6]}
    return G, summed, small_parts
```

```python
import functools
import math

import jax
import jax.numpy as jnp
from jax import lax
from jax.experimental import pallas as pl
from jax.experimental.pallas import tpu as pltpu

F32 = jnp.float32
BF16 = jnp.bfloat16

N_META = 16
PAD = 128
META0 = PAD - N_META
HEAD_DIM = 64
N_KV_HEADS = 4
Q_PER_KV = 4
SSM_GROUP = 16
SSM_STATE = 64
EPS = 1e-6
NEG_INF = -1e30
ROPE_THETA = 10000.0
ADAM_LR, ADAM_B1, ADAM_B2, ADAM_EPS, ADAM_WD, ADAM_STEP = 0.001, 0.9, 0.999, 1e-08, 0.01, 10
LANES = 128
PACK_W = 1024
VMEM_LIMIT = 56 * 1024 * 1024
MESH_AXES = ("x", "y", "c")
N_DEV = 8


def _cparams(sem=None):
    return pltpu.CompilerParams(dimension_semantics=sem, vmem_limit_bytes=VMEM_LIMIT)


def _row_tile(rows):
    for tm in (384, 256, 128, 64, 32, 16, 8):
        if rows % tm == 0:
            return tm
    raise ValueError(rows)


STREAM_BUDGET = 32 * 1024 * 1024


def _stream_tile(rows, bytes_per_row):
    for tm in range(rows, 0, -1):
        if rows % tm == 0 and (tm % 16 == 0 or tm == rows) and 2 * tm * bytes_per_row <= STREAM_BUDGET:
            return tm
    raise ValueError(rows)


TN_BUDGET = 44 * 1024 * 1024


def _tn_tile(rows, a, b, k1, tn):
    sa, sb = a.dtype.itemsize, b.dtype.itemsize
    fits = lambda tm: 2 * tm * (k1 * sa + tn * sb) + 3 * k1 * tn * 4 + tm * (k1 + tn) * 2 <= TN_BUDGET
    divisors = [tm for tm in range(rows, 7, -8) if rows % tm == 0 and fits(tm)]
    whole = [tm for tm in divisors if tm % MXU_DIM == 0]
    if whole or divisors:
        return (whole or divisors)[0]
    raise ValueError(rows)


def _dot(a, b):
    return jnp.dot(a.astype(BF16), b.astype(BF16), preferred_element_type=F32)


def _dot_nt(a, b):
    return lax.dot_general(a.astype(BF16), b.astype(BF16), (((1,), (1,)), ((), ())), preferred_element_type=F32)


def _dot_tn(a, b):
    return lax.dot_general(a.astype(BF16), b.astype(BF16), (((0,), (0,)), ((), ())), preferred_element_type=F32)


def _rms(x, g):
    rstd = lax.rsqrt(jnp.mean(x * x, axis=-1, keepdims=True) + EPS)
    y = x * rstd
    return y * g, y, rstd


def _rms_bwd(dhn, y, rstd, g):
    dyn = dhn * g
    dx = rstd * (dyn - y * jnp.mean(dyn * y, axis=-1, keepdims=True))
    return dx, jnp.sum(dhn * y, axis=0, keepdims=True)


def _sigmoid(x):
    return 1.0 / (1.0 + jnp.exp(-x))


_GELU_C = math.sqrt(2.0 / math.pi)


def _gelu(y):
    t = jnp.tanh(_GELU_C * (y + 0.044715 * y * y * y))
    return 0.5 * y * (1.0 + t), t


def _gelu_grad(y, t):
    return 0.5 * (1.0 + t) + 0.5 * y * (1.0 - t * t) * _GELU_C * (1.0 + 3.0 * 0.044715 * y * y)


class _Rider:
    def __init__(self, ins, outs, sems, start, mid, finish):
        self.ins, self.outs, self.sems, self.start, self.mid, self.finish = ins, outs, sems, start, mid, finish


def _join_riders(r1, r2):
    ni, no, ns = len(r1.ins), len(r1.outs), len(r1.sems)

    def both(f1, f2):
        def phase(ins, outs, sems):
            if f1 is not None:
                f1(ins[:ni], outs[:no], sems[:ns])
            if f2 is not None:
                f2(ins[ni:], outs[no:], sems[ns:])
        return phase

    mid = both(r1.mid, r2.mid) if (r1.mid is not None or r2.mid is not None) else None
    return _Rider(r1.ins + r2.ins, r1.outs + r2.outs, r1.sems + r2.sems,
                  both(r1.start, r2.start), mid, both(r1.finish, r2.finish))


def _run_rider(name, rider):
    def kern(*refs):
        ni, no = len(rider.ins), len(rider.outs)
        parts = refs[:ni], refs[ni:ni + no], refs[ni + no:]
        rider.start(*parts)
        if rider.mid is not None:
            rider.mid(*parts)
        rider.finish(*parts)

    return pl.pallas_call(
        kern, name=name, out_shape=list(rider.outs), in_specs=[ANY] * len(rider.ins),
        out_specs=[ANY] * len(rider.outs), scratch_shapes=list(rider.sems),
    )(*rider.ins)


def _rowcall(name, body, rows, row_ins, const_ins, row_outs, acc_outs=(), tm=None, row_in_maps=None, rider=None):
    tm = tm or _row_tile(rows)
    steps = rows // tm
    in_specs = []
    for k, a in enumerate(row_ins):
        if row_in_maps is not None and row_in_maps[k] is not None:
            in_specs.append(pl.BlockSpec(*row_in_maps[k]))
        else:
            in_specs.append(pl.BlockSpec((tm, a.shape[1]), lambda i: (i, 0)))
    for a in const_ins:
        in_specs.append(pl.BlockSpec(a.shape, lambda i, nd=a.ndim: (0,) * nd, pipeline_mode=pl.Buffered(1)))
    out_shape, out_specs = [], []
    for w, dt in row_outs:
        out_shape.append(jax.ShapeDtypeStruct((rows, w), dt))
        out_specs.append(pl.BlockSpec((tm, w), lambda i: (i, 0)))
    for shp, dt in acc_outs:
        out_shape.append(jax.ShapeDtypeStruct(shp, dt))
        out_specs.append(pl.BlockSpec(shp, lambda i, nd=len(shp): (0,) * nd))

    if rider is None:
        def kern(*refs):
            body(pl.program_id(0), *refs)

        return pl.pallas_call(
            kern, name=name, grid=(steps,), in_specs=in_specs, out_specs=out_specs, out_shape=out_shape,
            compiler_params=_cparams(("arbitrary",)),
        )(*row_ins, *const_ins)

    n_in, n_out = len(in_specs), len(out_specs)
    r_in, r_out = len(rider.ins), len(rider.outs)

    def kern_r(*refs):
        step = pl.program_id(0)
        ins, rins = refs[:n_in], refs[n_in:n_in + r_in]
        outs = refs[n_in + r_in:n_in + r_in + n_out]
        routs = refs[n_in + r_in + n_out:n_in + r_in + n_out + r_out]
        sems = refs[n_in + r_in + n_out + r_out:]

        @pl.when(step == 0)
        def _():
            rider.start(rins, routs, sems)

        if rider.mid is not None:
            @pl.when(step == (3 * steps) // 4)
            def _():
                rider.mid(rins, routs, sems)

        body(step, *ins, *outs)

        @pl.when(step == steps - 1)
        def _():
            rider.finish(rins, routs, sems)

    return pl.pallas_call(
        kern_r, name=name, grid=(steps,), in_specs=in_specs + [ANY] * r_in, out_specs=out_specs + [ANY] * r_out,
        out_shape=out_shape + list(rider.outs), scratch_shapes=list(rider.sems),
        compiler_params=_cparams(("arbitrary",)),
    )(*row_ins, *const_ins, *rider.ins)


def _acc(step, ref, val):
    @pl.when(step == 0)
    def _():
        ref[...] = val

    @pl.when(step != 0)
    def _():
        ref[...] += val


def _embed(x, meta):
    bsz, seq, d = x.shape
    nb = seq // PAD + 1

    def kern(x_ref, m_ref, o_ref):
        i = pl.program_id(1)

        @pl.when(i == 0)
        def _():
            o_ref[0, 0:META0, :] = jnp.zeros((META0, d), F32)
            o_ref[0, META0:PAD, :] = m_ref[...]

        @pl.when(i != 0)
        def _():
            o_ref[0] = x_ref[0]

    return pl.pallas_call(
        kern, name="embed", grid=(bsz, nb),
        in_specs=[pl.BlockSpec((1, PAD, d), lambda b, i: (b, jnp.maximum(i - 1, 0), 0)),
                  pl.BlockSpec((N_META, d), lambda b, i: (0, 0))],
        out_specs=pl.BlockSpec((1, PAD, d), lambda b, i: (b, i, 0)),
        out_shape=jax.ShapeDtypeStruct((bsz, seq + PAD, d), F32),
        compiler_params=_cparams(("arbitrary", "arbitrary")),
    )(x, meta)


def _loss(h6, target):
    bsz, lp, d = h6.shape
    nb = lp // PAD

    def kern(h_ref, t_ref, l_ref, d_ref):
        b, i = pl.program_id(0), pl.program_id(1)

        @pl.when((b == 0) & (i == 0))
        def _():
            l_ref[...] = jnp.zeros_like(l_ref)

        @pl.when(i == 0)
        def _():
            d_ref[0] = jnp.zeros((PAD, d), F32)

        @pl.when(i != 0)
        def _():
            e = h_ref[0] - t_ref[0]
            d_ref[0] = e * (1.0 / d)
            l_ref[...] += 0.5 * jnp.sum(jnp.mean(e * e, axis=-1, keepdims=True))

    return pl.pallas_call(
        kern, name="loss", grid=(bsz, nb),
        in_specs=[pl.BlockSpec((1, PAD, d), lambda b, i: (b, i, 0)),
                  pl.BlockSpec((1, PAD, d), lambda b, i: (b, jnp.maximum(i - 1, 0), 0))],
        out_specs=[pl.BlockSpec((1, LANES), lambda b, i: (0, 0)),
                   pl.BlockSpec((1, PAD, d), lambda b, i: (b, i, 0))],
        out_shape=[jax.ShapeDtypeStruct((1, LANES), F32), jax.ShapeDtypeStruct((bsz, lp, d), F32)],
        compiler_params=_cparams(("arbitrary", "arbitrary")),
    )(h6, target)


def _meta_sum(dh0):
    bsz, lp, d = dh0.shape

    def kern(d_ref, o_ref):
        _acc(pl.program_id(0), o_ref, d_ref[0, META0:PAD, :])

    return pl.pallas_call(
        kern, name="meta_sum", grid=(bsz,),
        in_specs=[pl.BlockSpec((1, PAD, d), lambda b: (b, 0, 0))],
        out_specs=pl.BlockSpec((N_META, d), lambda b: (0, 0)),
        out_shape=jax.ShapeDtypeStruct((N_META, d), F32),
        compiler_params=_cparams(("arbitrary",)),
    )(dh0)


MXU_DIM = 256


def _ffn_chunks(f):
    unit = MXU_DIM if f % MXU_DIM == 0 else LANES
    assert f % unit == 0
    first = (f // unit + 1) // 2 * unit
    return [(0, first), (first, f)] if first < f else [(0, f)]


def _ffn_fwd(name, h, g, wgu, wd, rider=None):
    rows, d = h.shape
    f = wd.shape[0]
    chunks = _ffn_chunks(f)

    def body(step, h_ref, g_ref, wgu_ref, wd_ref, o_ref, ab_ref):
        hx = h_ref[...]
        hb = _rms(hx, g_ref[...])[0].astype(BF16)
        acc = jnp.zeros(hx.shape, F32)
        for lo, hi in chunks:
            ga, ua = slice(lo, hi), slice(f + lo, f + hi)
            a = _dot(hb, wgu_ref[:, ga])
            b = _dot(hb, wgu_ref[:, ua])
            ab_ref[:, ga] = a.astype(BF16)
            ab_ref[:, ua] = b.astype(BF16)
            acc = acc + _dot(a * _sigmoid(a) * b, wd_ref[ga, :])
        o_ref[...] = hx + 0.5 * acc

    return _rowcall(name, body, rows, [h], [g, wgu, wd], [(d, F32), (2 * f, BF16)], rider=rider)


def _ffn_bwd(name, h, ab, dout, g, wgu, wd, rider=None):
    rows, d = h.shape
    f = wd.shape[0]
    chunks = _ffn_chunks(f)

    def body(step, h_ref, ab_ref, do_ref, g_ref, wgu_ref, wd_ref, dh_ref, hn_ref, dab_ref, act_ref, dg_ref):
        hx, dout_x, gx = h_ref[...], do_ref[...], g_ref[...]
        hn, y, rstd = _rms(hx, gx)
        hn_ref[...] = hn.astype(BF16)
        dhalf = (0.5 * dout_x).astype(BF16)
        dhn = jnp.zeros(hx.shape, F32)
        for lo, hi in chunks:
            ga, ua = slice(lo, hi), slice(f + lo, f + hi)
            a = ab_ref[:, ga].astype(F32)
            b = ab_ref[:, ua].astype(F32)
            s = _sigmoid(a)
            silu = a * s
            act_ref[:, ga] = (silu * b).astype(BF16)
            dact = _dot_nt(dhalf, wd_ref[ga, :])
            da = (dact * b * (s + silu * (1.0 - s))).astype(BF16)
            db = (dact * silu).astype(BF16)
            dab_ref[:, ga] = da
            dab_ref[:, ua] = db
            dhn = dhn + _dot_nt(da, wgu_ref[:, ga]) + _dot_nt(db, wgu_ref[:, ua])
        dx, dg = _rms_bwd(dhn, y, rstd, gx)
        dh_ref[...] = dout_x + dx
        _acc(step, dg_ref, dg)

    return _rowcall(name, body, rows, [h, ab, dout], [g, wgu, wd],
                    [(d, F32), (d, BF16), (2 * f, BF16), (f, BF16)], [((1, d), F32)], rider=rider)


def _mm_tn(name, a, b, scale=1.0):
    rows, k1 = a.shape
    k2 = b.shape[1]
    tn = k2
    for cand in (512, 704, 1408, 1024):
        if k2 % cand == 0 and k1 * cand * 4 <= 6 * 1024 * 1024:
            tn = cand
    tm = _tn_tile(rows, a, b, k1, tn)
    steps = rows // tm

    def kern(a_ref, b_ref, o_ref):
        bx = b_ref[...]
        if scale != 1.0:
            bx = bx * scale
        _acc(pl.program_id(1), o_ref, _dot_tn(a_ref[...], bx))

    return pl.pallas_call(
        kern, name=name, grid=(k2 // tn, steps),
        in_specs=[pl.BlockSpec((tm, k1), lambda j, i: (i, 0)), pl.BlockSpec((tm, tn), lambda j, i: (i, j))],
        out_specs=pl.BlockSpec((k1, tn), lambda j, i: (0, j)),
        out_shape=jax.ShapeDtypeStruct((k1, k2), F32),
        compiler_params=_cparams(("arbitrary", "arbitrary")),
    )(a, b)


def _mm_tn_blockdiag(name, a, b, states_first):
    rows = a.shape[0]
    ka, kb = a.shape[1], b.shape[1]
    qa, qb = (ka // 4, kb // 2) if states_first else (ka // 2, kb // 4)
    tm = _tn_tile(rows, a, b, qa, qb)
    steps = rows // tm
    wide = lambda part, k: 2 * part + k
    amap = (lambda p, k, i: (i, wide(p, k))) if states_first else (lambda p, k, i: (i, k))
    bmap = (lambda p, k, i: (i, k)) if states_first else (lambda p, k, i: (i, wide(p, k)))
    omap = (lambda p, k, i: (wide(p, k), k)) if states_first else (lambda p, k, i: (k, wide(p, k)))

    def kern(a_ref, b_ref, o_ref):
        _acc(pl.program_id(2), o_ref, _dot_tn(a_ref[...], b_ref[...]))

    return pl.pallas_call(
        kern, name=name, grid=(2, 2, steps),
        in_specs=[pl.BlockSpec((tm, qa), amap), pl.BlockSpec((tm, qb), bmap)],
        out_specs=pl.BlockSpec((qa, qb), omap), out_shape=jax.ShapeDtypeStruct((ka, kb), F32),
        compiler_params=_cparams(("arbitrary", "arbitrary", "arbitrary")),
    )(a, b)


def _mm_tn_slots(name, a, b, scale):
    rows, k1 = a.shape
    k2 = b.shape[1]
    tn = 512 if k2 % 512 == 0 else k2
    sr = k1 // N_DEV
    tm = _tn_tile(rows, a, b, k1, tn)
    steps = rows // tm

    def kern(a_ref, b_ref, o_ref):
        bx = b_ref[...]
        if scale != 1.0:
            bx = bx * scale
        res = _dot_tn(a_ref[...], bx)
        step = pl.program_id(1)
        for s in range(N_DEV):
            _acc(step, o_ref.at[s], res[s * sr:(s + 1) * sr])

    return pl.pallas_call(
        kern, name=name, grid=(k2 // tn, steps),
        in_specs=[pl.BlockSpec((tm, k1), lambda j, i: (i, 0)), pl.BlockSpec((tm, tn), lambda j, i: (i, j))],
        out_specs=pl.BlockSpec((N_DEV, sr, tn), lambda j, i: (0, 0, j)),
        out_shape=jax.ShapeDtypeStruct((N_DEV, sr, k2), F32),
        compiler_params=_cparams(("arbitrary", "arbitrary")),
    )(a, b)


def _proj_fwd(name, h, g, w):
    rows = h.shape[0]

    def body(step, h_ref, g_ref, w_ref, o_ref):
        o_ref[...] = _dot(_rms(h_ref[...], g_ref[...])[0], w_ref[...])

    return _rowcall(name, body, rows, [h], [g, w], [(w.shape[1], F32)])[0]


def _proj_bwd(name, h, g, w, dy, dres, rider=None):
    rows, d = h.shape

    def body(step, h_ref, dy_ref, dr_ref, g_ref, w_ref, dh_ref, dg_ref, dw_ref):
        gx = g_ref[...]
        hn, y, rstd = _rms(h_ref[...], gx)
        dyx = dy_ref[...]
        dx, dg = _rms_bwd(_dot_nt(dyx, w_ref[...]), y, rstd, gx)
        dh_ref[...] = dr_ref[...] + dx
        _acc(step, dg_ref, dg)
        _acc(step, dw_ref, _dot_tn(hn, dyx))

    return _rowcall(name, body, rows, [h, dy, dres], [g, w], [(d, F32)], [((1, d), F32), (w.shape, F32)],
                    rider=rider)


def _lin_res_fwd(name, a, w, res):
    rows = a.shape[0]

    def body(step, a_ref, r_ref, w_ref, o_ref):
        o_ref[...] = r_ref[...] + _dot(a_ref[...], w_ref[...])

    return _rowcall(name, body, rows, [a, res], [w], [(w.shape[1], F32)])[0]


def _lin_bwd(name, a, w, dy, rider=None):
    rows, k = a.shape

    def body(step, a_ref, dy_ref, w_ref, da_ref, dw_ref):
        dyx = dy_ref[...]
        da_ref[...] = _dot_nt(dyx, w_ref[...])
        _acc(step, dw_ref, _dot_tn(a_ref[...], dyx))

    return _rowcall(name, body, rows, [a, dy], [w], [(k, F32)], [(w.shape, F32)], rider=rider)


def _s5_param_fn(lr, li, ls, brt, bit):
    step = jnp.exp(ls)
    mag = jnp.exp(lr * step)
    ar = mag * jnp.cos(li * step)
    ai = mag * jnp.sin(li * step)
    den = lr * lr + li * li
    nr, ni = ar - 1.0, ai
    cr = (nr * lr + ni * li) / den
    ci = (ni * lr - nr * li) / den
    return ar, ai, cr * brt - ci * bit, cr * bit + ci * brt


def _s5_params_fwd(lr, li, ls, brt, bit):
    def kern(lr_ref, li_ref, ls_ref, br_ref, bi_ref, ar_ref, ai_ref, bbr_ref, bbi_ref):
        ar, ai, bbr, bbi = _s5_param_fn(lr_ref[...], li_ref[...], ls_ref[...], br_ref[...], bi_ref[...])
        ar_ref[...], ai_ref[...], bbr_ref[...], bbi_ref[...] = ar, ai, bbr, bbi

    sd = jax.ShapeDtypeStruct
    return pl.pallas_call(
        kern, name="s5_params_fwd",
        out_shape=[sd(lr.shape, F32), sd(lr.shape, F32), sd(brt.shape, F32), sd(brt.shape, F32)],
    )(lr, li, ls, brt, bit)


def _s5_params_bwd(lr, li, ls, brt, bit, dar, dai, dbbr, dbbi):
    def kern(lr_ref, li_ref, ls_ref, br_ref, bi_ref, dar_ref, dai_ref, dbbr_ref, dbbi_ref,
             dlr_ref, dli_ref, dls_ref, dbr_ref, dbi_ref):
        _, vjp = jax.vjp(_s5_param_fn, lr_ref[...], li_ref[...], ls_ref[...], br_ref[...], bi_ref[...])
        dlr, dli, dls, dbr, dbi = vjp((dar_ref[...], dai_ref[...], dbbr_ref[...], dbbi_ref[...]))
        dlr_ref[...], dli_ref[...], dls_ref[...], dbr_ref[...], dbi_ref[...] = dlr, dli, dls, dbr, dbi

    sd = jax.ShapeDtypeStruct
    return pl.pallas_call(
        kern, name="s5_params_bwd",
        out_shape=[sd(lr.shape, F32), sd(lr.shape, F32), sd(ls.shape, F32), sd(brt.shape, F32), sd(brt.shape, F32)],
    )(lr, li, ls, brt, bit, dar, dai, dbbr, dbbi)


SCAN_LW = 512


SCAN_SEGS = 8
SCAN_UNROLL = 4


def _cmul(xr, xi, yr, yi):
    return xr * yr - xi * yi, xr * yi + xi * yr


def _scan_tables(a_ref, tab_ref, conj, seg_len):
    ns = a_ref.shape[1]
    ar = jnp.broadcast_to(a_ref[0:1, :], (8, ns))
    ai = jnp.broadcast_to(a_ref[1:2, :], (8, ns))
    if conj:
        ai = -ai
    big, base, e = None, (ar, ai), seg_len
    while e:
        if e & 1:
            big = base if big is None else _cmul(*big, *base)
        base = _cmul(*base, *base)
        e >>= 1
    big2 = _cmul(*big, *big)
    big4 = _cmul(*big2, *big2)
    for k, v in enumerate((ar, ai) + big + big2 + big4):
        tab_ref[k] = v


def _scan_block(x_ref, tab_ref, carry_ref, t_rows, ns, reverse):
    sl = t_rows // SCAN_SEGS
    assert sl % SCAN_UNROLL == 0
    row = lax.broadcasted_iota(jnp.int32, (8, SCAN_LW), 0)
    zero = jnp.zeros((8, SCAN_LW), F32)
    for lc in range(ns // SCAN_LW):
        lre = pl.ds(lc * SCAN_LW, SCAN_LW)
        lim = pl.ds(ns + lc * SCAN_LW, SCAN_LW)
        ar, ai = tab_ref[0, :, lre], tab_ref[1, :, lre]

        def rows_of(k, u):
            j = k * SCAN_UNROLL + u
            return pl.ds(pl.multiple_of(((sl - 1 - j) if reverse else j) * SCAN_SEGS, SCAN_SEGS), SCAN_SEGS)

        def local(k, s, lre=lre, lim=lim, ar=ar, ai=ai):
            sr, si = s
            for u in range(SCAN_UNROLL):
                rows = rows_of(k, u)
                tr, ti = _cmul(ar, ai, sr, si)
                sr, si = x_ref[rows, lre] + tr, x_ref[rows, lim] + ti
                x_ref[rows, lre], x_ref[rows, lim] = sr, si
            return sr, si

        er, ei = lax.fori_loop(0, sl // SCAN_UNROLL, local, (zero, zero))
        if reverse:
            cr = jnp.where(row == 7, carry_ref[:, lre], pltpu.roll(er, 7, 0))
            ci = jnp.where(row == 7, carry_ref[:, lim], pltpu.roll(ei, 7, 0))
        else:
            cr = jnp.where(row == 0, carry_ref[:, lre], pltpu.roll(er, 1, 0))
            ci = jnp.where(row == 0, carry_ref[:, lim], pltpu.roll(ei, 1, 0))
        for lvl, dsh in enumerate((1, 2, 4)):
            pr, pi = tab_ref[2 + 2 * lvl, :, lre], tab_ref[3 + 2 * lvl, :, lre]
            if reverse:
                keep, shift = row < 8 - dsh, 8 - dsh
            else:
                keep, shift = row >= dsh, dsh
            sr = jnp.where(keep, pltpu.roll(cr, shift, 0), 0.0)
            si = jnp.where(keep, pltpu.roll(ci, shift, 0), 0.0)
            tr, ti = _cmul(pr, pi, sr, si)
            cr, ci = cr + tr, ci + ti
        tr, ti = _cmul(tab_ref[2, :, lre], tab_ref[3, :, lre], cr, ci)
        edge = 0 if reverse else 7
        carry_ref[:, lre] = jnp.broadcast_to((er + tr)[edge:edge + 1, :], (8, SCAN_LW))
        carry_ref[:, lim] = jnp.broadcast_to((ei + ti)[edge:edge + 1, :], (8, SCAN_LW))

        def fix(k, t, lre=lre, lim=lim, ar=ar, ai=ai):
            tr, ti = t
            for u in range(SCAN_UNROLL):
                rows = rows_of(k, u)
                tr, ti = _cmul(ar, ai, tr, ti)
                x_ref[rows, lre] = x_ref[rows, lre] + tr
                x_ref[rows, lim] = x_ref[rows, lim] + ti
            return tr, ti

        lax.fori_loop(0, sl // SCAN_UNROLL, fix, (cr, ci))


def _bd_expand(u, w_ref, x_ref, ns):
    hh, sh = u.shape[1] // 2, ns // 2
    ub = u.astype(BF16)
    for part in range(2):
        for k in range(2):
            cols = slice(part * ns + k * sh, part * ns + (k + 1) * sh)
            x_ref[:, cols] = jnp.dot(ub[:, k * hh:(k + 1) * hh], w_ref[k * hh:(k + 1) * hh, cols],
                                     preferred_element_type=F32)


def _bd_contract(x_ref, w_ref, ns):
    hh, sh = w_ref.shape[1] // 2, ns // 2
    halves = []
    for k in range(2):
        acc = None
        for part in range(2):
            rows = slice(part * ns + k * sh, part * ns + (k + 1) * sh)
            t = jnp.dot(x_ref[:, rows].astype(BF16), w_ref[rows, k * hh:(k + 1) * hh], preferred_element_type=F32)
            acc = t if acc is None else acc + t
        halves.append(acc)
    return jnp.concatenate(halves, axis=1)


def _scan_rows(lp):
    for t in (384, 256, 128):
        if lp % t == 0:
            return t
    raise ValueError(lp)


def _seg_perm(t_rows):
    r = jnp.arange(t_rows)
    src = (r % SCAN_SEGS) * (t_rows // SCAN_SEGS) + r // SCAN_SEGS
    p = (src[:, None] == r[None, :]).astype(BF16)
    return p, p.T


def _permute_rows(p_ref, v):
    return jnp.dot(p_ref[...], v.astype(BF16), preferred_element_type=F32)


def _unpermute_rows(pt_ref, v):
    hi = v.astype(BF16)
    lo = (v - hi.astype(F32)).astype(BF16)
    pt = pt_ref[...]
    return jnp.dot(pt, hi, preferred_element_type=F32) + jnp.dot(pt, lo, preferred_element_type=F32)


def _s5_scan_fwd(u, bfull, cfull, a2, dvec, bsz):
    rows, hw = u.shape
    ns = a2.shape[1]
    lp = rows // bsz
    t_rows = _scan_rows(lp)
    nc = lp // t_rows
    pmat, pmat_t = _seg_perm(t_rows)

    def kern(u_ref, b_ref, c_ref, a_ref, d_ref, p_ref, pt_ref, y_ref, x_ref, up_ref, tab_ref, carry_ref):
        c = pl.program_id(1)

        @pl.when((pl.program_id(0) == 0) & (c == 0))
        def _():
            _scan_tables(a_ref, tab_ref, False, t_rows // SCAN_SEGS)

        @pl.when(c == 0)
        def _():
            carry_ref[...] = jnp.zeros_like(carry_ref)

        ux = u_ref[...]
        up = _permute_rows(p_ref, ux)
        up_ref[...] = up.astype(BF16)
        _bd_expand(up, b_ref, x_ref, ns)
        _scan_block(x_ref, tab_ref, carry_ref, t_rows, ns, reverse=False)
        y_ref[...] = _unpermute_rows(pt_ref, _bd_contract(x_ref, c_ref, ns)) + d_ref[...] * ux

    const = lambda shp: pl.BlockSpec(shp, lambda b, c: (0,) * len(shp), pipeline_mode=pl.Buffered(1))
    blk = lambda b, c: (b * nc + c, 0)
    return pl.pallas_call(
        kern, name="s5_scan_fwd", grid=(bsz, nc),
        in_specs=[pl.BlockSpec((t_rows, hw), blk), const(bfull.shape), const(cfull.shape), const(a2.shape),
                  const(dvec.shape), const(pmat.shape), const(pmat.shape)],
        out_specs=[pl.BlockSpec((t_rows, hw), blk), pl.BlockSpec((t_rows, 2 * ns), blk),
                   pl.BlockSpec((t_rows, hw), blk)],
        out_shape=[jax.ShapeDtypeStruct((rows, hw), F32), jax.ShapeDtypeStruct((rows, 2 * ns), F32),
                   jax.ShapeDtypeStruct((rows, hw), BF16)],
        scratch_shapes=[pltpu.VMEM((8, 8, ns), F32), pltpu.VMEM((8, 2 * ns), F32)],
        compiler_params=_cparams(("arbitrary", "arbitrary")),
    )(u, bfull, cfull, a2, dvec, pmat, pmat_t)


def _s5_scan_bwd(dy, u, xs, ctfull, btfull, a2, dvec, bsz):
    rows, hw = u.shape
    ns = a2.shape[1]
    lp = rows // bsz
    t_rows = _scan_rows(lp)
    nc = lp // t_rows
    blk = lambda b, c: (b * nc + (nc - 1 - c), 0)
    pmat, pmat_t = _seg_perm(t_rows)

    def prev8(b, c):
        first = (b * nc + (nc - 1 - c)) * (t_rows // 8)
        return (jnp.maximum(first - 1, 0), 0)

    def kern(dy_ref, u_ref, x_ref, xp_ref, ct_ref, bt_ref, a_ref, d_ref, p_ref, pt_ref,
             du_ref, gx_ref, dyp_ref, da_ref, dd_ref, tab_ref, carry_ref):
        b, c = pl.program_id(0), pl.program_id(1)
        first = (b == 0) & (c == 0)

        @pl.when(first)
        def _():
            _scan_tables(a_ref, tab_ref, True, t_rows // SCAN_SEGS)

        @pl.when(c == 0)
        def _():
            carry_ref[...] = jnp.zeros_like(carry_ref)

        dyx, ux = dy_ref[...], u_ref[...]
        dyp = _permute_rows(p_ref, dyx)
        dyp_ref[...] = dyp.astype(BF16)
        _bd_expand(dyp, ct_ref, gx_ref, ns)
        _scan_block(gx_ref, tab_ref, carry_ref, t_rows, ns, reverse=True)
        gx = gx_ref[...]
        du_ref[...] = _unpermute_rows(pt_ref, _bd_contract(gx_ref, bt_ref, ns)) + d_ref[...] * dyx
        seq_start = c == nc - 1
        row8 = lax.broadcasted_iota(jnp.int32, (8, 1), 0)
        head = pltpu.roll(x_ref[t_rows - 8:t_rows, :], 1, 0)
        head = jnp.where(row8 == 0, jnp.where(seq_start, 0.0, xp_ref[7:8, :]), head)
        xprev = jnp.concatenate([head, x_ref[0:t_rows - 8, :]], axis=0)
        xr, xi, gr, gi = xprev[:, :ns], xprev[:, ns:], gx[:, :ns], gx[:, ns:]
        da = jnp.concatenate([jnp.sum(xr * gr + xi * gi, axis=0, keepdims=True),
                              jnp.sum(xr * gi - xi * gr, axis=0, keepdims=True)], axis=1)
        dd = jnp.sum(dyx * ux, axis=0, keepdims=True)

        @pl.when(first)
        def _():
            da_ref[...] = da
            dd_ref[...] = dd

        @pl.when(jnp.logical_not(first))
        def _():
            da_ref[...] += da
            dd_ref[...] += dd

    const = lambda shp: pl.BlockSpec(shp, lambda b, c: (0,) * len(shp), pipeline_mode=pl.Buffered(1))
    return pl.pallas_call(
        kern, name="s5_scan_bwd", grid=(bsz, nc),
        in_specs=[pl.BlockSpec((t_rows, hw), blk), pl.BlockSpec((t_rows, hw), blk),
                  pl.BlockSpec((t_rows, 2 * ns), blk), pl.BlockSpec((8, 2 * ns), prev8),
                  const(ctfull.shape), const(btfull.shape), const(a2.shape), const(dvec.shape),
                  const(pmat.shape), const(pmat.shape)],
        out_specs=[pl.BlockSpec((t_rows, hw), blk), pl.BlockSpec((t_rows, 2 * ns), blk),
                   pl.BlockSpec((t_rows, hw), blk),
                   pl.BlockSpec((1, 2 * ns), lambda b, c: (0, 0)), pl.BlockSpec((1, hw), lambda b, c: (0, 0))],
        out_shape=[jax.ShapeDtypeStruct((rows, hw), F32), jax.ShapeDtypeStruct((rows, 2 * ns), F32),
                   jax.ShapeDtypeStruct((rows, hw), BF16),
                   jax.ShapeDtypeStruct((1, 2 * ns), F32), jax.ShapeDtypeStruct((1, hw), F32)],
        scratch_shapes=[pltpu.VMEM((8, 8, ns), F32), pltpu.VMEM((8, 2 * ns), F32)],
        compiler_params=_cparams(("arbitrary", "arbitrary")),
    )(dy, u, xs, xs, ctfull, btfull, a2, dvec, pmat, pmat_t)


def _glu_fwd(y, h1, wout):
    rows, d = h1.shape

    def body(step, y_ref, h_ref, w_ref, o_ref):
        z = _dot(_gelu(y_ref[...])[0], w_ref[...])
        o_ref[...] = h_ref[...] + z[:, :d] * _sigmoid(z[:, d:])

    return _rowcall("glu_fwd", body, rows, [y, h1], [wout], [(d, F32)])[0]


def _glu_bwd(y, dh2, wout, rider=None):
    rows, d = dh2.shape
    hw = y.shape[1]

    def body(step, y_ref, dh_ref, w_ref, dy_ref, dw_ref):
        yx, dh = y_ref[...], dh_ref[...]
        gl, t = _gelu(yx)
        z = _dot(gl, w_ref[...])
        za, sg = z[:, :d], _sigmoid(z[:, d:])
        dza = dh * sg
        dzg = dh * za * sg * (1.0 - sg)
        dgl = _dot_nt(dza, w_ref[:, :d]) + _dot_nt(dzg, w_ref[:, d:])
        dy_ref[...] = dgl * _gelu_grad(yx, t)
        for half, dz in enumerate((dza, dzg)):
            dw = _dot_tn(gl, dz)
            for s in range(N_DEV // 2):
                _acc(step, dw_ref.at[half * (N_DEV // 2) + s], dw[:, s * cw:(s + 1) * cw])

    cw = 2 * d // N_DEV
    return _rowcall("glu_bwd", body, rows, [y, dh2], [wout], [(hw, F32)], [((N_DEV, hw, cw), F32)], rider=rider)


def _gmean64(x2, gmat):
    hi = x2.astype(BF16)
    r1 = x2 - hi.astype(F32)
    mid = r1.astype(BF16)
    lo = (r1 - mid.astype(F32)).astype(BF16)
    outs = []
    for j in range(x2.shape[1] // LANES):
        sl = slice(j * LANES, (j + 1) * LANES)
        f = lambda p: jnp.dot(p[:, sl], gmat, preferred_element_type=F32)
        outs.append(f(hi) + f(mid) + f(lo))
    return outs[0] if len(outs) == 1 else jnp.concatenate(outs, axis=1)


def _swap32(x):
    w = x.shape[1]
    lane = lax.broadcasted_iota(jnp.int32, (1, w), 1)
    return jnp.where((lane & 32) == 0, pltpu.roll(x, w - 32, 1), pltpu.roll(x, 32, 1))


def _tile_lanes(t, w):
    reps = w // t.shape[1]
    return t if reps == 1 else jnp.concatenate([t] * reps, axis=1)


def _headrope_fwd(name, raw, w, gain, cos, sin, gmat, lp):
    rows = raw.shape[0]
    tm = _row_tile(lp)
    per = lp // tm

    def body(step, x_ref, c_ref, s_ref, g_ref, gm_ref, o_ref):
        x = x_ref[...]
        rstd = lax.rsqrt(_gmean64(x * x, gm_ref[...]) + EPS)
        z = x * rstd * g_ref[...]
        o_ref[...] = z * _tile_lanes(c_ref[...], w) + _swap32(z) * _tile_lanes(s_ref[...], w)

    maps = [((tm, w), lambda i: (i, 0)), ((tm, LANES), lambda i: (i % per, 0)), ((tm, LANES), lambda i: (i % per, 0))]
    return _rowcall(name, body, rows, [raw, cos, sin], [gain, gmat], [(w, F32)], tm=tm, row_in_maps=maps)[0]


def _headrope_bwd(name, raw, w, dout, gain, cos, sin, gmat, lp):
    rows = raw.shape[0]
    tm = _row_tile(lp)
    per = lp // tm

    def body(step, x_ref, do_ref, c_ref, s_ref, g_ref, gm_ref, dx_ref, dg_ref):
        x, dout_x, gx, gm = x_ref[...], do_ref[...], g_ref[...], gm_ref[...]
        rstd = lax.rsqrt(_gmean64(x * x, gm) + EPS)
        yn = x * rstd
        dz = dout_x * _tile_lanes(c_ref[...], w) + _swap32(dout_x * _tile_lanes(s_ref[...], w))
        dyn = dz * gx
        dx_ref[...] = rstd * (dyn - yn * _gmean64(dyn * yn, gm))
        dg = jnp.sum(dz * yn, axis=0, keepdims=True)
        sh = w // 2
        while sh >= HEAD_DIM:
            dg = dg + pltpu.roll(dg, sh, 1)
            sh //= 2
        _acc(step, dg_ref, dg)

    maps = [((tm, w), lambda i: (i, 0)), None, ((tm, LANES), lambda i: (i % per, 0)), ((tm, LANES), lambda i: (i % per, 0))]
    return _rowcall(name, body, rows, [raw, dout, cos, sin], [gain, gmat], [(w, F32)], [((1, w), F32)],
                    tm=tm, row_in_maps=maps)


KVW = N_KV_HEADS * HEAD_DIM
QB = 128


def _fold4(x):
    y = x + pltpu.roll(x, 128, 1)
    return y + pltpu.roll(y, 64, 1)


def _attn_scores(i, q_ref, k0_ref, kp_ref, kc_ref, sink_ref, h):
    lane = lax.broadcasted_iota(jnp.int32, (1, KVW), 1) // HEAD_DIM
    qh = q_ref[:, h * KVW:(h + 1) * KVW]
    qs = jnp.concatenate([jnp.where(lane == g, qh, 0.0) for g in range(Q_PER_KV)], axis=0).astype(BF16)
    hsel = lane == h
    kx = _expand_kv((k0_ref, kp_ref, kc_ref), hsel)
    scale = HEAD_DIM ** -0.5
    s0, sb = [_dot_nt(qs, k) * scale for k in kx]
    k0j = lax.broadcasted_iota(jnp.int32, (Q_PER_KV * QB, QB), 1)
    s0 = jnp.where(k0j >= META0, s0, NEG_INF)
    qi = lax.broadcasted_iota(jnp.int32, (Q_PER_KV * QB, 2 * QB), 0) % QB
    kj = lax.broadcasted_iota(jnp.int32, (Q_PER_KV * QB, 2 * QB), 1)
    in_prev = (kj < QB) & (kj > qi) & (i >= 2)
    in_cur = (kj >= QB) & (kj - QB <= qi)
    sb = jnp.where(in_prev | in_cur, sb, NEG_INF)
    rowg = lax.broadcasted_iota(jnp.int32, (Q_PER_KV * QB, 1), 0) // QB
    sink = jnp.zeros((Q_PER_KV * QB, 1), F32)
    for g in range(Q_PER_KV):
        sink = jnp.where(rowg == g, sink_ref[0, h * Q_PER_KV + g], sink)
    m = jnp.maximum(jnp.maximum(jnp.max(s0, axis=1, keepdims=True), jnp.max(sb, axis=1, keepdims=True)), sink)
    p0, pb, ps = jnp.exp(s0 - m), jnp.exp(sb - m), jnp.exp(sink - m)
    den = jnp.sum(p0, axis=1, keepdims=True) + jnp.sum(pb, axis=1, keepdims=True) + ps
    return qs, kx, (p0, pb), ps, den, lane, hsel


def _expand_kv(refs, hsel):
    x0, xp, xc = [_fold4(jnp.where(hsel, r[...], 0.0)).astype(BF16) for r in refs]
    return [x0, jnp.concatenate([xp, xc], axis=0)]


def _unstack(x, lane):
    out = jnp.where(lane == 0, x[0:QB], 0.0)
    for g in range(1, Q_PER_KV):
        out = out + jnp.where(lane == g, x[g * QB:(g + 1) * QB], 0.0)
    return out


def _attn_specs(nb, d):
    qspec = pl.BlockSpec((None, QB, d), lambda b, i: (b, i, 0))
    k0 = pl.BlockSpec((None, QB, KVW), lambda b, i: (b, 0, 0))
    kp = pl.BlockSpec((None, QB, KVW), lambda b, i: (b, jnp.maximum(i - 1, 0), 0))
    kc = pl.BlockSpec((None, QB, KVW), lambda b, i: (b, i, 0))
    v0 = pl.BlockSpec((None, QB, KVW), lambda b, i: (b, 0, 1))
    vp = pl.BlockSpec((None, QB, KVW), lambda b, i: (b, jnp.maximum(i - 1, 0), 1))
    vc = pl.BlockSpec((None, QB, KVW), lambda b, i: (b, i, 1))
    sink = pl.BlockSpec(memory_space=pltpu.SMEM)
    return qspec, [k0, kp, kc], [v0, vp, vc], sink


def _attn_fwd(q, k, kv, sinks):
    bsz, lp, d = q.shape
    nb = lp // QB
    qspec, kspecs, vspecs, sspec = _attn_specs(nb, d)

    def kern(q_ref, k0_ref, kp_ref, kc_ref, v0_ref, vp_ref, vc_ref, sink_ref, o_ref):
        i = pl.program_id(1)
        for h in range(N_KV_HEADS):
            qs, kx, ps3, psink, den, lane, hsel = _attn_scores(i, q_ref, k0_ref, kp_ref, kc_ref, sink_ref, h)
            vx = _expand_kv((v0_ref, vp_ref, vc_ref), hsel)
            o = _dot(ps3[0], vx[0]) + _dot(ps3[1], vx[1])
            o_ref[:, h * KVW:(h + 1) * KVW] = _unstack(o / den, lane)

    return pl.pallas_call(
        kern, name="attn_fwd", grid=(bsz, nb),
        in_specs=[qspec] + kspecs + vspecs + [sspec],
        out_specs=qspec, out_shape=jax.ShapeDtypeStruct((bsz, lp, d), F32),
        compiler_params=_cparams(("arbitrary", "arbitrary")),
    )(q, k, k, k, kv, kv, kv, sinks)


def _attn_bwd(q, k, kv, sinks, o, do):
    bsz, lp, d = q.shape
    nb = lp // QB
    qspec, kspecs, vspecs, sspec = _attn_specs(nb, d)
    full = pl.BlockSpec((None, lp, KVW), lambda b, i: (b, 0, 0))

    def kern(q_ref, k0_ref, kp_ref, kc_ref, v0_ref, vp_ref, vc_ref, sink_ref, o_ref, do_ref,
             dq_ref, dk_ref, dv_ref, ds_ref):
        b, i = pl.program_id(0), pl.program_id(1)

        @pl.when(i == 0)
        def _():
            dk_ref[...] = jnp.zeros_like(dk_ref)
            dv_ref[...] = jnp.zeros_like(dv_ref)

        @pl.when((b == 0) & (i == 0))
        def _():
            ds_ref[...] = jnp.zeros_like(ds_ref)

        lane128 = lax.broadcasted_iota(jnp.int32, (1, LANES), 1)
        rowg = lax.broadcasted_iota(jnp.int32, (Q_PER_KV * QB, 1), 0) // QB
        dk_acc = [jnp.zeros((QB, KVW), F32), jnp.zeros((2 * QB, KVW), F32)]
        dv_acc = [jnp.zeros((QB, KVW), F32), jnp.zeros((2 * QB, KVW), F32)]
        dsink = jnp.zeros((1, LANES), F32)
        for h in range(N_KV_HEADS):
            qs, kx, ps3, psink, den, lane, hsel = _attn_scores(i, q_ref, k0_ref, kp_ref, kc_ref, sink_ref, h)
            vx = _expand_kv((v0_ref, vp_ref, vc_ref), hsel)
            sl = slice(h * KVW, (h + 1) * KVW)
            doh, oh = do_ref[:, sl], o_ref[:, sl]
            dos = jnp.concatenate([jnp.where(lane == g, doh, 0.0) for g in range(Q_PER_KV)], axis=0)
            ost = jnp.concatenate([jnp.where(lane == g, oh, 0.0) for g in range(Q_PER_KV)], axis=0)
            delta = jnp.sum(dos * ost, axis=1, keepdims=True)
            inv = 1.0 / den
            dosb = dos.astype(BF16)
            dqs = jnp.zeros((Q_PER_KV * QB, KVW), F32)
            for n in range(2):
                pn = ps3[n] * inv
                ds = pn * (_dot_nt(dosb, vx[n]) - delta) * (HEAD_DIM ** -0.5)
                dqs = dqs + _dot(ds, kx[n])
                dk_acc[n] = dk_acc[n] + jnp.where(hsel, _fold4(_dot_tn(ds, qs)), 0.0)
                dv_acc[n] = dv_acc[n] + jnp.where(hsel, _fold4(_dot_tn(pn, dosb)), 0.0)
            dq_ref[:, sl] = _unstack(dqs, lane)
            dsk = -(psink * inv) * delta
            for g in range(Q_PER_KV):
                val = jnp.sum(jnp.where(rowg == g, dsk, 0.0), axis=0, keepdims=True)
                dsink = dsink + jnp.where(lane128 == h * Q_PER_KV + g, val, 0.0)
        ds_ref[...] += dsink
        r0 = pl.ds(0, QB)
        rp = pl.ds(pl.multiple_of(jnp.maximum(i - 1, 0) * QB, QB), QB)
        rc = pl.ds(pl.multiple_of(i * QB, QB), QB)
        for acc, ref in ((dk_acc, dk_ref), (dv_acc, dv_ref)):
            ref[r0, :] += acc[0]
            ref[rp, :] += acc[1][:QB]
            ref[rc, :] += acc[1][QB:]

    return pl.pallas_call(
        kern, name="attn_bwd", grid=(bsz, nb),
        in_specs=[qspec] + kspecs + vspecs + [sspec, qspec, qspec],
        out_specs=[qspec, full, full, pl.BlockSpec((1, LANES), lambda b, i: (0, 0))],
        out_shape=[jax.ShapeDtypeStruct((bsz, lp, d), F32), jax.ShapeDtypeStruct((bsz, lp, KVW), F32),
                   jax.ShapeDtypeStruct((bsz, lp, KVW), F32), jax.ShapeDtypeStruct((1, LANES), F32)],
        compiler_params=_cparams(("arbitrary", "arbitrary")),
    )(q, k, k, k, kv, kv, kv, sinks, o, do)


def _concat_cols(name, a, b):
    rows = a.shape[0]

    def body(step, a_ref, b_ref, o_ref):
        o_ref[...] = jnp.concatenate([a_ref[...], b_ref[...]], axis=1)

    return _rowcall(name, body, rows, [a, b], [], [(a.shape[1] + b.shape[1], F32)])[0]


def _adamw(name, w, m, v, parts):
    rows, wd = w.shape
    n = parts.shape[0]
    tm = _stream_tile(rows, wd * (7 * 4 + n * parts.dtype.itemsize))

    def kern(w_ref, m_ref, v_ref, p_ref, g_ref, d_ref, m2_ref, v2_ref):
        g = p_ref[0].astype(F32)
        for k in range(1, n):
            g = g + p_ref[k].astype(F32)
        m2 = ADAM_B1 * m_ref[...] + (1.0 - ADAM_B1) * g
        v2 = ADAM_B2 * v_ref[...] + (1.0 - ADAM_B2) * (g * g)
        mh = m2 / (1.0 - ADAM_B1 ** ADAM_STEP)
        vh = v2 / (1.0 - ADAM_B2 ** ADAM_STEP)
        g_ref[...] = g
        d_ref[...] = -ADAM_LR * (mh / (jnp.sqrt(vh) + ADAM_EPS) + ADAM_WD * w_ref[...])
        m2_ref[...] = m2
        v2_ref[...] = v2

    spec = pl.BlockSpec((tm, wd), lambda i: (i, 0))
    sd = jax.ShapeDtypeStruct((rows, wd), F32)
    return pl.pallas_call(
        kern, name=name, grid=(rows // tm,),
        in_specs=[spec, spec, spec, pl.BlockSpec((n, tm, wd), lambda i: (0, i, 0))],
        out_specs=[spec] * 4, out_shape=[sd] * 4,
        compiler_params=_cparams(("arbitrary",)),
    )(w, m, v, parts)


def _pair_sum(name, parts, theirs, my_c):
    n, _, rows, wd = parts.shape
    tm = _stream_tile(rows, wd * (4 + 4 + 2))

    def kern(c_ref, a_ref, b_ref, o_ref):
        o_ref[...] = (a_ref[...] + b_ref[...]).astype(BF16)

    return pl.pallas_call(
        kern, name=name,
        grid_spec=pltpu.PrefetchScalarGridSpec(
            num_scalar_prefetch=1, grid=(n, rows // tm),
            in_specs=[pl.BlockSpec((None, None, tm, wd), lambda k, i, c: (k, c[0], i, 0)),
                      pl.BlockSpec((None, tm, wd), lambda k, i, c: (k, i, 0))],
            out_specs=pl.BlockSpec((None, tm, wd), lambda k, i, c: (k, i, 0))),
        out_shape=jax.ShapeDtypeStruct((n, rows, wd), BF16), compiler_params=_cparams(("arbitrary", "arbitrary")),
    )(my_c, parts, theirs)


MESH = pl.DeviceIdType.MESH
ANY = pl.BlockSpec(memory_space=pl.ANY)


def _place():
    x, y, c = lax.axis_index("x"), lax.axis_index("y"), lax.axis_index("c")
    return x, y, c, [(1 - x, y), (x, 1 - y), (1 - x, 1 - y)]


def _gather_rider(shards):
    n = len(shards)

    def copy(refs, a, k, block, to, own=False):
        x_refs, out_refs, (send_sems, recv_sems, _) = refs
        px, py, pc = block
        slot = out_refs[a].at[4 * px + 2 * py + pc]
        return pltpu.make_async_remote_copy(
            src_ref=x_refs[a] if own else slot, dst_ref=slot,
            send_sem=send_sems.at[a, k], recv_sem=recv_sems.at[a, k], device_id=to, device_id_type=MESH)

    def local(refs, a):
        x, y, c, _ = _place()
        return pltpu.make_async_copy(refs[0][a], refs[1][a].at[4 * x + 2 * y + c], refs[2][2].at[a])

    def first(refs):
        x, y, c, chips = _place()
        out = []
        for a in range(n):
            out.append(copy(refs, a, 0, (x, y, c), (x, y, 1 - c), own=True))
            out += [copy(refs, a, 1 + j, (x, y, c), (*chip, c), own=True) for j, chip in enumerate(chips)]
        return out

    def passed(refs):
        x, y, c, chips = _place()
        return [copy(refs, a, 4 + j, (*chip, c), (x, y, 1 - c)) for j, chip in enumerate(chips) for a in range(n)]

    def start(*refs):
        for a in range(n):
            local(refs, a).start()
        for cp in first(refs):
            cp.start()

    def mid(*refs):
        x, y, c, chips = _place()
        fwd = passed(refs)
        for j, chip in enumerate(chips):
            for a in range(n):
                copy(refs, a, 1 + j, (*chip, c), (x, y, c)).wait_recv()
                fwd[j * n + a].start()

    def finish(*refs):
        x, y, c, chips = _place()
        for a in range(n):
            copy(refs, a, 0, (x, y, 1 - c), (x, y, c)).wait_recv()
            for j, chip in enumerate(chips):
                copy(refs, a, 4 + j, (*chip, 1 - c), (x, y, c)).wait_recv()
        for cp in first(refs) + passed(refs):
            cp.wait_send()
        for a in range(n):
            local(refs, a).wait()

    return _Rider(list(shards), [jax.ShapeDtypeStruct((N_DEV,) + s.shape, s.dtype) for s in shards],
                  [pltpu.SemaphoreType.DMA((n, 7)), pltpu.SemaphoreType.DMA((n, 7)), pltpu.SemaphoreType.DMA((n,))],
                  start, mid, finish)


def _swap_rider(parts):
    n = len(parts)

    def copies(p_refs, out_refs, sems):
        x, y, c, _ = _place()
        return [pltpu.make_async_remote_copy(
            src_ref=p_refs[a].at[:, 1 - c], dst_ref=out_refs[a], send_sem=sems[0].at[a], recv_sem=sems[1].at[a],
            device_id=(x, y, 1 - c), device_id_type=MESH) for a in range(n)]

    def start(*refs):
        for cp in copies(*refs):
            cp.start()

    def finish(*refs):
        for cp in copies(*refs):
            cp.wait()

    return _Rider(list(parts), [jax.ShapeDtypeStruct((p.shape[0],) + p.shape[2:], p.dtype) for p in parts],
                  [pltpu.SemaphoreType.DMA((n,)), pltpu.SemaphoreType.DMA((n,))], start, None, finish)


def _scatter_rider(sums):
    n = len(sums)

    def copy(refs, a, j, block):
        s_refs, out_refs, (send_sems, recv_sems, _) = refs
        x, y, c, chips = _place()
        px, py = chips[j]
        return pltpu.make_async_remote_copy(
            src_ref=s_refs[a].at[2 * px + py], dst_ref=out_refs[a].at[block],
            send_sem=send_sems.at[a, j], recv_sem=recv_sems.at[a, j], device_id=(px, py, c), device_id_type=MESH)

    def local(refs, a):
        x, y, c, _ = _place()
        return pltpu.make_async_copy(refs[0][a].at[2 * x + y], refs[1][a].at[2 * x + y], refs[2][2].at[a])

    def sends(refs):
        x, y, c, _ = _place()
        return [copy(refs, a, j, 2 * x + y) for j in range(3) for a in range(n)]

    def start(*refs):
        for a in range(n):
            local(refs, a).start()
        for cp in sends(refs):
            cp.start()

    def finish(*refs):
        x, y, c, chips = _place()
        for j, (px, py) in enumerate(chips):
            for a in range(n):
                copy(refs, a, j, 2 * px + py).wait_recv()
        for cp in sends(refs):
            cp.wait_send()
        for a in range(n):
            local(refs, a).wait()

    return _Rider(list(sums), [jax.ShapeDtypeStruct(s.shape, s.dtype) for s in sums],
                  [pltpu.SemaphoreType.DMA((n, 3)), pltpu.SemaphoreType.DMA((n, 3)), pltpu.SemaphoreType.DMA((n,))],
                  start, None, finish)


BIG = (("ffn1_w_gate_up", 2), ("ffn1_w_down", 1), ("ffn2_w_gate_up", 2), ("ffn2_w_down", 1), ("ssm_w_in", 1),
       ("ssm_w_out", 2), ("w_kv", 0), ("attn_w_q", 1), ("attn_w_o", 1))
SMALL = ("ffn1_norm", "mix_norm", "ffn2_norm", "ssm_lambda_re", "ssm_lambda_im", "ssm_b_re", "ssm_b_im",
         "ssm_c_re", "ssm_c_im", "ssm_log_step", "kv_norm", "k_norm", "q_norm", "attn_sinks")
COLS = (("meta_tokens", 1), ("ssm_d", 1))
WEIGHTS = ("meta_tokens", "ffn1_norm", "ffn1_w_gate_up", "ffn1_w_down", "mix_norm", "ffn2_norm", "ffn2_w_gate_up",
           "ffn2_w_down", "ssm_w_in", "ssm_lambda_re", "ssm_lambda_im", "ssm_b_re", "ssm_b_im", "ssm_c_re",
           "ssm_c_im", "ssm_log_step", "ssm_d", "ssm_w_out", "kv_norm", "w_kv", "k_norm", "attn_w_q", "q_norm",
           "attn_sinks", "attn_w_o")


def _rows_of(a, width):
    n = math.prod(a.shape)
    if n % width == 0:
        r = a.reshape(n // width, width)
    else:
        assert n < width
        r = jnp.pad(a.reshape(1, n), ((0, 0), (0, width - n)))
    return jnp.pad(r, ((0, (-r.shape[0]) % 8), (0, 0)))


def _pack_small(arrs, width):
    return jnp.concatenate([_rows_of(a.astype(F32), width) for a in arrs], axis=0)


def _unpack_small(buf, shapes, width):
    out, off = [], 0
    for shp in shapes:
        n = math.prod(shp)
        r = max(n // width, 1)
        out.append(buf[off:off + r].reshape(shp) if n % width == 0 else buf[off, :n].reshape(shp))
        off += r + (-r) % 8
    return out


def _shape2d(shp):
    return (math.prod(shp[:-1]), shp[-1])


def _unshard(g, axis):
    g = jnp.moveaxis(g, 0, axis)
    shp = g.shape
    return g.reshape(shp[:axis] + (shp[axis] * shp[axis + 1],) + shp[axis + 2:])


def _shard(full, axis):
    shp = full.shape
    g = full.reshape(shp[:axis] + (N_DEV, shp[axis] // N_DEV) + shp[axis + 1:])
    return jnp.moveaxis(g, axis, 0)


def _blockdiag(blocks):
    g, r, c = blocks.shape
    eye = jnp.eye(g, dtype=blocks.dtype)
    return (eye[:, None, :, None] * blocks[:, :, None, :]).reshape(g * r, g * c)


def _diagblocks(full, g):
    r, c = full.shape[0] // g, full.shape[1] // g
    f = full.reshape(g, r, g, c)
    idx = jnp.arange(g)
    return f[idx, :, idx, :]


def kernel(x, meta_tokens, ffn1_norm, ffn1_w_gate_up, ffn1_w_down, mix_norm, ffn2_norm, ffn2_w_gate_up, ffn2_w_down, ssm_w_in, ssm_lambda_re, ssm_lambda_im, ssm_b_re, ssm_b_im, ssm_c_re, ssm_c_im, ssm_log_step, ssm_d, ssm_w_out, kv_norm, w_kv, k_norm, attn_w_q, q_norm, attn_sinks, attn_w_o, loss_target, m_meta_tokens, m_ffn1_norm, m_ffn1_w_gate_up, m_ffn1_w_down, m_mix_norm, m_ffn2_norm, m_ffn2_w_gate_up, m_ffn2_w_down, m_ssm_w_in, m_ssm_lambda_re, m_ssm_lambda_im, m_ssm_b_re, m_ssm_b_im, m_ssm_c_re, m_ssm_c_im, m_ssm_log_step, m_ssm_d, m_ssm_w_out, m_kv_norm, m_w_kv, m_k_norm, m_attn_w_q, m_q_norm, m_attn_sinks, m_attn_w_o, v_meta_tokens, v_ffn1_norm, v_ffn1_w_gate_up, v_ffn1_w_down, v_mix_norm, v_ffn2_norm, v_ffn2_w_gate_up, v_ffn2_w_down, v_ssm_w_in, v_ssm_lambda_re, v_ssm_lambda_im, v_ssm_b_re, v_ssm_b_im, v_ssm_c_re, v_ssm_c_im, v_ssm_log_step, v_ssm_d, v_ssm_w_out, v_kv_norm, v_w_kv, v_k_norm, v_attn_w_q, v_q_norm, v_attn_sinks, v_attn_w_o):
    args = dict(locals())
    W = {n: args[n] for n in WEIGHTS}
    M = {n: args["m_" + n] for n in WEIGHTS}
    V = {n: args["v_" + n] for n in WEIGHTS}
    my_x, my_y, my_c = (lax.axis_index(a) for a in MESH_AXES)
    my_dev = 4 * my_x + 2 * my_y + my_c

    big_names = [n for n, _ in BIG]
    s2d = {n: _shape2d(W[n].shape) for n in big_names}
    col_w = W["meta_tokens"].shape[1]

    grads, summed, small_parts = _local_step(x, loss_target, W, my_c.astype(jnp.int32).reshape(1))
    loss = lax.psum(grads.pop("loss"), MESH_AXES)
    grad_x = grads.pop("x")

    outs = [{}, {}, {}, {}]
    for n in big_names:
        r4 = _adamw("adamw_" + n, W[n].reshape(s2d[n]), M[n].reshape(s2d[n]), V[n].reshape(s2d[n]), summed[n])
        for k in range(4):
            outs[k][n] = r4[k].reshape(W[n].shape)

    small_names = list(SMALL) + [n for n, _ in COLS]
    small_shapes = [grads[n].shape for n in small_names]
    zero_cols = [jnp.zeros(grads[n].shape, F32) for n, _ in COLS]
    packs = lambda d: _pack_small([d[n] for n in SMALL] + zero_cols, PACK_W)
    r4 = _adamw("adamw_small", packs(W), packs(M), packs(V), small_parts)
    gsmall = None
    for k in range(4):
        un = dict(zip(small_names, _unpack_small(r4[k], small_shapes, PACK_W)))
        gsmall = un if k == 0 else gsmall
        outs[k].update({n: un[n] for n in SMALL})
    col_g = [lax.dynamic_slice_in_dim(gsmall[n], my_dev * W[n].shape[1], W[n].shape[1], axis=1) for n, _ in COLS]
    packc = lambda d: _pack_small([d[n] for n, _ in COLS], col_w)
    r4 = _adamw("adamw_cols", packc(W), packc(M), packc(V), _pack_small(col_g, col_w)[None])
    col_shapes = [W[n].shape for n, _ in COLS]
    for k in range(4):
        outs[k].update(dict(zip([n for n, _ in COLS], _unpack_small(r4[k], col_shapes, col_w))))

    res = [[outs[k][n] for n in WEIGHTS] for k in range(4)]
    return (loss, grad_x, *res[0], *res[1], *res[2], *res[3])


def _local_step(x, target, P, c_arr):
    bsz, seq, d = x.shape
    lp = seq + PAD
    rows = bsz * lp
    depth = P["ffn1_norm"].shape[0]
    assert depth == 2
    bf = lambda a: a.astype(BF16)
    row = lambda a: a.reshape(1, -1)

    def shard(n, l=None):
        a = P[n] if l is None else P[n][l]
        return bf(a.reshape(_shape2d(a.shape)))

    rowsharded = lambda g: g.reshape((g.shape[0] * g.shape[1],) + g.shape[2:])
    colsharded = lambda g: _unshard(g, 1)
    col_w = P["meta_tokens"].shape[1]
    g0 = _run_rider("gather_first", _gather_rider(
        [shard("ffn1_w_gate_up", 0), shard("ffn1_w_down", 0), shard("ssm_w_in", 0),
         _pack_small([P["meta_tokens"], P["ssm_d"]], col_w)]))
    ffn_w = {("ffn1", 0): (colsharded(g0[0]), rowsharded(g0[1]))}
    w_in = rowsharded(g0[2])
    meta_full = _unshard(g0[3][:, :N_META], 1)
    dvec = _unshard(g0[3][:, N_META:N_META + 1, :P["ssm_d"].shape[1]], 1)

    pos = (jnp.arange(lp, dtype=F32) - float(META0))[:, None]
    half = HEAD_DIM // 2
    freqs = ROPE_THETA ** (-jnp.arange(0, half, dtype=F32) * 2.0 / HEAD_DIM)
    ang = pos * freqs[None, :]
    cos_t = jnp.tile(jnp.cos(ang), (1, LANES // half))
    sin_t = jnp.tile(jnp.concatenate([-jnp.sin(ang), jnp.sin(ang)], axis=1), (1, LANES // HEAD_DIM))
    gi = jnp.arange(LANES) // HEAD_DIM
    gmat = jnp.where(gi[:, None] == gi[None, :], 1.0 / HEAD_DIM, 0.0).astype(BF16)

    g_n, c_n, p_n = P["ssm_lambda_re"].shape[1], SSM_GROUP, SSM_STATE
    ns = g_n * p_n
    lr = P["ssm_lambda_re"][0].reshape(g_n, 1, p_n)
    li = P["ssm_lambda_im"][0].reshape(g_n, 1, p_n)
    ls = P["ssm_log_step"][0].reshape(g_n, 1, 1)
    brt = P["ssm_b_re"][0].transpose(0, 2, 1)
    bit = P["ssm_b_im"][0].transpose(0, 2, 1)
    ar, ai, bbr, bbi = _s5_params_fwd(lr, li, ls, brt, bit)
    a2 = jnp.concatenate([ar.reshape(1, ns), ai.reshape(1, ns)], axis=0)
    bfull = jnp.concatenate([_blockdiag(bbr), _blockdiag(bbi)], axis=1)
    cre_t = P["ssm_c_re"][0].transpose(0, 2, 1)
    cim_t = P["ssm_c_im"][0].transpose(0, 2, 1)
    cfull = jnp.concatenate([_blockdiag(cre_t), -_blockdiag(cim_t)], axis=0)

    ffn = lambda which, l: (row(P[which + "_norm"][l]),) + ffn_w[which, l]
    mix0, mix1, kvn = row(P["mix_norm"][0]), row(P["mix_norm"][1]), row(P["kv_norm"])
    kgain = jnp.tile(P["k_norm"].reshape(1, HEAD_DIM), (1, KVW // HEAD_DIM))
    qgain = jnp.tile(P["q_norm"].reshape(1, HEAD_DIM), (1, d // HEAD_DIM))
    sinks = P["attn_sinks"].reshape(1, -1)

    h0 = _embed(x, meta_full).reshape(rows, d)
    h1, ab_f1_0, g_wout, g_gu, g_d, g_kv = _ffn_fwd("ffn1_0_fwd", h0, *ffn("ffn1", 0), rider=_gather_rider(
        [shard("ssm_w_out", 0), shard("ffn2_w_gate_up", 0), shard("ffn2_w_down", 0), shard("w_kv")]))
    w_out, w_kv = colsharded(g_wout), rowsharded(g_kv)
    ffn_w["ffn2", 0] = (colsharded(g_gu), rowsharded(g_d))
    u = _proj_fwd("ssm_in_fwd", h1, mix0, w_in)
    y, xs, u_perm = _s5_scan_fwd(u, bf(bfull), bf(cfull), a2, dvec, bsz)
    h2 = _glu_fwd(y, h1, w_out)
    h3, ab_f2_0, g_gu, g_d, g_q, g_o = _ffn_fwd("ffn2_0_fwd", h2, *ffn("ffn2", 0), rider=_gather_rider(
        [shard("ffn1_w_gate_up", 1), shard("ffn1_w_down", 1), shard("attn_w_q", 0), shard("attn_w_o", 0)]))
    w_q, w_o = rowsharded(g_q), rowsharded(g_o)
    ffn_w["ffn1", 1] = (colsharded(g_gu), rowsharded(g_d))
    kv = _proj_fwd("kv_fwd", h3, kvn, w_kv)
    k = _headrope_fwd("k_rope_fwd", kv, KVW, kgain, cos_t, sin_t, gmat, lp)
    h4, ab_f1_1, g_gu, g_d = _ffn_fwd("ffn1_1_fwd", h3, *ffn("ffn1", 1), rider=_gather_rider(
        [shard("ffn2_w_gate_up", 1), shard("ffn2_w_down", 1)]))
    ffn_w["ffn2", 1] = (colsharded(g_gu), rowsharded(g_d))
    q_raw = _proj_fwd("q_fwd", h4, mix1, w_q)
    q = _headrope_fwd("q_rope_fwd", q_raw, d, qgain, cos_t, sin_t, gmat, lp)
    r3 = lambda a: a.reshape(bsz, lp, a.shape[-1])
    o = _attn_fwd(r3(q), r3(k), r3(kv), sinks).reshape(rows, d)
    h5 = _lin_res_fwd("attn_out_fwd", o, w_o, h4)
    h6, ab_f2_1 = _ffn_fwd("ffn2_1_fwd", h5, *ffn("ffn2", 1))
    loss, dh6 = _loss(r3(h6), target)
    dh6 = dh6.reshape(rows, d)

    G = {"loss": loss[0, 0]}

    def ffn_back(name, which, l, h, ab, dout, rider=None):
        g, wgu, wd = ffn(which, l)
        dh, hn, dab, act, dg, *rode = _ffn_bwd(name, h, ab, dout, g, wgu, wd, rider=rider)
        parts = [_shard(_mm_tn(name + "_wgu", hn, dab), 1), _mm_tn_slots(name + "_wd", act, dout, 0.5)]
        return dh, dg, parts, rode

    swap_of = lambda parts: _swap_rider([p.reshape((4, 2) + p.shape[1:]) for p in parts])

    def pair_sums(tag, parts, theirs):
        return [_pair_sum("pair_sum_%s_%d" % (tag, k), p.reshape((4, 2) + p.shape[1:]), t, c_arr)
                for k, (p, t) in enumerate(zip(parts, theirs))]

    dh5, dg_f2_1, parts_a, _ = ffn_back("ffn2_1_bwd", "ffn2", 1, h5, ab_f2_1, dh6)
    do, dw_o, *theirs = _lin_bwd("attn_out_bwd", o, w_o, dh5, rider=swap_of(parts_a))
    sums_a = pair_sums("ffn2_1", parts_a, theirs)
    dq, dk, dv, dsinks = _attn_bwd(r3(q), r3(k), r3(kv), sinks, r3(o), r3(do))
    dq_raw, dqg = _headrope_bwd("q_rope_bwd", q_raw, d, dq.reshape(rows, d), qgain, cos_t, sin_t, gmat, lp)
    dh4, dg_mix1, dw_q = _proj_bwd("q_bwd", h4, mix1, w_q, dq_raw, dh5)
    dh3, dg_f1_1, parts_b, red_a = ffn_back("ffn1_1_bwd", "ffn1", 1, h3, ab_f1_1, dh4, rider=_scatter_rider(sums_a))
    dk_raw, dkg = _headrope_bwd("k_rope_bwd", kv, KVW, dk.reshape(rows, KVW), kgain, cos_t, sin_t, gmat, lp)
    dkv = _concat_cols("dkv_concat", dk_raw, dv.reshape(rows, KVW))
    dh3, dg_kvn, dw_kv, *theirs = _proj_bwd("kv_bwd", h3, kvn, w_kv, dkv, dh3, rider=swap_of(parts_b))
    sums_b = pair_sums("ffn1_1", parts_b, theirs)
    dh2, dg_f2_0, parts_c, red_b = ffn_back("ffn2_0_bwd", "ffn2", 0, h2, ab_f2_0, dh3, rider=_scatter_rider(sums_b))
    dy, dw_out, *theirs = _glu_bwd(y, dh2, w_out, rider=swap_of(parts_c))
    sums_c = pair_sums("ffn2_0", parts_c, theirs)
    ctfull = jnp.concatenate([_blockdiag(P["ssm_c_re"][0]), -_blockdiag(P["ssm_c_im"][0])], axis=1)
    btfull = jnp.concatenate([_blockdiag(bbr.transpose(0, 2, 1)), _blockdiag(bbi.transpose(0, 2, 1))], axis=0)
    du, gx, dy_perm, da, dd = _s5_scan_bwd(dy, u, xs, bf(ctfull), bf(btfull), a2, dvec, bsz)
    dbfull = _mm_tn_blockdiag("ssm_db", u_perm, gx, False)
    dcfull = _mm_tn_blockdiag("ssm_dc", xs, dy_perm, True)
    dh1, dg_mix0, dw_in = _proj_bwd("ssm_in_bwd", h1, mix0, w_in, du, dh2)
    dh0, dg_f1_0, parts_d, red_c = ffn_back("ffn1_0_bwd", "ffn1", 0, h0, ab_f1_0, dh1, rider=_scatter_rider(sums_c))
    dbbr = _diagblocks(dbfull[:, :ns], g_n)
    dbbi = _diagblocks(dbfull[:, ns:], g_n)
    dlr, dli, dls, dbrt, dbit = _s5_params_bwd(lr, li, ls, brt, bit, da[:, :ns].reshape(g_n, 1, p_n),
                                               da[:, ns:].reshape(g_n, 1, p_n), dbbr, dbbi)
    dh0 = r3(dh0)
    G["x"] = dh0[:, PAD:, :]
    G["meta_tokens"] = _meta_sum(dh0)
    G["ffn1_norm"] = jnp.concatenate([dg_f1_0, dg_f1_1], axis=0)
    G["ffn2_norm"] = jnp.concatenate([dg_f2_0, dg_f2_1], axis=0)
    G["mix_norm"] = jnp.concatenate([dg_mix0, dg_mix1], axis=0)
    G["ssm_lambda_re"] = dlr.reshape(1, g_n, p_n)
    G["ssm_lambda_im"] = dli.reshape(1, g_n, p_n)
    G["ssm_log_step"] = dls.reshape(1, g_n)
    G["ssm_b_re"] = dbrt.transpose(0, 2, 1)[None]
    G["ssm_b_im"] = dbit.transpose(0, 2, 1)[None]
    G["ssm_c_re"] = _diagblocks(dcfull[:ns], g_n).transpose(0, 2, 1)[None]
    G["ssm_c_im"] = -_diagblocks(dcfull[ns:], g_n).transpose(0, 2, 1)[None]
    G["ssm_d"] = dd
    G["kv_norm"] = dg_kvn.reshape(-1)
    G["k_norm"] = dkg[0, :HEAD_DIM]
    G["q_norm"] = dqg[:, :HEAD_DIM]
    G["attn_sinks"] = dsinks[:, :N_KV_HEADS * Q_PER_KV]

    slots = lambda g: g.reshape((N_DEV, g.shape[0] // N_DEV) + g.shape[1:])
    parts_d = parts_d + [slots(dw_in), dw_out, slots(dw_kv), slots(dw_q), slots(dw_o)]
    small_pack = _pack_small([G[n] for n in list(SMALL) + [n for n, _ in COLS]], PACK_W)
    *theirs, small_parts = _run_rider("grad_swap_last", _join_riders(swap_of(parts_d), _gather_rider([small_pack])))
    red_d = _run_rider("grad_scatter_last", _scatter_rider(pair_sums("last", parts_d, theirs)))
    both = lambda lo, hi: jnp.concatenate([lo, hi], axis=1)
    summed = {"ffn1_w_gate_up": both(red_d[0], red_b[0]), "ffn1_w_down": both(red_d[1], red_b[1]),
              "ffn2_w_gate_up": both(red_c[0], red_a[0]), "ffn2_w_down": both(red_c[1], red_a[1]),
              "ssm_w_in": red_d[2], "ssm_w_out": red_d[3], "w_kv": red_d[4], "attn_w_q": red_d[5],
              "attn_w_o": red_d[6]}
    return G, summed, small_parts
```

```python
import functools
import math

import jax
import jax.numpy as jnp
from jax import lax
from jax.experimental import pallas as pl
from jax.experimental.pallas import tpu as pltpu

F32 = jnp.float32
BF16 = jnp.bfloat16

N_META = 16
PAD = 128
META0 = PAD - N_META
HEAD_DIM = 64
N_KV_HEADS = 4
Q_PER_KV = 4
SSM_GROUP = 16
SSM_STATE = 64
EPS = 1e-6
NEG_INF = -1e30
ROPE_THETA = 10000.0
ADAM_LR, ADAM_B1, ADAM_B2, ADAM_EPS, ADAM_WD, ADAM_STEP = 0.001, 0.9, 0.999, 1e-08, 0.01, 10
LANES = 128
PACK_W = 1024
VMEM_LIMIT = 56 * 1024 * 1024
MESH_AXES = ("x", "y", "c")
N_DEV = 8


def _cparams(sem=None):
    return pltpu.CompilerParams(dimension_semantics=sem, vmem_limit_bytes=VMEM_LIMIT)


def _row_tile(rows):
    for tm in (384, 256, 128, 64, 32, 16, 8):
        if rows % tm == 0:
            return tm
    raise ValueError(rows)


STREAM_BUDGET = 32 * 1024 * 1024


def _stream_tile(rows, bytes_per_row):
    for tm in range(rows, 0, -1):
        if rows % tm == 0 and (tm % 16 == 0 or tm == rows) and 2 * tm * bytes_per_row <= STREAM_BUDGET:
            return tm
    raise ValueError(rows)


TN_BUDGET = 52 * 1024 * 1024
TN_MAX_ROWS = 2816


def _tn_tile(rows, a, b, k1, tn):
    sa, sb = a.dtype.itemsize, b.dtype.itemsize
    fits = lambda tm: 2 * tm * (k1 * sa + tn * sb) + 3 * k1 * tn * 4 + tm * k1 * 2 <= TN_BUDGET
    divisors = [tm for tm in range(min(rows, TN_MAX_ROWS), 7, -8) if rows % tm == 0 and fits(tm)]
    good = [tm for tm in divisors if -(-tm // MXU_DIM) * MXU_DIM <= 1.1 * tm]
    if good or divisors:
        return (good or divisors)[0]
    raise ValueError(rows)


def _dot(a, b):
    return jnp.dot(a.astype(BF16), b.astype(BF16), preferred_element_type=F32)


def _dot_nt(a, b):
    return lax.dot_general(a.astype(BF16), b.astype(BF16), (((1,), (1,)), ((), ())), preferred_element_type=F32)


def _dot_tn(a, b):
    return lax.dot_general(a.astype(BF16), b.astype(BF16), (((0,), (0,)), ((), ())), preferred_element_type=F32)


def _rms(x, g):
    rstd = lax.rsqrt(jnp.mean(x * x, axis=-1, keepdims=True) + EPS)
    y = x * rstd
    return y * g, y, rstd


def _rms_bwd(dhn, y, rstd, g):
    dyn = dhn * g
    dx = rstd * (dyn - y * jnp.mean(dyn * y, axis=-1, keepdims=True))
    return dx, jnp.sum(dhn * y, axis=0, keepdims=True)


def _sigmoid(x):
    return 1.0 / (1.0 + jnp.exp(-x))


_GELU_C = math.sqrt(2.0 / math.pi)


def _gelu(y):
    t = jnp.tanh(_GELU_C * (y + 0.044715 * y * y * y))
    return 0.5 * y * (1.0 + t), t


def _gelu_grad(y, t):
    return 0.5 * (1.0 + t) + 0.5 * y * (1.0 - t * t) * _GELU_C * (1.0 + 3.0 * 0.044715 * y * y)


class _Rider:
    def __init__(self, ins, outs, sems, start, mid, finish):
        self.ins, self.outs, self.sems, self.start, self.mid, self.finish = ins, outs, sems, start, mid, finish


def _join_riders(r1, r2):
    ni, no, ns = len(r1.ins), len(r1.outs), len(r1.sems)

    def both(f1, f2):
        def phase(ins, outs, sems):
            if f1 is not None:
                f1(ins[:ni], outs[:no], sems[:ns])
            if f2 is not None:
                f2(ins[ni:], outs[no:], sems[ns:])
        return phase

    mid = both(r1.mid, r2.mid) if (r1.mid is not None or r2.mid is not None) else None
    return _Rider(r1.ins + r2.ins, r1.outs + r2.outs, r1.sems + r2.sems,
                  both(r1.start, r2.start), mid, both(r1.finish, r2.finish))


def _run_rider(name, rider):
    def kern(*refs):
        ni, no = len(rider.ins), len(rider.outs)
        parts = refs[:ni], refs[ni:ni + no], refs[ni + no:]
        rider.start(*parts)
        if rider.mid is not None:
            rider.mid(*parts)
        rider.finish(*parts)

    return pl.pallas_call(
        kern, name=name, out_shape=list(rider.outs), in_specs=[ANY] * len(rider.ins),
        out_specs=[ANY] * len(rider.outs), scratch_shapes=list(rider.sems),
    )(*rider.ins)


def _rowcall(name, body, rows, row_ins, const_ins, row_outs, acc_outs=(), tm=None, row_in_maps=None, rider=None):
    tm = tm or _row_tile(rows)
    steps = rows // tm
    in_specs = []
    for k, a in enumerate(row_ins):
        if row_in_maps is not None and row_in_maps[k] is not None:
            in_specs.append(pl.BlockSpec(*row_in_maps[k]))
        else:
            in_specs.append(pl.BlockSpec((tm, a.shape[1]), lambda i: (i, 0)))
    for a in const_ins:
        in_specs.append(pl.BlockSpec(a.shape, lambda i, nd=a.ndim: (0,) * nd, pipeline_mode=pl.Buffered(1)))
    out_shape, out_specs = [], []
    for w, dt in row_outs:
        out_shape.append(jax.ShapeDtypeStruct((rows, w), dt))
        out_specs.append(pl.BlockSpec((tm, w), lambda i: (i, 0)))
    for shp, dt in acc_outs:
        out_shape.append(jax.ShapeDtypeStruct(shp, dt))
        out_specs.append(pl.BlockSpec(shp, lambda i, nd=len(shp): (0,) * nd))

    if rider is None:
        def kern(*refs):
            body(pl.program_id(0), *refs)

        return pl.pallas_call(
            kern, name=name, grid=(steps,), in_specs=in_specs, out_specs=out_specs, out_shape=out_shape,
            compiler_params=_cparams(("arbitrary",)),
        )(*row_ins, *const_ins)

    n_in, n_out = len(in_specs), len(out_specs)
    r_in, r_out = len(rider.ins), len(rider.outs)

    def kern_r(*refs):
        step = pl.program_id(0)
        ins, rins = refs[:n_in], refs[n_in:n_in + r_in]
        outs = refs[n_in + r_in:n_in + r_in + n_out]
        routs = refs[n_in + r_in + n_out:n_in + r_in + n_out + r_out]
        sems = refs[n_in + r_in + n_out + r_out:]

        @pl.when(step == 0)
        def _():
            rider.start(rins, routs, sems)

        if rider.mid is not None:
            @pl.when(step == (3 * steps) // 4)
            def _():
                rider.mid(rins, routs, sems)

        body(step, *ins, *outs)

        @pl.when(step == steps - 1)
        def _():
            rider.finish(rins, routs, sems)

    return pl.pallas_call(
        kern_r, name=name, grid=(steps,), in_specs=in_specs + [ANY] * r_in, out_specs=out_specs + [ANY] * r_out,
        out_shape=out_shape + list(rider.outs), scratch_shapes=list(rider.sems),
        compiler_params=_cparams(("arbitrary",)),
    )(*row_ins, *const_ins, *rider.ins)


def _acc(step, ref, val):
    @pl.when(step == 0)
    def _():
        ref[...] = val

    @pl.when(step != 0)
    def _():
        ref[...] += val


def _embed(x, meta):
    bsz, seq, d = x.shape
    nb = seq // PAD + 1

    def kern(x_ref, m_ref, o_ref):
        i = pl.program_id(1)

        @pl.when(i == 0)
        def _():
            o_ref[0, 0:META0, :] = jnp.zeros((META0, d), F32)
            o_ref[0, META0:PAD, :] = m_ref[...]

        @pl.when(i != 0)
        def _():
            o_ref[0] = x_ref[0]

    return pl.pallas_call(
        kern, name="embed", grid=(bsz, nb),
        in_specs=[pl.BlockSpec((1, PAD, d), lambda b, i: (b, jnp.maximum(i - 1, 0), 0)),
                  pl.BlockSpec((N_META, d), lambda b, i: (0, 0))],
        out_specs=pl.BlockSpec((1, PAD, d), lambda b, i: (b, i, 0)),
        out_shape=jax.ShapeDtypeStruct((bsz, seq + PAD, d), F32),
        compiler_params=_cparams(("arbitrary", "arbitrary")),
    )(x, meta)


def _loss(h6, target):
    bsz, lp, d = h6.shape
    nb = lp // PAD

    def kern(h_ref, t_ref, l_ref, d_ref):
        b, i = pl.program_id(0), pl.program_id(1)

        @pl.when((b == 0) & (i == 0))
        def _():
            l_ref[...] = jnp.zeros_like(l_ref)

        @pl.when(i == 0)
        def _():
            d_ref[0] = jnp.zeros((PAD, d), F32)

        @pl.when(i != 0)
        def _():
            e = h_ref[0] - t_ref[0]
            d_ref[0] = e * (1.0 / d)
            l_ref[...] += 0.5 * jnp.sum(jnp.mean(e * e, axis=-1, keepdims=True))

    return pl.pallas_call(
        kern, name="loss", grid=(bsz, nb),
        in_specs=[pl.BlockSpec((1, PAD, d), lambda b, i: (b, i, 0)),
                  pl.BlockSpec((1, PAD, d), lambda b, i: (b, jnp.maximum(i - 1, 0), 0))],
        out_specs=[pl.BlockSpec((1, LANES), lambda b, i: (0, 0)),
                   pl.BlockSpec((1, PAD, d), lambda b, i: (b, i, 0))],
        out_shape=[jax.ShapeDtypeStruct((1, LANES), F32), jax.ShapeDtypeStruct((bsz, lp, d), F32)],
        compiler_params=_cparams(("arbitrary", "arbitrary")),
    )(h6, target)


def _meta_sum(dh0):
    bsz, lp, d = dh0.shape

    def kern(d_ref, o_ref):
        _acc(pl.program_id(0), o_ref, d_ref[0, META0:PAD, :])

    return pl.pallas_call(
        kern, name="meta_sum", grid=(bsz,),
        in_specs=[pl.BlockSpec((1, PAD, d), lambda b: (b, 0, 0))],
        out_specs=pl.BlockSpec((N_META, d), lambda b: (0, 0)),
        out_shape=jax.ShapeDtypeStruct((N_META, d), F32),
        compiler_params=_cparams(("arbitrary",)),
    )(dh0)


MXU_DIM = 256


def _ffn_chunks(f):
    unit = MXU_DIM if f % MXU_DIM == 0 else LANES
    assert f % unit == 0
    first = (f // unit + 1) // 2 * unit
    return [(0, first), (first, f)] if first < f else [(0, f)]


def _ffn_fwd(name, h, g, wgu, wd, rider=None):
    rows, d = h.shape
    f = wd.shape[0]
    chunks = _ffn_chunks(f)

    def body(step, h_ref, g_ref, wgu_ref, wd_ref, o_ref, ab_ref):
        hx = h_ref[...]
        hb = _rms(hx, g_ref[...])[0].astype(BF16)
        acc = jnp.zeros(hx.shape, F32)
        for lo, hi in chunks:
            ga, ua = slice(lo, hi), slice(f + lo, f + hi)
            a = _dot(hb, wgu_ref[:, ga])
            b = _dot(hb, wgu_ref[:, ua])
            ab_ref[:, ga] = a.astype(BF16)
            ab_ref[:, ua] = b.astype(BF16)
            acc = acc + _dot(a * _sigmoid(a) * b, wd_ref[ga, :])
        o_ref[...] = hx + 0.5 * acc

    return _rowcall(name, body, rows, [h], [g, wgu, wd], [(d, F32), (2 * f, BF16)], rider=rider)


def _ffn_bwd(name, h, ab, dout, g, wgu, wd, rider=None):
    rows, d = h.shape
    f = wd.shape[0]
    chunks = _ffn_chunks(f)

    def body(step, h_ref, ab_ref, do_ref, g_ref, wgu_ref, wd_ref, dh_ref, hn_ref, dab_ref, act_ref, dg_ref):
        hx, dout_x, gx = h_ref[...], do_ref[...], g_ref[...]
        hn, y, rstd = _rms(hx, gx)
        hn_ref[...] = hn.astype(BF16)
        dhalf = (0.5 * dout_x).astype(BF16)
        dhn = jnp.zeros(hx.shape, F32)
        for lo, hi in chunks:
            ga, ua = slice(lo, hi), slice(f + lo, f + hi)
            a = ab_ref[:, ga].astype(F32)
            b = ab_ref[:, ua].astype(F32)
            s = _sigmoid(a)
            silu = a * s
            act_ref[:, ga] = (silu * b).astype(BF16)
            dact = _dot_nt(dhalf, wd_ref[ga, :])
            da = (dact * b * (s + silu * (1.0 - s))).astype(BF16)
            db = (dact * silu).astype(BF16)
            dab_ref[:, ga] = da
            dab_ref[:, ua] = db
            dhn = dhn + _dot_nt(da, wgu_ref[:, ga]) + _dot_nt(db, wgu_ref[:, ua])
        dx, dg = _rms_bwd(dhn, y, rstd, gx)
        dh_ref[...] = dout_x + dx
        _acc(step, dg_ref, dg)

    return _rowcall(name, body, rows, [h, ab, dout], [g, wgu, wd],
                    [(d, F32), (d, BF16), (2 * f, BF16), (f, BF16)], [((1, d), F32)], rider=rider)


def _mm_tn(name, a, b, scale=1.0):
    rows, k1 = a.shape
    k2 = b.shape[1]
    tn = k2
    for cand in (512, 704, 1408, 1024):
        if k2 % cand == 0 and k1 * cand * 4 <= 6 * 1024 * 1024:
            tn = cand
    tm = _tn_tile(rows, a, b, k1, tn)
    steps = rows // tm

    def kern(a_ref, b_ref, o_ref):
        bx = b_ref[...]
        if scale != 1.0:
            bx = bx * scale
        _acc(pl.program_id(1), o_ref, _dot_tn(a_ref[...], bx))

    return pl.pallas_call(
        kern, name=name, grid=(k2 // tn, steps),
        in_specs=[pl.BlockSpec((tm, k1), lambda j, i: (i, 0)), pl.BlockSpec((tm, tn), lambda j, i: (i, j))],
        out_specs=pl.BlockSpec((k1, tn), lambda j, i: (0, j)),
        out_shape=jax.ShapeDtypeStruct((k1, k2), F32),
        compiler_params=_cparams(("arbitrary", "arbitrary")),
    )(a, b)


def _mm_tn_blockdiag(name, a, b, states_first):
    rows = a.shape[0]
    ka, kb = a.shape[1], b.shape[1]
    qa, qb = (ka // 4, kb // 2) if states_first else (ka // 2, kb // 4)
    tm = _tn_tile(rows, a, b, qa, qb)
    steps = rows // tm
    wide = lambda part, k: 2 * part + k
    amap = (lambda p, k, i: (i, wide(p, k))) if states_first else (lambda p, k, i: (i, k))
    bmap = (lambda p, k, i: (i, k)) if states_first else (lambda p, k, i: (i, wide(p, k)))
    omap = (lambda p, k, i: (wide(p, k), k)) if states_first else (lambda p, k, i: (k, wide(p, k)))

    def kern(a_ref, b_ref, o_ref):
        _acc(pl.program_id(2), o_ref, _dot_tn(a_ref[...], b_ref[...]))

    return pl.pallas_call(
        kern, name=name, grid=(2, 2, steps),
        in_specs=[pl.BlockSpec((tm, qa), amap), pl.BlockSpec((tm, qb), bmap)],
        out_specs=pl.BlockSpec((qa, qb), omap), out_shape=jax.ShapeDtypeStruct((ka, kb), F32),
        compiler_params=_cparams(("arbitrary", "arbitrary", "arbitrary")),
    )(a, b)


def _mm_tn_slots(name, a, b, scale):
    rows, k1 = a.shape
    k2 = b.shape[1]
    tn = 512 if k2 % 512 == 0 else k2
    sr = k1 // N_DEV
    tm = _tn_tile(rows, a, b, k1, tn)
    steps = rows // tm

    def kern(a_ref, b_ref, o_ref):
        bx = b_ref[...]
        if scale != 1.0:
            bx = bx * scale
        res = _dot_tn(a_ref[...], bx)
        step = pl.program_id(1)
        for s in range(N_DEV):
            _acc(step, o_ref.at[s], res[s * sr:(s + 1) * sr])

    return pl.pallas_call(
        kern, name=name, grid=(k2 // tn, steps),
        in_specs=[pl.BlockSpec((tm, k1), lambda j, i: (i, 0)), pl.BlockSpec((tm, tn), lambda j, i: (i, j))],
        out_specs=pl.BlockSpec((N_DEV, sr, tn), lambda j, i: (0, 0, j)),
        out_shape=jax.ShapeDtypeStruct((N_DEV, sr, k2), F32),
        compiler_params=_cparams(("arbitrary", "arbitrary")),
    )(a, b)


def _proj_fwd(name, h, g, w):
    rows = h.shape[0]

    def body(step, h_ref, g_ref, w_ref, o_ref):
        o_ref[...] = _dot(_rms(h_ref[...], g_ref[...])[0], w_ref[...])

    return _rowcall(name, body, rows, [h], [g, w], [(w.shape[1], F32)])[0]


def _proj_bwd(name, h, g, w, dy, dres, rider=None):
    rows, d = h.shape

    def body(step, h_ref, dy_ref, dr_ref, g_ref, w_ref, dh_ref, dg_ref, dw_ref):
        gx = g_ref[...]
        hn, y, rstd = _rms(h_ref[...], gx)
        dyx = dy_ref[...]
        dx, dg = _rms_bwd(_dot_nt(dyx, w_ref[...]), y, rstd, gx)
        dh_ref[...] = dr_ref[...] + dx
        _acc(step, dg_ref, dg)
        _acc(step, dw_ref, _dot_tn(hn, dyx))

    return _rowcall(name, body, rows, [h, dy, dres], [g, w], [(d, F32)], [((1, d), F32), (w.shape, F32)],
                    rider=rider)


def _lin_res_fwd(name, a, w, res):
    rows = a.shape[0]

    def body(step, a_ref, r_ref, w_ref, o_ref):
        o_ref[...] = r_ref[...] + _dot(a_ref[...], w_ref[...])

    return _rowcall(name, body, rows, [a, res], [w], [(w.shape[1], F32)])[0]


def _lin_bwd(name, a, w, dy, rider=None):
    rows, k = a.shape

    def body(step, a_ref, dy_ref, w_ref, da_ref, dw_ref):
        dyx = dy_ref[...]
        da_ref[...] = _dot_nt(dyx, w_ref[...])
        _acc(step, dw_ref, _dot_tn(a_ref[...], dyx))

    return _rowcall(name, body, rows, [a, dy], [w], [(k, F32)], [(w.shape, F32)], rider=rider)


def _s5_param_fn(lr, li, ls, brt, bit):
    step = jnp.exp(ls)
    mag = jnp.exp(lr * step)
    ar = mag * jnp.cos(li * step)
    ai = mag * jnp.sin(li * step)
    den = lr * lr + li * li
    nr, ni = ar - 1.0, ai
    cr = (nr * lr + ni * li) / den
    ci = (ni * lr - nr * li) / den
    return ar, ai, cr * brt - ci * bit, cr * bit + ci * brt


def _s5_params_fwd(lr, li, ls, brt, bit):
    def kern(lr_ref, li_ref, ls_ref, br_ref, bi_ref, ar_ref, ai_ref, bbr_ref, bbi_ref):
        ar, ai, bbr, bbi = _s5_param_fn(lr_ref[...], li_ref[...], ls_ref[...], br_ref[...], bi_ref[...])
        ar_ref[...], ai_ref[...], bbr_ref[...], bbi_ref[...] = ar, ai, bbr, bbi

    sd = jax.ShapeDtypeStruct
    return pl.pallas_call(
        kern, name="s5_params_fwd",
        out_shape=[sd(lr.shape, F32), sd(lr.shape, F32), sd(brt.shape, F32), sd(brt.shape, F32)],
    )(lr, li, ls, brt, bit)


def _s5_params_bwd(lr, li, ls, brt, bit, dar, dai, dbbr, dbbi):
    def kern(lr_ref, li_ref, ls_ref, br_ref, bi_ref, dar_ref, dai_ref, dbbr_ref, dbbi_ref,
             dlr_ref, dli_ref, dls_ref, dbr_ref, dbi_ref):
        _, vjp = jax.vjp(_s5_param_fn, lr_ref[...], li_ref[...], ls_ref[...], br_ref[...], bi_ref[...])
        dlr, dli, dls, dbr, dbi = vjp((dar_ref[...], dai_ref[...], dbbr_ref[...], dbbi_ref[...]))
        dlr_ref[...], dli_ref[...], dls_ref[...], dbr_ref[...], dbi_ref[...] = dlr, dli, dls, dbr, dbi

    sd = jax.ShapeDtypeStruct
    return pl.pallas_call(
        kern, name="s5_params_bwd",
        out_shape=[sd(lr.shape, F32), sd(lr.shape, F32), sd(ls.shape, F32), sd(brt.shape, F32), sd(brt.shape, F32)],
    )(lr, li, ls, brt, bit, dar, dai, dbbr, dbbi)


SCAN_LW = 512


SCAN_SEGS = 8
SCAN_UNROLL = 4


def _cmul(xr, xi, yr, yi):
    return xr * yr - xi * yi, xr * yi + xi * yr


def _scan_tables(a_ref, tab_ref, conj, seg_len):
    ns = a_ref.shape[1]
    ar = jnp.broadcast_to(a_ref[0:1, :], (8, ns))
    ai = jnp.broadcast_to(a_ref[1:2, :], (8, ns))
    if conj:
        ai = -ai
    big, base, e = None, (ar, ai), seg_len
    while e:
        if e & 1:
            big = base if big is None else _cmul(*big, *base)
        base = _cmul(*base, *base)
        e >>= 1
    big2 = _cmul(*big, *big)
    big4 = _cmul(*big2, *big2)
    for k, v in enumerate((ar, ai) + big + big2 + big4):
        tab_ref[k] = v


def _scan_block(x_ref, tab_ref, carry_ref, t_rows, ns, reverse):
    sl = t_rows // SCAN_SEGS
    assert sl % SCAN_UNROLL == 0
    row = lax.broadcasted_iota(jnp.int32, (8, SCAN_LW), 0)
    zero = jnp.zeros((8, SCAN_LW), F32)
    for lc in range(ns // SCAN_LW):
        lre = pl.ds(lc * SCAN_LW, SCAN_LW)
        lim = pl.ds(ns + lc * SCAN_LW, SCAN_LW)
        ar, ai = tab_ref[0, :, lre], tab_ref[1, :, lre]

        def rows_of(k, u):
            j = k * SCAN_UNROLL + u
            return pl.ds(pl.multiple_of(((sl - 1 - j) if reverse else j) * SCAN_SEGS, SCAN_SEGS), SCAN_SEGS)

        def local(k, s, lre=lre, lim=lim, ar=ar, ai=ai):
            sr, si = s
            for u in range(SCAN_UNROLL):
                rows = rows_of(k, u)
                tr, ti = _cmul(ar, ai, sr, si)
                sr, si = x_ref[rows, lre] + tr, x_ref[rows, lim] + ti
                x_ref[rows, lre], x_ref[rows, lim] = sr, si
            return sr, si

        er, ei = lax.fori_loop(0, sl // SCAN_UNROLL, local, (zero, zero))
        if reverse:
            cr = jnp.where(row == 7, carry_ref[:, lre], pltpu.roll(er, 7, 0))
            ci = jnp.where(row == 7, carry_ref[:, lim], pltpu.roll(ei, 7, 0))
        else:
            cr = jnp.where(row == 0, carry_ref[:, lre], pltpu.roll(er, 1, 0))
            ci = jnp.where(row == 0, carry_ref[:, lim], pltpu.roll(ei, 1, 0))
        for lvl, dsh in enumerate((1, 2, 4)):
            pr, pi = tab_ref[2 + 2 * lvl, :, lre], tab_ref[3 + 2 * lvl, :, lre]
            if reverse:
                keep, shift = row < 8 - dsh, 8 - dsh
            else:
                keep, shift = row >= dsh, dsh
            sr = jnp.where(keep, pltpu.roll(cr, shift, 0), 0.0)
            si = jnp.where(keep, pltpu.roll(ci, shift, 0), 0.0)
            tr, ti = _cmul(pr, pi, sr, si)
            cr, ci = cr + tr, ci + ti
        tr, ti = _cmul(tab_ref[2, :, lre], tab_ref[3, :, lre], cr, ci)
        edge = 0 if reverse else 7
        carry_ref[:, lre] = jnp.broadcast_to((er + tr)[edge:edge + 1, :], (8, SCAN_LW))
        carry_ref[:, lim] = jnp.broadcast_to((ei + ti)[edge:edge + 1, :], (8, SCAN_LW))

        def fix(k, t, lre=lre, lim=lim, ar=ar, ai=ai):
            tr, ti = t
            for u in range(SCAN_UNROLL):
                rows = rows_of(k, u)
                tr, ti = _cmul(ar, ai, tr, ti)
                x_ref[rows, lre] = x_ref[rows, lre] + tr
                x_ref[rows, lim] = x_ref[rows, lim] + ti
            return tr, ti

        lax.fori_loop(0, sl // SCAN_UNROLL, fix, (cr, ci))


def _bd_expand(u, w_ref, x_ref, ns):
    hh, sh = u.shape[1] // 2, ns // 2
    ub = u.astype(BF16)
    for part in range(2):
        for k in range(2):
            cols = slice(part * ns + k * sh, part * ns + (k + 1) * sh)
            x_ref[:, cols] = jnp.dot(ub[:, k * hh:(k + 1) * hh], w_ref[k * hh:(k + 1) * hh, cols],
                                     preferred_element_type=F32)


def _bd_contract(x_ref, w_ref, ns):
    hh, sh = w_ref.shape[1] // 2, ns // 2
    halves = []
    for k in range(2):
        acc = None
        for part in range(2):
            rows = slice(part * ns + k * sh, part * ns + (k + 1) * sh)
            t = jnp.dot(x_ref[:, rows].astype(BF16), w_ref[rows, k * hh:(k + 1) * hh], preferred_element_type=F32)
            acc = t if acc is None else acc + t
        halves.append(acc)
    return jnp.concatenate(halves, axis=1)


def _scan_rows(lp):
    for t in (384, 256, 128):
        if lp % t == 0:
            return t
    raise ValueError(lp)


def _seg_perm(t_rows):
    r = jnp.arange(t_rows)
    src = (r % SCAN_SEGS) * (t_rows // SCAN_SEGS) + r // SCAN_SEGS
    p = (src[:, None] == r[None, :]).astype(BF16)
    return p, p.T


def _permute_rows(p_ref, v):
    return jnp.dot(p_ref[...], v.astype(BF16), preferred_element_type=F32)


def _unpermute_rows(pt_ref, v):
    hi = v.astype(BF16)
    lo = (v - hi.astype(F32)).astype(BF16)
    pt = pt_ref[...]
    return jnp.dot(pt, hi, preferred_element_type=F32) + jnp.dot(pt, lo, preferred_element_type=F32)


def _s5_scan_fwd(u, bfull, cfull, a2, dvec, bsz):
    rows, hw = u.shape
    ns = a2.shape[1]
    lp = rows // bsz
    t_rows = _scan_rows(lp)
    nc = lp // t_rows
    pmat, pmat_t = _seg_perm(t_rows)

    def kern(u_ref, b_ref, c_ref, a_ref, d_ref, p_ref, pt_ref, y_ref, x_ref, up_ref, tab_ref, carry_ref):
        c = pl.program_id(1)

        @pl.when((pl.program_id(0) == 0) & (c == 0))
        def _():
            _scan_tables(a_ref, tab_ref, False, t_rows // SCAN_SEGS)

        @pl.when(c == 0)
        def _():
            carry_ref[...] = jnp.zeros_like(carry_ref)

        ux = u_ref[...]
        up = _permute_rows(p_ref, ux)
        up_ref[...] = up.astype(BF16)
        _bd_expand(up, b_ref, x_ref, ns)
        _scan_block(x_ref, tab_ref, carry_ref, t_rows, ns, reverse=False)
        y_ref[...] = _unpermute_rows(pt_ref, _bd_contract(x_ref, c_ref, ns)) + d_ref[...] * ux

    const = lambda shp: pl.BlockSpec(shp, lambda b, c: (0,) * len(shp), pipeline_mode=pl.Buffered(1))
    blk = lambda b, c: (b * nc + c, 0)
    return pl.pallas_call(
        kern, name="s5_scan_fwd", grid=(bsz, nc),
        in_specs=[pl.BlockSpec((t_rows, hw), blk), const(bfull.shape), const(cfull.shape), const(a2.shape),
                  const(dvec.shape), const(pmat.shape), const(pmat.shape)],
        out_specs=[pl.BlockSpec((t_rows, hw), blk), pl.BlockSpec((t_rows, 2 * ns), blk),
                   pl.BlockSpec((t_rows, hw), blk)],
        out_shape=[jax.ShapeDtypeStruct((rows, hw), F32), jax.ShapeDtypeStruct((rows, 2 * ns), F32),
                   jax.ShapeDtypeStruct((rows, hw), BF16)],
        scratch_shapes=[pltpu.VMEM((8, 8, ns), F32), pltpu.VMEM((8, 2 * ns), F32)],
        compiler_params=_cparams(("arbitrary", "arbitrary")),
    )(u, bfull, cfull, a2, dvec, pmat, pmat_t)


def _s5_scan_bwd(dy, u, xs, ctfull, btfull, a2, dvec, bsz):
    rows, hw = u.shape
    ns = a2.shape[1]
    lp = rows // bsz
    t_rows = _scan_rows(lp)
    nc = lp // t_rows
    blk = lambda b, c: (b * nc + (nc - 1 - c), 0)
    pmat, pmat_t = _seg_perm(t_rows)

    def prev8(b, c):
        first = (b * nc + (nc - 1 - c)) * (t_rows // 8)
        return (jnp.maximum(first - 1, 0), 0)

    def kern(dy_ref, u_ref, x_ref, xp_ref, ct_ref, bt_ref, a_ref, d_ref, p_ref, pt_ref,
             du_ref, gx_ref, dyp_ref, da_ref, dd_ref, tab_ref, carry_ref):
        b, c = pl.program_id(0), pl.program_id(1)
        first = (b == 0) & (c == 0)

        @pl.when(first)
        def _():
            _scan_tables(a_ref, tab_ref, True, t_rows // SCAN_SEGS)

        @pl.when(c == 0)
        def _():
            carry_ref[...] = jnp.zeros_like(carry_ref)

        dyx, ux = dy_ref[...], u_ref[...]
        dyp = _permute_rows(p_ref, dyx)
        dyp_ref[...] = dyp.astype(BF16)
        _bd_expand(dyp, ct_ref, gx_ref, ns)
        _scan_block(gx_ref, tab_ref, carry_ref, t_rows, ns, reverse=True)
        gx = gx_ref[...]
        du_ref[...] = _unpermute_rows(pt_ref, _bd_contract(gx_ref, bt_ref, ns)) + d_ref[...] * dyx
        seq_start = c == nc - 1
        row8 = lax.broadcasted_iota(jnp.int32, (8, 1), 0)
        head = pltpu.roll(x_ref[t_rows - 8:t_rows, :], 1, 0)
        head = jnp.where(row8 == 0, jnp.where(seq_start, 0.0, xp_ref[7:8, :]), head)
        xprev = jnp.concatenate([head, x_ref[0:t_rows - 8, :]], axis=0)
        xr, xi, gr, gi = xprev[:, :ns], xprev[:, ns:], gx[:, :ns], gx[:, ns:]
        da = jnp.concatenate([jnp.sum(xr * gr + xi * gi, axis=0, keepdims=True),
                              jnp.sum(xr * gi - xi * gr, axis=0, keepdims=True)], axis=1)
        dd = jnp.sum(dyx * ux, axis=0, keepdims=True)

        @pl.when(first)
        def _():
            da_ref[...] = da
            dd_ref[...] = dd

        @pl.when(jnp.logical_not(first))
        def _():
            da_ref[...] += da
            dd_ref[...] += dd

    const = lambda shp: pl.BlockSpec(shp, lambda b, c: (0,) * len(shp), pipeline_mode=pl.Buffered(1))
    return pl.pallas_call(
        kern, name="s5_scan_bwd", grid=(bsz, nc),
        in_specs=[pl.BlockSpec((t_rows, hw), blk), pl.BlockSpec((t_rows, hw), blk),
                  pl.BlockSpec((t_rows, 2 * ns), blk), pl.BlockSpec((8, 2 * ns), prev8),
                  const(ctfull.shape), const(btfull.shape), const(a2.shape), const(dvec.shape),
                  const(pmat.shape), const(pmat.shape)],
        out_specs=[pl.BlockSpec((t_rows, hw), blk), pl.BlockSpec((t_rows, 2 * ns), blk),
                   pl.BlockSpec((t_rows, hw), blk),
                   pl.BlockSpec((1, 2 * ns), lambda b, c: (0, 0)), pl.BlockSpec((1, hw), lambda b, c: (0, 0))],
        out_shape=[jax.ShapeDtypeStruct((rows, hw), F32), jax.ShapeDtypeStruct((rows, 2 * ns), F32),
                   jax.ShapeDtypeStruct((rows, hw), BF16),
                   jax.ShapeDtypeStruct((1, 2 * ns), F32), jax.ShapeDtypeStruct((1, hw), F32)],
        scratch_shapes=[pltpu.VMEM((8, 8, ns), F32), pltpu.VMEM((8, 2 * ns), F32)],
        compiler_params=_cparams(("arbitrary", "arbitrary")),
    )(dy, u, xs, xs, ctfull, btfull, a2, dvec, pmat, pmat_t)


def _glu_fwd(y, h1, wout):
    rows, d = h1.shape

    def body(step, y_ref, h_ref, w_ref, o_ref):
        z = _dot(_gelu(y_ref[...])[0], w_ref[...])
        o_ref[...] = h_ref[...] + z[:, :d] * _sigmoid(z[:, d:])

    return _rowcall("glu_fwd", body, rows, [y, h1], [wout], [(d, F32)])[0]


def _glu_bwd(y, dh2, wout, rider=None):
    rows, d = dh2.shape
    hw = y.shape[1]

    def body(step, y_ref, dh_ref, w_ref, dy_ref, dw_ref):
        yx, dh = y_ref[...], dh_ref[...]
        gl, t = _gelu(yx)
        z = _dot(gl, w_ref[...])
        za, sg = z[:, :d], _sigmoid(z[:, d:])
        dza = dh * sg
        dzg = dh * za * sg * (1.0 - sg)
        dgl = _dot_nt(dza, w_ref[:, :d]) + _dot_nt(dzg, w_ref[:, d:])
        dy_ref[...] = dgl * _gelu_grad(yx, t)
        for half, dz in enumerate((dza, dzg)):
            dw = _dot_tn(gl, dz)
            for s in range(N_DEV // 2):
                _acc(step, dw_ref.at[half * (N_DEV // 2) + s], dw[:, s * cw:(s + 1) * cw])

    cw = 2 * d // N_DEV
    return _rowcall("glu_bwd", body, rows, [y, dh2], [wout], [(hw, F32)], [((N_DEV, hw, cw), F32)], rider=rider)


def _gmean64(x2, gmat):
    hi = x2.astype(BF16)
    r1 = x2 - hi.astype(F32)
    mid = r1.astype(BF16)
    lo = (r1 - mid.astype(F32)).astype(BF16)
    outs = []
    for j in range(x2.shape[1] // LANES):
        sl = slice(j * LANES, (j + 1) * LANES)
        f = lambda p: jnp.dot(p[:, sl], gmat, preferred_element_type=F32)
        outs.append(f(hi) + f(mid) + f(lo))
    return outs[0] if len(outs) == 1 else jnp.concatenate(outs, axis=1)


def _swap32(x):
    w = x.shape[1]
    lane = lax.broadcasted_iota(jnp.int32, (1, w), 1)
    return jnp.where((lane & 32) == 0, pltpu.roll(x, w - 32, 1), pltpu.roll(x, 32, 1))


def _tile_lanes(t, w):
    reps = w // t.shape[1]
    return t if reps == 1 else jnp.concatenate([t] * reps, axis=1)


def _headrope_fwd(name, raw, w, gain, cos, sin, gmat, lp):
    rows = raw.shape[0]
    tm = _row_tile(lp)
    per = lp // tm

    def body(step, x_ref, c_ref, s_ref, g_ref, gm_ref, o_ref):
        x = x_ref[...]
        rstd = lax.rsqrt(_gmean64(x * x, gm_ref[...]) + EPS)
        z = x * rstd * g_ref[...]
        o_ref[...] = z * _tile_lanes(c_ref[...], w) + _swap32(z) * _tile_lanes(s_ref[...], w)

    maps = [((tm, w), lambda i: (i, 0)), ((tm, LANES), lambda i: (i % per, 0)), ((tm, LANES), lambda i: (i % per, 0))]
    return _rowcall(name, body, rows, [raw, cos, sin], [gain, gmat], [(w, F32)], tm=tm, row_in_maps=maps)[0]


def _headrope_bwd(name, raw, w, dout, gain, cos, sin, gmat, lp):
    rows = raw.shape[0]
    tm = _row_tile(lp)
    per = lp // tm

    def body(step, x_ref, do_ref, c_ref, s_ref, g_ref, gm_ref, dx_ref, dg_ref):
        x, dout_x, gx, gm = x_ref[...], do_ref[...], g_ref[...], gm_ref[...]
        rstd = lax.rsqrt(_gmean64(x * x, gm) + EPS)
        yn = x * rstd
        dz = dout_x * _tile_lanes(c_ref[...], w) + _swap32(dout_x * _tile_lanes(s_ref[...], w))
        dyn = dz * gx
        dx_ref[...] = rstd * (dyn - yn * _gmean64(dyn * yn, gm))
        dg = jnp.sum(dz * yn, axis=0, keepdims=True)
        sh = w // 2
        while sh >= HEAD_DIM:
            dg = dg + pltpu.roll(dg, sh, 1)
            sh //= 2
        _acc(step, dg_ref, dg)

    maps = [((tm, w), lambda i: (i, 0)), None, ((tm, LANES), lambda i: (i % per, 0)), ((tm, LANES), lambda i: (i % per, 0))]
    return _rowcall(name, body, rows, [raw, dout, cos, sin], [gain, gmat], [(w, F32)], [((1, w), F32)],
                    tm=tm, row_in_maps=maps)


KVW = N_KV_HEADS * HEAD_DIM
QB = 128


def _fold4(x):
    y = x + pltpu.roll(x, 128, 1)
    return y + pltpu.roll(y, 64, 1)


ATTN_SCALE = HEAD_DIM ** -0.5
assert math.log2(ATTN_SCALE).is_integer()


def _attn_masks(i):
    k0j = lax.broadcasted_iota(jnp.int32, (Q_PER_KV * QB, QB), 1)
    qi = lax.broadcasted_iota(jnp.int32, (Q_PER_KV * QB, 2 * QB), 0) % QB
    kj = lax.broadcasted_iota(jnp.int32, (Q_PER_KV * QB, 2 * QB), 1)
    in_prev = (kj < QB) & (kj > qi) & (i >= 2)
    in_cur = (kj >= QB) & (kj - QB <= qi)
    return k0j >= META0, in_prev | in_cur


def _attn_scores(masks, q_ref, k0_ref, kp_ref, kc_ref, sink_ref, h):
    lane = lax.broadcasted_iota(jnp.int32, (1, KVW), 1) // HEAD_DIM
    qh = q_ref[:, h * KVW:(h + 1) * KVW] * ATTN_SCALE
    qs = jnp.concatenate([jnp.where(lane == g, qh, 0.0) for g in range(Q_PER_KV)], axis=0).astype(BF16)
    hsel = lane == h
    kx = _expand_kv((k0_ref, kp_ref, kc_ref), hsel)
    s0 = jnp.where(masks[0], _dot_nt(qs, kx[0]), NEG_INF)
    sb = jnp.where(masks[1], _dot_nt(qs, kx[1]), NEG_INF)
    rowg = lax.broadcasted_iota(jnp.int32, (Q_PER_KV * QB, 1), 0) // QB
    sink = jnp.zeros((Q_PER_KV * QB, 1), F32)
    for g in range(Q_PER_KV):
        sink = jnp.where(rowg == g, sink_ref[0, h * Q_PER_KV + g], sink)
    m = jnp.maximum(jnp.maximum(jnp.max(s0, axis=1, keepdims=True), jnp.max(sb, axis=1, keepdims=True)), sink)
    p0, pb, ps = jnp.exp(s0 - m), jnp.exp(sb - m), jnp.exp(sink - m)
    den = jnp.sum(p0, axis=1, keepdims=True) + jnp.sum(pb, axis=1, keepdims=True) + ps
    return qs, kx, (p0, pb), ps, den, lane, hsel


def _expand_kv(refs, hsel):
    x0, xp, xc = [_fold4(jnp.where(hsel, r[...], 0.0)).astype(BF16) for r in refs]
    return [x0, jnp.concatenate([xp, xc], axis=0)]


def _unstack(x, lane):
    out = jnp.where(lane == 0, x[0:QB], 0.0)
    for g in range(1, Q_PER_KV):
        out = out + jnp.where(lane == g, x[g * QB:(g + 1) * QB], 0.0)
    return out


def _attn_specs(nb, d):
    qspec = pl.BlockSpec((None, QB, d), lambda b, i: (b, i, 0))
    k0 = pl.BlockSpec((None, QB, KVW), lambda b, i: (b, 0, 0))
    kp = pl.BlockSpec((None, QB, KVW), lambda b, i: (b, jnp.maximum(i - 1, 0), 0))
    kc = pl.BlockSpec((None, QB, KVW), lambda b, i: (b, i, 0))
    v0 = pl.BlockSpec((None, QB, KVW), lambda b, i: (b, 0, 1))
    vp = pl.BlockSpec((None, QB, KVW), lambda b, i: (b, jnp.maximum(i - 1, 0), 1))
    vc = pl.BlockSpec((None, QB, KVW), lambda b, i: (b, i, 1))
    sink = pl.BlockSpec(memory_space=pltpu.SMEM)
    return qspec, [k0, kp, kc], [v0, vp, vc], sink


def _attn_fwd(q, k, kv, sinks):
    bsz, lp, d = q.shape
    nb = lp // QB
    qspec, kspecs, vspecs, sspec = _attn_specs(nb, d)

    def kern(q_ref, k0_ref, kp_ref, kc_ref, v0_ref, vp_ref, vc_ref, sink_ref, o_ref):
        masks = _attn_masks(pl.program_id(1))
        for h in range(N_KV_HEADS):
            qs, kx, ps3, psink, den, lane, hsel = _attn_scores(masks, q_ref, k0_ref, kp_ref, kc_ref, sink_ref, h)
            vx = _expand_kv((v0_ref, vp_ref, vc_ref), hsel)
            o = _dot(ps3[0], vx[0]) + _dot(ps3[1], vx[1])
            o_ref[:, h * KVW:(h + 1) * KVW] = _unstack(o / den, lane)

    return pl.pallas_call(
        kern, name="attn_fwd", grid=(bsz, nb),
        in_specs=[qspec] + kspecs + vspecs + [sspec],
        out_specs=qspec, out_shape=jax.ShapeDtypeStruct((bsz, lp, d), F32),
        compiler_params=_cparams(("arbitrary", "arbitrary")),
    )(q, k, k, k, kv, kv, kv, sinks)


def _attn_bwd(q, k, kv, sinks, o, do):
    bsz, lp, d = q.shape
    nb = lp // QB
    qspec, kspecs, vspecs, sspec = _attn_specs(nb, d)
    full = pl.BlockSpec((None, lp, KVW), lambda b, i: (b, 0, 0))

    def kern(q_ref, k0_ref, kp_ref, kc_ref, v0_ref, vp_ref, vc_ref, sink_ref, o_ref, do_ref,
             dq_ref, dk_ref, dv_ref, ds_ref):
        b, i = pl.program_id(0), pl.program_id(1)

        @pl.when(i == 0)
        def _():
            dk_ref[...] = jnp.zeros_like(dk_ref)
            dv_ref[...] = jnp.zeros_like(dv_ref)

        @pl.when((b == 0) & (i == 0))
        def _():
            ds_ref[...] = jnp.zeros_like(ds_ref)

        lane128 = lax.broadcasted_iota(jnp.int32, (1, LANES), 1)
        rowg = lax.broadcasted_iota(jnp.int32, (Q_PER_KV * QB, 1), 0) // QB
        dk_acc = [jnp.zeros((QB, KVW), F32), jnp.zeros((2 * QB, KVW), F32)]
        dv_acc = [jnp.zeros((QB, KVW), F32), jnp.zeros((2 * QB, KVW), F32)]
        dsink = jnp.zeros((1, LANES), F32)
        masks = _attn_masks(i)
        for h in range(N_KV_HEADS):
            qs, kx, ps3, psink, den, lane, hsel = _attn_scores(masks, q_ref, k0_ref, kp_ref, kc_ref, sink_ref, h)
            vx = _expand_kv((v0_ref, vp_ref, vc_ref), hsel)
            sl = slice(h * KVW, (h + 1) * KVW)
            doh, oh = do_ref[:, sl], o_ref[:, sl]
            dos = jnp.concatenate([jnp.where(lane == g, doh, 0.0) for g in range(Q_PER_KV)], axis=0)
            ost = jnp.concatenate([jnp.where(lane == g, oh, 0.0) for g in range(Q_PER_KV)], axis=0)
            delta = jnp.sum(dos * ost, axis=1, keepdims=True)
            inv = 1.0 / den
            dosb = dos.astype(BF16)
            dqs = jnp.zeros((Q_PER_KV * QB, KVW), F32)
            for n in range(2):
                pn = ps3[n] * inv
                ds = pn * (_dot_nt(dosb, vx[n]) - delta)
                dqs = dqs + _dot(ds, kx[n])
                dk_acc[n] = dk_acc[n] + jnp.where(hsel, _fold4(_dot_tn(ds, qs)), 0.0)
                dv_acc[n] = dv_acc[n] + jnp.where(hsel, _fold4(_dot_tn(pn, dosb)), 0.0)
            dq_ref[:, sl] = _unstack(dqs, lane) * ATTN_SCALE
            dsk = -(psink * inv) * delta
            for g in range(Q_PER_KV):
                val = jnp.sum(jnp.where(rowg == g, dsk, 0.0), axis=0, keepdims=True)
                dsink = dsink + jnp.where(lane128 == h * Q_PER_KV + g, val, 0.0)
        ds_ref[...] += dsink
        r0 = pl.ds(0, QB)
        rp = pl.ds(pl.multiple_of(jnp.maximum(i - 1, 0) * QB, QB), QB)
        rc = pl.ds(pl.multiple_of(i * QB, QB), QB)
        for acc, ref in ((dk_acc, dk_ref), (dv_acc, dv_ref)):
            ref[r0, :] += acc[0]
            ref[rp, :] += acc[1][:QB]
            ref[rc, :] += acc[1][QB:]

    return pl.pallas_call(
        kern, name="attn_bwd", grid=(bsz, nb),
        in_specs=[qspec] + kspecs + vspecs + [sspec, qspec, qspec],
        out_specs=[qspec, full, full, pl.BlockSpec((1, LANES), lambda b, i: (0, 0))],
        out_shape=[jax.ShapeDtypeStruct((bsz, lp, d), F32), jax.ShapeDtypeStruct((bsz, lp, KVW), F32),
                   jax.ShapeDtypeStruct((bsz, lp, KVW), F32), jax.ShapeDtypeStruct((1, LANES), F32)],
        compiler_params=_cparams(("arbitrary", "arbitrary")),
    )(q, k, k, k, kv, kv, kv, sinks, o, do)


def _concat_cols(name, a, b):
    rows = a.shape[0]

    def body(step, a_ref, b_ref, o_ref):
        o_ref[...] = jnp.concatenate([a_ref[...], b_ref[...]], axis=1)

    return _rowcall(name, body, rows, [a, b], [], [(a.shape[1] + b.shape[1], F32)])[0]


def _adamw(name, w, m, v, parts):
    rows, wd = w.shape
    n = parts.shape[0]
    tm = _stream_tile(rows, wd * (7 * 4 + n * parts.dtype.itemsize))

    def kern(w_ref, m_ref, v_ref, p_ref, g_ref, d_ref, m2_ref, v2_ref):
        g = p_ref[0].astype(F32)
        for k in range(1, n):
            g = g + p_ref[k].astype(F32)
        m2 = ADAM_B1 * m_ref[...] + (1.0 - ADAM_B1) * g
        v2 = ADAM_B2 * v_ref[...] + (1.0 - ADAM_B2) * (g * g)
        mh = m2 / (1.0 - ADAM_B1 ** ADAM_STEP)
        vh = v2 / (1.0 - ADAM_B2 ** ADAM_STEP)
        g_ref[...] = g
        d_ref[...] = -ADAM_LR * (mh / (jnp.sqrt(vh) + ADAM_EPS) + ADAM_WD * w_ref[...])
        m2_ref[...] = m2
        v2_ref[...] = v2

    spec = pl.BlockSpec((tm, wd), lambda i: (i, 0))
    sd = jax.ShapeDtypeStruct((rows, wd), F32)
    return pl.pallas_call(
        kern, name=name, grid=(rows // tm,),
        in_specs=[spec, spec, spec, pl.BlockSpec((n, tm, wd), lambda i: (0, i, 0))],
        out_specs=[spec] * 4, out_shape=[sd] * 4,
        compiler_params=_cparams(("arbitrary",)),
    )(w, m, v, parts)


def _pair_sum(name, parts, theirs, my_c):
    n, _, rows, wd = parts.shape
    tm = _stream_tile(rows, wd * (4 + 4 + 2))

    def kern(c_ref, a_ref, b_ref, o_ref):
        o_ref[...] = (a_ref[...] + b_ref[...]).astype(BF16)

    return pl.pallas_call(
        kern, name=name,
        grid_spec=pltpu.PrefetchScalarGridSpec(
            num_scalar_prefetch=1, grid=(n, rows // tm),
            in_specs=[pl.BlockSpec((None, None, tm, wd), lambda k, i, c: (k, c[0], i, 0)),
                      pl.BlockSpec((None, tm, wd), lambda k, i, c: (k, i, 0))],
            out_specs=pl.BlockSpec((None, tm, wd), lambda k, i, c: (k, i, 0))),
        out_shape=jax.ShapeDtypeStruct((n, rows, wd), BF16), compiler_params=_cparams(("arbitrary", "arbitrary")),
    )(my_c, parts, theirs)


MESH = pl.DeviceIdType.MESH
ANY = pl.BlockSpec(memory_space=pl.ANY)


def _place():
    x, y, c = lax.axis_index("x"), lax.axis_index("y"), lax.axis_index("c")
    return x, y, c, [(1 - x, y), (x, 1 - y), (1 - x, 1 - y)]


def _gather_rider(shards):
    n = len(shards)

    def copy(refs, a, k, block, to, own=False):
        x_refs, out_refs, (send_sems, recv_sems, _) = refs
        px, py, pc = block
        slot = out_refs[a].at[4 * px + 2 * py + pc]
        return pltpu.make_async_remote_copy(
            src_ref=x_refs[a] if own else slot, dst_ref=slot,
            send_sem=send_sems.at[a, k], recv_sem=recv_sems.at[a, k], device_id=to, device_id_type=MESH)

    def local(refs, a):
        x, y, c, _ = _place()
        return pltpu.make_async_copy(refs[0][a], refs[1][a].at[4 * x + 2 * y + c], refs[2][2].at[a])

    def first(refs):
        x, y, c, chips = _place()
        out = []
        for a in range(n):
            out.append(copy(refs, a, 0, (x, y, c), (x, y, 1 - c), own=True))
            out += [copy(refs, a, 1 + j, (x, y, c), (*chip, c), own=True) for j, chip in enumerate(chips)]
        return out

    def passed(refs):
        x, y, c, chips = _place()
        return [copy(refs, a, 4 + j, (*chip, c), (x, y, 1 - c)) for j, chip in enumerate(chips) for a in range(n)]

    def start(*refs):
        for a in range(n):
            local(refs, a).start()
        for cp in first(refs):
            cp.start()

    def mid(*refs):
        x, y, c, chips = _place()
        fwd = passed(refs)
        for j, chip in enumerate(chips):
            for a in range(n):
                copy(refs, a, 1 + j, (*chip, c), (x, y, c)).wait_recv()
                fwd[j * n + a].start()

    def finish(*refs):
        x, y, c, chips = _place()
        for a in range(n):
            copy(refs, a, 0, (x, y, 1 - c), (x, y, c)).wait_recv()
            for j, chip in enumerate(chips):
                copy(refs, a, 4 + j, (*chip, 1 - c), (x, y, c)).wait_recv()
        for cp in first(refs) + passed(refs):
            cp.wait_send()
        for a in range(n):
            local(refs, a).wait()

    return _Rider(list(shards), [jax.ShapeDtypeStruct((N_DEV,) + s.shape, s.dtype) for s in shards],
                  [pltpu.SemaphoreType.DMA((n, 7)), pltpu.SemaphoreType.DMA((n, 7)), pltpu.SemaphoreType.DMA((n,))],
                  start, mid, finish)


def _swap_rider(parts):
    n = len(parts)

    def copies(p_refs, out_refs, sems):
        x, y, c, _ = _place()
        return [pltpu.make_async_remote_copy(
            src_ref=p_refs[a].at[:, 1 - c], dst_ref=out_refs[a], send_sem=sems[0].at[a], recv_sem=sems[1].at[a],
            device_id=(x, y, 1 - c), device_id_type=MESH) for a in range(n)]

    def start(*refs):
        for cp in copies(*refs):
            cp.start()

    def finish(*refs):
        for cp in copies(*refs):
            cp.wait()

    return _Rider(list(parts), [jax.ShapeDtypeStruct((p.shape[0],) + p.shape[2:], p.dtype) for p in parts],
                  [pltpu.SemaphoreType.DMA((n,)), pltpu.SemaphoreType.DMA((n,))], start, None, finish)


def _scatter_rider(sums):
    n = len(sums)

    def copy(refs, a, j, block):
        s_refs, out_refs, (send_sems, recv_sems, _) = refs
        x, y, c, chips = _place()
        px, py = chips[j]
        return pltpu.make_async_remote_copy(
            src_ref=s_refs[a].at[2 * px + py], dst_ref=out_refs[a].at[block],
            send_sem=send_sems.at[a, j], recv_sem=recv_sems.at[a, j], device_id=(px, py, c), device_id_type=MESH)

    def local(refs, a):
        x, y, c, _ = _place()
        return pltpu.make_async_copy(refs[0][a].at[2 * x + y], refs[1][a].at[2 * x + y], refs[2][2].at[a])

    def sends(refs):
        x, y, c, _ = _place()
        return [copy(refs, a, j, 2 * x + y) for j in range(3) for a in range(n)]

    def start(*refs):
        for a in range(n):
            local(refs, a).start()
        for cp in sends(refs):
            cp.start()

    def finish(*refs):
        x, y, c, chips = _place()
        for j, (px, py) in enumerate(chips):
            for a in range(n):
                copy(refs, a, j, 2 * px + py).wait_recv()
        for cp in sends(refs):
            cp.wait_send()
        for a in range(n):
            local(refs, a).wait()

    return _Rider(list(sums), [jax.ShapeDtypeStruct(s.shape, s.dtype) for s in sums],
                  [pltpu.SemaphoreType.DMA((n, 3)), pltpu.SemaphoreType.DMA((n, 3)), pltpu.SemaphoreType.DMA((n,))],
                  start, None, finish)


BIG = (("ffn1_w_gate_up", 2), ("ffn1_w_down", 1), ("ffn2_w_gate_up", 2), ("ffn2_w_down", 1), ("ssm_w_in", 1),
       ("ssm_w_out", 2), ("w_kv", 0), ("attn_w_q", 1), ("attn_w_o", 1))
SMALL = ("ffn1_norm", "mix_norm", "ffn2_norm", "ssm_lambda_re", "ssm_lambda_im", "ssm_b_re", "ssm_b_im",
         "ssm_c_re", "ssm_c_im", "ssm_log_step", "kv_norm", "k_norm", "q_norm", "attn_sinks")
COLS = (("meta_tokens", 1), ("ssm_d", 1))
WEIGHTS = ("meta_tokens", "ffn1_norm", "ffn1_w_gate_up", "ffn1_w_down", "mix_norm", "ffn2_norm", "ffn2_w_gate_up",
           "ffn2_w_down", "ssm_w_in", "ssm_lambda_re", "ssm_lambda_im", "ssm_b_re", "ssm_b_im", "ssm_c_re",
           "ssm_c_im", "ssm_log_step", "ssm_d", "ssm_w_out", "kv_norm", "w_kv", "k_norm", "attn_w_q", "q_norm",
           "attn_sinks", "attn_w_o")


def _rows_of(a, width):
    n = math.prod(a.shape)
    if n % width == 0:
        r = a.reshape(n // width, width)
    else:
        assert n < width
        r = jnp.pad(a.reshape(1, n), ((0, 0), (0, width - n)))
    return jnp.pad(r, ((0, (-r.shape[0]) % 8), (0, 0)))


def _pack_small(arrs, width):
    return jnp.concatenate([_rows_of(a.astype(F32), width) for a in arrs], axis=0)


def _unpack_small(buf, shapes, width):
    out, off = [], 0
    for shp in shapes:
        n = math.prod(shp)
        r = max(n // width, 1)
        out.append(buf[off:off + r].reshape(shp) if n % width == 0 else buf[off, :n].reshape(shp))
        off += r + (-r) % 8
    return out


def _shape2d(shp):
    return (math.prod(shp[:-1]), shp[-1])


def _unshard(g, axis):
    g = jnp.moveaxis(g, 0, axis)
    shp = g.shape
    return g.reshape(shp[:axis] + (shp[axis] * shp[axis + 1],) + shp[axis + 2:])


def _shard(full, axis):
    shp = full.shape
    g = full.reshape(shp[:axis] + (N_DEV, shp[axis] // N_DEV) + shp[axis + 1:])
    return jnp.moveaxis(g, axis, 0)


def _blockdiag(blocks):
    g, r, c = blocks.shape
    eye = jnp.eye(g, dtype=blocks.dtype)
    return (eye[:, None, :, None] * blocks[:, :, None, :]).reshape(g * r, g * c)


def _diagblocks(full, g):
    r, c = full.shape[0] // g, full.shape[1] // g
    f = full.reshape(g, r, g, c)
    idx = jnp.arange(g)
    return f[idx, :, idx, :]


def kernel(x, meta_tokens, ffn1_norm, ffn1_w_gate_up, ffn1_w_down, mix_norm, ffn2_norm, ffn2_w_gate_up, ffn2_w_down, ssm_w_in, ssm_lambda_re, ssm_lambda_im, ssm_b_re, ssm_b_im, ssm_c_re, ssm_c_im, ssm_log_step, ssm_d, ssm_w_out, kv_norm, w_kv, k_norm, attn_w_q, q_norm, attn_sinks, attn_w_o, loss_target, m_meta_tokens, m_ffn1_norm, m_ffn1_w_gate_up, m_ffn1_w_down, m_mix_norm, m_ffn2_norm, m_ffn2_w_gate_up, m_ffn2_w_down, m_ssm_w_in, m_ssm_lambda_re, m_ssm_lambda_im, m_ssm_b_re, m_ssm_b_im, m_ssm_c_re, m_ssm_c_im, m_ssm_log_step, m_ssm_d, m_ssm_w_out, m_kv_norm, m_w_kv, m_k_norm, m_attn_w_q, m_q_norm, m_attn_sinks, m_attn_w_o, v_meta_tokens, v_ffn1_norm, v_ffn1_w_gate_up, v_ffn1_w_down, v_mix_norm, v_ffn2_norm, v_ffn2_w_gate_up, v_ffn2_w_down, v_ssm_w_in, v_ssm_lambda_re, v_ssm_lambda_im, v_ssm_b_re, v_ssm_b_im, v_ssm_c_re, v_ssm_c_im, v_ssm_log_step, v_ssm_d, v_ssm_w_out, v_kv_norm, v_w_kv, v_k_norm, v_attn_w_q, v_q_norm, v_attn_sinks, v_attn_w_o):
    args = dict(locals())
    W = {n: args[n] for n in WEIGHTS}
    M = {n: args["m_" + n] for n in WEIGHTS}
    V = {n: args["v_" + n] for n in WEIGHTS}
    my_x, my_y, my_c = (lax.axis_index(a) for a in MESH_AXES)
    my_dev = 4 * my_x + 2 * my_y + my_c

    big_names = [n for n, _ in BIG]
    s2d = {n: _shape2d(W[n].shape) for n in big_names}
    col_w = W["meta_tokens"].shape[1]

    grads, summed, small_parts = _local_step(x, loss_target, W, my_c.astype(jnp.int32).reshape(1))
    loss = lax.psum(grads.pop("loss"), MESH_AXES)
    grad_x = grads.pop("x")

    outs = [{}, {}, {}, {}]
    for n in big_names:
        r4 = _adamw("adamw_" + n, W[n].reshape(s2d[n]), M[n].reshape(s2d[n]), V[n].reshape(s2d[n]), summed[n])
        for k in range(4):
            outs[k][n] = r4[k].reshape(W[n].shape)

    small_names = list(SMALL) + [n for n, _ in COLS]
    small_shapes = [grads[n].shape for n in small_names]
    zero_cols = [jnp.zeros(grads[n].shape, F32) for n, _ in COLS]
    packs = lambda d: _pack_small([d[n] for n in SMALL] + zero_cols, PACK_W)
    r4 = _adamw("adamw_small", packs(W), packs(M), packs(V), small_parts)
    gsmall = None
    for k in range(4):
        un = dict(zip(small_names, _unpack_small(r4[k], small_shapes, PACK_W)))
        gsmall = un if k == 0 else gsmall
        outs[k].update({n: un[n] for n in SMALL})
    col_g = [lax.dynamic_slice_in_dim(gsmall[n], my_dev * W[n].shape[1], W[n].shape[1], axis=1) for n, _ in COLS]
    packc = lambda d: _pack_small([d[n] for n, _ in COLS], col_w)
    r4 = _adamw("adamw_cols", packc(W), packc(M), packc(V), _pack_small(col_g, col_w)[None])
    col_shapes = [W[n].shape for n, _ in COLS]
    for k in range(4):
        outs[k].update(dict(zip([n for n, _ in COLS], _unpack_small(r4[k], col_shapes, col_w))))

    res = [[outs[k][n] for n in WEIGHTS] for k in range(4)]
    return (loss, grad_x, *res[0], *res[1], *res[2], *res[3])


def _local_step(x, target, P, c_arr):
    bsz, seq, d = x.shape
    lp = seq + PAD
    rows = bsz * lp
    depth = P["ffn1_norm"].shape[0]
    assert depth == 2
    bf = lambda a: a.astype(BF16)
    row = lambda a: a.reshape(1, -1)

    def shard(n, l=None):
        a = P[n] if l is None else P[n][l]
        return bf(a.reshape(_shape2d(a.shape)))

    rowsharded = lambda g: g.reshape((g.shape[0] * g.shape[1],) + g.shape[2:])
    colsharded = lambda g: _unshard(g, 1)
    col_w = P["meta_tokens"].shape[1]
    g0 = _run_rider("gather_first", _gather_rider(
        [shard("ffn1_w_gate_up", 0), shard("ffn1_w_down", 0), shard("ssm_w_in", 0),
         _pack_small([P["meta_tokens"], P["ssm_d"]], col_w)]))
    ffn_w = {("ffn1", 0): (colsharded(g0[0]), rowsharded(g0[1]))}
    w_in = rowsharded(g0[2])
    meta_full = _unshard(g0[3][:, :N_META], 1)
    dvec = _unshard(g0[3][:, N_META:N_META + 1, :P["ssm_d"].shape[1]], 1)

    pos = (jnp.arange(lp, dtype=F32) - float(META0))[:, None]
    half = HEAD_DIM // 2
    freqs = ROPE_THETA ** (-jnp.arange(0, half, dtype=F32) * 2.0 / HEAD_DIM)
    ang = pos * freqs[None, :]
    cos_t = jnp.tile(jnp.cos(ang), (1, LANES // half))
    sin_t = jnp.tile(jnp.concatenate([-jnp.sin(ang), jnp.sin(ang)], axis=1), (1, LANES // HEAD_DIM))
    gi = jnp.arange(LANES) // HEAD_DIM
    gmat = jnp.where(gi[:, None] == gi[None, :], 1.0 / HEAD_DIM, 0.0).astype(BF16)

    g_n, c_n, p_n = P["ssm_lambda_re"].shape[1], SSM_GROUP, SSM_STATE
    ns = g_n * p_n
    lr = P["ssm_lambda_re"][0].reshape(g_n, 1, p_n)
    li = P["ssm_lambda_im"][0].reshape(g_n, 1, p_n)
    ls = P["ssm_log_step"][0].reshape(g_n, 1, 1)
    brt = P["ssm_b_re"][0].transpose(0, 2, 1)
    bit = P["ssm_b_im"][0].transpose(0, 2, 1)
    ar, ai, bbr, bbi = _s5_params_fwd(lr, li, ls, brt, bit)
    a2 = jnp.concatenate([ar.reshape(1, ns), ai.reshape(1, ns)], axis=0)
    bfull = jnp.concatenate([_blockdiag(bbr), _blockdiag(bbi)], axis=1)
    cre_t = P["ssm_c_re"][0].transpose(0, 2, 1)
    cim_t = P["ssm_c_im"][0].transpose(0, 2, 1)
    cfull = jnp.concatenate([_blockdiag(cre_t), -_blockdiag(cim_t)], axis=0)

    ffn = lambda which, l: (row(P[which + "_norm"][l]),) + ffn_w[which, l]
    mix0, mix1, kvn = row(P["mix_norm"][0]), row(P["mix_norm"][1]), row(P["kv_norm"])
    kgain = jnp.tile(P["k_norm"].reshape(1, HEAD_DIM), (1, KVW // HEAD_DIM))
    qgain = jnp.tile(P["q_norm"].reshape(1, HEAD_DIM), (1, d // HEAD_DIM))
    sinks = P["attn_sinks"].reshape(1, -1)

    h0 = _embed(x, meta_full).reshape(rows, d)
    h1, ab_f1_0, g_wout, g_gu, g_d, g_kv = _ffn_fwd("ffn1_0_fwd", h0, *ffn("ffn1", 0), rider=_gather_rider(
        [shard("ssm_w_out", 0), shard("ffn2_w_gate_up", 0), shard("ffn2_w_down", 0), shard("w_kv")]))
    w_out, w_kv = colsharded(g_wout), rowsharded(g_kv)
    ffn_w["ffn2", 0] = (colsharded(g_gu), rowsharded(g_d))
    u = _proj_fwd("ssm_in_fwd", h1, mix0, w_in)
    y, xs, u_perm = _s5_scan_fwd(u, bf(bfull), bf(cfull), a2, dvec, bsz)
    h2 = _glu_fwd(y, h1, w_out)
    h3, ab_f2_0, g_gu, g_d, g_q, g_o = _ffn_fwd("ffn2_0_fwd", h2, *ffn("ffn2", 0), rider=_gather_rider(
        [shard("ffn1_w_gate_up", 1), shard("ffn1_w_down", 1), shard("attn_w_q", 0), shard("attn_w_o", 0)]))
    w_q, w_o = rowsharded(g_q), rowsharded(g_o)
    ffn_w["ffn1", 1] = (colsharded(g_gu), rowsharded(g_d))
    kv = _proj_fwd("kv_fwd", h3, kvn, w_kv)
    k = _headrope_fwd("k_rope_fwd", kv, KVW, kgain, cos_t, sin_t, gmat, lp)
    h4, ab_f1_1, g_gu, g_d = _ffn_fwd("ffn1_1_fwd", h3, *ffn("ffn1", 1), rider=_gather_rider(
        [shard("ffn2_w_gate_up", 1), shard("ffn2_w_down", 1)]))
    ffn_w["ffn2", 1] = (colsharded(g_gu), rowsharded(g_d))
    q_raw = _proj_fwd("q_fwd", h4, mix1, w_q)
    q = _headrope_fwd("q_rope_fwd", q_raw, d, qgain, cos_t, sin_t, gmat, lp)
    r3 = lambda a: a.reshape(bsz, lp, a.shape[-1])
    o = _attn_fwd(r3(q), r3(k), r3(kv), sinks).reshape(rows, d)
    h5 = _lin_res_fwd("attn_out_fwd", o, w_o, h4)
    h6, ab_f2_1 = _ffn_fwd("ffn2_1_fwd", h5, *ffn("ffn2", 1))
    loss, dh6 = _loss(r3(h6), target)
    dh6 = dh6.reshape(rows, d)

    G = {"loss": loss[0, 0]}

    def ffn_back(name, which, l, h, ab, dout, rider=None):
        g, wgu, wd = ffn(which, l)
        dh, hn, dab, act, dg, *rode = _ffn_bwd(name, h, ab, dout, g, wgu, wd, rider=rider)
        parts = [_shard(_mm_tn(name + "_wgu", hn, dab), 1), _mm_tn_slots(name + "_wd", act, dout, 0.5)]
        return dh, dg, parts, rode

    slots = lambda g: g.reshape((N_DEV, g.shape[0] // N_DEV) + g.shape[1:])
    swap_of = lambda parts: _swap_rider([p.reshape((4, 2) + p.shape[1:]) for p in parts])

    def pair_sums(tag, parts, theirs):
        return [_pair_sum("pair_sum_%s_%d" % (tag, k), p.reshape((4, 2) + p.shape[1:]), t, c_arr)
                for k, (p, t) in enumerate(zip(parts, theirs))]

    dh5, dg_f2_1, parts_a, _ = ffn_back("ffn2_1_bwd", "ffn2", 1, h5, ab_f2_1, dh6)
    do, dw_o, *theirs = _lin_bwd("attn_out_bwd", o, w_o, dh5, rider=swap_of(parts_a))
    sums_a = pair_sums("ffn2_1", parts_a, theirs)
    dq, dk, dv, dsinks = _attn_bwd(r3(q), r3(k), r3(kv), sinks, r3(o), r3(do))
    dq_raw, dqg = _headrope_bwd("q_rope_bwd", q_raw, d, dq.reshape(rows, d), qgain, cos_t, sin_t, gmat, lp)
    dh4, dg_mix1, dw_q = _proj_bwd("q_bwd", h4, mix1, w_q, dq_raw, dh5)
    dh3, dg_f1_1, parts_b, red_a = ffn_back("ffn1_1_bwd", "ffn1", 1, h3, ab_f1_1, dh4, rider=_scatter_rider(sums_a))
    dk_raw, dkg = _headrope_bwd("k_rope_bwd", kv, KVW, dk.reshape(rows, KVW), kgain, cos_t, sin_t, gmat, lp)
    parts_b = parts_b + [slots(dw_q), slots(dw_o)]
    dkv = _concat_cols("dkv_concat", dk_raw, dv.reshape(rows, KVW))
    dh3, dg_kvn, dw_kv, *theirs = _proj_bwd("kv_bwd", h3, kvn, w_kv, dkv, dh3, rider=swap_of(parts_b))
    sums_b = pair_sums("ffn1_1", parts_b, theirs)
    dh2, dg_f2_0, parts_c, red_b = ffn_back("ffn2_0_bwd", "ffn2", 0, h2, ab_f2_0, dh3, rider=_scatter_rider(sums_b))
    parts_c = parts_c + [slots(dw_kv)]
    dy, dw_out, *theirs = _glu_bwd(y, dh2, w_out, rider=swap_of(parts_c))
    sums_c = pair_sums("ffn2_0", parts_c, theirs)
    ctfull = jnp.concatenate([_blockdiag(P["ssm_c_re"][0]), -_blockdiag(P["ssm_c_im"][0])], axis=1)
    btfull = jnp.concatenate([_blockdiag(bbr.transpose(0, 2, 1)), _blockdiag(bbi.transpose(0, 2, 1))], axis=0)
    du, gx, dy_perm, da, dd = _s5_scan_bwd(dy, u, xs, bf(ctfull), bf(btfull), a2, dvec, bsz)
    dbfull = _mm_tn_blockdiag("ssm_db", u_perm, gx, False)
    dcfull = _mm_tn_blockdiag("ssm_dc", xs, dy_perm, True)
    dh1, dg_mix0, dw_in = _proj_bwd("ssm_in_bwd", h1, mix0, w_in, du, dh2)
    dh0, dg_f1_0, parts_d, red_c = ffn_back("ffn1_0_bwd", "ffn1", 0, h0, ab_f1_0, dh1, rider=_scatter_rider(sums_c))
    dbbr = _diagblocks(dbfull[:, :ns], g_n)
    dbbi = _diagblocks(dbfull[:, ns:], g_n)
    dlr, dli, dls, dbrt, dbit = _s5_params_bwd(lr, li, ls, brt, bit, da[:, :ns].reshape(g_n, 1, p_n),
                                               da[:, ns:].reshape(g_n, 1, p_n), dbbr, dbbi)
    dh0 = r3(dh0)
    G["x"] = dh0[:, PAD:, :]
    G["meta_tokens"] = _meta_sum(dh0)
    G["ffn1_norm"] = jnp.concatenate([dg_f1_0, dg_f1_1], axis=0)
    G["ffn2_norm"] = jnp.concatenate([dg_f2_0, dg_f2_1], axis=0)
    G["mix_norm"] = jnp.concatenate([dg_mix0, dg_mix1], axis=0)
    G["ssm_lambda_re"] = dlr.reshape(1, g_n, p_n)
    G["ssm_lambda_im"] = dli.reshape(1, g_n, p_n)
    G["ssm_log_step"] = dls.reshape(1, g_n)
    G["ssm_b_re"] = dbrt.transpose(0, 2, 1)[None]
    G["ssm_b_im"] = dbit.transpose(0, 2, 1)[None]
    G["ssm_c_re"] = _diagblocks(dcfull[:ns], g_n).transpose(0, 2, 1)[None]
    G["ssm_c_im"] = -_diagblocks(dcfull[ns:], g_n).transpose(0, 2, 1)[None]
    G["ssm_d"] = dd
    G["kv_norm"] = dg_kvn.reshape(-1)
    G["k_norm"] = dkg[0, :HEAD_DIM]
    G["q_norm"] = dqg[:, :HEAD_DIM]
    G["attn_sinks"] = dsinks[:, :N_KV_HEADS * Q_PER_KV]

    parts_d = parts_d + [slots(dw_in), dw_out]
    small_pack = _pack_small([G[n] for n in list(SMALL) + [n for n, _ in COLS]], PACK_W)
    *theirs, small_parts = _run_rider("grad_swap_last", _join_riders(swap_of(parts_d), _gather_rider([small_pack])))
    red_d = _run_rider("grad_scatter_last", _scatter_rider(pair_sums("last", parts_d, theirs)))
    both = lambda lo, hi: jnp.concatenate([lo, hi], axis=1)
    summed = {"ffn1_w_gate_up": both(red_d[0], red_b[0]), "ffn1_w_down": both(red_d[1], red_b[1]),
              "ffn2_w_gate_up": both(red_c[0], red_a[0]), "ffn2_w_down": both(red_c[1], red_a[1]),
              "ssm_w_in": red_d[2], "ssm_w_out": red_d[3], "w_kv": red_c[2], "attn_w_q": red_b[2],
              "attn_w_o": red_b[3]}
    return G, summed, small_parts
```

```python
import functools
import math

import jax
import jax.numpy as jnp
from jax import lax
from jax.experimental import pallas as pl
from jax.experimental.pallas import tpu as pltpu

F32 = jnp.float32
BF16 = jnp.bfloat16

N_META = 16
PAD = 128
META0 = PAD - N_META
HEAD_DIM = 64
N_KV_HEADS = 4
Q_PER_KV = 4
SSM_GROUP = 16
SSM_STATE = 64
EPS = 1e-6
NEG_INF = -1e30
ROPE_THETA = 10000.0
ADAM_LR, ADAM_B1, ADAM_B2, ADAM_EPS, ADAM_WD, ADAM_STEP = 0.001, 0.9, 0.999, 1e-08, 0.01, 10
LANES = 128
PACK_W = 1024
VMEM_LIMIT = 56 * 1024 * 1024
MESH_AXES = ("x", "y", "c")
N_DEV = 8


def _cparams(sem=None):
    return pltpu.CompilerParams(dimension_semantics=sem, vmem_limit_bytes=VMEM_LIMIT)


def _row_tile(rows, light=False):
    for tm in ((768,) if light else ()) + (384, 256, 128, 64, 32, 16, 8):
        if rows % tm == 0:
            return tm
    raise ValueError(rows)


STREAM_BUDGET = 32 * 1024 * 1024


def _stream_tile(rows, bytes_per_row):
    for tm in range(rows, 0, -1):
        if rows % tm == 0 and (tm % 16 == 0 or tm == rows) and 2 * tm * bytes_per_row <= STREAM_BUDGET:
            return tm
    raise ValueError(rows)


TN_BUDGET = 52 * 1024 * 1024
TN_MAX_ROWS = 2816


def _tn_tile(rows, a, b, k1, tn):
    sa, sb = a.dtype.itemsize, b.dtype.itemsize
    fits = lambda tm: 2 * tm * (k1 * sa + tn * sb) + 3 * k1 * tn * 4 + tm * k1 * 2 <= TN_BUDGET
    divisors = [tm for tm in range(min(rows, TN_MAX_ROWS), 7, -8) if rows % tm == 0 and fits(tm)]
    good = [tm for tm in divisors if -(-tm // MXU_DIM) * MXU_DIM <= 1.1 * tm]
    if good or divisors:
        return (good or divisors)[0]
    raise ValueError(rows)


def _dot(a, b):
    return jnp.dot(a.astype(BF16), b.astype(BF16), preferred_element_type=F32)


def _dot_nt(a, b):
    return lax.dot_general(a.astype(BF16), b.astype(BF16), (((1,), (1,)), ((), ())), preferred_element_type=F32)


def _dot_tn(a, b):
    return lax.dot_general(a.astype(BF16), b.astype(BF16), (((0,), (0,)), ((), ())), preferred_element_type=F32)


def _rms(x, g):
    rstd = lax.rsqrt(jnp.mean(x * x, axis=-1, keepdims=True) + EPS)
    y = x * rstd
    return y * g, y, rstd


def _rms_bwd(dhn, y, rstd, g):
    dyn = dhn * g
    dx = rstd * (dyn - y * jnp.mean(dyn * y, axis=-1, keepdims=True))
    return dx, jnp.sum(dhn * y, axis=0, keepdims=True)


def _sigmoid(x):
    return 1.0 / (1.0 + jnp.exp(-x))


_GELU_C = math.sqrt(2.0 / math.pi)


def _gelu(y):
    t = jnp.tanh(_GELU_C * (y + 0.044715 * y * y * y))
    return 0.5 * y * (1.0 + t), t


def _gelu_grad(y, t):
    return 0.5 * (1.0 + t) + 0.5 * y * (1.0 - t * t) * _GELU_C * (1.0 + 3.0 * 0.044715 * y * y)


class _Rider:
    def __init__(self, ins, outs, sems, start, mid, finish):
        self.ins, self.outs, self.sems, self.start, self.mid, self.finish = ins, outs, sems, start, mid, finish


def _join_riders(r1, r2):
    ni, no, ns = len(r1.ins), len(r1.outs), len(r1.sems)

    def both(f1, f2):
        def phase(ins, outs, sems):
            if f1 is not None:
                f1(ins[:ni], outs[:no], sems[:ns])
            if f2 is not None:
                f2(ins[ni:], outs[no:], sems[ns:])
        return phase

    mid = both(r1.mid, r2.mid) if (r1.mid is not None or r2.mid is not None) else None
    return _Rider(r1.ins + r2.ins, r1.outs + r2.outs, r1.sems + r2.sems,
                  both(r1.start, r2.start), mid, both(r1.finish, r2.finish))


def _run_rider(name, rider):
    def kern(*refs):
        ni, no = len(rider.ins), len(rider.outs)
        parts = refs[:ni], refs[ni:ni + no], refs[ni + no:]
        rider.start(*parts)
        if rider.mid is not None:
            rider.mid(*parts)
        rider.finish(*parts)

    return pl.pallas_call(
        kern, name=name, out_shape=list(rider.outs), in_specs=[ANY] * len(rider.ins),
        out_specs=[ANY] * len(rider.outs), scratch_shapes=list(rider.sems),
    )(*rider.ins)


def _rowcall(name, body, rows, row_ins, const_ins, row_outs, acc_outs=(), tm=None, row_in_maps=None, rider=None,
             light=False):
    tm = tm or _row_tile(rows, light)
    steps = rows // tm
    in_specs = []
    for k, a in enumerate(row_ins):
        if row_in_maps is not None and row_in_maps[k] is not None:
            in_specs.append(pl.BlockSpec(*row_in_maps[k]))
        else:
            in_specs.append(pl.BlockSpec((tm, a.shape[1]), lambda i: (i, 0)))
    for a in const_ins:
        in_specs.append(pl.BlockSpec(a.shape, lambda i, nd=a.ndim: (0,) * nd, pipeline_mode=pl.Buffered(1)))
    out_shape, out_specs = [], []
    for w, dt in row_outs:
        out_shape.append(jax.ShapeDtypeStruct((rows, w), dt))
        out_specs.append(pl.BlockSpec((tm, w), lambda i: (i, 0)))
    for shp, dt in acc_outs:
        out_shape.append(jax.ShapeDtypeStruct(shp, dt))
        out_specs.append(pl.BlockSpec(shp, lambda i, nd=len(shp): (0,) * nd))

    if rider is None:
        def kern(*refs):
            body(pl.program_id(0), *refs)

        return pl.pallas_call(
            kern, name=name, grid=(steps,), in_specs=in_specs, out_specs=out_specs, out_shape=out_shape,
            compiler_params=_cparams(("arbitrary",)),
        )(*row_ins, *const_ins)

    n_in, n_out = len(in_specs), len(out_specs)
    r_in, r_out = len(rider.ins), len(rider.outs)

    def kern_r(*refs):
        step = pl.program_id(0)
        ins, rins = refs[:n_in], refs[n_in:n_in + r_in]
        outs = refs[n_in + r_in:n_in + r_in + n_out]
        routs = refs[n_in + r_in + n_out:n_in + r_in + n_out + r_out]
        sems = refs[n_in + r_in + n_out + r_out:]

        @pl.when(step == 0)
        def _():
            rider.start(rins, routs, sems)

        if rider.mid is not None:
            @pl.when(step == (3 * steps) // 4)
            def _():
                rider.mid(rins, routs, sems)

        body(step, *ins, *outs)

        @pl.when(step == steps - 1)
        def _():
            rider.finish(rins, routs, sems)

    return pl.pallas_call(
        kern_r, name=name, grid=(steps,), in_specs=in_specs + [ANY] * r_in, out_specs=out_specs + [ANY] * r_out,
        out_shape=out_shape + list(rider.outs), scratch_shapes=list(rider.sems),
        compiler_params=_cparams(("arbitrary",)),
    )(*row_ins, *const_ins, *rider.ins)


def _acc(step, ref, val):
    @pl.when(step == 0)
    def _():
        ref[...] = val

    @pl.when(step != 0)
    def _():
        ref[...] += val


def _embed(x, meta):
    bsz, seq, d = x.shape
    nb = seq // PAD + 1

    def kern(x_ref, m_ref, o_ref):
        i = pl.program_id(1)

        @pl.when(i == 0)
        def _():
            o_ref[0, 0:META0, :] = jnp.zeros((META0, d), F32)
            o_ref[0, META0:PAD, :] = m_ref[...]

        @pl.when(i != 0)
        def _():
            o_ref[0] = x_ref[0]

    return pl.pallas_call(
        kern, name="embed", grid=(bsz, nb),
        in_specs=[pl.BlockSpec((1, PAD, d), lambda b, i: (b, jnp.maximum(i - 1, 0), 0)),
                  pl.BlockSpec((N_META, d), lambda b, i: (0, 0))],
        out_specs=pl.BlockSpec((1, PAD, d), lambda b, i: (b, i, 0)),
        out_shape=jax.ShapeDtypeStruct((bsz, seq + PAD, d), F32),
        compiler_params=_cparams(("arbitrary", "arbitrary")),
    )(x, meta)


def _loss(h6, target):
    bsz, lp, d = h6.shape
    nb = lp // PAD

    def kern(h_ref, t_ref, l_ref, d_ref):
        b, i = pl.program_id(0), pl.program_id(1)

        @pl.when((b == 0) & (i == 0))
        def _():
            l_ref[...] = jnp.zeros_like(l_ref)

        @pl.when(i == 0)
        def _():
            d_ref[0] = jnp.zeros((PAD, d), F32)

        @pl.when(i != 0)
        def _():
            e = h_ref[0] - t_ref[0]
            d_ref[0] = e * (1.0 / d)
            l_ref[...] += 0.5 * jnp.sum(jnp.mean(e * e, axis=-1, keepdims=True))

    return pl.pallas_call(
        kern, name="loss", grid=(bsz, nb),
        in_specs=[pl.BlockSpec((1, PAD, d), lambda b, i: (b, i, 0)),
                  pl.BlockSpec((1, PAD, d), lambda b, i: (b, jnp.maximum(i - 1, 0), 0))],
        out_specs=[pl.BlockSpec((1, LANES), lambda b, i: (0, 0)),
                   pl.BlockSpec((1, PAD, d), lambda b, i: (b, i, 0))],
        out_shape=[jax.ShapeDtypeStruct((1, LANES), F32), jax.ShapeDtypeStruct((bsz, lp, d), F32)],
        compiler_params=_cparams(("arbitrary", "arbitrary")),
    )(h6, target)


def _meta_sum(dh0):
    bsz, lp, d = dh0.shape

    def kern(d_ref, o_ref):
        _acc(pl.program_id(0), o_ref, d_ref[0, META0:PAD, :])

    return pl.pallas_call(
        kern, name="meta_sum", grid=(bsz,),
        in_specs=[pl.BlockSpec((1, PAD, d), lambda b: (b, 0, 0))],
        out_specs=pl.BlockSpec((N_META, d), lambda b: (0, 0)),
        out_shape=jax.ShapeDtypeStruct((N_META, d), F32),
        compiler_params=_cparams(("arbitrary",)),
    )(dh0)


MXU_DIM = 256


def _ffn_chunks(f):
    unit = MXU_DIM if f % MXU_DIM == 0 else LANES
    assert f % unit == 0
    first = (f // unit + 1) // 2 * unit
    return [(0, first), (first, f)] if first < f else [(0, f)]


def _ffn_fwd(name, h, g, wgu, wd, rider=None):
    rows, d = h.shape
    f = wd.shape[0]
    chunks = _ffn_chunks(f)

    def body(step, h_ref, g_ref, wgu_ref, wd_ref, o_ref, ab_ref):
        hx = h_ref[...]
        hb = _rms(hx, g_ref[...])[0].astype(BF16)
        acc = jnp.zeros(hx.shape, F32)
        for lo, hi in chunks:
            ga, ua = slice(lo, hi), slice(f + lo, f + hi)
            a = _dot(hb, wgu_ref[:, ga])
            b = _dot(hb, wgu_ref[:, ua])
            ab_ref[:, ga] = a.astype(BF16)
            ab_ref[:, ua] = b.astype(BF16)
            acc = acc + _dot(a * _sigmoid(a) * b, wd_ref[ga, :])
        o_ref[...] = hx + 0.5 * acc

    return _rowcall(name, body, rows, [h], [g, wgu, wd], [(d, F32), (2 * f, BF16)], rider=rider)


def _ffn_bwd(name, h, ab, dout, g, wgu, wd, rider=None):
    rows, d = h.shape
    f = wd.shape[0]
    chunks = _ffn_chunks(f)

    def body(step, h_ref, ab_ref, do_ref, g_ref, wgu_ref, wd_ref, dh_ref, hn_ref, dab_ref, act_ref, dg_ref):
        hx, dout_x, gx = h_ref[...], do_ref[...], g_ref[...]
        hn, y, rstd = _rms(hx, gx)
        hn_ref[...] = hn.astype(BF16)
        dhalf = (0.5 * dout_x).astype(BF16)
        dhn = jnp.zeros(hx.shape, F32)
        for lo, hi in chunks:
            ga, ua = slice(lo, hi), slice(f + lo, f + hi)
            a = ab_ref[:, ga].astype(F32)
            b = ab_ref[:, ua].astype(F32)
            s = _sigmoid(a)
            silu = a * s
            act_ref[:, ga] = (silu * b).astype(BF16)
            dact = _dot_nt(dhalf, wd_ref[ga, :])
            da = (dact * b * (s + silu * (1.0 - s))).astype(BF16)
            db = (dact * silu).astype(BF16)
            dab_ref[:, ga] = da
            dab_ref[:, ua] = db
            dhn = dhn + _dot_nt(da, wgu_ref[:, ga]) + _dot_nt(db, wgu_ref[:, ua])
        dx, dg = _rms_bwd(dhn, y, rstd, gx)
        dh_ref[...] = dout_x + dx
        _acc(step, dg_ref, dg)

    return _rowcall(name, body, rows, [h, ab, dout], [g, wgu, wd],
                    [(d, F32), (d, BF16), (2 * f, BF16), (f, BF16)], [((1, d), F32)], rider=rider)


def _mm_tn(name, a, b, scale=1.0):
    rows, k1 = a.shape
    k2 = b.shape[1]
    tn = k2
    for cand in (512, 704, 1408, 1024):
        if k2 % cand == 0 and k1 * cand * 4 <= 6 * 1024 * 1024:
            tn = cand
    tm = _tn_tile(rows, a, b, k1, tn)
    steps = rows // tm

    def kern(a_ref, b_ref, o_ref):
        bx = b_ref[...]
        if scale != 1.0:
            bx = bx * scale
        _acc(pl.program_id(1), o_ref, _dot_tn(a_ref[...], bx))

    return pl.pallas_call(
        kern, name=name, grid=(k2 // tn, steps),
        in_specs=[pl.BlockSpec((tm, k1), lambda j, i: (i, 0)), pl.BlockSpec((tm, tn), lambda j, i: (i, j))],
        out_specs=pl.BlockSpec((k1, tn), lambda j, i: (0, j)),
        out_shape=jax.ShapeDtypeStruct((k1, k2), F32),
        compiler_params=_cparams(("arbitrary", "arbitrary")),
    )(a, b)


def _mm_tn_blockdiag(name, a, b, states_first):
    rows = a.shape[0]
    ka, kb = a.shape[1], b.shape[1]
    qa, qb = (ka // 4, kb // 2) if states_first else (ka // 2, kb // 4)
    tm = _tn_tile(rows, a, b, qa, qb)
    steps = rows // tm
    wide = lambda part, k: 2 * part + k
    amap = (lambda p, k, i: (i, wide(p, k))) if states_first else (lambda p, k, i: (i, k))
    bmap = (lambda p, k, i: (i, k)) if states_first else (lambda p, k, i: (i, wide(p, k)))
    omap = (lambda p, k, i: (wide(p, k), k)) if states_first else (lambda p, k, i: (k, wide(p, k)))

    def kern(a_ref, b_ref, o_ref):
        _acc(pl.program_id(2), o_ref, _dot_tn(a_ref[...], b_ref[...]))

    return pl.pallas_call(
        kern, name=name, grid=(2, 2, steps),
        in_specs=[pl.BlockSpec((tm, qa), amap), pl.BlockSpec((tm, qb), bmap)],
        out_specs=pl.BlockSpec((qa, qb), omap), out_shape=jax.ShapeDtypeStruct((ka, kb), F32),
        compiler_params=_cparams(("arbitrary", "arbitrary", "arbitrary")),
    )(a, b)


def _mm_tn_slots(name, a, b, scale):
    rows, k1 = a.shape
    k2 = b.shape[1]
    tn = 512 if k2 % 512 == 0 else k2
    sr = k1 // N_DEV
    tm = _tn_tile(rows, a, b, k1, tn)
    steps = rows // tm

    def kern(a_ref, b_ref, o_ref):
        bx = b_ref[...]
        if scale != 1.0:
            bx = bx * scale
        res = _dot_tn(a_ref[...], bx)
        step = pl.program_id(1)
        for s in range(N_DEV):
            _acc(step, o_ref.at[s], res[s * sr:(s + 1) * sr])

    return pl.pallas_call(
        kern, name=name, grid=(k2 // tn, steps),
        in_specs=[pl.BlockSpec((tm, k1), lambda j, i: (i, 0)), pl.BlockSpec((tm, tn), lambda j, i: (i, j))],
        out_specs=pl.BlockSpec((N_DEV, sr, tn), lambda j, i: (0, 0, j)),
        out_shape=jax.ShapeDtypeStruct((N_DEV, sr, k2), F32),
        compiler_params=_cparams(("arbitrary", "arbitrary")),
    )(a, b)


def _proj_fwd(name, h, g, w):
    rows = h.shape[0]

    def body(step, h_ref, g_ref, w_ref, o_ref):
        o_ref[...] = _dot(_rms(h_ref[...], g_ref[...])[0], w_ref[...])

    return _rowcall(name, body, rows, [h], [g, w], [(w.shape[1], F32)], light=True)[0]


def _proj_bwd(name, h, g, w, dy, dres, rider=None):
    rows, d = h.shape

    def body(step, h_ref, dy_ref, dr_ref, g_ref, w_ref, dh_ref, dg_ref, dw_ref):
        gx = g_ref[...]
        hn, y, rstd = _rms(h_ref[...], gx)
        dyx = dy_ref[...]
        dx, dg = _rms_bwd(_dot_nt(dyx, w_ref[...]), y, rstd, gx)
        dh_ref[...] = dr_ref[...] + dx
        _acc(step, dg_ref, dg)
        _acc(step, dw_ref, _dot_tn(hn, dyx))

    return _rowcall(name, body, rows, [h, dy, dres], [g, w], [(d, F32)], [((1, d), F32), (w.shape, F32)],
                    rider=rider, light=True)


def _lin_res_fwd(name, a, w, res):
    rows = a.shape[0]

    def body(step, a_ref, r_ref, w_ref, o_ref):
        o_ref[...] = r_ref[...] + _dot(a_ref[...], w_ref[...])

    return _rowcall(name, body, rows, [a, res], [w], [(w.shape[1], F32)], light=True)[0]


def _lin_bwd(name, a, w, dy, rider=None):
    rows, k = a.shape

    def body(step, a_ref, dy_ref, w_ref, da_ref, dw_ref):
        dyx = dy_ref[...]
        da_ref[...] = _dot_nt(dyx, w_ref[...])
        _acc(step, dw_ref, _dot_tn(a_ref[...], dyx))

    return _rowcall(name, body, rows, [a, dy], [w], [(k, F32)], [(w.shape, F32)], rider=rider, light=True)


def _s5_param_fn(lr, li, ls, brt, bit):
    step = jnp.exp(ls)
    mag = jnp.exp(lr * step)
    ar = mag * jnp.cos(li * step)
    ai = mag * jnp.sin(li * step)
    den = lr * lr + li * li
    nr, ni = ar - 1.0, ai
    cr = (nr * lr + ni * li) / den
    ci = (ni * lr - nr * li) / den
    return ar, ai, cr * brt - ci * bit, cr * bit + ci * brt


def _s5_params_fwd(lr, li, ls, brt, bit):
    def kern(lr_ref, li_ref, ls_ref, br_ref, bi_ref, ar_ref, ai_ref, bbr_ref, bbi_ref):
        ar, ai, bbr, bbi = _s5_param_fn(lr_ref[...], li_ref[...], ls_ref[...], br_ref[...], bi_ref[...])
        ar_ref[...], ai_ref[...], bbr_ref[...], bbi_ref[...] = ar, ai, bbr, bbi

    sd = jax.ShapeDtypeStruct
    return pl.pallas_call(
        kern, name="s5_params_fwd",
        out_shape=[sd(lr.shape, F32), sd(lr.shape, F32), sd(brt.shape, F32), sd(brt.shape, F32)],
    )(lr, li, ls, brt, bit)


def _s5_params_bwd(lr, li, ls, brt, bit, dar, dai, dbbr, dbbi):
    def kern(lr_ref, li_ref, ls_ref, br_ref, bi_ref, dar_ref, dai_ref, dbbr_ref, dbbi_ref,
             dlr_ref, dli_ref, dls_ref, dbr_ref, dbi_ref):
        _, vjp = jax.vjp(_s5_param_fn, lr_ref[...], li_ref[...], ls_ref[...], br_ref[...], bi_ref[...])
        dlr, dli, dls, dbr, dbi = vjp((dar_ref[...], dai_ref[...], dbbr_ref[...], dbbi_ref[...]))
        dlr_ref[...], dli_ref[...], dls_ref[...], dbr_ref[...], dbi_ref[...] = dlr, dli, dls, dbr, dbi

    sd = jax.ShapeDtypeStruct
    return pl.pallas_call(
        kern, name="s5_params_bwd",
        out_shape=[sd(lr.shape, F32), sd(lr.shape, F32), sd(ls.shape, F32), sd(brt.shape, F32), sd(brt.shape, F32)],
    )(lr, li, ls, brt, bit, dar, dai, dbbr, dbbi)


SCAN_LW = 512


SCAN_SEGS = 8
SCAN_UNROLL = 4


def _cmul(xr, xi, yr, yi):
    return xr * yr - xi * yi, xr * yi + xi * yr


def _scan_tables(a_ref, tab_ref, conj, seg_len):
    ns = a_ref.shape[1]
    ar = jnp.broadcast_to(a_ref[0:1, :], (8, ns))
    ai = jnp.broadcast_to(a_ref[1:2, :], (8, ns))
    if conj:
        ai = -ai
    big, base, e = None, (ar, ai), seg_len
    while e:
        if e & 1:
            big = base if big is None else _cmul(*big, *base)
        base = _cmul(*base, *base)
        e >>= 1
    big2 = _cmul(*big, *big)
    big4 = _cmul(*big2, *big2)
    for k, v in enumerate((ar, ai) + big + big2 + big4):
        tab_ref[k] = v


def _scan_block(x_ref, tab_ref, carry_ref, t_rows, ns, reverse):
    sl = t_rows // SCAN_SEGS
    assert sl % SCAN_UNROLL == 0
    row = lax.broadcasted_iota(jnp.int32, (8, SCAN_LW), 0)
    zero = jnp.zeros((8, SCAN_LW), F32)
    for lc in range(ns // SCAN_LW):
        lre = pl.ds(lc * SCAN_LW, SCAN_LW)
        lim = pl.ds(ns + lc * SCAN_LW, SCAN_LW)
        ar, ai = tab_ref[0, :, lre], tab_ref[1, :, lre]

        def rows_of(k, u):
            j = k * SCAN_UNROLL + u
            return pl.ds(pl.multiple_of(((sl - 1 - j) if reverse else j) * SCAN_SEGS, SCAN_SEGS), SCAN_SEGS)

        def local(k, s, lre=lre, lim=lim, ar=ar, ai=ai):
            sr, si = s
            for u in range(SCAN_UNROLL):
                rows = rows_of(k, u)
                tr, ti = _cmul(ar, ai, sr, si)
                sr, si = x_ref[rows, lre] + tr, x_ref[rows, lim] + ti
                x_ref[rows, lre], x_ref[rows, lim] = sr, si
            return sr, si

        er, ei = lax.fori_loop(0, sl // SCAN_UNROLL, local, (zero, zero))
        if reverse:
            cr = jnp.where(row == 7, carry_ref[:, lre], pltpu.roll(er, 7, 0))
            ci = jnp.where(row == 7, carry_ref[:, lim], pltpu.roll(ei, 7, 0))
        else:
            cr = jnp.where(row == 0, carry_ref[:, lre], pltpu.roll(er, 1, 0))
            ci = jnp.where(row == 0, carry_ref[:, lim], pltpu.roll(ei, 1, 0))
        for lvl, dsh in enumerate((1, 2, 4)):
            pr, pi = tab_ref[2 + 2 * lvl, :, lre], tab_ref[3 + 2 * lvl, :, lre]
            if reverse:
                keep, shift = row < 8 - dsh, 8 - dsh
            else:
                keep, shift = row >= dsh, dsh
            sr = jnp.where(keep, pltpu.roll(cr, shift, 0), 0.0)
            si = jnp.where(keep, pltpu.roll(ci, shift, 0), 0.0)
            tr, ti = _cmul(pr, pi, sr, si)
            cr, ci = cr + tr, ci + ti
        tr, ti = _cmul(tab_ref[2, :, lre], tab_ref[3, :, lre], cr, ci)
        edge = 0 if reverse else 7
        carry_ref[:, lre] = jnp.broadcast_to((er + tr)[edge:edge + 1, :], (8, SCAN_LW))
        carry_ref[:, lim] = jnp.broadcast_to((ei + ti)[edge:edge + 1, :], (8, SCAN_LW))

        def fix(k, t, lre=lre, lim=lim, ar=ar, ai=ai):
            tr, ti = t
            for u in range(SCAN_UNROLL):
                rows = rows_of(k, u)
                tr, ti = _cmul(ar, ai, tr, ti)
                x_ref[rows, lre] = x_ref[rows, lre] + tr
                x_ref[rows, lim] = x_ref[rows, lim] + ti
            return tr, ti

        lax.fori_loop(0, sl // SCAN_UNROLL, fix, (cr, ci))


def _bd_expand(u, w_ref, x_ref, ns):
    hh, sh = u.shape[1] // 2, ns // 2
    ub = u.astype(BF16)
    for part in range(2):
        for k in range(2):
            cols = slice(part * ns + k * sh, part * ns + (k + 1) * sh)
            x_ref[:, cols] = jnp.dot(ub[:, k * hh:(k + 1) * hh], w_ref[k * hh:(k + 1) * hh, cols],
                                     preferred_element_type=F32)


def _bd_contract(x_ref, w_ref, ns):
    hh, sh = w_ref.shape[1] // 2, ns // 2
    halves = []
    for k in range(2):
        acc = None
        for part in range(2):
            rows = slice(part * ns + k * sh, part * ns + (k + 1) * sh)
            t = jnp.dot(x_ref[:, rows].astype(BF16), w_ref[rows, k * hh:(k + 1) * hh], preferred_element_type=F32)
            acc = t if acc is None else acc + t
        halves.append(acc)
    return jnp.concatenate(halves, axis=1)


def _scan_rows(lp):
    for t in (384, 256, 128):
        if lp % t == 0:
            return t
    raise ValueError(lp)


def _seg_perm(t_rows):
    r = jnp.arange(t_rows)
    src = (r % SCAN_SEGS) * (t_rows // SCAN_SEGS) + r // SCAN_SEGS
    p = (src[:, None] == r[None, :]).astype(BF16)
    return p, p.T


def _permute_rows(p_ref, v):
    return jnp.dot(p_ref[...], v.astype(BF16), preferred_element_type=F32)


def _unpermute_rows(pt_ref, v):
    hi = v.astype(BF16)
    lo = (v - hi.astype(F32)).astype(BF16)
    pt = pt_ref[...]
    return jnp.dot(pt, hi, preferred_element_type=F32) + jnp.dot(pt, lo, preferred_element_type=F32)


def _s5_scan_fwd(u, bfull, cfull, a2, dvec, bsz):
    rows, hw = u.shape
    ns = a2.shape[1]
    lp = rows // bsz
    t_rows = _scan_rows(lp)
    nc = lp // t_rows
    pmat, pmat_t = _seg_perm(t_rows)

    def kern(u_ref, b_ref, c_ref, a_ref, d_ref, p_ref, pt_ref, y_ref, x_ref, up_ref, tab_ref, carry_ref):
        c = pl.program_id(1)

        @pl.when((pl.program_id(0) == 0) & (c == 0))
        def _():
            _scan_tables(a_ref, tab_ref, False, t_rows // SCAN_SEGS)

        @pl.when(c == 0)
        def _():
            carry_ref[...] = jnp.zeros_like(carry_ref)

        ux = u_ref[...]
        up = _permute_rows(p_ref, ux)
        up_ref[...] = up.astype(BF16)
        _bd_expand(up, b_ref, x_ref, ns)
        _scan_block(x_ref, tab_ref, carry_ref, t_rows, ns, reverse=False)
        y_ref[...] = _unpermute_rows(pt_ref, _bd_contract(x_ref, c_ref, ns)) + d_ref[...] * ux

    const = lambda shp: pl.BlockSpec(shp, lambda b, c: (0,) * len(shp), pipeline_mode=pl.Buffered(1))
    blk = lambda b, c: (b * nc + c, 0)
    return pl.pallas_call(
        kern, name="s5_scan_fwd", grid=(bsz, nc),
        in_specs=[pl.BlockSpec((t_rows, hw), blk), const(bfull.shape), const(cfull.shape), const(a2.shape),
                  const(dvec.shape), const(pmat.shape), const(pmat.shape)],
        out_specs=[pl.BlockSpec((t_rows, hw), blk), pl.BlockSpec((t_rows, 2 * ns), blk),
                   pl.BlockSpec((t_rows, hw), blk)],
        out_shape=[jax.ShapeDtypeStruct((rows, hw), F32), jax.ShapeDtypeStruct((rows, 2 * ns), F32),
                   jax.ShapeDtypeStruct((rows, hw), BF16)],
        scratch_shapes=[pltpu.VMEM((8, 8, ns), F32), pltpu.VMEM((8, 2 * ns), F32)],
        compiler_params=_cparams(("arbitrary", "arbitrary")),
    )(u, bfull, cfull, a2, dvec, pmat, pmat_t)


def _s5_scan_bwd(dy, u, xs, ctfull, btfull, a2, dvec, bsz):
    rows, hw = u.shape
    ns = a2.shape[1]
    lp = rows // bsz
    t_rows = _scan_rows(lp)
    nc = lp // t_rows
    blk = lambda b, c: (b * nc + (nc - 1 - c), 0)
    pmat, pmat_t = _seg_perm(t_rows)

    def prev8(b, c):
        first = (b * nc + (nc - 1 - c)) * (t_rows // 8)
        return (jnp.maximum(first - 1, 0), 0)

    def kern(dy_ref, u_ref, x_ref, xp_ref, ct_ref, bt_ref, a_ref, d_ref, p_ref, pt_ref,
             du_ref, gx_ref, dyp_ref, da_ref, dd_ref, tab_ref, carry_ref):
        b, c = pl.program_id(0), pl.program_id(1)
        first = (b == 0) & (c == 0)

        @pl.when(first)
        def _():
            _scan_tables(a_ref, tab_ref, True, t_rows // SCAN_SEGS)

        @pl.when(c == 0)
        def _():
            carry_ref[...] = jnp.zeros_like(carry_ref)

        dyx, ux = dy_ref[...], u_ref[...]
        dyp = _permute_rows(p_ref, dyx)
        dyp_ref[...] = dyp.astype(BF16)
        _bd_expand(dyp, ct_ref, gx_ref, ns)
        _scan_block(gx_ref, tab_ref, carry_ref, t_rows, ns, reverse=True)
        gx = gx_ref[...]
        du_ref[...] = _unpermute_rows(pt_ref, _bd_contract(gx_ref, bt_ref, ns)) + d_ref[...] * dyx
        seq_start = c == nc - 1
        row8 = lax.broadcasted_iota(jnp.int32, (8, 1), 0)
        head = pltpu.roll(x_ref[t_rows - 8:t_rows, :], 1, 0)
        head = jnp.where(row8 == 0, jnp.where(seq_start, 0.0, xp_ref[7:8, :]), head)
        xprev = jnp.concatenate([head, x_ref[0:t_rows - 8, :]], axis=0)
        xr, xi, gr, gi = xprev[:, :ns], xprev[:, ns:], gx[:, :ns], gx[:, ns:]
        da = jnp.concatenate([jnp.sum(xr * gr + xi * gi, axis=0, keepdims=True),
                              jnp.sum(xr * gi - xi * gr, axis=0, keepdims=True)], axis=1)
        dd = jnp.sum(dyx * ux, axis=0, keepdims=True)

        @pl.when(first)
        def _():
            da_ref[...] = da
            dd_ref[...] = dd

        @pl.when(jnp.logical_not(first))
        def _():
            da_ref[...] += da
            dd_ref[...] += dd

    const = lambda shp: pl.BlockSpec(shp, lambda b, c: (0,) * len(shp), pipeline_mode=pl.Buffered(1))
    return pl.pallas_call(
        kern, name="s5_scan_bwd", grid=(bsz, nc),
        in_specs=[pl.BlockSpec((t_rows, hw), blk), pl.BlockSpec((t_rows, hw), blk),
                  pl.BlockSpec((t_rows, 2 * ns), blk), pl.BlockSpec((8, 2 * ns), prev8),
                  const(ctfull.shape), const(btfull.shape), const(a2.shape), const(dvec.shape),
                  const(pmat.shape), const(pmat.shape)],
        out_specs=[pl.BlockSpec((t_rows, hw), blk), pl.BlockSpec((t_rows, 2 * ns), blk),
                   pl.BlockSpec((t_rows, hw), blk),
                   pl.BlockSpec((1, 2 * ns), lambda b, c: (0, 0)), pl.BlockSpec((1, hw), lambda b, c: (0, 0))],
        out_shape=[jax.ShapeDtypeStruct((rows, hw), F32), jax.ShapeDtypeStruct((rows, 2 * ns), F32),
                   jax.ShapeDtypeStruct((rows, hw), BF16),
                   jax.ShapeDtypeStruct((1, 2 * ns), F32), jax.ShapeDtypeStruct((1, hw), F32)],
        scratch_shapes=[pltpu.VMEM((8, 8, ns), F32), pltpu.VMEM((8, 2 * ns), F32)],
        compiler_params=_cparams(("arbitrary", "arbitrary")),
    )(dy, u, xs, xs, ctfull, btfull, a2, dvec, pmat, pmat_t)


def _glu_fwd(y, h1, wout):
    rows, d = h1.shape

    def body(step, y_ref, h_ref, w_ref, o_ref):
        z = _dot(_gelu(y_ref[...])[0], w_ref[...])
        o_ref[...] = h_ref[...] + z[:, :d] * _sigmoid(z[:, d:])

    return _rowcall("glu_fwd", body, rows, [y, h1], [wout], [(d, F32)], light=True)[0]


def _glu_bwd(y, dh2, wout, rider=None):
    rows, d = dh2.shape
    hw = y.shape[1]

    def body(step, y_ref, dh_ref, w_ref, dy_ref, dw_ref):
        yx, dh = y_ref[...], dh_ref[...]
        gl, t = _gelu(yx)
        z = _dot(gl, w_ref[...])
        za, sg = z[:, :d], _sigmoid(z[:, d:])
        dza = dh * sg
        dzg = dh * za * sg * (1.0 - sg)
        dgl = _dot_nt(dza, w_ref[:, :d]) + _dot_nt(dzg, w_ref[:, d:])
        dy_ref[...] = dgl * _gelu_grad(yx, t)
        for half, dz in enumerate((dza, dzg)):
            dw = _dot_tn(gl, dz)
            for s in range(N_DEV // 2):
                _acc(step, dw_ref.at[half * (N_DEV // 2) + s], dw[:, s * cw:(s + 1) * cw])

    cw = 2 * d // N_DEV
    return _rowcall("glu_bwd", body, rows, [y, dh2], [wout], [(hw, F32)], [((N_DEV, hw, cw), F32)], rider=rider,
                    light=True)


def _gmean64(x2, gmat):
    hi = x2.astype(BF16)
    r1 = x2 - hi.astype(F32)
    mid = r1.astype(BF16)
    lo = (r1 - mid.astype(F32)).astype(BF16)
    outs = []
    for j in range(x2.shape[1] // LANES):
        sl = slice(j * LANES, (j + 1) * LANES)
        f = lambda p: jnp.dot(p[:, sl], gmat, preferred_element_type=F32)
        outs.append(f(hi) + f(mid) + f(lo))
    return outs[0] if len(outs) == 1 else jnp.concatenate(outs, axis=1)


def _swap32(x):
    w = x.shape[1]
    lane = lax.broadcasted_iota(jnp.int32, (1, w), 1)
    return jnp.where((lane & 32) == 0, pltpu.roll(x, w - 32, 1), pltpu.roll(x, 32, 1))


def _tile_lanes(t, w):
    reps = w // t.shape[1]
    return t if reps == 1 else jnp.concatenate([t] * reps, axis=1)


def _headrope_fwd(name, raw, w, gain, cos, sin, gmat, lp):
    rows = raw.shape[0]
    tm = _row_tile(lp)
    per = lp // tm

    def body(step, x_ref, c_ref, s_ref, g_ref, gm_ref, o_ref):
        x = x_ref[...]
        rstd = lax.rsqrt(_gmean64(x * x, gm_ref[...]) + EPS)
        z = x * rstd * g_ref[...]
        o_ref[...] = z * _tile_lanes(c_ref[...], w) + _swap32(z) * _tile_lanes(s_ref[...], w)

    maps = [((tm, w), lambda i: (i, 0)), ((tm, LANES), lambda i: (i % per, 0)), ((tm, LANES), lambda i: (i % per, 0))]
    return _rowcall(name, body, rows, [raw, cos, sin], [gain, gmat], [(w, F32)], tm=tm, row_in_maps=maps)[0]


def _headrope_bwd(name, raw, w, dout, gain, cos, sin, gmat, lp):
    rows = raw.shape[0]
    tm = _row_tile(lp)
    per = lp // tm

    def body(step, x_ref, do_ref, c_ref, s_ref, g_ref, gm_ref, dx_ref, dg_ref):
        x, dout_x, gx, gm = x_ref[...], do_ref[...], g_ref[...], gm_ref[...]
        rstd = lax.rsqrt(_gmean64(x * x, gm) + EPS)
        yn = x * rstd
        dz = dout_x * _tile_lanes(c_ref[...], w) + _swap32(dout_x * _tile_lanes(s_ref[...], w))
        dyn = dz * gx
        dx_ref[...] = rstd * (dyn - yn * _gmean64(dyn * yn, gm))
        dg = jnp.sum(dz * yn, axis=0, keepdims=True)
        sh = w // 2
        while sh >= HEAD_DIM:
            dg = dg + pltpu.roll(dg, sh, 1)
            sh //= 2
        _acc(step, dg_ref, dg)

    maps = [((tm, w), lambda i: (i, 0)), None, ((tm, LANES), lambda i: (i % per, 0)), ((tm, LANES), lambda i: (i % per, 0))]
    return _rowcall(name, body, rows, [raw, dout, cos, sin], [gain, gmat], [(w, F32)], [((1, w), F32)],
                    tm=tm, row_in_maps=maps)


KVW = N_KV_HEADS * HEAD_DIM
QB = 128


def _fold4(x):
    y = x + pltpu.roll(x, 128, 1)
    return y + pltpu.roll(y, 64, 1)


ATTN_SCALE = HEAD_DIM ** -0.5
assert math.log2(ATTN_SCALE).is_integer()


def _attn_masks(i):
    k0j = lax.broadcasted_iota(jnp.int32, (Q_PER_KV * QB, QB), 1)
    qi = lax.broadcasted_iota(jnp.int32, (Q_PER_KV * QB, 2 * QB), 0) % QB
    kj = lax.broadcasted_iota(jnp.int32, (Q_PER_KV * QB, 2 * QB), 1)
    in_prev = (kj < QB) & (kj > qi) & (i >= 2)
    in_cur = (kj >= QB) & (kj - QB <= qi)
    return k0j >= META0, in_prev | in_cur


def _attn_scores(i, q_ref, k0_ref, kp_ref, kc_ref, sink_ref, h):
    masks = _attn_masks(i)
    lane = lax.broadcasted_iota(jnp.int32, (1, KVW), 1) // HEAD_DIM
    qh = q_ref[:, h * KVW:(h + 1) * KVW] * ATTN_SCALE
    qs = jnp.concatenate([jnp.where(lane == g, qh, 0.0) for g in range(Q_PER_KV)], axis=0).astype(BF16)
    hsel = lane == h
    kx = _expand_kv((k0_ref, kp_ref, kc_ref), hsel)
    s0 = jnp.where(masks[0], _dot_nt(qs, kx[0]), NEG_INF)
    sb = jnp.where(masks[1], _dot_nt(qs, kx[1]), NEG_INF)
    rowg = lax.broadcasted_iota(jnp.int32, (Q_PER_KV * QB, 1), 0) // QB
    sink = jnp.zeros((Q_PER_KV * QB, 1), F32)
    for g in range(Q_PER_KV):
        sink = jnp.where(rowg == g, sink_ref[0, h * Q_PER_KV + g], sink)
    m = jnp.maximum(jnp.maximum(jnp.max(s0, axis=1, keepdims=True), jnp.max(sb, axis=1, keepdims=True)), sink)
    p0, pb, ps = jnp.exp(s0 - m), jnp.exp(sb - m), jnp.exp(sink - m)
    den = jnp.sum(p0, axis=1, keepdims=True) + jnp.sum(pb, axis=1, keepdims=True) + ps
    return qs, kx, (p0, pb), ps, den, lane, hsel


def _expand_kv(refs, hsel):
    x0, xp, xc = [_fold4(jnp.where(hsel, r[...], 0.0)).astype(BF16) for r in refs]
    return [x0, jnp.concatenate([xp, xc], axis=0)]


def _unstack(x, lane):
    out = jnp.where(lane == 0, x[0:QB], 0.0)
    for g in range(1, Q_PER_KV):
        out = out + jnp.where(lane == g, x[g * QB:(g + 1) * QB], 0.0)
    return out


def _attn_specs(nb, d):
    qspec = pl.BlockSpec((None, QB, d), lambda b, i: (b, i, 0))
    k0 = pl.BlockSpec((None, QB, KVW), lambda b, i: (b, 0, 0))
    kp = pl.BlockSpec((None, QB, KVW), lambda b, i: (b, jnp.maximum(i - 1, 0), 0))
    kc = pl.BlockSpec((None, QB, KVW), lambda b, i: (b, i, 0))
    v0 = pl.BlockSpec((None, QB, KVW), lambda b, i: (b, 0, 1))
    vp = pl.BlockSpec((None, QB, KVW), lambda b, i: (b, jnp.maximum(i - 1, 0), 1))
    vc = pl.BlockSpec((None, QB, KVW), lambda b, i: (b, i, 1))
    sink = pl.BlockSpec(memory_space=pltpu.SMEM)
    return qspec, [k0, kp, kc], [v0, vp, vc], sink


def _attn_fwd(q, k, kv, sinks):
    bsz, lp, d = q.shape
    nb = lp // QB
    qspec, kspecs, vspecs, sspec = _attn_specs(nb, d)

    def kern(q_ref, k0_ref, kp_ref, kc_ref, v0_ref, vp_ref, vc_ref, sink_ref, o_ref):
        i = pl.program_id(1)
        for h in range(N_KV_HEADS):
            qs, kx, ps3, psink, den, lane, hsel = _attn_scores(i, q_ref, k0_ref, kp_ref, kc_ref, sink_ref, h)
            vx = _expand_kv((v0_ref, vp_ref, vc_ref), hsel)
            o = _dot(ps3[0], vx[0]) + _dot(ps3[1], vx[1])
            o_ref[:, h * KVW:(h + 1) * KVW] = _unstack(o / den, lane)

    return pl.pallas_call(
        kern, name="attn_fwd", grid=(bsz, nb),
        in_specs=[qspec] + kspecs + vspecs + [sspec],
        out_specs=qspec, out_shape=jax.ShapeDtypeStruct((bsz, lp, d), F32),
        compiler_params=_cparams(("arbitrary", "arbitrary")),
    )(q, k, k, k, kv, kv, kv, sinks)


def _attn_bwd(q, k, kv, sinks, o, do):
    bsz, lp, d = q.shape
    nb = lp // QB
    qspec, kspecs, vspecs, sspec = _attn_specs(nb, d)
    full = pl.BlockSpec((None, lp, KVW), lambda b, i: (b, 0, 0))

    def kern(q_ref, k0_ref, kp_ref, kc_ref, v0_ref, vp_ref, vc_ref, sink_ref, o_ref, do_ref,
             dq_ref, dk_ref, dv_ref, ds_ref):
        b, i = pl.program_id(0), pl.program_id(1)

        @pl.when(i == 0)
        def _():
            dk_ref[...] = jnp.zeros_like(dk_ref)
            dv_ref[...] = jnp.zeros_like(dv_ref)

        @pl.when((b == 0) & (i == 0))
        def _():
            ds_ref[...] = jnp.zeros_like(ds_ref)

        lane128 = lax.broadcasted_iota(jnp.int32, (1, LANES), 1)
        rowg = lax.broadcasted_iota(jnp.int32, (Q_PER_KV * QB, 1), 0) // QB
        dk_acc = [jnp.zeros((QB, KVW), F32), jnp.zeros((2 * QB, KVW), F32)]
        dv_acc = [jnp.zeros((QB, KVW), F32), jnp.zeros((2 * QB, KVW), F32)]
        dsink = jnp.zeros((1, LANES), F32)
        for h in range(N_KV_HEADS):
            qs, kx, ps3, psink, den, lane, hsel = _attn_scores(i, q_ref, k0_ref, kp_ref, kc_ref, sink_ref, h)
            vx = _expand_kv((v0_ref, vp_ref, vc_ref), hsel)
            sl = slice(h * KVW, (h + 1) * KVW)
            doh, oh = do_ref[:, sl], o_ref[:, sl]
            dos = jnp.concatenate([jnp.where(lane == g, doh, 0.0) for g in range(Q_PER_KV)], axis=0)
            ost = jnp.concatenate([jnp.where(lane == g, oh, 0.0) for g in range(Q_PER_KV)], axis=0)
            delta = jnp.sum(dos * ost, axis=1, keepdims=True)
            inv = 1.0 / den
            dosb = dos.astype(BF16)
            dqs = jnp.zeros((Q_PER_KV * QB, KVW), F32)
            for n in range(2):
                pn = ps3[n] * inv
                ds = pn * (_dot_nt(dosb, vx[n]) - delta)
                dqs = dqs + _dot(ds, kx[n])
                dk_acc[n] = dk_acc[n] + jnp.where(hsel, _fold4(_dot_tn(ds, qs)), 0.0)
                dv_acc[n] = dv_acc[n] + jnp.where(hsel, _fold4(_dot_tn(pn, dosb)), 0.0)
            dq_ref[:, sl] = _unstack(dqs, lane) * ATTN_SCALE
            dsk = -(psink * inv) * delta
            for g in range(Q_PER_KV):
                val = jnp.sum(jnp.where(rowg == g, dsk, 0.0), axis=0, keepdims=True)
                dsink = dsink + jnp.where(lane128 == h * Q_PER_KV + g, val, 0.0)
        ds_ref[...] += dsink
        r0 = pl.ds(0, QB)
        rp = pl.ds(pl.multiple_of(jnp.maximum(i - 1, 0) * QB, QB), QB)
        rc = pl.ds(pl.multiple_of(i * QB, QB), QB)
        for acc, ref in ((dk_acc, dk_ref), (dv_acc, dv_ref)):
            ref[r0, :] += acc[0]
            ref[rp, :] += acc[1][:QB]
            ref[rc, :] += acc[1][QB:]

    return pl.pallas_call(
        kern, name="attn_bwd", grid=(bsz, nb),
        in_specs=[qspec] + kspecs + vspecs + [sspec, qspec, qspec],
        out_specs=[qspec, full, full, pl.BlockSpec((1, LANES), lambda b, i: (0, 0))],
        out_shape=[jax.ShapeDtypeStruct((bsz, lp, d), F32), jax.ShapeDtypeStruct((bsz, lp, KVW), F32),
                   jax.ShapeDtypeStruct((bsz, lp, KVW), F32), jax.ShapeDtypeStruct((1, LANES), F32)],
        compiler_params=_cparams(("arbitrary", "arbitrary")),
    )(q, k, k, k, kv, kv, kv, sinks, o, do)


def _concat_cols(name, a, b):
    rows = a.shape[0]

    def body(step, a_ref, b_ref, o_ref):
        o_ref[...] = jnp.concatenate([a_ref[...], b_ref[...]], axis=1)

    return _rowcall(name, body, rows, [a, b], [], [(a.shape[1] + b.shape[1], F32)], light=True)[0]


def _adamw(name, w, m, v, parts):
    rows, wd = w.shape
    n = parts.shape[0]
    tm = _stream_tile(rows, wd * (7 * 4 + n * parts.dtype.itemsize))

    def kern(w_ref, m_ref, v_ref, p_ref, g_ref, d_ref, m2_ref, v2_ref):
        g = p_ref[0].astype(F32)
        for k in range(1, n):
            g = g + p_ref[k].astype(F32)
        m2 = ADAM_B1 * m_ref[...] + (1.0 - ADAM_B1) * g
        v2 = ADAM_B2 * v_ref[...] + (1.0 - ADAM_B2) * (g * g)
        mh = m2 / (1.0 - ADAM_B1 ** ADAM_STEP)
        vh = v2 / (1.0 - ADAM_B2 ** ADAM_STEP)
        g_ref[...] = g
        d_ref[...] = -ADAM_LR * (mh / (jnp.sqrt(vh) + ADAM_EPS) + ADAM_WD * w_ref[...])
        m2_ref[...] = m2
        v2_ref[...] = v2

    spec = pl.BlockSpec((tm, wd), lambda i: (i, 0))
    sd = jax.ShapeDtypeStruct((rows, wd), F32)
    return pl.pallas_call(
        kern, name=name, grid=(rows // tm,),
        in_specs=[spec, spec, spec, pl.BlockSpec((n, tm, wd), lambda i: (0, i, 0))],
        out_specs=[spec] * 4, out_shape=[sd] * 4,
        compiler_params=_cparams(("arbitrary",)),
    )(w, m, v, parts)


def _pair_sum(name, parts, theirs, my_c):
    n, _, rows, wd = parts.shape
    tm = _stream_tile(rows, wd * (4 + 4 + 2))

    def kern(c_ref, a_ref, b_ref, o_ref):
        o_ref[...] = (a_ref[...] + b_ref[...]).astype(BF16)

    return pl.pallas_call(
        kern, name=name,
        grid_spec=pltpu.PrefetchScalarGridSpec(
            num_scalar_prefetch=1, grid=(n, rows // tm),
            in_specs=[pl.BlockSpec((None, None, tm, wd), lambda k, i, c: (k, c[0], i, 0)),
                      pl.BlockSpec((None, tm, wd), lambda k, i, c: (k, i, 0))],
            out_specs=pl.BlockSpec((None, tm, wd), lambda k, i, c: (k, i, 0))),
        out_shape=jax.ShapeDtypeStruct((n, rows, wd), BF16), compiler_params=_cparams(("arbitrary", "arbitrary")),
    )(my_c, parts, theirs)


MESH = pl.DeviceIdType.MESH
ANY = pl.BlockSpec(memory_space=pl.ANY)


def _place():
    x, y, c = lax.axis_index("x"), lax.axis_index("y"), lax.axis_index("c")
    return x, y, c, [(1 - x, y), (x, 1 - y), (1 - x, 1 - y)]


def _gather_rider(shards):
    n = len(shards)

    def copy(refs, a, k, block, to, own=False):
        x_refs, out_refs, (send_sems, recv_sems, _) = refs
        px, py, pc = block
        slot = out_refs[a].at[4 * px + 2 * py + pc]
        return pltpu.make_async_remote_copy(
            src_ref=x_refs[a] if own else slot, dst_ref=slot,
            send_sem=send_sems.at[a, k], recv_sem=recv_sems.at[a, k], device_id=to, device_id_type=MESH)

    def local(refs, a):
        x, y, c, _ = _place()
        return pltpu.make_async_copy(refs[0][a], refs[1][a].at[4 * x + 2 * y + c], refs[2][2].at[a])

    def first(refs):
        x, y, c, chips = _place()
        out = []
        for a in range(n):
            out.append(copy(refs, a, 0, (x, y, c), (x, y, 1 - c), own=True))
            out += [copy(refs, a, 1 + j, (x, y, c), (*chip, c), own=True) for j, chip in enumerate(chips)]
        return out

    def passed(refs):
        x, y, c, chips = _place()
        return [copy(refs, a, 4 + j, (*chip, c), (x, y, 1 - c)) for j, chip in enumerate(chips) for a in range(n)]

    def start(*refs):
        for a in range(n):
            local(refs, a).start()
        for cp in first(refs):
            cp.start()

    def mid(*refs):
        x, y, c, chips = _place()
        fwd = passed(refs)
        for j, chip in enumerate(chips):
            for a in range(n):
                copy(refs, a, 1 + j, (*chip, c), (x, y, c)).wait_recv()
                fwd[j * n + a].start()

    def finish(*refs):
        x, y, c, chips = _place()
        for a in range(n):
            copy(refs, a, 0, (x, y, 1 - c), (x, y, c)).wait_recv()
            for j, chip in enumerate(chips):
                copy(refs, a, 4 + j, (*chip, 1 - c), (x, y, c)).wait_recv()
        for cp in first(refs) + passed(refs):
            cp.wait_send()
        for a in range(n):
            local(refs, a).wait()

    return _Rider(list(shards), [jax.ShapeDtypeStruct((N_DEV,) + s.shape, s.dtype) for s in shards],
                  [pltpu.SemaphoreType.DMA((n, 7)), pltpu.SemaphoreType.DMA((n, 7)), pltpu.SemaphoreType.DMA((n,))],
                  start, mid, finish)


def _swap_rider(parts):
    n = len(parts)

    def copies(p_refs, out_refs, sems):
        x, y, c, _ = _place()
        return [pltpu.make_async_remote_copy(
            src_ref=p_refs[a].at[:, 1 - c], dst_ref=out_refs[a], send_sem=sems[0].at[a], recv_sem=sems[1].at[a],
            device_id=(x, y, 1 - c), device_id_type=MESH) for a in range(n)]

    def start(*refs):
        for cp in copies(*refs):
            cp.start()

    def finish(*refs):
        for cp in copies(*refs):
            cp.wait()

    return _Rider(list(parts), [jax.ShapeDtypeStruct((p.shape[0],) + p.shape[2:], p.dtype) for p in parts],
                  [pltpu.SemaphoreType.DMA((n,)), pltpu.SemaphoreType.DMA((n,))], start, None, finish)


def _scatter_rider(sums):
    n = len(sums)

    def copy(refs, a, j, block):
        s_refs, out_refs, (send_sems, recv_sems, _) = refs
        x, y, c, chips = _place()
        px, py = chips[j]
        return pltpu.make_async_remote_copy(
            src_ref=s_refs[a].at[2 * px + py], dst_ref=out_refs[a].at[block],
            send_sem=send_sems.at[a, j], recv_sem=recv_sems.at[a, j], device_id=(px, py, c), device_id_type=MESH)

    def local(refs, a):
        x, y, c, _ = _place()
        return pltpu.make_async_copy(refs[0][a].at[2 * x + y], refs[1][a].at[2 * x + y], refs[2][2].at[a])

    def sends(refs):
        x, y, c, _ = _place()
        return [copy(refs, a, j, 2 * x + y) for j in range(3) for a in range(n)]

    def start(*refs):
        for a in range(n):
            local(refs, a).start()
        for cp in sends(refs):
            cp.start()

    def finish(*refs):
        x, y, c, chips = _place()
        for j, (px, py) in enumerate(chips):
            for a in range(n):
                copy(refs, a, j, 2 * px + py).wait_recv()
        for cp in sends(refs):
            cp.wait_send()
        for a in range(n):
            local(refs, a).wait()

    return _Rider(list(sums), [jax.ShapeDtypeStruct(s.shape, s.dtype) for s in sums],
                  [pltpu.SemaphoreType.DMA((n, 3)), pltpu.SemaphoreType.DMA((n, 3)), pltpu.SemaphoreType.DMA((n,))],
                  start, None, finish)


BIG = (("ffn1_w_gate_up", 2), ("ffn1_w_down", 1), ("ffn2_w_gate_up", 2), ("ffn2_w_down", 1), ("ssm_w_in", 1),
       ("ssm_w_out", 2), ("w_kv", 0), ("attn_w_q", 1), ("attn_w_o", 1))
SMALL = ("ffn1_norm", "mix_norm", "ffn2_norm", "ssm_lambda_re", "ssm_lambda_im", "ssm_b_re", "ssm_b_im",
         "ssm_c_re", "ssm_c_im", "ssm_log_step", "kv_norm", "k_norm", "q_norm", "attn_sinks")
COLS = (("meta_tokens", 1), ("ssm_d", 1))
WEIGHTS = ("meta_tokens", "ffn1_norm", "ffn1_w_gate_up", "ffn1_w_down", "mix_norm", "ffn2_norm", "ffn2_w_gate_up",
           "ffn2_w_down", "ssm_w_in", "ssm_lambda_re", "ssm_lambda_im", "ssm_b_re", "ssm_b_im", "ssm_c_re",
           "ssm_c_im", "ssm_log_step", "ssm_d", "ssm_w_out", "kv_norm", "w_kv", "k_norm", "attn_w_q", "q_norm",
           "attn_sinks", "attn_w_o")


def _rows_of(a, width):
    n = math.prod(a.shape)
    if n % width == 0:
        r = a.reshape(n // width, width)
    else:
        assert n < width
        r = jnp.pad(a.reshape(1, n), ((0, 0), (0, width - n)))
    return jnp.pad(r, ((0, (-r.shape[0]) % 8), (0, 0)))


def _pack_small(arrs, width):
    return jnp.concatenate([_rows_of(a.astype(F32), width) for a in arrs], axis=0)


def _unpack_small(buf, shapes, width):
    out, off = [], 0
    for shp in shapes:
        n = math.prod(shp)
        r = max(n // width, 1)
        out.append(buf[off:off + r].reshape(shp) if n % width == 0 else buf[off, :n].reshape(shp))
        off += r + (-r) % 8
    return out


def _shape2d(shp):
    return (math.prod(shp[:-1]), shp[-1])


def _unshard(g, axis):
    g = jnp.moveaxis(g, 0, axis)
    shp = g.shape
    return g.reshape(shp[:axis] + (shp[axis] * shp[axis + 1],) + shp[axis + 2:])


def _shard(full, axis):
    shp = full.shape
    g = full.reshape(shp[:axis] + (N_DEV, shp[axis] // N_DEV) + shp[axis + 1:])
    return jnp.moveaxis(g, axis, 0)


def _blockdiag(blocks):
    g, r, c = blocks.shape
    eye = jnp.eye(g, dtype=blocks.dtype)
    return (eye[:, None, :, None] * blocks[:, :, None, :]).reshape(g * r, g * c)


def _diagblocks(full, g):
    r, c = full.shape[0] // g, full.shape[1] // g
    f = full.reshape(g, r, g, c)
    idx = jnp.arange(g)
    return f[idx, :, idx, :]


def kernel(x, meta_tokens, ffn1_norm, ffn1_w_gate_up, ffn1_w_down, mix_norm, ffn2_norm, ffn2_w_gate_up, ffn2_w_down, ssm_w_in, ssm_lambda_re, ssm_lambda_im, ssm_b_re, ssm_b_im, ssm_c_re, ssm_c_im, ssm_log_step, ssm_d, ssm_w_out, kv_norm, w_kv, k_norm, attn_w_q, q_norm, attn_sinks, attn_w_o, loss_target, m_meta_tokens, m_ffn1_norm, m_ffn1_w_gate_up, m_ffn1_w_down, m_mix_norm, m_ffn2_norm, m_ffn2_w_gate_up, m_ffn2_w_down, m_ssm_w_in, m_ssm_lambda_re, m_ssm_lambda_im, m_ssm_b_re, m_ssm_b_im, m_ssm_c_re, m_ssm_c_im, m_ssm_log_step, m_ssm_d, m_ssm_w_out, m_kv_norm, m_w_kv, m_k_norm, m_attn_w_q, m_q_norm, m_attn_sinks, m_attn_w_o, v_meta_tokens, v_ffn1_norm, v_ffn1_w_gate_up, v_ffn1_w_down, v_mix_norm, v_ffn2_norm, v_ffn2_w_gate_up, v_ffn2_w_down, v_ssm_w_in, v_ssm_lambda_re, v_ssm_lambda_im, v_ssm_b_re, v_ssm_b_im, v_ssm_c_re, v_ssm_c_im, v_ssm_log_step, v_ssm_d, v_ssm_w_out, v_kv_norm, v_w_kv, v_k_norm, v_attn_w_q, v_q_norm, v_attn_sinks, v_attn_w_o):
    args = dict(locals())
    W = {n: args[n] for n in WEIGHTS}
    M = {n: args["m_" + n] for n in WEIGHTS}
    V = {n: args["v_" + n] for n in WEIGHTS}
    my_x, my_y, my_c = (lax.axis_index(a) for a in MESH_AXES)
    my_dev = 4 * my_x + 2 * my_y + my_c

    big_names = [n for n, _ in BIG]
    s2d = {n: _shape2d(W[n].shape) for n in big_names}
    col_w = W["meta_tokens"].shape[1]

    grads, summed, small_parts = _local_step(x, loss_target, W, my_c.astype(jnp.int32).reshape(1))
    loss = lax.psum(grads.pop("loss"), MESH_AXES)
    grad_x = grads.pop("x")

    outs = [{}, {}, {}, {}]
    for n in big_names:
        r4 = _adamw("adamw_" + n, W[n].reshape(s2d[n]), M[n].reshape(s2d[n]), V[n].reshape(s2d[n]), summed[n])
        for k in range(4):
            outs[k][n] = r4[k].reshape(W[n].shape)

    small_names = list(SMALL) + [n for n, _ in COLS]
    small_shapes = [grads[n].shape for n in small_names]
    zero_cols = [jnp.zeros(grads[n].shape, F32) for n, _ in COLS]
    packs = lambda d: _pack_small([d[n] for n in SMALL] + zero_cols, PACK_W)
    r4 = _adamw("adamw_small", packs(W), packs(M), packs(V), small_parts)
    gsmall = None
    for k in range(4):
        un = dict(zip(small_names, _unpack_small(r4[k], small_shapes, PACK_W)))
        gsmall = un if k == 0 else gsmall
        outs[k].update({n: un[n] for n in SMALL})
    col_g = [lax.dynamic_slice_in_dim(gsmall[n], my_dev * W[n].shape[1], W[n].shape[1], axis=1) for n, _ in COLS]
    packc = lambda d: _pack_small([d[n] for n, _ in COLS], col_w)
    r4 = _adamw("adamw_cols", packc(W), packc(M), packc(V), _pack_small(col_g, col_w)[None])
    col_shapes = [W[n].shape for n, _ in COLS]
    for k in range(4):
        outs[k].update(dict(zip([n for n, _ in COLS], _unpack_small(r4[k], col_shapes, col_w))))

    res = [[outs[k][n] for n in WEIGHTS] for k in range(4)]
    return (loss, grad_x, *res[0], *res[1], *res[2], *res[3])


def _local_step(x, target, P, c_arr):
    bsz, seq, d = x.shape
    lp = seq + PAD
    rows = bsz * lp
    depth = P["ffn1_norm"].shape[0]
    assert depth == 2
    bf = lambda a: a.astype(BF16)
    row = lambda a: a.reshape(1, -1)

    def shard(n, l=None):
        a = P[n] if l is None else P[n][l]
        return bf(a.reshape(_shape2d(a.shape)))

    rowsharded = lambda g: g.reshape((g.shape[0] * g.shape[1],) + g.shape[2:])
    colsharded = lambda g: _unshard(g, 1)
    col_w = P["meta_tokens"].shape[1]
    g0 = _run_rider("gather_first", _gather_rider(
        [shard("ffn1_w_gate_up", 0), shard("ffn1_w_down", 0), shard("ssm_w_in", 0),
         _pack_small([P["meta_tokens"], P["ssm_d"]], col_w)]))
    ffn_w = {("ffn1", 0): (colsharded(g0[0]), rowsharded(g0[1]))}
    w_in = rowsharded(g0[2])
    meta_full = _unshard(g0[3][:, :N_META], 1)
    dvec = _unshard(g0[3][:, N_META:N_META + 1, :P["ssm_d"].shape[1]], 1)

    pos = (jnp.arange(lp, dtype=F32) - float(META0))[:, None]
    half = HEAD_DIM // 2
    freqs = ROPE_THETA ** (-jnp.arange(0, half, dtype=F32) * 2.0 / HEAD_DIM)
    ang = pos * freqs[None, :]
    cos_t = jnp.tile(jnp.cos(ang), (1, LANES // half))
    sin_t = jnp.tile(jnp.concatenate([-jnp.sin(ang), jnp.sin(ang)], axis=1), (1, LANES // HEAD_DIM))
    gi = jnp.arange(LANES) // HEAD_DIM
    gmat = jnp.where(gi[:, None] == gi[None, :], 1.0 / HEAD_DIM, 0.0).astype(BF16)

    g_n, c_n, p_n = P["ssm_lambda_re"].shape[1], SSM_GROUP, SSM_STATE
    ns = g_n * p_n
    lr = P["ssm_lambda_re"][0].reshape(g_n, 1, p_n)
    li = P["ssm_lambda_im"][0].reshape(g_n, 1, p_n)
    ls = P["ssm_log_step"][0].reshape(g_n, 1, 1)
    brt = P["ssm_b_re"][0].transpose(0, 2, 1)
    bit = P["ssm_b_im"][0].transpose(0, 2, 1)
    ar, ai, bbr, bbi = _s5_params_fwd(lr, li, ls, brt, bit)
    a2 = jnp.concatenate([ar.reshape(1, ns), ai.reshape(1, ns)], axis=0)
    bfull = jnp.concatenate([_blockdiag(bbr), _blockdiag(bbi)], axis=1)
    cre_t = P["ssm_c_re"][0].transpose(0, 2, 1)
    cim_t = P["ssm_c_im"][0].transpose(0, 2, 1)
    cfull = jnp.concatenate([_blockdiag(cre_t), -_blockdiag(cim_t)], axis=0)

    ffn = lambda which, l: (row(P[which + "_norm"][l]),) + ffn_w[which, l]
    mix0, mix1, kvn = row(P["mix_norm"][0]), row(P["mix_norm"][1]), row(P["kv_norm"])
    kgain = jnp.tile(P["k_norm"].reshape(1, HEAD_DIM), (1, KVW // HEAD_DIM))
    qgain = jnp.tile(P["q_norm"].reshape(1, HEAD_DIM), (1, d // HEAD_DIM))
    sinks = P["attn_sinks"].reshape(1, -1)

    h0 = _embed(x, meta_full).reshape(rows, d)
    h1, ab_f1_0, g_wout, g_gu, g_d, g_kv = _ffn_fwd("ffn1_0_fwd", h0, *ffn("ffn1", 0), rider=_gather_rider(
        [shard("ssm_w_out", 0), shard("ffn2_w_gate_up", 0), shard("ffn2_w_down", 0), shard("w_kv")]))
    w_out, w_kv = colsharded(g_wout), rowsharded(g_kv)
    ffn_w["ffn2", 0] = (colsharded(g_gu), rowsharded(g_d))
    u = _proj_fwd("ssm_in_fwd", h1, mix0, w_in)
    y, xs, u_perm = _s5_scan_fwd(u, bf(bfull), bf(cfull), a2, dvec, bsz)
    h2 = _glu_fwd(y, h1, w_out)
    h3, ab_f2_0, g_gu, g_d, g_q, g_o = _ffn_fwd("ffn2_0_fwd", h2, *ffn("ffn2", 0), rider=_gather_rider(
        [shard("ffn1_w_gate_up", 1), shard("ffn1_w_down", 1), shard("attn_w_q", 0), shard("attn_w_o", 0)]))
    w_q, w_o = rowsharded(g_q), rowsharded(g_o)
    ffn_w["ffn1", 1] = (colsharded(g_gu), rowsharded(g_d))
    kv = _proj_fwd("kv_fwd", h3, kvn, w_kv)
    k = _headrope_fwd("k_rope_fwd", kv, KVW, kgain, cos_t, sin_t, gmat, lp)
    h4, ab_f1_1, g_gu, g_d = _ffn_fwd("ffn1_1_fwd", h3, *ffn("ffn1", 1), rider=_gather_rider(
        [shard("ffn2_w_gate_up", 1), shard("ffn2_w_down", 1)]))
    ffn_w["ffn2", 1] = (colsharded(g_gu), rowsharded(g_d))
    q_raw = _proj_fwd("q_fwd", h4, mix1, w_q)
    q = _headrope_fwd("q_rope_fwd", q_raw, d, qgain, cos_t, sin_t, gmat, lp)
    r3 = lambda a: a.reshape(bsz, lp, a.shape[-1])
    o = _attn_fwd(r3(q), r3(k), r3(kv), sinks).reshape(rows, d)
    h5 = _lin_res_fwd("attn_out_fwd", o, w_o, h4)
    h6, ab_f2_1 = _ffn_fwd("ffn2_1_fwd", h5, *ffn("ffn2", 1))
    loss, dh6 = _loss(r3(h6), target)
    dh6 = dh6.reshape(rows, d)

    G = {"loss": loss[0, 0]}

    def ffn_back(name, which, l, h, ab, dout, rider=None):
        g, wgu, wd = ffn(which, l)
        dh, hn, dab, act, dg, *rode = _ffn_bwd(name, h, ab, dout, g, wgu, wd, rider=rider)
        parts = [_shard(_mm_tn(name + "_wgu", hn, dab), 1), _mm_tn_slots(name + "_wd", act, dout, 0.5)]
        return dh, dg, parts, rode

    slots = lambda g: g.reshape((N_DEV, g.shape[0] // N_DEV) + g.shape[1:])
    swap_of = lambda parts: _swap_rider([p.reshape((4, 2) + p.shape[1:]) for p in parts])

    def pair_sums(tag, parts, theirs):
        return [_pair_sum("pair_sum_%s_%d" % (tag, k), p.reshape((4, 2) + p.shape[1:]), t, c_arr)
                for k, (p, t) in enumerate(zip(parts, theirs))]

    dh5, dg_f2_1, parts_a, _ = ffn_back("ffn2_1_bwd", "ffn2", 1, h5, ab_f2_1, dh6)
    do, dw_o, *theirs = _lin_bwd("attn_out_bwd", o, w_o, dh5, rider=swap_of(parts_a))
    sums_a = pair_sums("ffn2_1", parts_a, theirs)
    dq, dk, dv, dsinks = _attn_bwd(r3(q), r3(k), r3(kv), sinks, r3(o), r3(do))
    dq_raw, dqg = _headrope_bwd("q_rope_bwd", q_raw, d, dq.reshape(rows, d), qgain, cos_t, sin_t, gmat, lp)
    dh4, dg_mix1, dw_q = _proj_bwd("q_bwd", h4, mix1, w_q, dq_raw, dh5)
    dh3, dg_f1_1, parts_b, red_a = ffn_back("ffn1_1_bwd", "ffn1", 1, h3, ab_f1_1, dh4, rider=_scatter_rider(sums_a))
    dk_raw, dkg = _headrope_bwd("k_rope_bwd", kv, KVW, dk.reshape(rows, KVW), kgain, cos_t, sin_t, gmat, lp)
    parts_b = parts_b + [slots(dw_q), slots(dw_o)]
    dkv = _concat_cols("dkv_concat", dk_raw, dv.reshape(rows, KVW))
    dh3, dg_kvn, dw_kv, *theirs = _proj_bwd("kv_bwd", h3, kvn, w_kv, dkv, dh3, rider=swap_of(parts_b))
    sums_b = pair_sums("ffn1_1", parts_b, theirs)
    dh2, dg_f2_0, parts_c, red_b = ffn_back("ffn2_0_bwd", "ffn2", 0, h2, ab_f2_0, dh3, rider=_scatter_rider(sums_b))
    parts_c = parts_c + [slots(dw_kv)]
    dy, dw_out, *theirs = _glu_bwd(y, dh2, w_out, rider=swap_of(parts_c))
    sums_c = pair_sums("ffn2_0", parts_c, theirs)
    ctfull = jnp.concatenate([_blockdiag(P["ssm_c_re"][0]), -_blockdiag(P["ssm_c_im"][0])], axis=1)
    btfull = jnp.concatenate([_blockdiag(bbr.transpose(0, 2, 1)), _blockdiag(bbi.transpose(0, 2, 1))], axis=0)
    du, gx, dy_perm, da, dd = _s5_scan_bwd(dy, u, xs, bf(ctfull), bf(btfull), a2, dvec, bsz)
    dbfull = _mm_tn_blockdiag("ssm_db", u_perm, gx, False)
    dcfull = _mm_tn_blockdiag("ssm_dc", xs, dy_perm, True)
    dh1, dg_mix0, dw_in = _proj_bwd("ssm_in_bwd", h1, mix0, w_in, du, dh2)
    dh0, dg_f1_0, parts_d, red_c = ffn_back("ffn1_0_bwd", "ffn1", 0, h0, ab_f1_0, dh1, rider=_scatter_rider(sums_c))
    dbbr = _diagblocks(dbfull[:, :ns], g_n)
    dbbi = _diagblocks(dbfull[:, ns:], g_n)
    dlr, dli, dls, dbrt, dbit = _s5_params_bwd(lr, li, ls, brt, bit, da[:, :ns].reshape(g_n, 1, p_n),
                                               da[:, ns:].reshape(g_n, 1, p_n), dbbr, dbbi)
    dh0 = r3(dh0)
    G["x"] = dh0[:, PAD:, :]
    G["meta_tokens"] = _meta_sum(dh0)
    G["ffn1_norm"] = jnp.concatenate([dg_f1_0, dg_f1_1], axis=0)
    G["ffn2_norm"] = jnp.concatenate([dg_f2_0, dg_f2_1], axis=0)
    G["mix_norm"] = jnp.concatenate([dg_mix0, dg_mix1], axis=0)
    G["ssm_lambda_re"] = dlr.reshape(1, g_n, p_n)
    G["ssm_lambda_im"] = dli.reshape(1, g_n, p_n)
    G["ssm_log_step"] = dls.reshape(1, g_n)
    G["ssm_b_re"] = dbrt.transpose(0, 2, 1)[None]
    G["ssm_b_im"] = dbit.transpose(0, 2, 1)[None]
    G["ssm_c_re"] = _diagblocks(dcfull[:ns], g_n).transpose(0, 2, 1)[None]
    G["ssm_c_im"] = -_diagblocks(dcfull[ns:], g_n).transpose(0, 2, 1)[None]
    G["ssm_d"] = dd
    G["kv_norm"] = dg_kvn.reshape(-1)
    G["k_norm"] = dkg[0, :HEAD_DIM]
    G["q_norm"] = dqg[:, :HEAD_DIM]
    G["attn_sinks"] = dsinks[:, :N_KV_HEADS * Q_PER_KV]

    parts_d = parts_d + [slots(dw_in), dw_out]
    small_pack = _pack_small([G[n] for n in list(SMALL) + [n for n, _ in COLS]], PACK_W)
    *theirs, small_parts = _run_rider("grad_swap_last", _join_riders(swap_of(parts_d), _gather_rider([small_pack])))
    red_d = _run_rider("grad_scatter_last", _scatter_rider(pair_sums("last", parts_d, theirs)))
    both = lambda lo, hi: jnp.concatenate([lo, hi], axis=1)
    summed = {"ffn1_w_gate_up": both(red_d[0], red_b[0]), "ffn1_w_down": both(red_d[1], red_b[1]),
              "ffn2_w_gate_up": both(red_c[0], red_a[0]), "ffn2_w_down": both(red_c[1], red_a[1]),
              "ssm_w_in": red_d[2], "ssm_w_out": red_d[3], "w_kv": red_c[2], "attn_w_q": red_b[2],
              "attn_w_o": red_b[3]}
    return G, summed, small_parts
```

```python
import functools
import math

import jax
import jax.numpy as jnp
from jax import lax
from jax.experimental import pallas as pl
from jax.experimental.pallas import tpu as pltpu

F32 = jnp.float32
BF16 = jnp.bfloat16

N_META = 16
PAD = 128
META0 = PAD - N_META
HEAD_DIM = 64
N_KV_HEADS = 4
Q_PER_KV = 4
SSM_GROUP = 16
SSM_STATE = 64
EPS = 1e-6
NEG_INF = -1e30
ROPE_THETA = 10000.0
ADAM_LR, ADAM_B1, ADAM_B2, ADAM_EPS, ADAM_WD, ADAM_STEP = 0.001, 0.9, 0.999, 1e-08, 0.01, 10
LANES = 128
PACK_W = 1024
VMEM_LIMIT = 56 * 1024 * 1024
MESH_AXES = ("x", "y", "c")
N_DEV = 8


def _cparams(sem=None):
    return pltpu.CompilerParams(dimension_semantics=sem, vmem_limit_bytes=VMEM_LIMIT)


def _row_tile(rows, light=False):
    for tm in ((768,) if light else ()) + (384, 256, 128, 64, 32, 16, 8):
        if rows % tm == 0:
            return tm
    raise ValueError(rows)


STREAM_BUDGET = 32 * 1024 * 1024


def _stream_tile(rows, bytes_per_row):
    for tm in range(rows, 0, -1):
        if rows % tm == 0 and (tm % 16 == 0 or tm == rows) and 2 * tm * bytes_per_row <= STREAM_BUDGET:
            return tm
    raise ValueError(rows)


TN_BUDGET = 52 * 1024 * 1024
TN_MAX_ROWS = 2816


def _tn_tile(rows, a, b, k1, tn):
    sa, sb = a.dtype.itemsize, b.dtype.itemsize
    fits = lambda tm: 2 * tm * (k1 * sa + tn * sb) + 3 * k1 * tn * 4 + tm * k1 * 2 <= TN_BUDGET
    divisors = [tm for tm in range(min(rows, TN_MAX_ROWS), 7, -8) if rows % tm == 0 and fits(tm)]
    good = [tm for tm in divisors if -(-tm // MXU_DIM) * MXU_DIM <= 1.1 * tm]
    if good or divisors:
        return (good or divisors)[0]
    raise ValueError(rows)


def _dot(a, b):
    return jnp.dot(a.astype(BF16), b.astype(BF16), preferred_element_type=F32)


def _dot_nt(a, b):
    return lax.dot_general(a.astype(BF16), b.astype(BF16), (((1,), (1,)), ((), ())), preferred_element_type=F32)


def _dot_tn(a, b):
    return lax.dot_general(a.astype(BF16), b.astype(BF16), (((0,), (0,)), ((), ())), preferred_element_type=F32)


def _rms(x, g):
    rstd = lax.rsqrt(jnp.mean(x * x, axis=-1, keepdims=True) + EPS)
    y = x * rstd
    return y * g, y, rstd


def _rms_bwd(dhn, y, rstd, g):
    dyn = dhn * g
    dx = rstd * (dyn - y * jnp.mean(dyn * y, axis=-1, keepdims=True))
    return dx, jnp.sum(dhn * y, axis=0, keepdims=True)


def _sigmoid(x):
    return 1.0 / (1.0 + jnp.exp(-x))


_GELU_C = math.sqrt(2.0 / math.pi)


def _gelu(y):
    t = jnp.tanh(_GELU_C * (y + 0.044715 * y * y * y))
    return 0.5 * y * (1.0 + t), t


def _gelu_grad(y, t):
    return 0.5 * (1.0 + t) + 0.5 * y * (1.0 - t * t) * _GELU_C * (1.0 + 3.0 * 0.044715 * y * y)


class _Rider:
    def __init__(self, ins, outs, sems, start, mid, finish):
        self.ins, self.outs, self.sems, self.start, self.mid, self.finish = ins, outs, sems, start, mid, finish


def _join_riders(r1, r2):
    ni, no, ns = len(r1.ins), len(r1.outs), len(r1.sems)

    def both(f1, f2):
        def phase(ins, outs, sems):
            if f1 is not None:
                f1(ins[:ni], outs[:no], sems[:ns])
            if f2 is not None:
                f2(ins[ni:], outs[no:], sems[ns:])
        return phase

    mid = both(r1.mid, r2.mid) if (r1.mid is not None or r2.mid is not None) else None
    return _Rider(r1.ins + r2.ins, r1.outs + r2.outs, r1.sems + r2.sems,
                  both(r1.start, r2.start), mid, both(r1.finish, r2.finish))


def _run_rider(name, rider):
    def kern(*refs):
        ni, no = len(rider.ins), len(rider.outs)
        parts = refs[:ni], refs[ni:ni + no], refs[ni + no:]
        rider.start(*parts)
        if rider.mid is not None:
            rider.mid(*parts)
        rider.finish(*parts)

    return pl.pallas_call(
        kern, name=name, out_shape=list(rider.outs), in_specs=[ANY] * len(rider.ins),
        out_specs=[ANY] * len(rider.outs), scratch_shapes=list(rider.sems),
    )(*rider.ins)


def _rowcall(name, body, rows, row_ins, const_ins, row_outs, acc_outs=(), tm=None, row_in_maps=None, rider=None,
             light=False):
    tm = tm or _row_tile(rows, light)
    steps = rows // tm
    in_specs = []
    for k, a in enumerate(row_ins):
        if row_in_maps is not None and row_in_maps[k] is not None:
            in_specs.append(pl.BlockSpec(*row_in_maps[k]))
        else:
            in_specs.append(pl.BlockSpec((tm, a.shape[1]), lambda i: (i, 0)))
    for a in const_ins:
        in_specs.append(pl.BlockSpec(a.shape, lambda i, nd=a.ndim: (0,) * nd, pipeline_mode=pl.Buffered(1)))
    out_shape, out_specs = [], []
    for w, dt in row_outs:
        out_shape.append(jax.ShapeDtypeStruct((rows, w), dt))
        out_specs.append(pl.BlockSpec((tm, w), lambda i: (i, 0)))
    for shp, dt in acc_outs:
        out_shape.append(jax.ShapeDtypeStruct(shp, dt))
        out_specs.append(pl.BlockSpec(shp, lambda i, nd=len(shp): (0,) * nd))

    if rider is None:
        def kern(*refs):
            body(pl.program_id(0), *refs)

        return pl.pallas_call(
            kern, name=name, grid=(steps,), in_specs=in_specs, out_specs=out_specs, out_shape=out_shape,
            compiler_params=_cparams(("arbitrary",)),
        )(*row_ins, *const_ins)

    n_in, n_out = len(in_specs), len(out_specs)
    r_in, r_out = len(rider.ins), len(rider.outs)

    def kern_r(*refs):
        step = pl.program_id(0)
        ins, rins = refs[:n_in], refs[n_in:n_in + r_in]
        outs = refs[n_in + r_in:n_in + r_in + n_out]
        routs = refs[n_in + r_in + n_out:n_in + r_in + n_out + r_out]
        sems = refs[n_in + r_in + n_out + r_out:]

        @pl.when(step == 0)
        def _():
            rider.start(rins, routs, sems)

        if rider.mid is not None:
            @pl.when(step == (3 * steps) // 4)
            def _():
                rider.mid(rins, routs, sems)

        body(step, *ins, *outs)

        @pl.when(step == steps - 1)
        def _():
            rider.finish(rins, routs, sems)

    return pl.pallas_call(
        kern_r, name=name, grid=(steps,), in_specs=in_specs + [ANY] * r_in, out_specs=out_specs + [ANY] * r_out,
        out_shape=out_shape + list(rider.outs), scratch_shapes=list(rider.sems),
        compiler_params=_cparams(("arbitrary",)),
    )(*row_ins, *const_ins, *rider.ins)


def _acc(step, ref, val):
    @pl.when(step == 0)
    def _():
        ref[...] = val

    @pl.when(step != 0)
    def _():
        ref[...] += val


def _embed(x, meta):
    bsz, seq, d = x.shape

    def kern(x_ref, m_ref, o_ref, head_ref, sems):
        head_ref[0:META0, :] = jnp.zeros((META0, d), F32)
        head_ref[META0:PAD, :] = m_ref[...]
        copies = []
        for b in range(bsz):
            copies.append(pltpu.make_async_copy(x_ref.at[b], o_ref.at[b, pl.ds(PAD, seq)], sems.at[2 * b]))
            copies.append(pltpu.make_async_copy(head_ref, o_ref.at[b, pl.ds(0, PAD)], sems.at[2 * b + 1]))
        for cp in copies:
            cp.start()
        for cp in copies:
            cp.wait()

    return pl.pallas_call(
        kern, name="embed", in_specs=[ANY, pl.BlockSpec(memory_space=pltpu.VMEM)], out_specs=ANY,
        out_shape=jax.ShapeDtypeStruct((bsz, seq + PAD, d), F32),
        scratch_shapes=[pltpu.VMEM((PAD, d), F32), pltpu.SemaphoreType.DMA((2 * bsz,))],
    )(x, meta)


def _loss(h6, target):
    bsz, lp, d = h6.shape
    nb = lp // PAD

    def kern(h_ref, t_ref, l_ref, d_ref):
        b, i = pl.program_id(0), pl.program_id(1)

        @pl.when((b == 0) & (i == 0))
        def _():
            l_ref[...] = jnp.zeros_like(l_ref)

        @pl.when(i == 0)
        def _():
            d_ref[0] = jnp.zeros((PAD, d), F32)

        @pl.when(i != 0)
        def _():
            e = h_ref[0] - t_ref[0]
            d_ref[0] = e * (1.0 / d)
            l_ref[...] += 0.5 * jnp.sum(jnp.mean(e * e, axis=-1, keepdims=True))

    return pl.pallas_call(
        kern, name="loss", grid=(bsz, nb),
        in_specs=[pl.BlockSpec((1, PAD, d), lambda b, i: (b, i, 0)),
                  pl.BlockSpec((1, PAD, d), lambda b, i: (b, jnp.maximum(i - 1, 0), 0))],
        out_specs=[pl.BlockSpec((1, LANES), lambda b, i: (0, 0)),
                   pl.BlockSpec((1, PAD, d), lambda b, i: (b, i, 0))],
        out_shape=[jax.ShapeDtypeStruct((1, LANES), F32), jax.ShapeDtypeStruct((bsz, lp, d), F32)],
        compiler_params=_cparams(("arbitrary", "arbitrary")),
    )(h6, target)


def _meta_sum(dh0):
    bsz, lp, d = dh0.shape

    def kern(d_ref, o_ref):
        _acc(pl.program_id(0), o_ref, d_ref[0, META0:PAD, :])

    return pl.pallas_call(
        kern, name="meta_sum", grid=(bsz,),
        in_specs=[pl.BlockSpec((1, PAD, d), lambda b: (b, 0, 0))],
        out_specs=pl.BlockSpec((N_META, d), lambda b: (0, 0)),
        out_shape=jax.ShapeDtypeStruct((N_META, d), F32),
        compiler_params=_cparams(("arbitrary",)),
    )(dh0)


MXU_DIM = 256


def _ffn_chunks(f):
    unit = MXU_DIM if f % MXU_DIM == 0 else LANES
    assert f % unit == 0
    first = (f // unit + 1) // 2 * unit
    return [(0, first), (first, f)] if first < f else [(0, f)]


def _ffn_fwd(name, h, g, wgu, wd, rider=None):
    rows, d = h.shape
    f = wd.shape[0]
    chunks = _ffn_chunks(f)

    def body(step, h_ref, g_ref, wgu_ref, wd_ref, o_ref, ab_ref):
        hx = h_ref[...]
        hb = _rms(hx, g_ref[...])[0].astype(BF16)
        acc = jnp.zeros(hx.shape, F32)
        for lo, hi in chunks:
            ga, ua = slice(lo, hi), slice(f + lo, f + hi)
            a = _dot(hb, wgu_ref[:, ga])
            b = _dot(hb, wgu_ref[:, ua])
            ab_ref[:, ga] = a.astype(BF16)
            ab_ref[:, ua] = b.astype(BF16)
            acc = acc + _dot(a * _sigmoid(a) * b, wd_ref[ga, :])
        o_ref[...] = hx + 0.5 * acc

    return _rowcall(name, body, rows, [h], [g, wgu, wd], [(d, F32), (2 * f, BF16)], rider=rider)


def _ffn_bwd(name, h, ab, dout, g, wgu, wd, rider=None):
    rows, d = h.shape
    f = wd.shape[0]
    chunks = _ffn_chunks(f)

    def body(step, h_ref, ab_ref, do_ref, g_ref, wgu_ref, wd_ref, dh_ref, hn_ref, dab_ref, act_ref, dg_ref):
        hx, dout_x, gx = h_ref[...], do_ref[...], g_ref[...]
        hn, y, rstd = _rms(hx, gx)
        hn_ref[...] = hn.astype(BF16)
        dhalf = (0.5 * dout_x).astype(BF16)
        dhn = jnp.zeros(hx.shape, F32)
        for lo, hi in chunks:
            ga, ua = slice(lo, hi), slice(f + lo, f + hi)
            a = ab_ref[:, ga].astype(F32)
            b = ab_ref[:, ua].astype(F32)
            s = _sigmoid(a)
            silu = a * s
            act_ref[:, ga] = (silu * b).astype(BF16)
            dact = _dot_nt(dhalf, wd_ref[ga, :])
            da = (dact * b * (s + silu * (1.0 - s))).astype(BF16)
            db = (dact * silu).astype(BF16)
            dab_ref[:, ga] = da
            dab_ref[:, ua] = db
            dhn = dhn + _dot_nt(da, wgu_ref[:, ga]) + _dot_nt(db, wgu_ref[:, ua])
        dx, dg = _rms_bwd(dhn, y, rstd, gx)
        dh_ref[...] = dout_x + dx
        _acc(step, dg_ref, dg)

    return _rowcall(name, body, rows, [h, ab, dout], [g, wgu, wd],
                    [(d, F32), (d, BF16), (2 * f, BF16), (f, BF16)], [((1, d), F32)], rider=rider)


def _mm_tn(name, a, b, scale=1.0):
    rows, k1 = a.shape
    k2 = b.shape[1]
    tn = k2
    for cand in (512, 704, 1408, 1024):
        if k2 % cand == 0 and k1 * cand * 4 <= 6 * 1024 * 1024:
            tn = cand
    tm = _tn_tile(rows, a, b, k1, tn)
    steps = rows // tm

    def kern(a_ref, b_ref, o_ref):
        bx = b_ref[...]
        if scale != 1.0:
            bx = bx * scale
        _acc(pl.program_id(1), o_ref, _dot_tn(a_ref[...], bx))

    return pl.pallas_call(
        kern, name=name, grid=(k2 // tn, steps),
        in_specs=[pl.BlockSpec((tm, k1), lambda j, i: (i, 0)), pl.BlockSpec((tm, tn), lambda j, i: (i, j))],
        out_specs=pl.BlockSpec((k1, tn), lambda j, i: (0, j)),
        out_shape=jax.ShapeDtypeStruct((k1, k2), F32),
        compiler_params=_cparams(("arbitrary", "arbitrary")),
    )(a, b)


def _mm_tn_blockdiag(name, a, b, states_first):
    rows = a.shape[0]
    ka, kb = a.shape[1], b.shape[1]
    qa, qb = (ka // 4, kb // 2) if states_first else (ka // 2, kb // 4)
    tm = _tn_tile(rows, a, b, qa, qb)
    steps = rows // tm
    wide = lambda part, k: 2 * part + k
    amap = (lambda p, k, i: (i, wide(p, k))) if states_first else (lambda p, k, i: (i, k))
    bmap = (lambda p, k, i: (i, k)) if states_first else (lambda p, k, i: (i, wide(p, k)))
    omap = (lambda p, k, i: (wide(p, k), k)) if states_first else (lambda p, k, i: (k, wide(p, k)))

    def kern(a_ref, b_ref, o_ref):
        _acc(pl.program_id(2), o_ref, _dot_tn(a_ref[...], b_ref[...]))

    return pl.pallas_call(
        kern, name=name, grid=(2, 2, steps),
        in_specs=[pl.BlockSpec((tm, qa), amap), pl.BlockSpec((tm, qb), bmap)],
        out_specs=pl.BlockSpec((qa, qb), omap), out_shape=jax.ShapeDtypeStruct((ka, kb), F32),
        compiler_params=_cparams(("arbitrary", "arbitrary", "arbitrary")),
    )(a, b)


def _mm_tn_slots(name, a, b, scale):
    rows, k1 = a.shape
    k2 = b.shape[1]
    tn = 512 if k2 % 512 == 0 else k2
    sr = k1 // N_DEV
    tm = _tn_tile(rows, a, b, k1, tn)
    steps = rows // tm

    def kern(a_ref, b_ref, o_ref):
        bx = b_ref[...]
        if scale != 1.0:
            bx = bx * scale
        res = _dot_tn(a_ref[...], bx)
        step = pl.program_id(1)
        for s in range(N_DEV):
            _acc(step, o_ref.at[s], res[s * sr:(s + 1) * sr])

    return pl.pallas_call(
        kern, name=name, grid=(k2 // tn, steps),
        in_specs=[pl.BlockSpec((tm, k1), lambda j, i: (i, 0)), pl.BlockSpec((tm, tn), lambda j, i: (i, j))],
        out_specs=pl.BlockSpec((N_DEV, sr, tn), lambda j, i: (0, 0, j)),
        out_shape=jax.ShapeDtypeStruct((N_DEV, sr, k2), F32),
        compiler_params=_cparams(("arbitrary", "arbitrary")),
    )(a, b)


def _proj_fwd(name, h, g, w):
    rows = h.shape[0]

    def body(step, h_ref, g_ref, w_ref, o_ref):
        o_ref[...] = _dot(_rms(h_ref[...], g_ref[...])[0], w_ref[...])

    return _rowcall(name, body, rows, [h], [g, w], [(w.shape[1], F32)], light=True)[0]


def _proj_bwd(name, h, g, w, dy, dres, rider=None):
    rows, d = h.shape

    def body(step, h_ref, dy_ref, dr_ref, g_ref, w_ref, dh_ref, dg_ref, dw_ref):
        gx = g_ref[...]
        hn, y, rstd = _rms(h_ref[...], gx)
        dyx = dy_ref[...]
        dx, dg = _rms_bwd(_dot_nt(dyx, w_ref[...]), y, rstd, gx)
        dh_ref[...] = dr_ref[...] + dx
        _acc(step, dg_ref, dg)
        _acc(step, dw_ref, _dot_tn(hn, dyx))

    return _rowcall(name, body, rows, [h, dy, dres], [g, w], [(d, F32)], [((1, d), F32), (w.shape, F32)],
                    rider=rider, light=True)


def _lin_res_fwd(name, a, w, res):
    rows = a.shape[0]

    def body(step, a_ref, r_ref, w_ref, o_ref):
        o_ref[...] = r_ref[...] + _dot(a_ref[...], w_ref[...])

    return _rowcall(name, body, rows, [a, res], [w], [(w.shape[1], F32)], light=True)[0]


def _lin_bwd(name, a, w, dy, rider=None):
    rows, k = a.shape

    def body(step, a_ref, dy_ref, w_ref, da_ref, dw_ref):
        dyx = dy_ref[...]
        da_ref[...] = _dot_nt(dyx, w_ref[...])
        _acc(step, dw_ref, _dot_tn(a_ref[...], dyx))

    return _rowcall(name, body, rows, [a, dy], [w], [(k, F32)], [(w.shape, F32)], rider=rider, light=True)


def _s5_param_fn(lr, li, ls, brt, bit):
    step = jnp.exp(ls)
    mag = jnp.exp(lr * step)
    ar = mag * jnp.cos(li * step)
    ai = mag * jnp.sin(li * step)
    den = lr * lr + li * li
    nr, ni = ar - 1.0, ai
    cr = (nr * lr + ni * li) / den
    ci = (ni * lr - nr * li) / den
    return ar, ai, cr * brt - ci * bit, cr * bit + ci * brt


def _s5_params_fwd(lr, li, ls, brt, bit):
    def kern(lr_ref, li_ref, ls_ref, br_ref, bi_ref, ar_ref, ai_ref, bbr_ref, bbi_ref):
        ar, ai, bbr, bbi = _s5_param_fn(lr_ref[...], li_ref[...], ls_ref[...], br_ref[...], bi_ref[...])
        ar_ref[...], ai_ref[...], bbr_ref[...], bbi_ref[...] = ar, ai, bbr, bbi

    sd = jax.ShapeDtypeStruct
    return pl.pallas_call(
        kern, name="s5_params_fwd",
        out_shape=[sd(lr.shape, F32), sd(lr.shape, F32), sd(brt.shape, F32), sd(brt.shape, F32)],
    )(lr, li, ls, brt, bit)


def _s5_params_bwd(lr, li, ls, brt, bit, dar, dai, dbbr, dbbi):
    def kern(lr_ref, li_ref, ls_ref, br_ref, bi_ref, dar_ref, dai_ref, dbbr_ref, dbbi_ref,
             dlr_ref, dli_ref, dls_ref, dbr_ref, dbi_ref):
        _, vjp = jax.vjp(_s5_param_fn, lr_ref[...], li_ref[...], ls_ref[...], br_ref[...], bi_ref[...])
        dlr, dli, dls, dbr, dbi = vjp((dar_ref[...], dai_ref[...], dbbr_ref[...], dbbi_ref[...]))
        dlr_ref[...], dli_ref[...], dls_ref[...], dbr_ref[...], dbi_ref[...] = dlr, dli, dls, dbr, dbi

    sd = jax.ShapeDtypeStruct
    return pl.pallas_call(
        kern, name="s5_params_bwd",
        out_shape=[sd(lr.shape, F32), sd(lr.shape, F32), sd(ls.shape, F32), sd(brt.shape, F32), sd(brt.shape, F32)],
    )(lr, li, ls, brt, bit, dar, dai, dbbr, dbbi)


SCAN_LW = 512


SCAN_SEGS = 8
SCAN_UNROLL = 8


def _cmul(xr, xi, yr, yi):
    return xr * yr - xi * yi, xr * yi + xi * yr


def _scan_tables(a_ref, tab_ref, conj, seg_len):
    ns = a_ref.shape[1]
    ar = jnp.broadcast_to(a_ref[0:1, :], (8, ns))
    ai = jnp.broadcast_to(a_ref[1:2, :], (8, ns))
    if conj:
        ai = -ai
    big, base, e = None, (ar, ai), seg_len
    while e:
        if e & 1:
            big = base if big is None else _cmul(*big, *base)
        base = _cmul(*base, *base)
        e >>= 1
    big2 = _cmul(*big, *big)
    big4 = _cmul(*big2, *big2)
    for k, v in enumerate((ar, ai) + big + big2 + big4):
        tab_ref[k] = v


def _scan_block(x_ref, tab_ref, carry_ref, t_rows, ns, reverse):
    sl = t_rows // SCAN_SEGS
    assert sl % SCAN_UNROLL == 0
    row = lax.broadcasted_iota(jnp.int32, (8, SCAN_LW), 0)
    zero = jnp.zeros((8, SCAN_LW), F32)
    for lc in range(ns // SCAN_LW):
        lre = pl.ds(lc * SCAN_LW, SCAN_LW)
        lim = pl.ds(ns + lc * SCAN_LW, SCAN_LW)
        ar, ai = tab_ref[0, :, lre], tab_ref[1, :, lre]

        def rows_of(k, u):
            j = k * SCAN_UNROLL + u
            return pl.ds(pl.multiple_of(((sl - 1 - j) if reverse else j) * SCAN_SEGS, SCAN_SEGS), SCAN_SEGS)

        def local(k, s, lre=lre, lim=lim, ar=ar, ai=ai):
            sr, si = s
            for u in range(SCAN_UNROLL):
                rows = rows_of(k, u)
                tr, ti = _cmul(ar, ai, sr, si)
                sr, si = x_ref[rows, lre] + tr, x_ref[rows, lim] + ti
                x_ref[rows, lre], x_ref[rows, lim] = sr, si
            return sr, si

        er, ei = lax.fori_loop(0, sl // SCAN_UNROLL, local, (zero, zero))
        if reverse:
            cr = jnp.where(row == 7, carry_ref[:, lre], pltpu.roll(er, 7, 0))
            ci = jnp.where(row == 7, carry_ref[:, lim], pltpu.roll(ei, 7, 0))
        else:
            cr = jnp.where(row == 0, carry_ref[:, lre], pltpu.roll(er, 1, 0))
            ci = jnp.where(row == 0, carry_ref[:, lim], pltpu.roll(ei, 1, 0))
        for lvl, dsh in enumerate((1, 2, 4)):
            pr, pi = tab_ref[2 + 2 * lvl, :, lre], tab_ref[3 + 2 * lvl, :, lre]
            if reverse:
                keep, shift = row < 8 - dsh, 8 - dsh
            else:
                keep, shift = row >= dsh, dsh
            sr = jnp.where(keep, pltpu.roll(cr, shift, 0), 0.0)
            si = jnp.where(keep, pltpu.roll(ci, shift, 0), 0.0)
            tr, ti = _cmul(pr, pi, sr, si)
            cr, ci = cr + tr, ci + ti
        tr, ti = _cmul(tab_ref[2, :, lre], tab_ref[3, :, lre], cr, ci)
        edge = 0 if reverse else 7
        carry_ref[:, lre] = jnp.broadcast_to((er + tr)[edge:edge + 1, :], (8, SCAN_LW))
        carry_ref[:, lim] = jnp.broadcast_to((ei + ti)[edge:edge + 1, :], (8, SCAN_LW))

        def fix(k, t, lre=lre, lim=lim, ar=ar, ai=ai):
            tr, ti = t
            for u in range(SCAN_UNROLL):
                rows = rows_of(k, u)
                tr, ti = _cmul(ar, ai, tr, ti)
                x_ref[rows, lre] = x_ref[rows, lre] + tr
                x_ref[rows, lim] = x_ref[rows, lim] + ti
            return tr, ti

        lax.fori_loop(0, sl // SCAN_UNROLL, fix, (cr, ci))


def _bd_expand(u, w_ref, x_ref, ns):
    hh, sh = u.shape[1] // 2, ns // 2
    ub = u.astype(BF16)
    for part in range(2):
        for k in range(2):
            cols = slice(part * ns + k * sh, part * ns + (k + 1) * sh)
            x_ref[:, cols] = jnp.dot(ub[:, k * hh:(k + 1) * hh], w_ref[k * hh:(k + 1) * hh, cols],
                                     preferred_element_type=F32)


def _bd_contract(x_ref, w_ref, ns):
    hh, sh = w_ref.shape[1] // 2, ns // 2
    halves = []
    for k in range(2):
        acc = None
        for part in range(2):
            rows = slice(part * ns + k * sh, part * ns + (k + 1) * sh)
            t = jnp.dot(x_ref[:, rows].astype(BF16), w_ref[rows, k * hh:(k + 1) * hh], preferred_element_type=F32)
            acc = t if acc is None else acc + t
        halves.append(acc)
    return jnp.concatenate(halves, axis=1)


def _scan_rows(lp):
    for t in (384, 256, 128):
        if lp % t == 0:
            return t
    raise ValueError(lp)


def _seg_perm(t_rows):
    r = jnp.arange(t_rows)
    src = (r % SCAN_SEGS) * (t_rows // SCAN_SEGS) + r // SCAN_SEGS
    p = (src[:, None] == r[None, :]).astype(BF16)
    return p, p.T


def _permute_rows(p_ref, v):
    return jnp.dot(p_ref[...], v.astype(BF16), preferred_element_type=F32)


def _unpermute_rows(pt_ref, v):
    hi = v.astype(BF16)
    lo = (v - hi.astype(F32)).astype(BF16)
    pt = pt_ref[...]
    return jnp.dot(pt, hi, preferred_element_type=F32) + jnp.dot(pt, lo, preferred_element_type=F32)


def _s5_scan_fwd(u, bfull, cfull, a2, dvec, bsz):
    rows, hw = u.shape
    ns = a2.shape[1]
    lp = rows // bsz
    t_rows = _scan_rows(lp)
    nc = lp // t_rows
    pmat, pmat_t = _seg_perm(t_rows)

    def kern(u_ref, b_ref, c_ref, a_ref, d_ref, p_ref, pt_ref, y_ref, x_ref, up_ref, tab_ref, carry_ref):
        c = pl.program_id(1)

        @pl.when((pl.program_id(0) == 0) & (c == 0))
        def _():
            _scan_tables(a_ref, tab_ref, False, t_rows // SCAN_SEGS)

        @pl.when(c == 0)
        def _():
            carry_ref[...] = jnp.zeros_like(carry_ref)

        ux = u_ref[...]
        up = _permute_rows(p_ref, ux)
        up_ref[...] = up.astype(BF16)
        _bd_expand(up, b_ref, x_ref, ns)
        _scan_block(x_ref, tab_ref, carry_ref, t_rows, ns, reverse=False)
        y_ref[...] = _unpermute_rows(pt_ref, _bd_contract(x_ref, c_ref, ns)) + d_ref[...] * ux

    const = lambda shp: pl.BlockSpec(shp, lambda b, c: (0,) * len(shp), pipeline_mode=pl.Buffered(1))
    blk = lambda b, c: (b * nc + c, 0)
    return pl.pallas_call(
        kern, name="s5_scan_fwd", grid=(bsz, nc),
        in_specs=[pl.BlockSpec((t_rows, hw), blk), const(bfull.shape), const(cfull.shape), const(a2.shape),
                  const(dvec.shape), const(pmat.shape), const(pmat.shape)],
        out_specs=[pl.BlockSpec((t_rows, hw), blk), pl.BlockSpec((t_rows, 2 * ns), blk),
                   pl.BlockSpec((t_rows, hw), blk)],
        out_shape=[jax.ShapeDtypeStruct((rows, hw), F32), jax.ShapeDtypeStruct((rows, 2 * ns), F32),
                   jax.ShapeDtypeStruct((rows, hw), BF16)],
        scratch_shapes=[pltpu.VMEM((8, 8, ns), F32), pltpu.VMEM((8, 2 * ns), F32)],
        compiler_params=_cparams(("arbitrary", "arbitrary")),
    )(u, bfull, cfull, a2, dvec, pmat, pmat_t)


def _s5_scan_bwd(dy, u, xs, ctfull, btfull, a2, dvec, bsz):
    rows, hw = u.shape
    ns = a2.shape[1]
    lp = rows // bsz
    t_rows = _scan_rows(lp)
    nc = lp // t_rows
    blk = lambda b, c: (b * nc + (nc - 1 - c), 0)
    pmat, pmat_t = _seg_perm(t_rows)

    def prev8(b, c):
        first = (b * nc + (nc - 1 - c)) * (t_rows // 8)
        return (jnp.maximum(first - 1, 0), 0)

    def kern(dy_ref, u_ref, x_ref, xp_ref, ct_ref, bt_ref, a_ref, d_ref, p_ref, pt_ref,
             du_ref, gx_ref, dyp_ref, da_ref, dd_ref, tab_ref, carry_ref):
        b, c = pl.program_id(0), pl.program_id(1)
        first = (b == 0) & (c == 0)

        @pl.when(first)
        def _():
            _scan_tables(a_ref, tab_ref, True, t_rows // SCAN_SEGS)

        @pl.when(c == 0)
        def _():
            carry_ref[...] = jnp.zeros_like(carry_ref)

        dyx, ux = dy_ref[...], u_ref[...]
        dyp = _permute_rows(p_ref, dyx)
        dyp_ref[...] = dyp.astype(BF16)
        _bd_expand(dyp, ct_ref, gx_ref, ns)
        _scan_block(gx_ref, tab_ref, carry_ref, t_rows, ns, reverse=True)
        gx = gx_ref[...]
        du_ref[...] = _unpermute_rows(pt_ref, _bd_contract(gx_ref, bt_ref, ns)) + d_ref[...] * dyx
        seq_start = c == nc - 1
        row8 = lax.broadcasted_iota(jnp.int32, (8, 1), 0)
        head = pltpu.roll(x_ref[t_rows - 8:t_rows, :], 1, 0)
        head = jnp.where(row8 == 0, jnp.where(seq_start, 0.0, xp_ref[7:8, :]), head)
        xprev = jnp.concatenate([head, x_ref[0:t_rows - 8, :]], axis=0)
        xr, xi, gr, gi = xprev[:, :ns], xprev[:, ns:], gx[:, :ns], gx[:, ns:]
        da = jnp.concatenate([jnp.sum(xr * gr + xi * gi, axis=0, keepdims=True),
                              jnp.sum(xr * gi - xi * gr, axis=0, keepdims=True)], axis=1)
        dd = jnp.sum(dyx * ux, axis=0, keepdims=True)

        @pl.when(first)
        def _():
            da_ref[...] = da
            dd_ref[...] = dd

        @pl.when(jnp.logical_not(first))
        def _():
            da_ref[...] += da
            dd_ref[...] += dd

    const = lambda shp: pl.BlockSpec(shp, lambda b, c: (0,) * len(shp), pipeline_mode=pl.Buffered(1))
    return pl.pallas_call(
        kern, name="s5_scan_bwd", grid=(bsz, nc),
        in_specs=[pl.BlockSpec((t_rows, hw), blk), pl.BlockSpec((t_rows, hw), blk),
                  pl.BlockSpec((t_rows, 2 * ns), blk), pl.BlockSpec((8, 2 * ns), prev8),
                  const(ctfull.shape), const(btfull.shape), const(a2.shape), const(dvec.shape),
                  const(pmat.shape), const(pmat.shape)],
        out_specs=[pl.BlockSpec((t_rows, hw), blk), pl.BlockSpec((t_rows, 2 * ns), blk),
                   pl.BlockSpec((t_rows, hw), blk),
                   pl.BlockSpec((1, 2 * ns), lambda b, c: (0, 0)), pl.BlockSpec((1, hw), lambda b, c: (0, 0))],
        out_shape=[jax.ShapeDtypeStruct((rows, hw), F32), jax.ShapeDtypeStruct((rows, 2 * ns), F32),
                   jax.ShapeDtypeStruct((rows, hw), BF16),
                   jax.ShapeDtypeStruct((1, 2 * ns), F32), jax.ShapeDtypeStruct((1, hw), F32)],
        scratch_shapes=[pltpu.VMEM((8, 8, ns), F32), pltpu.VMEM((8, 2 * ns), F32)],
        compiler_params=_cparams(("arbitrary", "arbitrary")),
    )(dy, u, xs, xs, ctfull, btfull, a2, dvec, pmat, pmat_t)


def _glu_fwd(y, h1, wout):
    rows, d = h1.shape

    def body(step, y_ref, h_ref, w_ref, o_ref):
        z = _dot(_gelu(y_ref[...])[0], w_ref[...])
        o_ref[...] = h_ref[...] + z[:, :d] * _sigmoid(z[:, d:])

    return _rowcall("glu_fwd", body, rows, [y, h1], [wout], [(d, F32)], light=True)[0]


def _glu_bwd(y, dh2, wout, rider=None):
    rows, d = dh2.shape
    hw = y.shape[1]

    def body(step, y_ref, dh_ref, w_ref, dy_ref, dw_ref):
        yx, dh = y_ref[...], dh_ref[...]
        gl, t = _gelu(yx)
        z = _dot(gl, w_ref[...])
        za, sg = z[:, :d], _sigmoid(z[:, d:])
        dza = dh * sg
        dzg = dh * za * sg * (1.0 - sg)
        dgl = _dot_nt(dza, w_ref[:, :d]) + _dot_nt(dzg, w_ref[:, d:])
        dy_ref[...] = dgl * _gelu_grad(yx, t)
        for half, dz in enumerate((dza, dzg)):
            dw = _dot_tn(gl, dz)
            for s in range(N_DEV // 2):
                _acc(step, dw_ref.at[half * (N_DEV // 2) + s], dw[:, s * cw:(s + 1) * cw])

    cw = 2 * d // N_DEV
    return _rowcall("glu_bwd", body, rows, [y, dh2], [wout], [(hw, F32)], [((N_DEV, hw, cw), F32)], rider=rider,
                    light=True)


def _gmean64(x2, gmat):
    hi = x2.astype(BF16)
    r1 = x2 - hi.astype(F32)
    mid = r1.astype(BF16)
    lo = (r1 - mid.astype(F32)).astype(BF16)
    outs = []
    for j in range(x2.shape[1] // LANES):
        sl = slice(j * LANES, (j + 1) * LANES)
        f = lambda p: jnp.dot(p[:, sl], gmat, preferred_element_type=F32)
        outs.append(f(hi) + f(mid) + f(lo))
    return outs[0] if len(outs) == 1 else jnp.concatenate(outs, axis=1)


def _swap32(x):
    w = x.shape[1]
    lane = lax.broadcasted_iota(jnp.int32, (1, w), 1)
    return jnp.where((lane & 32) == 0, pltpu.roll(x, w - 32, 1), pltpu.roll(x, 32, 1))


def _tile_lanes(t, w):
    reps = w // t.shape[1]
    return t if reps == 1 else jnp.concatenate([t] * reps, axis=1)


def _headrope_fwd(name, raw, w, gain, cos, sin, gmat, lp):
    rows = raw.shape[0]
    tm = _row_tile(lp)
    per = lp // tm

    def body(step, x_ref, c_ref, s_ref, g_ref, gm_ref, o_ref):
        x = x_ref[...]
        rstd = lax.rsqrt(_gmean64(x * x, gm_ref[...]) + EPS)
        z = x * rstd * g_ref[...]
        o_ref[...] = z * _tile_lanes(c_ref[...], w) + _swap32(z) * _tile_lanes(s_ref[...], w)

    maps = [((tm, w), lambda i: (i, 0)), ((tm, LANES), lambda i: (i % per, 0)), ((tm, LANES), lambda i: (i % per, 0))]
    return _rowcall(name, body, rows, [raw, cos, sin], [gain, gmat], [(w, F32)], tm=tm, row_in_maps=maps)[0]


def _headrope_bwd(name, raw, w, dout, gain, cos, sin, gmat, lp):
    rows = raw.shape[0]
    tm = _row_tile(lp)
    per = lp // tm

    def body(step, x_ref, do_ref, c_ref, s_ref, g_ref, gm_ref, dx_ref, dg_ref):
        x, dout_x, gx, gm = x_ref[...], do_ref[...], g_ref[...], gm_ref[...]
        rstd = lax.rsqrt(_gmean64(x * x, gm) + EPS)
        yn = x * rstd
        dz = dout_x * _tile_lanes(c_ref[...], w) + _swap32(dout_x * _tile_lanes(s_ref[...], w))
        dyn = dz * gx
        dx_ref[...] = rstd * (dyn - yn * _gmean64(dyn * yn, gm))
        dg = jnp.sum(dz * yn, axis=0, keepdims=True)
        sh = w // 2
        while sh >= HEAD_DIM:
            dg = dg + pltpu.roll(dg, sh, 1)
            sh //= 2
        _acc(step, dg_ref, dg)

    maps = [((tm, w), lambda i: (i, 0)), None, ((tm, LANES), lambda i: (i % per, 0)), ((tm, LANES), lambda i: (i % per, 0))]
    return _rowcall(name, body, rows, [raw, dout, cos, sin], [gain, gmat], [(w, F32)], [((1, w), F32)],
                    tm=tm, row_in_maps=maps)


KVW = N_KV_HEADS * HEAD_DIM
QB = 128


def _fold4(x):
    y = x + pltpu.roll(x, 128, 1)
    return y + pltpu.roll(y, 64, 1)


ATTN_SCALE = HEAD_DIM ** -0.5


def _attn_masks(i):
    k0j = lax.broadcasted_iota(jnp.int32, (Q_PER_KV * QB, QB), 1)
    qi = lax.broadcasted_iota(jnp.int32, (Q_PER_KV * QB, 2 * QB), 0) % QB
    kj = lax.broadcasted_iota(jnp.int32, (Q_PER_KV * QB, 2 * QB), 1)
    in_prev = (kj < QB) & (kj > qi) & (i >= 2)
    in_cur = (kj >= QB) & (kj - QB <= qi)
    return k0j >= META0, in_prev | in_cur


def _attn_scores(i, q_ref, k0_ref, kp_ref, kc_ref, sink_ref, h):
    masks = _attn_masks(i)
    lane = lax.broadcasted_iota(jnp.int32, (1, KVW), 1) // HEAD_DIM
    qh = q_ref[:, h * KVW:(h + 1) * KVW]
    qs = jnp.concatenate([jnp.where(lane == g, qh, 0.0) for g in range(Q_PER_KV)], axis=0).astype(BF16)
    hsel = lane == h
    kx = _expand_kv((k0_ref, kp_ref, kc_ref), hsel)
    s0 = jnp.where(masks[0], _dot_nt(qs, kx[0]) * ATTN_SCALE, NEG_INF)
    sb = jnp.where(masks[1], _dot_nt(qs, kx[1]) * ATTN_SCALE, NEG_INF)
    rowg = lax.broadcasted_iota(jnp.int32, (Q_PER_KV * QB, 1), 0) // QB
    sink = jnp.zeros((Q_PER_KV * QB, 1), F32)
    for g in range(Q_PER_KV):
        sink = jnp.where(rowg == g, sink_ref[0, h * Q_PER_KV + g], sink)
    m = jnp.maximum(jnp.maximum(jnp.max(s0, axis=1, keepdims=True), jnp.max(sb, axis=1, keepdims=True)), sink)
    p0, pb, ps = jnp.exp(s0 - m), jnp.exp(sb - m), jnp.exp(sink - m)
    den = jnp.sum(p0, axis=1, keepdims=True) + jnp.sum(pb, axis=1, keepdims=True) + ps
    return qs, kx, (p0, pb), ps, den, lane, hsel


def _expand_kv(refs, hsel):
    x0, xp, xc = [_fold4(jnp.where(hsel, r[...], 0.0)).astype(BF16) for r in refs]
    return [x0, jnp.concatenate([xp, xc], axis=0)]


def _unstack(x, lane):
    out = jnp.where(lane == 0, x[0:QB], 0.0)
    for g in range(1, Q_PER_KV):
        out = out + jnp.where(lane == g, x[g * QB:(g + 1) * QB], 0.0)
    return out


def _attn_specs(nb, d):
    qspec = pl.BlockSpec((None, QB, d), lambda b, i: (b, i, 0))
    k0 = pl.BlockSpec((None, QB, KVW), lambda b, i: (b, 0, 0))
    kp = pl.BlockSpec((None, QB, KVW), lambda b, i: (b, jnp.maximum(i - 1, 0), 0))
    kc = pl.BlockSpec((None, QB, KVW), lambda b, i: (b, i, 0))
    v0 = pl.BlockSpec((None, QB, KVW), lambda b, i: (b, 0, 1))
    vp = pl.BlockSpec((None, QB, KVW), lambda b, i: (b, jnp.maximum(i - 1, 0), 1))
    vc = pl.BlockSpec((None, QB, KVW), lambda b, i: (b, i, 1))
    sink = pl.BlockSpec(memory_space=pltpu.SMEM)
    return qspec, [k0, kp, kc], [v0, vp, vc], sink


def _attn_fwd(q, k, kv, sinks):
    bsz, lp, d = q.shape
    nb = lp // QB
    qspec, kspecs, vspecs, sspec = _attn_specs(nb, d)

    def kern(q_ref, k0_ref, kp_ref, kc_ref, v0_ref, vp_ref, vc_ref, sink_ref, o_ref):
        i = pl.program_id(1)
        for h in range(N_KV_HEADS):
            qs, kx, ps3, psink, den, lane, hsel = _attn_scores(i, q_ref, k0_ref, kp_ref, kc_ref, sink_ref, h)
            vx = _expand_kv((v0_ref, vp_ref, vc_ref), hsel)
            o = _dot(ps3[0], vx[0]) + _dot(ps3[1], vx[1])
            o_ref[:, h * KVW:(h + 1) * KVW] = _unstack(o * (1.0 / den), lane)

    return pl.pallas_call(
        kern, name="attn_fwd", grid=(bsz, nb),
        in_specs=[qspec] + kspecs + vspecs + [sspec],
        out_specs=qspec, out_shape=jax.ShapeDtypeStruct((bsz, lp, d), F32),
        compiler_params=_cparams(("arbitrary", "arbitrary")),
    )(q, k, k, k, kv, kv, kv, sinks)


def _attn_bwd(q, k, kv, sinks, o, do):
    bsz, lp, d = q.shape
    nb = lp // QB
    qspec, kspecs, vspecs, sspec = _attn_specs(nb, d)
    full = pl.BlockSpec((None, lp, KVW), lambda b, i: (b, 0, 0))

    def kern(q_ref, k0_ref, kp_ref, kc_ref, v0_ref, vp_ref, vc_ref, sink_ref, o_ref, do_ref,
             dq_ref, dk_ref, dv_ref, ds_ref):
        b, i = pl.program_id(0), pl.program_id(1)

        @pl.when(i == 0)
        def _():
            dk_ref[...] = jnp.zeros_like(dk_ref)
            dv_ref[...] = jnp.zeros_like(dv_ref)

        @pl.when((b == 0) & (i == 0))
        def _():
            ds_ref[...] = jnp.zeros_like(ds_ref)

        lane128 = lax.broadcasted_iota(jnp.int32, (1, LANES), 1)
        rowg = lax.broadcasted_iota(jnp.int32, (Q_PER_KV * QB, 1), 0) // QB
        dk_acc = [jnp.zeros((QB, KVW), F32), jnp.zeros((2 * QB, KVW), F32)]
        dv_acc = [jnp.zeros((QB, KVW), F32), jnp.zeros((2 * QB, KVW), F32)]
        dsink = jnp.zeros((1, LANES), F32)
        for h in range(N_KV_HEADS):
            qs, kx, ps3, psink, den, lane, hsel = _attn_scores(i, q_ref, k0_ref, kp_ref, kc_ref, sink_ref, h)
            vx = _expand_kv((v0_ref, vp_ref, vc_ref), hsel)
            sl = slice(h * KVW, (h + 1) * KVW)
            doh, oh = do_ref[:, sl], o_ref[:, sl]
            dos = jnp.concatenate([jnp.where(lane == g, doh, 0.0) for g in range(Q_PER_KV)], axis=0)
            ost = jnp.concatenate([jnp.where(lane == g, oh, 0.0) for g in range(Q_PER_KV)], axis=0)
            delta = jnp.sum(dos * ost, axis=1, keepdims=True)
            inv = 1.0 / den
            dosb = dos.astype(BF16)
            dqs = jnp.zeros((Q_PER_KV * QB, KVW), F32)
            for n in range(2):
                pn = ps3[n] * inv
                ds = pn * (_dot_nt(dosb, vx[n]) - delta) * ATTN_SCALE
                dqs = dqs + _dot(ds, kx[n])
                dk_acc[n] = dk_acc[n] + jnp.where(hsel, _fold4(_dot_tn(ds, qs)), 0.0)
                dv_acc[n] = dv_acc[n] + jnp.where(hsel, _fold4(_dot_tn(pn, dosb)), 0.0)
            dq_ref[:, sl] = _unstack(dqs, lane)
            dsk = -(psink * inv) * delta
            for g in range(Q_PER_KV):
                val = jnp.sum(jnp.where(rowg == g, dsk, 0.0), axis=0, keepdims=True)
                dsink = dsink + jnp.where(lane128 == h * Q_PER_KV + g, val, 0.0)
        ds_ref[...] += dsink
        r0 = pl.ds(0, QB)
        rp = pl.ds(pl.multiple_of(jnp.maximum(i - 1, 0) * QB, QB), QB)
        rc = pl.ds(pl.multiple_of(i * QB, QB), QB)
        for acc, ref in ((dk_acc, dk_ref), (dv_acc, dv_ref)):
            ref[r0, :] += acc[0]
            ref[rp, :] += acc[1][:QB]
            ref[rc, :] += acc[1][QB:]

    return pl.pallas_call(
        kern, name="attn_bwd", grid=(bsz, nb),
        in_specs=[qspec] + kspecs + vspecs + [sspec, qspec, qspec],
        out_specs=[qspec, full, full, pl.BlockSpec((1, LANES), lambda b, i: (0, 0))],
        out_shape=[jax.ShapeDtypeStruct((bsz, lp, d), F32), jax.ShapeDtypeStruct((bsz, lp, KVW), F32),
                   jax.ShapeDtypeStruct((bsz, lp, KVW), F32), jax.ShapeDtypeStruct((1, LANES), F32)],
        compiler_params=_cparams(("arbitrary", "arbitrary")),
    )(q, k, k, k, kv, kv, kv, sinks, o, do)


def _concat_cols(name, a, b):
    rows = a.shape[0]

    def body(step, a_ref, b_ref, o_ref):
        o_ref[...] = jnp.concatenate([a_ref[...], b_ref[...]], axis=1)

    return _rowcall(name, body, rows, [a, b], [], [(a.shape[1] + b.shape[1], F32)], light=True)[0]


def _adamw(name, w, m, v, parts):
    rows, wd = w.shape
    n = parts.shape[0]
    tm = _stream_tile(rows, wd * (7 * 4 + n * parts.dtype.itemsize))

    def kern(w_ref, m_ref, v_ref, p_ref, g_ref, d_ref, m2_ref, v2_ref):
        g = p_ref[0].astype(F32)
        for k in range(1, n):
            g = g + p_ref[k].astype(F32)
        m2 = ADAM_B1 * m_ref[...] + (1.0 - ADAM_B1) * g
        v2 = ADAM_B2 * v_ref[...] + (1.0 - ADAM_B2) * (g * g)
        mh = m2 / (1.0 - ADAM_B1 ** ADAM_STEP)
        vh = v2 / (1.0 - ADAM_B2 ** ADAM_STEP)
        g_ref[...] = g
        d_ref[...] = -ADAM_LR * (mh / (jnp.sqrt(vh) + ADAM_EPS) + ADAM_WD * w_ref[...])
        m2_ref[...] = m2
        v2_ref[...] = v2

    spec = pl.BlockSpec((tm, wd), lambda i: (i, 0))
    sd = jax.ShapeDtypeStruct((rows, wd), F32)
    return pl.pallas_call(
        kern, name=name, grid=(rows // tm,),
        in_specs=[spec, spec, spec, pl.BlockSpec((n, tm, wd), lambda i: (0, i, 0))],
        out_specs=[spec] * 4, out_shape=[sd] * 4,
        compiler_params=_cparams(("arbitrary",)),
    )(w, m, v, parts)


def _pair_sum(name, parts, theirs, my_c):
    n, _, rows, wd = parts.shape
    tm = _stream_tile(rows, wd * (4 + 4 + 2))

    def kern(c_ref, a_ref, b_ref, o_ref):
        o_ref[...] = (a_ref[...] + b_ref[...]).astype(BF16)

    return pl.pallas_call(
        kern, name=name,
        grid_spec=pltpu.PrefetchScalarGridSpec(
            num_scalar_prefetch=1, grid=(n, rows // tm),
            in_specs=[pl.BlockSpec((None, None, tm, wd), lambda k, i, c: (k, c[0], i, 0)),
                      pl.BlockSpec((None, tm, wd), lambda k, i, c: (k, i, 0))],
            out_specs=pl.BlockSpec((None, tm, wd), lambda k, i, c: (k, i, 0))),
        out_shape=jax.ShapeDtypeStruct((n, rows, wd), BF16), compiler_params=_cparams(("arbitrary", "arbitrary")),
    )(my_c, parts, theirs)


MESH = pl.DeviceIdType.MESH
ANY = pl.BlockSpec(memory_space=pl.ANY)


def _place():
    x, y, c = lax.axis_index("x"), lax.axis_index("y"), lax.axis_index("c")
    return x, y, c, [(1 - x, y), (x, 1 - y), (1 - x, 1 - y)]


def _gather_rider(shards):
    n = len(shards)

    def copy(refs, a, k, block, to, own=False):
        x_refs, out_refs, (send_sems, recv_sems, _) = refs
        px, py, pc = block
        slot = out_refs[a].at[4 * px + 2 * py + pc]
        return pltpu.make_async_remote_copy(
            src_ref=x_refs[a] if own else slot, dst_ref=slot,
            send_sem=send_sems.at[a, k], recv_sem=recv_sems.at[a, k], device_id=to, device_id_type=MESH)

    def local(refs, a):
        x, y, c, _ = _place()
        return pltpu.make_async_copy(refs[0][a], refs[1][a].at[4 * x + 2 * y + c], refs[2][2].at[a])

    def first(refs):
        x, y, c, chips = _place()
        out = []
        for a in range(n):
            out.append(copy(refs, a, 0, (x, y, c), (x, y, 1 - c), own=True))
            out += [copy(refs, a, 1 + j, (x, y, c), (*chip, c), own=True) for j, chip in enumerate(chips)]
        return out

    def passed(refs):
        x, y, c, chips = _place()
        return [copy(refs, a, 4 + j, (*chip, c), (x, y, 1 - c)) for j, chip in enumerate(chips) for a in range(n)]

    def start(*refs):
        for a in range(n):
            local(refs, a).start()
        for cp in first(refs):
            cp.start()

    def mid(*refs):
        x, y, c, chips = _place()
        fwd = passed(refs)
        for j, chip in enumerate(chips):
            for a in range(n):
                copy(refs, a, 1 + j, (*chip, c), (x, y, c)).wait_recv()
                fwd[j * n + a].start()

    def finish(*refs):
        x, y, c, chips = _place()
        for a in range(n):
            copy(refs, a, 0, (x, y, 1 - c), (x, y, c)).wait_recv()
            for j, chip in enumerate(chips):
                copy(refs, a, 4 + j, (*chip, 1 - c), (x, y, c)).wait_recv()
        for cp in first(refs) + passed(refs):
            cp.wait_send()
        for a in range(n):
            local(refs, a).wait()

    return _Rider(list(shards), [jax.ShapeDtypeStruct((N_DEV,) + s.shape, s.dtype) for s in shards],
                  [pltpu.SemaphoreType.DMA((n, 7)), pltpu.SemaphoreType.DMA((n, 7)), pltpu.SemaphoreType.DMA((n,))],
                  start, mid, finish)


def _swap_rider(parts):
    n = len(parts)

    def copies(p_refs, out_refs, sems):
        x, y, c, _ = _place()
        return [pltpu.make_async_remote_copy(
            src_ref=p_refs[a].at[:, 1 - c], dst_ref=out_refs[a], send_sem=sems[0].at[a], recv_sem=sems[1].at[a],
            device_id=(x, y, 1 - c), device_id_type=MESH) for a in range(n)]

    def start(*refs):
        for cp in copies(*refs):
            cp.start()

    def finish(*refs):
        for cp in copies(*refs):
            cp.wait()

    return _Rider(list(parts), [jax.ShapeDtypeStruct((p.shape[0],) + p.shape[2:], p.dtype) for p in parts],
                  [pltpu.SemaphoreType.DMA((n,)), pltpu.SemaphoreType.DMA((n,))], start, None, finish)


def _scatter_rider(sums):
    n = len(sums)

    def copy(refs, a, j, block):
        s_refs, out_refs, (send_sems, recv_sems, _) = refs
        x, y, c, chips = _place()
        px, py = chips[j]
        return pltpu.make_async_remote_copy(
            src_ref=s_refs[a].at[2 * px + py], dst_ref=out_refs[a].at[block],
            send_sem=send_sems.at[a, j], recv_sem=recv_sems.at[a, j], device_id=(px, py, c), device_id_type=MESH)

    def local(refs, a):
        x, y, c, _ = _place()
        return pltpu.make_async_copy(refs[0][a].at[2 * x + y], refs[1][a].at[2 * x + y], refs[2][2].at[a])

    def sends(refs):
        x, y, c, _ = _place()
        return [copy(refs, a, j, 2 * x + y) for j in range(3) for a in range(n)]

    def start(*refs):
        for a in range(n):
            local(refs, a).start()
        for cp in sends(refs):
            cp.start()

    def finish(*refs):
        x, y, c, chips = _place()
        for j, (px, py) in enumerate(chips):
            for a in range(n):
                copy(refs, a, j, 2 * px + py).wait_recv()
        for cp in sends(refs):
            cp.wait_send()
        for a in range(n):
            local(refs, a).wait()

    return _Rider(list(sums), [jax.ShapeDtypeStruct(s.shape, s.dtype) for s in sums],
                  [pltpu.SemaphoreType.DMA((n, 3)), pltpu.SemaphoreType.DMA((n, 3)), pltpu.SemaphoreType.DMA((n,))],
                  start, None, finish)


BIG = (("ffn1_w_gate_up", 2), ("ffn1_w_down", 1), ("ffn2_w_gate_up", 2), ("ffn2_w_down", 1), ("ssm_w_in", 1),
       ("ssm_w_out", 2), ("w_kv", 0), ("attn_w_q", 1), ("attn_w_o", 1))
SMALL = ("ffn1_norm", "mix_norm", "ffn2_norm", "ssm_lambda_re", "ssm_lambda_im", "ssm_b_re", "ssm_b_im",
         "ssm_c_re", "ssm_c_im", "ssm_log_step", "kv_norm", "k_norm", "q_norm", "attn_sinks")
COLS = (("meta_tokens", 1), ("ssm_d", 1))
WEIGHTS = ("meta_tokens", "ffn1_norm", "ffn1_w_gate_up", "ffn1_w_down", "mix_norm", "ffn2_norm", "ffn2_w_gate_up",
           "ffn2_w_down", "ssm_w_in", "ssm_lambda_re", "ssm_lambda_im", "ssm_b_re", "ssm_b_im", "ssm_c_re",
           "ssm_c_im", "ssm_log_step", "ssm_d", "ssm_w_out", "kv_norm", "w_kv", "k_norm", "attn_w_q", "q_norm",
           "attn_sinks", "attn_w_o")


def _rows_of(a, width):
    n = math.prod(a.shape)
    if n % width == 0:
        r = a.reshape(n // width, width)
    else:
        assert n < width
        r = jnp.pad(a.reshape(1, n), ((0, 0), (0, width - n)))
    return jnp.pad(r, ((0, (-r.shape[0]) % 8), (0, 0)))


def _pack_small(arrs, width):
    return jnp.concatenate([_rows_of(a.astype(F32), width) for a in arrs], axis=0)


def _unpack_small(buf, shapes, width):
    out, off = [], 0
    for shp in shapes:
        n = math.prod(shp)
        r = max(n // width, 1)
        out.append(buf[off:off + r].reshape(shp) if n % width == 0 else buf[off, :n].reshape(shp))
        off += r + (-r) % 8
    return out


def _shape2d(shp):
    return (math.prod(shp[:-1]), shp[-1])


def _unshard(g, axis):
    g = jnp.moveaxis(g, 0, axis)
    shp = g.shape
    return g.reshape(shp[:axis] + (shp[axis] * shp[axis + 1],) + shp[axis + 2:])


def _shard(full, axis):
    shp = full.shape
    g = full.reshape(shp[:axis] + (N_DEV, shp[axis] // N_DEV) + shp[axis + 1:])
    return jnp.moveaxis(g, axis, 0)


def _blockdiag(blocks):
    g, r, c = blocks.shape
    eye = jnp.eye(g, dtype=blocks.dtype)
    return (eye[:, None, :, None] * blocks[:, :, None, :]).reshape(g * r, g * c)


def _diagblocks(full, g):
    r, c = full.shape[0] // g, full.shape[1] // g
    f = full.reshape(g, r, g, c)
    idx = jnp.arange(g)
    return f[idx, :, idx, :]


def kernel(x, meta_tokens, ffn1_norm, ffn1_w_gate_up, ffn1_w_down, mix_norm, ffn2_norm, ffn2_w_gate_up, ffn2_w_down, ssm_w_in, ssm_lambda_re, ssm_lambda_im, ssm_b_re, ssm_b_im, ssm_c_re, ssm_c_im, ssm_log_step, ssm_d, ssm_w_out, kv_norm, w_kv, k_norm, attn_w_q, q_norm, attn_sinks, attn_w_o, loss_target, m_meta_tokens, m_ffn1_norm, m_ffn1_w_gate_up, m_ffn1_w_down, m_mix_norm, m_ffn2_norm, m_ffn2_w_gate_up, m_ffn2_w_down, m_ssm_w_in, m_ssm_lambda_re, m_ssm_lambda_im, m_ssm_b_re, m_ssm_b_im, m_ssm_c_re, m_ssm_c_im, m_ssm_log_step, m_ssm_d, m_ssm_w_out, m_kv_norm, m_w_kv, m_k_norm, m_attn_w_q, m_q_norm, m_attn_sinks, m_attn_w_o, v_meta_tokens, v_ffn1_norm, v_ffn1_w_gate_up, v_ffn1_w_down, v_mix_norm, v_ffn2_norm, v_ffn2_w_gate_up, v_ffn2_w_down, v_ssm_w_in, v_ssm_lambda_re, v_ssm_lambda_im, v_ssm_b_re, v_ssm_b_im, v_ssm_c_re, v_ssm_c_im, v_ssm_log_step, v_ssm_d, v_ssm_w_out, v_kv_norm, v_w_kv, v_k_norm, v_attn_w_q, v_q_norm, v_attn_sinks, v_attn_w_o):
    args = dict(locals())
    W = {n: args[n] for n in WEIGHTS}
    M = {n: args["m_" + n] for n in WEIGHTS}
    V = {n: args["v_" + n] for n in WEIGHTS}
    my_x, my_y, my_c = (lax.axis_index(a) for a in MESH_AXES)
    my_dev = 4 * my_x + 2 * my_y + my_c

    big_names = [n for n, _ in BIG]
    s2d = {n: _shape2d(W[n].shape) for n in big_names}
    col_w = W["meta_tokens"].shape[1]

    grads, summed, small_parts = _local_step(x, loss_target, W, my_c.astype(jnp.int32).reshape(1))
    loss = lax.psum(grads.pop("loss"), MESH_AXES)
    grad_x = grads.pop("x")

    outs = [{}, {}, {}, {}]
    for n in big_names:
        r4 = _adamw("adamw_" + n, W[n].reshape(s2d[n]), M[n].reshape(s2d[n]), V[n].reshape(s2d[n]), summed[n])
        for k in range(4):
            outs[k][n] = r4[k].reshape(W[n].shape)

    small_names = list(SMALL) + [n for n, _ in COLS]
    small_shapes = [grads[n].shape for n in small_names]
    zero_cols = [jnp.zeros(grads[n].shape, F32) for n, _ in COLS]
    packs = lambda d: _pack_small([d[n] for n in SMALL] + zero_cols, PACK_W)
    r4 = _adamw("adamw_small", packs(W), packs(M), packs(V), small_parts)
    gsmall = None
    for k in range(4):
        un = dict(zip(small_names, _unpack_small(r4[k], small_shapes, PACK_W)))
        gsmall = un if k == 0 else gsmall
        outs[k].update({n: un[n] for n in SMALL})
    col_g = [lax.dynamic_slice_in_dim(gsmall[n], my_dev * W[n].shape[1], W[n].shape[1], axis=1) for n, _ in COLS]
    packc = lambda d: _pack_small([d[n] for n, _ in COLS], col_w)
    r4 = _adamw("adamw_cols", packc(W), packc(M), packc(V), _pack_small(col_g, col_w)[None])
    col_shapes = [W[n].shape for n, _ in COLS]
    for k in range(4):
        outs[k].update(dict(zip([n for n, _ in COLS], _unpack_small(r4[k], col_shapes, col_w))))

    res = [[outs[k][n] for n in WEIGHTS] for k in range(4)]
    return (loss, grad_x, *res[0], *res[1], *res[2], *res[3])


def _local_step(x, target, P, c_arr):
    bsz, seq, d = x.shape
    lp = seq + PAD
    rows = bsz * lp
    depth = P["ffn1_norm"].shape[0]
    assert depth == 2
    bf = lambda a: a.astype(BF16)
    row = lambda a: a.reshape(1, -1)

    def shard(n, l=None):
        a = P[n] if l is None else P[n][l]
        return bf(a.reshape(_shape2d(a.shape)))

    rowsharded = lambda g: g.reshape((g.shape[0] * g.shape[1],) + g.shape[2:])
    colsharded = lambda g: _unshard(g, 1)
    col_w = P["meta_tokens"].shape[1]
    g0 = _run_rider("gather_first", _gather_rider(
        [shard("ffn1_w_gate_up", 0), shard("ffn1_w_down", 0), shard("ssm_w_in", 0),
         _pack_small([P["meta_tokens"], P["ssm_d"]], col_w)]))
    ffn_w = {("ffn1", 0): (colsharded(g0[0]), rowsharded(g0[1]))}
    w_in = rowsharded(g0[2])
    meta_full = _unshard(g0[3][:, :N_META], 1)
    dvec = _unshard(g0[3][:, N_META:N_META + 1, :P["ssm_d"].shape[1]], 1)

    pos = (jnp.arange(lp, dtype=F32) - float(META0))[:, None]
    half = HEAD_DIM // 2
    freqs = ROPE_THETA ** (-jnp.arange(0, half, dtype=F32) * 2.0 / HEAD_DIM)
    ang = pos * freqs[None, :]
    cos_t = jnp.tile(jnp.cos(ang), (1, LANES // half))
    sin_t = jnp.tile(jnp.concatenate([-jnp.sin(ang), jnp.sin(ang)], axis=1), (1, LANES // HEAD_DIM))
    gi = jnp.arange(LANES) // HEAD_DIM
    gmat = jnp.where(gi[:, None] == gi[None, :], 1.0 / HEAD_DIM, 0.0).astype(BF16)

    g_n, c_n, p_n = P["ssm_lambda_re"].shape[1], SSM_GROUP, SSM_STATE
    ns = g_n * p_n
    lr = P["ssm_lambda_re"][0].reshape(g_n, 1, p_n)
    li = P["ssm_lambda_im"][0].reshape(g_n, 1, p_n)
    ls = P["ssm_log_step"][0].reshape(g_n, 1, 1)
    brt = P["ssm_b_re"][0].transpose(0, 2, 1)
    bit = P["ssm_b_im"][0].transpose(0, 2, 1)
    ar, ai, bbr, bbi = _s5_params_fwd(lr, li, ls, brt, bit)
    a2 = jnp.concatenate([ar.reshape(1, ns), ai.reshape(1, ns)], axis=0)
    bfull = jnp.concatenate([_blockdiag(bbr), _blockdiag(bbi)], axis=1)
    cre_t = P["ssm_c_re"][0].transpose(0, 2, 1)
    cim_t = P["ssm_c_im"][0].transpose(0, 2, 1)
    cfull = jnp.concatenate([_blockdiag(cre_t), -_blockdiag(cim_t)], axis=0)

    ffn = lambda which, l: (row(P[which + "_norm"][l]),) + ffn_w[which, l]
    mix0, mix1, kvn = row(P["mix_norm"][0]), row(P["mix_norm"][1]), row(P["kv_norm"])
    kgain = jnp.tile(P["k_norm"].reshape(1, HEAD_DIM), (1, KVW // HEAD_DIM))
    qgain = jnp.tile(P["q_norm"].reshape(1, HEAD_DIM), (1, d // HEAD_DIM))
    sinks = P["attn_sinks"].reshape(1, -1)

    h0 = _embed(x, meta_full).reshape(rows, d)
    h1, ab_f1_0, g_wout, g_gu, g_d, g_kv = _ffn_fwd("ffn1_0_fwd", h0, *ffn("ffn1", 0), rider=_gather_rider(
        [shard("ssm_w_out", 0), shard("ffn2_w_gate_up", 0), shard("ffn2_w_down", 0), shard("w_kv")]))
    w_out, w_kv = colsharded(g_wout), rowsharded(g_kv)
    ffn_w["ffn2", 0] = (colsharded(g_gu), rowsharded(g_d))
    u = _proj_fwd("ssm_in_fwd", h1, mix0, w_in)
    y, xs, u_perm = _s5_scan_fwd(u, bf(bfull), bf(cfull), a2, dvec, bsz)
    h2 = _glu_fwd(y, h1, w_out)
    h3, ab_f2_0, g_gu, g_d, g_q, g_o = _ffn_fwd("ffn2_0_fwd", h2, *ffn("ffn2", 0), rider=_gather_rider(
        [shard("ffn1_w_gate_up", 1), shard("ffn1_w_down", 1), shard("attn_w_q", 0), shard("attn_w_o", 0)]))
    w_q, w_o = rowsharded(g_q), rowsharded(g_o)
    ffn_w["ffn1", 1] = (colsharded(g_gu), rowsharded(g_d))
    kv = _proj_fwd("kv_fwd", h3, kvn, w_kv)
    k = _headrope_fwd("k_rope_fwd", kv, KVW, kgain, cos_t, sin_t, gmat, lp)
    h4, ab_f1_1, g_gu, g_d = _ffn_fwd("ffn1_1_fwd", h3, *ffn("ffn1", 1), rider=_gather_rider(
        [shard("ffn2_w_gate_up", 1), shard("ffn2_w_down", 1)]))
    ffn_w["ffn2", 1] = (colsharded(g_gu), rowsharded(g_d))
    q_raw = _proj_fwd("q_fwd", h4, mix1, w_q)
    q = _headrope_fwd("q_rope_fwd", q_raw, d, qgain, cos_t, sin_t, gmat, lp)
    r3 = lambda a: a.reshape(bsz, lp, a.shape[-1])
    o = _attn_fwd(r3(q), r3(k), r3(kv), sinks).reshape(rows, d)
    h5 = _lin_res_fwd("attn_out_fwd", o, w_o, h4)
    h6, ab_f2_1 = _ffn_fwd("ffn2_1_fwd", h5, *ffn("ffn2", 1))
    loss, dh6 = _loss(r3(h6), target)
    dh6 = dh6.reshape(rows, d)

    G = {"loss": loss[0, 0]}

    def ffn_back(name, which, l, h, ab, dout, rider=None):
        g, wgu, wd = ffn(which, l)
        dh, hn, dab, act, dg, *rode = _ffn_bwd(name, h, ab, dout, g, wgu, wd, rider=rider)
        parts = [_shard(_mm_tn(name + "_wgu", hn, dab), 1), _mm_tn_slots(name + "_wd", act, dout, 0.5)]
        return dh, dg, parts, rode

    slots = lambda g: g.reshape((N_DEV, g.shape[0] // N_DEV) + g.shape[1:])
    swap_of = lambda parts: _swap_rider([p.reshape((4, 2) + p.shape[1:]) for p in parts])

    def pair_sums(tag, parts, theirs):
        return [_pair_sum("pair_sum_%s_%d" % (tag, k), p.reshape((4, 2) + p.shape[1:]), t, c_arr)
                for k, (p, t) in enumerate(zip(parts, theirs))]

    dh5, dg_f2_1, parts_a, _ = ffn_back("ffn2_1_bwd", "ffn2", 1, h5, ab_f2_1, dh6)
    do, dw_o, *theirs = _lin_bwd("attn_out_bwd", o, w_o, dh5, rider=swap_of(parts_a))
    sums_a = pair_sums("ffn2_1", parts_a, theirs)
    dq, dk, dv, dsinks = _attn_bwd(r3(q), r3(k), r3(kv), sinks, r3(o), r3(do))
    dq_raw, dqg = _headrope_bwd("q_rope_bwd", q_raw, d, dq.reshape(rows, d), qgain, cos_t, sin_t, gmat, lp)
    dh4, dg_mix1, dw_q = _proj_bwd("q_bwd", h4, mix1, w_q, dq_raw, dh5)
    dh3, dg_f1_1, parts_b, red_a = ffn_back("ffn1_1_bwd", "ffn1", 1, h3, ab_f1_1, dh4, rider=_scatter_rider(sums_a))
    dk_raw, dkg = _headrope_bwd("k_rope_bwd", kv, KVW, dk.reshape(rows, KVW), kgain, cos_t, sin_t, gmat, lp)
    parts_b = parts_b + [slots(dw_q), slots(dw_o)]
    dkv = _concat_cols("dkv_concat", dk_raw, dv.reshape(rows, KVW))
    dh3, dg_kvn, dw_kv, *theirs = _proj_bwd("kv_bwd", h3, kvn, w_kv, dkv, dh3, rider=swap_of(parts_b))
    sums_b = pair_sums("ffn1_1", parts_b, theirs)
    dh2, dg_f2_0, parts_c, red_b = ffn_back("ffn2_0_bwd", "ffn2", 0, h2, ab_f2_0, dh3, rider=_scatter_rider(sums_b))
    parts_c = parts_c + [slots(dw_kv)]
    dy, dw_out, *theirs = _glu_bwd(y, dh2, w_out, rider=swap_of(parts_c))
    sums_c = pair_sums("ffn2_0", parts_c, theirs)
    ctfull = jnp.concatenate([_blockdiag(P["ssm_c_re"][0]), -_blockdiag(P["ssm_c_im"][0])], axis=1)
    btfull = jnp.concatenate([_blockdiag(bbr.transpose(0, 2, 1)), _blockdiag(bbi.transpose(0, 2, 1))], axis=0)
    du, gx, dy_perm, da, dd = _s5_scan_bwd(dy, u, xs, bf(ctfull), bf(btfull), a2, dvec, bsz)
    dbfull = _mm_tn_blockdiag("ssm_db", u_perm, gx, False)
    dcfull = _mm_tn_blockdiag("ssm_dc", xs, dy_perm, True)
    dh1, dg_mix0, dw_in = _proj_bwd("ssm_in_bwd", h1, mix0, w_in, du, dh2)
    dh0, dg_f1_0, parts_d, red_c = ffn_back("ffn1_0_bwd", "ffn1", 0, h0, ab_f1_0, dh1, rider=_scatter_rider(sums_c))
    dbbr = _diagblocks(dbfull[:, :ns], g_n)
    dbbi = _diagblocks(dbfull[:, ns:], g_n)
    dlr, dli, dls, dbrt, dbit = _s5_params_bwd(lr, li, ls, brt, bit, da[:, :ns].reshape(g_n, 1, p_n),
                                               da[:, ns:].reshape(g_n, 1, p_n), dbbr, dbbi)
    dh0 = r3(dh0)
    G["x"] = dh0[:, PAD:, :]
    G["meta_tokens"] = _meta_sum(dh0)
    G["ffn1_norm"] = jnp.concatenate([dg_f1_0, dg_f1_1], axis=0)
    G["ffn2_norm"] = jnp.concatenate([dg_f2_0, dg_f2_1], axis=0)
    G["mix_norm"] = jnp.concatenate([dg_mix0, dg_mix1], axis=0)
    G["ssm_lambda_re"] = dlr.reshape(1, g_n, p_n)
    G["ssm_lambda_im"] = dli.reshape(1, g_n, p_n)
    G["ssm_log_step"] = dls.reshape(1, g_n)
    G["ssm_b_re"] = dbrt.transpose(0, 2, 1)[None]
    G["ssm_b_im"] = dbit.transpose(0, 2, 1)[None]
    G["ssm_c_re"] = _diagblocks(dcfull[:ns], g_n).transpose(0, 2, 1)[None]
    G["ssm_c_im"] = -_diagblocks(dcfull[ns:], g_n).transpose(0, 2, 1)[None]
    G["ssm_d"] = dd
    G["kv_norm"] = dg_kvn.reshape(-1)
    G["k_norm"] = dkg[0, :HEAD_DIM]
    G["q_norm"] = dqg[:, :HEAD_DIM]
    G["attn_sinks"] = dsinks[:, :N_KV_HEADS * Q_PER_KV]

    parts_d = parts_d + [slots(dw_in), dw_out]
    small_pack = _pack_small([G[n] for n in list(SMALL) + [n for n, _ in COLS]], PACK_W)
    *theirs, small_parts = _run_rider("grad_swap_last", _join_riders(swap_of(parts_d), _gather_rider([small_pack])))
    red_d = _run_rider("grad_scatter_last", _scatter_rider(pair_sums("last", parts_d, theirs)))
    both = lambda lo, hi: jnp.concatenate([lo, hi], axis=1)
    summed = {"ffn1_w_gate_up": both(red_d[0], red_b[0]), "ffn1_w_down": both(red_d[1], red_b[1]),
              "ffn2_w_gate_up": both(red_c[0], red_a[0]), "ffn2_w_down": both(red_c[1], red_a[1]),
              "ssm_w_in": red_d[2], "ssm_w_out": red_d[3], "w_kv": red_c[2], "attn_w_q": red_b[2],
              "attn_w_o": red_b[3]}
    return G, summed, small_parts
```

```python
import functools
import math

import jax
import jax.numpy as jnp
from jax import lax
from jax.experimental import pallas as pl
from jax.experimental.pallas import tpu as pltpu

F32 = jnp.float32
BF16 = jnp.bfloat16

N_META = 16
PAD = 128
META0 = PAD - N_META
HEAD_DIM = 64
N_KV_HEADS = 4
Q_PER_KV = 4
SSM_GROUP = 16
SSM_STATE = 64
EPS = 1e-6
NEG_INF = -1e30
ROPE_THETA = 10000.0
ADAM_LR, ADAM_B1, ADAM_B2, ADAM_EPS, ADAM_WD, ADAM_STEP = 0.001, 0.9, 0.999, 1e-08, 0.01, 10
LANES = 128
PACK_W = 1024
VMEM_LIMIT = 56 * 1024 * 1024
MESH_AXES = ("x", "y", "c")
N_DEV = 8


def _cparams(sem=None):
    return pltpu.CompilerParams(dimension_semantics=sem, vmem_limit_bytes=VMEM_LIMIT)


def _row_tile(rows, light=False):
    for tm in ((768,) if light else ()) + (384, 256, 128, 64, 32, 16, 8):
        if rows % tm == 0:
            return tm
    raise ValueError(rows)


STREAM_BUDGET = 32 * 1024 * 1024


def _stream_tile(rows, bytes_per_row):
    for tm in range(rows, 0, -1):
        if rows % tm == 0 and (tm % 16 == 0 or tm == rows) and 2 * tm * bytes_per_row <= STREAM_BUDGET:
            return tm
    raise ValueError(rows)


TN_BUDGET = 52 * 1024 * 1024
TN_MAX_ROWS = 2816


def _tn_tile(rows, a, b, k1, tn):
    sa, sb = a.dtype.itemsize, b.dtype.itemsize
    fits = lambda tm: 2 * tm * (k1 * sa + tn * sb) + 3 * k1 * tn * 4 + tm * k1 * 2 <= TN_BUDGET
    divisors = [tm for tm in range(min(rows, TN_MAX_ROWS), 7, -8) if rows % tm == 0 and fits(tm)]
    good = [tm for tm in divisors if -(-tm // MXU_DIM) * MXU_DIM <= 1.1 * tm]
    if good or divisors:
        return (good or divisors)[0]
    raise ValueError(rows)


def _dot(a, b):
    return jnp.dot(a.astype(BF16), b.astype(BF16), preferred_element_type=F32)


def _dot_nt(a, b):
    return lax.dot_general(a.astype(BF16), b.astype(BF16), (((1,), (1,)), ((), ())), preferred_element_type=F32)


def _dot_tn(a, b):
    return lax.dot_general(a.astype(BF16), b.astype(BF16), (((0,), (0,)), ((), ())), preferred_element_type=F32)


def _rms(x, g):
    rstd = lax.rsqrt(jnp.mean(x * x, axis=-1, keepdims=True) + EPS)
    y = x * rstd
    return y * g, y, rstd


def _rms_bwd(dhn, y, rstd, g):
    dyn = dhn * g
    dx = rstd * (dyn - y * jnp.mean(dyn * y, axis=-1, keepdims=True))
    return dx, jnp.sum(dhn * y, axis=0, keepdims=True)


def _sigmoid(x):
    return 1.0 / (1.0 + jnp.exp(-x))


_GELU_C = math.sqrt(2.0 / math.pi)


def _gelu(y):
    t = jnp.tanh(_GELU_C * (y + 0.044715 * y * y * y))
    return 0.5 * y * (1.0 + t), t


def _gelu_grad(y, t):
    return 0.5 * (1.0 + t) + 0.5 * y * (1.0 - t * t) * _GELU_C * (1.0 + 3.0 * 0.044715 * y * y)


class _Rider:
    def __init__(self, ins, outs, sems, start, mid, finish):
        self.ins, self.outs, self.sems, self.start, self.mid, self.finish = ins, outs, sems, start, mid, finish


def _join_riders(r1, r2):
    ni, no, ns = len(r1.ins), len(r1.outs), len(r1.sems)

    def both(f1, f2):
        def phase(ins, outs, sems):
            if f1 is not None:
                f1(ins[:ni], outs[:no], sems[:ns])
            if f2 is not None:
                f2(ins[ni:], outs[no:], sems[ns:])
        return phase

    mid = both(r1.mid, r2.mid) if (r1.mid is not None or r2.mid is not None) else None
    return _Rider(r1.ins + r2.ins, r1.outs + r2.outs, r1.sems + r2.sems,
                  both(r1.start, r2.start), mid, both(r1.finish, r2.finish))


def _run_rider(name, rider):
    def kern(*refs):
        ni, no = len(rider.ins), len(rider.outs)
        parts = refs[:ni], refs[ni:ni + no], refs[ni + no:]
        rider.start(*parts)
        if rider.mid is not None:
            rider.mid(*parts)
        rider.finish(*parts)

    return pl.pallas_call(
        kern, name=name, out_shape=list(rider.outs), in_specs=[ANY] * len(rider.ins),
        out_specs=[ANY] * len(rider.outs), scratch_shapes=list(rider.sems),
    )(*rider.ins)


def _rowcall(name, body, rows, row_ins, const_ins, row_outs, acc_outs=(), tm=None, row_in_maps=None, rider=None,
             light=False):
    tm = tm or _row_tile(rows, light)
    steps = rows // tm
    in_specs = []
    for k, a in enumerate(row_ins):
        if row_in_maps is not None and row_in_maps[k] is not None:
            in_specs.append(pl.BlockSpec(*row_in_maps[k]))
        else:
            in_specs.append(pl.BlockSpec((tm, a.shape[1]), lambda i: (i, 0)))
    for a in const_ins:
        in_specs.append(pl.BlockSpec(a.shape, lambda i, nd=a.ndim: (0,) * nd, pipeline_mode=pl.Buffered(1)))
    out_shape, out_specs = [], []
    for w, dt in row_outs:
        out_shape.append(jax.ShapeDtypeStruct((rows, w), dt))
        out_specs.append(pl.BlockSpec((tm, w), lambda i: (i, 0)))
    for shp, dt in acc_outs:
        out_shape.append(jax.ShapeDtypeStruct(shp, dt))
        out_specs.append(pl.BlockSpec(shp, lambda i, nd=len(shp): (0,) * nd))

    if rider is None:
        def kern(*refs):
            body(pl.program_id(0), *refs)

        return pl.pallas_call(
            kern, name=name, grid=(steps,), in_specs=in_specs, out_specs=out_specs, out_shape=out_shape,
            compiler_params=_cparams(("arbitrary",)),
        )(*row_ins, *const_ins)

    n_in, n_out = len(in_specs), len(out_specs)
    r_in, r_out = len(rider.ins), len(rider.outs)

    def kern_r(*refs):
        step = pl.program_id(0)
        ins, rins = refs[:n_in], refs[n_in:n_in + r_in]
        outs = refs[n_in + r_in:n_in + r_in + n_out]
        routs = refs[n_in + r_in + n_out:n_in + r_in + n_out + r_out]
        sems = refs[n_in + r_in + n_out + r_out:]

        @pl.when(step == 0)
        def _():
            rider.start(rins, routs, sems)

        if rider.mid is not None:
            @pl.when(step == (3 * steps) // 4)
            def _():
                rider.mid(rins, routs, sems)

        body(step, *ins, *outs)

        @pl.when(step == steps - 1)
        def _():
            rider.finish(rins, routs, sems)

    return pl.pallas_call(
        kern_r, name=name, grid=(steps,), in_specs=in_specs + [ANY] * r_in, out_specs=out_specs + [ANY] * r_out,
        out_shape=out_shape + list(rider.outs), scratch_shapes=list(rider.sems),
        compiler_params=_cparams(("arbitrary",)),
    )(*row_ins, *const_ins, *rider.ins)


def _acc(step, ref, val):
    @pl.when(step == 0)
    def _():
        ref[...] = val

    @pl.when(step != 0)
    def _():
        ref[...] += val


def _embed(x, meta):
    bsz, seq, d = x.shape
    nb = seq // PAD + 1

    def kern(x_ref, m_ref, o_ref):
        i = pl.program_id(1)

        @pl.when(i == 0)
        def _():
            o_ref[0, 0:META0, :] = jnp.zeros((META0, d), F32)
            o_ref[0, META0:PAD, :] = m_ref[...]

        @pl.when(i != 0)
        def _():
            o_ref[0] = x_ref[0]

    return pl.pallas_call(
        kern, name="embed", grid=(bsz, nb),
        in_specs=[pl.BlockSpec((1, PAD, d), lambda b, i: (b, jnp.maximum(i - 1, 0), 0)),
                  pl.BlockSpec((N_META, d), lambda b, i: (0, 0))],
        out_specs=pl.BlockSpec((1, PAD, d), lambda b, i: (b, i, 0)),
        out_shape=jax.ShapeDtypeStruct((bsz, seq + PAD, d), F32),
        compiler_params=_cparams(("arbitrary", "arbitrary")),
    )(x, meta)


def _meta_sum(dh0):
    bsz, lp, d = dh0.shape

    def kern(d_ref, o_ref):
        _acc(pl.program_id(0), o_ref, d_ref[0, META0:PAD, :])

    return pl.pallas_call(
        kern, name="meta_sum", grid=(bsz,),
        in_specs=[pl.BlockSpec((1, PAD, d), lambda b: (b, 0, 0))],
        out_specs=pl.BlockSpec((N_META, d), lambda b: (0, 0)),
        out_shape=jax.ShapeDtypeStruct((N_META, d), F32),
        compiler_params=_cparams(("arbitrary",)),
    )(dh0)


MXU_DIM = 256


def _ffn_chunks(f):
    unit = MXU_DIM if f % MXU_DIM == 0 else LANES
    assert f % unit == 0
    first = (f // unit + 1) // 2 * unit
    return [(0, first), (first, f)] if first < f else [(0, f)]


def _ffn_fwd(name, h, g, wgu, wd, rider=None, loss_target=None, lp=None):
    rows, d = h.shape
    f = wd.shape[0]
    chunks = _ffn_chunks(f)
    tm = _row_tile(rows)
    nblk = tm // PAD

    def body(step, h_ref, *refs):
        t_refs, (g_ref, wgu_ref, wd_ref, o_ref, ab_ref), l_refs = refs[:nt], refs[nt:nt + 5], refs[nt + 5:]
        hx = h_ref[...]
        hb = _rms(hx, g_ref[...])[0].astype(BF16)
        acc = jnp.zeros(hx.shape, F32)
        for lo, hi in chunks:
            ga, ua = slice(lo, hi), slice(f + lo, f + hi)
            a = _dot(hb, wgu_ref[:, ga])
            b = _dot(hb, wgu_ref[:, ua])
            ab_ref[:, ga] = a.astype(BF16)
            ab_ref[:, ua] = b.astype(BF16)
            acc = acc + _dot(a * _sigmoid(a) * b, wd_ref[ga, :])
        out = hx + 0.5 * acc
        if not nt:
            o_ref[...] = out
            return
        err = out - jnp.concatenate([t[...] for t in t_refs], axis=0)
        rid = lax.broadcasted_iota(jnp.int32, (tm, 1), 0)
        err = jnp.where((step % per == 0) & (rid < PAD), 0.0, err)
        o_ref[...] = err * (1.0 / d)
        part = 0.5 * jnp.sum(jnp.mean(err * err, axis=-1, keepdims=True))
        _acc(step, l_refs[0], jnp.broadcast_to(part, (1, LANES)))

    if loss_target is None:
        nt = 0
        return _rowcall(name, body, rows, [h], [g, wgu, wd], [(d, F32), (2 * f, BF16)], rider=rider, tm=tm)
    assert tm % PAD == 0 and lp % tm == 0 and rider is None
    nt, per = nblk, lp // tm
    tblocks = (lp - PAD) // PAD

    def tmap(k):
        return lambda i: ((i // per) * tblocks + jnp.clip((i % per) * nblk - 1 + k, 0, tblocks - 1), 0)

    maps = [None] + [((PAD, d), tmap(k)) for k in range(nblk)]
    return _rowcall(name, body, rows, [h] + [loss_target] * nblk, [g, wgu, wd], [(d, F32), (2 * f, BF16)],
                    [((1, LANES), F32)], tm=tm, row_in_maps=maps)


def _ffn_bwd(name, h, ab, dout, g, wgu, wd, rider=None):
    rows, d = h.shape
    f = wd.shape[0]
    chunks = _ffn_chunks(f)

    def body(step, h_ref, ab_ref, do_ref, g_ref, wgu_ref, wd_ref, dh_ref, hn_ref, dab_ref, act_ref, dg_ref):
        hx, dout_x, gx = h_ref[...], do_ref[...], g_ref[...]
        hn, y, rstd = _rms(hx, gx)
        hn_ref[...] = hn.astype(BF16)
        dhalf = (0.5 * dout_x).astype(BF16)
        dhn = jnp.zeros(hx.shape, F32)
        for lo, hi in chunks:
            ga, ua = slice(lo, hi), slice(f + lo, f + hi)
            a = ab_ref[:, ga].astype(F32)
            b = ab_ref[:, ua].astype(F32)
            s = _sigmoid(a)
            silu = a * s
            act_ref[:, ga] = (silu * b).astype(BF16)
            dact = _dot_nt(dhalf, wd_ref[ga, :])
            da = (dact * b * (s + silu * (1.0 - s))).astype(BF16)
            db = (dact * silu).astype(BF16)
            dab_ref[:, ga] = da
            dab_ref[:, ua] = db
            dhn = dhn + _dot_nt(da, wgu_ref[:, ga]) + _dot_nt(db, wgu_ref[:, ua])
        dx, dg = _rms_bwd(dhn, y, rstd, gx)
        dh_ref[...] = dout_x + dx
        _acc(step, dg_ref, dg)

    return _rowcall(name, body, rows, [h, ab, dout], [g, wgu, wd],
                    [(d, F32), (d, BF16), (2 * f, BF16), (f, BF16)], [((1, d), F32)], rider=rider)


def _mm_tn(name, a, b, scale=1.0):
    rows, k1 = a.shape
    k2 = b.shape[1]
    tn = k2
    for cand in (512, 704, 1408, 1024):
        if k2 % cand == 0 and k1 * cand * 4 <= 6 * 1024 * 1024:
            tn = cand
    tm = _tn_tile(rows, a, b, k1, tn)
    steps = rows // tm

    def kern(a_ref, b_ref, o_ref):
        bx = b_ref[...]
        if scale != 1.0:
            bx = bx * scale
        _acc(pl.program_id(1), o_ref, _dot_tn(a_ref[...], bx))

    return pl.pallas_call(
        kern, name=name, grid=(k2 // tn, steps),
        in_specs=[pl.BlockSpec((tm, k1), lambda j, i: (i, 0)), pl.BlockSpec((tm, tn), lambda j, i: (i, j))],
        out_specs=pl.BlockSpec((k1, tn), lambda j, i: (0, j)),
        out_shape=jax.ShapeDtypeStruct((k1, k2), F32),
        compiler_params=_cparams(("arbitrary", "arbitrary")),
    )(a, b)


def _mm_tn_blockdiag(name, a, b, states_first):
    rows = a.shape[0]
    ka, kb = a.shape[1], b.shape[1]
    qa, qb = (ka // 4, kb // 2) if states_first else (ka // 2, kb // 4)
    tm = _tn_tile(rows, a, b, qa, qb)
    steps = rows // tm
    wide = lambda part, k: 2 * part + k
    amap = (lambda p, k, i: (i, wide(p, k))) if states_first else (lambda p, k, i: (i, k))
    bmap = (lambda p, k, i: (i, k)) if states_first else (lambda p, k, i: (i, wide(p, k)))
    omap = (lambda p, k, i: (wide(p, k), k)) if states_first else (lambda p, k, i: (k, wide(p, k)))

    def kern(a_ref, b_ref, o_ref):
        _acc(pl.program_id(2), o_ref, _dot_tn(a_ref[...], b_ref[...]))

    return pl.pallas_call(
        kern, name=name, grid=(2, 2, steps),
        in_specs=[pl.BlockSpec((tm, qa), amap), pl.BlockSpec((tm, qb), bmap)],
        out_specs=pl.BlockSpec((qa, qb), omap), out_shape=jax.ShapeDtypeStruct((ka, kb), F32),
        compiler_params=_cparams(("arbitrary", "arbitrary", "arbitrary")),
    )(a, b)


def _mm_tn_slots(name, a, b, scale):
    rows, k1 = a.shape
    k2 = b.shape[1]
    tn = 512 if k2 % 512 == 0 else k2
    sr = k1 // N_DEV
    tm = _tn_tile(rows, a, b, k1, tn)
    steps = rows // tm

    def kern(a_ref, b_ref, o_ref):
        bx = b_ref[...]
        if scale != 1.0:
            bx = bx * scale
        res = _dot_tn(a_ref[...], bx)
        step = pl.program_id(1)
        for s in range(N_DEV):
            _acc(step, o_ref.at[s], res[s * sr:(s + 1) * sr])

    return pl.pallas_call(
        kern, name=name, grid=(k2 // tn, steps),
        in_specs=[pl.BlockSpec((tm, k1), lambda j, i: (i, 0)), pl.BlockSpec((tm, tn), lambda j, i: (i, j))],
        out_specs=pl.BlockSpec((N_DEV, sr, tn), lambda j, i: (0, 0, j)),
        out_shape=jax.ShapeDtypeStruct((N_DEV, sr, k2), F32),
        compiler_params=_cparams(("arbitrary", "arbitrary")),
    )(a, b)


def _proj_fwd(name, h, g, w):
    rows = h.shape[0]

    def body(step, h_ref, g_ref, w_ref, o_ref):
        o_ref[...] = _dot(_rms(h_ref[...], g_ref[...])[0], w_ref[...])

    return _rowcall(name, body, rows, [h], [g, w], [(w.shape[1], F32)], light=True)[0]


def _proj_bwd(name, h, g, w, dy, dres, rider=None):
    rows, d = h.shape

    def body(step, h_ref, dy_ref, dr_ref, g_ref, w_ref, dh_ref, dg_ref, dw_ref):
        gx = g_ref[...]
        hn, y, rstd = _rms(h_ref[...], gx)
        dyx = dy_ref[...]
        dx, dg = _rms_bwd(_dot_nt(dyx, w_ref[...]), y, rstd, gx)
        dh_ref[...] = dr_ref[...] + dx
        _acc(step, dg_ref, dg)
        _acc(step, dw_ref, _dot_tn(hn, dyx))

    return _rowcall(name, body, rows, [h, dy, dres], [g, w], [(d, F32)], [((1, d), F32), (w.shape, F32)],
                    rider=rider, light=True)


def _lin_res_fwd(name, a, w, res):
    rows = a.shape[0]

    def body(step, a_ref, r_ref, w_ref, o_ref):
        o_ref[...] = r_ref[...] + _dot(a_ref[...], w_ref[...])

    return _rowcall(name, body, rows, [a, res], [w], [(w.shape[1], F32)], light=True)[0]


def _lin_bwd(name, a, w, dy, rider=None):
    rows, k = a.shape

    def body(step, a_ref, dy_ref, w_ref, da_ref, dw_ref):
        dyx = dy_ref[...]
        da_ref[...] = _dot_nt(dyx, w_ref[...])
        _acc(step, dw_ref, _dot_tn(a_ref[...], dyx))

    return _rowcall(name, body, rows, [a, dy], [w], [(k, F32)], [(w.shape, F32)], rider=rider, light=True)


def _s5_param_fn(lr, li, ls, brt, bit):
    step = jnp.exp(ls)
    mag = jnp.exp(lr * step)
    ar = mag * jnp.cos(li * step)
    ai = mag * jnp.sin(li * step)
    den = lr * lr + li * li
    nr, ni = ar - 1.0, ai
    cr = (nr * lr + ni * li) / den
    ci = (ni * lr - nr * li) / den
    return ar, ai, cr * brt - ci * bit, cr * bit + ci * brt


def _s5_params_fwd(lr, li, ls, brt, bit):
    def kern(lr_ref, li_ref, ls_ref, br_ref, bi_ref, ar_ref, ai_ref, bbr_ref, bbi_ref):
        ar, ai, bbr, bbi = _s5_param_fn(lr_ref[...], li_ref[...], ls_ref[...], br_ref[...], bi_ref[...])
        ar_ref[...], ai_ref[...], bbr_ref[...], bbi_ref[...] = ar, ai, bbr, bbi

    sd = jax.ShapeDtypeStruct
    return pl.pallas_call(
        kern, name="s5_params_fwd",
        out_shape=[sd(lr.shape, F32), sd(lr.shape, F32), sd(brt.shape, F32), sd(brt.shape, F32)],
    )(lr, li, ls, brt, bit)


def _s5_params_bwd(lr, li, ls, brt, bit, dar, dai, dbbr, dbbi):
    def kern(lr_ref, li_ref, ls_ref, br_ref, bi_ref, dar_ref, dai_ref, dbbr_ref, dbbi_ref,
             dlr_ref, dli_ref, dls_ref, dbr_ref, dbi_ref):
        _, vjp = jax.vjp(_s5_param_fn, lr_ref[...], li_ref[...], ls_ref[...], br_ref[...], bi_ref[...])
        dlr, dli, dls, dbr, dbi = vjp((dar_ref[...], dai_ref[...], dbbr_ref[...], dbbi_ref[...]))
        dlr_ref[...], dli_ref[...], dls_ref[...], dbr_ref[...], dbi_ref[...] = dlr, dli, dls, dbr, dbi

    sd = jax.ShapeDtypeStruct
    return pl.pallas_call(
        kern, name="s5_params_bwd",
        out_shape=[sd(lr.shape, F32), sd(lr.shape, F32), sd(ls.shape, F32), sd(brt.shape, F32), sd(brt.shape, F32)],
    )(lr, li, ls, brt, bit, dar, dai, dbbr, dbbi)


SCAN_LW = 512


SCAN_SEGS = 8
SCAN_UNROLL = 8


def _cmul(xr, xi, yr, yi):
    return xr * yr - xi * yi, xr * yi + xi * yr


def _scan_tables(a_ref, tab_ref, conj, seg_len):
    ns = a_ref.shape[1]
    ar = jnp.broadcast_to(a_ref[0:1, :], (8, ns))
    ai = jnp.broadcast_to(a_ref[1:2, :], (8, ns))
    if conj:
        ai = -ai
    big, base, e = None, (ar, ai), seg_len
    while e:
        if e & 1:
            big = base if big is None else _cmul(*big, *base)
        base = _cmul(*base, *base)
        e >>= 1
    big2 = _cmul(*big, *big)
    big4 = _cmul(*big2, *big2)
    for k, v in enumerate((ar, ai) + big + big2 + big4):
        tab_ref[k] = v


def _scan_block(x_ref, tab_ref, carry_ref, t_rows, ns, reverse):
    sl = t_rows // SCAN_SEGS
    assert sl % SCAN_UNROLL == 0
    row = lax.broadcasted_iota(jnp.int32, (8, SCAN_LW), 0)
    zero = jnp.zeros((8, SCAN_LW), F32)
    for lc in range(ns // SCAN_LW):
        lre = pl.ds(lc * SCAN_LW, SCAN_LW)
        lim = pl.ds(ns + lc * SCAN_LW, SCAN_LW)
        ar, ai = tab_ref[0, :, lre], tab_ref[1, :, lre]

        def rows_of(k, u):
            j = k * SCAN_UNROLL + u
            return pl.ds(pl.multiple_of(((sl - 1 - j) if reverse else j) * SCAN_SEGS, SCAN_SEGS), SCAN_SEGS)

        def local(k, s, lre=lre, lim=lim, ar=ar, ai=ai):
            sr, si = s
            for u in range(SCAN_UNROLL):
                rows = rows_of(k, u)
                tr, ti = _cmul(ar, ai, sr, si)
                sr, si = x_ref[rows, lre] + tr, x_ref[rows, lim] + ti
                x_ref[rows, lre], x_ref[rows, lim] = sr, si
            return sr, si

        er, ei = lax.fori_loop(0, sl // SCAN_UNROLL, local, (zero, zero))
        if reverse:
            cr = jnp.where(row == 7, carry_ref[:, lre], pltpu.roll(er, 7, 0))
            ci = jnp.where(row == 7, carry_ref[:, lim], pltpu.roll(ei, 7, 0))
        else:
            cr = jnp.where(row == 0, carry_ref[:, lre], pltpu.roll(er, 1, 0))
            ci = jnp.where(row == 0, carry_ref[:, lim], pltpu.roll(ei, 1, 0))
        for lvl, dsh in enumerate((1, 2, 4)):
            pr, pi = tab_ref[2 + 2 * lvl, :, lre], tab_ref[3 + 2 * lvl, :, lre]
            if reverse:
                keep, shift = row < 8 - dsh, 8 - dsh
            else:
                keep, shift = row >= dsh, dsh
            sr = jnp.where(keep, pltpu.roll(cr, shift, 0), 0.0)
            si = jnp.where(keep, pltpu.roll(ci, shift, 0), 0.0)
            tr, ti = _cmul(pr, pi, sr, si)
            cr, ci = cr + tr, ci + ti
        tr, ti = _cmul(tab_ref[2, :, lre], tab_ref[3, :, lre], cr, ci)
        edge = 0 if reverse else 7
        carry_ref[:, lre] = jnp.broadcast_to((er + tr)[edge:edge + 1, :], (8, SCAN_LW))
        carry_ref[:, lim] = jnp.broadcast_to((ei + ti)[edge:edge + 1, :], (8, SCAN_LW))

        def fix(k, t, lre=lre, lim=lim, ar=ar, ai=ai):
            tr, ti = t
            for u in range(SCAN_UNROLL):
                rows = rows_of(k, u)
                tr, ti = _cmul(ar, ai, tr, ti)
                x_ref[rows, lre] = x_ref[rows, lre] + tr
                x_ref[rows, lim] = x_ref[rows, lim] + ti
            return tr, ti

        lax.fori_loop(0, sl // SCAN_UNROLL, fix, (cr, ci))


def _bd_expand(u, w_ref, x_ref, ns):
    hh, sh = u.shape[1] // 2, ns // 2
    ub = u.astype(BF16)
    for part in range(2):
        for k in range(2):
            cols = slice(part * ns + k * sh, part * ns + (k + 1) * sh)
            x_ref[:, cols] = jnp.dot(ub[:, k * hh:(k + 1) * hh], w_ref[k * hh:(k + 1) * hh, cols],
                                     preferred_element_type=F32)


def _bd_contract(x_ref, w_ref, ns):
    hh, sh = w_ref.shape[1] // 2, ns // 2
    halves = []
    for k in range(2):
        acc = None
        for part in range(2):
            rows = slice(part * ns + k * sh, part * ns + (k + 1) * sh)
            t = jnp.dot(x_ref[:, rows].astype(BF16), w_ref[rows, k * hh:(k + 1) * hh], preferred_element_type=F32)
            acc = t if acc is None else acc + t
        halves.append(acc)
    return jnp.concatenate(halves, axis=1)


def _scan_rows(lp):
    for t in (384, 256, 128):
        if lp % t == 0:
            return t
    raise ValueError(lp)


def _seg_perm(t_rows):
    r = jnp.arange(t_rows)
    src = (r % SCAN_SEGS) * (t_rows // SCAN_SEGS) + r // SCAN_SEGS
    p = (src[:, None] == r[None, :]).astype(BF16)
    return p, p.T


def _permute_rows(p_ref, v):
    return jnp.dot(p_ref[...], v.astype(BF16), preferred_element_type=F32)


def _unpermute_rows(pt_ref, v):
    hi = v.astype(BF16)
    lo = (v - hi.astype(F32)).astype(BF16)
    pt = pt_ref[...]
    return jnp.dot(pt, hi, preferred_element_type=F32) + jnp.dot(pt, lo, preferred_element_type=F32)


def _s5_scan_fwd(u, bfull, cfull, a2, dvec, bsz):
    rows, hw = u.shape
    ns = a2.shape[1]
    lp = rows // bsz
    t_rows = _scan_rows(lp)
    nc = lp // t_rows
    pmat, pmat_t = _seg_perm(t_rows)

    def kern(u_ref, b_ref, c_ref, a_ref, d_ref, p_ref, pt_ref, y_ref, x_ref, up_ref, tab_ref, carry_ref):
        c = pl.program_id(1)

        @pl.when((pl.program_id(0) == 0) & (c == 0))
        def _():
            _scan_tables(a_ref, tab_ref, False, t_rows // SCAN_SEGS)

        @pl.when(c == 0)
        def _():
            carry_ref[...] = jnp.zeros_like(carry_ref)

        ux = u_ref[...]
        up = _permute_rows(p_ref, ux)
        up_ref[...] = up.astype(BF16)
        _bd_expand(up, b_ref, x_ref, ns)
        _scan_block(x_ref, tab_ref, carry_ref, t_rows, ns, reverse=False)
        y_ref[...] = _unpermute_rows(pt_ref, _bd_contract(x_ref, c_ref, ns)) + d_ref[...] * ux

    const = lambda shp: pl.BlockSpec(shp, lambda b, c: (0,) * len(shp), pipeline_mode=pl.Buffered(1))
    blk = lambda b, c: (b * nc + c, 0)
    return pl.pallas_call(
        kern, name="s5_scan_fwd", grid=(bsz, nc),
        in_specs=[pl.BlockSpec((t_rows, hw), blk), const(bfull.shape), const(cfull.shape), const(a2.shape),
                  const(dvec.shape), const(pmat.shape), const(pmat.shape)],
        out_specs=[pl.BlockSpec((t_rows, hw), blk), pl.BlockSpec((t_rows, 2 * ns), blk),
                   pl.BlockSpec((t_rows, hw), blk)],
        out_shape=[jax.ShapeDtypeStruct((rows, hw), F32), jax.ShapeDtypeStruct((rows, 2 * ns), F32),
                   jax.ShapeDtypeStruct((rows, hw), BF16)],
        scratch_shapes=[pltpu.VMEM((8, 8, ns), F32), pltpu.VMEM((8, 2 * ns), F32)],
        compiler_params=_cparams(("arbitrary", "arbitrary")),
    )(u, bfull, cfull, a2, dvec, pmat, pmat_t)


def _s5_scan_bwd(dy, u, xs, ctfull, btfull, a2, dvec, bsz):
    rows, hw = u.shape
    ns = a2.shape[1]
    lp = rows // bsz
    t_rows = _scan_rows(lp)
    nc = lp // t_rows
    blk = lambda b, c: (b * nc + (nc - 1 - c), 0)
    pmat, pmat_t = _seg_perm(t_rows)

    def prev8(b, c):
        first = (b * nc + (nc - 1 - c)) * (t_rows // 8)
        return (jnp.maximum(first - 1, 0), 0)

    def kern(dy_ref, u_ref, x_ref, xp_ref, ct_ref, bt_ref, a_ref, d_ref, p_ref, pt_ref,
             du_ref, gx_ref, dyp_ref, da_ref, dd_ref, tab_ref, carry_ref):
        b, c = pl.program_id(0), pl.program_id(1)
        first = (b == 0) & (c == 0)

        @pl.when(first)
        def _():
            _scan_tables(a_ref, tab_ref, True, t_rows // SCAN_SEGS)

        @pl.when(c == 0)
        def _():
            carry_ref[...] = jnp.zeros_like(carry_ref)

        dyx, ux = dy_ref[...], u_ref[...]
        dyp = _permute_rows(p_ref, dyx)
        dyp_ref[...] = dyp.astype(BF16)
        _bd_expand(dyp, ct_ref, gx_ref, ns)
        _scan_block(gx_ref, tab_ref, carry_ref, t_rows, ns, reverse=True)
        gx = gx_ref[...]
        du_ref[...] = _unpermute_rows(pt_ref, _bd_contract(gx_ref, bt_ref, ns)) + d_ref[...] * dyx
        seq_start = c == nc - 1
        row8 = lax.broadcasted_iota(jnp.int32, (8, 1), 0)
        head = pltpu.roll(x_ref[t_rows - 8:t_rows, :], 1, 0)
        head = jnp.where(row8 == 0, jnp.where(seq_start, 0.0, xp_ref[7:8, :]), head)
        xprev = jnp.concatenate([head, x_ref[0:t_rows - 8, :]], axis=0)
        xr, xi, gr, gi = xprev[:, :ns], xprev[:, ns:], gx[:, :ns], gx[:, ns:]
        da = jnp.concatenate([jnp.sum(xr * gr + xi * gi, axis=0, keepdims=True),
                              jnp.sum(xr * gi - xi * gr, axis=0, keepdims=True)], axis=1)
        dd = jnp.sum(dyx * ux, axis=0, keepdims=True)

        @pl.when(first)
        def _():
            da_ref[...] = da
            dd_ref[...] = dd

        @pl.when(jnp.logical_not(first))
        def _():
            da_ref[...] += da
            dd_ref[...] += dd

    const = lambda shp: pl.BlockSpec(shp, lambda b, c: (0,) * len(shp), pipeline_mode=pl.Buffered(1))
    return pl.pallas_call(
        kern, name="s5_scan_bwd", grid=(bsz, nc),
        in_specs=[pl.BlockSpec((t_rows, hw), blk), pl.BlockSpec((t_rows, hw), blk),
                  pl.BlockSpec((t_rows, 2 * ns), blk), pl.BlockSpec((8, 2 * ns), prev8),
                  const(ctfull.shape), const(btfull.shape), const(a2.shape), const(dvec.shape),
                  const(pmat.shape), const(pmat.shape)],
        out_specs=[pl.BlockSpec((t_rows, hw), blk), pl.BlockSpec((t_rows, 2 * ns), blk),
                   pl.BlockSpec((t_rows, hw), blk),
                   pl.BlockSpec((1, 2 * ns), lambda b, c: (0, 0)), pl.BlockSpec((1, hw), lambda b, c: (0, 0))],
        out_shape=[jax.ShapeDtypeStruct((rows, hw), F32), jax.ShapeDtypeStruct((rows, 2 * ns), F32),
                   jax.ShapeDtypeStruct((rows, hw), BF16),
                   jax.ShapeDtypeStruct((1, 2 * ns), F32), jax.ShapeDtypeStruct((1, hw), F32)],
        scratch_shapes=[pltpu.VMEM((8, 8, ns), F32), pltpu.VMEM((8, 2 * ns), F32)],
        compiler_params=_cparams(("arbitrary", "arbitrary")),
    )(dy, u, xs, xs, ctfull, btfull, a2, dvec, pmat, pmat_t)


def _glu_fwd(y, h1, wout):
    rows, d = h1.shape

    def body(step, y_ref, h_ref, w_ref, o_ref):
        z = _dot(_gelu(y_ref[...])[0], w_ref[...])
        o_ref[...] = h_ref[...] + z[:, :d] * _sigmoid(z[:, d:])

    return _rowcall("glu_fwd", body, rows, [y, h1], [wout], [(d, F32)], light=True)[0]


def _glu_bwd(y, dh2, wout, rider=None):
    rows, d = dh2.shape
    hw = y.shape[1]

    def body(step, y_ref, dh_ref, w_ref, dy_ref, dw_ref):
        yx, dh = y_ref[...], dh_ref[...]
        gl, t = _gelu(yx)
        z = _dot(gl, w_ref[...])
        za, sg = z[:, :d], _sigmoid(z[:, d:])
        dza = dh * sg
        dzg = dh * za * sg * (1.0 - sg)
        dgl = _dot_nt(dza, w_ref[:, :d]) + _dot_nt(dzg, w_ref[:, d:])
        dy_ref[...] = dgl * _gelu_grad(yx, t)
        for half, dz in enumerate((dza, dzg)):
            dw = _dot_tn(gl, dz)
            for s in range(N_DEV // 2):
                _acc(step, dw_ref.at[half * (N_DEV // 2) + s], dw[:, s * cw:(s + 1) * cw])

    cw = 2 * d // N_DEV
    return _rowcall("glu_bwd", body, rows, [y, dh2], [wout], [(hw, F32)], [((N_DEV, hw, cw), F32)], rider=rider,
                    light=True)


def _gmean64(x2, gmat):
    hi = x2.astype(BF16)
    r1 = x2 - hi.astype(F32)
    mid = r1.astype(BF16)
    lo = (r1 - mid.astype(F32)).astype(BF16)
    outs = []
    for j in range(x2.shape[1] // LANES):
        sl = slice(j * LANES, (j + 1) * LANES)
        f = lambda p: jnp.dot(p[:, sl], gmat, preferred_element_type=F32)
        outs.append(f(hi) + f(mid) + f(lo))
    return outs[0] if len(outs) == 1 else jnp.concatenate(outs, axis=1)


def _swap32(x):
    w = x.shape[1]
    lane = lax.broadcasted_iota(jnp.int32, (1, w), 1)
    return jnp.where((lane & 32) == 0, pltpu.roll(x, w - 32, 1), pltpu.roll(x, 32, 1))


def _tile_lanes(t, w):
    reps = w // t.shape[1]
    return t if reps == 1 else jnp.concatenate([t] * reps, axis=1)


def _headrope_fwd(name, raw, w, gain, cos, sin, gmat, lp):
    rows = raw.shape[0]
    tm = _row_tile(lp)
    per = lp // tm

    def body(step, x_ref, c_ref, s_ref, g_ref, gm_ref, o_ref):
        x = x_ref[...]
        rstd = lax.rsqrt(_gmean64(x * x, gm_ref[...]) + EPS)
        z = x * rstd * g_ref[...]
        o_ref[...] = z * _tile_lanes(c_ref[...], w) + _swap32(z) * _tile_lanes(s_ref[...], w)

    maps = [((tm, w), lambda i: (i, 0)), ((tm, LANES), lambda i: (i % per, 0)), ((tm, LANES), lambda i: (i % per, 0))]
    return _rowcall(name, body, rows, [raw, cos, sin], [gain, gmat], [(w, F32)], tm=tm, row_in_maps=maps)[0]


def _headrope_bwd(name, raw, w, dout, gain, cos, sin, gmat, lp):
    rows = raw.shape[0]
    tm = _row_tile(lp)
    per = lp // tm

    def body(step, x_ref, do_ref, c_ref, s_ref, g_ref, gm_ref, dx_ref, dg_ref):
        x, dout_x, gx, gm = x_ref[...], do_ref[...], g_ref[...], gm_ref[...]
        rstd = lax.rsqrt(_gmean64(x * x, gm) + EPS)
        yn = x * rstd
        dz = dout_x * _tile_lanes(c_ref[...], w) + _swap32(dout_x * _tile_lanes(s_ref[...], w))
        dyn = dz * gx
        dx_ref[...] = rstd * (dyn - yn * _gmean64(dyn * yn, gm))
        dg = jnp.sum(dz * yn, axis=0, keepdims=True)
        sh = w // 2
        while sh >= HEAD_DIM:
            dg = dg + pltpu.roll(dg, sh, 1)
            sh //= 2
        _acc(step, dg_ref, dg)

    maps = [((tm, w), lambda i: (i, 0)), None, ((tm, LANES), lambda i: (i % per, 0)), ((tm, LANES), lambda i: (i % per, 0))]
    return _rowcall(name, body, rows, [raw, dout, cos, sin], [gain, gmat], [(w, F32)], [((1, w), F32)],
                    tm=tm, row_in_maps=maps)


KVW = N_KV_HEADS * HEAD_DIM
QB = 128


def _fold4(x):
    y = x + pltpu.roll(x, 128, 1)
    return y + pltpu.roll(y, 64, 1)


ATTN_SCALE = HEAD_DIM ** -0.5


def _attn_masks(i):
    k0j = lax.broadcasted_iota(jnp.int32, (Q_PER_KV * QB, QB), 1)
    qi = lax.broadcasted_iota(jnp.int32, (Q_PER_KV * QB, 2 * QB), 0) % QB
    kj = lax.broadcasted_iota(jnp.int32, (Q_PER_KV * QB, 2 * QB), 1)
    in_prev = (kj < QB) & (kj > qi) & (i >= 2)
    in_cur = (kj >= QB) & (kj - QB <= qi)
    return k0j >= META0, in_prev | in_cur


def _attn_scores(i, q_ref, k0_ref, kp_ref, kc_ref, sink_ref, h):
    masks = _attn_masks(i)
    lane = lax.broadcasted_iota(jnp.int32, (1, KVW), 1) // HEAD_DIM
    qh = q_ref[:, h * KVW:(h + 1) * KVW]
    qs = jnp.concatenate([jnp.where(lane == g, qh, 0.0) for g in range(Q_PER_KV)], axis=0).astype(BF16)
    hsel = lane == h
    kx = _expand_kv((k0_ref, kp_ref, kc_ref), hsel)
    s0 = jnp.where(masks[0], _dot_nt(qs, kx[0]) * ATTN_SCALE, NEG_INF)
    sb = jnp.where(masks[1], _dot_nt(qs, kx[1]) * ATTN_SCALE, NEG_INF)
    rowg = lax.broadcasted_iota(jnp.int32, (Q_PER_KV * QB, 1), 0) // QB
    sink = jnp.zeros((Q_PER_KV * QB, 1), F32)
    for g in range(Q_PER_KV):
        sink = jnp.where(rowg == g, sink_ref[0, h * Q_PER_KV + g], sink)
    m = jnp.maximum(jnp.maximum(jnp.max(s0, axis=1, keepdims=True), jnp.max(sb, axis=1, keepdims=True)), sink)
    p0, pb, ps = jnp.exp(s0 - m), jnp.exp(sb - m), jnp.exp(sink - m)
    den = jnp.sum(p0, axis=1, keepdims=True) + jnp.sum(pb, axis=1, keepdims=True) + ps
    return qs, kx, (p0, pb), ps, den, lane, hsel


def _expand_kv(refs, hsel):
    x0, xp, xc = [_fold4(jnp.where(hsel, r[...], 0.0)).astype(BF16) for r in refs]
    return [x0, jnp.concatenate([xp, xc], axis=0)]


def _unstack(x, lane):
    out = jnp.where(lane == 0, x[0:QB], 0.0)
    for g in range(1, Q_PER_KV):
        out = out + jnp.where(lane == g, x[g * QB:(g + 1) * QB], 0.0)
    return out


def _attn_specs(nb, d):
    qspec = pl.BlockSpec((None, QB, d), lambda b, i: (b, i, 0))
    k0 = pl.BlockSpec((None, QB, KVW), lambda b, i: (b, 0, 0))
    kp = pl.BlockSpec((None, QB, KVW), lambda b, i: (b, jnp.maximum(i - 1, 0), 0))
    kc = pl.BlockSpec((None, QB, KVW), lambda b, i: (b, i, 0))
    v0 = pl.BlockSpec((None, QB, KVW), lambda b, i: (b, 0, 1))
    vp = pl.BlockSpec((None, QB, KVW), lambda b, i: (b, jnp.maximum(i - 1, 0), 1))
    vc = pl.BlockSpec((None, QB, KVW), lambda b, i: (b, i, 1))
    sink = pl.BlockSpec(memory_space=pltpu.SMEM)
    return qspec, [k0, kp, kc], [v0, vp, vc], sink


def _attn_fwd(q, k, kv, sinks):
    bsz, lp, d = q.shape
    nb = lp // QB
    qspec, kspecs, vspecs, sspec = _attn_specs(nb, d)

    def kern(q_ref, k0_ref, kp_ref, kc_ref, v0_ref, vp_ref, vc_ref, sink_ref, o_ref):
        i = pl.program_id(1)
        for h in range(N_KV_HEADS):
            qs, kx, ps3, psink, den, lane, hsel = _attn_scores(i, q_ref, k0_ref, kp_ref, kc_ref, sink_ref, h)
            vx = _expand_kv((v0_ref, vp_ref, vc_ref), hsel)
            o = _dot(ps3[0], vx[0]) + _dot(ps3[1], vx[1])
            o_ref[:, h * KVW:(h + 1) * KVW] = _unstack(o * (1.0 / den), lane)

    return pl.pallas_call(
        kern, name="attn_fwd", grid=(bsz, nb),
        in_specs=[qspec] + kspecs + vspecs + [sspec],
        out_specs=qspec, out_shape=jax.ShapeDtypeStruct((bsz, lp, d), F32),
        compiler_params=_cparams(("arbitrary", "arbitrary")),
    )(q, k, k, k, kv, kv, kv, sinks)


def _attn_bwd(q, k, kv, sinks, o, do):
    bsz, lp, d = q.shape
    nb = lp // QB
    qspec, kspecs, vspecs, sspec = _attn_specs(nb, d)
    full = pl.BlockSpec((None, lp, KVW), lambda b, i: (b, 0, 0))

    def kern(q_ref, k0_ref, kp_ref, kc_ref, v0_ref, vp_ref, vc_ref, sink_ref, o_ref, do_ref,
             dq_ref, dk_ref, dv_ref, ds_ref):
        b, i = pl.program_id(0), pl.program_id(1)

        @pl.when(i == 0)
        def _():
            dk_ref[...] = jnp.zeros_like(dk_ref)
            dv_ref[...] = jnp.zeros_like(dv_ref)

        @pl.when((b == 0) & (i == 0))
        def _():
            ds_ref[...] = jnp.zeros_like(ds_ref)

        lane128 = lax.broadcasted_iota(jnp.int32, (1, LANES), 1)
        rowg = lax.broadcasted_iota(jnp.int32, (Q_PER_KV * QB, 1), 0) // QB
        dk_acc = [jnp.zeros((QB, KVW), F32), jnp.zeros((2 * QB, KVW), F32)]
        dv_acc = [jnp.zeros((QB, KVW), F32), jnp.zeros((2 * QB, KVW), F32)]
        dsink = jnp.zeros((1, LANES), F32)
        for h in range(N_KV_HEADS):
            qs, kx, ps3, psink, den, lane, hsel = _attn_scores(i, q_ref, k0_ref, kp_ref, kc_ref, sink_ref, h)
            vx = _expand_kv((v0_ref, vp_ref, vc_ref), hsel)
            sl = slice(h * KVW, (h + 1) * KVW)
            doh, oh = do_ref[:, sl], o_ref[:, sl]
            dos = jnp.concatenate([jnp.where(lane == g, doh, 0.0) for g in range(Q_PER_KV)], axis=0)
            ost = jnp.concatenate([jnp.where(lane == g, oh, 0.0) for g in range(Q_PER_KV)], axis=0)
            delta = jnp.sum(dos * ost, axis=1, keepdims=True)
            inv = 1.0 / den
            dosb = dos.astype(BF16)
            dqs = jnp.zeros((Q_PER_KV * QB, KVW), F32)
            for n in range(2):
                pn = ps3[n] * inv
                ds = pn * (_dot_nt(dosb, vx[n]) - delta) * ATTN_SCALE
                dqs = dqs + _dot(ds, kx[n])
                dk_acc[n] = dk_acc[n] + jnp.where(hsel, _fold4(_dot_tn(ds, qs)), 0.0)
                dv_acc[n] = dv_acc[n] + jnp.where(hsel, _fold4(_dot_tn(pn, dosb)), 0.0)
            dq_ref[:, sl] = _unstack(dqs, lane)
            dsk = -(psink * inv) * delta
            for g in range(Q_PER_KV):
                val = jnp.sum(jnp.where(rowg == g, dsk, 0.0), axis=0, keepdims=True)
                dsink = dsink + jnp.where(lane128 == h * Q_PER_KV + g, val, 0.0)
        ds_ref[...] += dsink
        r0 = pl.ds(0, QB)
        rp = pl.ds(pl.multiple_of(jnp.maximum(i - 1, 0) * QB, QB), QB)
        rc = pl.ds(pl.multiple_of(i * QB, QB), QB)
        for acc, ref in ((dk_acc, dk_ref), (dv_acc, dv_ref)):
            ref[r0, :] += acc[0]
            ref[rp, :] += acc[1][:QB]
            ref[rc, :] += acc[1][QB:]

    return pl.pallas_call(
        kern, name="attn_bwd", grid=(bsz, nb),
        in_specs=[qspec] + kspecs + vspecs + [sspec, qspec, qspec],
        out_specs=[qspec, full, full, pl.BlockSpec((1, LANES), lambda b, i: (0, 0))],
        out_shape=[jax.ShapeDtypeStruct((bsz, lp, d), F32), jax.ShapeDtypeStruct((bsz, lp, KVW), F32),
                   jax.ShapeDtypeStruct((bsz, lp, KVW), F32), jax.ShapeDtypeStruct((1, LANES), F32)],
        compiler_params=_cparams(("arbitrary", "arbitrary")),
    )(q, k, k, k, kv, kv, kv, sinks, o, do)


def _concat_cols(name, a, b):
    rows = a.shape[0]

    def body(step, a_ref, b_ref, o_ref):
        o_ref[...] = jnp.concatenate([a_ref[...], b_ref[...]], axis=1)

    return _rowcall(name, body, rows, [a, b], [], [(a.shape[1] + b.shape[1], F32)], light=True)[0]


def _adamw(name, w, m, v, parts):
    rows, wd = w.shape
    n = parts.shape[0]
    tm = _stream_tile(rows, wd * (7 * 4 + n * parts.dtype.itemsize))

    def kern(w_ref, m_ref, v_ref, p_ref, g_ref, d_ref, m2_ref, v2_ref):
        g = p_ref[0].astype(F32)
        for k in range(1, n):
            g = g + p_ref[k].astype(F32)
        m2 = ADAM_B1 * m_ref[...] + (1.0 - ADAM_B1) * g
        v2 = ADAM_B2 * v_ref[...] + (1.0 - ADAM_B2) * (g * g)
        mh = m2 / (1.0 - ADAM_B1 ** ADAM_STEP)
        vh = v2 / (1.0 - ADAM_B2 ** ADAM_STEP)
        g_ref[...] = g
        d_ref[...] = -ADAM_LR * (mh / (jnp.sqrt(vh) + ADAM_EPS) + ADAM_WD * w_ref[...])
        m2_ref[...] = m2
        v2_ref[...] = v2

    spec = pl.BlockSpec((tm, wd), lambda i: (i, 0))
    sd = jax.ShapeDtypeStruct((rows, wd), F32)
    return pl.pallas_call(
        kern, name=name, grid=(rows // tm,),
        in_specs=[spec, spec, spec, pl.BlockSpec((n, tm, wd), lambda i: (0, i, 0))],
        out_specs=[spec] * 4, out_shape=[sd] * 4,
        compiler_params=_cparams(("arbitrary",)),
    )(w, m, v, parts)


def _pair_sum(name, parts, theirs, my_c):
    n, _, rows, wd = parts.shape
    tm = _stream_tile(rows, wd * (4 + 4 + 2))

    def kern(c_ref, a_ref, b_ref, o_ref):
        o_ref[...] = (a_ref[...] + b_ref[...]).astype(BF16)

    return pl.pallas_call(
        kern, name=name,
        grid_spec=pltpu.PrefetchScalarGridSpec(
            num_scalar_prefetch=1, grid=(n, rows // tm),
            in_specs=[pl.BlockSpec((None, None, tm, wd), lambda k, i, c: (k, c[0], i, 0)),
                      pl.BlockSpec((None, tm, wd), lambda k, i, c: (k, i, 0))],
            out_specs=pl.BlockSpec((None, tm, wd), lambda k, i, c: (k, i, 0))),
        out_shape=jax.ShapeDtypeStruct((n, rows, wd), BF16), compiler_params=_cparams(("arbitrary", "arbitrary")),
    )(my_c, parts, theirs)


MESH = pl.DeviceIdType.MESH
ANY = pl.BlockSpec(memory_space=pl.ANY)


def _place():
    x, y, c = lax.axis_index("x"), lax.axis_index("y"), lax.axis_index("c")
    return x, y, c, [(1 - x, y), (x, 1 - y), (1 - x, 1 - y)]


def _gather_rider(shards):
    n = len(shards)

    def copy(refs, a, k, block, to, own=False):
        x_refs, out_refs, (send_sems, recv_sems, _) = refs
        px, py, pc = block
        slot = out_refs[a].at[4 * px + 2 * py + pc]
        return pltpu.make_async_remote_copy(
            src_ref=x_refs[a] if own else slot, dst_ref=slot,
            send_sem=send_sems.at[a, k], recv_sem=recv_sems.at[a, k], device_id=to, device_id_type=MESH)

    def local(refs, a):
        x, y, c, _ = _place()
        return pltpu.make_async_copy(refs[0][a], refs[1][a].at[4 * x + 2 * y + c], refs[2][2].at[a])

    def first(refs):
        x, y, c, chips = _place()
        out = []
        for a in range(n):
            out.append(copy(refs, a, 0, (x, y, c), (x, y, 1 - c), own=True))
            out += [copy(refs, a, 1 + j, (x, y, c), (*chip, c), own=True) for j, chip in enumerate(chips)]
        return out

    def passed(refs):
        x, y, c, chips = _place()
        return [copy(refs, a, 4 + j, (*chip, c), (x, y, 1 - c)) for j, chip in enumerate(chips) for a in range(n)]

    def start(*refs):
        for a in range(n):
            local(refs, a).start()
        for cp in first(refs):
            cp.start()

    def mid(*refs):
        x, y, c, chips = _place()
        fwd = passed(refs)
        for j, chip in enumerate(chips):
            for a in range(n):
                copy(refs, a, 1 + j, (*chip, c), (x, y, c)).wait_recv()
                fwd[j * n + a].start()

    def finish(*refs):
        x, y, c, chips = _place()
        for a in range(n):
            copy(refs, a, 0, (x, y, 1 - c), (x, y, c)).wait_recv()
            for j, chip in enumerate(chips):
                copy(refs, a, 4 + j, (*chip, 1 - c), (x, y, c)).wait_recv()
        for cp in first(refs) + passed(refs):
            cp.wait_send()
        for a in range(n):
            local(refs, a).wait()

    return _Rider(list(shards), [jax.ShapeDtypeStruct((N_DEV,) + s.shape, s.dtype) for s in shards],
                  [pltpu.SemaphoreType.DMA((n, 7)), pltpu.SemaphoreType.DMA((n, 7)), pltpu.SemaphoreType.DMA((n,))],
                  start, mid, finish)


def _swap_rider(parts):
    n = len(parts)

    def copies(p_refs, out_refs, sems):
        x, y, c, _ = _place()
        return [pltpu.make_async_remote_copy(
            src_ref=p_refs[a].at[:, 1 - c], dst_ref=out_refs[a], send_sem=sems[0].at[a], recv_sem=sems[1].at[a],
            device_id=(x, y, 1 - c), device_id_type=MESH) for a in range(n)]

    def start(*refs):
        for cp in copies(*refs):
            cp.start()

    def finish(*refs):
        for cp in copies(*refs):
            cp.wait()

    return _Rider(list(parts), [jax.ShapeDtypeStruct((p.shape[0],) + p.shape[2:], p.dtype) for p in parts],
                  [pltpu.SemaphoreType.DMA((n,)), pltpu.SemaphoreType.DMA((n,))], start, None, finish)


def _scatter_rider(sums):
    n = len(sums)

    def copy(refs, a, j, block):
        s_refs, out_refs, (send_sems, recv_sems, _) = refs
        x, y, c, chips = _place()
        px, py = chips[j]
        return pltpu.make_async_remote_copy(
            src_ref=s_refs[a].at[2 * px + py], dst_ref=out_refs[a].at[block],
            send_sem=send_sems.at[a, j], recv_sem=recv_sems.at[a, j], device_id=(px, py, c), device_id_type=MESH)

    def local(refs, a):
        x, y, c, _ = _place()
        return pltpu.make_async_copy(refs[0][a].at[2 * x + y], refs[1][a].at[2 * x + y], refs[2][2].at[a])

    def sends(refs):
        x, y, c, _ = _place()
        return [copy(refs, a, j, 2 * x + y) for j in range(3) for a in range(n)]

    def start(*refs):
        for a in range(n):
            local(refs, a).start()
        for cp in sends(refs):
            cp.start()

    def finish(*refs):
        x, y, c, chips = _place()
        for j, (px, py) in enumerate(chips):
            for a in range(n):
                copy(refs, a, j, 2 * px + py).wait_recv()
        for cp in sends(refs):
            cp.wait_send()
        for a in range(n):
            local(refs, a).wait()

    return _Rider(list(sums), [jax.ShapeDtypeStruct(s.shape, s.dtype) for s in sums],
                  [pltpu.SemaphoreType.DMA((n, 3)), pltpu.SemaphoreType.DMA((n, 3)), pltpu.SemaphoreType.DMA((n,))],
                  start, None, finish)


BIG = (("ffn1_w_gate_up", 2), ("ffn1_w_down", 1), ("ffn2_w_gate_up", 2), ("ffn2_w_down", 1), ("ssm_w_in", 1),
       ("ssm_w_out", 2), ("w_kv", 0), ("attn_w_q", 1), ("attn_w_o", 1))
SMALL = ("ffn1_norm", "mix_norm", "ffn2_norm", "ssm_lambda_re", "ssm_lambda_im", "ssm_b_re", "ssm_b_im",
         "ssm_c_re", "ssm_c_im", "ssm_log_step", "kv_norm", "k_norm", "q_norm", "attn_sinks")
COLS = (("meta_tokens", 1), ("ssm_d", 1))
WEIGHTS = ("meta_tokens", "ffn1_norm", "ffn1_w_gate_up", "ffn1_w_down", "mix_norm", "ffn2_norm", "ffn2_w_gate_up",
           "ffn2_w_down", "ssm_w_in", "ssm_lambda_re", "ssm_lambda_im", "ssm_b_re", "ssm_b_im", "ssm_c_re",
           "ssm_c_im", "ssm_log_step", "ssm_d", "ssm_w_out", "kv_norm", "w_kv", "k_norm", "attn_w_q", "q_norm",
           "attn_sinks", "attn_w_o")


def _rows_of(a, width):
    n = math.prod(a.shape)
    if n % width == 0:
        r = a.reshape(n // width, width)
    else:
        assert n < width
        r = jnp.pad(a.reshape(1, n), ((0, 0), (0, width - n)))
    return jnp.pad(r, ((0, (-r.shape[0]) % 8), (0, 0)))


def _pack_small(arrs, width):
    return jnp.concatenate([_rows_of(a.astype(F32), width) for a in arrs], axis=0)


def _unpack_small(buf, shapes, width):
    out, off = [], 0
    for shp in shapes:
        n = math.prod(shp)
        r = max(n // width, 1)
        out.append(buf[off:off + r].reshape(shp) if n % width == 0 else buf[off, :n].reshape(shp))
        off += r + (-r) % 8
    return out


def _shape2d(shp):
    return (math.prod(shp[:-1]), shp[-1])


def _unshard(g, axis):
    g = jnp.moveaxis(g, 0, axis)
    shp = g.shape
    return g.reshape(shp[:axis] + (shp[axis] * shp[axis + 1],) + shp[axis + 2:])


def _shard(full, axis):
    shp = full.shape
    g = full.reshape(shp[:axis] + (N_DEV, shp[axis] // N_DEV) + shp[axis + 1:])
    return jnp.moveaxis(g, axis, 0)


def _blockdiag(blocks):
    g, r, c = blocks.shape
    eye = jnp.eye(g, dtype=blocks.dtype)
    return (eye[:, None, :, None] * blocks[:, :, None, :]).reshape(g * r, g * c)


def _diagblocks(full, g):
    r, c = full.shape[0] // g, full.shape[1] // g
    f = full.reshape(g, r, g, c)
    idx = jnp.arange(g)
    return f[idx, :, idx, :]


def kernel(x, meta_tokens, ffn1_norm, ffn1_w_gate_up, ffn1_w_down, mix_norm, ffn2_norm, ffn2_w_gate_up, ffn2_w_down, ssm_w_in, ssm_lambda_re, ssm_lambda_im, ssm_b_re, ssm_b_im, ssm_c_re, ssm_c_im, ssm_log_step, ssm_d, ssm_w_out, kv_norm, w_kv, k_norm, attn_w_q, q_norm, attn_sinks, attn_w_o, loss_target, m_meta_tokens, m_ffn1_norm, m_ffn1_w_gate_up, m_ffn1_w_down, m_mix_norm, m_ffn2_norm, m_ffn2_w_gate_up, m_ffn2_w_down, m_ssm_w_in, m_ssm_lambda_re, m_ssm_lambda_im, m_ssm_b_re, m_ssm_b_im, m_ssm_c_re, m_ssm_c_im, m_ssm_log_step, m_ssm_d, m_ssm_w_out, m_kv_norm, m_w_kv, m_k_norm, m_attn_w_q, m_q_norm, m_attn_sinks, m_attn_w_o, v_meta_tokens, v_ffn1_norm, v_ffn1_w_gate_up, v_ffn1_w_down, v_mix_norm, v_ffn2_norm, v_ffn2_w_gate_up, v_ffn2_w_down, v_ssm_w_in, v_ssm_lambda_re, v_ssm_lambda_im, v_ssm_b_re, v_ssm_b_im, v_ssm_c_re, v_ssm_c_im, v_ssm_log_step, v_ssm_d, v_ssm_w_out, v_kv_norm, v_w_kv, v_k_norm, v_attn_w_q, v_q_norm, v_attn_sinks, v_attn_w_o):
    args = dict(locals())
    W = {n: args[n] for n in WEIGHTS}
    M = {n: args["m_" + n] for n in WEIGHTS}
    V = {n: args["v_" + n] for n in WEIGHTS}
    my_x, my_y, my_c = (lax.axis_index(a) for a in MESH_AXES)
    my_dev = 4 * my_x + 2 * my_y + my_c

    big_names = [n for n, _ in BIG]
    s2d = {n: _shape2d(W[n].shape) for n in big_names}
    col_w = W["meta_tokens"].shape[1]

    grads, summed, small_parts = _local_step(x, loss_target, W, my_c.astype(jnp.int32).reshape(1))
    loss = lax.psum(grads.pop("loss"), MESH_AXES)
    grad_x = grads.pop("x")

    outs = [{}, {}, {}, {}]
    for n in big_names:
        r4 = _adamw("adamw_" + n, W[n].reshape(s2d[n]), M[n].reshape(s2d[n]), V[n].reshape(s2d[n]), summed[n])
        for k in range(4):
            outs[k][n] = r4[k].reshape(W[n].shape)

    small_names = list(SMALL) + [n for n, _ in COLS]
    small_shapes = [grads[n].shape for n in small_names]
    zero_cols = [jnp.zeros(grads[n].shape, F32) for n, _ in COLS]
    packs = lambda d: _pack_small([d[n] for n in SMALL] + zero_cols, PACK_W)
    r4 = _adamw("adamw_small", packs(W), packs(M), packs(V), small_parts)
    gsmall = None
    for k in range(4):
        un = dict(zip(small_names, _unpack_small(r4[k], small_shapes, PACK_W)))
        gsmall = un if k == 0 else gsmall
        outs[k].update({n: un[n] for n in SMALL})
    col_g = [lax.dynamic_slice_in_dim(gsmall[n], my_dev * W[n].shape[1], W[n].shape[1], axis=1) for n, _ in COLS]
    packc = lambda d: _pack_small([d[n] for n, _ in COLS], col_w)
    r4 = _adamw("adamw_cols", packc(W), packc(M), packc(V), _pack_small(col_g, col_w)[None])
    col_shapes = [W[n].shape for n, _ in COLS]
    for k in range(4):
        outs[k].update(dict(zip([n for n, _ in COLS], _unpack_small(r4[k], col_shapes, col_w))))

    res = [[outs[k][n] for n in WEIGHTS] for k in range(4)]
    return (loss, grad_x, *res[0], *res[1], *res[2], *res[3])


def _local_step(x, target, P, c_arr):
    bsz, seq, d = x.shape
    lp = seq + PAD
    rows = bsz * lp
    depth = P["ffn1_norm"].shape[0]
    assert depth == 2
    bf = lambda a: a.astype(BF16)
    row = lambda a: a.reshape(1, -1)

    def shard(n, l=None):
        a = P[n] if l is None else P[n][l]
        return bf(a.reshape(_shape2d(a.shape)))

    rowsharded = lambda g: g.reshape((g.shape[0] * g.shape[1],) + g.shape[2:])
    colsharded = lambda g: _unshard(g, 1)
    col_w = P["meta_tokens"].shape[1]
    g0 = _run_rider("gather_first", _gather_rider(
        [shard("ffn1_w_gate_up", 0), shard("ffn1_w_down", 0), shard("ssm_w_in", 0),
         _pack_small([P["meta_tokens"], P["ssm_d"]], col_w)]))
    ffn_w = {("ffn1", 0): (colsharded(g0[0]), rowsharded(g0[1]))}
    w_in = rowsharded(g0[2])
    meta_full = _unshard(g0[3][:, :N_META], 1)
    dvec = _unshard(g0[3][:, N_META:N_META + 1, :P["ssm_d"].shape[1]], 1)

    pos = (jnp.arange(lp, dtype=F32) - float(META0))[:, None]
    half = HEAD_DIM // 2
    freqs = ROPE_THETA ** (-jnp.arange(0, half, dtype=F32) * 2.0 / HEAD_DIM)
    ang = pos * freqs[None, :]
    cos_t = jnp.tile(jnp.cos(ang), (1, LANES // half))
    sin_t = jnp.tile(jnp.concatenate([-jnp.sin(ang), jnp.sin(ang)], axis=1), (1, LANES // HEAD_DIM))
    gi = jnp.arange(LANES) // HEAD_DIM
    gmat = jnp.where(gi[:, None] == gi[None, :], 1.0 / HEAD_DIM, 0.0).astype(BF16)

    g_n, c_n, p_n = P["ssm_lambda_re"].shape[1], SSM_GROUP, SSM_STATE
    ns = g_n * p_n
    lr = P["ssm_lambda_re"][0].reshape(g_n, 1, p_n)
    li = P["ssm_lambda_im"][0].reshape(g_n, 1, p_n)
    ls = P["ssm_log_step"][0].reshape(g_n, 1, 1)
    brt = P["ssm_b_re"][0].transpose(0, 2, 1)
    bit = P["ssm_b_im"][0].transpose(0, 2, 1)
    ar, ai, bbr, bbi = _s5_params_fwd(lr, li, ls, brt, bit)
    a2 = jnp.concatenate([ar.reshape(1, ns), ai.reshape(1, ns)], axis=0)
    bfull = jnp.concatenate([_blockdiag(bbr), _blockdiag(bbi)], axis=1)
    cre_t = P["ssm_c_re"][0].transpose(0, 2, 1)
    cim_t = P["ssm_c_im"][0].transpose(0, 2, 1)
    cfull = jnp.concatenate([_blockdiag(cre_t), -_blockdiag(cim_t)], axis=0)

    ffn = lambda which, l: (row(P[which + "_norm"][l]),) + ffn_w[which, l]
    mix0, mix1, kvn = row(P["mix_norm"][0]), row(P["mix_norm"][1]), row(P["kv_norm"])
    kgain = jnp.tile(P["k_norm"].reshape(1, HEAD_DIM), (1, KVW // HEAD_DIM))
    qgain = jnp.tile(P["q_norm"].reshape(1, HEAD_DIM), (1, d // HEAD_DIM))
    sinks = P["attn_sinks"].reshape(1, -1)

    h0 = _embed(x, meta_full).reshape(rows, d)
    h1, ab_f1_0, g_wout, g_gu, g_d, g_kv = _ffn_fwd("ffn1_0_fwd", h0, *ffn("ffn1", 0), rider=_gather_rider(
        [shard("ssm_w_out", 0), shard("ffn2_w_gate_up", 0), shard("ffn2_w_down", 0), shard("w_kv")]))
    w_out, w_kv = colsharded(g_wout), rowsharded(g_kv)
    ffn_w["ffn2", 0] = (colsharded(g_gu), rowsharded(g_d))
    u = _proj_fwd("ssm_in_fwd", h1, mix0, w_in)
    y, xs, u_perm = _s5_scan_fwd(u, bf(bfull), bf(cfull), a2, dvec, bsz)
    h2 = _glu_fwd(y, h1, w_out)
    h3, ab_f2_0, g_gu, g_d, g_q, g_o = _ffn_fwd("ffn2_0_fwd", h2, *ffn("ffn2", 0), rider=_gather_rider(
        [shard("ffn1_w_gate_up", 1), shard("ffn1_w_down", 1), shard("attn_w_q", 0), shard("attn_w_o", 0)]))
    w_q, w_o = rowsharded(g_q), rowsharded(g_o)
    ffn_w["ffn1", 1] = (colsharded(g_gu), rowsharded(g_d))
    kv = _proj_fwd("kv_fwd", h3, kvn, w_kv)
    k = _headrope_fwd("k_rope_fwd", kv, KVW, kgain, cos_t, sin_t, gmat, lp)
    h4, ab_f1_1, g_gu, g_d = _ffn_fwd("ffn1_1_fwd", h3, *ffn("ffn1", 1), rider=_gather_rider(
        [shard("ffn2_w_gate_up", 1), shard("ffn2_w_down", 1)]))
    ffn_w["ffn2", 1] = (colsharded(g_gu), rowsharded(g_d))
    q_raw = _proj_fwd("q_fwd", h4, mix1, w_q)
    q = _headrope_fwd("q_rope_fwd", q_raw, d, qgain, cos_t, sin_t, gmat, lp)
    r3 = lambda a: a.reshape(bsz, lp, a.shape[-1])
    o = _attn_fwd(r3(q), r3(k), r3(kv), sinks).reshape(rows, d)
    h5 = _lin_res_fwd("attn_out_fwd", o, w_o, h4)
    dh6, ab_f2_1, loss = _ffn_fwd("ffn2_1_fwd", h5, *ffn("ffn2", 1), loss_target=target.reshape(bsz * seq, d), lp=lp)

    G = {"loss": loss[0, 0]}

    def ffn_back(name, which, l, h, ab, dout, rider=None):
        g, wgu, wd = ffn(which, l)
        dh, hn, dab, act, dg, *rode = _ffn_bwd(name, h, ab, dout, g, wgu, wd, rider=rider)
        parts = [_shard(_mm_tn(name + "_wgu", hn, dab), 1), _mm_tn_slots(name + "_wd", act, dout, 0.5)]
        return dh, dg, parts, rode

    slots = lambda g: g.reshape((N_DEV, g.shape[0] // N_DEV) + g.shape[1:])
    swap_of = lambda parts: _swap_rider([p.reshape((4, 2) + p.shape[1:]) for p in parts])

    def pair_sums(tag, parts, theirs):
        return [_pair_sum("pair_sum_%s_%d" % (tag, k), p.reshape((4, 2) + p.shape[1:]), t, c_arr)
                for k, (p, t) in enumerate(zip(parts, theirs))]

    dh5, dg_f2_1, parts_a, _ = ffn_back("ffn2_1_bwd", "ffn2", 1, h5, ab_f2_1, dh6)
    do, dw_o, *theirs = _lin_bwd("attn_out_bwd", o, w_o, dh5, rider=swap_of(parts_a))
    sums_a = pair_sums("ffn2_1", parts_a, theirs)
    dq, dk, dv, dsinks = _attn_bwd(r3(q), r3(k), r3(kv), sinks, r3(o), r3(do))
    dq_raw, dqg = _headrope_bwd("q_rope_bwd", q_raw, d, dq.reshape(rows, d), qgain, cos_t, sin_t, gmat, lp)
    dh4, dg_mix1, dw_q = _proj_bwd("q_bwd", h4, mix1, w_q, dq_raw, dh5)
    dh3, dg_f1_1, parts_b, red_a = ffn_back("ffn1_1_bwd", "ffn1", 1, h3, ab_f1_1, dh4, rider=_scatter_rider(sums_a))
    dk_raw, dkg = _headrope_bwd("k_rope_bwd", kv, KVW, dk.reshape(rows, KVW), kgain, cos_t, sin_t, gmat, lp)
    parts_b = parts_b + [slots(dw_q), slots(dw_o)]
    dkv = _concat_cols("dkv_concat", dk_raw, dv.reshape(rows, KVW))
    dh3, dg_kvn, dw_kv, *theirs = _proj_bwd("kv_bwd", h3, kvn, w_kv, dkv, dh3, rider=swap_of(parts_b))
    sums_b = pair_sums("ffn1_1", parts_b, theirs)
    dh2, dg_f2_0, parts_c, red_b = ffn_back("ffn2_0_bwd", "ffn2", 0, h2, ab_f2_0, dh3, rider=_scatter_rider(sums_b))
    parts_c = parts_c + [slots(dw_kv)]
    dy, dw_out, *theirs = _glu_bwd(y, dh2, w_out, rider=swap_of(parts_c))
    sums_c = pair_sums("ffn2_0", parts_c, theirs)
    ctfull = jnp.concatenate([_blockdiag(P["ssm_c_re"][0]), -_blockdiag(P["ssm_c_im"][0])], axis=1)
    btfull = jnp.concatenate([_blockdiag(bbr.transpose(0, 2, 1)), _blockdiag(bbi.transpose(0, 2, 1))], axis=0)
    du, gx, dy_perm, da, dd = _s5_scan_bwd(dy, u, xs, bf(ctfull), bf(btfull), a2, dvec, bsz)
    dbfull = _mm_tn_blockdiag("ssm_db", u_perm, gx, False)
    dcfull = _mm_tn_blockdiag("ssm_dc", xs, dy_perm, True)
    dh1, dg_mix0, dw_in = _proj_bwd("ssm_in_bwd", h1, mix0, w_in, du, dh2)
    dh0, dg_f1_0, parts_d, red_c = ffn_back("ffn1_0_bwd", "ffn1", 0, h0, ab_f1_0, dh1, rider=_scatter_rider(sums_c))
    dbbr = _diagblocks(dbfull[:, :ns], g_n)
    dbbi = _diagblocks(dbfull[:, ns:], g_n)
    dlr, dli, dls, dbrt, dbit = _s5_params_bwd(lr, li, ls, brt, bit, da[:, :ns].reshape(g_n, 1, p_n),
                                               da[:, ns:].reshape(g_n, 1, p_n), dbbr, dbbi)
    dh0 = r3(dh0)
    G["x"] = dh0[:, PAD:, :]
    G["meta_tokens"] = _meta_sum(dh0)
    G["ffn1_norm"] = jnp.concatenate([dg_f1_0, dg_f1_1], axis=0)
    G["ffn2_norm"] = jnp.concatenate([dg_f2_0, dg_f2_1], axis=0)
    G["mix_norm"] = jnp.concatenate([dg_mix0, dg_mix1], axis=0)
    G["ssm_lambda_re"] = dlr.reshape(1, g_n, p_n)
    G["ssm_lambda_im"] = dli.reshape(1, g_n, p_n)
    G["ssm_log_step"] = dls.reshape(1, g_n)
    G["ssm_b_re"] = dbrt.transpose(0, 2, 1)[None]
    G["ssm_b_im"] = dbit.transpose(0, 2, 1)[None]
    G["ssm_c_re"] = _diagblocks(dcfull[:ns], g_n).transpose(0, 2, 1)[None]
    G["ssm_c_im"] = -_diagblocks(dcfull[ns:], g_n).transpose(0, 2, 1)[None]
    G["ssm_d"] = dd
    G["kv_norm"] = dg_kvn.reshape(-1)
    G["k_norm"] = dkg[0, :HEAD_DIM]
    G["q_norm"] = dqg[:, :HEAD_DIM]
    G["attn_sinks"] = dsinks[:, :N_KV_HEADS * Q_PER_KV]

    parts_d = parts_d + [slots(dw_in), dw_out]
    small_pack = _pack_small([G[n] for n in list(SMALL) + [n for n, _ in COLS]], PACK_W)
    *theirs, small_parts = _run_rider("grad_swap_last", _join_riders(swap_of(parts_d), _gather_rider([small_pack])))
    red_d = _run_rider("grad_scatter_last", _scatter_rider(pair_sums("last", parts_d, theirs)))
    both = lambda lo, hi: jnp.concatenate([lo, hi], axis=1)
    summed = {"ffn1_w_gate_up": both(red_d[0], red_b[0]), "ffn1_w_down": both(red_d[1], red_b[1]),
              "ffn2_w_gate_up": both(red_c[0], red_a[0]), "ffn2_w_down": both(red_c[1], red_a[1]),
              "ssm_w_in": red_d[2], "ssm_w_out": red_d[3], "w_kv": red_c[2], "attn_w_q": red_b[2],
              "attn_w_o": red_b[3]}
    return G, summed, small_parts
```

```python
import functools
import math

import jax
import jax.numpy as jnp
from jax import lax
from jax.experimental import pallas as pl
from jax.experimental.pallas import tpu as pltpu

F32 = jnp.float32
BF16 = jnp.bfloat16

N_META = 16
PAD = 128
META0 = PAD - N_META
HEAD_DIM = 64
N_KV_HEADS = 4
Q_PER_KV = 4
SSM_GROUP = 16
SSM_STATE = 64
EPS = 1e-6
NEG_INF = -1e30
ROPE_THETA = 10000.0
ADAM_LR, ADAM_B1, ADAM_B2, ADAM_EPS, ADAM_WD, ADAM_STEP = 0.001, 0.9, 0.999, 1e-08, 0.01, 10
LANES = 128
PACK_W = 1024
VMEM_LIMIT = 56 * 1024 * 1024
MESH_AXES = ("x", "y", "c")
N_DEV = 8


def _cparams(sem=None):
    return pltpu.CompilerParams(dimension_semantics=sem, vmem_limit_bytes=VMEM_LIMIT)


def _row_tile(rows, light=False):
    for tm in ((768,) if light else ()) + (384, 256, 128, 64, 32, 16, 8):
        if rows % tm == 0:
            return tm
    raise ValueError(rows)


STREAM_BUDGET = 32 * 1024 * 1024


def _stream_tile(rows, bytes_per_row):
    for tm in range(rows, 0, -1):
        if rows % tm == 0 and (tm % 16 == 0 or tm == rows) and 2 * tm * bytes_per_row <= STREAM_BUDGET:
            return tm
    raise ValueError(rows)


TN_BUDGET = 52 * 1024 * 1024
TN_MAX_ROWS = 2816


def _tn_tile(rows, a, b, k1, tn):
    sa, sb = a.dtype.itemsize, b.dtype.itemsize
    fits = lambda tm: 2 * tm * (k1 * sa + tn * sb) + 3 * k1 * tn * 4 + tm * k1 * 2 <= TN_BUDGET
    divisors = [tm for tm in range(min(rows, TN_MAX_ROWS), 7, -8) if rows % tm == 0 and fits(tm)]
    good = [tm for tm in divisors if -(-tm // MXU_DIM) * MXU_DIM <= 1.1 * tm]
    if good or divisors:
        return (good or divisors)[0]
    raise ValueError(rows)


def _dot(a, b):
    return jnp.dot(a.astype(BF16), b.astype(BF16), preferred_element_type=F32)


def _dot_nt(a, b):
    return lax.dot_general(a.astype(BF16), b.astype(BF16), (((1,), (1,)), ((), ())), preferred_element_type=F32)


def _dot_tn(a, b):
    return lax.dot_general(a.astype(BF16), b.astype(BF16), (((0,), (0,)), ((), ())), preferred_element_type=F32)


def _rms(x, g):
    rstd = lax.rsqrt(jnp.mean(x * x, axis=-1, keepdims=True) + EPS)
    y = x * rstd
    return y * g, y, rstd


def _rms_bwd(dhn, y, rstd, g):
    dyn = dhn * g
    dx = rstd * (dyn - y * jnp.mean(dyn * y, axis=-1, keepdims=True))
    return dx, jnp.sum(dhn * y, axis=0, keepdims=True)


def _sigmoid(x):
    return 1.0 / (1.0 + jnp.exp(-x))


_GELU_C = math.sqrt(2.0 / math.pi)


def _gelu(y):
    t = jnp.tanh(_GELU_C * (y + 0.044715 * y * y * y))
    return 0.5 * y * (1.0 + t), t


def _gelu_grad(y, t):
    return 0.5 * (1.0 + t) + 0.5 * y * (1.0 - t * t) * _GELU_C * (1.0 + 3.0 * 0.044715 * y * y)


class _Rider:
    def __init__(self, ins, outs, sems, start, mid, finish):
        self.ins, self.outs, self.sems, self.start, self.mid, self.finish = ins, outs, sems, start, mid, finish


def _join_riders(r1, r2):
    ni, no, ns = len(r1.ins), len(r1.outs), len(r1.sems)

    def both(f1, f2):
        def phase(ins, outs, sems):
            if f1 is not None:
                f1(ins[:ni], outs[:no], sems[:ns])
            if f2 is not None:
                f2(ins[ni:], outs[no:], sems[ns:])
        return phase

    mid = both(r1.mid, r2.mid) if (r1.mid is not None or r2.mid is not None) else None
    return _Rider(r1.ins + r2.ins, r1.outs + r2.outs, r1.sems + r2.sems,
                  both(r1.start, r2.start), mid, both(r1.finish, r2.finish))


def _run_rider(name, rider):
    def kern(*refs):
        ni, no = len(rider.ins), len(rider.outs)
        parts = refs[:ni], refs[ni:ni + no], refs[ni + no:]
        rider.start(*parts)
        if rider.mid is not None:
            rider.mid(*parts)
        rider.finish(*parts)

    return pl.pallas_call(
        kern, name=name, out_shape=list(rider.outs), in_specs=[ANY] * len(rider.ins),
        out_specs=[ANY] * len(rider.outs), scratch_shapes=list(rider.sems),
    )(*rider.ins)


def _rowcall(name, body, rows, row_ins, const_ins, row_outs, acc_outs=(), tm=None, row_in_maps=None, rider=None,
             light=False):
    tm = tm or _row_tile(rows, light)
    steps = rows // tm
    in_specs = []
    for k, a in enumerate(row_ins):
        if row_in_maps is not None and row_in_maps[k] is not None:
            in_specs.append(pl.BlockSpec(*row_in_maps[k]))
        else:
            in_specs.append(pl.BlockSpec((tm, a.shape[1]), lambda i: (i, 0)))
    for a in const_ins:
        in_specs.append(pl.BlockSpec(a.shape, lambda i, nd=a.ndim: (0,) * nd, pipeline_mode=pl.Buffered(1)))
    out_shape, out_specs = [], []
    for w, dt in row_outs:
        out_shape.append(jax.ShapeDtypeStruct((rows, w), dt))
        out_specs.append(pl.BlockSpec((tm, w), lambda i: (i, 0)))
    for shp, dt in acc_outs:
        out_shape.append(jax.ShapeDtypeStruct(shp, dt))
        out_specs.append(pl.BlockSpec(shp, lambda i, nd=len(shp): (0,) * nd))

    if rider is None:
        def kern(*refs):
            body(pl.program_id(0), *refs)

        return pl.pallas_call(
            kern, name=name, grid=(steps,), in_specs=in_specs, out_specs=out_specs, out_shape=out_shape,
            compiler_params=_cparams(("arbitrary",)),
        )(*row_ins, *const_ins)

    n_in, n_out = len(in_specs), len(out_specs)
    r_in, r_out = len(rider.ins), len(rider.outs)

    def kern_r(*refs):
        step = pl.program_id(0)
        ins, rins = refs[:n_in], refs[n_in:n_in + r_in]
        outs = refs[n_in + r_in:n_in + r_in + n_out]
        routs = refs[n_in + r_in + n_out:n_in + r_in + n_out + r_out]
        sems = refs[n_in + r_in + n_out + r_out:]

        @pl.when(step == 0)
        def _():
            rider.start(rins, routs, sems)

        if rider.mid is not None:
            @pl.when(step == (3 * steps) // 4)
            def _():
                rider.mid(rins, routs, sems)

        body(step, *ins, *outs)

        @pl.when(step == steps - 1)
        def _():
            rider.finish(rins, routs, sems)

    return pl.pallas_call(
        kern_r, name=name, grid=(steps,), in_specs=in_specs + [ANY] * r_in, out_specs=out_specs + [ANY] * r_out,
        out_shape=out_shape + list(rider.outs), scratch_shapes=list(rider.sems),
        compiler_params=_cparams(("arbitrary",)),
    )(*row_ins, *const_ins, *rider.ins)


def _acc(step, ref, val):
    @pl.when(step == 0)
    def _():
        ref[...] = val

    @pl.when(step != 0)
    def _():
        ref[...] += val


def _embed(x, meta):
    bsz, seq, d = x.shape
    nb = seq // PAD + 1

    def kern(x_ref, m_ref, o_ref):
        i = pl.program_id(1)

        @pl.when(i == 0)
        def _():
            o_ref[0, 0:META0, :] = jnp.zeros((META0, d), F32)
            o_ref[0, META0:PAD, :] = m_ref[...]

        @pl.when(i != 0)
        def _():
            o_ref[0] = x_ref[0]

    return pl.pallas_call(
        kern, name="embed", grid=(bsz, nb),
        in_specs=[pl.BlockSpec((1, PAD, d), lambda b, i: (b, jnp.maximum(i - 1, 0), 0)),
                  pl.BlockSpec((N_META, d), lambda b, i: (0, 0))],
        out_specs=pl.BlockSpec((1, PAD, d), lambda b, i: (b, i, 0)),
        out_shape=jax.ShapeDtypeStruct((bsz, seq + PAD, d), F32),
        compiler_params=_cparams(("arbitrary", "arbitrary")),
    )(x, meta)


def _meta_sum(dh0):
    bsz, lp, d = dh0.shape

    def kern(d_ref, o_ref):
        _acc(pl.program_id(0), o_ref, d_ref[0, META0:PAD, :])

    return pl.pallas_call(
        kern, name="meta_sum", grid=(bsz,),
        in_specs=[pl.BlockSpec((1, PAD, d), lambda b: (b, 0, 0))],
        out_specs=pl.BlockSpec((N_META, d), lambda b: (0, 0)),
        out_shape=jax.ShapeDtypeStruct((N_META, d), F32),
        compiler_params=_cparams(("arbitrary",)),
    )(dh0)


MXU_DIM = 256


def _ffn_chunks(f):
    unit = MXU_DIM if f % MXU_DIM == 0 else LANES
    assert f % unit == 0
    first = (f // unit + 1) // 2 * unit
    return [(0, first), (first, f)] if first < f else [(0, f)]


def _ffn_fwd(name, h, g, wgu, wd, rider=None, loss_target=None, lp=None):
    rows, d = h.shape
    f = wd.shape[0]
    chunks = _ffn_chunks(f)
    tm = _row_tile(rows)
    nblk = tm // PAD

    def body(step, h_ref, *refs):
        t_refs, (g_ref, wgu_ref, wd_ref, o_ref, ab_ref), l_refs = refs[:nt], refs[nt:nt + 5], refs[nt + 5:]
        hx = h_ref[...]
        hb = _rms(hx, g_ref[...])[0].astype(BF16)
        acc = jnp.zeros(hx.shape, F32)
        for lo, hi in chunks:
            ga, ua = slice(lo, hi), slice(f + lo, f + hi)
            a = _dot_nt(hb, wgu_ref[ga, :])
            b = _dot_nt(hb, wgu_ref[ua, :])
            ab_ref[:, ga] = a.astype(BF16)
            ab_ref[:, ua] = b.astype(BF16)
            acc = acc + _dot(a * _sigmoid(a) * b, wd_ref[ga, :])
        out = hx + 0.5 * acc
        if not nt:
            o_ref[...] = out
            return
        err = out - jnp.concatenate([t[...] for t in t_refs], axis=0)
        rid = lax.broadcasted_iota(jnp.int32, (tm, 1), 0)
        err = jnp.where((step % per == 0) & (rid < PAD), 0.0, err)
        o_ref[...] = err * (1.0 / d)
        part = 0.5 * jnp.sum(jnp.mean(err * err, axis=-1, keepdims=True))
        _acc(step, l_refs[0], jnp.broadcast_to(part, (1, LANES)))

    if loss_target is None:
        nt = 0
        return _rowcall(name, body, rows, [h], [g, wgu, wd], [(d, F32), (2 * f, BF16)], rider=rider, tm=tm)
    assert tm % PAD == 0 and lp % tm == 0 and rider is None
    nt, per = nblk, lp // tm
    tblocks = (lp - PAD) // PAD

    def tmap(k):
        return lambda i: ((i // per) * tblocks + jnp.clip((i % per) * nblk - 1 + k, 0, tblocks - 1), 0)

    maps = [None] + [((PAD, d), tmap(k)) for k in range(nblk)]
    return _rowcall(name, body, rows, [h] + [loss_target] * nblk, [g, wgu, wd], [(d, F32), (2 * f, BF16)],
                    [((1, LANES), F32)], tm=tm, row_in_maps=maps)


def _ffn_bwd(name, h, ab, dout, g, wgu, wd, rider=None):
    rows, d = h.shape
    f = wd.shape[0]
    chunks = _ffn_chunks(f)

    def body(step, h_ref, ab_ref, do_ref, g_ref, wgu_ref, wd_ref, dh_ref, hn_ref, dab_ref, act_ref, dg_ref):
        hx, dout_x, gx = h_ref[...], do_ref[...], g_ref[...]
        hn, y, rstd = _rms(hx, gx)
        hn_ref[...] = hn.astype(BF16)
        dhalf = (0.5 * dout_x).astype(BF16)
        dhn = jnp.zeros(hx.shape, F32)
        for lo, hi in chunks:
            ga, ua = slice(lo, hi), slice(f + lo, f + hi)
            a = ab_ref[:, ga].astype(F32)
            b = ab_ref[:, ua].astype(F32)
            s = _sigmoid(a)
            silu = a * s
            act_ref[:, ga] = (silu * b).astype(BF16)
            dact = _dot_nt(dhalf, wd_ref[ga, :])
            da = (dact * b * (s + silu * (1.0 - s))).astype(BF16)
            db = (dact * silu).astype(BF16)
            dab_ref[:, ga] = da
            dab_ref[:, ua] = db
            dhn = dhn + _dot(da, wgu_ref[ga, :]) + _dot(db, wgu_ref[ua, :])
        dx, dg = _rms_bwd(dhn, y, rstd, gx)
        dh_ref[...] = dout_x + dx
        _acc(step, dg_ref, dg)

    return _rowcall(name, body, rows, [h, ab, dout], [g, wgu, wd],
                    [(d, F32), (d, BF16), (2 * f, BF16), (f, BF16)], [((1, d), F32)], rider=rider)


def _mm_tn(name, a, b, scale=1.0):
    rows, k1 = a.shape
    k2 = b.shape[1]
    tn = k2
    for cand in (512, 704, 1408, 1024):
        if k2 % cand == 0 and k1 * cand * 4 <= 6 * 1024 * 1024:
            tn = cand
    tm = _tn_tile(rows, a, b, k1, tn)
    steps = rows // tm

    def kern(a_ref, b_ref, o_ref):
        bx = b_ref[...]
        if scale != 1.0:
            bx = bx * scale
        _acc(pl.program_id(1), o_ref, _dot_tn(a_ref[...], bx))

    return pl.pallas_call(
        kern, name=name, grid=(k2 // tn, steps),
        in_specs=[pl.BlockSpec((tm, k1), lambda j, i: (i, 0)), pl.BlockSpec((tm, tn), lambda j, i: (i, j))],
        out_specs=pl.BlockSpec((k1, tn), lambda j, i: (0, j)),
        out_shape=jax.ShapeDtypeStruct((k1, k2), F32),
        compiler_params=_cparams(("arbitrary", "arbitrary")),
    )(a, b)


def _mm_tn_blockdiag(name, a, b, states_first):
    rows = a.shape[0]
    ka, kb = a.shape[1], b.shape[1]
    qa, qb = (ka // 4, kb // 2) if states_first else (ka // 2, kb // 4)
    tm = _tn_tile(rows, a, b, qa, qb)
    steps = rows // tm
    wide = lambda part, k: 2 * part + k
    amap = (lambda p, k, i: (i, wide(p, k))) if states_first else (lambda p, k, i: (i, k))
    bmap = (lambda p, k, i: (i, k)) if states_first else (lambda p, k, i: (i, wide(p, k)))
    omap = (lambda p, k, i: (wide(p, k), k)) if states_first else (lambda p, k, i: (k, wide(p, k)))

    def kern(a_ref, b_ref, o_ref):
        _acc(pl.program_id(2), o_ref, _dot_tn(a_ref[...], b_ref[...]))

    return pl.pallas_call(
        kern, name=name, grid=(2, 2, steps),
        in_specs=[pl.BlockSpec((tm, qa), amap), pl.BlockSpec((tm, qb), bmap)],
        out_specs=pl.BlockSpec((qa, qb), omap), out_shape=jax.ShapeDtypeStruct((ka, kb), F32),
        compiler_params=_cparams(("arbitrary", "arbitrary", "arbitrary")),
    )(a, b)


def _mm_tn_slots(name, a, b, scale):
    rows, k1 = a.shape
    k2 = b.shape[1]
    tn = 512 if k2 % 512 == 0 else k2
    sr = k1 // N_DEV
    tm = _tn_tile(rows, a, b, k1, tn)
    steps = rows // tm

    def kern(a_ref, b_ref, o_ref):
        bx = b_ref[...]
        if scale != 1.0:
            bx = bx * scale
        res = _dot_tn(a_ref[...], bx)
        step = pl.program_id(1)
        for s in range(N_DEV):
            _acc(step, o_ref.at[s], res[s * sr:(s + 1) * sr])

    return pl.pallas_call(
        kern, name=name, grid=(k2 // tn, steps),
        in_specs=[pl.BlockSpec((tm, k1), lambda j, i: (i, 0)), pl.BlockSpec((tm, tn), lambda j, i: (i, j))],
        out_specs=pl.BlockSpec((N_DEV, sr, tn), lambda j, i: (0, 0, j)),
        out_shape=jax.ShapeDtypeStruct((N_DEV, sr, k2), F32),
        compiler_params=_cparams(("arbitrary", "arbitrary")),
    )(a, b)


def _proj_fwd(name, h, g, w):
    rows = h.shape[0]

    def body(step, h_ref, g_ref, w_ref, o_ref):
        o_ref[...] = _dot(_rms(h_ref[...], g_ref[...])[0], w_ref[...])

    return _rowcall(name, body, rows, [h], [g, w], [(w.shape[1], F32)], light=True)[0]


def _proj_bwd(name, h, g, w, dy, dres, rider=None):
    rows, d = h.shape

    def body(step, h_ref, dy_ref, dr_ref, g_ref, w_ref, dh_ref, dg_ref, dw_ref):
        gx = g_ref[...]
        hn, y, rstd = _rms(h_ref[...], gx)
        dyx = dy_ref[...]
        dx, dg = _rms_bwd(_dot_nt(dyx, w_ref[...]), y, rstd, gx)
        dh_ref[...] = dr_ref[...] + dx
        _acc(step, dg_ref, dg)
        _acc(step, dw_ref, _dot_tn(hn, dyx))

    return _rowcall(name, body, rows, [h, dy, dres], [g, w], [(d, F32)], [((1, d), F32), (w.shape, F32)],
                    rider=rider, light=True)


def _lin_res_fwd(name, a, w, res):
    rows = a.shape[0]

    def body(step, a_ref, r_ref, w_ref, o_ref):
        o_ref[...] = r_ref[...] + _dot(a_ref[...], w_ref[...])

    return _rowcall(name, body, rows, [a, res], [w], [(w.shape[1], F32)], light=True)[0]


def _lin_bwd(name, a, w, dy, rider=None):
    rows, k = a.shape

    def body(step, a_ref, dy_ref, w_ref, da_ref, dw_ref):
        dyx = dy_ref[...]
        da_ref[...] = _dot_nt(dyx, w_ref[...])
        _acc(step, dw_ref, _dot_tn(a_ref[...], dyx))

    return _rowcall(name, body, rows, [a, dy], [w], [(k, F32)], [(w.shape, F32)], rider=rider, light=True)


def _s5_param_fn(lr, li, ls, brt, bit):
    step = jnp.exp(ls)
    mag = jnp.exp(lr * step)
    ar = mag * jnp.cos(li * step)
    ai = mag * jnp.sin(li * step)
    den = lr * lr + li * li
    nr, ni = ar - 1.0, ai
    cr = (nr * lr + ni * li) / den
    ci = (ni * lr - nr * li) / den
    return ar, ai, cr * brt - ci * bit, cr * bit + ci * brt


def _s5_params_fwd(lr, li, ls, brt, bit):
    def kern(lr_ref, li_ref, ls_ref, br_ref, bi_ref, ar_ref, ai_ref, bbr_ref, bbi_ref):
        ar, ai, bbr, bbi = _s5_param_fn(lr_ref[...], li_ref[...], ls_ref[...], br_ref[...], bi_ref[...])
        ar_ref[...], ai_ref[...], bbr_ref[...], bbi_ref[...] = ar, ai, bbr, bbi

    sd = jax.ShapeDtypeStruct
    return pl.pallas_call(
        kern, name="s5_params_fwd",
        out_shape=[sd(lr.shape, F32), sd(lr.shape, F32), sd(brt.shape, F32), sd(brt.shape, F32)],
    )(lr, li, ls, brt, bit)


def _s5_params_bwd(lr, li, ls, brt, bit, dar, dai, dbbr, dbbi):
    def kern(lr_ref, li_ref, ls_ref, br_ref, bi_ref, dar_ref, dai_ref, dbbr_ref, dbbi_ref,
             dlr_ref, dli_ref, dls_ref, dbr_ref, dbi_ref):
        _, vjp = jax.vjp(_s5_param_fn, lr_ref[...], li_ref[...], ls_ref[...], br_ref[...], bi_ref[...])
        dlr, dli, dls, dbr, dbi = vjp((dar_ref[...], dai_ref[...], dbbr_ref[...], dbbi_ref[...]))
        dlr_ref[...], dli_ref[...], dls_ref[...], dbr_ref[...], dbi_ref[...] = dlr, dli, dls, dbr, dbi

    sd = jax.ShapeDtypeStruct
    return pl.pallas_call(
        kern, name="s5_params_bwd",
        out_shape=[sd(lr.shape, F32), sd(lr.shape, F32), sd(ls.shape, F32), sd(brt.shape, F32), sd(brt.shape, F32)],
    )(lr, li, ls, brt, bit, dar, dai, dbbr, dbbi)


SCAN_LW = 512


SCAN_SEGS = 8
SCAN_UNROLL = 8


def _cmul(xr, xi, yr, yi):
    return xr * yr - xi * yi, xr * yi + xi * yr


def _scan_tables(a_ref, tab_ref, conj, seg_len):
    ns = a_ref.shape[1]
    ar = jnp.broadcast_to(a_ref[0:1, :], (8, ns))
    ai = jnp.broadcast_to(a_ref[1:2, :], (8, ns))
    if conj:
        ai = -ai
    big, base, e = None, (ar, ai), seg_len
    while e:
        if e & 1:
            big = base if big is None else _cmul(*big, *base)
        base = _cmul(*base, *base)
        e >>= 1
    big2 = _cmul(*big, *big)
    big4 = _cmul(*big2, *big2)
    for k, v in enumerate((ar, ai) + big + big2 + big4):
        tab_ref[k] = v


def _scan_block(x_ref, tab_ref, carry_ref, t_rows, ns, reverse):
    sl = t_rows // SCAN_SEGS
    assert sl % SCAN_UNROLL == 0
    row = lax.broadcasted_iota(jnp.int32, (8, SCAN_LW), 0)
    zero = jnp.zeros((8, SCAN_LW), F32)
    for lc in range(ns // SCAN_LW):
        lre = pl.ds(lc * SCAN_LW, SCAN_LW)
        lim = pl.ds(ns + lc * SCAN_LW, SCAN_LW)
        ar, ai = tab_ref[0, :, lre], tab_ref[1, :, lre]

        def rows_of(k, u):
            j = k * SCAN_UNROLL + u
            return pl.ds(pl.multiple_of(((sl - 1 - j) if reverse else j) * SCAN_SEGS, SCAN_SEGS), SCAN_SEGS)

        def local(k, s, lre=lre, lim=lim, ar=ar, ai=ai):
            sr, si = s
            for u in range(SCAN_UNROLL):
                rows = rows_of(k, u)
                tr, ti = _cmul(ar, ai, sr, si)
                sr, si = x_ref[rows, lre] + tr, x_ref[rows, lim] + ti
                x_ref[rows, lre], x_ref[rows, lim] = sr, si
            return sr, si

        er, ei = lax.fori_loop(0, sl // SCAN_UNROLL, local, (zero, zero))
        if reverse:
            cr = jnp.where(row == 7, carry_ref[:, lre], pltpu.roll(er, 7, 0))
            ci = jnp.where(row == 7, carry_ref[:, lim], pltpu.roll(ei, 7, 0))
        else:
            cr = jnp.where(row == 0, carry_ref[:, lre], pltpu.roll(er, 1, 0))
            ci = jnp.where(row == 0, carry_ref[:, lim], pltpu.roll(ei, 1, 0))
        for lvl, dsh in enumerate((1, 2, 4)):
            pr, pi = tab_ref[2 + 2 * lvl, :, lre], tab_ref[3 + 2 * lvl, :, lre]
            if reverse:
                keep, shift = row < 8 - dsh, 8 - dsh
            else:
                keep, shift = row >= dsh, dsh
            sr = jnp.where(keep, pltpu.roll(cr, shift, 0), 0.0)
            si = jnp.where(keep, pltpu.roll(ci, shift, 0), 0.0)
            tr, ti = _cmul(pr, pi, sr, si)
            cr, ci = cr + tr, ci + ti
        tr, ti = _cmul(tab_ref[2, :, lre], tab_ref[3, :, lre], cr, ci)
        edge = 0 if reverse else 7
        carry_ref[:, lre] = jnp.broadcast_to((er + tr)[edge:edge + 1, :], (8, SCAN_LW))
        carry_ref[:, lim] = jnp.broadcast_to((ei + ti)[edge:edge + 1, :], (8, SCAN_LW))

        def fix(k, t, lre=lre, lim=lim, ar=ar, ai=ai):
            tr, ti = t
            for u in range(SCAN_UNROLL):
                rows = rows_of(k, u)
                tr, ti = _cmul(ar, ai, tr, ti)
                x_ref[rows, lre] = x_ref[rows, lre] + tr
                x_ref[rows, lim] = x_ref[rows, lim] + ti
            return tr, ti

        lax.fori_loop(0, sl // SCAN_UNROLL, fix, (cr, ci))


def _bd_expand(u, w_ref, x_ref, ns):
    hh, sh = u.shape[1] // 2, ns // 2
    ub = u.astype(BF16)
    for part in range(2):
        for k in range(2):
            cols = slice(part * ns + k * sh, part * ns + (k + 1) * sh)
            x_ref[:, cols] = jnp.dot(ub[:, k * hh:(k + 1) * hh], w_ref[k * hh:(k + 1) * hh, cols],
                                     preferred_element_type=F32)


def _bd_contract(x_ref, w_ref, ns):
    hh, sh = w_ref.shape[1] // 2, ns // 2
    halves = []
    for k in range(2):
        acc = None
        for part in range(2):
            rows = slice(part * ns + k * sh, part * ns + (k + 1) * sh)
            t = jnp.dot(x_ref[:, rows].astype(BF16), w_ref[rows, k * hh:(k + 1) * hh], preferred_element_type=F32)
            acc = t if acc is None else acc + t
        halves.append(acc)
    return jnp.concatenate(halves, axis=1)


def _scan_rows(lp):
    for t in (384, 256, 128):
        if lp % t == 0:
            return t
    raise ValueError(lp)


def _seg_perm(t_rows):
    r = jnp.arange(t_rows)
    src = (r % SCAN_SEGS) * (t_rows // SCAN_SEGS) + r // SCAN_SEGS
    p = (src[:, None] == r[None, :]).astype(BF16)
    return p, p.T


def _permute_rows(p_ref, v):
    return jnp.dot(p_ref[...], v.astype(BF16), preferred_element_type=F32)


def _unpermute_rows(pt_ref, v):
    hi = v.astype(BF16)
    lo = (v - hi.astype(F32)).astype(BF16)
    pt = pt_ref[...]
    return jnp.dot(pt, hi, preferred_element_type=F32) + jnp.dot(pt, lo, preferred_element_type=F32)


def _s5_scan_fwd(u, bfull, cfull, a2, dvec, bsz):
    rows, hw = u.shape
    ns = a2.shape[1]
    lp = rows // bsz
    t_rows = _scan_rows(lp)
    nc = lp // t_rows
    pmat, pmat_t = _seg_perm(t_rows)

    def kern(u_ref, b_ref, c_ref, a_ref, d_ref, p_ref, pt_ref, y_ref, x_ref, up_ref, tab_ref, carry_ref):
        c = pl.program_id(1)

        @pl.when((pl.program_id(0) == 0) & (c == 0))
        def _():
            _scan_tables(a_ref, tab_ref, False, t_rows // SCAN_SEGS)

        @pl.when(c == 0)
        def _():
            carry_ref[...] = jnp.zeros_like(carry_ref)

        ux = u_ref[...]
        up = _permute_rows(p_ref, ux)
        up_ref[...] = up.astype(BF16)
        _bd_expand(up, b_ref, x_ref, ns)
        _scan_block(x_ref, tab_ref, carry_ref, t_rows, ns, reverse=False)
        y_ref[...] = _unpermute_rows(pt_ref, _bd_contract(x_ref, c_ref, ns)) + d_ref[...] * ux

    const = lambda shp: pl.BlockSpec(shp, lambda b, c: (0,) * len(shp), pipeline_mode=pl.Buffered(1))
    blk = lambda b, c: (b * nc + c, 0)
    return pl.pallas_call(
        kern, name="s5_scan_fwd", grid=(bsz, nc),
        in_specs=[pl.BlockSpec((t_rows, hw), blk), const(bfull.shape), const(cfull.shape), const(a2.shape),
                  const(dvec.shape), const(pmat.shape), const(pmat.shape)],
        out_specs=[pl.BlockSpec((t_rows, hw), blk), pl.BlockSpec((t_rows, 2 * ns), blk),
                   pl.BlockSpec((t_rows, hw), blk)],
        out_shape=[jax.ShapeDtypeStruct((rows, hw), F32), jax.ShapeDtypeStruct((rows, 2 * ns), F32),
                   jax.ShapeDtypeStruct((rows, hw), BF16)],
        scratch_shapes=[pltpu.VMEM((8, 8, ns), F32), pltpu.VMEM((8, 2 * ns), F32)],
        compiler_params=_cparams(("arbitrary", "arbitrary")),
    )(u, bfull, cfull, a2, dvec, pmat, pmat_t)


def _s5_scan_bwd(dy, u, xs, ctfull, btfull, a2, dvec, bsz):
    rows, hw = u.shape
    ns = a2.shape[1]
    lp = rows // bsz
    t_rows = _scan_rows(lp)
    nc = lp // t_rows
    blk = lambda b, c: (b * nc + (nc - 1 - c), 0)
    pmat, pmat_t = _seg_perm(t_rows)

    def prev8(b, c):
        first = (b * nc + (nc - 1 - c)) * (t_rows // 8)
        return (jnp.maximum(first - 1, 0), 0)

    def kern(dy_ref, u_ref, x_ref, xp_ref, ct_ref, bt_ref, a_ref, d_ref, p_ref, pt_ref,
             du_ref, gx_ref, dyp_ref, da_ref, dd_ref, tab_ref, carry_ref):
        b, c = pl.program_id(0), pl.program_id(1)
        first = (b == 0) & (c == 0)

        @pl.when(first)
        def _():
            _scan_tables(a_ref, tab_ref, True, t_rows // SCAN_SEGS)

        @pl.when(c == 0)
        def _():
            carry_ref[...] = jnp.zeros_like(carry_ref)

        dyx, ux = dy_ref[...], u_ref[...]
        dyp = _permute_rows(p_ref, dyx)
        dyp_ref[...] = dyp.astype(BF16)
        _bd_expand(dyp, ct_ref, gx_ref, ns)
        _scan_block(gx_ref, tab_ref, carry_ref, t_rows, ns, reverse=True)
        gx = gx_ref[...]
        du_ref[...] = _unpermute_rows(pt_ref, _bd_contract(gx_ref, bt_ref, ns)) + d_ref[...] * dyx
        seq_start = c == nc - 1
        row8 = lax.broadcasted_iota(jnp.int32, (8, 1), 0)
        head = pltpu.roll(x_ref[t_rows - 8:t_rows, :], 1, 0)
        head = jnp.where(row8 == 0, jnp.where(seq_start, 0.0, xp_ref[7:8, :]), head)
        xprev = jnp.concatenate([head, x_ref[0:t_rows - 8, :]], axis=0)
        xr, xi, gr, gi = xprev[:, :ns], xprev[:, ns:], gx[:, :ns], gx[:, ns:]
        da = jnp.concatenate([jnp.sum(xr * gr + xi * gi, axis=0, keepdims=True),
                              jnp.sum(xr * gi - xi * gr, axis=0, keepdims=True)], axis=1)
        dd = jnp.sum(dyx * ux, axis=0, keepdims=True)

        @pl.when(first)
        def _():
            da_ref[...] = da
            dd_ref[...] = dd

        @pl.when(jnp.logical_not(first))
        def _():
            da_ref[...] += da
            dd_ref[...] += dd

    const = lambda shp: pl.BlockSpec(shp, lambda b, c: (0,) * len(shp), pipeline_mode=pl.Buffered(1))
    return pl.pallas_call(
        kern, name="s5_scan_bwd", grid=(bsz, nc),
        in_specs=[pl.BlockSpec((t_rows, hw), blk), pl.BlockSpec((t_rows, hw), blk),
                  pl.BlockSpec((t_rows, 2 * ns), blk), pl.BlockSpec((8, 2 * ns), prev8),
                  const(ctfull.shape), const(btfull.shape), const(a2.shape), const(dvec.shape),
                  const(pmat.shape), const(pmat.shape)],
        out_specs=[pl.BlockSpec((t_rows, hw), blk), pl.BlockSpec((t_rows, 2 * ns), blk),
                   pl.BlockSpec((t_rows, hw), blk),
                   pl.BlockSpec((1, 2 * ns), lambda b, c: (0, 0)), pl.BlockSpec((1, hw), lambda b, c: (0, 0))],
        out_shape=[jax.ShapeDtypeStruct((rows, hw), F32), jax.ShapeDtypeStruct((rows, 2 * ns), F32),
                   jax.ShapeDtypeStruct((rows, hw), BF16),
                   jax.ShapeDtypeStruct((1, 2 * ns), F32), jax.ShapeDtypeStruct((1, hw), F32)],
        scratch_shapes=[pltpu.VMEM((8, 8, ns), F32), pltpu.VMEM((8, 2 * ns), F32)],
        compiler_params=_cparams(("arbitrary", "arbitrary")),
    )(dy, u, xs, xs, ctfull, btfull, a2, dvec, pmat, pmat_t)


def _glu_fwd(y, h1, wout):
    rows, d = h1.shape

    def body(step, y_ref, h_ref, w_ref, o_ref):
        z = _dot(_gelu(y_ref[...])[0], w_ref[...])
        o_ref[...] = h_ref[...] + z[:, :d] * _sigmoid(z[:, d:])

    return _rowcall("glu_fwd", body, rows, [y, h1], [wout], [(d, F32)], light=True)[0]


def _glu_bwd(y, dh2, wout, rider=None):
    rows, d = dh2.shape
    hw = y.shape[1]

    def body(step, y_ref, dh_ref, w_ref, dy_ref, dw_ref):
        yx, dh = y_ref[...], dh_ref[...]
        gl, t = _gelu(yx)
        z = _dot(gl, w_ref[...])
        za, sg = z[:, :d], _sigmoid(z[:, d:])
        dza = dh * sg
        dzg = dh * za * sg * (1.0 - sg)
        dgl = _dot_nt(dza, w_ref[:, :d]) + _dot_nt(dzg, w_ref[:, d:])
        dy_ref[...] = dgl * _gelu_grad(yx, t)
        for half, dz in enumerate((dza, dzg)):
            dw = _dot_tn(gl, dz)
            for s in range(N_DEV // 2):
                _acc(step, dw_ref.at[half * (N_DEV // 2) + s], dw[:, s * cw:(s + 1) * cw])

    cw = 2 * d // N_DEV
    return _rowcall("glu_bwd", body, rows, [y, dh2], [wout], [(hw, F32)], [((N_DEV, hw, cw), F32)], rider=rider,
                    light=True)


def _gmean64(x2, gmat):
    hi = x2.astype(BF16)
    r1 = x2 - hi.astype(F32)
    mid = r1.astype(BF16)
    lo = (r1 - mid.astype(F32)).astype(BF16)
    outs = []
    for j in range(x2.shape[1] // LANES):
        sl = slice(j * LANES, (j + 1) * LANES)
        f = lambda p: jnp.dot(p[:, sl], gmat, preferred_element_type=F32)
        outs.append(f(hi) + f(mid) + f(lo))
    return outs[0] if len(outs) == 1 else jnp.concatenate(outs, axis=1)


def _swap32(x):
    w = x.shape[1]
    lane = lax.broadcasted_iota(jnp.int32, (1, w), 1)
    return jnp.where((lane & 32) == 0, pltpu.roll(x, w - 32, 1), pltpu.roll(x, 32, 1))


def _tile_lanes(t, w):
    reps = w // t.shape[1]
    return t if reps == 1 else jnp.concatenate([t] * reps, axis=1)


def _headrope_fwd(name, raw, w, gain, cos, sin, gmat, lp):
    rows = raw.shape[0]
    tm = _row_tile(lp)
    per = lp // tm

    def body(step, x_ref, c_ref, s_ref, g_ref, gm_ref, o_ref):
        x = x_ref[...]
        rstd = lax.rsqrt(_gmean64(x * x, gm_ref[...]) + EPS)
        z = x * rstd * g_ref[...]
        o_ref[...] = z * _tile_lanes(c_ref[...], w) + _swap32(z) * _tile_lanes(s_ref[...], w)

    maps = [((tm, w), lambda i: (i, 0)), ((tm, LANES), lambda i: (i % per, 0)), ((tm, LANES), lambda i: (i % per, 0))]
    return _rowcall(name, body, rows, [raw, cos, sin], [gain, gmat], [(w, F32)], tm=tm, row_in_maps=maps)[0]


def _headrope_bwd(name, raw, w, dout, gain, cos, sin, gmat, lp):
    rows = raw.shape[0]
    tm = _row_tile(lp)
    per = lp // tm

    def body(step, x_ref, do_ref, c_ref, s_ref, g_ref, gm_ref, dx_ref, dg_ref):
        x, dout_x, gx, gm = x_ref[...], do_ref[...], g_ref[...], gm_ref[...]
        rstd = lax.rsqrt(_gmean64(x * x, gm) + EPS)
        yn = x * rstd
        dz = dout_x * _tile_lanes(c_ref[...], w) + _swap32(dout_x * _tile_lanes(s_ref[...], w))
        dyn = dz * gx
        dx_ref[...] = rstd * (dyn - yn * _gmean64(dyn * yn, gm))
        dg = jnp.sum(dz * yn, axis=0, keepdims=True)
        sh = w // 2
        while sh >= HEAD_DIM:
            dg = dg + pltpu.roll(dg, sh, 1)
            sh //= 2
        _acc(step, dg_ref, dg)

    maps = [((tm, w), lambda i: (i, 0)), None, ((tm, LANES), lambda i: (i % per, 0)), ((tm, LANES), lambda i: (i % per, 0))]
    return _rowcall(name, body, rows, [raw, dout, cos, sin], [gain, gmat], [(w, F32)], [((1, w), F32)],
                    tm=tm, row_in_maps=maps)


KVW = N_KV_HEADS * HEAD_DIM
QB = 128


def _fold4(x):
    y = x + pltpu.roll(x, 128, 1)
    return y + pltpu.roll(y, 64, 1)


ATTN_SCALE = HEAD_DIM ** -0.5


def _attn_masks(i):
    k0j = lax.broadcasted_iota(jnp.int32, (Q_PER_KV * QB, QB), 1)
    qi = lax.broadcasted_iota(jnp.int32, (Q_PER_KV * QB, 2 * QB), 0) % QB
    kj = lax.broadcasted_iota(jnp.int32, (Q_PER_KV * QB, 2 * QB), 1)
    in_prev = (kj < QB) & (kj > qi) & (i >= 2)
    in_cur = (kj >= QB) & (kj - QB <= qi)
    return k0j >= META0, in_prev | in_cur


def _attn_scores(i, q_ref, k0_ref, kp_ref, kc_ref, sink_ref, h):
    masks = _attn_masks(i)
    lane = lax.broadcasted_iota(jnp.int32, (1, KVW), 1) // HEAD_DIM
    qh = q_ref[:, h * KVW:(h + 1) * KVW]
    qs = jnp.concatenate([jnp.where(lane == g, qh, 0.0) for g in range(Q_PER_KV)], axis=0).astype(BF16)
    hsel = lane == h
    kx = _expand_kv((k0_ref, kp_ref, kc_ref), hsel)
    s0 = jnp.where(masks[0], _dot_nt(qs, kx[0]) * ATTN_SCALE, NEG_INF)
    sb = jnp.where(masks[1], _dot_nt(qs, kx[1]) * ATTN_SCALE, NEG_INF)
    rowg = lax.broadcasted_iota(jnp.int32, (Q_PER_KV * QB, 1), 0) // QB
    sink = jnp.zeros((Q_PER_KV * QB, 1), F32)
    for g in range(Q_PER_KV):
        sink = jnp.where(rowg == g, sink_ref[0, h * Q_PER_KV + g], sink)
    m = jnp.maximum(jnp.maximum(jnp.max(s0, axis=1, keepdims=True), jnp.max(sb, axis=1, keepdims=True)), sink)
    p0, pb, ps = jnp.exp(s0 - m), jnp.exp(sb - m), jnp.exp(sink - m)
    den = jnp.sum(p0, axis=1, keepdims=True) + jnp.sum(pb, axis=1, keepdims=True) + ps
    return qs, kx, (p0, pb), ps, den, lane, hsel


def _expand_kv(refs, hsel):
    x0, xp, xc = [_fold4(jnp.where(hsel, r[...], 0.0)).astype(BF16) for r in refs]
    return [x0, jnp.concatenate([xp, xc], axis=0)]


def _unstack(x, lane):
    out = jnp.where(lane == 0, x[0:QB], 0.0)
    for g in range(1, Q_PER_KV):
        out = out + jnp.where(lane == g, x[g * QB:(g + 1) * QB], 0.0)
    return out


def _attn_specs(nb, d):
    qspec = pl.BlockSpec((None, QB, d), lambda b, i: (b, i, 0))
    k0 = pl.BlockSpec((None, QB, KVW), lambda b, i: (b, 0, 0))
    kp = pl.BlockSpec((None, QB, KVW), lambda b, i: (b, jnp.maximum(i - 1, 0), 0))
    kc = pl.BlockSpec((None, QB, KVW), lambda b, i: (b, i, 0))
    v0 = pl.BlockSpec((None, QB, KVW), lambda b, i: (b, 0, 1))
    vp = pl.BlockSpec((None, QB, KVW), lambda b, i: (b, jnp.maximum(i - 1, 0), 1))
    vc = pl.BlockSpec((None, QB, KVW), lambda b, i: (b, i, 1))
    sink = pl.BlockSpec(memory_space=pltpu.SMEM)
    return qspec, [k0, kp, kc], [v0, vp, vc], sink


def _attn_fwd(q, k, kv, sinks):
    bsz, lp, d = q.shape
    nb = lp // QB
    qspec, kspecs, vspecs, sspec = _attn_specs(nb, d)

    def kern(q_ref, k0_ref, kp_ref, kc_ref, v0_ref, vp_ref, vc_ref, sink_ref, o_ref):
        i = pl.program_id(1)
        for h in range(N_KV_HEADS):
            qs, kx, ps3, psink, den, lane, hsel = _attn_scores(i, q_ref, k0_ref, kp_ref, kc_ref, sink_ref, h)
            vx = _expand_kv((v0_ref, vp_ref, vc_ref), hsel)
            o = _dot(ps3[0], vx[0]) + _dot(ps3[1], vx[1])
            o_ref[:, h * KVW:(h + 1) * KVW] = _unstack(o * (1.0 / den), lane)

    return pl.pallas_call(
        kern, name="attn_fwd", grid=(bsz, nb),
        in_specs=[qspec] + kspecs + vspecs + [sspec],
        out_specs=qspec, out_shape=jax.ShapeDtypeStruct((bsz, lp, d), F32),
        compiler_params=_cparams(("arbitrary", "arbitrary")),
    )(q, k, k, k, kv, kv, kv, sinks)


def _attn_bwd(q, k, kv, sinks, o, do):
    bsz, lp, d = q.shape
    nb = lp // QB
    qspec, kspecs, vspecs, sspec = _attn_specs(nb, d)
    full = pl.BlockSpec((None, lp, KVW), lambda b, i: (b, 0, 0))

    def kern(q_ref, k0_ref, kp_ref, kc_ref, v0_ref, vp_ref, vc_ref, sink_ref, o_ref, do_ref,
             dq_ref, dk_ref, dv_ref, ds_ref):
        b, i = pl.program_id(0), pl.program_id(1)

        @pl.when(i == 0)
        def _():
            dk_ref[...] = jnp.zeros_like(dk_ref)
            dv_ref[...] = jnp.zeros_like(dv_ref)

        @pl.when((b == 0) & (i == 0))
        def _():
            ds_ref[...] = jnp.zeros_like(ds_ref)

        lane128 = lax.broadcasted_iota(jnp.int32, (1, LANES), 1)
        rowg = lax.broadcasted_iota(jnp.int32, (Q_PER_KV * QB, 1), 0) // QB
        dk_acc = [jnp.zeros((QB, KVW), F32), jnp.zeros((2 * QB, KVW), F32)]
        dv_acc = [jnp.zeros((QB, KVW), F32), jnp.zeros((2 * QB, KVW), F32)]
        dsink = jnp.zeros((1, LANES), F32)
        for h in range(N_KV_HEADS):
            qs, kx, ps3, psink, den, lane, hsel = _attn_scores(i, q_ref, k0_ref, kp_ref, kc_ref, sink_ref, h)
            vx = _expand_kv((v0_ref, vp_ref, vc_ref), hsel)
            sl = slice(h * KVW, (h + 1) * KVW)
            doh, oh = do_ref[:, sl], o_ref[:, sl]
            dos = jnp.concatenate([jnp.where(lane == g, doh, 0.0) for g in range(Q_PER_KV)], axis=0)
            ost = jnp.concatenate([jnp.where(lane == g, oh, 0.0) for g in range(Q_PER_KV)], axis=0)
            delta = jnp.sum(dos * ost, axis=1, keepdims=True)
            inv = 1.0 / den
            dosb = dos.astype(BF16)
            dqs = jnp.zeros((Q_PER_KV * QB, KVW), F32)
            for n in range(2):
                pn = ps3[n] * inv
                ds = pn * (_dot_nt(dosb, vx[n]) - delta) * ATTN_SCALE
                dqs = dqs + _dot(ds, kx[n])
                dk_acc[n] = dk_acc[n] + jnp.where(hsel, _fold4(_dot_tn(ds, qs)), 0.0)
                dv_acc[n] = dv_acc[n] + jnp.where(hsel, _fold4(_dot_tn(pn, dosb)), 0.0)
            dq_ref[:, sl] = _unstack(dqs, lane)
            dsk = -(psink * inv) * delta
            for g in range(Q_PER_KV):
                val = jnp.sum(jnp.where(rowg == g, dsk, 0.0), axis=0, keepdims=True)
                dsink = dsink + jnp.where(lane128 == h * Q_PER_KV + g, val, 0.0)
        ds_ref[...] += dsink
        r0 = pl.ds(0, QB)
        rp = pl.ds(pl.multiple_of(jnp.maximum(i - 1, 0) * QB, QB), QB)
        rc = pl.ds(pl.multiple_of(i * QB, QB), QB)
        for acc, ref in ((dk_acc, dk_ref), (dv_acc, dv_ref)):
            ref[r0, :] += acc[0]
            ref[rp, :] += acc[1][:QB]
            ref[rc, :] += acc[1][QB:]

    return pl.pallas_call(
        kern, name="attn_bwd", grid=(bsz, nb),
        in_specs=[qspec] + kspecs + vspecs + [sspec, qspec, qspec],
        out_specs=[qspec, full, full, pl.BlockSpec((1, LANES), lambda b, i: (0, 0))],
        out_shape=[jax.ShapeDtypeStruct((bsz, lp, d), F32), jax.ShapeDtypeStruct((bsz, lp, KVW), F32),
                   jax.ShapeDtypeStruct((bsz, lp, KVW), F32), jax.ShapeDtypeStruct((1, LANES), F32)],
        compiler_params=_cparams(("arbitrary", "arbitrary")),
    )(q, k, k, k, kv, kv, kv, sinks, o, do)


def _concat_cols(name, a, b):
    rows = a.shape[0]

    def body(step, a_ref, b_ref, o_ref):
        o_ref[...] = jnp.concatenate([a_ref[...], b_ref[...]], axis=1)

    return _rowcall(name, body, rows, [a, b], [], [(a.shape[1] + b.shape[1], F32)], light=True)[0]


def _adamw(name, w, m, v, parts):
    rows, wd = w.shape
    n = parts.shape[0]
    tm = _stream_tile(rows, wd * (7 * 4 + n * parts.dtype.itemsize))

    def kern(w_ref, m_ref, v_ref, p_ref, g_ref, d_ref, m2_ref, v2_ref):
        g = p_ref[0].astype(F32)
        for k in range(1, n):
            g = g + p_ref[k].astype(F32)
        m2 = ADAM_B1 * m_ref[...] + (1.0 - ADAM_B1) * g
        v2 = ADAM_B2 * v_ref[...] + (1.0 - ADAM_B2) * (g * g)
        mh = m2 / (1.0 - ADAM_B1 ** ADAM_STEP)
        vh = v2 / (1.0 - ADAM_B2 ** ADAM_STEP)
        g_ref[...] = g
        d_ref[...] = -ADAM_LR * (mh / (jnp.sqrt(vh) + ADAM_EPS) + ADAM_WD * w_ref[...])
        m2_ref[...] = m2
        v2_ref[...] = v2

    spec = pl.BlockSpec((tm, wd), lambda i: (i, 0))
    sd = jax.ShapeDtypeStruct((rows, wd), F32)
    return pl.pallas_call(
        kern, name=name, grid=(rows // tm,),
        in_specs=[spec, spec, spec, pl.BlockSpec((n, tm, wd), lambda i: (0, i, 0))],
        out_specs=[spec] * 4, out_shape=[sd] * 4,
        compiler_params=_cparams(("arbitrary",)),
    )(w, m, v, parts)


def _pair_sum(name, parts, theirs, my_c):
    n, _, rows, wd = parts.shape
    tm = _stream_tile(rows, wd * (4 + 4 + 2))

    def kern(c_ref, a_ref, b_ref, o_ref):
        o_ref[...] = (a_ref[...] + b_ref[...]).astype(BF16)

    return pl.pallas_call(
        kern, name=name,
        grid_spec=pltpu.PrefetchScalarGridSpec(
            num_scalar_prefetch=1, grid=(n, rows // tm),
            in_specs=[pl.BlockSpec((None, None, tm, wd), lambda k, i, c: (k, c[0], i, 0)),
                      pl.BlockSpec((None, tm, wd), lambda k, i, c: (k, i, 0))],
            out_specs=pl.BlockSpec((None, tm, wd), lambda k, i, c: (k, i, 0))),
        out_shape=jax.ShapeDtypeStruct((n, rows, wd), BF16), compiler_params=_cparams(("arbitrary", "arbitrary")),
    )(my_c, parts, theirs)


MESH = pl.DeviceIdType.MESH
ANY = pl.BlockSpec(memory_space=pl.ANY)


def _place():
    x, y, c = lax.axis_index("x"), lax.axis_index("y"), lax.axis_index("c")
    return x, y, c, [(1 - x, y), (x, 1 - y), (1 - x, 1 - y)]


def _gather_rider(shards):
    n = len(shards)

    def copy(refs, a, k, block, to, own=False):
        x_refs, out_refs, (send_sems, recv_sems, _) = refs
        px, py, pc = block
        slot = out_refs[a].at[4 * px + 2 * py + pc]
        return pltpu.make_async_remote_copy(
            src_ref=x_refs[a] if own else slot, dst_ref=slot,
            send_sem=send_sems.at[a, k], recv_sem=recv_sems.at[a, k], device_id=to, device_id_type=MESH)

    def local(refs, a):
        x, y, c, _ = _place()
        return pltpu.make_async_copy(refs[0][a], refs[1][a].at[4 * x + 2 * y + c], refs[2][2].at[a])

    def first(refs):
        x, y, c, chips = _place()
        out = []
        for a in range(n):
            out.append(copy(refs, a, 0, (x, y, c), (x, y, 1 - c), own=True))
            out += [copy(refs, a, 1 + j, (x, y, c), (*chip, c), own=True) for j, chip in enumerate(chips)]
        return out

    def passed(refs):
        x, y, c, chips = _place()
        return [copy(refs, a, 4 + j, (*chip, c), (x, y, 1 - c)) for j, chip in enumerate(chips) for a in range(n)]

    def start(*refs):
        for a in range(n):
            local(refs, a).start()
        for cp in first(refs):
            cp.start()

    def mid(*refs):
        x, y, c, chips = _place()
        fwd = passed(refs)
        for j, chip in enumerate(chips):
            for a in range(n):
                copy(refs, a, 1 + j, (*chip, c), (x, y, c)).wait_recv()
                fwd[j * n + a].start()

    def finish(*refs):
        x, y, c, chips = _place()
        for a in range(n):
            copy(refs, a, 0, (x, y, 1 - c), (x, y, c)).wait_recv()
            for j, chip in enumerate(chips):
                copy(refs, a, 4 + j, (*chip, 1 - c), (x, y, c)).wait_recv()
        for cp in first(refs) + passed(refs):
            cp.wait_send()
        for a in range(n):
            local(refs, a).wait()

    return _Rider(list(shards), [jax.ShapeDtypeStruct((N_DEV,) + s.shape, s.dtype) for s in shards],
                  [pltpu.SemaphoreType.DMA((n, 7)), pltpu.SemaphoreType.DMA((n, 7)), pltpu.SemaphoreType.DMA((n,))],
                  start, mid, finish)


def _swap_rider(parts):
    n = len(parts)

    def copies(p_refs, out_refs, sems):
        x, y, c, _ = _place()
        return [pltpu.make_async_remote_copy(
            src_ref=p_refs[a].at[:, 1 - c], dst_ref=out_refs[a], send_sem=sems[0].at[a], recv_sem=sems[1].at[a],
            device_id=(x, y, 1 - c), device_id_type=MESH) for a in range(n)]

    def start(*refs):
        for cp in copies(*refs):
            cp.start()

    def finish(*refs):
        for cp in copies(*refs):
            cp.wait()

    return _Rider(list(parts), [jax.ShapeDtypeStruct((p.shape[0],) + p.shape[2:], p.dtype) for p in parts],
                  [pltpu.SemaphoreType.DMA((n,)), pltpu.SemaphoreType.DMA((n,))], start, None, finish)


def _scatter_rider(sums):
    n = len(sums)

    def copy(refs, a, j, block):
        s_refs, out_refs, (send_sems, recv_sems, _) = refs
        x, y, c, chips = _place()
        px, py = chips[j]
        return pltpu.make_async_remote_copy(
            src_ref=s_refs[a].at[2 * px + py], dst_ref=out_refs[a].at[block],
            send_sem=send_sems.at[a, j], recv_sem=recv_sems.at[a, j], device_id=(px, py, c), device_id_type=MESH)

    def local(refs, a):
        x, y, c, _ = _place()
        return pltpu.make_async_copy(refs[0][a].at[2 * x + y], refs[1][a].at[2 * x + y], refs[2][2].at[a])

    def sends(refs):
        x, y, c, _ = _place()
        return [copy(refs, a, j, 2 * x + y) for j in range(3) for a in range(n)]

    def start(*refs):
        for a in range(n):
            local(refs, a).start()
        for cp in sends(refs):
            cp.start()

    def finish(*refs):
        x, y, c, chips = _place()
        for j, (px, py) in enumerate(chips):
            for a in range(n):
                copy(refs, a, j, 2 * px + py).wait_recv()
        for cp in sends(refs):
            cp.wait_send()
        for a in range(n):
            local(refs, a).wait()

    return _Rider(list(sums), [jax.ShapeDtypeStruct(s.shape, s.dtype) for s in sums],
                  [pltpu.SemaphoreType.DMA((n, 3)), pltpu.SemaphoreType.DMA((n, 3)), pltpu.SemaphoreType.DMA((n,))],
                  start, None, finish)


BIG = (("ffn1_w_gate_up", 2), ("ffn1_w_down", 1), ("ffn2_w_gate_up", 2), ("ffn2_w_down", 1), ("ssm_w_in", 1),
       ("ssm_w_out", 2), ("w_kv", 0), ("attn_w_q", 1), ("attn_w_o", 1))
SMALL = ("ffn1_norm", "mix_norm", "ffn2_norm", "ssm_lambda_re", "ssm_lambda_im", "ssm_b_re", "ssm_b_im",
         "ssm_c_re", "ssm_c_im", "ssm_log_step", "kv_norm", "k_norm", "q_norm", "attn_sinks")
COLS = (("meta_tokens", 1), ("ssm_d", 1))
WEIGHTS = ("meta_tokens", "ffn1_norm", "ffn1_w_gate_up", "ffn1_w_down", "mix_norm", "ffn2_norm", "ffn2_w_gate_up",
           "ffn2_w_down", "ssm_w_in", "ssm_lambda_re", "ssm_lambda_im", "ssm_b_re", "ssm_b_im", "ssm_c_re",
           "ssm_c_im", "ssm_log_step", "ssm_d", "ssm_w_out", "kv_norm", "w_kv", "k_norm", "attn_w_q", "q_norm",
           "attn_sinks", "attn_w_o")


def _rows_of(a, width):
    n = math.prod(a.shape)
    if n % width == 0:
        r = a.reshape(n // width, width)
    else:
        assert n < width
        r = jnp.pad(a.reshape(1, n), ((0, 0), (0, width - n)))
    return jnp.pad(r, ((0, (-r.shape[0]) % 8), (0, 0)))


def _pack_small(arrs, width):
    return jnp.concatenate([_rows_of(a.astype(F32), width) for a in arrs], axis=0)


def _unpack_small(buf, shapes, width):
    out, off = [], 0
    for shp in shapes:
        n = math.prod(shp)
        r = max(n // width, 1)
        out.append(buf[off:off + r].reshape(shp) if n % width == 0 else buf[off, :n].reshape(shp))
        off += r + (-r) % 8
    return out


def _shape2d(shp):
    return (math.prod(shp[:-1]), shp[-1])


def _unshard(g, axis):
    g = jnp.moveaxis(g, 0, axis)
    shp = g.shape
    return g.reshape(shp[:axis] + (shp[axis] * shp[axis + 1],) + shp[axis + 2:])


def _shard(full, axis):
    shp = full.shape
    g = full.reshape(shp[:axis] + (N_DEV, shp[axis] // N_DEV) + shp[axis + 1:])
    return jnp.moveaxis(g, axis, 0)


def _blockdiag(blocks):
    g, r, c = blocks.shape
    eye = jnp.eye(g, dtype=blocks.dtype)
    return (eye[:, None, :, None] * blocks[:, :, None, :]).reshape(g * r, g * c)


def _diagblocks(full, g):
    r, c = full.shape[0] // g, full.shape[1] // g
    f = full.reshape(g, r, g, c)
    idx = jnp.arange(g)
    return f[idx, :, idx, :]


def kernel(x, meta_tokens, ffn1_norm, ffn1_w_gate_up, ffn1_w_down, mix_norm, ffn2_norm, ffn2_w_gate_up, ffn2_w_down, ssm_w_in, ssm_lambda_re, ssm_lambda_im, ssm_b_re, ssm_b_im, ssm_c_re, ssm_c_im, ssm_log_step, ssm_d, ssm_w_out, kv_norm, w_kv, k_norm, attn_w_q, q_norm, attn_sinks, attn_w_o, loss_target, m_meta_tokens, m_ffn1_norm, m_ffn1_w_gate_up, m_ffn1_w_down, m_mix_norm, m_ffn2_norm, m_ffn2_w_gate_up, m_ffn2_w_down, m_ssm_w_in, m_ssm_lambda_re, m_ssm_lambda_im, m_ssm_b_re, m_ssm_b_im, m_ssm_c_re, m_ssm_c_im, m_ssm_log_step, m_ssm_d, m_ssm_w_out, m_kv_norm, m_w_kv, m_k_norm, m_attn_w_q, m_q_norm, m_attn_sinks, m_attn_w_o, v_meta_tokens, v_ffn1_norm, v_ffn1_w_gate_up, v_ffn1_w_down, v_mix_norm, v_ffn2_norm, v_ffn2_w_gate_up, v_ffn2_w_down, v_ssm_w_in, v_ssm_lambda_re, v_ssm_lambda_im, v_ssm_b_re, v_ssm_b_im, v_ssm_c_re, v_ssm_c_im, v_ssm_log_step, v_ssm_d, v_ssm_w_out, v_kv_norm, v_w_kv, v_k_norm, v_attn_w_q, v_q_norm, v_attn_sinks, v_attn_w_o):
    args = dict(locals())
    W = {n: args[n] for n in WEIGHTS}
    M = {n: args["m_" + n] for n in WEIGHTS}
    V = {n: args["v_" + n] for n in WEIGHTS}
    my_x, my_y, my_c = (lax.axis_index(a) for a in MESH_AXES)
    my_dev = 4 * my_x + 2 * my_y + my_c

    big_names = [n for n, _ in BIG]
    s2d = {n: _shape2d(W[n].shape) for n in big_names}
    col_w = W["meta_tokens"].shape[1]

    grads, summed, small_parts = _local_step(x, loss_target, W, my_c.astype(jnp.int32).reshape(1))
    loss = lax.psum(grads.pop("loss"), MESH_AXES)
    grad_x = grads.pop("x")

    outs = [{}, {}, {}, {}]
    for n in big_names:
        r4 = _adamw("adamw_" + n, W[n].reshape(s2d[n]), M[n].reshape(s2d[n]), V[n].reshape(s2d[n]), summed[n])
        for k in range(4):
            outs[k][n] = r4[k].reshape(W[n].shape)

    small_names = list(SMALL) + [n for n, _ in COLS]
    small_shapes = [grads[n].shape for n in small_names]
    zero_cols = [jnp.zeros(grads[n].shape, F32) for n, _ in COLS]
    packs = lambda d: _pack_small([d[n] for n in SMALL] + zero_cols, PACK_W)
    r4 = _adamw("adamw_small", packs(W), packs(M), packs(V), small_parts)
    gsmall = None
    for k in range(4):
        un = dict(zip(small_names, _unpack_small(r4[k], small_shapes, PACK_W)))
        gsmall = un if k == 0 else gsmall
        outs[k].update({n: un[n] for n in SMALL})
    col_g = [lax.dynamic_slice_in_dim(gsmall[n], my_dev * W[n].shape[1], W[n].shape[1], axis=1) for n, _ in COLS]
    packc = lambda d: _pack_small([d[n] for n, _ in COLS], col_w)
    r4 = _adamw("adamw_cols", packc(W), packc(M), packc(V), _pack_small(col_g, col_w)[None])
    col_shapes = [W[n].shape for n, _ in COLS]
    for k in range(4):
        outs[k].update(dict(zip([n for n, _ in COLS], _unpack_small(r4[k], col_shapes, col_w))))

    res = [[outs[k][n] for n in WEIGHTS] for k in range(4)]
    return (loss, grad_x, *res[0], *res[1], *res[2], *res[3])


def _local_step(x, target, P, c_arr):
    bsz, seq, d = x.shape
    lp = seq + PAD
    rows = bsz * lp
    depth = P["ffn1_norm"].shape[0]
    assert depth == 2
    bf = lambda a: a.astype(BF16)
    row = lambda a: a.reshape(1, -1)

    def shard(n, l=None):
        a = P[n] if l is None else P[n][l]
        return bf(a.reshape(_shape2d(a.shape)))

    shard_t = lambda n, l: shard(n, l).T
    rowsharded = lambda g: g.reshape((g.shape[0] * g.shape[1],) + g.shape[2:])
    colsharded = lambda g: _unshard(g, 1)
    col_w = P["meta_tokens"].shape[1]
    g0 = _run_rider("gather_first", _gather_rider(
        [shard_t("ffn1_w_gate_up", 0), shard("ffn1_w_down", 0), shard("ssm_w_in", 0),
         _pack_small([P["meta_tokens"], P["ssm_d"]], col_w)]))
    ffn_w = {("ffn1", 0): (rowsharded(g0[0]), rowsharded(g0[1]))}
    w_in = rowsharded(g0[2])
    meta_full = _unshard(g0[3][:, :N_META], 1)
    dvec = _unshard(g0[3][:, N_META:N_META + 1, :P["ssm_d"].shape[1]], 1)

    pos = (jnp.arange(lp, dtype=F32) - float(META0))[:, None]
    half = HEAD_DIM // 2
    freqs = ROPE_THETA ** (-jnp.arange(0, half, dtype=F32) * 2.0 / HEAD_DIM)
    ang = pos * freqs[None, :]
    cos_t = jnp.tile(jnp.cos(ang), (1, LANES // half))
    sin_t = jnp.tile(jnp.concatenate([-jnp.sin(ang), jnp.sin(ang)], axis=1), (1, LANES // HEAD_DIM))
    gi = jnp.arange(LANES) // HEAD_DIM
    gmat = jnp.where(gi[:, None] == gi[None, :], 1.0 / HEAD_DIM, 0.0).astype(BF16)

    g_n, c_n, p_n = P["ssm_lambda_re"].shape[1], SSM_GROUP, SSM_STATE
    ns = g_n * p_n
    lr = P["ssm_lambda_re"][0].reshape(g_n, 1, p_n)
    li = P["ssm_lambda_im"][0].reshape(g_n, 1, p_n)
    ls = P["ssm_log_step"][0].reshape(g_n, 1, 1)
    brt = P["ssm_b_re"][0].transpose(0, 2, 1)
    bit = P["ssm_b_im"][0].transpose(0, 2, 1)
    ar, ai, bbr, bbi = _s5_params_fwd(lr, li, ls, brt, bit)
    a2 = jnp.concatenate([ar.reshape(1, ns), ai.reshape(1, ns)], axis=0)
    bfull = jnp.concatenate([_blockdiag(bbr), _blockdiag(bbi)], axis=1)
    cre_t = P["ssm_c_re"][0].transpose(0, 2, 1)
    cim_t = P["ssm_c_im"][0].transpose(0, 2, 1)
    cfull = jnp.concatenate([_blockdiag(cre_t), -_blockdiag(cim_t)], axis=0)

    ffn = lambda which, l: (row(P[which + "_norm"][l]),) + ffn_w[which, l]
    mix0, mix1, kvn = row(P["mix_norm"][0]), row(P["mix_norm"][1]), row(P["kv_norm"])
    kgain = jnp.tile(P["k_norm"].reshape(1, HEAD_DIM), (1, KVW // HEAD_DIM))
    qgain = jnp.tile(P["q_norm"].reshape(1, HEAD_DIM), (1, d // HEAD_DIM))
    sinks = P["attn_sinks"].reshape(1, -1)

    h0 = _embed(x, meta_full).reshape(rows, d)
    h1, ab_f1_0, g_wout, g_gu, g_d, g_kv = _ffn_fwd("ffn1_0_fwd", h0, *ffn("ffn1", 0), rider=_gather_rider(
        [shard("ssm_w_out", 0), shard_t("ffn2_w_gate_up", 0), shard("ffn2_w_down", 0), shard("w_kv")]))
    w_out, w_kv = colsharded(g_wout), rowsharded(g_kv)
    ffn_w["ffn2", 0] = (rowsharded(g_gu), rowsharded(g_d))
    u = _proj_fwd("ssm_in_fwd", h1, mix0, w_in)
    y, xs, u_perm = _s5_scan_fwd(u, bf(bfull), bf(cfull), a2, dvec, bsz)
    h2 = _glu_fwd(y, h1, w_out)
    h3, ab_f2_0, g_gu, g_d, g_q, g_o = _ffn_fwd("ffn2_0_fwd", h2, *ffn("ffn2", 0), rider=_gather_rider(
        [shard_t("ffn1_w_gate_up", 1), shard("ffn1_w_down", 1), shard("attn_w_q", 0), shard("attn_w_o", 0)]))
    w_q, w_o = rowsharded(g_q), rowsharded(g_o)
    ffn_w["ffn1", 1] = (rowsharded(g_gu), rowsharded(g_d))
    kv = _proj_fwd("kv_fwd", h3, kvn, w_kv)
    k = _headrope_fwd("k_rope_fwd", kv, KVW, kgain, cos_t, sin_t, gmat, lp)
    h4, ab_f1_1, g_gu, g_d = _ffn_fwd("ffn1_1_fwd", h3, *ffn("ffn1", 1), rider=_gather_rider(
        [shard_t("ffn2_w_gate_up", 1), shard("ffn2_w_down", 1)]))
    ffn_w["ffn2", 1] = (rowsharded(g_gu), rowsharded(g_d))
    q_raw = _proj_fwd("q_fwd", h4, mix1, w_q)
    q = _headrope_fwd("q_rope_fwd", q_raw, d, qgain, cos_t, sin_t, gmat, lp)
    r3 = lambda a: a.reshape(bsz, lp, a.shape[-1])
    o = _attn_fwd(r3(q), r3(k), r3(kv), sinks).reshape(rows, d)
    h5 = _lin_res_fwd("attn_out_fwd", o, w_o, h4)
    dh6, ab_f2_1, loss = _ffn_fwd("ffn2_1_fwd", h5, *ffn("ffn2", 1), loss_target=target.reshape(bsz * seq, d), lp=lp)

    G = {"loss": loss[0, 0]}

    def ffn_back(name, which, l, h, ab, dout, rider=None):
        g, wgu, wd = ffn(which, l)
        dh, hn, dab, act, dg, *rode = _ffn_bwd(name, h, ab, dout, g, wgu, wd, rider=rider)
        parts = [_shard(_mm_tn(name + "_wgu", hn, dab), 1), _mm_tn_slots(name + "_wd", act, dout, 0.5)]
        return dh, dg, parts, rode

    slots = lambda g: g.reshape((N_DEV, g.shape[0] // N_DEV) + g.shape[1:])
    swap_of = lambda parts: _swap_rider([p.reshape((4, 2) + p.shape[1:]) for p in parts])

    def pair_sums(tag, parts, theirs):
        return [_pair_sum("pair_sum_%s_%d" % (tag, k), p.reshape((4, 2) + p.shape[1:]), t, c_arr)
                for k, (p, t) in enumerate(zip(parts, theirs))]

    dh5, dg_f2_1, parts_a, _ = ffn_back("ffn2_1_bwd", "ffn2", 1, h5, ab_f2_1, dh6)
    do, dw_o, *theirs = _lin_bwd("attn_out_bwd", o, w_o, dh5, rider=swap_of(parts_a))
    sums_a = pair_sums("ffn2_1", parts_a, theirs)
    dq, dk, dv, dsinks = _attn_bwd(r3(q), r3(k), r3(kv), sinks, r3(o), r3(do))
    dq_raw, dqg = _headrope_bwd("q_rope_bwd", q_raw, d, dq.reshape(rows, d), qgain, cos_t, sin_t, gmat, lp)
    dh4, dg_mix1, dw_q = _proj_bwd("q_bwd", h4, mix1, w_q, dq_raw, dh5)
    dh3, dg_f1_1, parts_b, red_a = ffn_back("ffn1_1_bwd", "ffn1", 1, h3, ab_f1_1, dh4, rider=_scatter_rider(sums_a))
    dk_raw, dkg = _headrope_bwd("k_rope_bwd", kv, KVW, dk.reshape(rows, KVW), kgain, cos_t, sin_t, gmat, lp)
    parts_b = parts_b + [slots(dw_q), slots(dw_o)]
    dkv = _concat_cols("dkv_concat", dk_raw, dv.reshape(rows, KVW))
    dh3, dg_kvn, dw_kv, *theirs = _proj_bwd("kv_bwd", h3, kvn, w_kv, dkv, dh3, rider=swap_of(parts_b))
    sums_b = pair_sums("ffn1_1", parts_b, theirs)
    dh2, dg_f2_0, parts_c, red_b = ffn_back("ffn2_0_bwd", "ffn2", 0, h2, ab_f2_0, dh3, rider=_scatter_rider(sums_b))
    parts_c = parts_c + [slots(dw_kv)]
    dy, dw_out, *theirs = _glu_bwd(y, dh2, w_out, rider=swap_of(parts_c))
    sums_c = pair_sums("ffn2_0", parts_c, theirs)
    ctfull = jnp.concatenate([_blockdiag(P["ssm_c_re"][0]), -_blockdiag(P["ssm_c_im"][0])], axis=1)
    btfull = jnp.concatenate([_blockdiag(bbr.transpose(0, 2, 1)), _blockdiag(bbi.transpose(0, 2, 1))], axis=0)
    du, gx, dy_perm, da, dd = _s5_scan_bwd(dy, u, xs, bf(ctfull), bf(btfull), a2, dvec, bsz)
    dbfull = _mm_tn_blockdiag("ssm_db", u_perm, gx, False)
    dcfull = _mm_tn_blockdiag("ssm_dc", xs, dy_perm, True)
    dh1, dg_mix0, dw_in = _proj_bwd("ssm_in_bwd", h1, mix0, w_in, du, dh2)
    dh0, dg_f1_0, parts_d, red_c = ffn_back("ffn1_0_bwd", "ffn1", 0, h0, ab_f1_0, dh1, rider=_scatter_rider(sums_c))
    dbbr = _diagblocks(dbfull[:, :ns], g_n)
    dbbi = _diagblocks(dbfull[:, ns:], g_n)
    dlr, dli, dls, dbrt, dbit = _s5_params_bwd(lr, li, ls, brt, bit, da[:, :ns].reshape(g_n, 1, p_n),
                                               da[:, ns:].reshape(g_n, 1, p_n), dbbr, dbbi)
    dh0 = r3(dh0)
    G["x"] = dh0[:, PAD:, :]
    G["meta_tokens"] = _meta_sum(dh0)
    G["ffn1_norm"] = jnp.concatenate([dg_f1_0, dg_f1_1], axis=0)
    G["ffn2_norm"] = jnp.concatenate([dg_f2_0, dg_f2_1], axis=0)
    G["mix_norm"] = jnp.concatenate([dg_mix0, dg_mix1], axis=0)
    G["ssm_lambda_re"] = dlr.reshape(1, g_n, p_n)
    G["ssm_lambda_im"] = dli.reshape(1, g_n, p_n)
    G["ssm_log_step"] = dls.reshape(1, g_n)
    G["ssm_b_re"] = dbrt.transpose(0, 2, 1)[None]
    G["ssm_b_im"] = dbit.transpose(0, 2, 1)[None]
    G["ssm_c_re"] = _diagblocks(dcfull[:ns], g_n).transpose(0, 2, 1)[None]
    G["ssm_c_im"] = -_diagblocks(dcfull[ns:], g_n).transpose(0, 2, 1)[None]
    G["ssm_d"] = dd
    G["kv_norm"] = dg_kvn.reshape(-1)
    G["k_norm"] = dkg[0, :HEAD_DIM]
    G["q_norm"] = dqg[:, :HEAD_DIM]
    G["attn_sinks"] = dsinks[:, :N_KV_HEADS * Q_PER_KV]

    parts_d = parts_d + [slots(dw_in), dw_out]
    small_pack = _pack_small([G[n] for n in list(SMALL) + [n for n, _ in COLS]], PACK_W)
    *theirs, small_parts = _run_rider("grad_swap_last", _join_riders(swap_of(parts_d), _gather_rider([small_pack])))
    red_d = _run_rider("grad_scatter_last", _scatter_rider(pair_sums("last", parts_d, theirs)))
    both = lambda lo, hi: jnp.concatenate([lo, hi], axis=1)
    summed = {"ffn1_w_gate_up": both(red_d[0], red_b[0]), "ffn1_w_down": both(red_d[1], red_b[1]),
              "ffn2_w_gate_up": both(red_c[0], red_a[0]), "ffn2_w_down": both(red_c[1], red_a[1]),
              "ssm_w_in": red_d[2], "ssm_w_out": red_d[3], "w_kv": red_c[2], "attn_w_q": red_b[2],
              "attn_w_o": red_b[3]}
    return G, summed, small_parts
```

```python
import functools
import math

import jax
import jax.numpy as jnp
from jax import lax
from jax.experimental import pallas as pl
from jax.experimental.pallas import tpu as pltpu

F32 = jnp.float32
BF16 = jnp.bfloat16

N_META = 16
PAD = 128
META0 = PAD - N_META
HEAD_DIM = 64
N_KV_HEADS = 4
Q_PER_KV = 4
SSM_GROUP = 16
SSM_STATE = 64
EPS = 1e-6
NEG_INF = -1e30
ROPE_THETA = 10000.0
ADAM_LR, ADAM_B1, ADAM_B2, ADAM_EPS, ADAM_WD, ADAM_STEP = 0.001, 0.9, 0.999, 1e-08, 0.01, 10
LANES = 128
PACK_W = 1024
VMEM_LIMIT = 56 * 1024 * 1024
MESH_AXES = ("x", "y", "c")
N_DEV = 8


def _cparams(sem=None):
    return pltpu.CompilerParams(dimension_semantics=sem, vmem_limit_bytes=VMEM_LIMIT)


def _row_tile(rows, light=False):
    for tm in ((768,) if light else ()) + (384, 256, 128, 64, 32, 16, 8):
        if rows % tm == 0:
            return tm
    raise ValueError(rows)


STREAM_BUDGET = 32 * 1024 * 1024


def _stream_tile(rows, bytes_per_row):
    for tm in range(rows, 0, -1):
        if rows % tm == 0 and (tm % 16 == 0 or tm == rows) and 2 * tm * bytes_per_row <= STREAM_BUDGET:
            return tm
    raise ValueError(rows)


TN_BUDGET = 52 * 1024 * 1024
TN_MAX_ROWS = 2816


def _tn_tile(rows, a, b, k1, tn):
    sa, sb = a.dtype.itemsize, b.dtype.itemsize
    fits = lambda tm: 2 * tm * (k1 * sa + tn * sb) + 3 * k1 * tn * 4 + tm * k1 * 2 <= TN_BUDGET
    divisors = [tm for tm in range(min(rows, TN_MAX_ROWS), 7, -8) if rows % tm == 0 and fits(tm)]
    good = [tm for tm in divisors if -(-tm // MXU_DIM) * MXU_DIM <= 1.1 * tm]
    if good or divisors:
        return (good or divisors)[0]
    raise ValueError(rows)


def _dot(a, b):
    return jnp.dot(a.astype(BF16), b.astype(BF16), preferred_element_type=F32)


def _dot_nt(a, b):
    return lax.dot_general(a.astype(BF16), b.astype(BF16), (((1,), (1,)), ((), ())), preferred_element_type=F32)


def _dot_tn(a, b):
    return lax.dot_general(a.astype(BF16), b.astype(BF16), (((0,), (0,)), ((), ())), preferred_element_type=F32)


def _rms(x, g):
    rstd = lax.rsqrt(jnp.mean(x * x, axis=-1, keepdims=True) + EPS)
    y = x * rstd
    return y * g, y, rstd


def _rms_bwd(dhn, y, rstd, g):
    dyn = dhn * g
    dx = rstd * (dyn - y * jnp.mean(dyn * y, axis=-1, keepdims=True))
    return dx, jnp.sum(dhn * y, axis=0, keepdims=True)


def _sigmoid(x):
    return 1.0 / (1.0 + jnp.exp(-x))


_GELU_C = math.sqrt(2.0 / math.pi)


def _gelu(y):
    t = jnp.tanh(_GELU_C * (y + 0.044715 * y * y * y))
    return 0.5 * y * (1.0 + t), t


def _gelu_grad(y, t):
    return 0.5 * (1.0 + t) + 0.5 * y * (1.0 - t * t) * _GELU_C * (1.0 + 3.0 * 0.044715 * y * y)


class _Rider:
    def __init__(self, ins, outs, sems, start, mid, finish):
        self.ins, self.outs, self.sems, self.start, self.mid, self.finish = ins, outs, sems, start, mid, finish


def _join_riders(r1, r2):
    ni, no, ns = len(r1.ins), len(r1.outs), len(r1.sems)

    def both(f1, f2):
        def phase(ins, outs, sems):
            if f1 is not None:
                f1(ins[:ni], outs[:no], sems[:ns])
            if f2 is not None:
                f2(ins[ni:], outs[no:], sems[ns:])
        return phase

    mid = both(r1.mid, r2.mid) if (r1.mid is not None or r2.mid is not None) else None
    return _Rider(r1.ins + r2.ins, r1.outs + r2.outs, r1.sems + r2.sems,
                  both(r1.start, r2.start), mid, both(r1.finish, r2.finish))


def _run_rider(name, rider):
    def kern(*refs):
        ni, no = len(rider.ins), len(rider.outs)
        parts = refs[:ni], refs[ni:ni + no], refs[ni + no:]
        rider.start(*parts)
        if rider.mid is not None:
            rider.mid(*parts)
        rider.finish(*parts)

    return pl.pallas_call(
        kern, name=name, out_shape=list(rider.outs), in_specs=[ANY] * len(rider.ins),
        out_specs=[ANY] * len(rider.outs), scratch_shapes=list(rider.sems),
    )(*rider.ins)


def _rowcall(name, body, rows, row_ins, const_ins, row_outs, acc_outs=(), tm=None, row_in_maps=None, rider=None,
             light=False):
    tm = tm or _row_tile(rows, light)
    steps = rows // tm
    in_specs = []
    for k, a in enumerate(row_ins):
        if row_in_maps is not None and row_in_maps[k] is not None:
            in_specs.append(pl.BlockSpec(*row_in_maps[k]))
        else:
            in_specs.append(pl.BlockSpec((tm, a.shape[1]), lambda i: (i, 0)))
    for a in const_ins:
        in_specs.append(pl.BlockSpec(a.shape, lambda i, nd=a.ndim: (0,) * nd, pipeline_mode=pl.Buffered(1)))
    out_shape, out_specs = [], []
    for w, dt in row_outs:
        out_shape.append(jax.ShapeDtypeStruct((rows, w), dt))
        out_specs.append(pl.BlockSpec((tm, w), lambda i: (i, 0)))
    for shp, dt in acc_outs:
        out_shape.append(jax.ShapeDtypeStruct(shp, dt))
        out_specs.append(pl.BlockSpec(shp, lambda i, nd=len(shp): (0,) * nd))

    if rider is None:
        def kern(*refs):
            body(pl.program_id(0), *refs)

        return pl.pallas_call(
            kern, name=name, grid=(steps,), in_specs=in_specs, out_specs=out_specs, out_shape=out_shape,
            compiler_params=_cparams(("arbitrary",)),
        )(*row_ins, *const_ins)

    n_in, n_out = len(in_specs), len(out_specs)
    r_in, r_out = len(rider.ins), len(rider.outs)

    def kern_r(*refs):
        step = pl.program_id(0)
        ins, rins = refs[:n_in], refs[n_in:n_in + r_in]
        outs = refs[n_in + r_in:n_in + r_in + n_out]
        routs = refs[n_in + r_in + n_out:n_in + r_in + n_out + r_out]
        sems = refs[n_in + r_in + n_out + r_out:]

        @pl.when(step == 0)
        def _():
            rider.start(rins, routs, sems)

        if rider.mid is not None:
            @pl.when(step == (3 * steps) // 4)
            def _():
                rider.mid(rins, routs, sems)

        body(step, *ins, *outs)

        @pl.when(step == steps - 1)
        def _():
            rider.finish(rins, routs, sems)

    return pl.pallas_call(
        kern_r, name=name, grid=(steps,), in_specs=in_specs + [ANY] * r_in, out_specs=out_specs + [ANY] * r_out,
        out_shape=out_shape + list(rider.outs), scratch_shapes=list(rider.sems),
        compiler_params=_cparams(("arbitrary",)),
    )(*row_ins, *const_ins, *rider.ins)


def _acc(step, ref, val):
    @pl.when(step == 0)
    def _():
        ref[...] = val

    @pl.when(step != 0)
    def _():
        ref[...] += val


def _embed(x, meta):
    bsz, seq, d = x.shape
    nb = seq // PAD + 1

    def kern(x_ref, m_ref, o_ref):
        i = pl.program_id(1)

        @pl.when(i == 0)
        def _():
            o_ref[0, 0:META0, :] = jnp.zeros((META0, d), F32)
            o_ref[0, META0:PAD, :] = m_ref[...]

        @pl.when(i != 0)
        def _():
            o_ref[0] = x_ref[0]

    return pl.pallas_call(
        kern, name="embed", grid=(bsz, nb),
        in_specs=[pl.BlockSpec((1, PAD, d), lambda b, i: (b, jnp.maximum(i - 1, 0), 0)),
                  pl.BlockSpec((N_META, d), lambda b, i: (0, 0))],
        out_specs=pl.BlockSpec((1, PAD, d), lambda b, i: (b, i, 0)),
        out_shape=jax.ShapeDtypeStruct((bsz, seq + PAD, d), F32),
        compiler_params=_cparams(("arbitrary", "arbitrary")),
    )(x, meta)


def _meta_sum(dh0):
    bsz, lp, d = dh0.shape

    def kern(d_ref, o_ref):
        _acc(pl.program_id(0), o_ref, d_ref[0, META0:PAD, :])

    return pl.pallas_call(
        kern, name="meta_sum", grid=(bsz,),
        in_specs=[pl.BlockSpec((1, PAD, d), lambda b: (b, 0, 0))],
        out_specs=pl.BlockSpec((N_META, d), lambda b: (0, 0)),
        out_shape=jax.ShapeDtypeStruct((N_META, d), F32),
        compiler_params=_cparams(("arbitrary",)),
    )(dh0)


MXU_DIM = 256


def _ffn_chunks(f):
    unit = MXU_DIM if f % MXU_DIM == 0 else LANES
    assert f % unit == 0
    first = (f // unit + 1) // 2 * unit
    return [(0, first), (first, f)] if first < f else [(0, f)]


def _ffn_fwd(name, h, g, wgu, wd, rider=None, loss_target=None, lp=None):
    rows, d = h.shape
    f = wd.shape[0]
    chunks = _ffn_chunks(f)
    tm = _row_tile(rows)
    nblk = tm // PAD

    def body(step, h_ref, *refs):
        t_refs, (g_ref, wgu_ref, wd_ref, o_ref, ab_ref), l_refs = refs[:nt], refs[nt:nt + 5], refs[nt + 5:]
        hx = h_ref[...]
        hb = _rms(hx, g_ref[...])[0].astype(BF16)
        acc = jnp.zeros(hx.shape, F32)
        for lo, hi in chunks:
            ga, ua = slice(lo, hi), slice(f + lo, f + hi)
            a = _dot_nt(hb, wgu_ref[ga, :])
            b = _dot_nt(hb, wgu_ref[ua, :])
            ab_ref[:, ga] = a.astype(BF16)
            ab_ref[:, ua] = b.astype(BF16)
            acc = acc + _dot(a * _sigmoid(a) * b, wd_ref[ga, :])
        out = hx + 0.5 * acc
        if not nt:
            o_ref[...] = out
            return
        err = out - jnp.concatenate([t[...] for t in t_refs], axis=0)
        rid = lax.broadcasted_iota(jnp.int32, (tm, 1), 0)
        err = jnp.where((step % per == 0) & (rid < PAD), 0.0, err)
        o_ref[...] = err * (1.0 / d)
        part = 0.5 * jnp.sum(jnp.mean(err * err, axis=-1, keepdims=True))
        _acc(step, l_refs[0], jnp.broadcast_to(part, (1, LANES)))

    if loss_target is None:
        nt = 0
        return _rowcall(name, body, rows, [h], [g, wgu, wd], [(d, F32), (2 * f, BF16)], rider=rider, tm=tm)
    assert tm % PAD == 0 and lp % tm == 0 and rider is None
    nt, per = nblk, lp // tm
    tblocks = (lp - PAD) // PAD

    def tmap(k):
        return lambda i: ((i // per) * tblocks + jnp.clip((i % per) * nblk - 1 + k, 0, tblocks - 1), 0)

    maps = [None] + [((PAD, d), tmap(k)) for k in range(nblk)]
    return _rowcall(name, body, rows, [h] + [loss_target] * nblk, [g, wgu, wd], [(d, F32), (2 * f, BF16)],
                    [((1, LANES), F32)], tm=tm, row_in_maps=maps)


def _ffn_bwd(name, h, ab, dout, g, wgu, wd, rider=None):
    rows, d = h.shape
    f = wd.shape[0]
    chunks = _ffn_chunks(f)

    def body(step, h_ref, ab_ref, do_ref, g_ref, wgu_ref, wd_ref, dh_ref, hn_ref, dab_ref, act_ref, dg_ref):
        hx, dout_x, gx = h_ref[...], do_ref[...], g_ref[...]
        hn, y, rstd = _rms(hx, gx)
        hn_ref[...] = hn.astype(BF16)
        dhalf = (0.5 * dout_x).astype(BF16)
        dhn = jnp.zeros(hx.shape, F32)
        for lo, hi in chunks:
            ga, ua = slice(lo, hi), slice(f + lo, f + hi)
            a = ab_ref[:, ga].astype(F32)
            b = ab_ref[:, ua].astype(F32)
            s = _sigmoid(a)
            silu = a * s
            act_ref[:, ga] = (silu * b).astype(BF16)
            dact = _dot_nt(dhalf, wd_ref[ga, :])
            da = (dact * b * (s + silu * (1.0 - s))).astype(BF16)
            db = (dact * silu).astype(BF16)
            dab_ref[:, ga] = da
            dab_ref[:, ua] = db
            dhn = dhn + _dot(da, wgu_ref[ga, :]) + _dot(db, wgu_ref[ua, :])
        dx, dg = _rms_bwd(dhn, y, rstd, gx)
        dh_ref[...] = dout_x + dx
        _acc(step, dg_ref, dg)

    return _rowcall(name, body, rows, [h, ab, dout], [g, wgu, wd],
                    [(d, F32), (d, BF16), (2 * f, BF16), (f, BF16)], [((1, d), F32)], rider=rider)


def _mm_tn(name, a, b):
    rows, k1 = a.shape
    k2 = b.shape[1]
    tk = max(t for t in range(LANES, k1 + 1, LANES) if k1 % t == 0 and (t * k2 * 4 <= 6 * 1024 * 1024 or t == LANES))
    tm = _tn_tile(rows, a, b, tk, k2)
    steps = rows // tm

    def kern(a_ref, b_ref, o_ref):
        _acc(pl.program_id(1), o_ref, _dot_tn(a_ref[...], b_ref[...]))

    return pl.pallas_call(
        kern, name=name, grid=(k1 // tk, steps),
        in_specs=[pl.BlockSpec((tm, tk), lambda j, i: (i, j)), pl.BlockSpec((tm, k2), lambda j, i: (i, 0))],
        out_specs=pl.BlockSpec((tk, k2), lambda j, i: (j, 0)),
        out_shape=jax.ShapeDtypeStruct((k1, k2), F32),
        compiler_params=_cparams(("arbitrary", "arbitrary")),
    )(a, b)


def _mm_tn_blockdiag(name, a, b, states_first):
    rows = a.shape[0]
    ka, kb = a.shape[1], b.shape[1]
    qa, qb = (ka // 4, kb // 2) if states_first else (ka // 2, kb // 4)
    tm = _tn_tile(rows, a, b, qa, qb)
    steps = rows // tm
    wide = lambda part, k: 2 * part + k
    amap = (lambda p, k, i: (i, wide(p, k))) if states_first else (lambda p, k, i: (i, k))
    bmap = (lambda p, k, i: (i, k)) if states_first else (lambda p, k, i: (i, wide(p, k)))
    omap = (lambda p, k, i: (wide(p, k), k)) if states_first else (lambda p, k, i: (k, wide(p, k)))

    def kern(a_ref, b_ref, o_ref):
        _acc(pl.program_id(2), o_ref, _dot_tn(a_ref[...], b_ref[...]))

    return pl.pallas_call(
        kern, name=name, grid=(2, 2, steps),
        in_specs=[pl.BlockSpec((tm, qa), amap), pl.BlockSpec((tm, qb), bmap)],
        out_specs=pl.BlockSpec((qa, qb), omap), out_shape=jax.ShapeDtypeStruct((ka, kb), F32),
        compiler_params=_cparams(("arbitrary", "arbitrary", "arbitrary")),
    )(a, b)


def _mm_tn_slots(name, a, b, scale):
    rows, k1 = a.shape
    k2 = b.shape[1]
    tn = 512 if k2 % 512 == 0 else k2
    sr = k1 // N_DEV
    tm = _tn_tile(rows, a, b, k1, tn)
    steps = rows // tm

    def kern(a_ref, b_ref, o_ref):
        bx = b_ref[...]
        if scale != 1.0:
            bx = bx * scale
        res = _dot_tn(a_ref[...], bx)
        step = pl.program_id(1)
        for s in range(N_DEV):
            _acc(step, o_ref.at[s], res[s * sr:(s + 1) * sr])

    return pl.pallas_call(
        kern, name=name, grid=(k2 // tn, steps),
        in_specs=[pl.BlockSpec((tm, k1), lambda j, i: (i, 0)), pl.BlockSpec((tm, tn), lambda j, i: (i, j))],
        out_specs=pl.BlockSpec((N_DEV, sr, tn), lambda j, i: (0, 0, j)),
        out_shape=jax.ShapeDtypeStruct((N_DEV, sr, k2), F32),
        compiler_params=_cparams(("arbitrary", "arbitrary")),
    )(a, b)


def _proj_fwd(name, h, g, w):
    rows = h.shape[0]

    def body(step, h_ref, g_ref, w_ref, o_ref):
        o_ref[...] = _dot(_rms(h_ref[...], g_ref[...])[0], w_ref[...])

    return _rowcall(name, body, rows, [h], [g, w], [(w.shape[1], F32)], light=True)[0]


def _proj_bwd(name, h, g, w, dy, dres, rider=None):
    rows, d = h.shape

    def body(step, h_ref, dy_ref, dr_ref, g_ref, w_ref, dh_ref, dg_ref, dw_ref):
        gx = g_ref[...]
        hn, y, rstd = _rms(h_ref[...], gx)
        dyx = dy_ref[...]
        dx, dg = _rms_bwd(_dot_nt(dyx, w_ref[...]), y, rstd, gx)
        dh_ref[...] = dr_ref[...] + dx
        _acc(step, dg_ref, dg)
        _acc(step, dw_ref, _dot_tn(hn, dyx))

    return _rowcall(name, body, rows, [h, dy, dres], [g, w], [(d, F32)], [((1, d), F32), (w.shape, F32)],
                    rider=rider, light=True)


def _lin_res_fwd(name, a, w, res):
    rows = a.shape[0]

    def body(step, a_ref, r_ref, w_ref, o_ref):
        o_ref[...] = r_ref[...] + _dot(a_ref[...], w_ref[...])

    return _rowcall(name, body, rows, [a, res], [w], [(w.shape[1], F32)], light=True)[0]


def _lin_bwd(name, a, w, dy, rider=None):
    rows, k = a.shape

    def body(step, a_ref, dy_ref, w_ref, da_ref, dw_ref):
        dyx = dy_ref[...]
        da_ref[...] = _dot_nt(dyx, w_ref[...])
        _acc(step, dw_ref, _dot_tn(a_ref[...], dyx))

    return _rowcall(name, body, rows, [a, dy], [w], [(k, F32)], [(w.shape, F32)], rider=rider, light=True)


def _s5_param_fn(lr, li, ls, brt, bit):
    step = jnp.exp(ls)
    mag = jnp.exp(lr * step)
    ar = mag * jnp.cos(li * step)
    ai = mag * jnp.sin(li * step)
    den = lr * lr + li * li
    nr, ni = ar - 1.0, ai
    cr = (nr * lr + ni * li) / den
    ci = (ni * lr - nr * li) / den
    return ar, ai, cr * brt - ci * bit, cr * bit + ci * brt


def _s5_params_fwd(lr, li, ls, brt, bit):
    def kern(lr_ref, li_ref, ls_ref, br_ref, bi_ref, ar_ref, ai_ref, bbr_ref, bbi_ref):
        ar, ai, bbr, bbi = _s5_param_fn(lr_ref[...], li_ref[...], ls_ref[...], br_ref[...], bi_ref[...])
        ar_ref[...], ai_ref[...], bbr_ref[...], bbi_ref[...] = ar, ai, bbr, bbi

    sd = jax.ShapeDtypeStruct
    return pl.pallas_call(
        kern, name="s5_params_fwd",
        out_shape=[sd(lr.shape, F32), sd(lr.shape, F32), sd(brt.shape, F32), sd(brt.shape, F32)],
    )(lr, li, ls, brt, bit)


def _s5_params_bwd(lr, li, ls, brt, bit, dar, dai, dbbr, dbbi):
    def kern(lr_ref, li_ref, ls_ref, br_ref, bi_ref, dar_ref, dai_ref, dbbr_ref, dbbi_ref,
             dlr_ref, dli_ref, dls_ref, dbr_ref, dbi_ref):
        _, vjp = jax.vjp(_s5_param_fn, lr_ref[...], li_ref[...], ls_ref[...], br_ref[...], bi_ref[...])
        dlr, dli, dls, dbr, dbi = vjp((dar_ref[...], dai_ref[...], dbbr_ref[...], dbbi_ref[...]))
        dlr_ref[...], dli_ref[...], dls_ref[...], dbr_ref[...], dbi_ref[...] = dlr, dli, dls, dbr, dbi

    sd = jax.ShapeDtypeStruct
    return pl.pallas_call(
        kern, name="s5_params_bwd",
        out_shape=[sd(lr.shape, F32), sd(lr.shape, F32), sd(ls.shape, F32), sd(brt.shape, F32), sd(brt.shape, F32)],
    )(lr, li, ls, brt, bit, dar, dai, dbbr, dbbi)


SCAN_LW = 512


SCAN_SEGS = 8
SCAN_UNROLL = 8


def _cmul(xr, xi, yr, yi):
    return xr * yr - xi * yi, xr * yi + xi * yr


def _scan_tables(a_ref, tab_ref, conj, seg_len):
    ns = a_ref.shape[1]
    ar = jnp.broadcast_to(a_ref[0:1, :], (8, ns))
    ai = jnp.broadcast_to(a_ref[1:2, :], (8, ns))
    if conj:
        ai = -ai
    big, base, e = None, (ar, ai), seg_len
    while e:
        if e & 1:
            big = base if big is None else _cmul(*big, *base)
        base = _cmul(*base, *base)
        e >>= 1
    big2 = _cmul(*big, *big)
    big4 = _cmul(*big2, *big2)
    for k, v in enumerate((ar, ai) + big + big2 + big4):
        tab_ref[k] = v


def _scan_block(x_ref, tab_ref, carry_ref, t_rows, ns, reverse):
    sl = t_rows // SCAN_SEGS
    assert sl % SCAN_UNROLL == 0
    row = lax.broadcasted_iota(jnp.int32, (8, SCAN_LW), 0)
    zero = jnp.zeros((8, SCAN_LW), F32)
    for lc in range(ns // SCAN_LW):
        lre = pl.ds(lc * SCAN_LW, SCAN_LW)
        lim = pl.ds(ns + lc * SCAN_LW, SCAN_LW)
        ar, ai = tab_ref[0, :, lre], tab_ref[1, :, lre]

        def rows_of(k, u):
            j = k * SCAN_UNROLL + u
            return pl.ds(pl.multiple_of(((sl - 1 - j) if reverse else j) * SCAN_SEGS, SCAN_SEGS), SCAN_SEGS)

        def local(k, s, lre=lre, lim=lim, ar=ar, ai=ai):
            sr, si = s
            for u in range(SCAN_UNROLL):
                rows = rows_of(k, u)
                tr, ti = _cmul(ar, ai, sr, si)
                sr, si = x_ref[rows, lre] + tr, x_ref[rows, lim] + ti
                x_ref[rows, lre], x_ref[rows, lim] = sr, si
            return sr, si

        er, ei = lax.fori_loop(0, sl // SCAN_UNROLL, local, (zero, zero))
        if reverse:
            cr = jnp.where(row == 7, carry_ref[:, lre], pltpu.roll(er, 7, 0))
            ci = jnp.where(row == 7, carry_ref[:, lim], pltpu.roll(ei, 7, 0))
        else:
            cr = jnp.where(row == 0, carry_ref[:, lre], pltpu.roll(er, 1, 0))
            ci = jnp.where(row == 0, carry_ref[:, lim], pltpu.roll(ei, 1, 0))
        for lvl, dsh in enumerate((1, 2, 4)):
            pr, pi = tab_ref[2 + 2 * lvl, :, lre], tab_ref[3 + 2 * lvl, :, lre]
            if reverse:
                keep, shift = row < 8 - dsh, 8 - dsh
            else:
                keep, shift = row >= dsh, dsh
            sr = jnp.where(keep, pltpu.roll(cr, shift, 0), 0.0)
            si = jnp.where(keep, pltpu.roll(ci, shift, 0), 0.0)
            tr, ti = _cmul(pr, pi, sr, si)
            cr, ci = cr + tr, ci + ti
        tr, ti = _cmul(tab_ref[2, :, lre], tab_ref[3, :, lre], cr, ci)
        edge = 0 if reverse else 7
        carry_ref[:, lre] = jnp.broadcast_to((er + tr)[edge:edge + 1, :], (8, SCAN_LW))
        carry_ref[:, lim] = jnp.broadcast_to((ei + ti)[edge:edge + 1, :], (8, SCAN_LW))

        def fix(k, t, lre=lre, lim=lim, ar=ar, ai=ai):
            tr, ti = t
            for u in range(SCAN_UNROLL):
                rows = rows_of(k, u)
                tr, ti = _cmul(ar, ai, tr, ti)
                x_ref[rows, lre] = x_ref[rows, lre] + tr
                x_ref[rows, lim] = x_ref[rows, lim] + ti
            return tr, ti

        lax.fori_loop(0, sl // SCAN_UNROLL, fix, (cr, ci))


def _bd_expand(u, w_ref, x_ref, ns):
    hh, sh = u.shape[1] // 2, ns // 2
    ub = u.astype(BF16)
    for part in range(2):
        for k in range(2):
            cols = slice(part * ns + k * sh, part * ns + (k + 1) * sh)
            x_ref[:, cols] = jnp.dot(ub[:, k * hh:(k + 1) * hh], w_ref[k * hh:(k + 1) * hh, cols],
                                     preferred_element_type=F32)


def _bd_contract(x_ref, w_ref, ns):
    hh, sh = w_ref.shape[1] // 2, ns // 2
    halves = []
    for k in range(2):
        acc = None
        for part in range(2):
            rows = slice(part * ns + k * sh, part * ns + (k + 1) * sh)
            t = jnp.dot(x_ref[:, rows].astype(BF16), w_ref[rows, k * hh:(k + 1) * hh], preferred_element_type=F32)
            acc = t if acc is None else acc + t
        halves.append(acc)
    return jnp.concatenate(halves, axis=1)


def _scan_rows(lp):
    for t in (384, 256, 128):
        if lp % t == 0:
            return t
    raise ValueError(lp)


def _seg_perm(t_rows):
    r = jnp.arange(t_rows)
    src = (r % SCAN_SEGS) * (t_rows // SCAN_SEGS) + r // SCAN_SEGS
    p = (src[:, None] == r[None, :]).astype(BF16)
    return p, p.T


def _permute_rows(p_ref, v):
    return jnp.dot(p_ref[...], v.astype(BF16), preferred_element_type=F32)


def _unpermute_rows(pt_ref, v):
    hi = v.astype(BF16)
    lo = (v - hi.astype(F32)).astype(BF16)
    pt = pt_ref[...]
    return jnp.dot(pt, hi, preferred_element_type=F32) + jnp.dot(pt, lo, preferred_element_type=F32)


def _s5_scan_fwd(u, bfull, cfull, a2, dvec, bsz):
    rows, hw = u.shape
    ns = a2.shape[1]
    lp = rows // bsz
    t_rows = _scan_rows(lp)
    nc = lp // t_rows
    pmat, pmat_t = _seg_perm(t_rows)

    def kern(u_ref, b_ref, c_ref, a_ref, d_ref, p_ref, pt_ref, y_ref, x_ref, up_ref, tab_ref, carry_ref):
        c = pl.program_id(1)

        @pl.when((pl.program_id(0) == 0) & (c == 0))
        def _():
            _scan_tables(a_ref, tab_ref, False, t_rows // SCAN_SEGS)

        @pl.when(c == 0)
        def _():
            carry_ref[...] = jnp.zeros_like(carry_ref)

        ux = u_ref[...]
        up = _permute_rows(p_ref, ux)
        up_ref[...] = up.astype(BF16)
        _bd_expand(up, b_ref, x_ref, ns)
        _scan_block(x_ref, tab_ref, carry_ref, t_rows, ns, reverse=False)
        y_ref[...] = _unpermute_rows(pt_ref, _bd_contract(x_ref, c_ref, ns)) + d_ref[...] * ux

    const = lambda shp: pl.BlockSpec(shp, lambda b, c: (0,) * len(shp), pipeline_mode=pl.Buffered(1))
    blk = lambda b, c: (b * nc + c, 0)
    return pl.pallas_call(
        kern, name="s5_scan_fwd", grid=(bsz, nc),
        in_specs=[pl.BlockSpec((t_rows, hw), blk), const(bfull.shape), const(cfull.shape), const(a2.shape),
                  const(dvec.shape), const(pmat.shape), const(pmat.shape)],
        out_specs=[pl.BlockSpec((t_rows, hw), blk), pl.BlockSpec((t_rows, 2 * ns), blk),
                   pl.BlockSpec((t_rows, hw), blk)],
        out_shape=[jax.ShapeDtypeStruct((rows, hw), F32), jax.ShapeDtypeStruct((rows, 2 * ns), F32),
                   jax.ShapeDtypeStruct((rows, hw), BF16)],
        scratch_shapes=[pltpu.VMEM((8, 8, ns), F32), pltpu.VMEM((8, 2 * ns), F32)],
        compiler_params=_cparams(("arbitrary", "arbitrary")),
    )(u, bfull, cfull, a2, dvec, pmat, pmat_t)


def _s5_scan_bwd(dy, u, xs, ctfull, btfull, a2, dvec, bsz):
    rows, hw = u.shape
    ns = a2.shape[1]
    lp = rows // bsz
    t_rows = _scan_rows(lp)
    nc = lp // t_rows
    blk = lambda b, c: (b * nc + (nc - 1 - c), 0)
    pmat, pmat_t = _seg_perm(t_rows)

    def prev8(b, c):
        first = (b * nc + (nc - 1 - c)) * (t_rows // 8)
        return (jnp.maximum(first - 1, 0), 0)

    def kern(dy_ref, u_ref, x_ref, xp_ref, ct_ref, bt_ref, a_ref, d_ref, p_ref, pt_ref,
             du_ref, gx_ref, dyp_ref, da_ref, dd_ref, tab_ref, carry_ref):
        b, c = pl.program_id(0), pl.program_id(1)
        first = (b == 0) & (c == 0)

        @pl.when(first)
        def _():
            _scan_tables(a_ref, tab_ref, True, t_rows // SCAN_SEGS)

        @pl.when(c == 0)
        def _():
            carry_ref[...] = jnp.zeros_like(carry_ref)

        dyx, ux = dy_ref[...], u_ref[...]
        dyp = _permute_rows(p_ref, dyx)
        dyp_ref[...] = dyp.astype(BF16)
        _bd_expand(dyp, ct_ref, gx_ref, ns)
        _scan_block(gx_ref, tab_ref, carry_ref, t_rows, ns, reverse=True)
        gx = gx_ref[...]
        du_ref[...] = _unpermute_rows(pt_ref, _bd_contract(gx_ref, bt_ref, ns)) + d_ref[...] * dyx
        seq_start = c == nc - 1
        row8 = lax.broadcasted_iota(jnp.int32, (8, 1), 0)
        head = pltpu.roll(x_ref[t_rows - 8:t_rows, :], 1, 0)
        head = jnp.where(row8 == 0, jnp.where(seq_start, 0.0, xp_ref[7:8, :]), head)
        xprev = jnp.concatenate([head, x_ref[0:t_rows - 8, :]], axis=0)
        xr, xi, gr, gi = xprev[:, :ns], xprev[:, ns:], gx[:, :ns], gx[:, ns:]
        da = jnp.concatenate([jnp.sum(xr * gr + xi * gi, axis=0, keepdims=True),
                              jnp.sum(xr * gi - xi * gr, axis=0, keepdims=True)], axis=1)
        dd = jnp.sum(dyx * ux, axis=0, keepdims=True)

        @pl.when(first)
        def _():
            da_ref[...] = da
            dd_ref[...] = dd

        @pl.when(jnp.logical_not(first))
        def _():
            da_ref[...] += da
            dd_ref[...] += dd

    const = lambda shp: pl.BlockSpec(shp, lambda b, c: (0,) * len(shp), pipeline_mode=pl.Buffered(1))
    return pl.pallas_call(
        kern, name="s5_scan_bwd", grid=(bsz, nc),
        in_specs=[pl.BlockSpec((t_rows, hw), blk), pl.BlockSpec((t_rows, hw), blk),
                  pl.BlockSpec((t_rows, 2 * ns), blk), pl.BlockSpec((8, 2 * ns), prev8),
                  const(ctfull.shape), const(btfull.shape), const(a2.shape), const(dvec.shape),
                  const(pmat.shape), const(pmat.shape)],
        out_specs=[pl.BlockSpec((t_rows, hw), blk), pl.BlockSpec((t_rows, 2 * ns), blk),
                   pl.BlockSpec((t_rows, hw), blk),
                   pl.BlockSpec((1, 2 * ns), lambda b, c: (0, 0)), pl.BlockSpec((1, hw), lambda b, c: (0, 0))],
        out_shape=[jax.ShapeDtypeStruct((rows, hw), F32), jax.ShapeDtypeStruct((rows, 2 * ns), F32),
                   jax.ShapeDtypeStruct((rows, hw), BF16),
                   jax.ShapeDtypeStruct((1, 2 * ns), F32), jax.ShapeDtypeStruct((1, hw), F32)],
        scratch_shapes=[pltpu.VMEM((8, 8, ns), F32), pltpu.VMEM((8, 2 * ns), F32)],
        compiler_params=_cparams(("arbitrary", "arbitrary")),
    )(dy, u, xs, xs, ctfull, btfull, a2, dvec, pmat, pmat_t)


def _glu_fwd(y, h1, wout):
    rows, d = h1.shape

    def body(step, y_ref, h_ref, w_ref, o_ref):
        z = _dot(_gelu(y_ref[...])[0], w_ref[...])
        o_ref[...] = h_ref[...] + z[:, :d] * _sigmoid(z[:, d:])

    return _rowcall("glu_fwd", body, rows, [y, h1], [wout], [(d, F32)], light=True)[0]


def _glu_bwd(y, dh2, wout, rider=None):
    rows, d = dh2.shape
    hw = y.shape[1]

    def body(step, y_ref, dh_ref, w_ref, dy_ref, dw_ref):
        yx, dh = y_ref[...], dh_ref[...]
        gl, t = _gelu(yx)
        z = _dot(gl, w_ref[...])
        za, sg = z[:, :d], _sigmoid(z[:, d:])
        dza = dh * sg
        dzg = dh * za * sg * (1.0 - sg)
        dgl = _dot_nt(dza, w_ref[:, :d]) + _dot_nt(dzg, w_ref[:, d:])
        dy_ref[...] = dgl * _gelu_grad(yx, t)
        for half, dz in enumerate((dza, dzg)):
            dw = _dot_tn(gl, dz)
            for s in range(N_DEV // 2):
                _acc(step, dw_ref.at[half * (N_DEV // 2) + s], dw[:, s * cw:(s + 1) * cw])

    cw = 2 * d // N_DEV
    return _rowcall("glu_bwd", body, rows, [y, dh2], [wout], [(hw, F32)], [((N_DEV, hw, cw), F32)], rider=rider,
                    light=True)


def _gmean64(x2, gmat):
    hi = x2.astype(BF16)
    r1 = x2 - hi.astype(F32)
    mid = r1.astype(BF16)
    lo = (r1 - mid.astype(F32)).astype(BF16)
    outs = []
    for j in range(x2.shape[1] // LANES):
        sl = slice(j * LANES, (j + 1) * LANES)
        f = lambda p: jnp.dot(p[:, sl], gmat, preferred_element_type=F32)
        outs.append(f(hi) + f(mid) + f(lo))
    return outs[0] if len(outs) == 1 else jnp.concatenate(outs, axis=1)


def _swap32(x):
    w = x.shape[1]
    lane = lax.broadcasted_iota(jnp.int32, (1, w), 1)
    return jnp.where((lane & 32) == 0, pltpu.roll(x, w - 32, 1), pltpu.roll(x, 32, 1))


def _tile_lanes(t, w):
    reps = w // t.shape[1]
    return t if reps == 1 else jnp.concatenate([t] * reps, axis=1)


def _headrope_fwd(name, raw, w, gain, cos, sin, gmat, lp):
    rows = raw.shape[0]
    tm = _row_tile(lp)
    per = lp // tm

    def body(step, x_ref, c_ref, s_ref, g_ref, gm_ref, o_ref):
        x = x_ref[...]
        rstd = lax.rsqrt(_gmean64(x * x, gm_ref[...]) + EPS)
        z = x * rstd * g_ref[...]
        o_ref[...] = z * _tile_lanes(c_ref[...], w) + _swap32(z) * _tile_lanes(s_ref[...], w)

    maps = [((tm, w), lambda i: (i, 0)), ((tm, LANES), lambda i: (i % per, 0)), ((tm, LANES), lambda i: (i % per, 0))]
    return _rowcall(name, body, rows, [raw, cos, sin], [gain, gmat], [(w, F32)], tm=tm, row_in_maps=maps)[0]


def _headrope_bwd(name, raw, w, dout, gain, cos, sin, gmat, lp):
    rows = raw.shape[0]
    tm = _row_tile(lp)
    per = lp // tm

    def body(step, x_ref, do_ref, c_ref, s_ref, g_ref, gm_ref, dx_ref, dg_ref):
        x, dout_x, gx, gm = x_ref[...], do_ref[...], g_ref[...], gm_ref[...]
        rstd = lax.rsqrt(_gmean64(x * x, gm) + EPS)
        yn = x * rstd
        dz = dout_x * _tile_lanes(c_ref[...], w) + _swap32(dout_x * _tile_lanes(s_ref[...], w))
        dyn = dz * gx
        dx_ref[...] = rstd * (dyn - yn * _gmean64(dyn * yn, gm))
        dg = jnp.sum(dz * yn, axis=0, keepdims=True)
        sh = w // 2
        while sh >= HEAD_DIM:
            dg = dg + pltpu.roll(dg, sh, 1)
            sh //= 2
        _acc(step, dg_ref, dg)

    maps = [((tm, w), lambda i: (i, 0)), None, ((tm, LANES), lambda i: (i % per, 0)), ((tm, LANES), lambda i: (i % per, 0))]
    return _rowcall(name, body, rows, [raw, dout, cos, sin], [gain, gmat], [(w, F32)], [((1, w), F32)],
                    tm=tm, row_in_maps=maps)


KVW = N_KV_HEADS * HEAD_DIM
QB = 128


def _fold4(x):
    y = x + pltpu.roll(x, 128, 1)
    return y + pltpu.roll(y, 64, 1)


ATTN_SCALE = HEAD_DIM ** -0.5


def _attn_masks(i):
    k0j = lax.broadcasted_iota(jnp.int32, (Q_PER_KV * QB, QB), 1)
    qi = lax.broadcasted_iota(jnp.int32, (Q_PER_KV * QB, 2 * QB), 0) % QB
    kj = lax.broadcasted_iota(jnp.int32, (Q_PER_KV * QB, 2 * QB), 1)
    in_prev = (kj < QB) & (kj > qi) & (i >= 2)
    in_cur = (kj >= QB) & (kj - QB <= qi)
    return k0j >= META0, in_prev | in_cur


def _attn_scores(i, q_ref, k0_ref, kp_ref, kc_ref, sink_ref, h):
    masks = _attn_masks(i)
    lane = lax.broadcasted_iota(jnp.int32, (1, KVW), 1) // HEAD_DIM
    qh = q_ref[:, h * KVW:(h + 1) * KVW]
    qs = jnp.concatenate([jnp.where(lane == g, qh, 0.0) for g in range(Q_PER_KV)], axis=0).astype(BF16)
    hsel = lane == h
    kx = _expand_kv((k0_ref, kp_ref, kc_ref), hsel)
    s0 = jnp.where(masks[0], _dot_nt(qs, kx[0]) * ATTN_SCALE, NEG_INF)
    sb = jnp.where(masks[1], _dot_nt(qs, kx[1]) * ATTN_SCALE, NEG_INF)
    rowg = lax.broadcasted_iota(jnp.int32, (Q_PER_KV * QB, 1), 0) // QB
    sink = jnp.zeros((Q_PER_KV * QB, 1), F32)
    for g in range(Q_PER_KV):
        sink = jnp.where(rowg == g, sink_ref[0, h * Q_PER_KV + g], sink)
    m = jnp.maximum(jnp.maximum(jnp.max(s0, axis=1, keepdims=True), jnp.max(sb, axis=1, keepdims=True)), sink)
    p0, pb, ps = jnp.exp(s0 - m), jnp.exp(sb - m), jnp.exp(sink - m)
    den = jnp.sum(p0, axis=1, keepdims=True) + jnp.sum(pb, axis=1, keepdims=True) + ps
    return qs, kx, (p0, pb), ps, den, lane, hsel


def _expand_kv(refs, hsel):
    x0, xp, xc = [_fold4(jnp.where(hsel, r[...], 0.0)).astype(BF16) for r in refs]
    return [x0, jnp.concatenate([xp, xc], axis=0)]


def _unstack(x, lane):
    out = jnp.where(lane == 0, x[0:QB], 0.0)
    for g in range(1, Q_PER_KV):
        out = out + jnp.where(lane == g, x[g * QB:(g + 1) * QB], 0.0)
    return out


def _attn_specs(nb, d):
    qspec = pl.BlockSpec((None, QB, d), lambda b, i: (b, i, 0))
    k0 = pl.BlockSpec((None, QB, KVW), lambda b, i: (b, 0, 0))
    kp = pl.BlockSpec((None, QB, KVW), lambda b, i: (b, jnp.maximum(i - 1, 0), 0))
    kc = pl.BlockSpec((None, QB, KVW), lambda b, i: (b, i, 0))
    v0 = pl.BlockSpec((None, QB, KVW), lambda b, i: (b, 0, 1))
    vp = pl.BlockSpec((None, QB, KVW), lambda b, i: (b, jnp.maximum(i - 1, 0), 1))
    vc = pl.BlockSpec((None, QB, KVW), lambda b, i: (b, i, 1))
    sink = pl.BlockSpec(memory_space=pltpu.SMEM)
    return qspec, [k0, kp, kc], [v0, vp, vc], sink


def _attn_fwd(q, k, kv, sinks):
    bsz, lp, d = q.shape
    nb = lp // QB
    qspec, kspecs, vspecs, sspec = _attn_specs(nb, d)

    def kern(q_ref, k0_ref, kp_ref, kc_ref, v0_ref, vp_ref, vc_ref, sink_ref, o_ref):
        i = pl.program_id(1)
        for h in range(N_KV_HEADS):
            qs, kx, ps3, psink, den, lane, hsel = _attn_scores(i, q_ref, k0_ref, kp_ref, kc_ref, sink_ref, h)
            vx = _expand_kv((v0_ref, vp_ref, vc_ref), hsel)
            o = _dot(ps3[0], vx[0]) + _dot(ps3[1], vx[1])
            o_ref[:, h * KVW:(h + 1) * KVW] = _unstack(o * (1.0 / den), lane)

    return pl.pallas_call(
        kern, name="attn_fwd", grid=(bsz, nb),
        in_specs=[qspec] + kspecs + vspecs + [sspec],
        out_specs=qspec, out_shape=jax.ShapeDtypeStruct((bsz, lp, d), F32),
        compiler_params=_cparams(("arbitrary", "arbitrary")),
    )(q, k, k, k, kv, kv, kv, sinks)


def _attn_bwd(q, k, kv, sinks, o, do):
    bsz, lp, d = q.shape
    nb = lp // QB
    qspec, kspecs, vspecs, sspec = _attn_specs(nb, d)
    full = pl.BlockSpec((None, lp, KVW), lambda b, i: (b, 0, 0))

    def kern(q_ref, k0_ref, kp_ref, kc_ref, v0_ref, vp_ref, vc_ref, sink_ref, o_ref, do_ref,
             dq_ref, dk_ref, dv_ref, ds_ref):
        b, i = pl.program_id(0), pl.program_id(1)

        @pl.when(i == 0)
        def _():
            dk_ref[...] = jnp.zeros_like(dk_ref)
            dv_ref[...] = jnp.zeros_like(dv_ref)

        @pl.when((b == 0) & (i == 0))
        def _():
            ds_ref[...] = jnp.zeros_like(ds_ref)

        lane128 = lax.broadcasted_iota(jnp.int32, (1, LANES), 1)
        rowg = lax.broadcasted_iota(jnp.int32, (Q_PER_KV * QB, 1), 0) // QB
        dk_acc = [jnp.zeros((QB, KVW), F32), jnp.zeros((2 * QB, KVW), F32)]
        dv_acc = [jnp.zeros((QB, KVW), F32), jnp.zeros((2 * QB, KVW), F32)]
        dsink = jnp.zeros((1, LANES), F32)
        for h in range(N_KV_HEADS):
            qs, kx, ps3, psink, den, lane, hsel = _attn_scores(i, q_ref, k0_ref, kp_ref, kc_ref, sink_ref, h)
            vx = _expand_kv((v0_ref, vp_ref, vc_ref), hsel)
            sl = slice(h * KVW, (h + 1) * KVW)
            doh, oh = do_ref[:, sl], o_ref[:, sl]
            dos = jnp.concatenate([jnp.where(lane == g, doh, 0.0) for g in range(Q_PER_KV)], axis=0)
            ost = jnp.concatenate([jnp.where(lane == g, oh, 0.0) for g in range(Q_PER_KV)], axis=0)
            delta = jnp.sum(dos * ost, axis=1, keepdims=True)
            inv = 1.0 / den
            dosb = dos.astype(BF16)
            dqs = jnp.zeros((Q_PER_KV * QB, KVW), F32)
            for n in range(2):
                pn = ps3[n] * inv
                ds = pn * (_dot_nt(dosb, vx[n]) - delta) * ATTN_SCALE
                dqs = dqs + _dot(ds, kx[n])
                dk_acc[n] = dk_acc[n] + jnp.where(hsel, _fold4(_dot_tn(ds, qs)), 0.0)
                dv_acc[n] = dv_acc[n] + jnp.where(hsel, _fold4(_dot_tn(pn, dosb)), 0.0)
            dq_ref[:, sl] = _unstack(dqs, lane)
            dsk = -(psink * inv) * delta
            for g in range(Q_PER_KV):
                val = jnp.sum(jnp.where(rowg == g, dsk, 0.0), axis=0, keepdims=True)
                dsink = dsink + jnp.where(lane128 == h * Q_PER_KV + g, val, 0.0)
        ds_ref[...] += dsink
        r0 = pl.ds(0, QB)
        rp = pl.ds(pl.multiple_of(jnp.maximum(i - 1, 0) * QB, QB), QB)
        rc = pl.ds(pl.multiple_of(i * QB, QB), QB)
        for acc, ref in ((dk_acc, dk_ref), (dv_acc, dv_ref)):
            ref[r0, :] += acc[0]
            ref[rp, :] += acc[1][:QB]
            ref[rc, :] += acc[1][QB:]

    return pl.pallas_call(
        kern, name="attn_bwd", grid=(bsz, nb),
        in_specs=[qspec] + kspecs + vspecs + [sspec, qspec, qspec],
        out_specs=[qspec, full, full, pl.BlockSpec((1, LANES), lambda b, i: (0, 0))],
        out_shape=[jax.ShapeDtypeStruct((bsz, lp, d), F32), jax.ShapeDtypeStruct((bsz, lp, KVW), F32),
                   jax.ShapeDtypeStruct((bsz, lp, KVW), F32), jax.ShapeDtypeStruct((1, LANES), F32)],
        compiler_params=_cparams(("arbitrary", "arbitrary")),
    )(q, k, k, k, kv, kv, kv, sinks, o, do)


def _concat_cols(name, a, b):
    rows = a.shape[0]

    def body(step, a_ref, b_ref, o_ref):
        o_ref[...] = jnp.concatenate([a_ref[...], b_ref[...]], axis=1)

    return _rowcall(name, body, rows, [a, b], [], [(a.shape[1] + b.shape[1], F32)], light=True)[0]


def _adamw(name, w, m, v, parts):
    rows, wd = w.shape
    n = parts.shape[0]
    tm = _stream_tile(rows, wd * (7 * 4 + n * parts.dtype.itemsize))

    def kern(w_ref, m_ref, v_ref, p_ref, g_ref, d_ref, m2_ref, v2_ref):
        g = p_ref[0].astype(F32)
        for k in range(1, n):
            g = g + p_ref[k].astype(F32)
        m2 = ADAM_B1 * m_ref[...] + (1.0 - ADAM_B1) * g
        v2 = ADAM_B2 * v_ref[...] + (1.0 - ADAM_B2) * (g * g)
        mh = m2 / (1.0 - ADAM_B1 ** ADAM_STEP)
        vh = v2 / (1.0 - ADAM_B2 ** ADAM_STEP)
        g_ref[...] = g
        d_ref[...] = -ADAM_LR * (mh / (jnp.sqrt(vh) + ADAM_EPS) + ADAM_WD * w_ref[...])
        m2_ref[...] = m2
        v2_ref[...] = v2

    spec = pl.BlockSpec((tm, wd), lambda i: (i, 0))
    sd = jax.ShapeDtypeStruct((rows, wd), F32)
    return pl.pallas_call(
        kern, name=name, grid=(rows // tm,),
        in_specs=[spec, spec, spec, pl.BlockSpec((n, tm, wd), lambda i: (0, i, 0))],
        out_specs=[spec] * 4, out_shape=[sd] * 4,
        compiler_params=_cparams(("arbitrary",)),
    )(w, m, v, parts)


def _pair_sum(name, parts, theirs, my_c):
    n, _, rows, wd = parts.shape
    tm = _stream_tile(rows, wd * (4 + 4 + 2))

    def kern(c_ref, a_ref, b_ref, o_ref):
        o_ref[...] = (a_ref[...] + b_ref[...]).astype(BF16)

    return pl.pallas_call(
        kern, name=name,
        grid_spec=pltpu.PrefetchScalarGridSpec(
            num_scalar_prefetch=1, grid=(n, rows // tm),
            in_specs=[pl.BlockSpec((None, None, tm, wd), lambda k, i, c: (k, c[0], i, 0)),
                      pl.BlockSpec((None, tm, wd), lambda k, i, c: (k, i, 0))],
            out_specs=pl.BlockSpec((None, tm, wd), lambda k, i, c: (k, i, 0))),
        out_shape=jax.ShapeDtypeStruct((n, rows, wd), BF16), compiler_params=_cparams(("arbitrary", "arbitrary")),
    )(my_c, parts, theirs)


MESH = pl.DeviceIdType.MESH
ANY = pl.BlockSpec(memory_space=pl.ANY)


def _place():
    x, y, c = lax.axis_index("x"), lax.axis_index("y"), lax.axis_index("c")
    return x, y, c, [(1 - x, y), (x, 1 - y), (1 - x, 1 - y)]


def _gather_rider(shards):
    n = len(shards)

    def copy(refs, a, k, block, to, own=False):
        x_refs, out_refs, (send_sems, recv_sems, _) = refs
        px, py, pc = block
        slot = out_refs[a].at[4 * px + 2 * py + pc]
        return pltpu.make_async_remote_copy(
            src_ref=x_refs[a] if own else slot, dst_ref=slot,
            send_sem=send_sems.at[a, k], recv_sem=recv_sems.at[a, k], device_id=to, device_id_type=MESH)

    def local(refs, a):
        x, y, c, _ = _place()
        return pltpu.make_async_copy(refs[0][a], refs[1][a].at[4 * x + 2 * y + c], refs[2][2].at[a])

    def first(refs):
        x, y, c, chips = _place()
        out = []
        for a in range(n):
            out.append(copy(refs, a, 0, (x, y, c), (x, y, 1 - c), own=True))
            out += [copy(refs, a, 1 + j, (x, y, c), (*chip, c), own=True) for j, chip in enumerate(chips)]
        return out

    def passed(refs):
        x, y, c, chips = _place()
        return [copy(refs, a, 4 + j, (*chip, c), (x, y, 1 - c)) for j, chip in enumerate(chips) for a in range(n)]

    def start(*refs):
        for a in range(n):
            local(refs, a).start()
        for cp in first(refs):
            cp.start()

    def mid(*refs):
        x, y, c, chips = _place()
        fwd = passed(refs)
        for j, chip in enumerate(chips):
            for a in range(n):
                copy(refs, a, 1 + j, (*chip, c), (x, y, c)).wait_recv()
                fwd[j * n + a].start()

    def finish(*refs):
        x, y, c, chips = _place()
        for a in range(n):
            copy(refs, a, 0, (x, y, 1 - c), (x, y, c)).wait_recv()
            for j, chip in enumerate(chips):
                copy(refs, a, 4 + j, (*chip, 1 - c), (x, y, c)).wait_recv()
        for cp in first(refs) + passed(refs):
            cp.wait_send()
        for a in range(n):
            local(refs, a).wait()

    return _Rider(list(shards), [jax.ShapeDtypeStruct((N_DEV,) + s.shape, s.dtype) for s in shards],
                  [pltpu.SemaphoreType.DMA((n, 7)), pltpu.SemaphoreType.DMA((n, 7)), pltpu.SemaphoreType.DMA((n,))],
                  start, mid, finish)


def _swap_rider(parts):
    n = len(parts)

    def copies(p_refs, out_refs, sems):
        x, y, c, _ = _place()
        return [pltpu.make_async_remote_copy(
            src_ref=p_refs[a].at[:, 1 - c], dst_ref=out_refs[a], send_sem=sems[0].at[a], recv_sem=sems[1].at[a],
            device_id=(x, y, 1 - c), device_id_type=MESH) for a in range(n)]

    def start(*refs):
        for cp in copies(*refs):
            cp.start()

    def finish(*refs):
        for cp in copies(*refs):
            cp.wait()

    return _Rider(list(parts), [jax.ShapeDtypeStruct((p.shape[0],) + p.shape[2:], p.dtype) for p in parts],
                  [pltpu.SemaphoreType.DMA((n,)), pltpu.SemaphoreType.DMA((n,))], start, None, finish)


def _scatter_rider(sums):
    n = len(sums)

    def copy(refs, a, j, block):
        s_refs, out_refs, (send_sems, recv_sems, _) = refs
        x, y, c, chips = _place()
        px, py = chips[j]
        return pltpu.make_async_remote_copy(
            src_ref=s_refs[a].at[2 * px + py], dst_ref=out_refs[a].at[block],
            send_sem=send_sems.at[a, j], recv_sem=recv_sems.at[a, j], device_id=(px, py, c), device_id_type=MESH)

    def local(refs, a):
        x, y, c, _ = _place()
        return pltpu.make_async_copy(refs[0][a].at[2 * x + y], refs[1][a].at[2 * x + y], refs[2][2].at[a])

    def sends(refs):
        x, y, c, _ = _place()
        return [copy(refs, a, j, 2 * x + y) for j in range(3) for a in range(n)]

    def start(*refs):
        for a in range(n):
            local(refs, a).start()
        for cp in sends(refs):
            cp.start()

    def finish(*refs):
        x, y, c, chips = _place()
        for j, (px, py) in enumerate(chips):
            for a in range(n):
                copy(refs, a, j, 2 * px + py).wait_recv()
        for cp in sends(refs):
            cp.wait_send()
        for a in range(n):
            local(refs, a).wait()

    return _Rider(list(sums), [jax.ShapeDtypeStruct(s.shape, s.dtype) for s in sums],
                  [pltpu.SemaphoreType.DMA((n, 3)), pltpu.SemaphoreType.DMA((n, 3)), pltpu.SemaphoreType.DMA((n,))],
                  start, None, finish)


BIG = (("ffn1_w_gate_up", 2), ("ffn1_w_down", 1), ("ffn2_w_gate_up", 2), ("ffn2_w_down", 1), ("ssm_w_in", 1),
       ("ssm_w_out", 2), ("w_kv", 0), ("attn_w_q", 1), ("attn_w_o", 1))
SMALL = ("ffn1_norm", "mix_norm", "ffn2_norm", "ssm_lambda_re", "ssm_lambda_im", "ssm_b_re", "ssm_b_im",
         "ssm_c_re", "ssm_c_im", "ssm_log_step", "kv_norm", "k_norm", "q_norm", "attn_sinks")
COLS = (("meta_tokens", 1), ("ssm_d", 1))
WEIGHTS = ("meta_tokens", "ffn1_norm", "ffn1_w_gate_up", "ffn1_w_down", "mix_norm", "ffn2_norm", "ffn2_w_gate_up",
           "ffn2_w_down", "ssm_w_in", "ssm_lambda_re", "ssm_lambda_im", "ssm_b_re", "ssm_b_im", "ssm_c_re",
           "ssm_c_im", "ssm_log_step", "ssm_d", "ssm_w_out", "kv_norm", "w_kv", "k_norm", "attn_w_q", "q_norm",
           "attn_sinks", "attn_w_o")


def _rows_of(a, width):
    n = math.prod(a.shape)
    if n % width == 0:
        r = a.reshape(n // width, width)
    else:
        assert n < width
        r = jnp.pad(a.reshape(1, n), ((0, 0), (0, width - n)))
    return jnp.pad(r, ((0, (-r.shape[0]) % 8), (0, 0)))


def _pack_small(arrs, width):
    return jnp.concatenate([_rows_of(a.astype(F32), width) for a in arrs], axis=0)


def _unpack_small(buf, shapes, width):
    out, off = [], 0
    for shp in shapes:
        n = math.prod(shp)
        r = max(n // width, 1)
        out.append(buf[off:off + r].reshape(shp) if n % width == 0 else buf[off, :n].reshape(shp))
        off += r + (-r) % 8
    return out


def _shape2d(shp):
    return (math.prod(shp[:-1]), shp[-1])


def _unshard(g, axis):
    g = jnp.moveaxis(g, 0, axis)
    shp = g.shape
    return g.reshape(shp[:axis] + (shp[axis] * shp[axis + 1],) + shp[axis + 2:])


def _blockdiag(blocks):
    g, r, c = blocks.shape
    eye = jnp.eye(g, dtype=blocks.dtype)
    return (eye[:, None, :, None] * blocks[:, :, None, :]).reshape(g * r, g * c)


def _diagblocks(full, g):
    r, c = full.shape[0] // g, full.shape[1] // g
    f = full.reshape(g, r, g, c)
    idx = jnp.arange(g)
    return f[idx, :, idx, :]


def kernel(x, meta_tokens, ffn1_norm, ffn1_w_gate_up, ffn1_w_down, mix_norm, ffn2_norm, ffn2_w_gate_up, ffn2_w_down, ssm_w_in, ssm_lambda_re, ssm_lambda_im, ssm_b_re, ssm_b_im, ssm_c_re, ssm_c_im, ssm_log_step, ssm_d, ssm_w_out, kv_norm, w_kv, k_norm, attn_w_q, q_norm, attn_sinks, attn_w_o, loss_target, m_meta_tokens, m_ffn1_norm, m_ffn1_w_gate_up, m_ffn1_w_down, m_mix_norm, m_ffn2_norm, m_ffn2_w_gate_up, m_ffn2_w_down, m_ssm_w_in, m_ssm_lambda_re, m_ssm_lambda_im, m_ssm_b_re, m_ssm_b_im, m_ssm_c_re, m_ssm_c_im, m_ssm_log_step, m_ssm_d, m_ssm_w_out, m_kv_norm, m_w_kv, m_k_norm, m_attn_w_q, m_q_norm, m_attn_sinks, m_attn_w_o, v_meta_tokens, v_ffn1_norm, v_ffn1_w_gate_up, v_ffn1_w_down, v_mix_norm, v_ffn2_norm, v_ffn2_w_gate_up, v_ffn2_w_down, v_ssm_w_in, v_ssm_lambda_re, v_ssm_lambda_im, v_ssm_b_re, v_ssm_b_im, v_ssm_c_re, v_ssm_c_im, v_ssm_log_step, v_ssm_d, v_ssm_w_out, v_kv_norm, v_w_kv, v_k_norm, v_attn_w_q, v_q_norm, v_attn_sinks, v_attn_w_o):
    args = dict(locals())
    W = {n: args[n] for n in WEIGHTS}
    M = {n: args["m_" + n] for n in WEIGHTS}
    V = {n: args["v_" + n] for n in WEIGHTS}
    my_x, my_y, my_c = (lax.axis_index(a) for a in MESH_AXES)
    my_dev = 4 * my_x + 2 * my_y + my_c

    big_names = [n for n, _ in BIG]
    s2d = {n: _shape2d(W[n].shape) for n in big_names}
    col_w = W["meta_tokens"].shape[1]

    grads, summed, small_parts = _local_step(x, loss_target, W, my_c.astype(jnp.int32).reshape(1))
    loss = lax.psum(grads.pop("loss"), MESH_AXES)
    grad_x = grads.pop("x")

    outs = [{}, {}, {}, {}]
    for n in big_names:
        r4 = _adamw("adamw_" + n, W[n].reshape(s2d[n]), M[n].reshape(s2d[n]), V[n].reshape(s2d[n]), summed[n])
        for k in range(4):
            outs[k][n] = r4[k].reshape(W[n].shape)

    small_names = list(SMALL) + [n for n, _ in COLS]
    small_shapes = [grads[n].shape for n in small_names]
    zero_cols = [jnp.zeros(grads[n].shape, F32) for n, _ in COLS]
    packs = lambda d: _pack_small([d[n] for n in SMALL] + zero_cols, PACK_W)
    r4 = _adamw("adamw_small", packs(W), packs(M), packs(V), small_parts)
    gsmall = None
    for k in range(4):
        un = dict(zip(small_names, _unpack_small(r4[k], small_shapes, PACK_W)))
        gsmall = un if k == 0 else gsmall
        outs[k].update({n: un[n] for n in SMALL})
    col_g = [lax.dynamic_slice_in_dim(gsmall[n], my_dev * W[n].shape[1], W[n].shape[1], axis=1) for n, _ in COLS]
    packc = lambda d: _pack_small([d[n] for n, _ in COLS], col_w)
    r4 = _adamw("adamw_cols", packc(W), packc(M), packc(V), _pack_small(col_g, col_w)[None])
    col_shapes = [W[n].shape for n, _ in COLS]
    for k in range(4):
        outs[k].update(dict(zip([n for n, _ in COLS], _unpack_small(r4[k], col_shapes, col_w))))

    res = [[outs[k][n] for n in WEIGHTS] for k in range(4)]
    return (loss, grad_x, *res[0], *res[1], *res[2], *res[3])


def _local_step(x, target, P, c_arr):
    bsz, seq, d = x.shape
    lp = seq + PAD
    rows = bsz * lp
    depth = P["ffn1_norm"].shape[0]
    assert depth == 2
    bf = lambda a: a.astype(BF16)
    row = lambda a: a.reshape(1, -1)

    def shard(n, l=None):
        a = P[n] if l is None else P[n][l]
        return bf(a.reshape(_shape2d(a.shape)))

    shard_t = lambda n, l: shard(n, l).T
    rowsharded = lambda g: g.reshape((g.shape[0] * g.shape[1],) + g.shape[2:])
    colsharded = lambda g: _unshard(g, 1)
    col_w = P["meta_tokens"].shape[1]
    g0 = _run_rider("gather_first", _gather_rider(
        [shard_t("ffn1_w_gate_up", 0), shard("ffn1_w_down", 0), shard("ssm_w_in", 0),
         _pack_small([P["meta_tokens"], P["ssm_d"]], col_w)]))
    ffn_w = {("ffn1", 0): (rowsharded(g0[0]), rowsharded(g0[1]))}
    w_in = rowsharded(g0[2])
    meta_full = _unshard(g0[3][:, :N_META], 1)
    dvec = _unshard(g0[3][:, N_META:N_META + 1, :P["ssm_d"].shape[1]], 1)

    pos = (jnp.arange(lp, dtype=F32) - float(META0))[:, None]
    half = HEAD_DIM // 2
    freqs = ROPE_THETA ** (-jnp.arange(0, half, dtype=F32) * 2.0 / HEAD_DIM)
    ang = pos * freqs[None, :]
    cos_t = jnp.tile(jnp.cos(ang), (1, LANES // half))
    sin_t = jnp.tile(jnp.concatenate([-jnp.sin(ang), jnp.sin(ang)], axis=1), (1, LANES // HEAD_DIM))
    gi = jnp.arange(LANES) // HEAD_DIM
    gmat = jnp.where(gi[:, None] == gi[None, :], 1.0 / HEAD_DIM, 0.0).astype(BF16)

    g_n, c_n, p_n = P["ssm_lambda_re"].shape[1], SSM_GROUP, SSM_STATE
    ns = g_n * p_n
    lr = P["ssm_lambda_re"][0].reshape(g_n, 1, p_n)
    li = P["ssm_lambda_im"][0].reshape(g_n, 1, p_n)
    ls = P["ssm_log_step"][0].reshape(g_n, 1, 1)
    brt = P["ssm_b_re"][0].transpose(0, 2, 1)
    bit = P["ssm_b_im"][0].transpose(0, 2, 1)
    ar, ai, bbr, bbi = _s5_params_fwd(lr, li, ls, brt, bit)
    a2 = jnp.concatenate([ar.reshape(1, ns), ai.reshape(1, ns)], axis=0)
    bfull = jnp.concatenate([_blockdiag(bbr), _blockdiag(bbi)], axis=1)
    cre_t = P["ssm_c_re"][0].transpose(0, 2, 1)
    cim_t = P["ssm_c_im"][0].transpose(0, 2, 1)
    cfull = jnp.concatenate([_blockdiag(cre_t), -_blockdiag(cim_t)], axis=0)

    ffn = lambda which, l: (row(P[which + "_norm"][l]),) + ffn_w[which, l]
    mix0, mix1, kvn = row(P["mix_norm"][0]), row(P["mix_norm"][1]), row(P["kv_norm"])
    kgain = jnp.tile(P["k_norm"].reshape(1, HEAD_DIM), (1, KVW // HEAD_DIM))
    qgain = jnp.tile(P["q_norm"].reshape(1, HEAD_DIM), (1, d // HEAD_DIM))
    sinks = P["attn_sinks"].reshape(1, -1)

    h0 = _embed(x, meta_full).reshape(rows, d)
    h1, ab_f1_0, g_wout, g_gu, g_d, g_kv = _ffn_fwd("ffn1_0_fwd", h0, *ffn("ffn1", 0), rider=_gather_rider(
        [shard("ssm_w_out", 0), shard_t("ffn2_w_gate_up", 0), shard("ffn2_w_down", 0), shard("w_kv")]))
    w_out, w_kv = colsharded(g_wout), rowsharded(g_kv)
    ffn_w["ffn2", 0] = (rowsharded(g_gu), rowsharded(g_d))
    u = _proj_fwd("ssm_in_fwd", h1, mix0, w_in)
    y, xs, u_perm = _s5_scan_fwd(u, bf(bfull), bf(cfull), a2, dvec, bsz)
    h2 = _glu_fwd(y, h1, w_out)
    h3, ab_f2_0, g_gu, g_d, g_q, g_o = _ffn_fwd("ffn2_0_fwd", h2, *ffn("ffn2", 0), rider=_gather_rider(
        [shard_t("ffn1_w_gate_up", 1), shard("ffn1_w_down", 1), shard("attn_w_q", 0), shard("attn_w_o", 0)]))
    w_q, w_o = rowsharded(g_q), rowsharded(g_o)
    ffn_w["ffn1", 1] = (rowsharded(g_gu), rowsharded(g_d))
    kv = _proj_fwd("kv_fwd", h3, kvn, w_kv)
    k = _headrope_fwd("k_rope_fwd", kv, KVW, kgain, cos_t, sin_t, gmat, lp)
    h4, ab_f1_1, g_gu, g_d = _ffn_fwd("ffn1_1_fwd", h3, *ffn("ffn1", 1), rider=_gather_rider(
        [shard_t("ffn2_w_gate_up", 1), shard("ffn2_w_down", 1)]))
    ffn_w["ffn2", 1] = (rowsharded(g_gu), rowsharded(g_d))
    q_raw = _proj_fwd("q_fwd", h4, mix1, w_q)
    q = _headrope_fwd("q_rope_fwd", q_raw, d, qgain, cos_t, sin_t, gmat, lp)
    r3 = lambda a: a.reshape(bsz, lp, a.shape[-1])
    o = _attn_fwd(r3(q), r3(k), r3(kv), sinks).reshape(rows, d)
    h5 = _lin_res_fwd("attn_out_fwd", o, w_o, h4)
    dh6, ab_f2_1, loss = _ffn_fwd("ffn2_1_fwd", h5, *ffn("ffn2", 1), loss_target=target.reshape(bsz * seq, d), lp=lp)

    G = {"loss": loss[0, 0]}

    def ffn_back(name, which, l, h, ab, dout, rider=None):
        g, wgu, wd = ffn(which, l)
        dh, hn, dab, act, dg, *rode = _ffn_bwd(name, h, ab, dout, g, wgu, wd, rider=rider)
        dwgu_t = _mm_tn(name + "_wgu", dab, hn)
        parts = [slots(dwgu_t), _mm_tn_slots(name + "_wd", act, dout, 0.5)]
        return dh, dg, parts, rode

    slots = lambda g: g.reshape((N_DEV, g.shape[0] // N_DEV) + g.shape[1:])
    swap_of = lambda parts: _swap_rider([p.reshape((4, 2) + p.shape[1:]) for p in parts])

    def pair_sums(tag, parts, theirs):
        return [_pair_sum("pair_sum_%s_%d" % (tag, k), p.reshape((4, 2) + p.shape[1:]), t, c_arr)
                for k, (p, t) in enumerate(zip(parts, theirs))]

    dh5, dg_f2_1, parts_a, _ = ffn_back("ffn2_1_bwd", "ffn2", 1, h5, ab_f2_1, dh6)
    do, dw_o, *theirs = _lin_bwd("attn_out_bwd", o, w_o, dh5, rider=swap_of(parts_a))
    sums_a = pair_sums("ffn2_1", parts_a, theirs)
    dq, dk, dv, dsinks = _attn_bwd(r3(q), r3(k), r3(kv), sinks, r3(o), r3(do))
    dq_raw, dqg = _headrope_bwd("q_rope_bwd", q_raw, d, dq.reshape(rows, d), qgain, cos_t, sin_t, gmat, lp)
    dh4, dg_mix1, dw_q = _proj_bwd("q_bwd", h4, mix1, w_q, dq_raw, dh5)
    dh3, dg_f1_1, parts_b, red_a = ffn_back("ffn1_1_bwd", "ffn1", 1, h3, ab_f1_1, dh4, rider=_scatter_rider(sums_a))
    dk_raw, dkg = _headrope_bwd("k_rope_bwd", kv, KVW, dk.reshape(rows, KVW), kgain, cos_t, sin_t, gmat, lp)
    parts_b = parts_b + [slots(dw_q), slots(dw_o)]
    dkv = _concat_cols("dkv_concat", dk_raw, dv.reshape(rows, KVW))
    dh3, dg_kvn, dw_kv, *theirs = _proj_bwd("kv_bwd", h3, kvn, w_kv, dkv, dh3, rider=swap_of(parts_b))
    sums_b = pair_sums("ffn1_1", parts_b, theirs)
    dh2, dg_f2_0, parts_c, red_b = ffn_back("ffn2_0_bwd", "ffn2", 0, h2, ab_f2_0, dh3, rider=_scatter_rider(sums_b))
    parts_c = parts_c + [slots(dw_kv)]
    dy, dw_out, *theirs = _glu_bwd(y, dh2, w_out, rider=swap_of(parts_c))
    sums_c = pair_sums("ffn2_0", parts_c, theirs)
    ctfull = jnp.concatenate([_blockdiag(P["ssm_c_re"][0]), -_blockdiag(P["ssm_c_im"][0])], axis=1)
    btfull = jnp.concatenate([_blockdiag(bbr.transpose(0, 2, 1)), _blockdiag(bbi.transpose(0, 2, 1))], axis=0)
    du, gx, dy_perm, da, dd = _s5_scan_bwd(dy, u, xs, bf(ctfull), bf(btfull), a2, dvec, bsz)
    dbfull = _mm_tn_blockdiag("ssm_db", u_perm, gx, False)
    dcfull = _mm_tn_blockdiag("ssm_dc", xs, dy_perm, True)
    dh1, dg_mix0, dw_in = _proj_bwd("ssm_in_bwd", h1, mix0, w_in, du, dh2)
    dh0, dg_f1_0, parts_d, red_c = ffn_back("ffn1_0_bwd", "ffn1", 0, h0, ab_f1_0, dh1, rider=_scatter_rider(sums_c))
    dbbr = _diagblocks(dbfull[:, :ns], g_n)
    dbbi = _diagblocks(dbfull[:, ns:], g_n)
    dlr, dli, dls, dbrt, dbit = _s5_params_bwd(lr, li, ls, brt, bit, da[:, :ns].reshape(g_n, 1, p_n),
                                               da[:, ns:].reshape(g_n, 1, p_n), dbbr, dbbi)
    dh0 = r3(dh0)
    G["x"] = dh0[:, PAD:, :]
    G["meta_tokens"] = _meta_sum(dh0)
    G["ffn1_norm"] = jnp.concatenate([dg_f1_0, dg_f1_1], axis=0)
    G["ffn2_norm"] = jnp.concatenate([dg_f2_0, dg_f2_1], axis=0)
    G["mix_norm"] = jnp.concatenate([dg_mix0, dg_mix1], axis=0)
    G["ssm_lambda_re"] = dlr.reshape(1, g_n, p_n)
    G["ssm_lambda_im"] = dli.reshape(1, g_n, p_n)
    G["ssm_log_step"] = dls.reshape(1, g_n)
    G["ssm_b_re"] = dbrt.transpose(0, 2, 1)[None]
    G["ssm_b_im"] = dbit.transpose(0, 2, 1)[None]
    G["ssm_c_re"] = _diagblocks(dcfull[:ns], g_n).transpose(0, 2, 1)[None]
    G["ssm_c_im"] = -_diagblocks(dcfull[ns:], g_n).transpose(0, 2, 1)[None]
    G["ssm_d"] = dd
    G["kv_norm"] = dg_kvn.reshape(-1)
    G["k_norm"] = dkg[0, :HEAD_DIM]
    G["q_norm"] = dqg[:, :HEAD_DIM]
    G["attn_sinks"] = dsinks[:, :N_KV_HEADS * Q_PER_KV]

    parts_d = parts_d + [slots(dw_in), dw_out]
    small_pack = _pack_small([G[n] for n in list(SMALL) + [n for n, _ in COLS]], PACK_W)
    *theirs, small_parts = _run_rider("grad_swap_last", _join_riders(swap_of(parts_d), _gather_rider([small_pack])))
    red_d = _run_rider("grad_scatter_last", _scatter_rider(pair_sums("last", parts_d, theirs)))
    both = lambda lo, hi: jnp.concatenate([lo, hi], axis=1)
    gu = lambda lo, hi: both(jnp.swapaxes(lo, 1, 2), jnp.swapaxes(hi, 1, 2))
    summed = {"ffn1_w_gate_up": gu(red_d[0], red_b[0]), "ffn1_w_down": both(red_d[1], red_b[1]),
              "ffn2_w_gate_up": gu(red_c[0], red_a[0]), "ffn2_w_down": both(red_c[1], red_a[1]),
              "ssm_w_in": red_d[2], "ssm_w_out": red_d[3], "w_kv": red_c[2], "attn_w_q": red_b[2],
              "attn_w_o": red_b[3]}
    return G, summed, small_parts
```

```python
import functools
import math

import jax
import jax.numpy as jnp
from jax import lax
from jax.experimental import pallas as pl
from jax.experimental.pallas import tpu as pltpu

F32 = jnp.float32
BF16 = jnp.bfloat16

N_META = 16
PAD = 128
META0 = PAD - N_META
HEAD_DIM = 64
N_KV_HEADS = 4
Q_PER_KV = 4
SSM_GROUP = 16
SSM_STATE = 64
EPS = 1e-6
NEG_INF = -1e30
ROPE_THETA = 10000.0
ADAM_LR, ADAM_B1, ADAM_B2, ADAM_EPS, ADAM_WD, ADAM_STEP = 0.001, 0.9, 0.999, 1e-08, 0.01, 10
LANES = 128
PACK_W = 1024
VMEM_LIMIT = 56 * 1024 * 1024
MESH_AXES = ("x", "y", "c")
N_DEV = 8


def _cparams(sem=None):
    return pltpu.CompilerParams(dimension_semantics=sem, vmem_limit_bytes=VMEM_LIMIT)


def _row_tile(rows, light=False):
    for tm in ((768,) if light else ()) + (384, 256, 128, 64, 32, 16, 8):
        if rows % tm == 0:
            return tm
    raise ValueError(rows)


STREAM_BUDGET = 32 * 1024 * 1024


def _stream_tile(rows, bytes_per_row):
    for tm in range(rows, 0, -1):
        if rows % tm == 0 and (tm % 16 == 0 or tm == rows) and 2 * tm * bytes_per_row <= STREAM_BUDGET:
            return tm
    raise ValueError(rows)


TN_BUDGET = 52 * 1024 * 1024
TN_MAX_ROWS = 2816


def _tn_tile(rows, a, b, k1, tn):
    sa, sb = a.dtype.itemsize, b.dtype.itemsize
    fits = lambda tm: 2 * tm * (k1 * sa + tn * sb) + 3 * k1 * tn * 4 + tm * k1 * 2 <= TN_BUDGET
    divisors = [tm for tm in range(min(rows, TN_MAX_ROWS), 7, -8) if rows % tm == 0 and fits(tm)]
    good = [tm for tm in divisors if -(-tm // MXU_DIM) * MXU_DIM <= 1.1 * tm]
    if good or divisors:
        return (good or divisors)[0]
    raise ValueError(rows)


def _dot(a, b):
    return jnp.dot(a.astype(BF16), b.astype(BF16), preferred_element_type=F32)


def _dot_nt(a, b):
    return lax.dot_general(a.astype(BF16), b.astype(BF16), (((1,), (1,)), ((), ())), preferred_element_type=F32)


def _dot_tn(a, b):
    return lax.dot_general(a.astype(BF16), b.astype(BF16), (((0,), (0,)), ((), ())), preferred_element_type=F32)


def _rms(x, g):
    rstd = lax.rsqrt(jnp.mean(x * x, axis=-1, keepdims=True) + EPS)
    y = x * rstd
    return y * g, y, rstd


def _rms_bwd(dhn, y, rstd, g):
    dyn = dhn * g
    dx = rstd * (dyn - y * jnp.mean(dyn * y, axis=-1, keepdims=True))
    return dx, jnp.sum(dhn * y, axis=0, keepdims=True)


def _sigmoid(x):
    return 1.0 / (1.0 + jnp.exp(-x))


_GELU_C = math.sqrt(2.0 / math.pi)


def _gelu(y):
    t = jnp.tanh(_GELU_C * (y + 0.044715 * y * y * y))
    return 0.5 * y * (1.0 + t), t


def _gelu_grad(y, t):
    return 0.5 * (1.0 + t) + 0.5 * y * (1.0 - t * t) * _GELU_C * (1.0 + 3.0 * 0.044715 * y * y)


class _Rider:
    def __init__(self, ins, outs, sems, start, mid, finish):
        self.ins, self.outs, self.sems, self.start, self.mid, self.finish = ins, outs, sems, start, mid, finish


def _join_riders(r1, r2):
    ni, no, ns = len(r1.ins), len(r1.outs), len(r1.sems)

    def both(f1, f2):
        def phase(ins, outs, sems):
            if f1 is not None:
                f1(ins[:ni], outs[:no], sems[:ns])
            if f2 is not None:
                f2(ins[ni:], outs[no:], sems[ns:])
        return phase

    mid = both(r1.mid, r2.mid) if (r1.mid is not None or r2.mid is not None) else None
    return _Rider(r1.ins + r2.ins, r1.outs + r2.outs, r1.sems + r2.sems,
                  both(r1.start, r2.start), mid, both(r1.finish, r2.finish))


def _run_rider(name, rider):
    def kern(*refs):
        ni, no = len(rider.ins), len(rider.outs)
        parts = refs[:ni], refs[ni:ni + no], refs[ni + no:]
        rider.start(*parts)
        if rider.mid is not None:
            rider.mid(*parts)
        rider.finish(*parts)

    return pl.pallas_call(
        kern, name=name, out_shape=list(rider.outs), in_specs=[ANY] * len(rider.ins),
        out_specs=[ANY] * len(rider.outs), scratch_shapes=list(rider.sems),
    )(*rider.ins)


def _rowcall(name, body, rows, row_ins, const_ins, row_outs, acc_outs=(), tm=None, row_in_maps=None, rider=None,
             light=False):
    tm = tm or _row_tile(rows, light)
    steps = rows // tm
    in_specs = []
    for k, a in enumerate(row_ins):
        if row_in_maps is not None and row_in_maps[k] is not None:
            in_specs.append(pl.BlockSpec(*row_in_maps[k]))
        else:
            in_specs.append(pl.BlockSpec((tm, a.shape[1]), lambda i: (i, 0)))
    for a in const_ins:
        in_specs.append(pl.BlockSpec(a.shape, lambda i, nd=a.ndim: (0,) * nd, pipeline_mode=pl.Buffered(1)))
    out_shape, out_specs = [], []
    for w, dt in row_outs:
        out_shape.append(jax.ShapeDtypeStruct((rows, w), dt))
        out_specs.append(pl.BlockSpec((tm, w), lambda i: (i, 0)))
    for shp, dt in acc_outs:
        out_shape.append(jax.ShapeDtypeStruct(shp, dt))
        out_specs.append(pl.BlockSpec(shp, lambda i, nd=len(shp): (0,) * nd))

    if rider is None:
        def kern(*refs):
            body(pl.program_id(0), *refs)

        return pl.pallas_call(
            kern, name=name, grid=(steps,), in_specs=in_specs, out_specs=out_specs, out_shape=out_shape,
            compiler_params=_cparams(("arbitrary",)),
        )(*row_ins, *const_ins)

    n_in, n_out = len(in_specs), len(out_specs)
    r_in, r_out = len(rider.ins), len(rider.outs)

    def kern_r(*refs):
        step = pl.program_id(0)
        ins, rins = refs[:n_in], refs[n_in:n_in + r_in]
        outs = refs[n_in + r_in:n_in + r_in + n_out]
        routs = refs[n_in + r_in + n_out:n_in + r_in + n_out + r_out]
        sems = refs[n_in + r_in + n_out + r_out:]

        @pl.when(step == 0)
        def _():
            rider.start(rins, routs, sems)

        if rider.mid is not None:
            @pl.when(step == (3 * steps) // 4)
            def _():
                rider.mid(rins, routs, sems)

        body(step, *ins, *outs)

        @pl.when(step == steps - 1)
        def _():
            rider.finish(rins, routs, sems)

    return pl.pallas_call(
        kern_r, name=name, grid=(steps,), in_specs=in_specs + [ANY] * r_in, out_specs=out_specs + [ANY] * r_out,
        out_shape=out_shape + list(rider.outs), scratch_shapes=list(rider.sems),
        compiler_params=_cparams(("arbitrary",)),
    )(*row_ins, *const_ins, *rider.ins)


def _acc(step, ref, val):
    @pl.when(step == 0)
    def _():
        ref[...] = val

    @pl.when(step != 0)
    def _():
        ref[...] += val


def _embed(x, meta):
    bsz, seq, d = x.shape
    nb = seq // PAD + 1

    def kern(x_ref, m_ref, o_ref):
        i = pl.program_id(1)

        @pl.when(i == 0)
        def _():
            o_ref[0, 0:META0, :] = jnp.zeros((META0, d), F32)
            o_ref[0, META0:PAD, :] = m_ref[...]

        @pl.when(i != 0)
        def _():
            o_ref[0] = x_ref[0]

    return pl.pallas_call(
        kern, name="embed", grid=(bsz, nb),
        in_specs=[pl.BlockSpec((1, PAD, d), lambda b, i: (b, jnp.maximum(i - 1, 0), 0)),
                  pl.BlockSpec((N_META, d), lambda b, i: (0, 0))],
        out_specs=pl.BlockSpec((1, PAD, d), lambda b, i: (b, i, 0)),
        out_shape=jax.ShapeDtypeStruct((bsz, seq + PAD, d), F32),
        compiler_params=_cparams(("arbitrary", "arbitrary")),
    )(x, meta)


def _meta_sum(dh0):
    bsz, lp, d = dh0.shape

    def kern(d_ref, o_ref):
        _acc(pl.program_id(0), o_ref, d_ref[0, META0:PAD, :])

    return pl.pallas_call(
        kern, name="meta_sum", grid=(bsz,),
        in_specs=[pl.BlockSpec((1, PAD, d), lambda b: (b, 0, 0))],
        out_specs=pl.BlockSpec((N_META, d), lambda b: (0, 0)),
        out_shape=jax.ShapeDtypeStruct((N_META, d), F32),
        compiler_params=_cparams(("arbitrary",)),
    )(dh0)


MXU_DIM = 256


def _ffn_chunks(f):
    unit = MXU_DIM if f % MXU_DIM == 0 else LANES
    assert f % unit == 0
    first = (f // unit + 1) // 2 * unit
    return [(0, first), (first, f)] if first < f else [(0, f)]


def _ffn_fwd(name, h, g, wgu, wd, rider=None, loss_target=None, lp=None):
    rows, d = h.shape
    f = wd.shape[0]
    chunks = _ffn_chunks(f)
    tm = _row_tile(rows)
    nblk = tm // PAD

    def body(step, h_ref, *refs):
        t_refs, (g_ref, wgu_ref, wd_ref, o_ref, ab_ref), l_refs = refs[:nt], refs[nt:nt + 5], refs[nt + 5:]
        hx = h_ref[...]
        hb = _rms(hx, g_ref[...])[0].astype(BF16)
        acc = jnp.zeros(hx.shape, F32)
        for lo, hi in chunks:
            ga, ua = slice(lo, hi), slice(f + lo, f + hi)
            a = _dot_nt(hb, wgu_ref[ga, :])
            b = _dot_nt(hb, wgu_ref[ua, :])
            ab_ref[:, ga] = a.astype(BF16)
            ab_ref[:, ua] = b.astype(BF16)
            acc = acc + _dot(a * _sigmoid(a) * b, wd_ref[ga, :])
        out = hx + 0.5 * acc
        if not nt:
            o_ref[...] = out
            return
        err = out - jnp.concatenate([t[...] for t in t_refs], axis=0)
        rid = lax.broadcasted_iota(jnp.int32, (tm, 1), 0)
        err = jnp.where((step % per == 0) & (rid < PAD), 0.0, err)
        o_ref[...] = err * (1.0 / d)
        part = 0.5 * jnp.sum(jnp.mean(err * err, axis=-1, keepdims=True))
        _acc(step, l_refs[0], jnp.broadcast_to(part, (1, LANES)))

    if loss_target is None:
        nt = 0
        return _rowcall(name, body, rows, [h], [g, wgu, wd], [(d, F32), (2 * f, BF16)], rider=rider, tm=tm)
    assert tm % PAD == 0 and lp % tm == 0 and rider is None
    nt, per = nblk, lp // tm
    tblocks = (lp - PAD) // PAD

    def tmap(k):
        return lambda i: ((i // per) * tblocks + jnp.clip((i % per) * nblk - 1 + k, 0, tblocks - 1), 0)

    maps = [None] + [((PAD, d), tmap(k)) for k in range(nblk)]
    return _rowcall(name, body, rows, [h] + [loss_target] * nblk, [g, wgu, wd], [(d, F32), (2 * f, BF16)],
                    [((1, LANES), F32)], tm=tm, row_in_maps=maps)


def _ffn_bwd(name, h, ab, dout, g, wgu, wd, rider=None):
    rows, d = h.shape
    f = wd.shape[0]
    chunks = _ffn_chunks(f)

    def body(step, h_ref, ab_ref, do_ref, g_ref, wgu_ref, wd_ref, dh_ref, hn_ref, dab_ref, act_ref, dg_ref):
        hx, dout_x, gx = h_ref[...], do_ref[...], g_ref[...]
        hn, y, rstd = _rms(hx, gx)
        hn_ref[...] = hn.astype(BF16)
        dhalf = (0.5 * dout_x).astype(BF16)
        dhn = jnp.zeros(hx.shape, F32)
        for lo, hi in chunks:
            ga, ua = slice(lo, hi), slice(f + lo, f + hi)
            a = ab_ref[:, ga].astype(F32)
            b = ab_ref[:, ua].astype(F32)
            s = _sigmoid(a)
            silu = a * s
            act_ref[:, ga] = (silu * b).astype(BF16)
            dact = _dot_nt(dhalf, wd_ref[ga, :])
            da = (dact * b * (s + silu * (1.0 - s))).astype(BF16)
            db = (dact * silu).astype(BF16)
            dab_ref[:, ga] = da
            dab_ref[:, ua] = db
            dhn = dhn + _dot(da, wgu_ref[ga, :]) + _dot(db, wgu_ref[ua, :])
        dx, dg = _rms_bwd(dhn, y, rstd, gx)
        dh_ref[...] = dout_x + dx
        _acc(step, dg_ref, dg)

    return _rowcall(name, body, rows, [h, ab, dout], [g, wgu, wd],
                    [(d, F32), (d, BF16), (2 * f, BF16), (f, BF16)], [((1, d), F32)], rider=rider)


def _mm_tn(name, a, b):
    rows, k1 = a.shape
    k2 = b.shape[1]
    tk = max(t for t in range(LANES, k1 + 1, LANES) if k1 % t == 0 and (t * k2 * 4 <= 6 * 1024 * 1024 or t == LANES))
    tm = _tn_tile(rows, a, b, tk, k2)
    steps = rows // tm

    def kern(a_ref, b_ref, o_ref):
        _acc(pl.program_id(1), o_ref, _dot_tn(a_ref[...], b_ref[...]))

    return pl.pallas_call(
        kern, name=name, grid=(k1 // tk, steps),
        in_specs=[pl.BlockSpec((tm, tk), lambda j, i: (i, j)), pl.BlockSpec((tm, k2), lambda j, i: (i, 0))],
        out_specs=pl.BlockSpec((tk, k2), lambda j, i: (j, 0)),
        out_shape=jax.ShapeDtypeStruct((k1, k2), F32),
        compiler_params=_cparams(("arbitrary", "arbitrary")),
    )(a, b)


def _mm_tn_blockdiag(name, a, b, states_first):
    rows = a.shape[0]
    ka, kb = a.shape[1], b.shape[1]
    qa, qb = (ka // 4, kb // 2) if states_first else (ka // 2, kb // 4)
    tm = _tn_tile(rows, a, b, qa, qb)
    steps = rows // tm
    wide = lambda part, k: 2 * part + k
    amap = (lambda p, k, i: (i, wide(p, k))) if states_first else (lambda p, k, i: (i, k))
    bmap = (lambda p, k, i: (i, k)) if states_first else (lambda p, k, i: (i, wide(p, k)))
    omap = (lambda p, k, i: (wide(p, k), k)) if states_first else (lambda p, k, i: (k, wide(p, k)))

    def kern(a_ref, b_ref, o_ref):
        _acc(pl.program_id(2), o_ref, _dot_tn(a_ref[...], b_ref[...]))

    return pl.pallas_call(
        kern, name=name, grid=(2, 2, steps),
        in_specs=[pl.BlockSpec((tm, qa), amap), pl.BlockSpec((tm, qb), bmap)],
        out_specs=pl.BlockSpec((qa, qb), omap), out_shape=jax.ShapeDtypeStruct((ka, kb), F32),
        compiler_params=_cparams(("arbitrary", "arbitrary", "arbitrary")),
    )(a, b)


def _mm_tn_slots(name, a, b, scale):
    rows, k1 = a.shape
    k2 = b.shape[1]
    tn = 512 if k2 % 512 == 0 else k2
    sr = k1 // N_DEV
    tm = _tn_tile(rows, a, b, k1, tn)
    steps = rows // tm

    def kern(a_ref, b_ref, o_ref):
        bx = b_ref[...]
        if scale != 1.0:
            bx = bx * scale
        res = _dot_tn(a_ref[...], bx)
        step = pl.program_id(1)
        for s in range(N_DEV):
            _acc(step, o_ref.at[s], res[s * sr:(s + 1) * sr])

    return pl.pallas_call(
        kern, name=name, grid=(k2 // tn, steps),
        in_specs=[pl.BlockSpec((tm, k1), lambda j, i: (i, 0)), pl.BlockSpec((tm, tn), lambda j, i: (i, j))],
        out_specs=pl.BlockSpec((N_DEV, sr, tn), lambda j, i: (0, 0, j)),
        out_shape=jax.ShapeDtypeStruct((N_DEV, sr, k2), F32),
        compiler_params=_cparams(("arbitrary", "arbitrary")),
    )(a, b)


def _proj_fwd(name, h, g, w):
    rows = h.shape[0]

    def body(step, h_ref, g_ref, w_ref, o_ref):
        o_ref[...] = _dot(_rms(h_ref[...], g_ref[...])[0], w_ref[...])

    return _rowcall(name, body, rows, [h], [g, w], [(w.shape[1], F32)], light=True)[0]


def _proj_bwd(name, h, g, w, dy, dres, rider=None):
    rows, d = h.shape

    def body(step, h_ref, dy_ref, dr_ref, g_ref, w_ref, dh_ref, dg_ref, dw_ref):
        gx = g_ref[...]
        hn, y, rstd = _rms(h_ref[...], gx)
        dyx = dy_ref[...]
        dx, dg = _rms_bwd(_dot_nt(dyx, w_ref[...]), y, rstd, gx)
        dh_ref[...] = dr_ref[...] + dx
        _acc(step, dg_ref, dg)
        _acc(step, dw_ref, _dot_tn(hn, dyx))

    return _rowcall(name, body, rows, [h, dy, dres], [g, w], [(d, F32)], [((1, d), F32), (w.shape, F32)],
                    rider=rider, light=True)


def _lin_res_fwd(name, a, w, res):
    rows = a.shape[0]

    def body(step, a_ref, r_ref, w_ref, o_ref):
        o_ref[...] = r_ref[...] + _dot(a_ref[...], w_ref[...])

    return _rowcall(name, body, rows, [a, res], [w], [(w.shape[1], F32)], light=True)[0]


def _lin_bwd(name, a, w, dy, rider=None):
    rows, k = a.shape

    def body(step, a_ref, dy_ref, w_ref, da_ref, dw_ref):
        dyx = dy_ref[...]
        da_ref[...] = _dot_nt(dyx, w_ref[...])
        _acc(step, dw_ref, _dot_tn(a_ref[...], dyx))

    return _rowcall(name, body, rows, [a, dy], [w], [(k, F32)], [(w.shape, F32)], rider=rider, light=True)


def _s5_param_fn(lr, li, ls, brt, bit):
    step = jnp.exp(ls)
    mag = jnp.exp(lr * step)
    ar = mag * jnp.cos(li * step)
    ai = mag * jnp.sin(li * step)
    den = lr * lr + li * li
    nr, ni = ar - 1.0, ai
    cr = (nr * lr + ni * li) / den
    ci = (ni * lr - nr * li) / den
    return ar, ai, cr * brt - ci * bit, cr * bit + ci * brt


def _s5_params_fwd(lr, li, ls, brt, bit):
    def kern(lr_ref, li_ref, ls_ref, br_ref, bi_ref, ar_ref, ai_ref, bbr_ref, bbi_ref):
        ar, ai, bbr, bbi = _s5_param_fn(lr_ref[...], li_ref[...], ls_ref[...], br_ref[...], bi_ref[...])
        ar_ref[...], ai_ref[...], bbr_ref[...], bbi_ref[...] = ar, ai, bbr, bbi

    sd = jax.ShapeDtypeStruct
    return pl.pallas_call(
        kern, name="s5_params_fwd",
        out_shape=[sd(lr.shape, F32), sd(lr.shape, F32), sd(brt.shape, F32), sd(brt.shape, F32)],
    )(lr, li, ls, brt, bit)


def _s5_params_bwd(lr, li, ls, brt, bit, dar, dai, dbbr, dbbi):
    def kern(lr_ref, li_ref, ls_ref, br_ref, bi_ref, dar_ref, dai_ref, dbbr_ref, dbbi_ref,
             dlr_ref, dli_ref, dls_ref, dbr_ref, dbi_ref):
        _, vjp = jax.vjp(_s5_param_fn, lr_ref[...], li_ref[...], ls_ref[...], br_ref[...], bi_ref[...])
        dlr, dli, dls, dbr, dbi = vjp((dar_ref[...], dai_ref[...], dbbr_ref[...], dbbi_ref[...]))
        dlr_ref[...], dli_ref[...], dls_ref[...], dbr_ref[...], dbi_ref[...] = dlr, dli, dls, dbr, dbi

    sd = jax.ShapeDtypeStruct
    return pl.pallas_call(
        kern, name="s5_params_bwd",
        out_shape=[sd(lr.shape, F32), sd(lr.shape, F32), sd(ls.shape, F32), sd(brt.shape, F32), sd(brt.shape, F32)],
    )(lr, li, ls, brt, bit, dar, dai, dbbr, dbbi)


SCAN_LW = 512


SCAN_SEGS = 8
SCAN_UNROLL = 8


def _cmul(xr, xi, yr, yi):
    return xr * yr - xi * yi, xr * yi + xi * yr


def _scan_tables(a_ref, tab_ref, conj, seg_len):
    ns = a_ref.shape[1]
    ar = jnp.broadcast_to(a_ref[0:1, :], (8, ns))
    ai = jnp.broadcast_to(a_ref[1:2, :], (8, ns))
    if conj:
        ai = -ai
    big, base, e = None, (ar, ai), seg_len
    while e:
        if e & 1:
            big = base if big is None else _cmul(*big, *base)
        base = _cmul(*base, *base)
        e >>= 1
    big2 = _cmul(*big, *big)
    big4 = _cmul(*big2, *big2)
    for k, v in enumerate((ar, ai) + big + big2 + big4):
        tab_ref[k] = v


def _scan_block(x_ref, tab_ref, carry_ref, t_rows, ns, reverse):
    sl = t_rows // SCAN_SEGS
    assert sl % SCAN_UNROLL == 0
    row = lax.broadcasted_iota(jnp.int32, (8, SCAN_LW), 0)
    zero = jnp.zeros((8, SCAN_LW), F32)
    for lc in range(ns // SCAN_LW):
        lre = pl.ds(lc * SCAN_LW, SCAN_LW)
        lim = pl.ds(ns + lc * SCAN_LW, SCAN_LW)
        ar, ai = tab_ref[0, :, lre], tab_ref[1, :, lre]

        def rows_of(k, u):
            j = k * SCAN_UNROLL + u
            return pl.ds(pl.multiple_of(((sl - 1 - j) if reverse else j) * SCAN_SEGS, SCAN_SEGS), SCAN_SEGS)

        def local(k, s, lre=lre, lim=lim, ar=ar, ai=ai):
            sr, si = s
            for u in range(SCAN_UNROLL):
                rows = rows_of(k, u)
                tr, ti = _cmul(ar, ai, sr, si)
                sr, si = x_ref[rows, lre] + tr, x_ref[rows, lim] + ti
                x_ref[rows, lre], x_ref[rows, lim] = sr, si
            return sr, si

        er, ei = lax.fori_loop(0, sl // SCAN_UNROLL, local, (zero, zero))
        if reverse:
            cr = jnp.where(row == 7, carry_ref[:, lre], pltpu.roll(er, 7, 0))
            ci = jnp.where(row == 7, carry_ref[:, lim], pltpu.roll(ei, 7, 0))
        else:
            cr = jnp.where(row == 0, carry_ref[:, lre], pltpu.roll(er, 1, 0))
            ci = jnp.where(row == 0, carry_ref[:, lim], pltpu.roll(ei, 1, 0))
        for lvl, dsh in enumerate((1, 2, 4)):
            pr, pi = tab_ref[2 + 2 * lvl, :, lre], tab_ref[3 + 2 * lvl, :, lre]
            if reverse:
                keep, shift = row < 8 - dsh, 8 - dsh
            else:
                keep, shift = row >= dsh, dsh
            sr = jnp.where(keep, pltpu.roll(cr, shift, 0), 0.0)
            si = jnp.where(keep, pltpu.roll(ci, shift, 0), 0.0)
            tr, ti = _cmul(pr, pi, sr, si)
            cr, ci = cr + tr, ci + ti
        tr, ti = _cmul(tab_ref[2, :, lre], tab_ref[3, :, lre], cr, ci)
        edge = 0 if reverse else 7
        carry_ref[:, lre] = jnp.broadcast_to((er + tr)[edge:edge + 1, :], (8, SCAN_LW))
        carry_ref[:, lim] = jnp.broadcast_to((ei + ti)[edge:edge + 1, :], (8, SCAN_LW))

        def fix(k, t, lre=lre, lim=lim, ar=ar, ai=ai):
            tr, ti = t
            for u in range(SCAN_UNROLL):
                rows = rows_of(k, u)
                tr, ti = _cmul(ar, ai, tr, ti)
                x_ref[rows, lre] = x_ref[rows, lre] + tr
                x_ref[rows, lim] = x_ref[rows, lim] + ti
            return tr, ti

        lax.fori_loop(0, sl // SCAN_UNROLL, fix, (cr, ci))


def _bd_expand(u, w_ref, x_ref, ns):
    hh, sh = u.shape[1] // 2, ns // 2
    ub = u.astype(BF16)
    for part in range(2):
        for k in range(2):
            cols = slice(part * ns + k * sh, part * ns + (k + 1) * sh)
            x_ref[:, cols] = jnp.dot(ub[:, k * hh:(k + 1) * hh], w_ref[k * hh:(k + 1) * hh, cols],
                                     preferred_element_type=F32)


def _bd_contract(x_ref, w_ref, ns):
    hh, sh = w_ref.shape[1] // 2, ns // 2
    halves = []
    for k in range(2):
        acc = None
        for part in range(2):
            rows = slice(part * ns + k * sh, part * ns + (k + 1) * sh)
            t = jnp.dot(x_ref[:, rows].astype(BF16), w_ref[rows, k * hh:(k + 1) * hh], preferred_element_type=F32)
            acc = t if acc is None else acc + t
        halves.append(acc)
    return jnp.concatenate(halves, axis=1)


def _scan_rows(lp):
    for t in (384, 256, 128):
        if lp % t == 0:
            return t
    raise ValueError(lp)


def _seg_perm(t_rows):
    r = jnp.arange(t_rows)
    src = (r % SCAN_SEGS) * (t_rows // SCAN_SEGS) + r // SCAN_SEGS
    p = (src[:, None] == r[None, :]).astype(BF16)
    return p, p.T


def _permute_rows(p_ref, v):
    return jnp.dot(p_ref[...], v.astype(BF16), preferred_element_type=F32)


def _unpermute_rows(pt_ref, v):
    hi = v.astype(BF16)
    lo = (v - hi.astype(F32)).astype(BF16)
    pt = pt_ref[...]
    return jnp.dot(pt, hi, preferred_element_type=F32) + jnp.dot(pt, lo, preferred_element_type=F32)


def _s5_scan_fwd(u, bfull, cfull, a2, dvec, bsz):
    rows, hw = u.shape
    ns = a2.shape[1]
    lp = rows // bsz
    t_rows = _scan_rows(lp)
    nc = lp // t_rows
    pmat, pmat_t = _seg_perm(t_rows)

    def kern(u_ref, b_ref, c_ref, a_ref, d_ref, p_ref, pt_ref, y_ref, x_ref, up_ref, tab_ref, carry_ref):
        c = pl.program_id(1)

        @pl.when((pl.program_id(0) == 0) & (c == 0))
        def _():
            _scan_tables(a_ref, tab_ref, False, t_rows // SCAN_SEGS)

        @pl.when(c == 0)
        def _():
            carry_ref[...] = jnp.zeros_like(carry_ref)

        ux = u_ref[...]
        up = _permute_rows(p_ref, ux)
        up_ref[...] = up.astype(BF16)
        _bd_expand(up, b_ref, x_ref, ns)
        _scan_block(x_ref, tab_ref, carry_ref, t_rows, ns, reverse=False)
        y_ref[...] = _unpermute_rows(pt_ref, _bd_contract(x_ref, c_ref, ns)) + d_ref[...] * ux

    const = lambda shp: pl.BlockSpec(shp, lambda b, c: (0,) * len(shp), pipeline_mode=pl.Buffered(1))
    blk = lambda b, c: (b * nc + c, 0)
    return pl.pallas_call(
        kern, name="s5_scan_fwd", grid=(bsz, nc),
        in_specs=[pl.BlockSpec((t_rows, hw), blk), const(bfull.shape), const(cfull.shape), const(a2.shape),
                  const(dvec.shape), const(pmat.shape), const(pmat.shape)],
        out_specs=[pl.BlockSpec((t_rows, hw), blk), pl.BlockSpec((t_rows, 2 * ns), blk),
                   pl.BlockSpec((t_rows, hw), blk)],
        out_shape=[jax.ShapeDtypeStruct((rows, hw), F32), jax.ShapeDtypeStruct((rows, 2 * ns), F32),
                   jax.ShapeDtypeStruct((rows, hw), BF16)],
        scratch_shapes=[pltpu.VMEM((8, 8, ns), F32), pltpu.VMEM((8, 2 * ns), F32)],
        compiler_params=_cparams(("arbitrary", "arbitrary")),
    )(u, bfull, cfull, a2, dvec, pmat, pmat_t)


def _s5_scan_bwd(dy, u, xs, ctfull, btfull, a2, dvec, bsz):
    rows, hw = u.shape
    ns = a2.shape[1]
    lp = rows // bsz
    t_rows = _scan_rows(lp)
    nc = lp // t_rows
    blk = lambda b, c: (b * nc + (nc - 1 - c), 0)
    pmat, pmat_t = _seg_perm(t_rows)

    def prev8(b, c):
        first = (b * nc + (nc - 1 - c)) * (t_rows // 8)
        return (jnp.maximum(first - 1, 0), 0)

    def kern(dy_ref, u_ref, x_ref, xp_ref, ct_ref, bt_ref, a_ref, d_ref, p_ref, pt_ref,
             du_ref, gx_ref, dyp_ref, da_ref, dd_ref, tab_ref, carry_ref):
        b, c = pl.program_id(0), pl.program_id(1)
        first = (b == 0) & (c == 0)

        @pl.when(first)
        def _():
            _scan_tables(a_ref, tab_ref, True, t_rows // SCAN_SEGS)

        @pl.when(c == 0)
        def _():
            carry_ref[...] = jnp.zeros_like(carry_ref)

        dyx, ux = dy_ref[...], u_ref[...]
        dyp = _permute_rows(p_ref, dyx)
        dyp_ref[...] = dyp.astype(BF16)
        _bd_expand(dyp, ct_ref, gx_ref, ns)
        _scan_block(gx_ref, tab_ref, carry_ref, t_rows, ns, reverse=True)
        gx = gx_ref[...]
        du_ref[...] = _unpermute_rows(pt_ref, _bd_contract(gx_ref, bt_ref, ns)) + d_ref[...] * dyx
        seq_start = c == nc - 1
        row8 = lax.broadcasted_iota(jnp.int32, (8, 1), 0)
        head = pltpu.roll(x_ref[t_rows - 8:t_rows, :], 1, 0)
        head = jnp.where(row8 == 0, jnp.where(seq_start, 0.0, xp_ref[7:8, :]), head)
        xprev = jnp.concatenate([head, x_ref[0:t_rows - 8, :]], axis=0)
        xr, xi, gr, gi = xprev[:, :ns], xprev[:, ns:], gx[:, :ns], gx[:, ns:]
        da = jnp.concatenate([jnp.sum(xr * gr + xi * gi, axis=0, keepdims=True),
                              jnp.sum(xr * gi - xi * gr, axis=0, keepdims=True)], axis=1)
        dd = jnp.sum(dyx * ux, axis=0, keepdims=True)

        @pl.when(first)
        def _():
            da_ref[...] = da
            dd_ref[...] = dd

        @pl.when(jnp.logical_not(first))
        def _():
            da_ref[...] += da
            dd_ref[...] += dd

    const = lambda shp: pl.BlockSpec(shp, lambda b, c: (0,) * len(shp), pipeline_mode=pl.Buffered(1))
    return pl.pallas_call(
        kern, name="s5_scan_bwd", grid=(bsz, nc),
        in_specs=[pl.BlockSpec((t_rows, hw), blk), pl.BlockSpec((t_rows, hw), blk),
                  pl.BlockSpec((t_rows, 2 * ns), blk), pl.BlockSpec((8, 2 * ns), prev8),
                  const(ctfull.shape), const(btfull.shape), const(a2.shape), const(dvec.shape),
                  const(pmat.shape), const(pmat.shape)],
        out_specs=[pl.BlockSpec((t_rows, hw), blk), pl.BlockSpec((t_rows, 2 * ns), blk),
                   pl.BlockSpec((t_rows, hw), blk),
                   pl.BlockSpec((1, 2 * ns), lambda b, c: (0, 0)), pl.BlockSpec((1, hw), lambda b, c: (0, 0))],
        out_shape=[jax.ShapeDtypeStruct((rows, hw), F32), jax.ShapeDtypeStruct((rows, 2 * ns), F32),
                   jax.ShapeDtypeStruct((rows, hw), BF16),
                   jax.ShapeDtypeStruct((1, 2 * ns), F32), jax.ShapeDtypeStruct((1, hw), F32)],
        scratch_shapes=[pltpu.VMEM((8, 8, ns), F32), pltpu.VMEM((8, 2 * ns), F32)],
        compiler_params=_cparams(("arbitrary", "arbitrary")),
    )(dy, u, xs, xs, ctfull, btfull, a2, dvec, pmat, pmat_t)


def _glu_fwd(y, h1, wout):
    rows, d = h1.shape

    def body(step, y_ref, h_ref, w_ref, o_ref):
        z = _dot(_gelu(y_ref[...])[0], w_ref[...])
        o_ref[...] = h_ref[...] + z[:, :d] * _sigmoid(z[:, d:])

    return _rowcall("glu_fwd", body, rows, [y, h1], [wout], [(d, F32)], light=True)[0]


def _glu_bwd(y, dh2, wout, rider=None):
    rows, d = dh2.shape
    hw = y.shape[1]

    def body(step, y_ref, dh_ref, w_ref, dy_ref, dw_ref):
        yx, dh = y_ref[...], dh_ref[...]
        gl, t = _gelu(yx)
        z = _dot(gl, w_ref[...])
        za, sg = z[:, :d], _sigmoid(z[:, d:])
        dza = dh * sg
        dzg = dh * za * sg * (1.0 - sg)
        dgl = _dot_nt(dza, w_ref[:, :d]) + _dot_nt(dzg, w_ref[:, d:])
        dy_ref[...] = dgl * _gelu_grad(yx, t)
        for half, dz in enumerate((dza, dzg)):
            dw = _dot_tn(gl, dz)
            for s in range(N_DEV // 2):
                _acc(step, dw_ref.at[half * (N_DEV // 2) + s], dw[:, s * cw:(s + 1) * cw])

    cw = 2 * d // N_DEV
    return _rowcall("glu_bwd", body, rows, [y, dh2], [wout], [(hw, F32)], [((N_DEV, hw, cw), F32)], rider=rider,
                    light=True)


def _gmean64(x2, gmat):
    hi = x2.astype(BF16)
    r1 = x2 - hi.astype(F32)
    mid = r1.astype(BF16)
    lo = (r1 - mid.astype(F32)).astype(BF16)
    outs = []
    for j in range(x2.shape[1] // LANES):
        sl = slice(j * LANES, (j + 1) * LANES)
        f = lambda p: jnp.dot(p[:, sl], gmat, preferred_element_type=F32)
        outs.append(f(hi) + f(mid) + f(lo))
    return outs[0] if len(outs) == 1 else jnp.concatenate(outs, axis=1)


def _swap32(x):
    w = x.shape[1]
    lane = lax.broadcasted_iota(jnp.int32, (1, w), 1)
    return jnp.where((lane & 32) == 0, pltpu.roll(x, w - 32, 1), pltpu.roll(x, 32, 1))


def _tile_lanes(t, w):
    reps = w // t.shape[1]
    return t if reps == 1 else jnp.concatenate([t] * reps, axis=1)


def _headrope_fwd(name, raw, w, gain, cos, sin, gmat, lp):
    rows = raw.shape[0]
    tm = _row_tile(lp)
    per = lp // tm

    def body(step, x_ref, c_ref, s_ref, g_ref, gm_ref, o_ref):
        x = x_ref[...]
        rstd = lax.rsqrt(_gmean64(x * x, gm_ref[...]) + EPS)
        z = x * rstd * g_ref[...]
        o_ref[...] = z * _tile_lanes(c_ref[...], w) + _swap32(z) * _tile_lanes(s_ref[...], w)

    maps = [((tm, w), lambda i: (i, 0)), ((tm, LANES), lambda i: (i % per, 0)), ((tm, LANES), lambda i: (i % per, 0))]
    return _rowcall(name, body, rows, [raw, cos, sin], [gain, gmat], [(w, F32)], tm=tm, row_in_maps=maps)[0]


def _headrope_bwd(name, raw, w, dout, gain, cos, sin, gmat, lp, tail=None):
    rows = raw.shape[0]
    tm = _row_tile(lp)
    per = lp // tm
    wt = 0 if tail is None else tail.shape[1]

    def body(step, x_ref, do_ref, c_ref, s_ref, *refs):
        (g_ref, gm_ref, dx_ref, dg_ref), t_refs = refs[-4:], refs[:-4]
        x, dout_x, gx, gm = x_ref[...], do_ref[...], g_ref[...], gm_ref[...]
        rstd = lax.rsqrt(_gmean64(x * x, gm) + EPS)
        yn = x * rstd
        dz = dout_x * _tile_lanes(c_ref[...], w) + _swap32(dout_x * _tile_lanes(s_ref[...], w))
        dyn = dz * gx
        dx_ref[:, 0:w] = rstd * (dyn - yn * _gmean64(dyn * yn, gm))
        if t_refs:
            dx_ref[:, w:w + wt] = t_refs[0][...]
        dg = jnp.sum(dz * yn, axis=0, keepdims=True)
        sh = w // 2
        while sh >= HEAD_DIM:
            dg = dg + pltpu.roll(dg, sh, 1)
            sh //= 2
        _acc(step, dg_ref, dg)

    maps = [((tm, w), lambda i: (i, 0)), None, ((tm, LANES), lambda i: (i % per, 0)), ((tm, LANES), lambda i: (i % per, 0))]
    extra = [] if tail is None else [tail]
    return _rowcall(name, body, rows, [raw, dout, cos, sin] + extra, [gain, gmat], [(w + wt, F32)], [((1, w), F32)],
                    tm=tm, row_in_maps=maps + [None] * len(extra))


KVW = N_KV_HEADS * HEAD_DIM
QB = 128


def _fold4(x):
    y = x + pltpu.roll(x, 128, 1)
    return y + pltpu.roll(y, 64, 1)


ATTN_SCALE = HEAD_DIM ** -0.5


def _attn_masks(i):
    k0j = lax.broadcasted_iota(jnp.int32, (Q_PER_KV * QB, QB), 1)
    qi = lax.broadcasted_iota(jnp.int32, (Q_PER_KV * QB, 2 * QB), 0) % QB
    kj = lax.broadcasted_iota(jnp.int32, (Q_PER_KV * QB, 2 * QB), 1)
    in_prev = (kj < QB) & (kj > qi) & (i >= 2)
    in_cur = (kj >= QB) & (kj - QB <= qi)
    return k0j >= META0, in_prev | in_cur


def _attn_scores(i, q_ref, k0_ref, kp_ref, kc_ref, sink_ref, h):
    masks = _attn_masks(i)
    lane = lax.broadcasted_iota(jnp.int32, (1, KVW), 1) // HEAD_DIM
    qh = q_ref[:, h * KVW:(h + 1) * KVW]
    qs = jnp.concatenate([jnp.where(lane == g, qh, 0.0) for g in range(Q_PER_KV)], axis=0).astype(BF16)
    hsel = lane == h
    kx = _expand_kv((k0_ref, kp_ref, kc_ref), hsel)
    s0 = jnp.where(masks[0], _dot_nt(qs, kx[0]) * ATTN_SCALE, NEG_INF)
    sb = jnp.where(masks[1], _dot_nt(qs, kx[1]) * ATTN_SCALE, NEG_INF)
    rowg = lax.broadcasted_iota(jnp.int32, (Q_PER_KV * QB, 1), 0) // QB
    sink = jnp.zeros((Q_PER_KV * QB, 1), F32)
    for g in range(Q_PER_KV):
        sink = jnp.where(rowg == g, sink_ref[0, h * Q_PER_KV + g], sink)
    m = jnp.maximum(jnp.maximum(jnp.max(s0, axis=1, keepdims=True), jnp.max(sb, axis=1, keepdims=True)), sink)
    p0, pb, ps = jnp.exp(s0 - m), jnp.exp(sb - m), jnp.exp(sink - m)
    den = jnp.sum(p0, axis=1, keepdims=True) + jnp.sum(pb, axis=1, keepdims=True) + ps
    return qs, kx, (p0, pb), ps, den, lane, hsel


def _expand_kv(refs, hsel):
    x0, xp, xc = [_fold4(jnp.where(hsel, r[...], 0.0)).astype(BF16) for r in refs]
    return [x0, jnp.concatenate([xp, xc], axis=0)]


def _unstack(x, lane):
    out = jnp.where(lane == 0, x[0:QB], 0.0)
    for g in range(1, Q_PER_KV):
        out = out + jnp.where(lane == g, x[g * QB:(g + 1) * QB], 0.0)
    return out


def _attn_specs(nb, d):
    qspec = pl.BlockSpec((None, QB, d), lambda b, i: (b, i, 0))
    k0 = pl.BlockSpec((None, QB, KVW), lambda b, i: (b, 0, 0))
    kp = pl.BlockSpec((None, QB, KVW), lambda b, i: (b, jnp.maximum(i - 1, 0), 0))
    kc = pl.BlockSpec((None, QB, KVW), lambda b, i: (b, i, 0))
    v0 = pl.BlockSpec((None, QB, KVW), lambda b, i: (b, 0, 1))
    vp = pl.BlockSpec((None, QB, KVW), lambda b, i: (b, jnp.maximum(i - 1, 0), 1))
    vc = pl.BlockSpec((None, QB, KVW), lambda b, i: (b, i, 1))
    sink = pl.BlockSpec(memory_space=pltpu.SMEM)
    return qspec, [k0, kp, kc], [v0, vp, vc], sink


def _attn_fwd(q, k, kv, sinks):
    bsz, lp, d = q.shape
    nb = lp // QB
    qspec, kspecs, vspecs, sspec = _attn_specs(nb, d)

    def kern(q_ref, k0_ref, kp_ref, kc_ref, v0_ref, vp_ref, vc_ref, sink_ref, o_ref):
        i = pl.program_id(1)
        for h in range(N_KV_HEADS):
            qs, kx, ps3, psink, den, lane, hsel = _attn_scores(i, q_ref, k0_ref, kp_ref, kc_ref, sink_ref, h)
            vx = _expand_kv((v0_ref, vp_ref, vc_ref), hsel)
            o = _dot(ps3[0], vx[0]) + _dot(ps3[1], vx[1])
            o_ref[:, h * KVW:(h + 1) * KVW] = _unstack(o * (1.0 / den), lane)

    return pl.pallas_call(
        kern, name="attn_fwd", grid=(bsz, nb),
        in_specs=[qspec] + kspecs + vspecs + [sspec],
        out_specs=qspec, out_shape=jax.ShapeDtypeStruct((bsz, lp, d), F32),
        compiler_params=_cparams(("arbitrary", "arbitrary")),
    )(q, k, k, k, kv, kv, kv, sinks)


def _attn_bwd(q, k, kv, sinks, o, do):
    bsz, lp, d = q.shape
    nb = lp // QB
    qspec, kspecs, vspecs, sspec = _attn_specs(nb, d)
    full = pl.BlockSpec((None, lp, KVW), lambda b, i: (b, 0, 0))

    def kern(q_ref, k0_ref, kp_ref, kc_ref, v0_ref, vp_ref, vc_ref, sink_ref, o_ref, do_ref,
             dq_ref, dk_ref, dv_ref, ds_ref):
        b, i = pl.program_id(0), pl.program_id(1)

        @pl.when(i == 0)
        def _():
            dk_ref[...] = jnp.zeros_like(dk_ref)
            dv_ref[...] = jnp.zeros_like(dv_ref)

        @pl.when((b == 0) & (i == 0))
        def _():
            ds_ref[...] = jnp.zeros_like(ds_ref)

        lane128 = lax.broadcasted_iota(jnp.int32, (1, LANES), 1)
        rowg = lax.broadcasted_iota(jnp.int32, (Q_PER_KV * QB, 1), 0) // QB
        dk_acc = [jnp.zeros((QB, KVW), F32), jnp.zeros((2 * QB, KVW), F32)]
        dv_acc = [jnp.zeros((QB, KVW), F32), jnp.zeros((2 * QB, KVW), F32)]
        dsink = jnp.zeros((1, LANES), F32)
        for h in range(N_KV_HEADS):
            qs, kx, ps3, psink, den, lane, hsel = _attn_scores(i, q_ref, k0_ref, kp_ref, kc_ref, sink_ref, h)
            vx = _expand_kv((v0_ref, vp_ref, vc_ref), hsel)
            sl = slice(h * KVW, (h + 1) * KVW)
            doh, oh = do_ref[:, sl], o_ref[:, sl]
            dos = jnp.concatenate([jnp.where(lane == g, doh, 0.0) for g in range(Q_PER_KV)], axis=0)
            ost = jnp.concatenate([jnp.where(lane == g, oh, 0.0) for g in range(Q_PER_KV)], axis=0)
            delta = jnp.sum(dos * ost, axis=1, keepdims=True)
            inv = 1.0 / den
            dosb = dos.astype(BF16)
            dqs = jnp.zeros((Q_PER_KV * QB, KVW), F32)
            for n in range(2):
                pn = ps3[n] * inv
                ds = pn * (_dot_nt(dosb, vx[n]) - delta) * ATTN_SCALE
                dqs = dqs + _dot(ds, kx[n])
                dk_acc[n] = dk_acc[n] + jnp.where(hsel, _fold4(_dot_tn(ds, qs)), 0.0)
                dv_acc[n] = dv_acc[n] + jnp.where(hsel, _fold4(_dot_tn(pn, dosb)), 0.0)
            dq_ref[:, sl] = _unstack(dqs, lane)
            dsk = -(psink * inv) * delta
            for g in range(Q_PER_KV):
                val = jnp.sum(jnp.where(rowg == g, dsk, 0.0), axis=0, keepdims=True)
                dsink = dsink + jnp.where(lane128 == h * Q_PER_KV + g, val, 0.0)
        ds_ref[...] += dsink
        r0 = pl.ds(0, QB)
        rp = pl.ds(pl.multiple_of(jnp.maximum(i - 1, 0) * QB, QB), QB)
        rc = pl.ds(pl.multiple_of(i * QB, QB), QB)
        for acc, ref in ((dk_acc, dk_ref), (dv_acc, dv_ref)):
            ref[r0, :] += acc[0]
            ref[rp, :] += acc[1][:QB]
            ref[rc, :] += acc[1][QB:]

    return pl.pallas_call(
        kern, name="attn_bwd", grid=(bsz, nb),
        in_specs=[qspec] + kspecs + vspecs + [sspec, qspec, qspec],
        out_specs=[qspec, full, full, pl.BlockSpec((1, LANES), lambda b, i: (0, 0))],
        out_shape=[jax.ShapeDtypeStruct((bsz, lp, d), F32), jax.ShapeDtypeStruct((bsz, lp, KVW), F32),
                   jax.ShapeDtypeStruct((bsz, lp, KVW), F32), jax.ShapeDtypeStruct((1, LANES), F32)],
        compiler_params=_cparams(("arbitrary", "arbitrary")),
    )(q, k, k, k, kv, kv, kv, sinks, o, do)


def _adamw(name, w, m, v, parts):
    rows, wd = w.shape
    n = parts.shape[0]
    tm = _stream_tile(rows, wd * (7 * 4 + n * parts.dtype.itemsize))

    def kern(w_ref, m_ref, v_ref, p_ref, g_ref, d_ref, m2_ref, v2_ref):
        g = p_ref[0].astype(F32)
        for k in range(1, n):
            g = g + p_ref[k].astype(F32)
        m2 = ADAM_B1 * m_ref[...] + (1.0 - ADAM_B1) * g
        v2 = ADAM_B2 * v_ref[...] + (1.0 - ADAM_B2) * (g * g)
        mh = m2 / (1.0 - ADAM_B1 ** ADAM_STEP)
        vh = v2 / (1.0 - ADAM_B2 ** ADAM_STEP)
        g_ref[...] = g
        d_ref[...] = -ADAM_LR * (mh / (jnp.sqrt(vh) + ADAM_EPS) + ADAM_WD * w_ref[...])
        m2_ref[...] = m2
        v2_ref[...] = v2

    spec = pl.BlockSpec((tm, wd), lambda i: (i, 0))
    sd = jax.ShapeDtypeStruct((rows, wd), F32)
    return pl.pallas_call(
        kern, name=name, grid=(rows // tm,),
        in_specs=[spec, spec, spec, pl.BlockSpec((n, tm, wd), lambda i: (0, i, 0))],
        out_specs=[spec] * 4, out_shape=[sd] * 4,
        compiler_params=_cparams(("arbitrary",)),
    )(w, m, v, parts)


def _pair_sum(name, parts, theirs, my_c):
    n, _, rows, wd = parts.shape
    tm = _stream_tile(rows, wd * (4 + 4 + 2))

    def kern(c_ref, a_ref, b_ref, o_ref):
        o_ref[...] = (a_ref[...] + b_ref[...]).astype(BF16)

    return pl.pallas_call(
        kern, name=name,
        grid_spec=pltpu.PrefetchScalarGridSpec(
            num_scalar_prefetch=1, grid=(n, rows // tm),
            in_specs=[pl.BlockSpec((None, None, tm, wd), lambda k, i, c: (k, c[0], i, 0)),
                      pl.BlockSpec((None, tm, wd), lambda k, i, c: (k, i, 0))],
            out_specs=pl.BlockSpec((None, tm, wd), lambda k, i, c: (k, i, 0))),
        out_shape=jax.ShapeDtypeStruct((n, rows, wd), BF16), compiler_params=_cparams(("arbitrary", "arbitrary")),
    )(my_c, parts, theirs)


MESH = pl.DeviceIdType.MESH
ANY = pl.BlockSpec(memory_space=pl.ANY)


def _place():
    x, y, c = lax.axis_index("x"), lax.axis_index("y"), lax.axis_index("c")
    return x, y, c, [(1 - x, y), (x, 1 - y), (1 - x, 1 - y)]


def _gather_rider(shards):
    n = len(shards)

    def copy(refs, a, k, block, to, own=False):
        x_refs, out_refs, (send_sems, recv_sems, _) = refs
        px, py, pc = block
        slot = out_refs[a].at[4 * px + 2 * py + pc]
        return pltpu.make_async_remote_copy(
            src_ref=x_refs[a] if own else slot, dst_ref=slot,
            send_sem=send_sems.at[a, k], recv_sem=recv_sems.at[a, k], device_id=to, device_id_type=MESH)

    def local(refs, a):
        x, y, c, _ = _place()
        return pltpu.make_async_copy(refs[0][a], refs[1][a].at[4 * x + 2 * y + c], refs[2][2].at[a])

    def first(refs):
        x, y, c, chips = _place()
        out = []
        for a in range(n):
            out.append(copy(refs, a, 0, (x, y, c), (x, y, 1 - c), own=True))
            out += [copy(refs, a, 1 + j, (x, y, c), (*chip, c), own=True) for j, chip in enumerate(chips)]
        return out

    def passed(refs):
        x, y, c, chips = _place()
        return [copy(refs, a, 4 + j, (*chip, c), (x, y, 1 - c)) for j, chip in enumerate(chips) for a in range(n)]

    def start(*refs):
        for a in range(n):
            local(refs, a).start()
        for cp in first(refs):
            cp.start()

    def mid(*refs):
        x, y, c, chips = _place()
        fwd = passed(refs)
        for j, chip in enumerate(chips):
            for a in range(n):
                copy(refs, a, 1 + j, (*chip, c), (x, y, c)).wait_recv()
                fwd[j * n + a].start()

    def finish(*refs):
        x, y, c, chips = _place()
        for a in range(n):
            copy(refs, a, 0, (x, y, 1 - c), (x, y, c)).wait_recv()
            for j, chip in enumerate(chips):
                copy(refs, a, 4 + j, (*chip, 1 - c), (x, y, c)).wait_recv()
        for cp in first(refs) + passed(refs):
            cp.wait_send()
        for a in range(n):
            local(refs, a).wait()

    return _Rider(list(shards), [jax.ShapeDtypeStruct((N_DEV,) + s.shape, s.dtype) for s in shards],
                  [pltpu.SemaphoreType.DMA((n, 7)), pltpu.SemaphoreType.DMA((n, 7)), pltpu.SemaphoreType.DMA((n,))],
                  start, mid, finish)


def _swap_rider(parts):
    n = len(parts)

    def copies(p_refs, out_refs, sems):
        x, y, c, _ = _place()
        return [pltpu.make_async_remote_copy(
            src_ref=p_refs[a].at[:, 1 - c], dst_ref=out_refs[a], send_sem=sems[0].at[a], recv_sem=sems[1].at[a],
            device_id=(x, y, 1 - c), device_id_type=MESH) for a in range(n)]

    def start(*refs):
        for cp in copies(*refs):
            cp.start()

    def finish(*refs):
        for cp in copies(*refs):
            cp.wait()

    return _Rider(list(parts), [jax.ShapeDtypeStruct((p.shape[0],) + p.shape[2:], p.dtype) for p in parts],
                  [pltpu.SemaphoreType.DMA((n,)), pltpu.SemaphoreType.DMA((n,))], start, None, finish)


def _scatter_rider(sums):
    n = len(sums)

    def copy(refs, a, j, block):
        s_refs, out_refs, (send_sems, recv_sems, _) = refs
        x, y, c, chips = _place()
        px, py = chips[j]
        return pltpu.make_async_remote_copy(
            src_ref=s_refs[a].at[2 * px + py], dst_ref=out_refs[a].at[block],
            send_sem=send_sems.at[a, j], recv_sem=recv_sems.at[a, j], device_id=(px, py, c), device_id_type=MESH)

    def local(refs, a):
        x, y, c, _ = _place()
        return pltpu.make_async_copy(refs[0][a].at[2 * x + y], refs[1][a].at[2 * x + y], refs[2][2].at[a])

    def sends(refs):
        x, y, c, _ = _place()
        return [copy(refs, a, j, 2 * x + y) for j in range(3) for a in range(n)]

    def start(*refs):
        for a in range(n):
            local(refs, a).start()
        for cp in sends(refs):
            cp.start()

    def finish(*refs):
        x, y, c, chips = _place()
        for j, (px, py) in enumerate(chips):
            for a in range(n):
                copy(refs, a, j, 2 * px + py).wait_recv()
        for cp in sends(refs):
            cp.wait_send()
        for a in range(n):
            local(refs, a).wait()

    return _Rider(list(sums), [jax.ShapeDtypeStruct(s.shape, s.dtype) for s in sums],
                  [pltpu.SemaphoreType.DMA((n, 3)), pltpu.SemaphoreType.DMA((n, 3)), pltpu.SemaphoreType.DMA((n,))],
                  start, None, finish)


BIG = (("ffn1_w_gate_up", 2), ("ffn1_w_down", 1), ("ffn2_w_gate_up", 2), ("ffn2_w_down", 1), ("ssm_w_in", 1),
       ("ssm_w_out", 2), ("w_kv", 0), ("attn_w_q", 1), ("attn_w_o", 1))
SMALL = ("ffn1_norm", "mix_norm", "ffn2_norm", "ssm_lambda_re", "ssm_lambda_im", "ssm_b_re", "ssm_b_im",
         "ssm_c_re", "ssm_c_im", "ssm_log_step", "kv_norm", "k_norm", "q_norm", "attn_sinks")
COLS = (("meta_tokens", 1), ("ssm_d", 1))
WEIGHTS = ("meta_tokens", "ffn1_norm", "ffn1_w_gate_up", "ffn1_w_down", "mix_norm", "ffn2_norm", "ffn2_w_gate_up",
           "ffn2_w_down", "ssm_w_in", "ssm_lambda_re", "ssm_lambda_im", "ssm_b_re", "ssm_b_im", "ssm_c_re",
           "ssm_c_im", "ssm_log_step", "ssm_d", "ssm_w_out", "kv_norm", "w_kv", "k_norm", "attn_w_q", "q_norm",
           "attn_sinks", "attn_w_o")


def _rows_of(a, width):
    n = math.prod(a.shape)
    if n % width == 0:
        r = a.reshape(n // width, width)
    else:
        assert n < width
        r = jnp.pad(a.reshape(1, n), ((0, 0), (0, width - n)))
    return jnp.pad(r, ((0, (-r.shape[0]) % 8), (0, 0)))


def _pack_small(arrs, width):
    return jnp.concatenate([_rows_of(a.astype(F32), width) for a in arrs], axis=0)


def _unpack_small(buf, shapes, width):
    out, off = [], 0
    for shp in shapes:
        n = math.prod(shp)
        r = max(n // width, 1)
        out.append(buf[off:off + r].reshape(shp) if n % width == 0 else buf[off, :n].reshape(shp))
        off += r + (-r) % 8
    return out


def _shape2d(shp):
    return (math.prod(shp[:-1]), shp[-1])


def _unshard(g, axis):
    g = jnp.moveaxis(g, 0, axis)
    shp = g.shape
    return g.reshape(shp[:axis] + (shp[axis] * shp[axis + 1],) + shp[axis + 2:])


def _blockdiag(blocks):
    g, r, c = blocks.shape
    eye = jnp.eye(g, dtype=blocks.dtype)
    return (eye[:, None, :, None] * blocks[:, :, None, :]).reshape(g * r, g * c)


def _diagblocks(full, g):
    r, c = full.shape[0] // g, full.shape[1] // g
    return jnp.stack([full[k * r:(k + 1) * r, k * c:(k + 1) * c] for k in range(g)])


def kernel(x, meta_tokens, ffn1_norm, ffn1_w_gate_up, ffn1_w_down, mix_norm, ffn2_norm, ffn2_w_gate_up, ffn2_w_down, ssm_w_in, ssm_lambda_re, ssm_lambda_im, ssm_b_re, ssm_b_im, ssm_c_re, ssm_c_im, ssm_log_step, ssm_d, ssm_w_out, kv_norm, w_kv, k_norm, attn_w_q, q_norm, attn_sinks, attn_w_o, loss_target, m_meta_tokens, m_ffn1_norm, m_ffn1_w_gate_up, m_ffn1_w_down, m_mix_norm, m_ffn2_norm, m_ffn2_w_gate_up, m_ffn2_w_down, m_ssm_w_in, m_ssm_lambda_re, m_ssm_lambda_im, m_ssm_b_re, m_ssm_b_im, m_ssm_c_re, m_ssm_c_im, m_ssm_log_step, m_ssm_d, m_ssm_w_out, m_kv_norm, m_w_kv, m_k_norm, m_attn_w_q, m_q_norm, m_attn_sinks, m_attn_w_o, v_meta_tokens, v_ffn1_norm, v_ffn1_w_gate_up, v_ffn1_w_down, v_mix_norm, v_ffn2_norm, v_ffn2_w_gate_up, v_ffn2_w_down, v_ssm_w_in, v_ssm_lambda_re, v_ssm_lambda_im, v_ssm_b_re, v_ssm_b_im, v_ssm_c_re, v_ssm_c_im, v_ssm_log_step, v_ssm_d, v_ssm_w_out, v_kv_norm, v_w_kv, v_k_norm, v_attn_w_q, v_q_norm, v_attn_sinks, v_attn_w_o):
    args = dict(locals())
    W = {n: args[n] for n in WEIGHTS}
    M = {n: args["m_" + n] for n in WEIGHTS}
    V = {n: args["v_" + n] for n in WEIGHTS}
    my_x, my_y, my_c = (lax.axis_index(a) for a in MESH_AXES)
    my_dev = 4 * my_x + 2 * my_y + my_c

    big_names = [n for n, _ in BIG]
    s2d = {n: _shape2d(W[n].shape) for n in big_names}
    col_w = W["meta_tokens"].shape[1]

    grads, summed, small_parts = _local_step(x, loss_target, W, my_c.astype(jnp.int32).reshape(1))
    loss = lax.psum(grads.pop("loss"), MESH_AXES)
    grad_x = grads.pop("x")

    outs = [{}, {}, {}, {}]
    for n in big_names:
        r4 = _adamw("adamw_" + n, W[n].reshape(s2d[n]), M[n].reshape(s2d[n]), V[n].reshape(s2d[n]), summed[n])
        for k in range(4):
            outs[k][n] = r4[k].reshape(W[n].shape)

    small_names = list(SMALL) + [n for n, _ in COLS]
    small_shapes = [grads[n].shape for n in small_names]
    zero_cols = [jnp.zeros(grads[n].shape, F32) for n, _ in COLS]
    packs = lambda d: _pack_small([d[n] for n in SMALL] + zero_cols, PACK_W)
    r4 = _adamw("adamw_small", packs(W), packs(M), packs(V), small_parts)
    gsmall = None
    for k in range(4):
        un = dict(zip(small_names, _unpack_small(r4[k], small_shapes, PACK_W)))
        gsmall = un if k == 0 else gsmall
        outs[k].update({n: un[n] for n in SMALL})
    col_g = [lax.dynamic_slice_in_dim(gsmall[n], my_dev * W[n].shape[1], W[n].shape[1], axis=1) for n, _ in COLS]
    packc = lambda d: _pack_small([d[n] for n, _ in COLS], col_w)
    r4 = _adamw("adamw_cols", packc(W), packc(M), packc(V), _pack_small(col_g, col_w)[None])
    col_shapes = [W[n].shape for n, _ in COLS]
    for k in range(4):
        outs[k].update(dict(zip([n for n, _ in COLS], _unpack_small(r4[k], col_shapes, col_w))))

    res = [[outs[k][n] for n in WEIGHTS] for k in range(4)]
    return (loss, grad_x, *res[0], *res[1], *res[2], *res[3])


def _local_step(x, target, P, c_arr):
    bsz, seq, d = x.shape
    lp = seq + PAD
    rows = bsz * lp
    depth = P["ffn1_norm"].shape[0]
    assert depth == 2
    bf = lambda a: a.astype(BF16)
    row = lambda a: a.reshape(1, -1)

    def shard(n, l=None):
        a = P[n] if l is None else P[n][l]
        return bf(a.reshape(_shape2d(a.shape)))

    shard_t = lambda n, l: shard(n, l).T
    rowsharded = lambda g: g.reshape((g.shape[0] * g.shape[1],) + g.shape[2:])
    colsharded = lambda g: _unshard(g, 1)
    col_w = P["meta_tokens"].shape[1]
    g0 = _run_rider("gather_first", _gather_rider(
        [shard_t("ffn1_w_gate_up", 0), shard("ffn1_w_down", 0), shard("ssm_w_in", 0),
         _pack_small([P["meta_tokens"], P["ssm_d"]], col_w)]))
    ffn_w = {("ffn1", 0): (rowsharded(g0[0]), rowsharded(g0[1]))}
    w_in = rowsharded(g0[2])
    meta_full = _unshard(g0[3][:, :N_META], 1)
    dvec = _unshard(g0[3][:, N_META:N_META + 1, :P["ssm_d"].shape[1]], 1)

    pos = (jnp.arange(lp, dtype=F32) - float(META0))[:, None]
    half = HEAD_DIM // 2
    freqs = ROPE_THETA ** (-jnp.arange(0, half, dtype=F32) * 2.0 / HEAD_DIM)
    ang = pos * freqs[None, :]
    cos_t = jnp.tile(jnp.cos(ang), (1, LANES // half))
    sin_t = jnp.tile(jnp.concatenate([-jnp.sin(ang), jnp.sin(ang)], axis=1), (1, LANES // HEAD_DIM))
    gi = jnp.arange(LANES) // HEAD_DIM
    gmat = jnp.where(gi[:, None] == gi[None, :], 1.0 / HEAD_DIM, 0.0).astype(BF16)

    g_n, c_n, p_n = P["ssm_lambda_re"].shape[1], SSM_GROUP, SSM_STATE
    ns = g_n * p_n
    lr = P["ssm_lambda_re"][0].reshape(g_n, 1, p_n)
    li = P["ssm_lambda_im"][0].reshape(g_n, 1, p_n)
    ls = P["ssm_log_step"][0].reshape(g_n, 1, 1)
    brt = P["ssm_b_re"][0].transpose(0, 2, 1)
    bit = P["ssm_b_im"][0].transpose(0, 2, 1)
    ar, ai, bbr, bbi = _s5_params_fwd(lr, li, ls, brt, bit)
    a2 = jnp.concatenate([ar.reshape(1, ns), ai.reshape(1, ns)], axis=0)
    bfull = jnp.concatenate([_blockdiag(bbr), _blockdiag(bbi)], axis=1)
    cre_t = P["ssm_c_re"][0].transpose(0, 2, 1)
    cim_t = P["ssm_c_im"][0].transpose(0, 2, 1)
    cfull = jnp.concatenate([_blockdiag(cre_t), -_blockdiag(cim_t)], axis=0)

    ffn = lambda which, l: (row(P[which + "_norm"][l]),) + ffn_w[which, l]
    mix0, mix1, kvn = row(P["mix_norm"][0]), row(P["mix_norm"][1]), row(P["kv_norm"])
    kgain = jnp.tile(P["k_norm"].reshape(1, HEAD_DIM), (1, KVW // HEAD_DIM))
    qgain = jnp.tile(P["q_norm"].reshape(1, HEAD_DIM), (1, d // HEAD_DIM))
    sinks = P["attn_sinks"].reshape(1, -1)

    h0 = _embed(x, meta_full).reshape(rows, d)
    h1, ab_f1_0, g_wout, g_gu, g_d, g_kv = _ffn_fwd("ffn1_0_fwd", h0, *ffn("ffn1", 0), rider=_gather_rider(
        [shard("ssm_w_out", 0), shard_t("ffn2_w_gate_up", 0), shard("ffn2_w_down", 0), shard("w_kv")]))
    w_out, w_kv = colsharded(g_wout), rowsharded(g_kv)
    ffn_w["ffn2", 0] = (rowsharded(g_gu), rowsharded(g_d))
    u = _proj_fwd("ssm_in_fwd", h1, mix0, w_in)
    y, xs, u_perm = _s5_scan_fwd(u, bf(bfull), bf(cfull), a2, dvec, bsz)
    h2 = _glu_fwd(y, h1, w_out)
    h3, ab_f2_0, g_gu, g_d, g_q, g_o = _ffn_fwd("ffn2_0_fwd", h2, *ffn("ffn2", 0), rider=_gather_rider(
        [shard_t("ffn1_w_gate_up", 1), shard("ffn1_w_down", 1), shard("attn_w_q", 0), shard("attn_w_o", 0)]))
    w_q, w_o = rowsharded(g_q), rowsharded(g_o)
    ffn_w["ffn1", 1] = (rowsharded(g_gu), rowsharded(g_d))
    kv = _proj_fwd("kv_fwd", h3, kvn, w_kv)
    k = _headrope_fwd("k_rope_fwd", kv, KVW, kgain, cos_t, sin_t, gmat, lp)
    h4, ab_f1_1, g_gu, g_d = _ffn_fwd("ffn1_1_fwd", h3, *ffn("ffn1", 1), rider=_gather_rider(
        [shard_t("ffn2_w_gate_up", 1), shard("ffn2_w_down", 1)]))
    ffn_w["ffn2", 1] = (rowsharded(g_gu), rowsharded(g_d))
    q_raw = _proj_fwd("q_fwd", h4, mix1, w_q)
    q = _headrope_fwd("q_rope_fwd", q_raw, d, qgain, cos_t, sin_t, gmat, lp)
    r3 = lambda a: a.reshape(bsz, lp, a.shape[-1])
    o = _attn_fwd(r3(q), r3(k), r3(kv), sinks).reshape(rows, d)
    h5 = _lin_res_fwd("attn_out_fwd", o, w_o, h4)
    dh6, ab_f2_1, loss = _ffn_fwd("ffn2_1_fwd", h5, *ffn("ffn2", 1), loss_target=target.reshape(bsz * seq, d), lp=lp)

    G = {"loss": loss[0, 0]}

    def ffn_back(name, which, l, h, ab, dout, rider=None):
        g, wgu, wd = ffn(which, l)
        dh, hn, dab, act, dg, *rode = _ffn_bwd(name, h, ab, dout, g, wgu, wd, rider=rider)
        dwgu_t = _mm_tn(name + "_wgu", dab, hn)
        parts = [slots(dwgu_t), _mm_tn_slots(name + "_wd", act, dout, 0.5)]
        return dh, dg, parts, rode

    slots = lambda g: g.reshape((N_DEV, g.shape[0] // N_DEV) + g.shape[1:])
    swap_of = lambda parts: _swap_rider([p.reshape((4, 2) + p.shape[1:]) for p in parts])

    def pair_sums(tag, parts, theirs):
        return [_pair_sum("pair_sum_%s_%d" % (tag, k), p.reshape((4, 2) + p.shape[1:]), t, c_arr)
                for k, (p, t) in enumerate(zip(parts, theirs))]

    dh5, dg_f2_1, parts_a, _ = ffn_back("ffn2_1_bwd", "ffn2", 1, h5, ab_f2_1, dh6)
    do, dw_o, *theirs = _lin_bwd("attn_out_bwd", o, w_o, dh5, rider=swap_of(parts_a))
    sums_a = pair_sums("ffn2_1", parts_a, theirs)
    dq, dk, dv, dsinks = _attn_bwd(r3(q), r3(k), r3(kv), sinks, r3(o), r3(do))
    dq_raw, dqg = _headrope_bwd("q_rope_bwd", q_raw, d, dq.reshape(rows, d), qgain, cos_t, sin_t, gmat, lp)
    dh4, dg_mix1, dw_q = _proj_bwd("q_bwd", h4, mix1, w_q, dq_raw, dh5)
    dh3, dg_f1_1, parts_b, red_a = ffn_back("ffn1_1_bwd", "ffn1", 1, h3, ab_f1_1, dh4, rider=_scatter_rider(sums_a))
    dkv, dkg = _headrope_bwd("k_rope_bwd", kv, KVW, dk.reshape(rows, KVW), kgain, cos_t, sin_t, gmat, lp,
                             tail=dv.reshape(rows, KVW))
    parts_b = parts_b + [slots(dw_q), slots(dw_o)]
    dh3, dg_kvn, dw_kv, *theirs = _proj_bwd("kv_bwd", h3, kvn, w_kv, dkv, dh3, rider=swap_of(parts_b))
    sums_b = pair_sums("ffn1_1", parts_b, theirs)
    dh2, dg_f2_0, parts_c, red_b = ffn_back("ffn2_0_bwd", "ffn2", 0, h2, ab_f2_0, dh3, rider=_scatter_rider(sums_b))
    parts_c = parts_c + [slots(dw_kv)]
    dy, dw_out, *theirs = _glu_bwd(y, dh2, w_out, rider=swap_of(parts_c))
    sums_c = pair_sums("ffn2_0", parts_c, theirs)
    ctfull = jnp.concatenate([_blockdiag(P["ssm_c_re"][0]), -_blockdiag(P["ssm_c_im"][0])], axis=1)
    btfull = jnp.concatenate([_blockdiag(bbr.transpose(0, 2, 1)), _blockdiag(bbi.transpose(0, 2, 1))], axis=0)
    du, gx, dy_perm, da, dd = _s5_scan_bwd(dy, u, xs, bf(ctfull), bf(btfull), a2, dvec, bsz)
    dbfull = _mm_tn_blockdiag("ssm_db", u_perm, gx, False)
    dcfull = _mm_tn_blockdiag("ssm_dc", xs, dy_perm, True)
    dh1, dg_mix0, dw_in = _proj_bwd("ssm_in_bwd", h1, mix0, w_in, du, dh2)
    dh0, dg_f1_0, parts_d, red_c = ffn_back("ffn1_0_bwd", "ffn1", 0, h0, ab_f1_0, dh1, rider=_scatter_rider(sums_c))
    dbbr = _diagblocks(dbfull[:, :ns], g_n)
    dbbi = _diagblocks(dbfull[:, ns:], g_n)
    dlr, dli, dls, dbrt, dbit = _s5_params_bwd(lr, li, ls, brt, bit, da[:, :ns].reshape(g_n, 1, p_n),
                                               da[:, ns:].reshape(g_n, 1, p_n), dbbr, dbbi)
    dh0 = r3(dh0)
    G["x"] = dh0[:, PAD:, :]
    G["meta_tokens"] = _meta_sum(dh0)
    G["ffn1_norm"] = jnp.concatenate([dg_f1_0, dg_f1_1], axis=0)
    G["ffn2_norm"] = jnp.concatenate([dg_f2_0, dg_f2_1], axis=0)
    G["mix_norm"] = jnp.concatenate([dg_mix0, dg_mix1], axis=0)
    G["ssm_lambda_re"] = dlr.reshape(1, g_n, p_n)
    G["ssm_lambda_im"] = dli.reshape(1, g_n, p_n)
    G["ssm_log_step"] = dls.reshape(1, g_n)
    G["ssm_b_re"] = dbrt.transpose(0, 2, 1)[None]
    G["ssm_b_im"] = dbit.transpose(0, 2, 1)[None]
    G["ssm_c_re"] = _diagblocks(dcfull[:ns], g_n).transpose(0, 2, 1)[None]
    G["ssm_c_im"] = -_diagblocks(dcfull[ns:], g_n).transpose(0, 2, 1)[None]
    G["ssm_d"] = dd
    G["kv_norm"] = dg_kvn.reshape(-1)
    G["k_norm"] = dkg[0, :HEAD_DIM]
    G["q_norm"] = dqg[:, :HEAD_DIM]
    G["attn_sinks"] = dsinks[:, :N_KV_HEADS * Q_PER_KV]

    parts_d = parts_d + [slots(dw_in), dw_out]
    small_pack = _pack_small([G[n] for n in list(SMALL) + [n for n, _ in COLS]], PACK_W)
    *theirs, small_parts = _run_rider("grad_swap_last", _join_riders(swap_of(parts_d), _gather_rider([small_pack])))
    red_d = _run_rider("grad_scatter_last", _scatter_rider(pair_sums("last", parts_d, theirs)))
    both = lambda lo, hi: jnp.concatenate([lo, hi], axis=1)
    gu = lambda lo, hi: both(jnp.swapaxes(lo, 1, 2), jnp.swapaxes(hi, 1, 2))
    summed = {"ffn1_w_gate_up": gu(red_d[0], red_b[0]), "ffn1_w_down": both(red_d[1], red_b[1]),
              "ffn2_w_gate_up": gu(red_c[0], red_a[0]), "ffn2_w_down": both(red_c[1], red_a[1]),
              "ssm_w_in": red_d[2], "ssm_w_out": red_d[3], "w_kv": red_c[2], "attn_w_q": red_b[2],
              "attn_w_o": red_b[3]}
    return G, summed, small_parts
```

```python
import functools
import math

import jax
import jax.numpy as jnp
from jax import lax
from jax.experimental import pallas as pl
from jax.experimental.pallas import tpu as pltpu

F32 = jnp.float32
BF16 = jnp.bfloat16

N_META = 16
PAD = 128
META0 = PAD - N_META
HEAD_DIM = 64
N_KV_HEADS = 4
Q_PER_KV = 4
SSM_GROUP = 16
SSM_STATE = 64
EPS = 1e-6
NEG_INF = -1e30
ROPE_THETA = 10000.0
ADAM_LR, ADAM_B1, ADAM_B2, ADAM_EPS, ADAM_WD, ADAM_STEP = 0.001, 0.9, 0.999, 1e-08, 0.01, 10
LANES = 128
PACK_W = 1024
VMEM_LIMIT = 56 * 1024 * 1024
MESH_AXES = ("x", "y", "c")
N_DEV = 8


def _cparams(sem=None):
    return pltpu.CompilerParams(dimension_semantics=sem, vmem_limit_bytes=VMEM_LIMIT)


def _row_tile(rows, light=False):
    for tm in ((768,) if light else ()) + (384, 256, 128, 64, 32, 16, 8):
        if rows % tm == 0:
            return tm
    raise ValueError(rows)


STREAM_BUDGET = 32 * 1024 * 1024


def _stream_tile(rows, bytes_per_row):
    for tm in range(rows, 0, -1):
        if rows % tm == 0 and (tm % 16 == 0 or tm == rows) and 2 * tm * bytes_per_row <= STREAM_BUDGET:
            return tm
    raise ValueError(rows)


TN_BUDGET = 52 * 1024 * 1024
TN_MAX_ROWS = 2816


def _tn_tile(rows, a, b, k1, tn):
    sa, sb = a.dtype.itemsize, b.dtype.itemsize
    fits = lambda tm: 2 * tm * (k1 * sa + tn * sb) + 3 * k1 * tn * 4 + tm * k1 * 2 <= TN_BUDGET
    divisors = [tm for tm in range(min(rows, TN_MAX_ROWS), 7, -8) if rows % tm == 0 and fits(tm)]
    good = [tm for tm in divisors if -(-tm // MXU_DIM) * MXU_DIM <= 1.1 * tm]
    if good or divisors:
        return (good or divisors)[0]
    raise ValueError(rows)


def _dot(a, b):
    return jnp.dot(a.astype(BF16), b.astype(BF16), preferred_element_type=F32)


def _dot_nt(a, b):
    return lax.dot_general(a.astype(BF16), b.astype(BF16), (((1,), (1,)), ((), ())), preferred_element_type=F32)


def _dot_tn(a, b):
    return lax.dot_general(a.astype(BF16), b.astype(BF16), (((0,), (0,)), ((), ())), preferred_element_type=F32)


def _rms(x, g):
    rstd = lax.rsqrt(jnp.mean(x * x, axis=-1, keepdims=True) + EPS)
    y = x * rstd
    return y * g, y, rstd


def _rms_bwd(dhn, y, rstd, g):
    dyn = dhn * g
    dx = rstd * (dyn - y * jnp.mean(dyn * y, axis=-1, keepdims=True))
    return dx, jnp.sum(dhn * y, axis=0, keepdims=True)


def _sigmoid(x):
    return 1.0 / (1.0 + jnp.exp(-x))


_GELU_C = math.sqrt(2.0 / math.pi)


def _gelu(y):
    t = jnp.tanh(_GELU_C * (y + 0.044715 * y * y * y))
    return 0.5 * y * (1.0 + t), t


def _gelu_grad(y, t):
    return 0.5 * (1.0 + t) + 0.5 * y * (1.0 - t * t) * _GELU_C * (1.0 + 3.0 * 0.044715 * y * y)


class _Rider:
    def __init__(self, ins, outs, sems, start, mid, finish):
        self.ins, self.outs, self.sems, self.start, self.mid, self.finish = ins, outs, sems, start, mid, finish


def _join_riders(r1, r2):
    ni, no, ns = len(r1.ins), len(r1.outs), len(r1.sems)

    def both(f1, f2):
        def phase(ins, outs, sems):
            if f1 is not None:
                f1(ins[:ni], outs[:no], sems[:ns])
            if f2 is not None:
                f2(ins[ni:], outs[no:], sems[ns:])
        return phase

    mid = both(r1.mid, r2.mid) if (r1.mid is not None or r2.mid is not None) else None
    return _Rider(r1.ins + r2.ins, r1.outs + r2.outs, r1.sems + r2.sems,
                  both(r1.start, r2.start), mid, both(r1.finish, r2.finish))


def _run_rider(name, rider):
    def kern(*refs):
        ni, no = len(rider.ins), len(rider.outs)
        parts = refs[:ni], refs[ni:ni + no], refs[ni + no:]
        rider.start(*parts)
        if rider.mid is not None:
            rider.mid(*parts)
        rider.finish(*parts)

    return pl.pallas_call(
        kern, name=name, out_shape=list(rider.outs), in_specs=[ANY] * len(rider.ins),
        out_specs=[ANY] * len(rider.outs), scratch_shapes=list(rider.sems),
    )(*rider.ins)


def _rowcall(name, body, rows, row_ins, const_ins, row_outs, acc_outs=(), tm=None, row_in_maps=None, rider=None,
             light=False):
    tm = tm or _row_tile(rows, light)
    steps = rows // tm
    in_specs = []
    for k, a in enumerate(row_ins):
        if row_in_maps is not None and row_in_maps[k] is not None:
            in_specs.append(pl.BlockSpec(*row_in_maps[k]))
        else:
            in_specs.append(pl.BlockSpec((tm, a.shape[1]), lambda i: (i, 0)))
    for a in const_ins:
        in_specs.append(pl.BlockSpec(a.shape, lambda i, nd=a.ndim: (0,) * nd, pipeline_mode=pl.Buffered(1)))
    out_shape, out_specs = [], []
    for w, dt in row_outs:
        out_shape.append(jax.ShapeDtypeStruct((rows, w), dt))
        out_specs.append(pl.BlockSpec((tm, w), lambda i: (i, 0)))
    for shp, dt in acc_outs:
        out_shape.append(jax.ShapeDtypeStruct(shp, dt))
        out_specs.append(pl.BlockSpec(shp, lambda i, nd=len(shp): (0,) * nd))

    if rider is None:
        def kern(*refs):
            body(pl.program_id(0), *refs)

        return pl.pallas_call(
            kern, name=name, grid=(steps,), in_specs=in_specs, out_specs=out_specs, out_shape=out_shape,
            compiler_params=_cparams(("arbitrary",)),
        )(*row_ins, *const_ins)

    n_in, n_out = len(in_specs), len(out_specs)
    r_in, r_out = len(rider.ins), len(rider.outs)

    def kern_r(*refs):
        step = pl.program_id(0)
        ins, rins = refs[:n_in], refs[n_in:n_in + r_in]
        outs = refs[n_in + r_in:n_in + r_in + n_out]
        routs = refs[n_in + r_in + n_out:n_in + r_in + n_out + r_out]
        sems = refs[n_in + r_in + n_out + r_out:]

        @pl.when(step == 0)
        def _():
            rider.start(rins, routs, sems)

        if rider.mid is not None:
            @pl.when(step == (3 * steps) // 4)
            def _():
                rider.mid(rins, routs, sems)

        body(step, *ins, *outs)

        @pl.when(step == steps - 1)
        def _():
            rider.finish(rins, routs, sems)

    return pl.pallas_call(
        kern_r, name=name, grid=(steps,), in_specs=in_specs + [ANY] * r_in, out_specs=out_specs + [ANY] * r_out,
        out_shape=out_shape + list(rider.outs), scratch_shapes=list(rider.sems),
        compiler_params=_cparams(("arbitrary",)),
    )(*row_ins, *const_ins, *rider.ins)


def _acc(step, ref, val):
    @pl.when(step == 0)
    def _():
        ref[...] = val

    @pl.when(step != 0)
    def _():
        ref[...] += val


def _embed(x, meta):
    bsz, seq, d = x.shape
    nb = seq // PAD + 1

    def kern(x_ref, m_ref, o_ref):
        i = pl.program_id(1)

        @pl.when(i == 0)
        def _():
            o_ref[0, 0:META0, :] = jnp.zeros((META0, d), F32)
            o_ref[0, META0:PAD, :] = m_ref[...]

        @pl.when(i != 0)
        def _():
            o_ref[0] = x_ref[0]

    return pl.pallas_call(
        kern, name="embed", grid=(bsz, nb),
        in_specs=[pl.BlockSpec((1, PAD, d), lambda b, i: (b, jnp.maximum(i - 1, 0), 0)),
                  pl.BlockSpec((N_META, d), lambda b, i: (0, 0))],
        out_specs=pl.BlockSpec((1, PAD, d), lambda b, i: (b, i, 0)),
        out_shape=jax.ShapeDtypeStruct((bsz, seq + PAD, d), F32),
        compiler_params=_cparams(("arbitrary", "arbitrary")),
    )(x, meta)


def _meta_sum(dh0):
    bsz, lp, d = dh0.shape

    def kern(d_ref, o_ref):
        _acc(pl.program_id(0), o_ref, d_ref[0, META0:PAD, :])

    return pl.pallas_call(
        kern, name="meta_sum", grid=(bsz,),
        in_specs=[pl.BlockSpec((1, PAD, d), lambda b: (b, 0, 0))],
        out_specs=pl.BlockSpec((N_META, d), lambda b: (0, 0)),
        out_shape=jax.ShapeDtypeStruct((N_META, d), F32),
        compiler_params=_cparams(("arbitrary",)),
    )(dh0)


MXU_DIM = 256


def _ffn_chunks(f):
    unit = MXU_DIM if f % MXU_DIM == 0 else LANES
    assert f % unit == 0
    first = (f // unit + 1) // 2 * unit
    return [(0, first), (first, f)] if first < f else [(0, f)]


def _ffn_fwd(name, h, g, wgu, wd, rider=None, loss_target=None, lp=None):
    rows, d = h.shape
    f = wd.shape[0]
    chunks = _ffn_chunks(f)
    tm = _row_tile(rows)
    nblk = tm // PAD

    def body(step, h_ref, *refs):
        t_refs, (g_ref, wgu_ref, wd_ref, o_ref, ab_ref), l_refs = refs[:nt], refs[nt:nt + 5], refs[nt + 5:]
        hx = h_ref[...]
        hb = _rms(hx, g_ref[...])[0].astype(BF16)
        acc = jnp.zeros(hx.shape, F32)
        for lo, hi in chunks:
            ga, ua = slice(lo, hi), slice(f + lo, f + hi)
            a = _dot_nt(hb, wgu_ref[ga, :])
            b = _dot_nt(hb, wgu_ref[ua, :])
            ab_ref[:, ga] = a.astype(BF16)
            ab_ref[:, ua] = b.astype(BF16)
            acc = acc + _dot(a * _sigmoid(a) * b, wd_ref[ga, :])
        out = hx + 0.5 * acc
        if not nt:
            o_ref[...] = out
            return
        err = out - jnp.concatenate([t[...] for t in t_refs], axis=0)
        rid = lax.broadcasted_iota(jnp.int32, (tm, 1), 0)
        err = jnp.where((step % per == 0) & (rid < PAD), 0.0, err)
        o_ref[...] = err * (1.0 / d)
        part = 0.5 * jnp.sum(jnp.mean(err * err, axis=-1, keepdims=True))
        _acc(step, l_refs[0], jnp.broadcast_to(part, (1, LANES)))

    if loss_target is None:
        nt = 0
        return _rowcall(name, body, rows, [h], [g, wgu, wd], [(d, F32), (2 * f, BF16)], rider=rider, tm=tm)
    assert tm % PAD == 0 and lp % tm == 0 and rider is None
    nt, per = nblk, lp // tm
    tblocks = (lp - PAD) // PAD

    def tmap(k):
        return lambda i: ((i // per) * tblocks + jnp.clip((i % per) * nblk - 1 + k, 0, tblocks - 1), 0)

    maps = [None] + [((PAD, d), tmap(k)) for k in range(nblk)]
    return _rowcall(name, body, rows, [h] + [loss_target] * nblk, [g, wgu, wd], [(d, F32), (2 * f, BF16)],
                    [((1, LANES), F32)], tm=tm, row_in_maps=maps)


def _ffn_bwd(name, h, ab, dout, g, wgu, wd, rider=None):
    rows, d = h.shape
    f = wd.shape[0]
    chunks = _ffn_chunks(f)

    def body(step, h_ref, ab_ref, do_ref, g_ref, wgu_ref, wd_ref, dh_ref, hn_ref, dab_ref, act_ref, dg_ref):
        hx, dout_x, gx = h_ref[...], do_ref[...], g_ref[...]
        hn, y, rstd = _rms(hx, gx)
        hn_ref[...] = hn.astype(BF16)
        dhalf = (0.5 * dout_x).astype(BF16)
        dhn = jnp.zeros(hx.shape, F32)
        for lo, hi in chunks:
            ga, ua = slice(lo, hi), slice(f + lo, f + hi)
            a = ab_ref[:, ga].astype(F32)
            b = ab_ref[:, ua].astype(F32)
            s = _sigmoid(a)
            silu = a * s
            act_ref[:, ga] = (silu * b).astype(BF16)
            dact = _dot_nt(dhalf, wd_ref[ga, :])
            da = (dact * b * (s + silu * (1.0 - s))).astype(BF16)
            db = (dact * silu).astype(BF16)
            dab_ref[:, ga] = da
            dab_ref[:, ua] = db
            dhn = dhn + _dot(da, wgu_ref[ga, :]) + _dot(db, wgu_ref[ua, :])
        dx, dg = _rms_bwd(dhn, y, rstd, gx)
        dh_ref[...] = dout_x + dx
        _acc(step, dg_ref, dg)

    return _rowcall(name, body, rows, [h, ab, dout], [g, wgu, wd],
                    [(d, F32), (d, BF16), (2 * f, BF16), (f, BF16)], [((1, d), F32)], rider=rider)


def _mm_tn(name, a, b):
    rows, k1 = a.shape
    k2 = b.shape[1]
    tk = max(t for t in range(LANES, k1 + 1, LANES) if k1 % t == 0 and (t * k2 * 4 <= 6 * 1024 * 1024 or t == LANES))
    tm = _tn_tile(rows, a, b, tk, k2)
    steps = rows // tm

    def kern(a_ref, b_ref, o_ref):
        _acc(pl.program_id(1), o_ref, _dot_tn(a_ref[...], b_ref[...]))

    return pl.pallas_call(
        kern, name=name, grid=(k1 // tk, steps),
        in_specs=[pl.BlockSpec((tm, tk), lambda j, i: (i, j)), pl.BlockSpec((tm, k2), lambda j, i: (i, 0))],
        out_specs=pl.BlockSpec((tk, k2), lambda j, i: (j, 0)),
        out_shape=jax.ShapeDtypeStruct((k1, k2), F32),
        compiler_params=_cparams(("arbitrary", "arbitrary")),
    )(a, b)


def _mm_tn_slots(name, a, b, scale):
    rows, k1 = a.shape
    k2 = b.shape[1]
    tn = 512 if k2 % 512 == 0 else k2
    sr = k1 // N_DEV
    tm = _tn_tile(rows, a, b, k1, tn)
    steps = rows // tm

    def kern(a_ref, b_ref, o_ref):
        bx = b_ref[...]
        if scale != 1.0:
            bx = bx * scale
        res = _dot_tn(a_ref[...], bx)
        step = pl.program_id(1)
        for s in range(N_DEV):
            _acc(step, o_ref.at[s], res[s * sr:(s + 1) * sr])

    return pl.pallas_call(
        kern, name=name, grid=(k2 // tn, steps),
        in_specs=[pl.BlockSpec((tm, k1), lambda j, i: (i, 0)), pl.BlockSpec((tm, tn), lambda j, i: (i, j))],
        out_specs=pl.BlockSpec((N_DEV, sr, tn), lambda j, i: (0, 0, j)),
        out_shape=jax.ShapeDtypeStruct((N_DEV, sr, k2), F32),
        compiler_params=_cparams(("arbitrary", "arbitrary")),
    )(a, b)


def _proj_fwd(name, h, g, w):
    rows = h.shape[0]

    def body(step, h_ref, g_ref, w_ref, o_ref):
        o_ref[...] = _dot(_rms(h_ref[...], g_ref[...])[0], w_ref[...])

    return _rowcall(name, body, rows, [h], [g, w], [(w.shape[1], F32)], light=True)[0]


def _proj_bwd(name, h, g, w, dy, dres, rider=None):
    rows, d = h.shape

    def body(step, h_ref, dy_ref, dr_ref, g_ref, w_ref, dh_ref, dg_ref, dw_ref):
        gx = g_ref[...]
        hn, y, rstd = _rms(h_ref[...], gx)
        dyx = dy_ref[...]
        dx, dg = _rms_bwd(_dot_nt(dyx, w_ref[...]), y, rstd, gx)
        dh_ref[...] = dr_ref[...] + dx
        _acc(step, dg_ref, dg)
        _acc(step, dw_ref, _dot_tn(hn, dyx))

    return _rowcall(name, body, rows, [h, dy, dres], [g, w], [(d, F32)], [((1, d), F32), (w.shape, F32)],
                    rider=rider, light=True)


def _lin_res_fwd(name, a, w, res):
    rows = a.shape[0]

    def body(step, a_ref, r_ref, w_ref, o_ref):
        o_ref[...] = r_ref[...] + _dot(a_ref[...], w_ref[...])

    return _rowcall(name, body, rows, [a, res], [w], [(w.shape[1], F32)], light=True)[0]


def _lin_bwd(name, a, w, dy, rider=None):
    rows, k = a.shape

    def body(step, a_ref, dy_ref, w_ref, da_ref, dw_ref):
        dyx = dy_ref[...]
        da_ref[...] = _dot_nt(dyx, w_ref[...])
        _acc(step, dw_ref, _dot_tn(a_ref[...], dyx))

    return _rowcall(name, body, rows, [a, dy], [w], [(k, F32)], [(w.shape, F32)], rider=rider, light=True)


def _s5_param_fn(lr, li, ls, brt, bit):
    step = jnp.exp(ls)
    mag = jnp.exp(lr * step)
    ar = mag * jnp.cos(li * step)
    ai = mag * jnp.sin(li * step)
    den = lr * lr + li * li
    nr, ni = ar - 1.0, ai
    cr = (nr * lr + ni * li) / den
    ci = (ni * lr - nr * li) / den
    return ar, ai, cr * brt - ci * bit, cr * bit + ci * brt


def _s5_params_fwd(lr, li, ls, brt, bit):
    def kern(lr_ref, li_ref, ls_ref, br_ref, bi_ref, ar_ref, ai_ref, bbr_ref, bbi_ref):
        ar, ai, bbr, bbi = _s5_param_fn(lr_ref[...], li_ref[...], ls_ref[...], br_ref[...], bi_ref[...])
        ar_ref[...], ai_ref[...], bbr_ref[...], bbi_ref[...] = ar, ai, bbr, bbi

    sd = jax.ShapeDtypeStruct
    return pl.pallas_call(
        kern, name="s5_params_fwd",
        out_shape=[sd(lr.shape, F32), sd(lr.shape, F32), sd(brt.shape, F32), sd(brt.shape, F32)],
    )(lr, li, ls, brt, bit)


def _s5_params_bwd(lr, li, ls, brt, bit, dar, dai, dbbr, dbbi):
    def kern(lr_ref, li_ref, ls_ref, br_ref, bi_ref, dar_ref, dai_ref, dbbr_ref, dbbi_ref,
             dlr_ref, dli_ref, dls_ref, dbr_ref, dbi_ref):
        _, vjp = jax.vjp(_s5_param_fn, lr_ref[...], li_ref[...], ls_ref[...], br_ref[...], bi_ref[...])
        dlr, dli, dls, dbr, dbi = vjp((dar_ref[...], dai_ref[...], dbbr_ref[...], dbbi_ref[...]))
        dlr_ref[...], dli_ref[...], dls_ref[...], dbr_ref[...], dbi_ref[...] = dlr, dli, dls, dbr, dbi

    sd = jax.ShapeDtypeStruct
    return pl.pallas_call(
        kern, name="s5_params_bwd",
        out_shape=[sd(lr.shape, F32), sd(lr.shape, F32), sd(ls.shape, F32), sd(brt.shape, F32), sd(brt.shape, F32)],
    )(lr, li, ls, brt, bit, dar, dai, dbbr, dbbi)


SCAN_LW = 512


SCAN_SEGS = 8
SCAN_UNROLL = 8


def _cmul(xr, xi, yr, yi):
    return xr * yr - xi * yi, xr * yi + xi * yr


def _scan_tables(a_ref, tab_ref, conj, seg_len):
    ns = a_ref.shape[1]
    ar = jnp.broadcast_to(a_ref[0:1, :], (8, ns))
    ai = jnp.broadcast_to(a_ref[1:2, :], (8, ns))
    if conj:
        ai = -ai
    big, base, e = None, (ar, ai), seg_len
    while e:
        if e & 1:
            big = base if big is None else _cmul(*big, *base)
        base = _cmul(*base, *base)
        e >>= 1
    big2 = _cmul(*big, *big)
    big4 = _cmul(*big2, *big2)
    for k, v in enumerate((ar, ai) + big + big2 + big4):
        tab_ref[k] = v


def _scan_block(x_ref, tab_ref, carry_ref, t_rows, ns, reverse):
    sl = t_rows // SCAN_SEGS
    assert sl % SCAN_UNROLL == 0
    row = lax.broadcasted_iota(jnp.int32, (8, SCAN_LW), 0)
    zero = jnp.zeros((8, SCAN_LW), F32)
    for lc in range(ns // SCAN_LW):
        lre = pl.ds(lc * SCAN_LW, SCAN_LW)
        lim = pl.ds(ns + lc * SCAN_LW, SCAN_LW)
        ar, ai = tab_ref[0, :, lre], tab_ref[1, :, lre]

        def rows_of(k, u):
            j = k * SCAN_UNROLL + u
            return pl.ds(pl.multiple_of(((sl - 1 - j) if reverse else j) * SCAN_SEGS, SCAN_SEGS), SCAN_SEGS)

        def local(k, s, lre=lre, lim=lim, ar=ar, ai=ai):
            sr, si = s
            for u in range(SCAN_UNROLL):
                rows = rows_of(k, u)
                tr, ti = _cmul(ar, ai, sr, si)
                sr, si = x_ref[rows, lre] + tr, x_ref[rows, lim] + ti
                x_ref[rows, lre], x_ref[rows, lim] = sr, si
            return sr, si

        er, ei = lax.fori_loop(0, sl // SCAN_UNROLL, local, (zero, zero))
        if reverse:
            cr = jnp.where(row == 7, carry_ref[:, lre], pltpu.roll(er, 7, 0))
            ci = jnp.where(row == 7, carry_ref[:, lim], pltpu.roll(ei, 7, 0))
        else:
            cr = jnp.where(row == 0, carry_ref[:, lre], pltpu.roll(er, 1, 0))
            ci = jnp.where(row == 0, carry_ref[:, lim], pltpu.roll(ei, 1, 0))
        for lvl, dsh in enumerate((1, 2, 4)):
            pr, pi = tab_ref[2 + 2 * lvl, :, lre], tab_ref[3 + 2 * lvl, :, lre]
            if reverse:
                keep, shift = row < 8 - dsh, 8 - dsh
            else:
                keep, shift = row >= dsh, dsh
            sr = jnp.where(keep, pltpu.roll(cr, shift, 0), 0.0)
            si = jnp.where(keep, pltpu.roll(ci, shift, 0), 0.0)
            tr, ti = _cmul(pr, pi, sr, si)
            cr, ci = cr + tr, ci + ti
        tr, ti = _cmul(tab_ref[2, :, lre], tab_ref[3, :, lre], cr, ci)
        edge = 0 if reverse else 7
        carry_ref[:, lre] = jnp.broadcast_to((er + tr)[edge:edge + 1, :], (8, SCAN_LW))
        carry_ref[:, lim] = jnp.broadcast_to((ei + ti)[edge:edge + 1, :], (8, SCAN_LW))

        def fix(k, t, lre=lre, lim=lim, ar=ar, ai=ai):
            tr, ti = t
            for u in range(SCAN_UNROLL):
                rows = rows_of(k, u)
                tr, ti = _cmul(ar, ai, tr, ti)
                x_ref[rows, lre] = x_ref[rows, lre] + tr
                x_ref[rows, lim] = x_ref[rows, lim] + ti
            return tr, ti

        lax.fori_loop(0, sl // SCAN_UNROLL, fix, (cr, ci))


def _bd_expand(u, w_ref, x_ref, ns):
    hh, sh = u.shape[1] // 2, ns // 2
    ub = u.astype(BF16)
    for part in range(2):
        for k in range(2):
            cols = slice(part * ns + k * sh, part * ns + (k + 1) * sh)
            x_ref[:, cols] = jnp.dot(ub[:, k * hh:(k + 1) * hh], w_ref[k * hh:(k + 1) * hh, cols],
                                     preferred_element_type=F32)


def _bd_contract(x_ref, w_ref, ns):
    hh, sh = w_ref.shape[1] // 2, ns // 2
    halves = []
    for k in range(2):
        acc = None
        for part in range(2):
            rows = slice(part * ns + k * sh, part * ns + (k + 1) * sh)
            t = jnp.dot(x_ref[:, rows].astype(BF16), w_ref[rows, k * hh:(k + 1) * hh], preferred_element_type=F32)
            acc = t if acc is None else acc + t
        halves.append(acc)
    return jnp.concatenate(halves, axis=1)


def _scan_rows(lp):
    for t in (384, 256, 128):
        if lp % t == 0:
            return t
    raise ValueError(lp)


def _seg_perm(t_rows):
    r = jnp.arange(t_rows)
    src = (r % SCAN_SEGS) * (t_rows // SCAN_SEGS) + r // SCAN_SEGS
    p = (src[:, None] == r[None, :]).astype(BF16)
    return p, p.T


def _permute_rows(p_ref, v):
    return jnp.dot(p_ref[...], v.astype(BF16), preferred_element_type=F32)


def _unpermute_rows(pt_ref, v):
    hi = v.astype(BF16)
    lo = (v - hi.astype(F32)).astype(BF16)
    pt = pt_ref[...]
    return jnp.dot(pt, hi, preferred_element_type=F32) + jnp.dot(pt, lo, preferred_element_type=F32)


def _s5_scan_fwd(u, bfull, cfull, a2, dvec, bsz):
    rows, hw = u.shape
    ns = a2.shape[1]
    lp = rows // bsz
    t_rows = _scan_rows(lp)
    nc = lp // t_rows
    pmat, pmat_t = _seg_perm(t_rows)

    def kern(u_ref, b_ref, c_ref, a_ref, d_ref, p_ref, pt_ref, y_ref, x_ref, up_ref, tab_ref, carry_ref):
        c = pl.program_id(1)

        @pl.when((pl.program_id(0) == 0) & (c == 0))
        def _():
            _scan_tables(a_ref, tab_ref, False, t_rows // SCAN_SEGS)

        @pl.when(c == 0)
        def _():
            carry_ref[...] = jnp.zeros_like(carry_ref)

        ux = u_ref[...]
        up = _permute_rows(p_ref, ux)
        up_ref[...] = up.astype(BF16)
        _bd_expand(up, b_ref, x_ref, ns)
        _scan_block(x_ref, tab_ref, carry_ref, t_rows, ns, reverse=False)
        y_ref[...] = _unpermute_rows(pt_ref, _bd_contract(x_ref, c_ref, ns)) + d_ref[...] * ux

    const = lambda shp: pl.BlockSpec(shp, lambda b, c: (0,) * len(shp), pipeline_mode=pl.Buffered(1))
    blk = lambda b, c: (b * nc + c, 0)
    return pl.pallas_call(
        kern, name="s5_scan_fwd", grid=(bsz, nc),
        in_specs=[pl.BlockSpec((t_rows, hw), blk), const(bfull.shape), const(cfull.shape), const(a2.shape),
                  const(dvec.shape), const(pmat.shape), const(pmat.shape)],
        out_specs=[pl.BlockSpec((t_rows, hw), blk), pl.BlockSpec((t_rows, 2 * ns), blk),
                   pl.BlockSpec((t_rows, hw), blk)],
        out_shape=[jax.ShapeDtypeStruct((rows, hw), F32), jax.ShapeDtypeStruct((rows, 2 * ns), F32),
                   jax.ShapeDtypeStruct((rows, hw), BF16)],
        scratch_shapes=[pltpu.VMEM((8, 8, ns), F32), pltpu.VMEM((8, 2 * ns), F32)],
        compiler_params=_cparams(("arbitrary", "arbitrary")),
    )(u, bfull, cfull, a2, dvec, pmat, pmat_t)


def _s5_scan_bwd(dy, u, u_perm, xs, ctfull, btfull, a2, dvec, bsz):
    rows, hw = u.shape
    ns = a2.shape[1]
    lp = rows // bsz
    t_rows = _scan_rows(lp)
    nc = lp // t_rows
    hh, sh = hw // 2, ns // 2
    blk = lambda b, c: (b * nc + (nc - 1 - c), 0)
    pmat, pmat_t = _seg_perm(t_rows)

    def prev8(b, c):
        first = (b * nc + (nc - 1 - c)) * (t_rows // 8)
        return (jnp.maximum(first - 1, 0), 0)

    def kern(dy_ref, u_ref, up_ref, x_ref, xp_ref, ct_ref, bt_ref, a_ref, d_ref, p_ref, pt_ref,
             du_ref, da_ref, dd_ref, dbq_ref, dcq_ref, gx_ref, tab_ref, carry_ref, dbq_acc, dcq_acc, out_sems):
        b, c = pl.program_id(0), pl.program_id(1)
        first = (b == 0) & (c == 0)

        @pl.when(first)
        def _():
            _scan_tables(a_ref, tab_ref, True, t_rows // SCAN_SEGS)

        @pl.when(c == 0)
        def _():
            carry_ref[...] = jnp.zeros_like(carry_ref)

        dyx, ux = dy_ref[...], u_ref[...]
        dyp = _permute_rows(p_ref, dyx)
        _bd_expand(dyp, ct_ref, gx_ref, ns)
        _scan_block(gx_ref, tab_ref, carry_ref, t_rows, ns, reverse=True)
        gx = gx_ref[...]
        du_ref[...] = _unpermute_rows(pt_ref, _bd_contract(gx_ref, bt_ref, ns)) + d_ref[...] * dyx
        upx, dypb = up_ref[...], dyp.astype(BF16)
        step = b * nc + c
        for part in range(2):
            for k in range(2):
                cols = slice(part * ns + k * sh, part * ns + (k + 1) * sh)
                _acc(step, dbq_acc.at[2 * part + k], _dot_tn(upx[:, k * hh:(k + 1) * hh], gx_ref[:, cols]))
                _acc(step, dcq_acc.at[2 * part + k], _dot_tn(x_ref[:, cols], dypb[:, k * hh:(k + 1) * hh]))

        @pl.when((b == bsz - 1) & (c == nc - 1))
        def _():
            copies = [pltpu.make_async_copy(dbq_acc, dbq_ref, out_sems.at[0]),
                      pltpu.make_async_copy(dcq_acc, dcq_ref, out_sems.at[1])]
            for cp in copies:
                cp.start()
            for cp in copies:
                cp.wait()

        seq_start = c == nc - 1
        row8 = lax.broadcasted_iota(jnp.int32, (8, 1), 0)
        head = pltpu.roll(x_ref[t_rows - 8:t_rows, :], 1, 0)
        head = jnp.where(row8 == 0, jnp.where(seq_start, 0.0, xp_ref[7:8, :]), head)
        xprev = jnp.concatenate([head, x_ref[0:t_rows - 8, :]], axis=0)
        xr, xi, gr, gi = xprev[:, :ns], xprev[:, ns:], gx[:, :ns], gx[:, ns:]
        da = jnp.concatenate([jnp.sum(xr * gr + xi * gi, axis=0, keepdims=True),
                              jnp.sum(xr * gi - xi * gr, axis=0, keepdims=True)], axis=1)
        dd = jnp.sum(dyx * ux, axis=0, keepdims=True)

        @pl.when(first)
        def _():
            da_ref[...] = da
            dd_ref[...] = dd

        @pl.when(jnp.logical_not(first))
        def _():
            da_ref[...] += da
            dd_ref[...] += dd

    const = lambda shp: pl.BlockSpec(shp, lambda b, c: (0,) * len(shp), pipeline_mode=pl.Buffered(1))
    return pl.pallas_call(
        kern, name="s5_scan_bwd", grid=(bsz, nc),
        in_specs=[pl.BlockSpec((t_rows, hw), blk), pl.BlockSpec((t_rows, hw), blk), pl.BlockSpec((t_rows, hw), blk),
                  pl.BlockSpec((t_rows, 2 * ns), blk), pl.BlockSpec((8, 2 * ns), prev8),
                  const(ctfull.shape), const(btfull.shape), const(a2.shape), const(dvec.shape),
                  const(pmat.shape), const(pmat.shape)],
        out_specs=[pl.BlockSpec((t_rows, hw), blk),
                   pl.BlockSpec((1, 2 * ns), lambda b, c: (0, 0)), pl.BlockSpec((1, hw), lambda b, c: (0, 0)),
                   ANY, ANY],
        out_shape=[jax.ShapeDtypeStruct((rows, hw), F32),
                   jax.ShapeDtypeStruct((1, 2 * ns), F32), jax.ShapeDtypeStruct((1, hw), F32),
                   jax.ShapeDtypeStruct((4, hh, sh), F32), jax.ShapeDtypeStruct((4, sh, hh), F32)],
        scratch_shapes=[pltpu.VMEM((t_rows, 2 * ns), F32), pltpu.VMEM((8, 8, ns), F32), pltpu.VMEM((8, 2 * ns), F32),
                        pltpu.VMEM((4, hh, sh), F32), pltpu.VMEM((4, sh, hh), F32), pltpu.SemaphoreType.DMA((2,))],
        compiler_params=_cparams(("arbitrary", "arbitrary")),
    )(dy, u, u_perm, xs, xs, ctfull, btfull, a2, dvec, pmat, pmat_t)


def _glu_fwd(y, h1, wout):
    rows, d = h1.shape

    def body(step, y_ref, h_ref, w_ref, o_ref):
        z = _dot(_gelu(y_ref[...])[0], w_ref[...])
        o_ref[...] = h_ref[...] + z[:, :d] * _sigmoid(z[:, d:])

    return _rowcall("glu_fwd", body, rows, [y, h1], [wout], [(d, F32)], light=True)[0]


def _glu_bwd(y, dh2, wout, rider=None):
    rows, d = dh2.shape
    hw = y.shape[1]

    def body(step, y_ref, dh_ref, w_ref, dy_ref, dw_ref):
        yx, dh = y_ref[...], dh_ref[...]
        gl, t = _gelu(yx)
        z = _dot(gl, w_ref[...])
        za, sg = z[:, :d], _sigmoid(z[:, d:])
        dza = dh * sg
        dzg = dh * za * sg * (1.0 - sg)
        dgl = _dot_nt(dza, w_ref[:, :d]) + _dot_nt(dzg, w_ref[:, d:])
        dy_ref[...] = dgl * _gelu_grad(yx, t)
        for half, dz in enumerate((dza, dzg)):
            dw = _dot_tn(gl, dz)
            for s in range(N_DEV // 2):
                _acc(step, dw_ref.at[half * (N_DEV // 2) + s], dw[:, s * cw:(s + 1) * cw])

    cw = 2 * d // N_DEV
    return _rowcall("glu_bwd", body, rows, [y, dh2], [wout], [(hw, F32)], [((N_DEV, hw, cw), F32)], rider=rider,
                    light=True)


def _gmean64(x2, gmat):
    hi = x2.astype(BF16)
    r1 = x2 - hi.astype(F32)
    mid = r1.astype(BF16)
    lo = (r1 - mid.astype(F32)).astype(BF16)
    outs = []
    for j in range(x2.shape[1] // LANES):
        sl = slice(j * LANES, (j + 1) * LANES)
        f = lambda p: jnp.dot(p[:, sl], gmat, preferred_element_type=F32)
        outs.append(f(hi) + f(mid) + f(lo))
    return outs[0] if len(outs) == 1 else jnp.concatenate(outs, axis=1)


def _swap32(x):
    w = x.shape[1]
    lane = lax.broadcasted_iota(jnp.int32, (1, w), 1)
    return jnp.where((lane & 32) == 0, pltpu.roll(x, w - 32, 1), pltpu.roll(x, 32, 1))


def _tile_lanes(t, w):
    reps = w // t.shape[1]
    return t if reps == 1 else jnp.concatenate([t] * reps, axis=1)


def _headrope_fwd(name, raw, w, gain, cos, sin, gmat, lp):
    rows = raw.shape[0]
    tm = _row_tile(lp)
    per = lp // tm

    def body(step, x_ref, c_ref, s_ref, g_ref, gm_ref, o_ref):
        x = x_ref[...]
        rstd = lax.rsqrt(_gmean64(x * x, gm_ref[...]) + EPS)
        z = x * rstd * g_ref[...]
        o_ref[...] = z * _tile_lanes(c_ref[...], w) + _swap32(z) * _tile_lanes(s_ref[...], w)

    maps = [((tm, w), lambda i: (i, 0)), ((tm, LANES), lambda i: (i % per, 0)), ((tm, LANES), lambda i: (i % per, 0))]
    return _rowcall(name, body, rows, [raw, cos, sin], [gain, gmat], [(w, F32)], tm=tm, row_in_maps=maps)[0]


def _headrope_bwd(name, raw, w, dout, gain, cos, sin, gmat, lp, tail=None):
    rows = raw.shape[0]
    tm = _row_tile(lp)
    per = lp // tm
    wt = 0 if tail is None else tail.shape[1]

    def body(step, x_ref, do_ref, c_ref, s_ref, *refs):
        (g_ref, gm_ref, dx_ref, dg_ref), t_refs = refs[-4:], refs[:-4]
        x, dout_x, gx, gm = x_ref[...], do_ref[...], g_ref[...], gm_ref[...]
        rstd = lax.rsqrt(_gmean64(x * x, gm) + EPS)
        yn = x * rstd
        dz = dout_x * _tile_lanes(c_ref[...], w) + _swap32(dout_x * _tile_lanes(s_ref[...], w))
        dyn = dz * gx
        dx_ref[:, 0:w] = rstd * (dyn - yn * _gmean64(dyn * yn, gm))
        if t_refs:
            dx_ref[:, w:w + wt] = t_refs[0][...]
        dg = jnp.sum(dz * yn, axis=0, keepdims=True)
        sh = w // 2
        while sh >= HEAD_DIM:
            dg = dg + pltpu.roll(dg, sh, 1)
            sh //= 2
        _acc(step, dg_ref, dg)

    maps = [((tm, w), lambda i: (i, 0)), None, ((tm, LANES), lambda i: (i % per, 0)), ((tm, LANES), lambda i: (i % per, 0))]
    extra = [] if tail is None else [tail]
    return _rowcall(name, body, rows, [raw, dout, cos, sin] + extra, [gain, gmat], [(w + wt, F32)], [((1, w), F32)],
                    tm=tm, row_in_maps=maps + [None] * len(extra))


KVW = N_KV_HEADS * HEAD_DIM
QB = 128


def _fold4(x):
    y = x + pltpu.roll(x, 128, 1)
    return y + pltpu.roll(y, 64, 1)


ATTN_SCALE = HEAD_DIM ** -0.5


def _attn_masks(i):
    k0j = lax.broadcasted_iota(jnp.int32, (Q_PER_KV * QB, QB), 1)
    qi = lax.broadcasted_iota(jnp.int32, (Q_PER_KV * QB, 2 * QB), 0) % QB
    kj = lax.broadcasted_iota(jnp.int32, (Q_PER_KV * QB, 2 * QB), 1)
    in_prev = (kj < QB) & (kj > qi) & (i >= 2)
    in_cur = (kj >= QB) & (kj - QB <= qi)
    return k0j >= META0, in_prev | in_cur


def _attn_scores(i, q_ref, k0_ref, kp_ref, kc_ref, sink_ref, h):
    masks = _attn_masks(i)
    lane = lax.broadcasted_iota(jnp.int32, (1, KVW), 1) // HEAD_DIM
    qh = q_ref[:, h * KVW:(h + 1) * KVW]
    qs = jnp.concatenate([jnp.where(lane == g, qh, 0.0) for g in range(Q_PER_KV)], axis=0).astype(BF16)
    hsel = lane == h
    kx = _expand_kv((k0_ref, kp_ref, kc_ref), hsel)
    s0 = jnp.where(masks[0], _dot_nt(qs, kx[0]) * ATTN_SCALE, NEG_INF)
    sb = jnp.where(masks[1], _dot_nt(qs, kx[1]) * ATTN_SCALE, NEG_INF)
    rowg = lax.broadcasted_iota(jnp.int32, (Q_PER_KV * QB, 1), 0) // QB
    sink = jnp.zeros((Q_PER_KV * QB, 1), F32)
    for g in range(Q_PER_KV):
        sink = jnp.where(rowg == g, sink_ref[0, h * Q_PER_KV + g], sink)
    m = jnp.maximum(jnp.maximum(jnp.max(s0, axis=1, keepdims=True), jnp.max(sb, axis=1, keepdims=True)), sink)
    p0, pb, ps = jnp.exp(s0 - m), jnp.exp(sb - m), jnp.exp(sink - m)
    den = jnp.sum(p0, axis=1, keepdims=True) + jnp.sum(pb, axis=1, keepdims=True) + ps
    return qs, kx, (p0, pb), ps, den, lane, hsel


def _expand_kv(refs, hsel):
    x0, xp, xc = [_fold4(jnp.where(hsel, r[...], 0.0)).astype(BF16) for r in refs]
    return [x0, jnp.concatenate([xp, xc], axis=0)]


def _unstack(x, lane):
    out = jnp.where(lane == 0, x[0:QB], 0.0)
    for g in range(1, Q_PER_KV):
        out = out + jnp.where(lane == g, x[g * QB:(g + 1) * QB], 0.0)
    return out


def _attn_specs(nb, d):
    qspec = pl.BlockSpec((None, QB, d), lambda b, i: (b, i, 0))
    k0 = pl.BlockSpec((None, QB, KVW), lambda b, i: (b, 0, 0))
    kp = pl.BlockSpec((None, QB, KVW), lambda b, i: (b, jnp.maximum(i - 1, 0), 0))
    kc = pl.BlockSpec((None, QB, KVW), lambda b, i: (b, i, 0))
    v0 = pl.BlockSpec((None, QB, KVW), lambda b, i: (b, 0, 1))
    vp = pl.BlockSpec((None, QB, KVW), lambda b, i: (b, jnp.maximum(i - 1, 0), 1))
    vc = pl.BlockSpec((None, QB, KVW), lambda b, i: (b, i, 1))
    sink = pl.BlockSpec(memory_space=pltpu.SMEM)
    return qspec, [k0, kp, kc], [v0, vp, vc], sink


def _attn_fwd(q, k, kv, sinks):
    bsz, lp, d = q.shape
    nb = lp // QB
    qspec, kspecs, vspecs, sspec = _attn_specs(nb, d)

    def kern(q_ref, k0_ref, kp_ref, kc_ref, v0_ref, vp_ref, vc_ref, sink_ref, o_ref):
        i = pl.program_id(1)
        for h in range(N_KV_HEADS):
            qs, kx, ps3, psink, den, lane, hsel = _attn_scores(i, q_ref, k0_ref, kp_ref, kc_ref, sink_ref, h)
            vx = _expand_kv((v0_ref, vp_ref, vc_ref), hsel)
            o = _dot(ps3[0], vx[0]) + _dot(ps3[1], vx[1])
            o_ref[:, h * KVW:(h + 1) * KVW] = _unstack(o * (1.0 / den), lane)

    return pl.pallas_call(
        kern, name="attn_fwd", grid=(bsz, nb),
        in_specs=[qspec] + kspecs + vspecs + [sspec],
        out_specs=qspec, out_shape=jax.ShapeDtypeStruct((bsz, lp, d), F32),
        compiler_params=_cparams(("arbitrary", "arbitrary")),
    )(q, k, k, k, kv, kv, kv, sinks)


def _attn_bwd(q, k, kv, sinks, o, do):
    bsz, lp, d = q.shape
    nb = lp // QB
    qspec, kspecs, vspecs, sspec = _attn_specs(nb, d)
    full = pl.BlockSpec((None, lp, KVW), lambda b, i: (b, 0, 0))

    def kern(q_ref, k0_ref, kp_ref, kc_ref, v0_ref, vp_ref, vc_ref, sink_ref, o_ref, do_ref,
             dq_ref, dk_ref, dv_ref, ds_ref):
        b, i = pl.program_id(0), pl.program_id(1)

        @pl.when(i == 0)
        def _():
            dk_ref[...] = jnp.zeros_like(dk_ref)
            dv_ref[...] = jnp.zeros_like(dv_ref)

        @pl.when((b == 0) & (i == 0))
        def _():
            ds_ref[...] = jnp.zeros_like(ds_ref)

        lane128 = lax.broadcasted_iota(jnp.int32, (1, LANES), 1)
        rowg = lax.broadcasted_iota(jnp.int32, (Q_PER_KV * QB, 1), 0) // QB
        dk_acc = [jnp.zeros((QB, KVW), F32), jnp.zeros((2 * QB, KVW), F32)]
        dv_acc = [jnp.zeros((QB, KVW), F32), jnp.zeros((2 * QB, KVW), F32)]
        dsink = jnp.zeros((1, LANES), F32)
        for h in range(N_KV_HEADS):
            qs, kx, ps3, psink, den, lane, hsel = _attn_scores(i, q_ref, k0_ref, kp_ref, kc_ref, sink_ref, h)
            vx = _expand_kv((v0_ref, vp_ref, vc_ref), hsel)
            sl = slice(h * KVW, (h + 1) * KVW)
            doh, oh = do_ref[:, sl], o_ref[:, sl]
            dos = jnp.concatenate([jnp.where(lane == g, doh, 0.0) for g in range(Q_PER_KV)], axis=0)
            ost = jnp.concatenate([jnp.where(lane == g, oh, 0.0) for g in range(Q_PER_KV)], axis=0)
            delta = jnp.sum(dos * ost, axis=1, keepdims=True)
            inv = 1.0 / den
            dosb = dos.astype(BF16)
            dqs = jnp.zeros((Q_PER_KV * QB, KVW), F32)
            for n in range(2):
                pn = ps3[n] * inv
                ds = pn * (_dot_nt(dosb, vx[n]) - delta) * ATTN_SCALE
                dqs = dqs + _dot(ds, kx[n])
                dk_acc[n] = dk_acc[n] + jnp.where(hsel, _fold4(_dot_tn(ds, qs)), 0.0)
                dv_acc[n] = dv_acc[n] + jnp.where(hsel, _fold4(_dot_tn(pn, dosb)), 0.0)
            dq_ref[:, sl] = _unstack(dqs, lane)
            dsk = -(psink * inv) * delta
            for g in range(Q_PER_KV):
                val = jnp.sum(jnp.where(rowg == g, dsk, 0.0), axis=0, keepdims=True)
                dsink = dsink + jnp.where(lane128 == h * Q_PER_KV + g, val, 0.0)
        ds_ref[...] += dsink
        r0 = pl.ds(0, QB)
        rp = pl.ds(pl.multiple_of(jnp.maximum(i - 1, 0) * QB, QB), QB)
        rc = pl.ds(pl.multiple_of(i * QB, QB), QB)
        for acc, ref in ((dk_acc, dk_ref), (dv_acc, dv_ref)):
            ref[r0, :] += acc[0]
            ref[rp, :] += acc[1][:QB]
            ref[rc, :] += acc[1][QB:]

    return pl.pallas_call(
        kern, name="attn_bwd", grid=(bsz, nb),
        in_specs=[qspec] + kspecs + vspecs + [sspec, qspec, qspec],
        out_specs=[qspec, full, full, pl.BlockSpec((1, LANES), lambda b, i: (0, 0))],
        out_shape=[jax.ShapeDtypeStruct((bsz, lp, d), F32), jax.ShapeDtypeStruct((bsz, lp, KVW), F32),
                   jax.ShapeDtypeStruct((bsz, lp, KVW), F32), jax.ShapeDtypeStruct((1, LANES), F32)],
        compiler_params=_cparams(("arbitrary", "arbitrary")),
    )(q, k, k, k, kv, kv, kv, sinks, o, do)


def _adamw(name, w, m, v, parts):
    rows, wd = w.shape
    n = parts.shape[0]
    tm = _stream_tile(rows, wd * (7 * 4 + n * parts.dtype.itemsize))

    def kern(w_ref, m_ref, v_ref, p_ref, g_ref, d_ref, m2_ref, v2_ref):
        g = p_ref[0].astype(F32)
        for k in range(1, n):
            g = g + p_ref[k].astype(F32)
        m2 = ADAM_B1 * m_ref[...] + (1.0 - ADAM_B1) * g
        v2 = ADAM_B2 * v_ref[...] + (1.0 - ADAM_B2) * (g * g)
        mh = m2 / (1.0 - ADAM_B1 ** ADAM_STEP)
        vh = v2 / (1.0 - ADAM_B2 ** ADAM_STEP)
        g_ref[...] = g
        d_ref[...] = -ADAM_LR * (mh / (jnp.sqrt(vh) + ADAM_EPS) + ADAM_WD * w_ref[...])
        m2_ref[...] = m2
        v2_ref[...] = v2

    spec = pl.BlockSpec((tm, wd), lambda i: (i, 0))
    sd = jax.ShapeDtypeStruct((rows, wd), F32)
    return pl.pallas_call(
        kern, name=name, grid=(rows // tm,),
        in_specs=[spec, spec, spec, pl.BlockSpec((n, tm, wd), lambda i: (0, i, 0))],
        out_specs=[spec] * 4, out_shape=[sd] * 4,
        compiler_params=_cparams(("arbitrary",)),
    )(w, m, v, parts)


def _pair_sum(name, parts, theirs, my_c):
    n, _, rows, wd = parts.shape
    tm = _stream_tile(rows, wd * (4 + 4 + 2))

    def kern(c_ref, a_ref, b_ref, o_ref):
        o_ref[...] = (a_ref[...] + b_ref[...]).astype(BF16)

    return pl.pallas_call(
        kern, name=name,
        grid_spec=pltpu.PrefetchScalarGridSpec(
            num_scalar_prefetch=1, grid=(n, rows // tm),
            in_specs=[pl.BlockSpec((None, None, tm, wd), lambda k, i, c: (k, c[0], i, 0)),
                      pl.BlockSpec((None, tm, wd), lambda k, i, c: (k, i, 0))],
            out_specs=pl.BlockSpec((None, tm, wd), lambda k, i, c: (k, i, 0))),
        out_shape=jax.ShapeDtypeStruct((n, rows, wd), BF16), compiler_params=_cparams(("arbitrary", "arbitrary")),
    )(my_c, parts, theirs)


MESH = pl.DeviceIdType.MESH
ANY = pl.BlockSpec(memory_space=pl.ANY)


def _place():
    x, y, c = lax.axis_index("x"), lax.axis_index("y"), lax.axis_index("c")
    return x, y, c, [(1 - x, y), (x, 1 - y), (1 - x, 1 - y)]


def _gather_rider(shards):
    n = len(shards)

    def copy(refs, a, k, block, to, own=False):
        x_refs, out_refs, (send_sems, recv_sems, _) = refs
        px, py, pc = block
        slot = out_refs[a].at[4 * px + 2 * py + pc]
        return pltpu.make_async_remote_copy(
            src_ref=x_refs[a] if own else slot, dst_ref=slot,
            send_sem=send_sems.at[a, k], recv_sem=recv_sems.at[a, k], device_id=to, device_id_type=MESH)

    def local(refs, a):
        x, y, c, _ = _place()
        return pltpu.make_async_copy(refs[0][a], refs[1][a].at[4 * x + 2 * y + c], refs[2][2].at[a])

    def first(refs):
        x, y, c, chips = _place()
        out = []
        for a in range(n):
            out.append(copy(refs, a, 0, (x, y, c), (x, y, 1 - c), own=True))
            out += [copy(refs, a, 1 + j, (x, y, c), (*chip, c), own=True) for j, chip in enumerate(chips)]
        return out

    def passed(refs):
        x, y, c, chips = _place()
        return [copy(refs, a, 4 + j, (*chip, c), (x, y, 1 - c)) for j, chip in enumerate(chips) for a in range(n)]

    def start(*refs):
        for a in range(n):
            local(refs, a).start()
        for cp in first(refs):
            cp.start()

    def mid(*refs):
        x, y, c, chips = _place()
        fwd = passed(refs)
        for j, chip in enumerate(chips):
            for a in range(n):
                copy(refs, a, 1 + j, (*chip, c), (x, y, c)).wait_recv()
                fwd[j * n + a].start()

    def finish(*refs):
        x, y, c, chips = _place()
        for a in range(n):
            copy(refs, a, 0, (x, y, 1 - c), (x, y, c)).wait_recv()
            for j, chip in enumerate(chips):
                copy(refs, a, 4 + j, (*chip, 1 - c), (x, y, c)).wait_recv()
        for cp in first(refs) + passed(refs):
            cp.wait_send()
        for a in range(n):
            local(refs, a).wait()

    return _Rider(list(shards), [jax.ShapeDtypeStruct((N_DEV,) + s.shape, s.dtype) for s in shards],
                  [pltpu.SemaphoreType.DMA((n, 7)), pltpu.SemaphoreType.DMA((n, 7)), pltpu.SemaphoreType.DMA((n,))],
                  start, mid, finish)


def _swap_rider(parts):
    n = len(parts)

    def copies(p_refs, out_refs, sems):
        x, y, c, _ = _place()
        return [pltpu.make_async_remote_copy(
            src_ref=p_refs[a].at[:, 1 - c], dst_ref=out_refs[a], send_sem=sems[0].at[a], recv_sem=sems[1].at[a],
            device_id=(x, y, 1 - c), device_id_type=MESH) for a in range(n)]

    def start(*refs):
        for cp in copies(*refs):
            cp.start()

    def finish(*refs):
        for cp in copies(*refs):
            cp.wait()

    return _Rider(list(parts), [jax.ShapeDtypeStruct((p.shape[0],) + p.shape[2:], p.dtype) for p in parts],
                  [pltpu.SemaphoreType.DMA((n,)), pltpu.SemaphoreType.DMA((n,))], start, None, finish)


def _scatter_rider(sums):
    n = len(sums)

    def copy(refs, a, j, block):
        s_refs, out_refs, (send_sems, recv_sems, _) = refs
        x, y, c, chips = _place()
        px, py = chips[j]
        return pltpu.make_async_remote_copy(
            src_ref=s_refs[a].at[2 * px + py], dst_ref=out_refs[a].at[block],
            send_sem=send_sems.at[a, j], recv_sem=recv_sems.at[a, j], device_id=(px, py, c), device_id_type=MESH)

    def local(refs, a):
        x, y, c, _ = _place()
        return pltpu.make_async_copy(refs[0][a].at[2 * x + y], refs[1][a].at[2 * x + y], refs[2][2].at[a])

    def sends(refs):
        x, y, c, _ = _place()
        return [copy(refs, a, j, 2 * x + y) for j in range(3) for a in range(n)]

    def start(*refs):
        for a in range(n):
            local(refs, a).start()
        for cp in sends(refs):
            cp.start()

    def finish(*refs):
        x, y, c, chips = _place()
        for j, (px, py) in enumerate(chips):
            for a in range(n):
                copy(refs, a, j, 2 * px + py).wait_recv()
        for cp in sends(refs):
            cp.wait_send()
        for a in range(n):
            local(refs, a).wait()

    return _Rider(list(sums), [jax.ShapeDtypeStruct(s.shape, s.dtype) for s in sums],
                  [pltpu.SemaphoreType.DMA((n, 3)), pltpu.SemaphoreType.DMA((n, 3)), pltpu.SemaphoreType.DMA((n,))],
                  start, None, finish)


BIG = (("ffn1_w_gate_up", 2), ("ffn1_w_down", 1), ("ffn2_w_gate_up", 2), ("ffn2_w_down", 1), ("ssm_w_in", 1),
       ("ssm_w_out", 2), ("w_kv", 0), ("attn_w_q", 1), ("attn_w_o", 1))
SMALL = ("ffn1_norm", "mix_norm", "ffn2_norm", "ssm_lambda_re", "ssm_lambda_im", "ssm_b_re", "ssm_b_im",
         "ssm_c_re", "ssm_c_im", "ssm_log_step", "kv_norm", "k_norm", "q_norm", "attn_sinks")
COLS = (("meta_tokens", 1), ("ssm_d", 1))
WEIGHTS = ("meta_tokens", "ffn1_norm", "ffn1_w_gate_up", "ffn1_w_down", "mix_norm", "ffn2_norm", "ffn2_w_gate_up",
           "ffn2_w_down", "ssm_w_in", "ssm_lambda_re", "ssm_lambda_im", "ssm_b_re", "ssm_b_im", "ssm_c_re",
           "ssm_c_im", "ssm_log_step", "ssm_d", "ssm_w_out", "kv_norm", "w_kv", "k_norm", "attn_w_q", "q_norm",
           "attn_sinks", "attn_w_o")


def _rows_of(a, width):
    n = math.prod(a.shape)
    if n % width == 0:
        r = a.reshape(n // width, width)
    else:
        assert n < width
        r = jnp.pad(a.reshape(1, n), ((0, 0), (0, width - n)))
    return jnp.pad(r, ((0, (-r.shape[0]) % 8), (0, 0)))


def _pack_small(arrs, width):
    return jnp.concatenate([_rows_of(a.astype(F32), width) for a in arrs], axis=0)


def _unpack_small(buf, shapes, width):
    out, off = [], 0
    for shp in shapes:
        n = math.prod(shp)
        r = max(n // width, 1)
        out.append(buf[off:off + r].reshape(shp) if n % width == 0 else buf[off, :n].reshape(shp))
        off += r + (-r) % 8
    return out


def _shape2d(shp):
    return (math.prod(shp[:-1]), shp[-1])


def _unshard(g, axis):
    g = jnp.moveaxis(g, 0, axis)
    shp = g.shape
    return g.reshape(shp[:axis] + (shp[axis] * shp[axis + 1],) + shp[axis + 2:])


def _blockdiag(blocks):
    g, r, c = blocks.shape
    eye = jnp.eye(g, dtype=blocks.dtype)
    return (eye[:, None, :, None] * blocks[:, :, None, :]).reshape(g * r, g * c)


def _diagblocks(full, g):
    r, c = full.shape[0] // g, full.shape[1] // g
    return jnp.stack([full[k * r:(k + 1) * r, k * c:(k + 1) * c] for k in range(g)])


def kernel(x, meta_tokens, ffn1_norm, ffn1_w_gate_up, ffn1_w_down, mix_norm, ffn2_norm, ffn2_w_gate_up, ffn2_w_down, ssm_w_in, ssm_lambda_re, ssm_lambda_im, ssm_b_re, ssm_b_im, ssm_c_re, ssm_c_im, ssm_log_step, ssm_d, ssm_w_out, kv_norm, w_kv, k_norm, attn_w_q, q_norm, attn_sinks, attn_w_o, loss_target, m_meta_tokens, m_ffn1_norm, m_ffn1_w_gate_up, m_ffn1_w_down, m_mix_norm, m_ffn2_norm, m_ffn2_w_gate_up, m_ffn2_w_down, m_ssm_w_in, m_ssm_lambda_re, m_ssm_lambda_im, m_ssm_b_re, m_ssm_b_im, m_ssm_c_re, m_ssm_c_im, m_ssm_log_step, m_ssm_d, m_ssm_w_out, m_kv_norm, m_w_kv, m_k_norm, m_attn_w_q, m_q_norm, m_attn_sinks, m_attn_w_o, v_meta_tokens, v_ffn1_norm, v_ffn1_w_gate_up, v_ffn1_w_down, v_mix_norm, v_ffn2_norm, v_ffn2_w_gate_up, v_ffn2_w_down, v_ssm_w_in, v_ssm_lambda_re, v_ssm_lambda_im, v_ssm_b_re, v_ssm_b_im, v_ssm_c_re, v_ssm_c_im, v_ssm_log_step, v_ssm_d, v_ssm_w_out, v_kv_norm, v_w_kv, v_k_norm, v_attn_w_q, v_q_norm, v_attn_sinks, v_attn_w_o):
    args = dict(locals())
    W = {n: args[n] for n in WEIGHTS}
    M = {n: args["m_" + n] for n in WEIGHTS}
    V = {n: args["v_" + n] for n in WEIGHTS}
    my_x, my_y, my_c = (lax.axis_index(a) for a in MESH_AXES)
    my_dev = 4 * my_x + 2 * my_y + my_c

    big_names = [n for n, _ in BIG]
    s2d = {n: _shape2d(W[n].shape) for n in big_names}
    col_w = W["meta_tokens"].shape[1]

    grads, summed, small_parts = _local_step(x, loss_target, W, my_c.astype(jnp.int32).reshape(1))
    loss = lax.psum(grads.pop("loss"), MESH_AXES)
    grad_x = grads.pop("x")

    outs = [{}, {}, {}, {}]
    for n in big_names:
        r4 = _adamw("adamw_" + n, W[n].reshape(s2d[n]), M[n].reshape(s2d[n]), V[n].reshape(s2d[n]), summed[n])
        for k in range(4):
            outs[k][n] = r4[k].reshape(W[n].shape)

    small_names = list(SMALL) + [n for n, _ in COLS]
    small_shapes = [grads[n].shape for n in small_names]
    zero_cols = [jnp.zeros(grads[n].shape, F32) for n, _ in COLS]
    packs = lambda d: _pack_small([d[n] for n in SMALL] + zero_cols, PACK_W)
    r4 = _adamw("adamw_small", packs(W), packs(M), packs(V), small_parts)
    gsmall = None
    for k in range(4):
        un = dict(zip(small_names, _unpack_small(r4[k], small_shapes, PACK_W)))
        gsmall = un if k == 0 else gsmall
        outs[k].update({n: un[n] for n in SMALL})
    col_g = [lax.dynamic_slice_in_dim(gsmall[n], my_dev * W[n].shape[1], W[n].shape[1], axis=1) for n, _ in COLS]
    packc = lambda d: _pack_small([d[n] for n, _ in COLS], col_w)
    r4 = _adamw("adamw_cols", packc(W), packc(M), packc(V), _pack_small(col_g, col_w)[None])
    col_shapes = [W[n].shape for n, _ in COLS]
    for k in range(4):
        outs[k].update(dict(zip([n for n, _ in COLS], _unpack_small(r4[k], col_shapes, col_w))))

    res = [[outs[k][n] for n in WEIGHTS] for k in range(4)]
    return (loss, grad_x, *res[0], *res[1], *res[2], *res[3])


def _local_step(x, target, P, c_arr):
    bsz, seq, d = x.shape
    lp = seq + PAD
    rows = bsz * lp
    depth = P["ffn1_norm"].shape[0]
    assert depth == 2
    bf = lambda a: a.astype(BF16)
    row = lambda a: a.reshape(1, -1)

    def shard(n, l=None):
        a = P[n] if l is None else P[n][l]
        return bf(a.reshape(_shape2d(a.shape)))

    shard_t = lambda n, l: shard(n, l).T
    rowsharded = lambda g: g.reshape((g.shape[0] * g.shape[1],) + g.shape[2:])
    colsharded = lambda g: _unshard(g, 1)
    col_w = P["meta_tokens"].shape[1]
    g0 = _run_rider("gather_first", _gather_rider(
        [shard_t("ffn1_w_gate_up", 0), shard("ffn1_w_down", 0), shard("ssm_w_in", 0),
         _pack_small([P["meta_tokens"], P["ssm_d"]], col_w)]))
    ffn_w = {("ffn1", 0): (rowsharded(g0[0]), rowsharded(g0[1]))}
    w_in = rowsharded(g0[2])
    meta_full = _unshard(g0[3][:, :N_META], 1)
    dvec = _unshard(g0[3][:, N_META:N_META + 1, :P["ssm_d"].shape[1]], 1)

    pos = (jnp.arange(lp, dtype=F32) - float(META0))[:, None]
    half = HEAD_DIM // 2
    freqs = ROPE_THETA ** (-jnp.arange(0, half, dtype=F32) * 2.0 / HEAD_DIM)
    ang = pos * freqs[None, :]
    cos_t = jnp.tile(jnp.cos(ang), (1, LANES // half))
    sin_t = jnp.tile(jnp.concatenate([-jnp.sin(ang), jnp.sin(ang)], axis=1), (1, LANES // HEAD_DIM))
    gi = jnp.arange(LANES) // HEAD_DIM
    gmat = jnp.where(gi[:, None] == gi[None, :], 1.0 / HEAD_DIM, 0.0).astype(BF16)

    g_n, c_n, p_n = P["ssm_lambda_re"].shape[1], SSM_GROUP, SSM_STATE
    ns = g_n * p_n
    lr = P["ssm_lambda_re"][0].reshape(g_n, 1, p_n)
    li = P["ssm_lambda_im"][0].reshape(g_n, 1, p_n)
    ls = P["ssm_log_step"][0].reshape(g_n, 1, 1)
    brt = P["ssm_b_re"][0].transpose(0, 2, 1)
    bit = P["ssm_b_im"][0].transpose(0, 2, 1)
    ar, ai, bbr, bbi = _s5_params_fwd(lr, li, ls, brt, bit)
    a2 = jnp.concatenate([ar.reshape(1, ns), ai.reshape(1, ns)], axis=0)
    bfull = jnp.concatenate([_blockdiag(bbr), _blockdiag(bbi)], axis=1)
    cre_t = P["ssm_c_re"][0].transpose(0, 2, 1)
    cim_t = P["ssm_c_im"][0].transpose(0, 2, 1)
    cfull = jnp.concatenate([_blockdiag(cre_t), -_blockdiag(cim_t)], axis=0)

    ffn = lambda which, l: (row(P[which + "_norm"][l]),) + ffn_w[which, l]
    mix0, mix1, kvn = row(P["mix_norm"][0]), row(P["mix_norm"][1]), row(P["kv_norm"])
    kgain = jnp.tile(P["k_norm"].reshape(1, HEAD_DIM), (1, KVW // HEAD_DIM))
    qgain = jnp.tile(P["q_norm"].reshape(1, HEAD_DIM), (1, d // HEAD_DIM))
    sinks = P["attn_sinks"].reshape(1, -1)

    h0 = _embed(x, meta_full).reshape(rows, d)
    h1, ab_f1_0, g_wout, g_gu, g_d, g_kv = _ffn_fwd("ffn1_0_fwd", h0, *ffn("ffn1", 0), rider=_gather_rider(
        [shard("ssm_w_out", 0), shard_t("ffn2_w_gate_up", 0), shard("ffn2_w_down", 0), shard("w_kv")]))
    w_out, w_kv = colsharded(g_wout), rowsharded(g_kv)
    ffn_w["ffn2", 0] = (rowsharded(g_gu), rowsharded(g_d))
    u = _proj_fwd("ssm_in_fwd", h1, mix0, w_in)
    y, xs, u_perm = _s5_scan_fwd(u, bf(bfull), bf(cfull), a2, dvec, bsz)
    h2 = _glu_fwd(y, h1, w_out)
    h3, ab_f2_0, g_gu, g_d, g_q, g_o = _ffn_fwd("ffn2_0_fwd", h2, *ffn("ffn2", 0), rider=_gather_rider(
        [shard_t("ffn1_w_gate_up", 1), shard("ffn1_w_down", 1), shard("attn_w_q", 0), shard("attn_w_o", 0)]))
    w_q, w_o = rowsharded(g_q), rowsharded(g_o)
    ffn_w["ffn1", 1] = (rowsharded(g_gu), rowsharded(g_d))
    kv = _proj_fwd("kv_fwd", h3, kvn, w_kv)
    k = _headrope_fwd("k_rope_fwd", kv, KVW, kgain, cos_t, sin_t, gmat, lp)
    h4, ab_f1_1, g_gu, g_d = _ffn_fwd("ffn1_1_fwd", h3, *ffn("ffn1", 1), rider=_gather_rider(
        [shard_t("ffn2_w_gate_up", 1), shard("ffn2_w_down", 1)]))
    ffn_w["ffn2", 1] = (rowsharded(g_gu), rowsharded(g_d))
    q_raw = _proj_fwd("q_fwd", h4, mix1, w_q)
    q = _headrope_fwd("q_rope_fwd", q_raw, d, qgain, cos_t, sin_t, gmat, lp)
    r3 = lambda a: a.reshape(bsz, lp, a.shape[-1])
    o = _attn_fwd(r3(q), r3(k), r3(kv), sinks).reshape(rows, d)
    h5 = _lin_res_fwd("attn_out_fwd", o, w_o, h4)
    dh6, ab_f2_1, loss = _ffn_fwd("ffn2_1_fwd", h5, *ffn("ffn2", 1), loss_target=target.reshape(bsz * seq, d), lp=lp)

    G = {"loss": loss[0, 0]}

    def ffn_back(name, which, l, h, ab, dout, rider=None):
        g, wgu, wd = ffn(which, l)
        dh, hn, dab, act, dg, *rode = _ffn_bwd(name, h, ab, dout, g, wgu, wd, rider=rider)
        dwgu_t = _mm_tn(name + "_wgu", dab, hn)
        parts = [slots(dwgu_t), _mm_tn_slots(name + "_wd", act, dout, 0.5)]
        return dh, dg, parts, rode

    slots = lambda g: g.reshape((N_DEV, g.shape[0] // N_DEV) + g.shape[1:])
    swap_of = lambda parts: _swap_rider([p.reshape((4, 2) + p.shape[1:]) for p in parts])

    def pair_sums(tag, parts, theirs):
        return [_pair_sum("pair_sum_%s_%d" % (tag, k), p.reshape((4, 2) + p.shape[1:]), t, c_arr)
                for k, (p, t) in enumerate(zip(parts, theirs))]

    dh5, dg_f2_1, parts_a, _ = ffn_back("ffn2_1_bwd", "ffn2", 1, h5, ab_f2_1, dh6)
    do, dw_o, *theirs = _lin_bwd("attn_out_bwd", o, w_o, dh5, rider=swap_of(parts_a))
    sums_a = pair_sums("ffn2_1", parts_a, theirs)
    dq, dk, dv, dsinks = _attn_bwd(r3(q), r3(k), r3(kv), sinks, r3(o), r3(do))
    dq_raw, dqg = _headrope_bwd("q_rope_bwd", q_raw, d, dq.reshape(rows, d), qgain, cos_t, sin_t, gmat, lp)
    dh4, dg_mix1, dw_q = _proj_bwd("q_bwd", h4, mix1, w_q, dq_raw, dh5)
    dh3, dg_f1_1, parts_b, red_a = ffn_back("ffn1_1_bwd", "ffn1", 1, h3, ab_f1_1, dh4, rider=_scatter_rider(sums_a))
    dkv, dkg = _headrope_bwd("k_rope_bwd", kv, KVW, dk.reshape(rows, KVW), kgain, cos_t, sin_t, gmat, lp,
                             tail=dv.reshape(rows, KVW))
    parts_b = parts_b + [slots(dw_q), slots(dw_o)]
    dh3, dg_kvn, dw_kv, *theirs = _proj_bwd("kv_bwd", h3, kvn, w_kv, dkv, dh3, rider=swap_of(parts_b))
    sums_b = pair_sums("ffn1_1", parts_b, theirs)
    dh2, dg_f2_0, parts_c, red_b = ffn_back("ffn2_0_bwd", "ffn2", 0, h2, ab_f2_0, dh3, rider=_scatter_rider(sums_b))
    parts_c = parts_c + [slots(dw_kv)]
    dy, dw_out, *theirs = _glu_bwd(y, dh2, w_out, rider=swap_of(parts_c))
    sums_c = pair_sums("ffn2_0", parts_c, theirs)
    ctfull = jnp.concatenate([_blockdiag(P["ssm_c_re"][0]), -_blockdiag(P["ssm_c_im"][0])], axis=1)
    btfull = jnp.concatenate([_blockdiag(bbr.transpose(0, 2, 1)), _blockdiag(bbi.transpose(0, 2, 1))], axis=0)
    du, da, dd, dbq, dcq = _s5_scan_bwd(dy, u, u_perm, xs, bf(ctfull), bf(btfull), a2, dvec, bsz)
    quad_blocks = lambda q, part: jnp.concatenate([_diagblocks(q[2 * part], g_n // 2),
                                                   _diagblocks(q[2 * part + 1], g_n // 2)], axis=0)
    dh1, dg_mix0, dw_in = _proj_bwd("ssm_in_bwd", h1, mix0, w_in, du, dh2)
    dh0, dg_f1_0, parts_d, red_c = ffn_back("ffn1_0_bwd", "ffn1", 0, h0, ab_f1_0, dh1, rider=_scatter_rider(sums_c))
    dbbr = quad_blocks(dbq, 0)
    dbbi = quad_blocks(dbq, 1)
    dlr, dli, dls, dbrt, dbit = _s5_params_bwd(lr, li, ls, brt, bit, da[:, :ns].reshape(g_n, 1, p_n),
                                               da[:, ns:].reshape(g_n, 1, p_n), dbbr, dbbi)
    dh0 = r3(dh0)
    G["x"] = dh0[:, PAD:, :]
    G["meta_tokens"] = _meta_sum(dh0)
    G["ffn1_norm"] = jnp.concatenate([dg_f1_0, dg_f1_1], axis=0)
    G["ffn2_norm"] = jnp.concatenate([dg_f2_0, dg_f2_1], axis=0)
    G["mix_norm"] = jnp.concatenate([dg_mix0, dg_mix1], axis=0)
    G["ssm_lambda_re"] = dlr.reshape(1, g_n, p_n)
    G["ssm_lambda_im"] = dli.reshape(1, g_n, p_n)
    G["ssm_log_step"] = dls.reshape(1, g_n)
    G["ssm_b_re"] = dbrt.transpose(0, 2, 1)[None]
    G["ssm_b_im"] = dbit.transpose(0, 2, 1)[None]
    G["ssm_c_re"] = quad_blocks(dcq, 0).transpose(0, 2, 1)[None]
    G["ssm_c_im"] = -quad_blocks(dcq, 1).transpose(0, 2, 1)[None]
    G["ssm_d"] = dd
    G["kv_norm"] = dg_kvn.reshape(-1)
    G["k_norm"] = dkg[0, :HEAD_DIM]
    G["q_norm"] = dqg[:, :HEAD_DIM]
    G["attn_sinks"] = dsinks[:, :N_KV_HEADS * Q_PER_KV]

    parts_d = parts_d + [slots(dw_in), dw_out]
    small_pack = _pack_small([G[n] for n in list(SMALL) + [n for n, _ in COLS]], PACK_W)
    *theirs, small_parts = _run_rider("grad_swap_last", _join_riders(swap_of(parts_d), _gather_rider([small_pack])))
    red_d = _run_rider("grad_scatter_last", _scatter_rider(pair_sums("last", parts_d, theirs)))
    both = lambda lo, hi: jnp.concatenate([lo, hi], axis=1)
    gu = lambda lo, hi: both(jnp.swapaxes(lo, 1, 2), jnp.swapaxes(hi, 1, 2))
    summed = {"ffn1_w_gate_up": gu(red_d[0], red_b[0]), "ffn1_w_down": both(red_d[1], red_b[1]),
              "ffn2_w_gate_up": gu(red_c[0], red_a[0]), "ffn2_w_down": both(red_c[1], red_a[1]),
              "ssm_w_in": red_d[2], "ssm_w_out": red_d[3], "w_kv": red_c[2], "attn_w_q": red_b[2],
              "attn_w_o": red_b[3]}
    return G, summed, small_parts
```

```python
import functools
import math

import jax
import jax.numpy as jnp
from jax import lax
from jax.experimental import pallas as pl
from jax.experimental.pallas import tpu as pltpu

F32 = jnp.float32
BF16 = jnp.bfloat16

N_META = 16
PAD = 128
META0 = PAD - N_META
HEAD_DIM = 64
N_KV_HEADS = 4
Q_PER_KV = 4
SSM_GROUP = 16
SSM_STATE = 64
EPS = 1e-6
NEG_INF = -1e30
ROPE_THETA = 10000.0
ADAM_LR, ADAM_B1, ADAM_B2, ADAM_EPS, ADAM_WD, ADAM_STEP = 0.001, 0.9, 0.999, 1e-08, 0.01, 10
LANES = 128
PACK_W = 1024
VMEM_LIMIT = 56 * 1024 * 1024
MESH_AXES = ("x", "y", "c")
N_DEV = 8


def _cparams(sem=None):
    return pltpu.CompilerParams(dimension_semantics=sem, vmem_limit_bytes=VMEM_LIMIT)


def _row_tile(rows, light=False):
    for tm in ((768,) if light else ()) + (384, 256, 128, 64, 32, 16, 8):
        if rows % tm == 0:
            return tm
    raise ValueError(rows)


STREAM_BUDGET = 32 * 1024 * 1024


def _stream_tile(rows, bytes_per_row):
    for tm in range(rows, 0, -1):
        if rows % tm == 0 and (tm % 16 == 0 or tm == rows) and 2 * tm * bytes_per_row <= STREAM_BUDGET:
            return tm
    raise ValueError(rows)


TN_BUDGET = 52 * 1024 * 1024
TN_MAX_ROWS = 2816


def _tn_tile(rows, a, b, k1, tn):
    sa, sb = a.dtype.itemsize, b.dtype.itemsize
    fits = lambda tm: 2 * tm * (k1 * sa + tn * sb) + 3 * k1 * tn * 4 + tm * k1 * 2 <= TN_BUDGET
    divisors = [tm for tm in range(min(rows, TN_MAX_ROWS), 7, -8) if rows % tm == 0 and fits(tm)]
    good = [tm for tm in divisors if -(-tm // MXU_DIM) * MXU_DIM <= 1.1 * tm]
    if good or divisors:
        return (good or divisors)[0]
    raise ValueError(rows)


def _dot(a, b):
    return jnp.dot(a.astype(BF16), b.astype(BF16), preferred_element_type=F32)


def _dot_nt(a, b):
    return lax.dot_general(a.astype(BF16), b.astype(BF16), (((1,), (1,)), ((), ())), preferred_element_type=F32)


def _dot_tn(a, b):
    return lax.dot_general(a.astype(BF16), b.astype(BF16), (((0,), (0,)), ((), ())), preferred_element_type=F32)


def _rms(x, g):
    rstd = lax.rsqrt(jnp.mean(x * x, axis=-1, keepdims=True) + EPS)
    y = x * rstd
    return y * g, y, rstd


def _rms_bwd(dhn, y, rstd, g):
    dyn = dhn * g
    dx = rstd * (dyn - y * jnp.mean(dyn * y, axis=-1, keepdims=True))
    return dx, jnp.sum(dhn * y, axis=0, keepdims=True)


def _sigmoid(x):
    return 1.0 / (1.0 + jnp.exp(-x))


_GELU_C = math.sqrt(2.0 / math.pi)


def _gelu(y):
    t = jnp.tanh(_GELU_C * (y + 0.044715 * y * y * y))
    return 0.5 * y * (1.0 + t), t


def _gelu_grad(y, t):
    return 0.5 * (1.0 + t) + 0.5 * y * (1.0 - t * t) * _GELU_C * (1.0 + 3.0 * 0.044715 * y * y)


class _Rider:
    def __init__(self, ins, outs, sems, start, mid, finish):
        self.ins, self.outs, self.sems, self.start, self.mid, self.finish = ins, outs, sems, start, mid, finish


def _join_riders(r1, r2):
    ni, no, ns = len(r1.ins), len(r1.outs), len(r1.sems)

    def both(f1, f2):
        def phase(ins, outs, sems):
            if f1 is not None:
                f1(ins[:ni], outs[:no], sems[:ns])
            if f2 is not None:
                f2(ins[ni:], outs[no:], sems[ns:])
        return phase

    mid = both(r1.mid, r2.mid) if (r1.mid is not None or r2.mid is not None) else None
    return _Rider(r1.ins + r2.ins, r1.outs + r2.outs, r1.sems + r2.sems,
                  both(r1.start, r2.start), mid, both(r1.finish, r2.finish))


def _run_rider(name, rider):
    def kern(*refs):
        ni, no = len(rider.ins), len(rider.outs)
        parts = refs[:ni], refs[ni:ni + no], refs[ni + no:]
        rider.start(*parts)
        if rider.mid is not None:
            rider.mid(*parts)
        rider.finish(*parts)

    return pl.pallas_call(
        kern, name=name, out_shape=list(rider.outs), in_specs=[ANY] * len(rider.ins),
        out_specs=[ANY] * len(rider.outs), scratch_shapes=list(rider.sems),
    )(*rider.ins)


def _rowcall(name, body, rows, row_ins, const_ins, row_outs, acc_outs=(), tm=None, row_in_maps=None, rider=None,
             light=False):
    tm = tm or _row_tile(rows, light)
    steps = rows // tm
    in_specs = []
    for k, a in enumerate(row_ins):
        if row_in_maps is not None and row_in_maps[k] is not None:
            in_specs.append(pl.BlockSpec(*row_in_maps[k]))
        else:
            in_specs.append(pl.BlockSpec((tm, a.shape[1]), lambda i: (i, 0)))
    for a in const_ins:
        in_specs.append(pl.BlockSpec(a.shape, lambda i, nd=a.ndim: (0,) * nd, pipeline_mode=pl.Buffered(1)))
    out_shape, out_specs = [], []
    for w, dt in row_outs:
        out_shape.append(jax.ShapeDtypeStruct((rows, w), dt))
        out_specs.append(pl.BlockSpec((tm, w), lambda i: (i, 0)))
    for shp, dt in acc_outs:
        out_shape.append(jax.ShapeDtypeStruct(shp, dt))
        out_specs.append(pl.BlockSpec(shp, lambda i, nd=len(shp): (0,) * nd))

    if rider is None:
        def kern(*refs):
            body(pl.program_id(0), *refs)

        return pl.pallas_call(
            kern, name=name, grid=(steps,), in_specs=in_specs, out_specs=out_specs, out_shape=out_shape,
            compiler_params=_cparams(("arbitrary",)),
        )(*row_ins, *const_ins)

    n_in, n_out = len(in_specs), len(out_specs)
    r_in, r_out = len(rider.ins), len(rider.outs)

    def kern_r(*refs):
        step = pl.program_id(0)
        ins, rins = refs[:n_in], refs[n_in:n_in + r_in]
        outs = refs[n_in + r_in:n_in + r_in + n_out]
        routs = refs[n_in + r_in + n_out:n_in + r_in + n_out + r_out]
        sems = refs[n_in + r_in + n_out + r_out:]

        @pl.when(step == 0)
        def _():
            rider.start(rins, routs, sems)

        if rider.mid is not None:
            @pl.when(step == (3 * steps) // 4)
            def _():
                rider.mid(rins, routs, sems)

        body(step, *ins, *outs)

        @pl.when(step == steps - 1)
        def _():
            rider.finish(rins, routs, sems)

    return pl.pallas_call(
        kern_r, name=name, grid=(steps,), in_specs=in_specs + [ANY] * r_in, out_specs=out_specs + [ANY] * r_out,
        out_shape=out_shape + list(rider.outs), scratch_shapes=list(rider.sems),
        compiler_params=_cparams(("arbitrary",)),
    )(*row_ins, *const_ins, *rider.ins)


def _acc(step, ref, val):
    @pl.when(step == 0)
    def _():
        ref[...] = val

    @pl.when(step != 0)
    def _():
        ref[...] += val


def _embed(x, meta):
    bsz, seq, d = x.shape
    nb = seq // PAD + 1

    def kern(x_ref, m_ref, o_ref):
        i = pl.program_id(1)

        @pl.when(i == 0)
        def _():
            o_ref[0, 0:META0, :] = jnp.zeros((META0, d), F32)
            o_ref[0, META0:PAD, :] = m_ref[...]

        @pl.when(i != 0)
        def _():
            o_ref[0] = x_ref[0]

    return pl.pallas_call(
        kern, name="embed", grid=(bsz, nb),
        in_specs=[pl.BlockSpec((1, PAD, d), lambda b, i: (b, jnp.maximum(i - 1, 0), 0)),
                  pl.BlockSpec((N_META, d), lambda b, i: (0, 0))],
        out_specs=pl.BlockSpec((1, PAD, d), lambda b, i: (b, i, 0)),
        out_shape=jax.ShapeDtypeStruct((bsz, seq + PAD, d), F32),
        compiler_params=_cparams(("arbitrary", "arbitrary")),
    )(x, meta)


def _meta_sum(dh0):
    bsz, lp, d = dh0.shape

    def kern(d_ref, o_ref):
        _acc(pl.program_id(0), o_ref, d_ref[0, META0:PAD, :])

    return pl.pallas_call(
        kern, name="meta_sum", grid=(bsz,),
        in_specs=[pl.BlockSpec((1, PAD, d), lambda b: (b, 0, 0))],
        out_specs=pl.BlockSpec((N_META, d), lambda b: (0, 0)),
        out_shape=jax.ShapeDtypeStruct((N_META, d), F32),
        compiler_params=_cparams(("arbitrary",)),
    )(dh0)


MXU_DIM = 256


def _ffn_chunks(f):
    unit = MXU_DIM if f % MXU_DIM == 0 else LANES
    assert f % unit == 0
    first = (f // unit + 1) // 2 * unit
    return [(0, first), (first, f)] if first < f else [(0, f)]


def _ffn_fwd(name, h, g, wgu, wd, rider=None, loss_target=None, lp=None):
    rows, d = h.shape
    f = wd.shape[0]
    chunks = _ffn_chunks(f)
    tm = _row_tile(rows)
    nblk = tm // PAD

    def body(step, h_ref, *refs):
        t_refs, (g_ref, wgu_ref, wd_ref, o_ref, ab_ref), l_refs = refs[:nt], refs[nt:nt + 5], refs[nt + 5:]
        hx = h_ref[...]
        hb = _rms(hx, g_ref[...])[0].astype(BF16)
        acc = jnp.zeros(hx.shape, F32)
        for lo, hi in chunks:
            ga, ua = slice(lo, hi), slice(f + lo, f + hi)
            a = _dot_nt(hb, wgu_ref[ga, :])
            b = _dot_nt(hb, wgu_ref[ua, :])
            ab_ref[:, ga] = a.astype(BF16)
            ab_ref[:, ua] = b.astype(BF16)
            acc = acc + _dot(a * _sigmoid(a) * b, wd_ref[ga, :])
        out = hx + 0.5 * acc
        if not nt:
            o_ref[...] = out
            return
        err = out - jnp.concatenate([t[...] for t in t_refs], axis=0)
        rid = lax.broadcasted_iota(jnp.int32, (tm, 1), 0)
        err = jnp.where((step % per == 0) & (rid < PAD), 0.0, err)
        o_ref[...] = err * (1.0 / d)
        part = 0.5 * jnp.sum(jnp.mean(err * err, axis=-1, keepdims=True))
        _acc(step, l_refs[0], jnp.broadcast_to(part, (1, LANES)))

    if loss_target is None:
        nt = 0
        return _rowcall(name, body, rows, [h], [g, wgu, wd], [(d, F32), (2 * f, BF16)], rider=rider, tm=tm)
    assert tm % PAD == 0 and lp % tm == 0 and rider is None
    nt, per = nblk, lp // tm
    tblocks = (lp - PAD) // PAD

    def tmap(k):
        return lambda i: ((i // per) * tblocks + jnp.clip((i % per) * nblk - 1 + k, 0, tblocks - 1), 0)

    maps = [None] + [((PAD, d), tmap(k)) for k in range(nblk)]
    return _rowcall(name, body, rows, [h] + [loss_target] * nblk, [g, wgu, wd], [(d, F32), (2 * f, BF16)],
                    [((1, LANES), F32)], tm=tm, row_in_maps=maps)


def _ffn_bwd(name, h, ab, dout, g, wgu, wd, rider=None):
    rows, d = h.shape
    f = wd.shape[0]
    chunks = _ffn_chunks(f)

    def body(step, h_ref, ab_ref, do_ref, g_ref, wgu_ref, wd_ref, dh_ref, hn_ref, dab_ref, act_ref, dg_ref):
        hx, dout_x, gx = h_ref[...], do_ref[...], g_ref[...]
        hn, y, rstd = _rms(hx, gx)
        hn_ref[...] = hn.astype(BF16)
        dhalf = (0.5 * dout_x).astype(BF16)
        dhn = jnp.zeros(hx.shape, F32)
        for lo, hi in chunks:
            ga, ua = slice(lo, hi), slice(f + lo, f + hi)
            a = ab_ref[:, ga].astype(F32)
            b = ab_ref[:, ua].astype(F32)
            s = _sigmoid(a)
            silu = a * s
            act_ref[:, ga] = (silu * b).astype(BF16)
            dact = _dot_nt(dhalf, wd_ref[ga, :])
            da = (dact * b * (s + silu * (1.0 - s))).astype(BF16)
            db = (dact * silu).astype(BF16)
            dab_ref[:, ga] = da
            dab_ref[:, ua] = db
            dhn = dhn + _dot(da, wgu_ref[ga, :]) + _dot(db, wgu_ref[ua, :])
        dx, dg = _rms_bwd(dhn, y, rstd, gx)
        dh_ref[...] = dout_x + dx
        _acc(step, dg_ref, dg)

    return _rowcall(name, body, rows, [h, ab, dout], [g, wgu, wd],
                    [(d, F32), (d, BF16), (2 * f, BF16), (f, BF16)], [((1, d), F32)], rider=rider)


def _mm_tn(name, a, b):
    rows, k1 = a.shape
    k2 = b.shape[1]
    tk = max(t for t in range(LANES, k1 + 1, LANES) if k1 % t == 0 and (t * k2 * 4 <= 6 * 1024 * 1024 or t == LANES))
    tm = _tn_tile(rows, a, b, tk, k2)
    steps = rows // tm

    def kern(a_ref, b_ref, o_ref):
        _acc(pl.program_id(1), o_ref, _dot_tn(a_ref[...], b_ref[...]))

    return pl.pallas_call(
        kern, name=name, grid=(k1 // tk, steps),
        in_specs=[pl.BlockSpec((tm, tk), lambda j, i: (i, j)), pl.BlockSpec((tm, k2), lambda j, i: (i, 0))],
        out_specs=pl.BlockSpec((tk, k2), lambda j, i: (j, 0)),
        out_shape=jax.ShapeDtypeStruct((k1, k2), F32),
        compiler_params=_cparams(("arbitrary", "arbitrary")),
    )(a, b)


def _mm_tn_blockdiag(name, a, b, states_first):
    rows = a.shape[0]
    ka, kb = a.shape[1], b.shape[1]
    qa, qb = (ka // 4, kb // 2) if states_first else (ka // 2, kb // 4)
    tm = _tn_tile(rows, a, b, qa, qb)
    steps = rows // tm
    wide = lambda part, k: 2 * part + k
    amap = (lambda p, k, i: (i, wide(p, k))) if states_first else (lambda p, k, i: (i, k))
    bmap = (lambda p, k, i: (i, k)) if states_first else (lambda p, k, i: (i, wide(p, k)))
    omap = (lambda p, k, i: (wide(p, k), k)) if states_first else (lambda p, k, i: (k, wide(p, k)))

    def kern(a_ref, b_ref, o_ref):
        _acc(pl.program_id(2), o_ref, _dot_tn(a_ref[...], b_ref[...]))

    return pl.pallas_call(
        kern, name=name, grid=(2, 2, steps),
        in_specs=[pl.BlockSpec((tm, qa), amap), pl.BlockSpec((tm, qb), bmap)],
        out_specs=pl.BlockSpec((qa, qb), omap), out_shape=jax.ShapeDtypeStruct((ka, kb), F32),
        compiler_params=_cparams(("arbitrary", "arbitrary", "arbitrary")),
    )(a, b)


def _mm_tn_slots(name, a, b, scale):
    rows, k1 = a.shape
    k2 = b.shape[1]
    tn = 512 if k2 % 512 == 0 else k2
    sr = k1 // N_DEV
    tm = _tn_tile(rows, a, b, k1, tn)
    steps = rows // tm

    def kern(a_ref, b_ref, o_ref):
        bx = b_ref[...]
        if scale != 1.0:
            bx = bx * scale
        res = _dot_tn(a_ref[...], bx)
        step = pl.program_id(1)
        for s in range(N_DEV):
            _acc(step, o_ref.at[s], res[s * sr:(s + 1) * sr])

    return pl.pallas_call(
        kern, name=name, grid=(k2 // tn, steps),
        in_specs=[pl.BlockSpec((tm, k1), lambda j, i: (i, 0)), pl.BlockSpec((tm, tn), lambda j, i: (i, j))],
        out_specs=pl.BlockSpec((N_DEV, sr, tn), lambda j, i: (0, 0, j)),
        out_shape=jax.ShapeDtypeStruct((N_DEV, sr, k2), F32),
        compiler_params=_cparams(("arbitrary", "arbitrary")),
    )(a, b)


def _proj_fwd(name, h, g, w):
    rows = h.shape[0]

    def body(step, h_ref, g_ref, w_ref, o_ref):
        o_ref[...] = _dot(_rms(h_ref[...], g_ref[...])[0], w_ref[...])

    return _rowcall(name, body, rows, [h], [g, w], [(w.shape[1], F32)], light=True)[0]


def _proj_bwd(name, h, g, w, dy, dres, rider=None):
    rows, d = h.shape

    def body(step, h_ref, dy_ref, dr_ref, g_ref, w_ref, dh_ref, dg_ref, dw_ref):
        gx = g_ref[...]
        hn, y, rstd = _rms(h_ref[...], gx)
        dyx = dy_ref[...]
        dx, dg = _rms_bwd(_dot_nt(dyx, w_ref[...]), y, rstd, gx)
        dh_ref[...] = dr_ref[...] + dx
        _acc(step, dg_ref, dg)
        _acc(step, dw_ref, _dot_tn(hn, dyx))

    return _rowcall(name, body, rows, [h, dy, dres], [g, w], [(d, F32)], [((1, d), F32), (w.shape, F32)],
                    rider=rider, light=True)


def _lin_res_fwd(name, a, w, res):
    rows = a.shape[0]

    def body(step, a_ref, r_ref, w_ref, o_ref):
        o_ref[...] = r_ref[...] + _dot(a_ref[...], w_ref[...])

    return _rowcall(name, body, rows, [a, res], [w], [(w.shape[1], F32)], light=True)[0]


def _lin_bwd(name, a, w, dy, rider=None):
    rows, k = a.shape

    def body(step, a_ref, dy_ref, w_ref, da_ref, dw_ref):
        dyx = dy_ref[...]
        da_ref[...] = _dot_nt(dyx, w_ref[...])
        _acc(step, dw_ref, _dot_tn(a_ref[...], dyx))

    return _rowcall(name, body, rows, [a, dy], [w], [(k, F32)], [(w.shape, F32)], rider=rider, light=True)


def _s5_param_fn(lr, li, ls, brt, bit):
    step = jnp.exp(ls)
    mag = jnp.exp(lr * step)
    ar = mag * jnp.cos(li * step)
    ai = mag * jnp.sin(li * step)
    den = lr * lr + li * li
    nr, ni = ar - 1.0, ai
    cr = (nr * lr + ni * li) / den
    ci = (ni * lr - nr * li) / den
    return ar, ai, cr * brt - ci * bit, cr * bit + ci * brt


def _s5_params_fwd(lr, li, ls, brt, bit):
    def kern(lr_ref, li_ref, ls_ref, br_ref, bi_ref, ar_ref, ai_ref, bbr_ref, bbi_ref):
        ar, ai, bbr, bbi = _s5_param_fn(lr_ref[...], li_ref[...], ls_ref[...], br_ref[...], bi_ref[...])
        ar_ref[...], ai_ref[...], bbr_ref[...], bbi_ref[...] = ar, ai, bbr, bbi

    sd = jax.ShapeDtypeStruct
    return pl.pallas_call(
        kern, name="s5_params_fwd",
        out_shape=[sd(lr.shape, F32), sd(lr.shape, F32), sd(brt.shape, F32), sd(brt.shape, F32)],
    )(lr, li, ls, brt, bit)


def _s5_params_bwd(lr, li, ls, brt, bit, dar, dai, dbbr, dbbi):
    def kern(lr_ref, li_ref, ls_ref, br_ref, bi_ref, dar_ref, dai_ref, dbbr_ref, dbbi_ref,
             dlr_ref, dli_ref, dls_ref, dbr_ref, dbi_ref):
        _, vjp = jax.vjp(_s5_param_fn, lr_ref[...], li_ref[...], ls_ref[...], br_ref[...], bi_ref[...])
        dlr, dli, dls, dbr, dbi = vjp((dar_ref[...], dai_ref[...], dbbr_ref[...], dbbi_ref[...]))
        dlr_ref[...], dli_ref[...], dls_ref[...], dbr_ref[...], dbi_ref[...] = dlr, dli, dls, dbr, dbi

    sd = jax.ShapeDtypeStruct
    return pl.pallas_call(
        kern, name="s5_params_bwd",
        out_shape=[sd(lr.shape, F32), sd(lr.shape, F32), sd(ls.shape, F32), sd(brt.shape, F32), sd(brt.shape, F32)],
    )(lr, li, ls, brt, bit, dar, dai, dbbr, dbbi)


SCAN_LW = 512


SCAN_SEGS = 8
SCAN_UNROLL = 8


def _cmul(xr, xi, yr, yi):
    return xr * yr - xi * yi, xr * yi + xi * yr


def _scan_tables(a_ref, tab_ref, conj, seg_len):
    ns = a_ref.shape[1]
    ar = jnp.broadcast_to(a_ref[0:1, :], (8, ns))
    ai = jnp.broadcast_to(a_ref[1:2, :], (8, ns))
    if conj:
        ai = -ai
    big, base, e = None, (ar, ai), seg_len
    while e:
        if e & 1:
            big = base if big is None else _cmul(*big, *base)
        base = _cmul(*base, *base)
        e >>= 1
    big2 = _cmul(*big, *big)
    big4 = _cmul(*big2, *big2)
    for k, v in enumerate((ar, ai) + big + big2 + big4):
        tab_ref[k] = v


def _scan_block(x_ref, tab_ref, carry_ref, t_rows, ns, reverse):
    sl = t_rows // SCAN_SEGS
    assert sl % SCAN_UNROLL == 0
    row = lax.broadcasted_iota(jnp.int32, (8, SCAN_LW), 0)
    zero = jnp.zeros((8, SCAN_LW), F32)
    for lc in range(ns // SCAN_LW):
        lre = pl.ds(lc * SCAN_LW, SCAN_LW)
        lim = pl.ds(ns + lc * SCAN_LW, SCAN_LW)
        ar, ai = tab_ref[0, :, lre], tab_ref[1, :, lre]

        def rows_of(k, u):
            j = k * SCAN_UNROLL + u
            return pl.ds(pl.multiple_of(((sl - 1 - j) if reverse else j) * SCAN_SEGS, SCAN_SEGS), SCAN_SEGS)

        def local(k, s, lre=lre, lim=lim, ar=ar, ai=ai):
            sr, si = s
            for u in range(SCAN_UNROLL):
                rows = rows_of(k, u)
                tr, ti = _cmul(ar, ai, sr, si)
                sr, si = x_ref[rows, lre] + tr, x_ref[rows, lim] + ti
                x_ref[rows, lre], x_ref[rows, lim] = sr, si
            return sr, si

        er, ei = lax.fori_loop(0, sl // SCAN_UNROLL, local, (zero, zero))
        if reverse:
            cr = jnp.where(row == 7, carry_ref[:, lre], pltpu.roll(er, 7, 0))
            ci = jnp.where(row == 7, carry_ref[:, lim], pltpu.roll(ei, 7, 0))
        else:
            cr = jnp.where(row == 0, carry_ref[:, lre], pltpu.roll(er, 1, 0))
            ci = jnp.where(row == 0, carry_ref[:, lim], pltpu.roll(ei, 1, 0))
        for lvl, dsh in enumerate((1, 2, 4)):
            pr, pi = tab_ref[2 + 2 * lvl, :, lre], tab_ref[3 + 2 * lvl, :, lre]
            if reverse:
                keep, shift = row < 8 - dsh, 8 - dsh
            else:
                keep, shift = row >= dsh, dsh
            sr = jnp.where(keep, pltpu.roll(cr, shift, 0), 0.0)
            si = jnp.where(keep, pltpu.roll(ci, shift, 0), 0.0)
            tr, ti = _cmul(pr, pi, sr, si)
            cr, ci = cr + tr, ci + ti
        tr, ti = _cmul(tab_ref[2, :, lre], tab_ref[3, :, lre], cr, ci)
        edge = 0 if reverse else 7
        carry_ref[:, lre] = jnp.broadcast_to((er + tr)[edge:edge + 1, :], (8, SCAN_LW))
        carry_ref[:, lim] = jnp.broadcast_to((ei + ti)[edge:edge + 1, :], (8, SCAN_LW))

        def fix(k, t, lre=lre, lim=lim, ar=ar, ai=ai):
            tr, ti = t
            for u in range(SCAN_UNROLL):
                rows = rows_of(k, u)
                tr, ti = _cmul(ar, ai, tr, ti)
                x_ref[rows, lre] = x_ref[rows, lre] + tr
                x_ref[rows, lim] = x_ref[rows, lim] + ti
            return tr, ti

        lax.fori_loop(0, sl // SCAN_UNROLL, fix, (cr, ci))


def _bd_expand(u, w_ref, x_ref, ns):
    hh, sh = u.shape[1] // 2, ns // 2
    ub = u.astype(BF16)
    for part in range(2):
        for k in range(2):
            cols = slice(part * ns + k * sh, part * ns + (k + 1) * sh)
            x_ref[:, cols] = jnp.dot(ub[:, k * hh:(k + 1) * hh], w_ref[k * hh:(k + 1) * hh, cols],
                                     preferred_element_type=F32)


def _bd_contract(x_ref, w_ref, ns):
    hh, sh = w_ref.shape[1] // 2, ns // 2
    halves = []
    for k in range(2):
        acc = None
        for part in range(2):
            rows = slice(part * ns + k * sh, part * ns + (k + 1) * sh)
            t = jnp.dot(x_ref[:, rows].astype(BF16), w_ref[rows, k * hh:(k + 1) * hh], preferred_element_type=F32)
            acc = t if acc is None else acc + t
        halves.append(acc)
    return jnp.concatenate(halves, axis=1)


def _scan_rows(lp):
    for t in (384, 256, 128):
        if lp % t == 0:
            return t
    raise ValueError(lp)


def _seg_perm(t_rows):
    r = jnp.arange(t_rows)
    src = (r % SCAN_SEGS) * (t_rows // SCAN_SEGS) + r // SCAN_SEGS
    p = (src[:, None] == r[None, :]).astype(BF16)
    return p, p.T


def _permute_rows(p_ref, v):
    return jnp.dot(p_ref[...], v.astype(BF16), preferred_element_type=F32)


def _unpermute_rows(pt_ref, v):
    hi = v.astype(BF16)
    lo = (v - hi.astype(F32)).astype(BF16)
    pt = pt_ref[...]
    return jnp.dot(pt, hi, preferred_element_type=F32) + jnp.dot(pt, lo, preferred_element_type=F32)


def _s5_scan_fwd(u, bfull, cfull, a2, dvec, bsz):
    rows, hw = u.shape
    ns = a2.shape[1]
    lp = rows // bsz
    t_rows = _scan_rows(lp)
    nc = lp // t_rows
    pmat, pmat_t = _seg_perm(t_rows)

    def kern(u_ref, b_ref, c_ref, a_ref, d_ref, p_ref, pt_ref, y_ref, x_ref, up_ref, tab_ref, carry_ref):
        c = pl.program_id(1)

        @pl.when((pl.program_id(0) == 0) & (c == 0))
        def _():
            _scan_tables(a_ref, tab_ref, False, t_rows // SCAN_SEGS)

        @pl.when(c == 0)
        def _():
            carry_ref[...] = jnp.zeros_like(carry_ref)

        ux = u_ref[...]
        up = _permute_rows(p_ref, ux)
        up_ref[...] = up.astype(BF16)
        _bd_expand(up, b_ref, x_ref, ns)
        _scan_block(x_ref, tab_ref, carry_ref, t_rows, ns, reverse=False)
        y_ref[...] = _unpermute_rows(pt_ref, _bd_contract(x_ref, c_ref, ns)) + d_ref[...] * ux

    const = lambda shp: pl.BlockSpec(shp, lambda b, c: (0,) * len(shp), pipeline_mode=pl.Buffered(1))
    blk = lambda b, c: (b * nc + c, 0)
    return pl.pallas_call(
        kern, name="s5_scan_fwd", grid=(bsz, nc),
        in_specs=[pl.BlockSpec((t_rows, hw), blk), const(bfull.shape), const(cfull.shape), const(a2.shape),
                  const(dvec.shape), const(pmat.shape), const(pmat.shape)],
        out_specs=[pl.BlockSpec((t_rows, hw), blk), pl.BlockSpec((t_rows, 2 * ns), blk),
                   pl.BlockSpec((t_rows, hw), blk)],
        out_shape=[jax.ShapeDtypeStruct((rows, hw), F32), jax.ShapeDtypeStruct((rows, 2 * ns), F32),
                   jax.ShapeDtypeStruct((rows, hw), BF16)],
        scratch_shapes=[pltpu.VMEM((8, 8, ns), F32), pltpu.VMEM((8, 2 * ns), F32)],
        compiler_params=_cparams(("arbitrary", "arbitrary")),
    )(u, bfull, cfull, a2, dvec, pmat, pmat_t)


def _s5_scan_bwd(dy, u, xs, ctfull, btfull, a2, dvec, bsz):
    rows, hw = u.shape
    ns = a2.shape[1]
    lp = rows // bsz
    t_rows = _scan_rows(lp)
    nc = lp // t_rows
    blk = lambda b, c: (b * nc + (nc - 1 - c), 0)
    pmat, pmat_t = _seg_perm(t_rows)

    def prev8(b, c):
        first = (b * nc + (nc - 1 - c)) * (t_rows // 8)
        return (jnp.maximum(first - 1, 0), 0)

    def kern(dy_ref, u_ref, x_ref, xp_ref, ct_ref, bt_ref, a_ref, d_ref, p_ref, pt_ref,
             du_ref, gx_ref, dyp_ref, da_ref, dd_ref, tab_ref, carry_ref):
        b, c = pl.program_id(0), pl.program_id(1)
        first = (b == 0) & (c == 0)

        @pl.when(first)
        def _():
            _scan_tables(a_ref, tab_ref, True, t_rows // SCAN_SEGS)

        @pl.when(c == 0)
        def _():
            carry_ref[...] = jnp.zeros_like(carry_ref)

        dyx, ux = dy_ref[...], u_ref[...]
        dyp = _permute_rows(p_ref, dyx)
        dyp_ref[...] = dyp.astype(BF16)
        _bd_expand(dyp, ct_ref, gx_ref, ns)
        _scan_block(gx_ref, tab_ref, carry_ref, t_rows, ns, reverse=True)
        du_ref[...] = _unpermute_rows(pt_ref, _bd_contract(gx_ref, bt_ref, ns)) + d_ref[...] * dyx
        seq_start = c == nc - 1
        row8 = lax.broadcasted_iota(jnp.int32, (8, 1), 0)
        head = pltpu.roll(x_ref[t_rows - 8:t_rows, :], 1, 0)
        head = jnp.where(row8 == 0, jnp.where(seq_start, 0.0, xp_ref[7:8, :]), head)
        def conj_dot(xr, xi, gr, gi):
            return (jnp.sum(xr * gr + xi * gi, axis=0, keepdims=True),
                    jnp.sum(xr * gi - xi * gr, axis=0, keepdims=True))

        da_re, da_im = [], []
        for lc in range(ns // SCAN_LW):
            lre, lim = slice(lc * SCAN_LW, (lc + 1) * SCAN_LW), slice(ns + lc * SCAN_LW, ns + (lc + 1) * SCAN_LW)
            r0, i0 = conj_dot(head[:, lre], head[:, lim], gx_ref[0:8, lre], gx_ref[0:8, lim])
            r1, i1 = conj_dot(x_ref[0:t_rows - 8, lre], x_ref[0:t_rows - 8, lim],
                              gx_ref[8:t_rows, lre], gx_ref[8:t_rows, lim])
            da_re.append(r0 + r1)
            da_im.append(i0 + i1)
        da = jnp.concatenate(da_re + da_im, axis=1)
        dd = jnp.sum(dyx * ux, axis=0, keepdims=True)

        @pl.when(first)
        def _():
            da_ref[...] = da
            dd_ref[...] = dd

        @pl.when(jnp.logical_not(first))
        def _():
            da_ref[...] += da
            dd_ref[...] += dd

    const = lambda shp: pl.BlockSpec(shp, lambda b, c: (0,) * len(shp), pipeline_mode=pl.Buffered(1))
    return pl.pallas_call(
        kern, name="s5_scan_bwd", grid=(bsz, nc),
        in_specs=[pl.BlockSpec((t_rows, hw), blk), pl.BlockSpec((t_rows, hw), blk),
                  pl.BlockSpec((t_rows, 2 * ns), blk), pl.BlockSpec((8, 2 * ns), prev8),
                  const(ctfull.shape), const(btfull.shape), const(a2.shape), const(dvec.shape),
                  const(pmat.shape), const(pmat.shape)],
        out_specs=[pl.BlockSpec((t_rows, hw), blk), pl.BlockSpec((t_rows, 2 * ns), blk),
                   pl.BlockSpec((t_rows, hw), blk),
                   pl.BlockSpec((1, 2 * ns), lambda b, c: (0, 0)), pl.BlockSpec((1, hw), lambda b, c: (0, 0))],
        out_shape=[jax.ShapeDtypeStruct((rows, hw), F32), jax.ShapeDtypeStruct((rows, 2 * ns), F32),
                   jax.ShapeDtypeStruct((rows, hw), BF16),
                   jax.ShapeDtypeStruct((1, 2 * ns), F32), jax.ShapeDtypeStruct((1, hw), F32)],
        scratch_shapes=[pltpu.VMEM((8, 8, ns), F32), pltpu.VMEM((8, 2 * ns), F32)],
        compiler_params=_cparams(("arbitrary", "arbitrary")),
    )(dy, u, xs, xs, ctfull, btfull, a2, dvec, pmat, pmat_t)


def _glu_fwd(y, h1, wout):
    rows, d = h1.shape

    def body(step, y_ref, h_ref, w_ref, o_ref):
        z = _dot(_gelu(y_ref[...])[0], w_ref[...])
        o_ref[...] = h_ref[...] + z[:, :d] * _sigmoid(z[:, d:])

    return _rowcall("glu_fwd", body, rows, [y, h1], [wout], [(d, F32)], light=True)[0]


def _glu_bwd(y, dh2, wout, rider=None):
    rows, d = dh2.shape
    hw = y.shape[1]

    def body(step, y_ref, dh_ref, w_ref, dy_ref, dw_ref):
        yx, dh = y_ref[...], dh_ref[...]
        gl, t = _gelu(yx)
        z = _dot(gl, w_ref[...])
        za, sg = z[:, :d], _sigmoid(z[:, d:])
        dza = dh * sg
        dzg = dh * za * sg * (1.0 - sg)
        dgl = _dot_nt(dza, w_ref[:, :d]) + _dot_nt(dzg, w_ref[:, d:])
        dy_ref[...] = dgl * _gelu_grad(yx, t)
        for half, dz in enumerate((dza, dzg)):
            dw = _dot_tn(gl, dz)
            for s in range(N_DEV // 2):
                _acc(step, dw_ref.at[half * (N_DEV // 2) + s], dw[:, s * cw:(s + 1) * cw])

    cw = 2 * d // N_DEV
    return _rowcall("glu_bwd", body, rows, [y, dh2], [wout], [(hw, F32)], [((N_DEV, hw, cw), F32)], rider=rider,
                    light=True)


def _gmean64(x2, gmat):
    hi = x2.astype(BF16)
    r1 = x2 - hi.astype(F32)
    mid = r1.astype(BF16)
    lo = (r1 - mid.astype(F32)).astype(BF16)
    outs = []
    for j in range(x2.shape[1] // LANES):
        sl = slice(j * LANES, (j + 1) * LANES)
        f = lambda p: jnp.dot(p[:, sl], gmat, preferred_element_type=F32)
        outs.append(f(hi) + f(mid) + f(lo))
    return outs[0] if len(outs) == 1 else jnp.concatenate(outs, axis=1)


def _swap32(x):
    w = x.shape[1]
    lane = lax.broadcasted_iota(jnp.int32, (1, w), 1)
    return jnp.where((lane & 32) == 0, pltpu.roll(x, w - 32, 1), pltpu.roll(x, 32, 1))


def _tile_lanes(t, w):
    reps = w // t.shape[1]
    return t if reps == 1 else jnp.concatenate([t] * reps, axis=1)


def _headrope_fwd(name, raw, w, gain, cos, sin, gmat, lp):
    rows = raw.shape[0]
    tm = _row_tile(lp)
    per = lp // tm

    def body(step, x_ref, c_ref, s_ref, g_ref, gm_ref, o_ref):
        x = x_ref[...]
        rstd = lax.rsqrt(_gmean64(x * x, gm_ref[...]) + EPS)
        z = x * rstd * g_ref[...]
        o_ref[...] = z * _tile_lanes(c_ref[...], w) + _swap32(z) * _tile_lanes(s_ref[...], w)

    maps = [((tm, w), lambda i: (i, 0)), ((tm, LANES), lambda i: (i % per, 0)), ((tm, LANES), lambda i: (i % per, 0))]
    return _rowcall(name, body, rows, [raw, cos, sin], [gain, gmat], [(w, F32)], tm=tm, row_in_maps=maps)[0]


def _headrope_bwd(name, raw, w, dout, gain, cos, sin, gmat, lp, tail=None):
    rows = raw.shape[0]
    tm = _row_tile(lp)
    per = lp // tm
    wt = 0 if tail is None else tail.shape[1]

    def body(step, x_ref, do_ref, c_ref, s_ref, *refs):
        (g_ref, gm_ref, dx_ref, dg_ref), t_refs = refs[-4:], refs[:-4]
        x, dout_x, gx, gm = x_ref[...], do_ref[...], g_ref[...], gm_ref[...]
        rstd = lax.rsqrt(_gmean64(x * x, gm) + EPS)
        yn = x * rstd
        dz = dout_x * _tile_lanes(c_ref[...], w) + _swap32(dout_x * _tile_lanes(s_ref[...], w))
        dyn = dz * gx
        dx_ref[:, 0:w] = rstd * (dyn - yn * _gmean64(dyn * yn, gm))
        if t_refs:
            dx_ref[:, w:w + wt] = t_refs[0][...]
        dg = jnp.sum(dz * yn, axis=0, keepdims=True)
        sh = w // 2
        while sh >= HEAD_DIM:
            dg = dg + pltpu.roll(dg, sh, 1)
            sh //= 2
        _acc(step, dg_ref, dg)

    maps = [((tm, w), lambda i: (i, 0)), None, ((tm, LANES), lambda i: (i % per, 0)), ((tm, LANES), lambda i: (i % per, 0))]
    extra = [] if tail is None else [tail]
    return _rowcall(name, body, rows, [raw, dout, cos, sin] + extra, [gain, gmat], [(w + wt, F32)], [((1, w), F32)],
                    tm=tm, row_in_maps=maps + [None] * len(extra))


KVW = N_KV_HEADS * HEAD_DIM
QB = 128


def _fold4(x):
    y = x + pltpu.roll(x, 128, 1)
    return y + pltpu.roll(y, 64, 1)


ATTN_SCALE = HEAD_DIM ** -0.5


def _attn_masks(i):
    k0j = lax.broadcasted_iota(jnp.int32, (Q_PER_KV * QB, QB), 1)
    qi = lax.broadcasted_iota(jnp.int32, (Q_PER_KV * QB, 2 * QB), 0) % QB
    kj = lax.broadcasted_iota(jnp.int32, (Q_PER_KV * QB, 2 * QB), 1)
    in_prev = (kj < QB) & (kj > qi) & (i >= 2)
    in_cur = (kj >= QB) & (kj - QB <= qi)
    return k0j >= META0, in_prev | in_cur


def _attn_scores(i, q_ref, k0_ref, kp_ref, kc_ref, sink_ref, h):
    masks = _attn_masks(i)
    lane = lax.broadcasted_iota(jnp.int32, (1, KVW), 1) // HEAD_DIM
    qh = q_ref[:, h * KVW:(h + 1) * KVW]
    qs = jnp.concatenate([jnp.where(lane == g, qh, 0.0) for g in range(Q_PER_KV)], axis=0).astype(BF16)
    hsel = lane == h
    kx = _expand_kv((k0_ref, kp_ref, kc_ref), hsel)
    s0 = jnp.where(masks[0], _dot_nt(qs, kx[0]) * ATTN_SCALE, NEG_INF)
    sb = jnp.where(masks[1], _dot_nt(qs, kx[1]) * ATTN_SCALE, NEG_INF)
    rowg = lax.broadcasted_iota(jnp.int32, (Q_PER_KV * QB, 1), 0) // QB
    sink = jnp.zeros((Q_PER_KV * QB, 1), F32)
    for g in range(Q_PER_KV):
        sink = jnp.where(rowg == g, sink_ref[0, h * Q_PER_KV + g], sink)
    m = jnp.maximum(jnp.maximum(jnp.max(s0, axis=1, keepdims=True), jnp.max(sb, axis=1, keepdims=True)), sink)
    p0, pb, ps = jnp.exp(s0 - m), jnp.exp(sb - m), jnp.exp(sink - m)
    den = jnp.sum(p0, axis=1, keepdims=True) + jnp.sum(pb, axis=1, keepdims=True) + ps
    return qs, kx, (p0, pb), ps, den, lane, hsel


def _expand_kv(refs, hsel):
    x0, xp, xc = [_fold4(jnp.where(hsel, r[...], 0.0)).astype(BF16) for r in refs]
    return [x0, jnp.concatenate([xp, xc], axis=0)]


def _unstack(x, lane):
    out = jnp.where(lane == 0, x[0:QB], 0.0)
    for g in range(1, Q_PER_KV):
        out = out + jnp.where(lane == g, x[g * QB:(g + 1) * QB], 0.0)
    return out


def _attn_specs(nb, d):
    qspec = pl.BlockSpec((None, QB, d), lambda b, i: (b, i, 0))
    k0 = pl.BlockSpec((None, QB, KVW), lambda b, i: (b, 0, 0))
    kp = pl.BlockSpec((None, QB, KVW), lambda b, i: (b, jnp.maximum(i - 1, 0), 0))
    kc = pl.BlockSpec((None, QB, KVW), lambda b, i: (b, i, 0))
    v0 = pl.BlockSpec((None, QB, KVW), lambda b, i: (b, 0, 1))
    vp = pl.BlockSpec((None, QB, KVW), lambda b, i: (b, jnp.maximum(i - 1, 0), 1))
    vc = pl.BlockSpec((None, QB, KVW), lambda b, i: (b, i, 1))
    sink = pl.BlockSpec(memory_space=pltpu.SMEM)
    return qspec, [k0, kp, kc], [v0, vp, vc], sink


def _attn_fwd(q, k, kv, sinks):
    bsz, lp, d = q.shape
    nb = lp // QB
    qspec, kspecs, vspecs, sspec = _attn_specs(nb, d)

    def kern(q_ref, k0_ref, kp_ref, kc_ref, v0_ref, vp_ref, vc_ref, sink_ref, o_ref):
        i = pl.program_id(1)
        for h in range(N_KV_HEADS):
            qs, kx, ps3, psink, den, lane, hsel = _attn_scores(i, q_ref, k0_ref, kp_ref, kc_ref, sink_ref, h)
            vx = _expand_kv((v0_ref, vp_ref, vc_ref), hsel)
            o = _dot(ps3[0], vx[0]) + _dot(ps3[1], vx[1])
            o_ref[:, h * KVW:(h + 1) * KVW] = _unstack(o * (1.0 / den), lane)

    return pl.pallas_call(
        kern, name="attn_fwd", grid=(bsz, nb),
        in_specs=[qspec] + kspecs + vspecs + [sspec],
        out_specs=qspec, out_shape=jax.ShapeDtypeStruct((bsz, lp, d), F32),
        compiler_params=_cparams(("arbitrary", "arbitrary")),
    )(q, k, k, k, kv, kv, kv, sinks)


def _attn_bwd(q, k, kv, sinks, o, do):
    bsz, lp, d = q.shape
    nb = lp // QB
    qspec, kspecs, vspecs, sspec = _attn_specs(nb, d)
    full = pl.BlockSpec((None, lp, KVW), lambda b, i: (b, 0, 0))

    def kern(q_ref, k0_ref, kp_ref, kc_ref, v0_ref, vp_ref, vc_ref, sink_ref, o_ref, do_ref,
             dq_ref, dk_ref, dv_ref, ds_ref):
        b, i = pl.program_id(0), pl.program_id(1)

        @pl.when(i == 0)
        def _():
            dk_ref[...] = jnp.zeros_like(dk_ref)
            dv_ref[...] = jnp.zeros_like(dv_ref)

        @pl.when((b == 0) & (i == 0))
        def _():
            ds_ref[...] = jnp.zeros_like(ds_ref)

        lane128 = lax.broadcasted_iota(jnp.int32, (1, LANES), 1)
        rowg = lax.broadcasted_iota(jnp.int32, (Q_PER_KV * QB, 1), 0) // QB
        dk_acc = [jnp.zeros((QB, KVW), F32), jnp.zeros((2 * QB, KVW), F32)]
        dv_acc = [jnp.zeros((QB, KVW), F32), jnp.zeros((2 * QB, KVW), F32)]
        dsink = jnp.zeros((1, LANES), F32)
        for h in range(N_KV_HEADS):
            qs, kx, ps3, psink, den, lane, hsel = _attn_scores(i, q_ref, k0_ref, kp_ref, kc_ref, sink_ref, h)
            vx = _expand_kv((v0_ref, vp_ref, vc_ref), hsel)
            sl = slice(h * KVW, (h + 1) * KVW)
            doh, oh = do_ref[:, sl], o_ref[:, sl]
            dos = jnp.concatenate([jnp.where(lane == g, doh, 0.0) for g in range(Q_PER_KV)], axis=0)
            ost = jnp.concatenate([jnp.where(lane == g, oh, 0.0) for g in range(Q_PER_KV)], axis=0)
            delta = jnp.sum(dos * ost, axis=1, keepdims=True)
            inv = 1.0 / den
            dosb = dos.astype(BF16)
            dqs = jnp.zeros((Q_PER_KV * QB, KVW), F32)
            for n in range(2):
                pn = ps3[n] * inv
                ds = pn * (_dot_nt(dosb, vx[n]) - delta) * ATTN_SCALE
                dqs = dqs + _dot(ds, kx[n])
                dk_acc[n] = dk_acc[n] + jnp.where(hsel, _fold4(_dot_tn(ds, qs)), 0.0)
                dv_acc[n] = dv_acc[n] + jnp.where(hsel, _fold4(_dot_tn(pn, dosb)), 0.0)
            dq_ref[:, sl] = _unstack(dqs, lane)
            dsk = -(psink * inv) * delta
            for g in range(Q_PER_KV):
                val = jnp.sum(jnp.where(rowg == g, dsk, 0.0), axis=0, keepdims=True)
                dsink = dsink + jnp.where(lane128 == h * Q_PER_KV + g, val, 0.0)
        ds_ref[...] += dsink
        r0 = pl.ds(0, QB)
        rp = pl.ds(pl.multiple_of(jnp.maximum(i - 1, 0) * QB, QB), QB)
        rc = pl.ds(pl.multiple_of(i * QB, QB), QB)
        for acc, ref in ((dk_acc, dk_ref), (dv_acc, dv_ref)):
            ref[r0, :] += acc[0]
            ref[rp, :] += acc[1][:QB]
            ref[rc, :] += acc[1][QB:]

    return pl.pallas_call(
        kern, name="attn_bwd", grid=(bsz, nb),
        in_specs=[qspec] + kspecs + vspecs + [sspec, qspec, qspec],
        out_specs=[qspec, full, full, pl.BlockSpec((1, LANES), lambda b, i: (0, 0))],
        out_shape=[jax.ShapeDtypeStruct((bsz, lp, d), F32), jax.ShapeDtypeStruct((bsz, lp, KVW), F32),
                   jax.ShapeDtypeStruct((bsz, lp, KVW), F32), jax.ShapeDtypeStruct((1, LANES), F32)],
        compiler_params=_cparams(("arbitrary", "arbitrary")),
    )(q, k, k, k, kv, kv, kv, sinks, o, do)


def _adamw(name, w, m, v, parts):
    rows, wd = w.shape
    n = parts.shape[0]
    tm = _stream_tile(rows, wd * (7 * 4 + n * parts.dtype.itemsize))

    def kern(w_ref, m_ref, v_ref, p_ref, g_ref, d_ref, m2_ref, v2_ref):
        g = p_ref[0].astype(F32)
        for k in range(1, n):
            g = g + p_ref[k].astype(F32)
        m2 = ADAM_B1 * m_ref[...] + (1.0 - ADAM_B1) * g
        v2 = ADAM_B2 * v_ref[...] + (1.0 - ADAM_B2) * (g * g)
        mh = m2 / (1.0 - ADAM_B1 ** ADAM_STEP)
        vh = v2 / (1.0 - ADAM_B2 ** ADAM_STEP)
        g_ref[...] = g
        d_ref[...] = -ADAM_LR * (mh / (jnp.sqrt(vh) + ADAM_EPS) + ADAM_WD * w_ref[...])
        m2_ref[...] = m2
        v2_ref[...] = v2

    spec = pl.BlockSpec((tm, wd), lambda i: (i, 0))
    sd = jax.ShapeDtypeStruct((rows, wd), F32)
    return pl.pallas_call(
        kern, name=name, grid=(rows // tm,),
        in_specs=[spec, spec, spec, pl.BlockSpec((n, tm, wd), lambda i: (0, i, 0))],
        out_specs=[spec] * 4, out_shape=[sd] * 4,
        compiler_params=_cparams(("arbitrary",)),
    )(w, m, v, parts)


def _pair_sum(name, parts, theirs, my_c):
    n, _, rows, wd = parts.shape
    tm = _stream_tile(rows, wd * (4 + 4 + 2))

    def kern(c_ref, a_ref, b_ref, o_ref):
        o_ref[...] = (a_ref[...] + b_ref[...]).astype(BF16)

    return pl.pallas_call(
        kern, name=name,
        grid_spec=pltpu.PrefetchScalarGridSpec(
            num_scalar_prefetch=1, grid=(n, rows // tm),
            in_specs=[pl.BlockSpec((None, None, tm, wd), lambda k, i, c: (k, c[0], i, 0)),
                      pl.BlockSpec((None, tm, wd), lambda k, i, c: (k, i, 0))],
            out_specs=pl.BlockSpec((None, tm, wd), lambda k, i, c: (k, i, 0))),
        out_shape=jax.ShapeDtypeStruct((n, rows, wd), BF16), compiler_params=_cparams(("arbitrary", "arbitrary")),
    )(my_c, parts, theirs)


MESH = pl.DeviceIdType.MESH
ANY = pl.BlockSpec(memory_space=pl.ANY)


def _place():
    x, y, c = lax.axis_index("x"), lax.axis_index("y"), lax.axis_index("c")
    return x, y, c, [(1 - x, y), (x, 1 - y), (1 - x, 1 - y)]


def _gather_rider(shards):
    n = len(shards)

    def copy(refs, a, k, block, to, own=False):
        x_refs, out_refs, (send_sems, recv_sems, _) = refs
        px, py, pc = block
        slot = out_refs[a].at[4 * px + 2 * py + pc]
        return pltpu.make_async_remote_copy(
            src_ref=x_refs[a] if own else slot, dst_ref=slot,
            send_sem=send_sems.at[a, k], recv_sem=recv_sems.at[a, k], device_id=to, device_id_type=MESH)

    def local(refs, a):
        x, y, c, _ = _place()
        return pltpu.make_async_copy(refs[0][a], refs[1][a].at[4 * x + 2 * y + c], refs[2][2].at[a])

    def first(refs):
        x, y, c, chips = _place()
        out = []
        for a in range(n):
            out.append(copy(refs, a, 0, (x, y, c), (x, y, 1 - c), own=True))
            out += [copy(refs, a, 1 + j, (x, y, c), (*chip, c), own=True) for j, chip in enumerate(chips)]
        return out

    def passed(refs):
        x, y, c, chips = _place()
        return [copy(refs, a, 4 + j, (*chip, c), (x, y, 1 - c)) for j, chip in enumerate(chips) for a in range(n)]

    def start(*refs):
        for a in range(n):
            local(refs, a).start()
        for cp in first(refs):
            cp.start()

    def mid(*refs):
        x, y, c, chips = _place()
        fwd = passed(refs)
        for j, chip in enumerate(chips):
            for a in range(n):
                copy(refs, a, 1 + j, (*chip, c), (x, y, c)).wait_recv()
                fwd[j * n + a].start()

    def finish(*refs):
        x, y, c, chips = _place()
        for a in range(n):
            copy(refs, a, 0, (x, y, 1 - c), (x, y, c)).wait_recv()
            for j, chip in enumerate(chips):
                copy(refs, a, 4 + j, (*chip, 1 - c), (x, y, c)).wait_recv()
        for cp in first(refs) + passed(refs):
            cp.wait_send()
        for a in range(n):
            local(refs, a).wait()

    return _Rider(list(shards), [jax.ShapeDtypeStruct((N_DEV,) + s.shape, s.dtype) for s in shards],
                  [pltpu.SemaphoreType.DMA((n, 7)), pltpu.SemaphoreType.DMA((n, 7)), pltpu.SemaphoreType.DMA((n,))],
                  start, mid, finish)


def _swap_rider(parts):
    n = len(parts)

    def copies(p_refs, out_refs, sems):
        x, y, c, _ = _place()
        return [pltpu.make_async_remote_copy(
            src_ref=p_refs[a].at[:, 1 - c], dst_ref=out_refs[a], send_sem=sems[0].at[a], recv_sem=sems[1].at[a],
            device_id=(x, y, 1 - c), device_id_type=MESH) for a in range(n)]

    def start(*refs):
        for cp in copies(*refs):
            cp.start()

    def finish(*refs):
        for cp in copies(*refs):
            cp.wait()

    return _Rider(list(parts), [jax.ShapeDtypeStruct((p.shape[0],) + p.shape[2:], p.dtype) for p in parts],
                  [pltpu.SemaphoreType.DMA((n,)), pltpu.SemaphoreType.DMA((n,))], start, None, finish)


def _scatter_rider(sums):
    n = len(sums)

    def copy(refs, a, j, block):
        s_refs, out_refs, (send_sems, recv_sems, _) = refs
        x, y, c, chips = _place()
        px, py = chips[j]
        return pltpu.make_async_remote_copy(
            src_ref=s_refs[a].at[2 * px + py], dst_ref=out_refs[a].at[block],
            send_sem=send_sems.at[a, j], recv_sem=recv_sems.at[a, j], device_id=(px, py, c), device_id_type=MESH)

    def local(refs, a):
        x, y, c, _ = _place()
        return pltpu.make_async_copy(refs[0][a].at[2 * x + y], refs[1][a].at[2 * x + y], refs[2][2].at[a])

    def sends(refs):
        x, y, c, _ = _place()
        return [copy(refs, a, j, 2 * x + y) for j in range(3) for a in range(n)]

    def start(*refs):
        for a in range(n):
            local(refs, a).start()
        for cp in sends(refs):
            cp.start()

    def finish(*refs):
        x, y, c, chips = _place()
        for j, (px, py) in enumerate(chips):
            for a in range(n):
                copy(refs, a, j, 2 * px + py).wait_recv()
        for cp in sends(refs):
            cp.wait_send()
        for a in range(n):
            local(refs, a).wait()

    return _Rider(list(sums), [jax.ShapeDtypeStruct(s.shape, s.dtype) for s in sums],
                  [pltpu.SemaphoreType.DMA((n, 3)), pltpu.SemaphoreType.DMA((n, 3)), pltpu.SemaphoreType.DMA((n,))],
                  start, None, finish)


BIG = (("ffn1_w_gate_up", 2), ("ffn1_w_down", 1), ("ffn2_w_gate_up", 2), ("ffn2_w_down", 1), ("ssm_w_in", 1),
       ("ssm_w_out", 2), ("w_kv", 0), ("attn_w_q", 1), ("attn_w_o", 1))
SMALL = ("ffn1_norm", "mix_norm", "ffn2_norm", "ssm_lambda_re", "ssm_lambda_im", "ssm_b_re", "ssm_b_im",
         "ssm_c_re", "ssm_c_im", "ssm_log_step", "kv_norm", "k_norm", "q_norm", "attn_sinks")
COLS = (("meta_tokens", 1), ("ssm_d", 1))
WEIGHTS = ("meta_tokens", "ffn1_norm", "ffn1_w_gate_up", "ffn1_w_down", "mix_norm", "ffn2_norm", "ffn2_w_gate_up",
           "ffn2_w_down", "ssm_w_in", "ssm_lambda_re", "ssm_lambda_im", "ssm_b_re", "ssm_b_im", "ssm_c_re",
           "ssm_c_im", "ssm_log_step", "ssm_d", "ssm_w_out", "kv_norm", "w_kv", "k_norm", "attn_w_q", "q_norm",
           "attn_sinks", "attn_w_o")


def _rows_of(a, width):
    n = math.prod(a.shape)
    if n % width == 0:
        r = a.reshape(n // width, width)
    else:
        assert n < width
        r = jnp.pad(a.reshape(1, n), ((0, 0), (0, width - n)))
    return jnp.pad(r, ((0, (-r.shape[0]) % 8), (0, 0)))


def _pack_small(arrs, width):
    return jnp.concatenate([_rows_of(a.astype(F32), width) for a in arrs], axis=0)


def _unpack_small(buf, shapes, width):
    out, off = [], 0
    for shp in shapes:
        n = math.prod(shp)
        r = max(n // width, 1)
        out.append(buf[off:off + r].reshape(shp) if n % width == 0 else buf[off, :n].reshape(shp))
        off += r + (-r) % 8
    return out


def _shape2d(shp):
    return (math.prod(shp[:-1]), shp[-1])


def _unshard(g, axis):
    g = jnp.moveaxis(g, 0, axis)
    shp = g.shape
    return g.reshape(shp[:axis] + (shp[axis] * shp[axis + 1],) + shp[axis + 2:])


def _blockdiag(blocks):
    g, r, c = blocks.shape
    eye = jnp.eye(g, dtype=blocks.dtype)
    return (eye[:, None, :, None] * blocks[:, :, None, :]).reshape(g * r, g * c)


def _diagblocks(full, g):
    r, c = full.shape[0] // g, full.shape[1] // g
    return jnp.stack([full[k * r:(k + 1) * r, k * c:(k + 1) * c] for k in range(g)])


def kernel(x, meta_tokens, ffn1_norm, ffn1_w_gate_up, ffn1_w_down, mix_norm, ffn2_norm, ffn2_w_gate_up, ffn2_w_down, ssm_w_in, ssm_lambda_re, ssm_lambda_im, ssm_b_re, ssm_b_im, ssm_c_re, ssm_c_im, ssm_log_step, ssm_d, ssm_w_out, kv_norm, w_kv, k_norm, attn_w_q, q_norm, attn_sinks, attn_w_o, loss_target, m_meta_tokens, m_ffn1_norm, m_ffn1_w_gate_up, m_ffn1_w_down, m_mix_norm, m_ffn2_norm, m_ffn2_w_gate_up, m_ffn2_w_down, m_ssm_w_in, m_ssm_lambda_re, m_ssm_lambda_im, m_ssm_b_re, m_ssm_b_im, m_ssm_c_re, m_ssm_c_im, m_ssm_log_step, m_ssm_d, m_ssm_w_out, m_kv_norm, m_w_kv, m_k_norm, m_attn_w_q, m_q_norm, m_attn_sinks, m_attn_w_o, v_meta_tokens, v_ffn1_norm, v_ffn1_w_gate_up, v_ffn1_w_down, v_mix_norm, v_ffn2_norm, v_ffn2_w_gate_up, v_ffn2_w_down, v_ssm_w_in, v_ssm_lambda_re, v_ssm_lambda_im, v_ssm_b_re, v_ssm_b_im, v_ssm_c_re, v_ssm_c_im, v_ssm_log_step, v_ssm_d, v_ssm_w_out, v_kv_norm, v_w_kv, v_k_norm, v_attn_w_q, v_q_norm, v_attn_sinks, v_attn_w_o):
    args = dict(locals())
    W = {n: args[n] for n in WEIGHTS}
    M = {n: args["m_" + n] for n in WEIGHTS}
    V = {n: args["v_" + n] for n in WEIGHTS}
    my_x, my_y, my_c = (lax.axis_index(a) for a in MESH_AXES)
    my_dev = 4 * my_x + 2 * my_y + my_c

    big_names = [n for n, _ in BIG]
    s2d = {n: _shape2d(W[n].shape) for n in big_names}
    col_w = W["meta_tokens"].shape[1]

    grads, summed, small_parts = _local_step(x, loss_target, W, my_c.astype(jnp.int32).reshape(1))
    loss = lax.psum(grads.pop("loss"), MESH_AXES)
    grad_x = grads.pop("x")

    outs = [{}, {}, {}, {}]
    for n in big_names:
        r4 = _adamw("adamw_" + n, W[n].reshape(s2d[n]), M[n].reshape(s2d[n]), V[n].reshape(s2d[n]), summed[n])
        for k in range(4):
            outs[k][n] = r4[k].reshape(W[n].shape)

    small_names = list(SMALL) + [n for n, _ in COLS]
    small_shapes = [grads[n].shape for n in small_names]
    zero_cols = [jnp.zeros(grads[n].shape, F32) for n, _ in COLS]
    packs = lambda d: _pack_small([d[n] for n in SMALL] + zero_cols, PACK_W)
    r4 = _adamw("adamw_small", packs(W), packs(M), packs(V), small_parts)
    gsmall = None
    for k in range(4):
        un = dict(zip(small_names, _unpack_small(r4[k], small_shapes, PACK_W)))
        gsmall = un if k == 0 else gsmall
        outs[k].update({n: un[n] for n in SMALL})
    col_g = [lax.dynamic_slice_in_dim(gsmall[n], my_dev * W[n].shape[1], W[n].shape[1], axis=1) for n, _ in COLS]
    packc = lambda d: _pack_small([d[n] for n, _ in COLS], col_w)
    r4 = _adamw("adamw_cols", packc(W), packc(M), packc(V), _pack_small(col_g, col_w)[None])
    col_shapes = [W[n].shape for n, _ in COLS]
    for k in range(4):
        outs[k].update(dict(zip([n for n, _ in COLS], _unpack_small(r4[k], col_shapes, col_w))))

    res = [[outs[k][n] for n in WEIGHTS] for k in range(4)]
    return (loss, grad_x, *res[0], *res[1], *res[2], *res[3])


def _local_step(x, target, P, c_arr):
    bsz, seq, d = x.shape
    lp = seq + PAD
    rows = bsz * lp
    depth = P["ffn1_norm"].shape[0]
    assert depth == 2
    bf = lambda a: a.astype(BF16)
    row = lambda a: a.reshape(1, -1)

    def shard(n, l=None):
        a = P[n] if l is None else P[n][l]
        return bf(a.reshape(_shape2d(a.shape)))

    shard_t = lambda n, l: shard(n, l).T
    rowsharded = lambda g: g.reshape((g.shape[0] * g.shape[1],) + g.shape[2:])
    colsharded = lambda g: _unshard(g, 1)
    col_w = P["meta_tokens"].shape[1]
    g0 = _run_rider("gather_first", _gather_rider(
        [shard_t("ffn1_w_gate_up", 0), shard("ffn1_w_down", 0), shard("ssm_w_in", 0),
         _pack_small([P["meta_tokens"], P["ssm_d"]], col_w)]))
    ffn_w = {("ffn1", 0): (rowsharded(g0[0]), rowsharded(g0[1]))}
    w_in = rowsharded(g0[2])
    meta_full = _unshard(g0[3][:, :N_META], 1)
    dvec = _unshard(g0[3][:, N_META:N_META + 1, :P["ssm_d"].shape[1]], 1)

    pos = (jnp.arange(lp, dtype=F32) - float(META0))[:, None]
    half = HEAD_DIM // 2
    freqs = ROPE_THETA ** (-jnp.arange(0, half, dtype=F32) * 2.0 / HEAD_DIM)
    ang = pos * freqs[None, :]
    cos_t = jnp.tile(jnp.cos(ang), (1, LANES // half))
    sin_t = jnp.tile(jnp.concatenate([-jnp.sin(ang), jnp.sin(ang)], axis=1), (1, LANES // HEAD_DIM))
    gi = jnp.arange(LANES) // HEAD_DIM
    gmat = jnp.where(gi[:, None] == gi[None, :], 1.0 / HEAD_DIM, 0.0).astype(BF16)

    g_n, c_n, p_n = P["ssm_lambda_re"].shape[1], SSM_GROUP, SSM_STATE
    ns = g_n * p_n
    lr = P["ssm_lambda_re"][0].reshape(g_n, 1, p_n)
    li = P["ssm_lambda_im"][0].reshape(g_n, 1, p_n)
    ls = P["ssm_log_step"][0].reshape(g_n, 1, 1)
    brt = P["ssm_b_re"][0].transpose(0, 2, 1)
    bit = P["ssm_b_im"][0].transpose(0, 2, 1)
    ar, ai, bbr, bbi = _s5_params_fwd(lr, li, ls, brt, bit)
    a2 = jnp.concatenate([ar.reshape(1, ns), ai.reshape(1, ns)], axis=0)
    bfull = jnp.concatenate([_blockdiag(bbr), _blockdiag(bbi)], axis=1)
    cre_t = P["ssm_c_re"][0].transpose(0, 2, 1)
    cim_t = P["ssm_c_im"][0].transpose(0, 2, 1)
    cfull = jnp.concatenate([_blockdiag(cre_t), -_blockdiag(cim_t)], axis=0)

    ffn = lambda which, l: (row(P[which + "_norm"][l]),) + ffn_w[which, l]
    mix0, mix1, kvn = row(P["mix_norm"][0]), row(P["mix_norm"][1]), row(P["kv_norm"])
    kgain = jnp.tile(P["k_norm"].reshape(1, HEAD_DIM), (1, KVW // HEAD_DIM))
    qgain = jnp.tile(P["q_norm"].reshape(1, HEAD_DIM), (1, d // HEAD_DIM))
    sinks = P["attn_sinks"].reshape(1, -1)

    h0 = _embed(x, meta_full).reshape(rows, d)
    h1, ab_f1_0, g_wout, g_gu, g_d, g_kv = _ffn_fwd("ffn1_0_fwd", h0, *ffn("ffn1", 0), rider=_gather_rider(
        [shard("ssm_w_out", 0), shard_t("ffn2_w_gate_up", 0), shard("ffn2_w_down", 0), shard("w_kv")]))
    w_out, w_kv = colsharded(g_wout), rowsharded(g_kv)
    ffn_w["ffn2", 0] = (rowsharded(g_gu), rowsharded(g_d))
    u = _proj_fwd("ssm_in_fwd", h1, mix0, w_in)
    y, xs, u_perm = _s5_scan_fwd(u, bf(bfull), bf(cfull), a2, dvec, bsz)
    h2 = _glu_fwd(y, h1, w_out)
    h3, ab_f2_0, g_gu, g_d, g_q, g_o = _ffn_fwd("ffn2_0_fwd", h2, *ffn("ffn2", 0), rider=_gather_rider(
        [shard_t("ffn1_w_gate_up", 1), shard("ffn1_w_down", 1), shard("attn_w_q", 0), shard("attn_w_o", 0)]))
    w_q, w_o = rowsharded(g_q), rowsharded(g_o)
    ffn_w["ffn1", 1] = (rowsharded(g_gu), rowsharded(g_d))
    kv = _proj_fwd("kv_fwd", h3, kvn, w_kv)
    k = _headrope_fwd("k_rope_fwd", kv, KVW, kgain, cos_t, sin_t, gmat, lp)
    h4, ab_f1_1, g_gu, g_d = _ffn_fwd("ffn1_1_fwd", h3, *ffn("ffn1", 1), rider=_gather_rider(
        [shard_t("ffn2_w_gate_up", 1), shard("ffn2_w_down", 1)]))
    ffn_w["ffn2", 1] = (rowsharded(g_gu), rowsharded(g_d))
    q_raw = _proj_fwd("q_fwd", h4, mix1, w_q)
    q = _headrope_fwd("q_rope_fwd", q_raw, d, qgain, cos_t, sin_t, gmat, lp)
    r3 = lambda a: a.reshape(bsz, lp, a.shape[-1])
    o = _attn_fwd(r3(q), r3(k), r3(kv), sinks).reshape(rows, d)
    h5 = _lin_res_fwd("attn_out_fwd", o, w_o, h4)
    dh6, ab_f2_1, loss = _ffn_fwd("ffn2_1_fwd", h5, *ffn("ffn2", 1), loss_target=target.reshape(bsz * seq, d), lp=lp)

    G = {"loss": loss[0, 0]}

    def ffn_back(name, which, l, h, ab, dout, rider=None):
        g, wgu, wd = ffn(which, l)
        dh, hn, dab, act, dg, *rode = _ffn_bwd(name, h, ab, dout, g, wgu, wd, rider=rider)
        dwgu_t = _mm_tn(name + "_wgu", dab, hn)
        parts = [slots(dwgu_t), _mm_tn_slots(name + "_wd", act, dout, 0.5)]
        return dh, dg, parts, rode

    slots = lambda g: g.reshape((N_DEV, g.shape[0] // N_DEV) + g.shape[1:])
    swap_of = lambda parts: _swap_rider([p.reshape((4, 2) + p.shape[1:]) for p in parts])

    def pair_sums(tag, parts, theirs):
        return [_pair_sum("pair_sum_%s_%d" % (tag, k), p.reshape((4, 2) + p.shape[1:]), t, c_arr)
                for k, (p, t) in enumerate(zip(parts, theirs))]

    dh5, dg_f2_1, parts_a, _ = ffn_back("ffn2_1_bwd", "ffn2", 1, h5, ab_f2_1, dh6)
    do, dw_o, *theirs = _lin_bwd("attn_out_bwd", o, w_o, dh5, rider=swap_of(parts_a))
    sums_a = pair_sums("ffn2_1", parts_a, theirs)
    dq, dk, dv, dsinks = _attn_bwd(r3(q), r3(k), r3(kv), sinks, r3(o), r3(do))
    dq_raw, dqg = _headrope_bwd("q_rope_bwd", q_raw, d, dq.reshape(rows, d), qgain, cos_t, sin_t, gmat, lp)
    dh4, dg_mix1, dw_q = _proj_bwd("q_bwd", h4, mix1, w_q, dq_raw, dh5)
    dh3, dg_f1_1, parts_b, red_a = ffn_back("ffn1_1_bwd", "ffn1", 1, h3, ab_f1_1, dh4, rider=_scatter_rider(sums_a))
    dkv, dkg = _headrope_bwd("k_rope_bwd", kv, KVW, dk.reshape(rows, KVW), kgain, cos_t, sin_t, gmat, lp,
                             tail=dv.reshape(rows, KVW))
    parts_b = parts_b + [slots(dw_q), slots(dw_o)]
    dh3, dg_kvn, dw_kv, *theirs = _proj_bwd("kv_bwd", h3, kvn, w_kv, dkv, dh3, rider=swap_of(parts_b))
    sums_b = pair_sums("ffn1_1", parts_b, theirs)
    dh2, dg_f2_0, parts_c, red_b = ffn_back("ffn2_0_bwd", "ffn2", 0, h2, ab_f2_0, dh3, rider=_scatter_rider(sums_b))
    parts_c = parts_c + [slots(dw_kv)]
    dy, dw_out, *theirs = _glu_bwd(y, dh2, w_out, rider=swap_of(parts_c))
    sums_c = pair_sums("ffn2_0", parts_c, theirs)
    ctfull = jnp.concatenate([_blockdiag(P["ssm_c_re"][0]), -_blockdiag(P["ssm_c_im"][0])], axis=1)
    btfull = jnp.concatenate([_blockdiag(bbr.transpose(0, 2, 1)), _blockdiag(bbi.transpose(0, 2, 1))], axis=0)
    du, gx, dy_perm, da, dd = _s5_scan_bwd(dy, u, xs, bf(ctfull), bf(btfull), a2, dvec, bsz)
    dbfull = _mm_tn_blockdiag("ssm_db", u_perm, gx, False)
    dcfull = _mm_tn_blockdiag("ssm_dc", xs, dy_perm, True)
    dh1, dg_mix0, dw_in = _proj_bwd("ssm_in_bwd", h1, mix0, w_in, du, dh2)
    dh0, dg_f1_0, parts_d, red_c = ffn_back("ffn1_0_bwd", "ffn1", 0, h0, ab_f1_0, dh1, rider=_scatter_rider(sums_c))
    dbbr = _diagblocks(dbfull[:, :ns], g_n)
    dbbi = _diagblocks(dbfull[:, ns:], g_n)
    dlr, dli, dls, dbrt, dbit = _s5_params_bwd(lr, li, ls, brt, bit, da[:, :ns].reshape(g_n, 1, p_n),
                                               da[:, ns:].reshape(g_n, 1, p_n), dbbr, dbbi)
    dh0 = r3(dh0)
    G["x"] = dh0[:, PAD:, :]
    G["meta_tokens"] = _meta_sum(dh0)
    G["ffn1_norm"] = jnp.concatenate([dg_f1_0, dg_f1_1], axis=0)
    G["ffn2_norm"] = jnp.concatenate([dg_f2_0, dg_f2_1], axis=0)
    G["mix_norm"] = jnp.concatenate([dg_mix0, dg_mix1], axis=0)
    G["ssm_lambda_re"] = dlr.reshape(1, g_n, p_n)
    G["ssm_lambda_im"] = dli.reshape(1, g_n, p_n)
    G["ssm_log_step"] = dls.reshape(1, g_n)
    G["ssm_b_re"] = dbrt.transpose(0, 2, 1)[None]
    G["ssm_b_im"] = dbit.transpose(0, 2, 1)[None]
    G["ssm_c_re"] = _diagblocks(dcfull[:ns], g_n).transpose(0, 2, 1)[None]
    G["ssm_c_im"] = -_diagblocks(dcfull[ns:], g_n).transpose(0, 2, 1)[None]
    G["ssm_d"] = dd
    G["kv_norm"] = dg_kvn.reshape(-1)
    G["k_norm"] = dkg[0, :HEAD_DIM]
    G["q_norm"] = dqg[:, :HEAD_DIM]
    G["attn_sinks"] = dsinks[:, :N_KV_HEADS * Q_PER_KV]

    parts_d = parts_d + [slots(dw_in), dw_out]
    small_pack = _pack_small([G[n] for n in list(SMALL) + [n for n, _ in COLS]], PACK_W)
    *theirs, small_parts = _run_rider("grad_swap_last", _join_riders(swap_of(parts_d), _gather_rider([small_pack])))
    red_d = _run_rider("grad_scatter_last", _scatter_rider(pair_sums("last", parts_d, theirs)))
    both = lambda lo, hi: jnp.concatenate([lo, hi], axis=1)
    gu = lambda lo, hi: both(jnp.swapaxes(lo, 1, 2), jnp.swapaxes(hi, 1, 2))
    summed = {"ffn1_w_gate_up": gu(red_d[0], red_b[0]), "ffn1_w_down": both(red_d[1], red_b[1]),
              "ffn2_w_gate_up": gu(red_c[0], red_a[0]), "ffn2_w_down": both(red_c[1], red_a[1]),
              "ssm_w_in": red_d[2], "ssm_w_out": red_d[3], "w_kv": red_c[2], "attn_w_q": red_b[2],
              "attn_w_o": red_b[3]}
    return G, summed, small_parts
```
